```python
import jax
import jax.numpy as jnp
from jax import lax
import numpy as np

D_MODEL = 2048
BATCH = 8
SEQ = 2048
DEPTH = 1

D_MIX = D_MODEL
RWKV_HEAD = 64
RWKV_WIDTH = D_MIX // 2
RWKV_HEADS = RWKV_WIDTH // RWKV_HEAD
DECAY_LORA = 64
AAA_LORA = 64
GATE_LORA = 160
GN_EPS = 64e-5
QK_NOPE = 128
QK_ROPE = 64
V_HEAD = 128
MLA_WIDTH = D_MIX - RWKV_WIDTH
MLA_HEADS = MLA_WIDTH // V_HEAD
Q_LORA = 512
KV_LORA = 512
ROPE_THETA = 10000.0
Q_BLOCK = 128
D_FF = 5632
CONV_W = 3
NORM_EPS = 1e-6
NEG_INF = -1e30

IN_SPLITS = (RWKV_WIDTH, RWKV_WIDTH, RWKV_WIDTH, DECAY_LORA, AAA_LORA, GATE_LORA,
             Q_LORA, KV_LORA, QK_ROPE)
D_IN = sum(IN_SPLITS)
RWKV_SHIFT_DIM = 3 * RWKV_WIDTH + DECAY_LORA + AAA_LORA + GATE_LORA

kernel_name = 'hybrid_rwkv7_mla_convglu'


def _split(t, sizes):
    idx = np.cumsum(sizes)[:-1].tolist()
    return jnp.split(t, idx, axis=-1)


def rms_norm(x, g):
    xf = x.astype(jnp.float32)
    y = xf * lax.rsqrt(jnp.mean(xf * xf, axis=-1, keepdims=True) + NORM_EPS)
    return (y * g.astype(jnp.float32)).astype(x.dtype)


def token_shift(h, mu):
    prev = jnp.pad(h, ((0, 0), (1, 0), (0, 0)))[:, :-1]
    return h + (prev - h) * mu


def apply_rope(t, positions):
    half = t.shape[-1] // 2
    inv_freq = ROPE_THETA ** (-jnp.arange(half, dtype=jnp.float32) / half)
    ang = positions.astype(jnp.float32)[..., None] * inv_freq
    ang = ang.reshape(ang.shape[:2] + (1,) * (t.ndim - 3) + (half,))
    cos, sin = jnp.cos(ang), jnp.sin(ang)
    tf = t.astype(jnp.float32)
    t1, t2 = tf[..., :half], tf[..., half:]
    return jnp.concatenate([t1 * cos - t2 * sin, t2 * cos + t1 * sin], axis=-1).astype(t.dtype)


def rwkv7_scan(r, w, k, v, a, b):
    bsz, _, h, n = r.shape

    def step(state, inp):
        r_t, w_t, k_t, v_t, a_t, b_t = inp
        sa = jnp.einsum('bhvk,bhk->bhv', state, a_t)
        state = (state * w_t[:, :, None, :]
                 + sa[..., None] * b_t[:, :, None, :]
                 + v_t[..., None] * k_t[:, :, None, :])
        return state, jnp.einsum('bhvk,bhk->bhv', state, r_t)

    init = jnp.zeros((bsz, h, n, n), jnp.float32)
    xs = tuple(jnp.moveaxis(t, 1, 0) for t in (r, w, k, v, a, b))
    _, out = lax.scan(step, init, xs)
    return jnp.moveaxis(out, 0, 1)


def rwkv7_mixer(h_r, h_k, h_v, h_w, h_a, h_g, w0, w2, a0, a2, g2, k_k, k_a, r_k, gn_w, gn_b):
    f32 = jnp.float32
    bsz, s, c = h_r.shape
    hd = (bsz, s, RWKV_HEADS, RWKV_HEAD)
    r = h_r.astype(f32)
    k = h_k.astype(f32)
    v = h_v.astype(f32)
    w_log = -jax.nn.softplus(-(w0.astype(f32) + jnp.tanh(h_w.astype(f32)) @ w2.astype(f32))) - 0.5
    decay = jnp.exp(-jnp.exp(w_log))
    a = jax.nn.sigmoid(a0.astype(f32) + h_a.astype(f32) @ a2.astype(f32))
    g = jax.nn.sigmoid(h_g.astype(f32)) @ g2.astype(f32)
    kk = (k * k_k).reshape(hd)
    kk = kk * lax.rsqrt(jnp.maximum(jnp.sum(kk * kk, axis=-1, keepdims=True), 1e-24))
    k = k * (1.0 + (a - 1.0) * k_a)
    r, k, v, decay, a = (t.reshape(hd) for t in (r, k, v, decay, a))
    y = rwkv7_scan(r, decay, k, v, -kk, kk * a)
    mu = jnp.mean(y, axis=-1, keepdims=True)
    var = jnp.mean(jnp.square(y - mu), axis=-1, keepdims=True)
    y = ((y - mu) * lax.rsqrt(var + GN_EPS)).reshape(bsz, s, c) * gn_w + gn_b
    bonus = jnp.sum(r * k * r_k, axis=-1, keepdims=True) * v
    return (y + bonus.reshape(bsz, s, c)) * g


def mla_mixer(c_q, c_kv, k_pe, positions, q_norm_g, w_uq, kv_norm_g, w_ukv):
    f32 = jnp.float32
    bsz, s, _ = c_q.shape
    q = (rms_norm(c_q, q_norm_g) @ w_uq).reshape(bsz, s, MLA_HEADS, QK_NOPE + QK_ROPE)
    q_nope = q[..., :QK_NOPE]
    q_pe = apply_rope(q[..., QK_NOPE:], positions)
    kv = (rms_norm(c_kv, kv_norm_g) @ w_ukv).reshape(bsz, s, MLA_HEADS, QK_NOPE + V_HEAD)
    k_nope, v = kv[..., :QK_NOPE], kv[..., QK_NOPE:]
    k_pe = apply_rope(k_pe, positions)
    scale = (QK_NOPE + QK_ROPE) ** -0.5
    outs = []
    for i in range(s // Q_BLOCK):
        q0, q1 = i * Q_BLOCK, (i + 1) * Q_BLOCK
        sc = (jnp.einsum('bqhd,bkhd->bhqk', q_nope[:, q0:q1], k_nope[:, :q1])
              + jnp.einsum('bqhr,bkr->bhqk', q_pe[:, q0:q1], k_pe[:, :q1])).astype(f32) * scale
        causal = (q0 + jnp.arange(Q_BLOCK))[:, None] >= jnp.arange(q1)[None, :]
        p = jax.nn.softmax(jnp.where(causal, sc, NEG_INF), axis=-1)
        outs.append(jnp.einsum('bhqk,bkhd->bqhd', p.astype(v.dtype), v[:, :q1]))
    return jnp.concatenate(outs, axis=1).reshape(bsz, s, MLA_WIDTH)


def conv_glu_ffn(h, w_gate, w_up, conv_w, conv_b, w_down):
    gate = h @ w_gate
    gate = lax.conv_general_dilated(
        gate, conv_w[:, None, :].astype(gate.dtype), window_strides=(1,),
        padding=[(CONV_W - 1, 0)], dimension_numbers=('NWC', 'WIO', 'NWC'),
        feature_group_count=D_FF) + conv_b
    return (jax.nn.silu(gate) * (h @ w_up)) @ w_down


def _fwd_setup_inputs(seed: int = 0) -> dict:
    key = jax.random.key(seed)
    ks = jax.random.split(key, 32)
    L = DEPTH

    def nrm(k, shape, scale):
        return jax.random.normal(k, shape, jnp.float32) * scale

    def gain(k, shape):
        return 1.0 + nrm(k, shape, 0.05)

    x = nrm(ks[0], (BATCH, SEQ, D_MODEL), 1.0)
    offset = jax.random.randint(ks[1], (BATCH, 1), 0, 4096, dtype=jnp.int32)
    positions = offset + jnp.arange(SEQ, dtype=jnp.int32)[None, :]
    return {
        'x': x,
        'positions': positions,
        'attn_norm_g': gain(ks[2], (L, D_MODEL)),
        'w_in': nrm(ks[3], (L, D_MODEL, D_IN), D_MODEL ** -0.5),
        'rwkv_mu': jax.random.uniform(ks[4], (L, RWKV_SHIFT_DIM), jnp.float32),
        'rwkv_w0': jax.random.uniform(ks[5], (L, RWKV_WIDTH), jnp.float32, -6.0, -1.0),
        'rwkv_w2': nrm(ks[6], (L, DECAY_LORA, RWKV_WIDTH), 0.1 * DECAY_LORA ** -0.5),
        'rwkv_a0': nrm(ks[7], (L, RWKV_WIDTH), 0.5),
        'rwkv_a2': nrm(ks[8], (L, AAA_LORA, RWKV_WIDTH), 0.5 * AAA_LORA ** -0.5),
        'rwkv_g2': nrm(ks[9], (L, GATE_LORA, RWKV_WIDTH), GATE_LORA ** -0.5),
        'rwkv_k_k': 0.85 + nrm(ks[10], (L, RWKV_WIDTH), 0.05),
        'rwkv_k_a': gain(ks[11], (L, RWKV_WIDTH)),
        'rwkv_r_k': nrm(ks[12], (L, RWKV_HEADS, RWKV_HEAD), 0.1),
        'rwkv_gn_w': gain(ks[13], (L, RWKV_WIDTH)),
        'rwkv_gn_b': nrm(ks[14], (L, RWKV_WIDTH), 0.02),
        'mla_q_norm_g': gain(ks[15], (L, Q_LORA)),
        'mla_w_uq': nrm(ks[16], (L, Q_LORA, MLA_HEADS * (QK_NOPE + QK_ROPE)), Q_LORA ** -0.5),
        'mla_kv_norm_g': gain(ks[17], (L, KV_LORA)),
        'mla_w_ukv': nrm(ks[18], (L, KV_LORA, MLA_HEADS * (QK_NOPE + V_HEAD)), KV_LORA ** -0.5),
        'w_out': nrm(ks[19], (L, D_MIX, D_MODEL), D_MIX ** -0.5),
        'ffn_norm_g': gain(ks[20], (L, D_MODEL)),
        'ffn_w_gate': nrm(ks[21], (L, D_MODEL, D_FF), D_MODEL ** -0.5),
        'ffn_w_up': nrm(ks[22], (L, D_MODEL, D_FF), D_MODEL ** -0.5),
        'ffn_conv_w': nrm(ks[23], (L, CONV_W, D_FF), CONV_W ** -0.5),
        'ffn_conv_b': nrm(ks[24], (L, D_FF), 0.02),
        'ffn_w_down': nrm(ks[25], (L, D_FF, D_MODEL), D_FF ** -0.5),
        'final_norm_g': gain(ks[26], (D_MODEL,)),
    }


def _fwd_reference(x, positions, attn_norm_g, w_in, rwkv_mu, rwkv_w0, rwkv_w2, rwkv_a0, rwkv_a2,
              rwkv_g2, rwkv_k_k, rwkv_k_a, rwkv_r_k, rwkv_gn_w, rwkv_gn_b, mla_q_norm_g,
              mla_w_uq, mla_kv_norm_g, mla_w_ukv, w_out, ffn_norm_g, ffn_w_gate, ffn_w_up,
              ffn_conv_w, ffn_conv_b, ffn_w_down, final_norm_g):
    for l in range(DEPTH):
        h = rms_norm(x, attn_norm_g[l])
        proj = h @ w_in[l]
        shifted = token_shift(proj[..., :RWKV_SHIFT_DIM], rwkv_mu[l])
        h_r, h_k, h_v, h_w, h_a, h_g = _split(shifted, IN_SPLITS[:6])
        c_q, c_kv, k_pe = _split(proj[..., RWKV_SHIFT_DIM:], IN_SPLITS[6:])
        y_rwkv = rwkv7_mixer(h_r, h_k, h_v, h_w, h_a, h_g, rwkv_w0[l], rwkv_w2[l], rwkv_a0[l],
                             rwkv_a2[l], rwkv_g2[l], rwkv_k_k[l], rwkv_k_a[l], rwkv_r_k[l],
                             rwkv_gn_w[l], rwkv_gn_b[l])
        y_mla = mla_mixer(c_q, c_kv, k_pe, positions, mla_q_norm_g[l], mla_w_uq[l],
                          mla_kv_norm_g[l], mla_w_ukv[l])
        y = jnp.concatenate([y_rwkv.astype(x.dtype), y_mla.astype(x.dtype)], axis=-1)
        x = x + y @ w_out[l]
        h = rms_norm(x, ffn_norm_g[l])
        x = x + conv_glu_ffn(h, ffn_w_gate[l], ffn_w_up[l], ffn_conv_w[l], ffn_conv_b[l],
                             ffn_w_down[l]).astype(x.dtype)
    return rms_norm(x, final_norm_g)


import jax as _jax
import jax.numpy as _jnp

TWIN_FORMAT = 'train_step'
FWD_PARAMS = ['x', 'positions', 'attn_norm_g', 'w_in', 'rwkv_mu', 'rwkv_w0', 'rwkv_w2', 'rwkv_a0', 'rwkv_a2', 'rwkv_g2', 'rwkv_k_k', 'rwkv_k_a', 'rwkv_r_k', 'rwkv_gn_w', 'rwkv_gn_b', 'mla_q_norm_g', 'mla_w_uq', 'mla_kv_norm_g', 'mla_w_ukv', 'w_out', 'ffn_norm_g', 'ffn_w_gate', 'ffn_w_up', 'ffn_conv_w', 'ffn_conv_b', 'ffn_w_down', 'final_norm_g']
TWIN_WEIGHTS = ['attn_norm_g', 'w_in', 'rwkv_mu', 'rwkv_w0', 'rwkv_w2', 'rwkv_a0', 'rwkv_a2', 'rwkv_g2', 'rwkv_k_k', 'rwkv_k_a', 'rwkv_r_k', 'rwkv_gn_w', 'rwkv_gn_b', 'mla_q_norm_g', 'mla_w_uq', 'mla_kv_norm_g', 'mla_w_ukv', 'w_out', 'ffn_norm_g', 'ffn_w_gate', 'ffn_w_up', 'ffn_conv_w', 'ffn_conv_b', 'ffn_w_down', 'final_norm_g']
TWIN_DIFF_INPUT = 'x'
TWIN_INPUTS = ['x', 'positions', 'attn_norm_g', 'w_in', 'rwkv_mu', 'rwkv_w0', 'rwkv_w2', 'rwkv_a0', 'rwkv_a2', 'rwkv_g2', 'rwkv_k_k', 'rwkv_k_a', 'rwkv_r_k', 'rwkv_gn_w', 'rwkv_gn_b', 'mla_q_norm_g', 'mla_w_uq', 'mla_kv_norm_g', 'mla_w_ukv', 'w_out', 'ffn_norm_g', 'ffn_w_gate', 'ffn_w_up', 'ffn_conv_w', 'ffn_conv_b', 'ffn_w_down', 'final_norm_g', 'loss_target', 'm_attn_norm_g', 'm_w_in', 'm_rwkv_mu', 'm_rwkv_w0', 'm_rwkv_w2', 'm_rwkv_a0', 'm_rwkv_a2', 'm_rwkv_g2', 'm_rwkv_k_k', 'm_rwkv_k_a', 'm_rwkv_r_k', 'm_rwkv_gn_w', 'm_rwkv_gn_b', 'm_mla_q_norm_g', 'm_mla_w_uq', 'm_mla_kv_norm_g', 'm_mla_w_ukv', 'm_w_out', 'm_ffn_norm_g', 'm_ffn_w_gate', 'm_ffn_w_up', 'm_ffn_conv_w', 'm_ffn_conv_b', 'm_ffn_w_down', 'm_final_norm_g', 'v_attn_norm_g', 'v_w_in', 'v_rwkv_mu', 'v_rwkv_w0', 'v_rwkv_w2', 'v_rwkv_a0', 'v_rwkv_a2', 'v_rwkv_g2', 'v_rwkv_k_k', 'v_rwkv_k_a', 'v_rwkv_r_k', 'v_rwkv_gn_w', 'v_rwkv_gn_b', 'v_mla_q_norm_g', 'v_mla_w_uq', 'v_mla_kv_norm_g', 'v_mla_w_ukv', 'v_w_out', 'v_ffn_norm_g', 'v_ffn_w_gate', 'v_ffn_w_up', 'v_ffn_conv_w', 'v_ffn_conv_b', 'v_ffn_w_down', 'v_final_norm_g']
TWIN_OUTPUTS = ['loss', 'grad_x', 'grad_attn_norm_g', 'grad_w_in', 'grad_rwkv_mu', 'grad_rwkv_w0', 'grad_rwkv_w2', 'grad_rwkv_a0', 'grad_rwkv_a2', 'grad_rwkv_g2', 'grad_rwkv_k_k', 'grad_rwkv_k_a', 'grad_rwkv_r_k', 'grad_rwkv_gn_w', 'grad_rwkv_gn_b', 'grad_mla_q_norm_g', 'grad_mla_w_uq', 'grad_mla_kv_norm_g', 'grad_mla_w_ukv', 'grad_w_out', 'grad_ffn_norm_g', 'grad_ffn_w_gate', 'grad_ffn_w_up', 'grad_ffn_conv_w', 'grad_ffn_conv_b', 'grad_ffn_w_down', 'grad_final_norm_g', 'delta_attn_norm_g', 'delta_w_in', 'delta_rwkv_mu', 'delta_rwkv_w0', 'delta_rwkv_w2', 'delta_rwkv_a0', 'delta_rwkv_a2', 'delta_rwkv_g2', 'delta_rwkv_k_k', 'delta_rwkv_k_a', 'delta_rwkv_r_k', 'delta_rwkv_gn_w', 'delta_rwkv_gn_b', 'delta_mla_q_norm_g', 'delta_mla_w_uq', 'delta_mla_kv_norm_g', 'delta_mla_w_ukv', 'delta_w_out', 'delta_ffn_norm_g', 'delta_ffn_w_gate', 'delta_ffn_w_up', 'delta_ffn_conv_w', 'delta_ffn_conv_b', 'delta_ffn_w_down', 'delta_final_norm_g', 'new_m_attn_norm_g', 'new_m_w_in', 'new_m_rwkv_mu', 'new_m_rwkv_w0', 'new_m_rwkv_w2', 'new_m_rwkv_a0', 'new_m_rwkv_a2', 'new_m_rwkv_g2', 'new_m_rwkv_k_k', 'new_m_rwkv_k_a', 'new_m_rwkv_r_k', 'new_m_rwkv_gn_w', 'new_m_rwkv_gn_b', 'new_m_mla_q_norm_g', 'new_m_mla_w_uq', 'new_m_mla_kv_norm_g', 'new_m_mla_w_ukv', 'new_m_w_out', 'new_m_ffn_norm_g', 'new_m_ffn_w_gate', 'new_m_ffn_w_up', 'new_m_ffn_conv_w', 'new_m_ffn_conv_b', 'new_m_ffn_w_down', 'new_m_final_norm_g', 'new_v_attn_norm_g', 'new_v_w_in', 'new_v_rwkv_mu', 'new_v_rwkv_w0', 'new_v_rwkv_w2', 'new_v_rwkv_a0', 'new_v_rwkv_a2', 'new_v_rwkv_g2', 'new_v_rwkv_k_k', 'new_v_rwkv_k_a', 'new_v_rwkv_r_k', 'new_v_rwkv_gn_w', 'new_v_rwkv_gn_b', 'new_v_mla_q_norm_g', 'new_v_mla_w_uq', 'new_v_mla_kv_norm_g', 'new_v_mla_w_ukv', 'new_v_w_out', 'new_v_ffn_norm_g', 'new_v_ffn_w_gate', 'new_v_ffn_w_up', 'new_v_ffn_conv_w', 'new_v_ffn_conv_b', 'new_v_ffn_w_down', 'new_v_final_norm_g']
TWIN_LEAF_KINDS = {'loss': 'loss', 'grad_x': 'grad_x', 'grad_attn_norm_g': 'grad_w', 'grad_w_in': 'grad_w', 'grad_rwkv_mu': 'grad_w', 'grad_rwkv_w0': 'grad_w', 'grad_rwkv_w2': 'grad_w', 'grad_rwkv_a0': 'grad_w', 'grad_rwkv_a2': 'grad_w', 'grad_rwkv_g2': 'grad_w', 'grad_rwkv_k_k': 'grad_w', 'grad_rwkv_k_a': 'grad_w', 'grad_rwkv_r_k': 'grad_w', 'grad_rwkv_gn_w': 'grad_w', 'grad_rwkv_gn_b': 'grad_w', 'grad_mla_q_norm_g': 'grad_w', 'grad_mla_w_uq': 'grad_w', 'grad_mla_kv_norm_g': 'grad_w', 'grad_mla_w_ukv': 'grad_w', 'grad_w_out': 'grad_w', 'grad_ffn_norm_g': 'grad_w', 'grad_ffn_w_gate': 'grad_w', 'grad_ffn_w_up': 'grad_w', 'grad_ffn_conv_w': 'grad_w', 'grad_ffn_conv_b': 'grad_w', 'grad_ffn_w_down': 'grad_w', 'grad_final_norm_g': 'grad_w', 'delta_attn_norm_g': 'delta_w', 'delta_w_in': 'delta_w', 'delta_rwkv_mu': 'delta_w', 'delta_rwkv_w0': 'delta_w', 'delta_rwkv_w2': 'delta_w', 'delta_rwkv_a0': 'delta_w', 'delta_rwkv_a2': 'delta_w', 'delta_rwkv_g2': 'delta_w', 'delta_rwkv_k_k': 'delta_w', 'delta_rwkv_k_a': 'delta_w', 'delta_rwkv_r_k': 'delta_w', 'delta_rwkv_gn_w': 'delta_w', 'delta_rwkv_gn_b': 'delta_w', 'delta_mla_q_norm_g': 'delta_w', 'delta_mla_w_uq': 'delta_w', 'delta_mla_kv_norm_g': 'delta_w', 'delta_mla_w_ukv': 'delta_w', 'delta_w_out': 'delta_w', 'delta_ffn_norm_g': 'delta_w', 'delta_ffn_w_gate': 'delta_w', 'delta_ffn_w_up': 'delta_w', 'delta_ffn_conv_w': 'delta_w', 'delta_ffn_conv_b': 'delta_w', 'delta_ffn_w_down': 'delta_w', 'delta_final_norm_g': 'delta_w', 'new_m_attn_norm_g': 'new_m', 'new_m_w_in': 'new_m', 'new_m_rwkv_mu': 'new_m', 'new_m_rwkv_w0': 'new_m', 'new_m_rwkv_w2': 'new_m', 'new_m_rwkv_a0': 'new_m', 'new_m_rwkv_a2': 'new_m', 'new_m_rwkv_g2': 'new_m', 'new_m_rwkv_k_k': 'new_m', 'new_m_rwkv_k_a': 'new_m', 'new_m_rwkv_r_k': 'new_m', 'new_m_rwkv_gn_w': 'new_m', 'new_m_rwkv_gn_b': 'new_m', 'new_m_mla_q_norm_g': 'new_m', 'new_m_mla_w_uq': 'new_m', 'new_m_mla_kv_norm_g': 'new_m', 'new_m_mla_w_ukv': 'new_m', 'new_m_w_out': 'new_m', 'new_m_ffn_norm_g': 'new_m', 'new_m_ffn_w_gate': 'new_m', 'new_m_ffn_w_up': 'new_m', 'new_m_ffn_conv_w': 'new_m', 'new_m_ffn_conv_b': 'new_m', 'new_m_ffn_w_down': 'new_m', 'new_m_final_norm_g': 'new_m', 'new_v_attn_norm_g': 'new_v', 'new_v_w_in': 'new_v', 'new_v_rwkv_mu': 'new_v', 'new_v_rwkv_w0': 'new_v', 'new_v_rwkv_w2': 'new_v', 'new_v_rwkv_a0': 'new_v', 'new_v_rwkv_a2': 'new_v', 'new_v_rwkv_g2': 'new_v', 'new_v_rwkv_k_k': 'new_v', 'new_v_rwkv_k_a': 'new_v', 'new_v_rwkv_r_k': 'new_v', 'new_v_rwkv_gn_w': 'new_v', 'new_v_rwkv_gn_b': 'new_v', 'new_v_mla_q_norm_g': 'new_v', 'new_v_mla_w_uq': 'new_v', 'new_v_mla_kv_norm_g': 'new_v', 'new_v_mla_w_ukv': 'new_v', 'new_v_w_out': 'new_v', 'new_v_ffn_norm_g': 'new_v', 'new_v_ffn_w_gate': 'new_v', 'new_v_ffn_w_up': 'new_v', 'new_v_ffn_conv_w': 'new_v', 'new_v_ffn_conv_b': 'new_v', 'new_v_ffn_w_down': 'new_v', 'new_v_final_norm_g': 'new_v'}


def _forward(args):
    return _fwd_reference(*[args[k] for k in FWD_PARAMS])


def _output_shape():
    out = _jax.eval_shape(lambda: _forward(_fwd_setup_inputs(0)))
    return out.shape, out.dtype

N_MICROBATCH = 1
ADAM_LR = 0.001
ADAM_B1 = 0.9
ADAM_B2 = 0.999
ADAM_EPS = 1e-08
ADAM_WD = 0.01
ADAM_STEP = 10
PER_EXAMPLE_BATCH_AXIS = {'x': 0, 'positions': 0, 'loss_target': 0}
SHARED_INPUTS = []
_WEIGHT_DTYPES = {'attn_norm_g': _jnp.float32, 'w_in': _jnp.float32, 'rwkv_mu': _jnp.float32, 'rwkv_w0': _jnp.float32, 'rwkv_w2': _jnp.float32, 'rwkv_a0': _jnp.float32, 'rwkv_a2': _jnp.float32, 'rwkv_g2': _jnp.float32, 'rwkv_k_k': _jnp.float32, 'rwkv_k_a': _jnp.float32, 'rwkv_r_k': _jnp.float32, 'rwkv_gn_w': _jnp.float32, 'rwkv_gn_b': _jnp.float32, 'mla_q_norm_g': _jnp.float32, 'mla_w_uq': _jnp.float32, 'mla_kv_norm_g': _jnp.float32, 'mla_w_ukv': _jnp.float32, 'w_out': _jnp.float32, 'ffn_norm_g': _jnp.float32, 'ffn_w_gate': _jnp.float32, 'ffn_w_up': _jnp.float32, 'ffn_conv_w': _jnp.float32, 'ffn_conv_b': _jnp.float32, 'ffn_w_down': _jnp.float32, 'final_norm_g': _jnp.float32}
MOMENT_SCALE = {'attn_norm_g': 4.981742e-02, 'w_in': 3.381485e-02, 'rwkv_mu': 5.974823e-02, 'rwkv_w0': 1.336343e-02, 'rwkv_w2': 1.657780e-03, 'rwkv_a0': 1.492676e-02, 'rwkv_a2': 1.408851e-02, 'rwkv_g2': 3.683995e-02, 'rwkv_k_k': 4.316829e-02, 'rwkv_k_a': 4.255588e-02, 'rwkv_r_k': 8.219660e-02, 'rwkv_gn_w': 3.597862e-02, 'rwkv_gn_b': 4.340819e-02, 'mla_q_norm_g': 1.669165e-02, 'mla_w_uq': 9.470350e-03, 'mla_kv_norm_g': 2.418716e-02, 'mla_w_ukv': 1.185915e-02, 'w_out': 2.718355e-02, 'ffn_norm_g': 4.526634e-02, 'ffn_w_gate': 1.970483e-02, 'ffn_w_up': 1.915267e-02, 'ffn_conv_w': 1.977194e-02, 'ffn_conv_b': 1.871145e-02, 'ffn_w_down': 3.183561e-02, 'final_norm_g': 8.012683e+00}


def _to_microbatches(a, axis):
    t = _jnp.moveaxis(a, axis, 0)
    t = t.reshape((N_MICROBATCH, t.shape[0] // N_MICROBATCH) + t.shape[1:])
    return _jnp.moveaxis(t, 1, axis + 1)


def setup_inputs(seed: int = 0) -> dict:
    inp = _fwd_setup_inputs(seed)
    key = _jax.random.fold_in(_jax.random.key(seed), 7919)
    shape, _ = _output_shape()
    out = dict(inp)
    out["loss_target"] = _jax.random.normal(_jax.random.fold_in(key, 0), shape, _jnp.float32)
    for i, name in enumerate(TWIN_WEIGHTS):
        w = inp[name].astype(_jnp.float32)
        if MOMENT_SCALE is None:
            s = _jnp.sqrt(_jnp.mean(_jnp.square(w)) + 1e-30)
        else:
            s = MOMENT_SCALE[name]
        km, kv = _jax.random.split(_jax.random.fold_in(key, i + 1))
        out[name] = w
        out["m_" + name] = s * _jax.random.normal(km, w.shape, _jnp.float32)
        out["v_" + name] = (s * s) * _jax.random.uniform(kv, w.shape, _jnp.float32, 0.5, 1.5)
    if N_MICROBATCH > 1:
        for name, axis in PER_EXAMPLE_BATCH_AXIS.items():
            out[name] = _to_microbatches(out[name], axis)
    return {'x': out['x'], 'positions': out['positions'], 'attn_norm_g': out['attn_norm_g'], 'w_in': out['w_in'], 'rwkv_mu': out['rwkv_mu'], 'rwkv_w0': out['rwkv_w0'], 'rwkv_w2': out['rwkv_w2'], 'rwkv_a0': out['rwkv_a0'], 'rwkv_a2': out['rwkv_a2'], 'rwkv_g2': out['rwkv_g2'], 'rwkv_k_k': out['rwkv_k_k'], 'rwkv_k_a': out['rwkv_k_a'], 'rwkv_r_k': out['rwkv_r_k'], 'rwkv_gn_w': out['rwkv_gn_w'], 'rwkv_gn_b': out['rwkv_gn_b'], 'mla_q_norm_g': out['mla_q_norm_g'], 'mla_w_uq': out['mla_w_uq'], 'mla_kv_norm_g': out['mla_kv_norm_g'], 'mla_w_ukv': out['mla_w_ukv'], 'w_out': out['w_out'], 'ffn_norm_g': out['ffn_norm_g'], 'ffn_w_gate': out['ffn_w_gate'], 'ffn_w_up': out['ffn_w_up'], 'ffn_conv_w': out['ffn_conv_w'], 'ffn_conv_b': out['ffn_conv_b'], 'ffn_w_down': out['ffn_w_down'], 'final_norm_g': out['final_norm_g'], 'loss_target': out['loss_target'], 'm_attn_norm_g': out['m_attn_norm_g'], 'm_w_in': out['m_w_in'], 'm_rwkv_mu': out['m_rwkv_mu'], 'm_rwkv_w0': out['m_rwkv_w0'], 'm_rwkv_w2': out['m_rwkv_w2'], 'm_rwkv_a0': out['m_rwkv_a0'], 'm_rwkv_a2': out['m_rwkv_a2'], 'm_rwkv_g2': out['m_rwkv_g2'], 'm_rwkv_k_k': out['m_rwkv_k_k'], 'm_rwkv_k_a': out['m_rwkv_k_a'], 'm_rwkv_r_k': out['m_rwkv_r_k'], 'm_rwkv_gn_w': out['m_rwkv_gn_w'], 'm_rwkv_gn_b': out['m_rwkv_gn_b'], 'm_mla_q_norm_g': out['m_mla_q_norm_g'], 'm_mla_w_uq': out['m_mla_w_uq'], 'm_mla_kv_norm_g': out['m_mla_kv_norm_g'], 'm_mla_w_ukv': out['m_mla_w_ukv'], 'm_w_out': out['m_w_out'], 'm_ffn_norm_g': out['m_ffn_norm_g'], 'm_ffn_w_gate': out['m_ffn_w_gate'], 'm_ffn_w_up': out['m_ffn_w_up'], 'm_ffn_conv_w': out['m_ffn_conv_w'], 'm_ffn_conv_b': out['m_ffn_conv_b'], 'm_ffn_w_down': out['m_ffn_w_down'], 'm_final_norm_g': out['m_final_norm_g'], 'v_attn_norm_g': out['v_attn_norm_g'], 'v_w_in': out['v_w_in'], 'v_rwkv_mu': out['v_rwkv_mu'], 'v_rwkv_w0': out['v_rwkv_w0'], 'v_rwkv_w2': out['v_rwkv_w2'], 'v_rwkv_a0': out['v_rwkv_a0'], 'v_rwkv_a2': out['v_rwkv_a2'], 'v_rwkv_g2': out['v_rwkv_g2'], 'v_rwkv_k_k': out['v_rwkv_k_k'], 'v_rwkv_k_a': out['v_rwkv_k_a'], 'v_rwkv_r_k': out['v_rwkv_r_k'], 'v_rwkv_gn_w': out['v_rwkv_gn_w'], 'v_rwkv_gn_b': out['v_rwkv_gn_b'], 'v_mla_q_norm_g': out['v_mla_q_norm_g'], 'v_mla_w_uq': out['v_mla_w_uq'], 'v_mla_kv_norm_g': out['v_mla_kv_norm_g'], 'v_mla_w_ukv': out['v_mla_w_ukv'], 'v_w_out': out['v_w_out'], 'v_ffn_norm_g': out['v_ffn_norm_g'], 'v_ffn_w_gate': out['v_ffn_w_gate'], 'v_ffn_w_up': out['v_ffn_w_up'], 'v_ffn_conv_w': out['v_ffn_conv_w'], 'v_ffn_conv_b': out['v_ffn_conv_b'], 'v_ffn_w_down': out['v_ffn_w_down'], 'v_final_norm_g': out['v_final_norm_g']}


def _loss(weights, diff, rest, loss_target):
    with _jax.named_scope("forward"):
        args = {**rest, TWIN_DIFF_INPUT: diff, **{k: w.astype(_WEIGHT_DTYPES[k]) for k, w in weights.items()}}
        y = _forward(args)
    with _jax.named_scope("loss_head"):
        err = _jnp.square(y.astype(_jnp.float32) - loss_target)
        return 0.5 * _jnp.sum(_jnp.mean(err, axis=-1)) if err.ndim else 0.5 * err


def _adamw(w, g, m, v):
    m = ADAM_B1 * m + (1.0 - ADAM_B1) * g
    v = ADAM_B2 * v + (1.0 - ADAM_B2) * _jnp.square(g)
    m_hat = m / (1.0 - ADAM_B1 ** ADAM_STEP)
    v_hat = v / (1.0 - ADAM_B2 ** ADAM_STEP)
    delta = -ADAM_LR * (m_hat / (_jnp.sqrt(v_hat) + ADAM_EPS) + ADAM_WD * w)
    return delta, m, v


def reference(x, positions, attn_norm_g, w_in, rwkv_mu, rwkv_w0, rwkv_w2, rwkv_a0, rwkv_a2, rwkv_g2, rwkv_k_k, rwkv_k_a, rwkv_r_k, rwkv_gn_w, rwkv_gn_b, mla_q_norm_g, mla_w_uq, mla_kv_norm_g, mla_w_ukv, w_out, ffn_norm_g, ffn_w_gate, ffn_w_up, ffn_conv_w, ffn_conv_b, ffn_w_down, final_norm_g, loss_target, m_attn_norm_g, m_w_in, m_rwkv_mu, m_rwkv_w0, m_rwkv_w2, m_rwkv_a0, m_rwkv_a2, m_rwkv_g2, m_rwkv_k_k, m_rwkv_k_a, m_rwkv_r_k, m_rwkv_gn_w, m_rwkv_gn_b, m_mla_q_norm_g, m_mla_w_uq, m_mla_kv_norm_g, m_mla_w_ukv, m_w_out, m_ffn_norm_g, m_ffn_w_gate, m_ffn_w_up, m_ffn_conv_w, m_ffn_conv_b, m_ffn_w_down, m_final_norm_g, v_attn_norm_g, v_w_in, v_rwkv_mu, v_rwkv_w0, v_rwkv_w2, v_rwkv_a0, v_rwkv_a2, v_rwkv_g2, v_rwkv_k_k, v_rwkv_k_a, v_rwkv_r_k, v_rwkv_gn_w, v_rwkv_gn_b, v_mla_q_norm_g, v_mla_w_uq, v_mla_kv_norm_g, v_mla_w_ukv, v_w_out, v_ffn_norm_g, v_ffn_w_gate, v_ffn_w_up, v_ffn_conv_w, v_ffn_conv_b, v_ffn_w_down, v_final_norm_g):
    given = dict(x=x, positions=positions, attn_norm_g=attn_norm_g, w_in=w_in, rwkv_mu=rwkv_mu, rwkv_w0=rwkv_w0, rwkv_w2=rwkv_w2, rwkv_a0=rwkv_a0, rwkv_a2=rwkv_a2, rwkv_g2=rwkv_g2, rwkv_k_k=rwkv_k_k, rwkv_k_a=rwkv_k_a, rwkv_r_k=rwkv_r_k, rwkv_gn_w=rwkv_gn_w, rwkv_gn_b=rwkv_gn_b, mla_q_norm_g=mla_q_norm_g, mla_w_uq=mla_w_uq, mla_kv_norm_g=mla_kv_norm_g, mla_w_ukv=mla_w_ukv, w_out=w_out, ffn_norm_g=ffn_norm_g, ffn_w_gate=ffn_w_gate, ffn_w_up=ffn_w_up, ffn_conv_w=ffn_conv_w, ffn_conv_b=ffn_conv_b, ffn_w_down=ffn_w_down, final_norm_g=final_norm_g, loss_target=loss_target, m_attn_norm_g=m_attn_norm_g, m_w_in=m_w_in, m_rwkv_mu=m_rwkv_mu, m_rwkv_w0=m_rwkv_w0, m_rwkv_w2=m_rwkv_w2, m_rwkv_a0=m_rwkv_a0, m_rwkv_a2=m_rwkv_a2, m_rwkv_g2=m_rwkv_g2, m_rwkv_k_k=m_rwkv_k_k, m_rwkv_k_a=m_rwkv_k_a, m_rwkv_r_k=m_rwkv_r_k, m_rwkv_gn_w=m_rwkv_gn_w, m_rwkv_gn_b=m_rwkv_gn_b, m_mla_q_norm_g=m_mla_q_norm_g, m_mla_w_uq=m_mla_w_uq, m_mla_kv_norm_g=m_mla_kv_norm_g, m_mla_w_ukv=m_mla_w_ukv, m_w_out=m_w_out, m_ffn_norm_g=m_ffn_norm_g, m_ffn_w_gate=m_ffn_w_gate, m_ffn_w_up=m_ffn_w_up, m_ffn_conv_w=m_ffn_conv_w, m_ffn_conv_b=m_ffn_conv_b, m_ffn_w_down=m_ffn_w_down, m_final_norm_g=m_final_norm_g, v_attn_norm_g=v_attn_norm_g, v_w_in=v_w_in, v_rwkv_mu=v_rwkv_mu, v_rwkv_w0=v_rwkv_w0, v_rwkv_w2=v_rwkv_w2, v_rwkv_a0=v_rwkv_a0, v_rwkv_a2=v_rwkv_a2, v_rwkv_g2=v_rwkv_g2, v_rwkv_k_k=v_rwkv_k_k, v_rwkv_k_a=v_rwkv_k_a, v_rwkv_r_k=v_rwkv_r_k, v_rwkv_gn_w=v_rwkv_gn_w, v_rwkv_gn_b=v_rwkv_gn_b, v_mla_q_norm_g=v_mla_q_norm_g, v_mla_w_uq=v_mla_w_uq, v_mla_kv_norm_g=v_mla_kv_norm_g, v_mla_w_ukv=v_mla_w_ukv, v_w_out=v_w_out, v_ffn_norm_g=v_ffn_norm_g, v_ffn_w_gate=v_ffn_w_gate, v_ffn_w_up=v_ffn_w_up, v_ffn_conv_w=v_ffn_conv_w, v_ffn_conv_b=v_ffn_conv_b, v_ffn_w_down=v_ffn_w_down, v_final_norm_g=v_final_norm_g)
    weights = {n: given[n] for n in TWIN_WEIGHTS}
    shared = {n: given[n] for n in SHARED_INPUTS}
    per_example = {n: given[n] for n in ['x', 'positions']}
    grad_fn = _jax.value_and_grad(_loss, argnums=(0, 1))

    def one_microbatch(ex, loss_target):
        ex = dict(ex)
        diff = ex.pop(TWIN_DIFF_INPUT)
        return grad_fn(weights, diff, {**shared, **ex}, loss_target)

    if N_MICROBATCH == 1:
        loss, (grad_w, grad_x) = one_microbatch(per_example, given["loss_target"])
    else:
        def body(carry, xs):
            loss_sum, grad_sum = carry
            l_k, (gw_k, gx_k) = one_microbatch(xs[0], xs[1])
            with _jax.named_scope("update"):
                return (loss_sum + l_k, _jax.tree.map(_jnp.add, grad_sum, gw_k)), gx_k

        init = (_jnp.zeros((), _jnp.float32), _jax.tree.map(_jnp.zeros_like, weights))
        (loss, grad_w), grad_x = _jax.lax.scan(body, init, (per_example, given["loss_target"]))
    with _jax.named_scope("update"):
        delta_w, new_m, new_v = {}, {}, {}
        for n in TWIN_WEIGHTS:
            delta_w[n], new_m[n], new_v[n] = _adamw(weights[n], grad_w[n], given["m_" + n], given["v_" + n])
    return (loss, grad_x, *[grad_w[n] for n in TWIN_WEIGHTS], *[delta_w[n] for n in TWIN_WEIGHTS],
            *[new_m[n] for n in TWIN_WEIGHTS], *[new_v[n] for n in TWIN_WEIGHTS])
```

```python
import functools
import math

import jax
import jax.numpy as jnp
import numpy as np
from jax import lax
from jax.experimental import pallas as pl
from jax.experimental.pallas import tpu as pltpu

F32 = jnp.float32
BF16 = jnp.bfloat16
HIGHEST = lax.Precision.HIGHEST
MESH = pl.DeviceIdType.MESH

N_DEV = 8
LANES = 128
SUBLANES = 8
VMEM_LIMIT = 48 * 1024 * 1024

NORM_EPS = 1e-6
GN_EPS = 64e-5
RWKV_HEAD = 64
QK_NOPE = 128
QK_ROPE = 64
V_HEAD = 128
ROPE_THETA = 10000.0
CONV_W = 3
NEG_INF = -1e30
SCAN_CHUNK = 64

ADAM_LR = 0.001
ADAM_B1 = 0.9
ADAM_B2 = 0.999
ADAM_EPS = 1e-08
ADAM_WD = 0.01
ADAM_STEP = 10

WEIGHTS = ['attn_norm_g', 'w_in', 'rwkv_mu', 'rwkv_w0', 'rwkv_w2', 'rwkv_a0', 'rwkv_a2', 'rwkv_g2', 'rwkv_k_k',
           'rwkv_k_a', 'rwkv_r_k', 'rwkv_gn_w', 'rwkv_gn_b', 'mla_q_norm_g', 'mla_w_uq', 'mla_kv_norm_g', 'mla_w_ukv',
           'w_out', 'ffn_norm_g', 'ffn_w_gate', 'ffn_w_up', 'ffn_conv_w', 'ffn_conv_b', 'ffn_w_down', 'final_norm_g']
BIG = {'w_in': 'col', 'mla_w_uq': 'col', 'mla_w_ukv': 'col', 'w_out': 'row', 'ffn_w_gate': 'col', 'ffn_w_up': 'col',
       'ffn_w_down': 'row'}
SMALL_SHARDED = {'rwkv_w2': 'col', 'rwkv_a2': 'col', 'rwkv_g2': 'col', 'ffn_conv_w': 'col'}
SHARDED = {**BIG, **SMALL_SHARDED}
REPLICATED = [n for n in WEIGHTS if n not in SHARDED]


def _round_up(n, m):
    return (n + m - 1) // m * m


def _pick(n, cap, unit):
    if n <= cap:
        return n
    best = None
    for t in range(unit, cap + 1, unit):
        if n % t == 0:
            best = t
    assert best is not None, (n, cap, unit)
    return best


def _params(sem):
    return pltpu.CompilerParams(dimension_semantics=sem, vmem_limit_bytes=VMEM_LIMIT)


def mm(name, a, b, add=None, out_dtype=F32):
    m, k = a.shape
    k2, n = b.shape
    assert k == k2, (name, a.shape, b.shape)
    tm = _pick(m, 512, SUBLANES * 2)
    tn = _pick(n, 640, LANES)
    has_add = add is not None

    def body(a_ref, b_ref, *rest):
        o_ref = rest[-1]
        acc = jnp.dot(a_ref[...].astype(BF16), b_ref[...].astype(BF16), preferred_element_type=F32)
        if has_add:
            acc = acc + rest[0][...].astype(F32)
        o_ref[...] = acc.astype(o_ref.dtype)

    in_specs = [pl.BlockSpec((tm, k), lambda i, j: (i, 0)), pl.BlockSpec((k, tn), lambda i, j: (0, j))]
    ops = [a, b]
    if has_add:
        in_specs.append(pl.BlockSpec((tm, tn), lambda i, j: (i, j)))
        ops.append(add)
    return pl.pallas_call(
        body, name=name, grid=(m // tm, n // tn), in_specs=in_specs,
        out_specs=pl.BlockSpec((tm, tn), lambda i, j: (i, j)),
        out_shape=jax.ShapeDtypeStruct((m, n), out_dtype),
        compiler_params=_params(("parallel", "parallel")),
    )(*ops)


def mm_nt(name, a, b, out_dtype=F32):
    m, k = a.shape
    n, k2 = b.shape
    assert k == k2, (name, a.shape, b.shape)
    tm = _pick(m, 512, SUBLANES * 2)
    tn = _pick(n, 1024, LANES)

    def body(a_ref, b_ref, o_ref):
        acc = lax.dot_general(a_ref[...].astype(BF16), b_ref[...].astype(BF16), (((1,), (1,)), ((), ())),
                              preferred_element_type=F32)
        o_ref[...] = acc.astype(o_ref.dtype)

    return pl.pallas_call(
        body, name=name, grid=(m // tm, n // tn),
        in_specs=[pl.BlockSpec((tm, k), lambda i, j: (i, 0)), pl.BlockSpec((tn, k), lambda i, j: (j, 0))],
        out_specs=pl.BlockSpec((tm, tn), lambda i, j: (i, j)),
        out_shape=jax.ShapeDtypeStruct((m, n), out_dtype),
        compiler_params=_params(("parallel", "parallel")),
    )(a, b)


def mm_sh(name, a, g, out_dtype=F32):
    m, k = a.shape
    nd, k2, nbp = g.shape
    assert k == k2, (name, a.shape, g.shape)
    tm = _pick(m, 512, SUBLANES * 2)

    def body(a_ref, b_ref, o_ref):
        o_ref[...] = jnp.dot(a_ref[...].astype(BF16), b_ref[...].astype(BF16), preferred_element_type=F32).astype(o_ref.dtype)

    return pl.pallas_call(
        body, name=name, grid=(m // tm, nd),
        in_specs=[pl.BlockSpec((tm, k), lambda i, j: (i, 0)), pl.BlockSpec((None, k, nbp), lambda i, j: (j, 0, 0))],
        out_specs=pl.BlockSpec((tm, nbp), lambda i, j: (i, j)),
        out_shape=jax.ShapeDtypeStruct((m, nd * nbp), out_dtype),
        compiler_params=_params(("parallel", "parallel")),
    )(a, g)


def mm_sh_nt(name, a, g, add=None):
    m, n = a.shape
    nd, k, nbp = g.shape
    assert n == nd * nbp, (name, a.shape, g.shape)
    tm = _pick(m, 512, SUBLANES * 2)
    has_add = add is not None

    def body(a_ref, b_ref, *rest):
        o_ref = rest[-1]
        part = lax.dot_general(a_ref[...].astype(BF16), b_ref[...].astype(BF16), (((1,), (1,)), ((), ())),
                               preferred_element_type=F32)

        @pl.when(pl.program_id(1) == 0)
        def _():
            o_ref[...] = part + rest[0][...] if has_add else part

        @pl.when(pl.program_id(1) != 0)
        def _():
            o_ref[...] += part

    in_specs = [pl.BlockSpec((tm, nbp), lambda i, j: (i, j)), pl.BlockSpec((None, k, nbp), lambda i, j: (j, 0, 0))]
    ops = [a, g]
    if has_add:
        in_specs.append(pl.BlockSpec((tm, k), lambda i, j: (i, 0)))
        ops.append(add)
    return pl.pallas_call(
        body, name=name, grid=(m // tm, nd), in_specs=in_specs,
        out_specs=pl.BlockSpec((tm, k), lambda i, j: (i, 0)),
        out_shape=jax.ShapeDtypeStruct((m, k), F32),
        compiler_params=_params(("parallel", "arbitrary")),
    )(*ops)


def mm_sh_out(name, at, b):
    k, m = at.shape
    m2, n = b.shape
    assert m == m2 and n % N_DEV == 0, (name, at.shape, b.shape)
    nbp = n // N_DEV
    tk = _pick(k, 512, SUBLANES * 2)

    def body(a_ref, b_ref, o_ref):
        o_ref[...] = jnp.dot(a_ref[...].astype(BF16), b_ref[...].astype(BF16), preferred_element_type=F32)

    return pl.pallas_call(
        body, name=name, grid=(k // tk, N_DEV),
        in_specs=[pl.BlockSpec((tk, m), lambda i, j: (i, 0)), pl.BlockSpec((m, nbp), lambda i, j: (0, j))],
        out_specs=pl.BlockSpec((None, tk, nbp), lambda i, j: (j, i, 0)),
        out_shape=jax.ShapeDtypeStruct((N_DEV, k, nbp), F32),
        compiler_params=_params(("parallel", "parallel")),
    )(at, b)


def pad_cols(y, nb, nbp):
    m = y.shape[0]
    if nb == nbp:
        return y
    return jnp.pad(y.reshape(m, N_DEV, nb), ((0, 0), (0, 0), (0, nbp - nb))).reshape(m, N_DEV * nbp)


def unpad_cols(y, nb, nbp):
    m = y.shape[0]
    if nb == nbp:
        return y
    return y.reshape(m, N_DEV, nbp)[:, :, :nb].reshape(m, N_DEV * nb)


def _in_spec(kind, a, tm):
    if kind == 'row':
        return pl.BlockSpec((tm, a.shape[1]), lambda h, i: (i, 0))
    if kind == 'hrow':
        return pl.BlockSpec((None, tm, a.shape[2]), lambda h, i: (h, i, 0))
    if kind == 'const':
        return pl.BlockSpec(a.shape, lambda h, i: (0, 0))
    assert kind == 'hconst', kind
    return pl.BlockSpec((None,) + a.shape[1:], lambda h, i: (h, 0, 0))


def _row_out(kind, c, dtype, heads, s, tm):
    if kind == 'row':
        assert heads == 1
        return jax.ShapeDtypeStruct((s, c), dtype), pl.BlockSpec((tm, c), lambda h, i: (i, 0))
    return jax.ShapeDtypeStruct((heads, s, c), dtype), pl.BlockSpec((None, tm, c), lambda h, i: (h, i, 0))


def rowwise(name, fn, arrs, kinds, outs, *, heads, s, tm):
    n_in = len(arrs)

    def body(*refs):
        vals = fn(*[r[...] for r in refs[:n_in]])
        for o, v in zip(refs[n_in:], vals, strict=True):
            o[...] = v.astype(o.dtype)

    shapes, specs = zip(*[_row_out(k, c, dt, heads, s, tm) for k, c, dt in outs])
    return pl.pallas_call(
        body, name=name, grid=(heads, s // tm),
        in_specs=[_in_spec(k, a, tm) for k, a in zip(kinds, arrs, strict=True)],
        out_specs=list(specs), out_shape=list(shapes),
        compiler_params=_params(("parallel", "parallel")),
    )(*arrs)


def rowwise_vjp(name, fn, arrs, kinds, cots, cot_kinds, wrt, *, heads, s, tm, out_dtypes=None, primal=False):
    n_in, n_cot = len(arrs), len(cots)
    nb = s // tm
    out_dtypes = out_dtypes or [F32] * len(wrt)

    def body(*refs):
        vals = [r[...] for r in refs[:n_in]]
        cvals = tuple(r[...].astype(F32) for r in refs[n_in:n_in + n_cot])
        outs = refs[n_in + n_cot:]

        def f(*dv):
            full = list(vals)
            for j, i in enumerate(wrt):
                full[i] = dv[j]
            return tuple(fn(*full))

        prim, vjp_fn = jax.vjp(f, *[vals[i].astype(F32) for i in wrt])
        grads = vjp_fn(cvals)
        for o, g in zip(outs[:len(wrt)], grads, strict=True):
            o[...] = g.astype(o.dtype)
        if primal:
            for o, p in zip(outs[len(wrt):], prim, strict=True):
                o[...] = p.astype(o.dtype)

    shapes, specs = [], []
    for i, dt in zip(wrt, out_dtypes, strict=True):
        kind, a = kinds[i], arrs[i]
        if kind in ('row', 'hrow'):
            c = a.shape[-1]
            sh, sp = _row_out('row' if (kind == 'row' and heads == 1) else 'hrow', c, dt, heads, s, tm)
        else:
            r, c = a.shape[-2:]
            sh = jax.ShapeDtypeStruct((heads, nb, r, c), dt)
            sp = pl.BlockSpec((None, None, r, c), lambda h, i: (h, i, 0, 0))
        shapes.append(sh)
        specs.append(sp)
    if primal:
        for ck, c in zip(cot_kinds, cots, strict=True):
            sh, sp = _row_out(ck, c.shape[-1], F32, heads, s, tm)
            shapes.append(sh)
            specs.append(sp)
    in_specs = [_in_spec(k, a, tm) for k, a in zip(kinds, arrs, strict=True)]
    in_specs += [_in_spec(k, a, tm) for k, a in zip(cot_kinds, cots, strict=True)]
    return pl.pallas_call(
        body, name=name, grid=(heads, nb), in_specs=in_specs, out_specs=specs, out_shape=shapes,
        compiler_params=_params(("parallel", "parallel")),
    )(*arrs, *cots)


def colsum(name, x):
    n, m = x.shape
    tc = _pick(m, 32768, LANES) if m % LANES == 0 else m

    def body(x_ref, o_ref):
        acc = x_ref[0:1, :].astype(F32)
        for r in range(1, n):
            acc = acc + x_ref[r:r + 1, :].astype(F32)
        o_ref[...] = acc

    return pl.pallas_call(
        body, name=name, grid=(m // tc,), in_specs=[pl.BlockSpec((n, tc), lambda j: (0, j))],
        out_specs=pl.BlockSpec((1, tc), lambda j: (0, j)), out_shape=jax.ShapeDtypeStruct((1, m), F32),
        compiler_params=_params(("parallel",)),
    )(x)


def sum_all(name, x):
    def body(x_ref, o_ref):
        o_ref[...] = jnp.sum(x_ref[...], keepdims=True)

    return pl.pallas_call(body, name=name, out_shape=jax.ShapeDtypeStruct((1, 1), F32))(x)


def sum_partials(name, p, per_head):
    h, nb, r, c = p.shape
    if per_head:
        flat = jnp.transpose(p, (1, 0, 2, 3)).reshape(nb, h * r * c)
        if nb == 1:
            return flat.reshape(h, r, c)
        return colsum(name, flat).reshape(h, r, c)
    flat = p.reshape(h * nb, r * c)
    if h * nb == 1:
        return flat.reshape(r, c)
    return colsum(name, flat).reshape(r, c)


def _rms_fn(x, g):
    xf = x.astype(F32)
    return (xf * lax.rsqrt(jnp.mean(xf * xf, axis=-1, keepdims=True) + NORM_EPS) * g,)


def _softplus(z):
    return jnp.maximum(z, 0.0) + jnp.log(1.0 + jnp.exp(-jnp.abs(z)))


def _rwkv_pre_fn(hk, hw, ha, hg, w0, w2, a0, a2, g2, k_k, k_a):
    zw = w0 + jnp.dot(jnp.tanh(hw), w2, preferred_element_type=F32)
    w_log = -_softplus(-zw) - 0.5
    decay = jnp.exp(-jnp.exp(w_log))
    a = jax.nn.sigmoid(a0 + jnp.dot(ha, a2, preferred_element_type=F32))
    g = jnp.dot(jax.nn.sigmoid(hg), g2, preferred_element_type=F32)
    kk = hk * k_k
    kk = kk * lax.rsqrt(jnp.maximum(jnp.sum(kk * kk, axis=-1, keepdims=True), 1e-24))
    k = hk * (1.0 + (a - 1.0) * k_a)
    return decay, k, -kk, kk * a, g


def _rwkv_post_fn(y, r, k, v, g, gn_w, gn_b, r_k):
    mu = jnp.mean(y, axis=-1, keepdims=True)
    var = jnp.mean(jnp.square(y - mu), axis=-1, keepdims=True)
    yn = (y - mu) * lax.rsqrt(var + GN_EPS) * gn_w + gn_b
    bonus = jnp.sum(r * k * r_k, axis=-1, keepdims=True) * v
    return ((yn + bonus) * g,)


def _rope_tables(pos, inv_freq2):
    ang = pos * inv_freq2
    return jnp.cos(ang), jnp.sin(ang)


def _rope(t, cos2, sin2, rot):
    return t * cos2 + jnp.dot(t, rot, precision=HIGHEST, preferred_element_type=F32) * sin2


def _mla_pre_fn(c_q, c_kv, k_pe, pos, q_g, kv_g, inv_freq2, rot):
    cos2, sin2 = _rope_tables(pos, inv_freq2)
    return _rms_fn(c_q, q_g)[0], _rms_fn(c_kv, kv_g)[0], _rope(k_pe, cos2, sin2, rot), cos2, sin2


def _mla_pre_grad_fn(c_q, c_kv, k_pe, pos, q_g, kv_g, inv_freq2, rot):
    return _mla_pre_fn(c_q, c_kv, k_pe, pos, q_g, kv_g, inv_freq2, rot)[:3]


def _rope_q_fn(q_pe, cos2, sin2, rot):
    return (_rope(q_pe, cos2, sin2, rot),)


def _loss_fn(x2, g, target):
    y = _rms_fn(x2, g)[0]
    return (0.5 * jnp.mean(jnp.square(y - target), axis=-1, keepdims=True),)


def _adamw_fn(w, g, m, v):
    m = ADAM_B1 * m + (1.0 - ADAM_B1) * g
    v = ADAM_B2 * v + (1.0 - ADAM_B2) * jnp.square(g)
    m_hat = m / (1.0 - ADAM_B1 ** ADAM_STEP)
    v_hat = v / (1.0 - ADAM_B2 ** ADAM_STEP)
    delta = -ADAM_LR * (m_hat / (jnp.sqrt(v_hat) + ADAM_EPS) + ADAM_WD * w)
    return delta, m, v


def _prev_halo_spec(c, tm):
    return pl.BlockSpec((SUBLANES, c), lambda i: (jnp.maximum(i * (tm // SUBLANES) - 1, 0), 0))


def _next_halo_spec(c, tm, s):
    return pl.BlockSpec((SUBLANES, c), lambda i: (jnp.minimum((i + 1) * (tm // SUBLANES), s // SUBLANES - 1), 0))


def _shift_down(p, halo, first_block, n):
    out = pltpu.roll(p, n, 0)
    row = lax.broadcasted_iota(jnp.int32, p.shape, 0)
    for j in range(n):
        top = jnp.where(first_block, 0.0, halo[SUBLANES - n + j:SUBLANES - n + j + 1, :])
        out = jnp.where(row == j, top, out)
    return out


def _shift_up(p, halo, last_block, n):
    rows = p.shape[0]
    out = pltpu.roll(p, rows - n, 0)
    row = lax.broadcasted_iota(jnp.int32, p.shape, 0)
    for j in range(n):
        bot = jnp.where(last_block, 0.0, halo[j:j + 1, :])
        out = jnp.where(row == rows - n + j, bot, out)
    return out


def token_shift_fwd(p, mu, tm):
    s, c = p.shape

    def body(p_ref, halo_ref, mu_ref, o_ref):
        pv = p_ref[...]
        prev = _shift_down(pv, halo_ref[...], pl.program_id(0) == 0, 1)
        o_ref[...] = pv + (prev - pv) * mu_ref[...]

    return pl.pallas_call(
        body, name="token_shift_fwd", grid=(s // tm,),
        in_specs=[pl.BlockSpec((tm, c), lambda i: (i, 0)), _prev_halo_spec(c, tm), pl.BlockSpec((1, c), lambda i: (0, 0))],
        out_specs=pl.BlockSpec((tm, c), lambda i: (i, 0)), out_shape=jax.ShapeDtypeStruct((s, c), F32),
        compiler_params=_params(("parallel",)),
    )(p, p, mu)


def token_shift_bwd(p, mu, ds, tm):
    s, c = p.shape
    nb = s // tm

    def body(p_ref, halo_ref, mu_ref, ds_ref, dsn_ref, dp_ref, dmu_ref):
        i = pl.program_id(0)
        pv, dsv, muv = p_ref[...], ds_ref[...], mu_ref[...]
        prev = _shift_down(pv, halo_ref[...], i == 0, 1)
        nxt = _shift_up(dsv, dsn_ref[...], i == nb - 1, 1)
        dp_ref[...] = dsv * (1.0 - muv) + nxt * muv
        dmu_ref[...] = jnp.sum(dsv * (prev - pv), axis=0, keepdims=True)

    return pl.pallas_call(
        body, name="token_shift_bwd", grid=(nb,),
        in_specs=[pl.BlockSpec((tm, c), lambda i: (i, 0)), _prev_halo_spec(c, tm), pl.BlockSpec((1, c), lambda i: (0, 0)),
                  pl.BlockSpec((tm, c), lambda i: (i, 0)), _next_halo_spec(c, tm, s)],
        out_specs=[pl.BlockSpec((tm, c), lambda i: (i, 0)), pl.BlockSpec((None, 1, c), lambda i: (i, 0, 0))],
        out_shape=[jax.ShapeDtypeStruct((s, c), F32), jax.ShapeDtypeStruct((nb, 1, c), F32)],
        compiler_params=_params(("parallel",)),
    )(p, p, mu, ds, ds)


def _ffn_tiles(s, f):
    return _pick(s, 256, SUBLANES), _pick(f, 1408, LANES)


def _conv_gate(gp, halo, first_block, cw, cb):
    p1 = _shift_down(gp, halo, first_block, 1)
    p2 = _shift_down(gp, halo, first_block, 2)
    return cw[0:1, :] * p2 + cw[1:2, :] * p1 + cw[2:3, :] * gp + cb, p1, p2


def ffn_act_fwd(gate_pre, up, conv_w, conv_b):
    s, f = gate_pre.shape
    tm, tc = _ffn_tiles(s, f)

    def body(gp_ref, halo_ref, up_ref, cw_ref, cb_ref, o_ref):
        gate, _, _ = _conv_gate(gp_ref[...], halo_ref[...], pl.program_id(0) == 0, cw_ref[...], cb_ref[...])
        o_ref[...] = (gate * jax.nn.sigmoid(gate) * up_ref[...]).astype(o_ref.dtype)

    blk = pl.BlockSpec((tm, tc), lambda i, j: (i, j))
    return pl.pallas_call(
        body, name="ffn_act_fwd", grid=(s // tm, f // tc),
        in_specs=[blk, pl.BlockSpec((SUBLANES, tc), lambda i, j: (jnp.maximum(i * (tm // SUBLANES) - 1, 0), j)), blk,
                  pl.BlockSpec((CONV_W, tc), lambda i, j: (0, j)), pl.BlockSpec((1, tc), lambda i, j: (0, j))],
        out_specs=blk, out_shape=jax.ShapeDtypeStruct((s, f), BF16),
        compiler_params=_params(("parallel", "parallel")),
    )(gate_pre, gate_pre, up, conv_w, conv_b)


def ffn_act_bwd1(gate_pre, up, conv_w, conv_b, d_act):
    s, f = gate_pre.shape
    tm, tc = _ffn_tiles(s, f)
    nb = s // tm

    def body(gp_ref, halo_ref, up_ref, cw_ref, cb_ref, da_ref, dg_ref, du_ref, dcw_ref, dcb_ref):
        gp = gp_ref[...]
        gate, p1, p2 = _conv_gate(gp, halo_ref[...], pl.program_id(0) == 0, cw_ref[...], cb_ref[...])
        sig = jax.nn.sigmoid(gate)
        da = da_ref[...].astype(F32)
        du_ref[...] = (da * gate * sig).astype(du_ref.dtype)
        dg = da * up_ref[...] * (sig * (1.0 + gate * (1.0 - sig)))
        dg_ref[...] = dg
        dcb_ref[...] = jnp.sum(dg, axis=0, keepdims=True)
        dcw_ref[0:1, :] = jnp.sum(dg * p2, axis=0, keepdims=True)
        dcw_ref[1:2, :] = jnp.sum(dg * p1, axis=0, keepdims=True)
        dcw_ref[2:3, :] = jnp.sum(dg * gp, axis=0, keepdims=True)

    blk = pl.BlockSpec((tm, tc), lambda i, j: (i, j))
    return pl.pallas_call(
        body, name="ffn_act_bwd1", grid=(nb, f // tc),
        in_specs=[blk, pl.BlockSpec((SUBLANES, tc), lambda i, j: (jnp.maximum(i * (tm // SUBLANES) - 1, 0), j)), blk,
                  pl.BlockSpec((CONV_W, tc), lambda i, j: (0, j)), pl.BlockSpec((1, tc), lambda i, j: (0, j)), blk],
        out_specs=[blk, blk, pl.BlockSpec((None, CONV_W, tc), lambda i, j: (i, 0, j)),
                   pl.BlockSpec((None, 1, tc), lambda i, j: (i, 0, j))],
        out_shape=[jax.ShapeDtypeStruct((s, f), F32), jax.ShapeDtypeStruct((s, f), BF16),
                   jax.ShapeDtypeStruct((nb, CONV_W, f), F32), jax.ShapeDtypeStruct((nb, 1, f), F32)],
        compiler_params=_params(("parallel", "parallel")),
    )(gate_pre, gate_pre, up, conv_w, conv_b, d_act)


def ffn_act_bwd2(d_gate, conv_w):
    s, f = d_gate.shape
    tm, tc = _ffn_tiles(s, f)
    nb = s // tm

    def body(dg_ref, halo_ref, cw_ref, o_ref):
        dg, cw = dg_ref[...], cw_ref[...]
        last = pl.program_id(0) == nb - 1
        n1 = _shift_up(dg, halo_ref[...], last, 1)
        n2 = _shift_up(dg, halo_ref[...], last, 2)
        o_ref[...] = (cw[2:3, :] * dg + cw[1:2, :] * n1 + cw[0:1, :] * n2).astype(o_ref.dtype)

    blk = pl.BlockSpec((tm, tc), lambda i, j: (i, j))
    return pl.pallas_call(
        body, name="ffn_act_bwd2", grid=(nb, f // tc),
        in_specs=[blk, pl.BlockSpec((SUBLANES, tc), lambda i, j: (jnp.minimum((i + 1) * (tm // SUBLANES), s // SUBLANES - 1), j)),
                  pl.BlockSpec((CONV_W, tc), lambda i, j: (0, j))],
        out_specs=blk, out_shape=jax.ShapeDtypeStruct((s, f), BF16),
        compiler_params=_params(("parallel", "parallel")),
    )(d_gate, d_gate, conv_w)


def _dot(x, y):
    return jnp.dot(x, y, precision=HIGHEST, preferred_element_type=F32)


def _dot_nt(x, y):
    return lax.dot_general(x, y, (((1,), (1,)), ((), ())), precision=HIGHEST, preferred_element_type=F32)


def _dot_tn(x, y):
    return lax.dot_general(x, y, (((0,), (0,)), ((), ())), precision=HIGHEST, preferred_element_type=F32)


def _scan_chunk(s0, r, w, k, v, a, b):
    t = r.shape[0]
    row = lax.broadcasted_iota(jnp.int32, (t, t), 0)
    col = lax.broadcasted_iota(jnp.int32, (t, t), 1)
    strict, incl = col < row, col <= row
    logw = jnp.log(w)
    cum = _dot(jnp.where(incl, 1.0, 0.0), logw)
    w_in, w_ex, w_inv = jnp.exp(cum), jnp.exp(cum - logw), jnp.exp(-cum)
    w_all = jnp.exp(jnp.sum(logw, axis=0, keepdims=True))
    at, rt, kt, bt = a * w_ex, r * w_in, k * w_inv, b * w_inv
    a_ab = jnp.where(strict, _dot_nt(at, bt), 0.0)
    a_ak = jnp.where(strict, _dot_nt(at, kt), 0.0)
    a_rk = jnp.where(incl, _dot_nt(rt, kt), 0.0)
    a_rb = jnp.where(incl, _dot_nt(rt, bt), 0.0)
    u = _dot_nt(at, s0) + _dot(a_ak, v)
    p = a_ab
    steps = int(math.log2(t))
    assert 2 ** steps == t
    for j in range(steps):
        u = u + _dot(p, u)
        if j < steps - 1:
            p = _dot(p, p)
    y = _dot_nt(rt, s0) + _dot(a_rk, v) + _dot(a_rb, u)
    s_new = s0 * w_all + _dot_tn(v, kt * w_all) + _dot_tn(u, bt * w_all)
    return y, s_new


def scan_fwd(r, w, k, v, a, b):
    h, s, n = r.shape
    t = min(SCAN_CHUNK, s)
    nc = s // t

    def body(r_ref, w_ref, k_ref, v_ref, a_ref, b_ref, y_ref, ck_ref, st_ref):
        @pl.when(pl.program_id(1) == 0)
        def _():
            st_ref[...] = jnp.zeros_like(st_ref)

        s0 = st_ref[...]
        ck_ref[...] = s0
        y, s_new = _scan_chunk(s0, r_ref[...], w_ref[...], k_ref[...], v_ref[...], a_ref[...], b_ref[...])
        y_ref[...] = y
        st_ref[...] = s_new

    blk = pl.BlockSpec((None, t, n), lambda hh, c: (hh, c, 0))
    return pl.pallas_call(
        body, name="rwkv_scan_fwd", grid=(h, nc), in_specs=[blk] * 6,
        out_specs=[blk, pl.BlockSpec((None, None, n, n), lambda hh, c: (hh, c, 0, 0))],
        out_shape=[jax.ShapeDtypeStruct((h, s, n), F32), jax.ShapeDtypeStruct((h, nc, n, n), F32)],
        scratch_shapes=[pltpu.VMEM((n, n), F32)],
        compiler_params=_params(("parallel", "arbitrary")),
    )(r, w, k, v, a, b)


def scan_bwd(r, w, k, v, a, b, ck, dy):
    h, s, n = r.shape
    t = min(SCAN_CHUNK, s)
    nc = s // t

    def body(r_ref, w_ref, k_ref, v_ref, a_ref, b_ref, ck_ref, dy_ref, dr_ref, dw_ref, dk_ref, dv_ref, da_ref, db_ref, ds_ref):
        @pl.when(pl.program_id(1) == 0)
        def _():
            ds_ref[...] = jnp.zeros_like(ds_ref)

        _, vjp_fn = jax.vjp(_scan_chunk, ck_ref[...], r_ref[...], w_ref[...], k_ref[...], v_ref[...], a_ref[...], b_ref[...])
        ds0, dr, dw, dk, dv, da, db = vjp_fn((dy_ref[...], ds_ref[...]))
        ds_ref[...] = ds0
        dr_ref[...], dw_ref[...], dk_ref[...], dv_ref[...], da_ref[...], db_ref[...] = dr, dw, dk, dv, da, db

    blk = pl.BlockSpec((None, t, n), lambda hh, c: (hh, nc - 1 - c, 0))
    return pl.pallas_call(
        body, name="rwkv_scan_bwd", grid=(h, nc),
        in_specs=[blk] * 6 + [pl.BlockSpec((None, None, n, n), lambda hh, c: (hh, nc - 1 - c, 0, 0)), blk],
        out_specs=[blk] * 6, out_shape=[jax.ShapeDtypeStruct((h, s, n), F32)] * 6,
        scratch_shapes=[pltpu.VMEM((n, n), F32)],
        compiler_params=_params(("parallel", "arbitrary")),
    )(r, w, k, v, a, b, ck, dy)


def _attn_scores(qn, qp, kn, kp, q0):
    scale = (QK_NOPE + QK_ROPE) ** -0.5
    sc = lax.dot_general(qn.astype(BF16), kn.astype(BF16), (((1,), (1,)), ((), ())), preferred_element_type=F32)
    sc = sc + lax.dot_general(qp.astype(BF16), kp.astype(BF16), (((1,), (1,)), ((), ())), preferred_element_type=F32)
    row = q0 + lax.broadcasted_iota(jnp.int32, sc.shape, 0)
    col = lax.broadcasted_iota(jnp.int32, sc.shape, 1)
    return jnp.where(row >= col, sc * scale, NEG_INF), scale


def attn_fwd(qn, qp, kn, kp, v):
    h, s, _ = qn.shape
    tq = _pick(s, 256, SUBLANES)

    def body(qn_ref, qp_ref, kn_ref, kp_ref, v_ref, o_ref, lse_ref):
        sc, _ = _attn_scores(qn_ref[...], qp_ref[...], kn_ref[...], kp_ref[...], pl.program_id(1) * tq)
        mx = jnp.max(sc, axis=-1, keepdims=True)
        e = jnp.exp(sc - mx)
        den = jnp.sum(e, axis=-1, keepdims=True)
        p = e / den
        o_ref[...] = jnp.dot(p.astype(BF16), v_ref[...].astype(BF16), preferred_element_type=F32)
        lse_ref[...] = mx + jnp.log(den)

    qblk = lambda c: pl.BlockSpec((None, tq, c), lambda hh, i: (hh, i, 0))
    kblk = lambda c: pl.BlockSpec((None, s, c), lambda hh, i: (hh, 0, 0))
    return pl.pallas_call(
        body, name="mla_attn_fwd", grid=(h, s // tq),
        in_specs=[qblk(QK_NOPE), qblk(QK_ROPE), kblk(QK_NOPE), pl.BlockSpec((s, QK_ROPE), lambda hh, i: (0, 0)), kblk(V_HEAD)],
        out_specs=[qblk(V_HEAD), qblk(1)],
        out_shape=[jax.ShapeDtypeStruct((h, s, V_HEAD), F32), jax.ShapeDtypeStruct((h, s, 1), F32)],
        compiler_params=_params(("parallel", "parallel")),
    )(qn, qp, kn, kp, v)


def attn_bwd(qn, qp, kn, kp, v, o, lse, do):
    h, s, _ = qn.shape
    tq = _pick(s, 256, SUBLANES)

    def body(qn_ref, qp_ref, kn_ref, kp_ref, v_ref, o_ref, lse_ref, do_ref, dqn_ref, dqp_ref, dkn_ref, dv_ref, dkp_ref):
        @pl.when(pl.program_id(1) == 0)
        def _():
            dkn_ref[...] = jnp.zeros_like(dkn_ref)
            dv_ref[...] = jnp.zeros_like(dv_ref)
            dkp_ref[...] = jnp.zeros_like(dkp_ref)

        qn_b, qp_b = qn_ref[...].astype(BF16), qp_ref[...].astype(BF16)
        kn_b, kp_b, v_b = kn_ref[...].astype(BF16), kp_ref[...].astype(BF16), v_ref[...].astype(BF16)
        sc, scale = _attn_scores(qn_b, qp_b, kn_b, kp_b, pl.program_id(1) * tq)
        p = jnp.exp(sc - lse_ref[...])
        dov = do_ref[...]
        do_b = dov.astype(BF16)
        p_b = p.astype(BF16)
        dv_ref[...] += lax.dot_general(p_b, do_b, (((0,), (0,)), ((), ())), preferred_element_type=F32)
        dp = lax.dot_general(do_b, v_b, (((1,), (1,)), ((), ())), preferred_element_type=F32)
        delta = jnp.sum(dov * o_ref[...], axis=-1, keepdims=True)
        ds = (p * (dp - delta) * scale).astype(BF16)
        dqn_ref[...] = jnp.dot(ds, kn_b, preferred_element_type=F32)
        dqp_ref[...] = jnp.dot(ds, kp_b, preferred_element_type=F32)
        dkn_ref[...] += lax.dot_general(ds, qn_b, (((0,), (0,)), ((), ())), preferred_element_type=F32)
        dkp_ref[...] += lax.dot_general(ds, qp_b, (((0,), (0,)), ((), ())), preferred_element_type=F32)

    qblk = lambda c: pl.BlockSpec((None, tq, c), lambda hh, i: (hh, i, 0))
    kblk = lambda c: pl.BlockSpec((None, s, c), lambda hh, i: (hh, 0, 0))
    return pl.pallas_call(
        body, name="mla_attn_bwd", grid=(h, s // tq),
        in_specs=[qblk(QK_NOPE), qblk(QK_ROPE), kblk(QK_NOPE), pl.BlockSpec((s, QK_ROPE), lambda hh, i: (0, 0)), kblk(V_HEAD),
                  qblk(V_HEAD), qblk(1), qblk(V_HEAD)],
        out_specs=[qblk(QK_NOPE), qblk(QK_ROPE), kblk(QK_NOPE), kblk(V_HEAD), kblk(QK_ROPE)],
        out_shape=[jax.ShapeDtypeStruct((h, s, QK_NOPE), F32), jax.ShapeDtypeStruct((h, s, QK_ROPE), F32),
                   jax.ShapeDtypeStruct((h, s, QK_NOPE), F32), jax.ShapeDtypeStruct((h, s, V_HEAD), F32),
                   jax.ShapeDtypeStruct((h, s, QK_ROPE), F32)],
        compiler_params=_params(("parallel", "arbitrary")),
    )(qn, qp, kn, kp, v, o, lse, do)


def _my_pos():
    return lax.axis_index("x"), lax.axis_index("y"), lax.axis_index("c")


def _dev_index(px, py, pc):
    return 4 * px + 2 * py + pc


def all_gather(name, shard):
    r, c = shard.shape

    def body(x_ref, out_ref, send_sems, recv_sems, local_sem):
        x, y, cc = _my_pos()
        me, sibling = (x, y, cc), (x, y, 1 - cc)
        chips = [(1 - x, y), (x, 1 - y), (1 - x, 1 - y)]

        def rows(px, py, pc):
            return out_ref.at[_dev_index(px, py, pc)]

        def copy(kk, block, to, src=None):
            return pltpu.make_async_remote_copy(
                src_ref=rows(*block) if src is None else src, dst_ref=rows(*block),
                send_sem=send_sems.at[kk], recv_sem=recv_sems.at[kk], device_id=to, device_id_type=MESH)

        mine = pltpu.make_async_copy(x_ref, rows(*me), local_sem)
        mine.start()
        first = [copy(0, me, sibling, src=x_ref)]
        first += [copy(1 + j, me, (*chip, cc), src=x_ref) for j, chip in enumerate(chips)]
        for cp in first:
            cp.start()
        passed = [copy(4 + j, (*chip, cc), sibling) for j, chip in enumerate(chips)]
        for j, chip in enumerate(chips):
            copy(1 + j, (*chip, cc), me).wait_recv()
            passed[j].start()
        copy(0, sibling, me).wait_recv()
        for j, chip in enumerate(chips):
            copy(4 + j, (*chip, 1 - cc), me).wait_recv()
        for cp in first + passed:
            cp.wait_send()
        mine.wait()

    return pl.pallas_call(
        body, name=name, out_shape=jax.ShapeDtypeStruct((N_DEV, r, c), shard.dtype),
        in_specs=[pl.BlockSpec(memory_space=pl.ANY)], out_specs=pl.BlockSpec(memory_space=pl.ANY),
        scratch_shapes=[pltpu.SemaphoreType.DMA((7,)), pltpu.SemaphoreType.DMA((7,)), pltpu.SemaphoreType.DMA],
    )(shard)


def _flip(kind):
    x, y, c = _my_pos()
    return {'c': (x, y, 1 - c), 'x': (1 - x, y, c), 'y': (x, 1 - y, c), 'xy': (1 - x, 1 - y, c)}[kind]


def exchange_sibling(name, g):
    _, r, c = g.shape

    def body(g_ref, out_ref, send_sems, recv_sems):
        x, y, cc = _my_pos()
        copies = []
        for px in range(2):
            for py in range(2):
                slot = 2 * px + py
                copies.append(pltpu.make_async_remote_copy(
                    src_ref=g_ref.at[_dev_index(px, py, 1 - cc)], dst_ref=out_ref.at[slot],
                    send_sem=send_sems.at[slot], recv_sem=recv_sems.at[slot], device_id=(x, y, 1 - cc), device_id_type=MESH))
        for cp in copies:
            cp.start()
        for cp in copies:
            cp.wait()

    return pl.pallas_call(
        body, name=name, out_shape=jax.ShapeDtypeStruct((4, r, c), g.dtype),
        in_specs=[pl.BlockSpec(memory_space=pl.ANY)], out_specs=pl.BlockSpec(memory_space=pl.ANY),
        scratch_shapes=[pltpu.SemaphoreType.DMA((4,)), pltpu.SemaphoreType.DMA((4,))],
    )(g)


def exchange_chips(name, hsum):
    _, r, c = hsum.shape

    def body(h_ref, out_ref, send_sems, recv_sems):
        x, y, cc = _my_pos()
        copies = []
        for j, (px, py) in enumerate([(1 - x, y), (x, 1 - y), (1 - x, 1 - y)]):
            copies.append(pltpu.make_async_remote_copy(
                src_ref=h_ref.at[2 * px + py], dst_ref=out_ref.at[j],
                send_sem=send_sems.at[j], recv_sem=recv_sems.at[j], device_id=(px, py, cc), device_id_type=MESH))
        for cp in copies:
            cp.start()
        for cp in copies:
            cp.wait()

    return pl.pallas_call(
        body, name=name, out_shape=jax.ShapeDtypeStruct((3, r, c), hsum.dtype),
        in_specs=[pl.BlockSpec(memory_space=pl.ANY)], out_specs=pl.BlockSpec(memory_space=pl.ANY),
        scratch_shapes=[pltpu.SemaphoreType.DMA((3,)), pltpu.SemaphoreType.DMA((3,))],
    )(hsum)


def add_slots(name, *terms):
    n, r, c = terms[0].shape
    tr = _pick(r, 512, SUBLANES)

    def body(*refs):
        acc = refs[0][...]
        for t in refs[1:-1]:
            acc = acc + t[...]
        refs[-1][...] = acc

    blk = pl.BlockSpec((None, tr, c), lambda s_, i: (s_, i, 0))
    return pl.pallas_call(
        body, name=name, grid=(n, r // tr), in_specs=[blk] * len(terms), out_specs=blk,
        out_shape=jax.ShapeDtypeStruct((n, r, c), F32), compiler_params=_params(("parallel", "parallel")),
    )(*terms)


def _rs_add_sibling(name, g, from_sibling, cc):
    _, r, c = g.shape
    tr = _pick(r, 512, SUBLANES)

    def body(cc_ref, g_ref, s_ref, o_ref):
        o_ref[...] = g_ref[...] + s_ref[...]

    blk = pl.BlockSpec((None, tr, c), lambda s_, i, cc_ref: (s_, i, 0))
    return pl.pallas_call(
        body, name=name,
        grid_spec=pltpu.PrefetchScalarGridSpec(
            num_scalar_prefetch=1, grid=(4, r // tr),
            in_specs=[pl.BlockSpec((None, None, tr, c), lambda s_, i, cc_ref: (s_, cc_ref[0], i, 0)), blk], out_specs=blk),
        out_shape=jax.ShapeDtypeStruct((4, r, c), F32), compiler_params=_params(("parallel", "parallel")),
    )(cc.reshape(1).astype(jnp.int32), g.reshape(4, 2, r, c), from_sibling)


def _rs_add_chips(name, chip_sum, from_chips, slot):
    _, r, c = chip_sum.shape
    tr = _pick(r, 512, SUBLANES)

    def body(slot_ref, h_ref, f0_ref, f1_ref, f2_ref, o_ref):
        o_ref[...] = ((h_ref[...] + f0_ref[...]) + f1_ref[...]) + f2_ref[...]

    def from_blk(j):
        return pl.BlockSpec((None, tr, c), lambda i, slot_ref: (j, i, 0))

    return pl.pallas_call(
        body, name=name,
        grid_spec=pltpu.PrefetchScalarGridSpec(
            num_scalar_prefetch=1, grid=(r // tr,),
            in_specs=[pl.BlockSpec((None, tr, c), lambda i, slot_ref: (slot_ref[0], i, 0)), from_blk(0), from_blk(1), from_blk(2)],
            out_specs=pl.BlockSpec((tr, c), lambda i, slot_ref: (i, 0))),
        out_shape=jax.ShapeDtypeStruct((r, c), F32), compiler_params=_params(("parallel",)),
    )(slot.reshape(1).astype(jnp.int32), chip_sum, from_chips, from_chips, from_chips)


def reduce_scatter(tag, g):
    x, y, cc = _my_pos()
    from_sibling = exchange_sibling("rs_sibling_" + tag, g)
    chip_sum = _rs_add_sibling("rs_add_sibling_" + tag, g, from_sibling, cc)
    from_chips = exchange_chips("rs_chips_" + tag, chip_sum)
    return _rs_add_chips("rs_add_chips_" + tag, chip_sum, from_chips, 2 * x + y)


PACK_W = 1024


class Pack:
    def __init__(self, entries, row_unit):
        self.entries = entries
        self.sizes = [int(np.prod(sh)) for _, sh in entries]
        self.offsets = np.concatenate([[0], np.cumsum(self.sizes)]).tolist()
        self.total = _round_up(self.offsets[-1], PACK_W * row_unit)
        self.rows = self.total // PACK_W

    def pack(self, arrays, dtype, lead=()):
        flat = [arrays[n].astype(dtype).reshape(lead + (-1,)) for n, _ in self.entries]
        pad = self.total - self.offsets[-1]
        if pad:
            flat.append(jnp.zeros(lead + (pad,), dtype))
        return jnp.concatenate(flat, axis=-1).reshape(lead + (self.rows, PACK_W))

    def unpack(self, buf, lead=()):
        flat = buf.reshape(lead + (self.total,))
        out = {}
        for (n, sh), off, sz in zip(self.entries, self.offsets, self.sizes):
            out[n] = lax.slice_in_dim(flat, off, off + sz, axis=len(lead)).reshape(lead + tuple(sh))
        return out


def _gathered_to_full(g, how):
    _, a, b = g.shape
    if how == 'row':
        return g.reshape(N_DEV * a, b)
    return jnp.transpose(g, (1, 0, 2)).reshape(a, N_DEV * b)


def _full_to_shards(w, how):
    a, b = w.shape
    if how == 'row':
        return w.reshape(N_DEV, a // N_DEV, b)
    return jnp.transpose(w.reshape(a, N_DEV, b // N_DEV), (1, 0, 2))


def _to_heads(t, width):
    s, c = t.shape
    return jnp.transpose(t.reshape(s, c // width, width), (1, 0, 2))


def _from_heads(t):
    h, s, w = t.shape
    return jnp.transpose(t, (1, 0, 2)).reshape(s, h * w)


def _rot_matrix():
    half = QK_ROPE // 2
    rot = np.zeros((QK_ROPE, QK_ROPE), np.float32)
    for i in range(half):
        rot[i + half, i] = -1.0
        rot[i, i + half] = 1.0
    return jnp.asarray(rot)


def _inv_freq2():
    half = QK_ROPE // 2
    inv = ROPE_THETA ** (-np.arange(half, dtype=np.float32) / half)
    return jnp.asarray(np.concatenate([inv, inv])[None, :].astype(np.float32))


def kernel(x, positions, attn_norm_g, w_in, rwkv_mu, rwkv_w0, rwkv_w2, rwkv_a0, rwkv_a2, rwkv_g2, rwkv_k_k, rwkv_k_a, rwkv_r_k, rwkv_gn_w, rwkv_gn_b, mla_q_norm_g, mla_w_uq, mla_kv_norm_g, mla_w_ukv, w_out, ffn_norm_g, ffn_w_gate, ffn_w_up, ffn_conv_w, ffn_conv_b, ffn_w_down, final_norm_g, loss_target, m_attn_norm_g, m_w_in, m_rwkv_mu, m_rwkv_w0, m_rwkv_w2, m_rwkv_a0, m_rwkv_a2, m_rwkv_g2, m_rwkv_k_k, m_rwkv_k_a, m_rwkv_r_k, m_rwkv_gn_w, m_rwkv_gn_b, m_mla_q_norm_g, m_mla_w_uq, m_mla_kv_norm_g, m_mla_w_ukv, m_w_out, m_ffn_norm_g, m_ffn_w_gate, m_ffn_w_up, m_ffn_conv_w, m_ffn_conv_b, m_ffn_w_down, m_final_norm_g, v_attn_norm_g, v_w_in, v_rwkv_mu, v_rwkv_w0, v_rwkv_w2, v_rwkv_a0, v_rwkv_a2, v_rwkv_g2, v_rwkv_k_k, v_rwkv_k_a, v_rwkv_r_k, v_rwkv_gn_w, v_rwkv_gn_b, v_mla_q_norm_g, v_mla_w_uq, v_mla_kv_norm_g, v_mla_w_ukv, v_w_out, v_ffn_norm_g, v_ffn_w_gate, v_ffn_w_up, v_ffn_conv_w, v_ffn_conv_b, v_ffn_w_down, v_final_norm_g):
    given = dict(locals())
    wts = {n: given[n] for n in WEIGHTS}
    mom_m = {n: given["m_" + n] for n in WEIGHTS}
    mom_v = {n: given["v_" + n] for n in WEIGHTS}
    out_shapes = {n: wts[n].shape for n in WEIGHTS}

    def local2d(n, a):
        if n == 'rwkv_r_k' or a.ndim <= 2:
            return a.reshape(1, -1)
        return a.reshape(a.shape[1:])

    w2d = {n: local2d(n, wts[n]) for n in WEIGHTS}
    m2d = {n: local2d(n, mom_m[n]) for n in WEIGHTS}
    v2d = {n: local2d(n, mom_v[n]) for n in WEIGHTS}

    xs = x.reshape(x.shape[1:])
    tgt = loss_target.reshape(loss_target.shape[1:])
    s, d = xs.shape
    c_rwkv = w2d['rwkv_w0'].shape[1]
    n_rh = c_rwkv // RWKV_HEAD
    decay_lora, aaa_lora, gate_lora = w2d['rwkv_w2'].shape[0], w2d['rwkv_a2'].shape[0], w2d['rwkv_g2'].shape[0]
    q_lora, kv_lora = w2d['mla_q_norm_g'].shape[1], w2d['mla_kv_norm_g'].shape[1]
    shift_dim = w2d['rwkv_mu'].shape[1]
    d_in = w2d['w_in'].shape[1] * N_DEV
    d_in_pad = _round_up(d_in, LANES)
    n_mh = w2d['mla_w_uq'].shape[1] * N_DEV // (QK_NOPE + QK_ROPE)
    d_ff = w2d['ffn_conv_b'].shape[1]
    tm = _pick(s, 256, SUBLANES)
    tm_wide = _pick(s, 128, SUBLANES)

    nb = {n: w2d[n].shape[1] for n in BIG if BIG[n] == 'col'}
    nbp = {n: _round_up(v_, LANES) for n, v_ in nb.items()}
    gathered = {}
    for n in BIG:
        w = w2d[n].astype(BF16)
        if BIG[n] == 'col':
            w = jnp.pad(w, ((0, 0), (0, nbp[n] - nb[n])))
        elif n == 'ffn_w_down':
            w = jnp.pad(w, ((0, nbp['ffn_w_gate'] - w.shape[0]), (0, 0)))
        g = all_gather("gather_" + n, w)
        gathered[n] = g if BIG[n] == 'col' else g.reshape(N_DEV * g.shape[1], g.shape[2])
    f_pad = N_DEV * nbp['ffn_w_gate']
    small_pack = Pack([(n, w2d[n].shape) for n in SMALL_SHARDED], 8)
    small_all = all_gather("gather_small", small_pack.pack(w2d, F32))
    full = {}
    for n, g in small_pack.unpack(small_all, lead=(N_DEV,)).items():
        full[n] = _gathered_to_full(g, SMALL_SHARDED[n])
    conv_w_pad = pad_cols(full['ffn_conv_w'], nb['ffn_w_gate'], nbp['ffn_w_gate'])
    conv_b_pad = pad_cols(w2d['ffn_conv_b'], nb['ffn_w_gate'], nbp['ffn_w_gate'])

    (h1,) = rowwise("rms_attn", _rms_fn, [xs, w2d['attn_norm_g']], ['row', 'const'], [('row', d, BF16)], heads=1, s=s, tm=tm)
    proj = unpad_cols(mm_sh("proj_in", h1, gathered['w_in']), nb['w_in'], nbp['w_in'])
    p_rwkv = proj[:, :shift_dim]
    c_q = proj[:, shift_dim:shift_dim + q_lora]
    c_kv = proj[:, shift_dim + q_lora:shift_dim + q_lora + kv_lora]
    k_pe = proj[:, shift_dim + q_lora + kv_lora:d_in]
    shifted = token_shift_fwd(p_rwkv, w2d['rwkv_mu'], tm_wide)
    o1, o2, o3 = c_rwkv, 2 * c_rwkv, 3 * c_rwkv
    hr = _to_heads(shifted[:, :o1], RWKV_HEAD)
    hk = _to_heads(shifted[:, o1:o2], RWKV_HEAD)
    hv = _to_heads(shifted[:, o2:o3], RWKV_HEAD)
    hw = shifted[:, o3:o3 + decay_lora]
    ha = shifted[:, o3 + decay_lora:o3 + decay_lora + aaa_lora]
    hg = shifted[:, o3 + decay_lora + aaa_lora:]

    def per_head(vec):
        return vec.reshape(n_rh, 1, RWKV_HEAD)

    def lora_heads(w):
        return jnp.transpose(w.reshape(w.shape[0], n_rh, RWKV_HEAD), (1, 0, 2))

    pre_args = [hk, hw, ha, hg, per_head(w2d['rwkv_w0']), lora_heads(full['rwkv_w2']), per_head(w2d['rwkv_a0']),
                lora_heads(full['rwkv_a2']), lora_heads(full['rwkv_g2']), per_head(w2d['rwkv_k_k']), per_head(w2d['rwkv_k_a'])]
    pre_kinds = ['hrow', 'row', 'row', 'row', 'hconst', 'hconst', 'hconst', 'hconst', 'hconst', 'hconst', 'hconst']
    decay, kx, a_sc, b_sc, gate_r = rowwise("rwkv_pre", _rwkv_pre_fn, pre_args, pre_kinds,
                                            [('hrow', RWKV_HEAD, F32)] * 5, heads=n_rh, s=s, tm=tm)
    y_scan, ckpt = scan_fwd(hr, decay, kx, hv, a_sc, b_sc)
    post_args = [y_scan, hr, kx, hv, gate_r, per_head(w2d['rwkv_gn_w']), per_head(w2d['rwkv_gn_b']), per_head(w2d['rwkv_r_k'])]
    post_kinds = ['hrow'] * 5 + ['hconst'] * 3
    (y_rwkv_h,) = rowwise("rwkv_post", _rwkv_post_fn, post_args, post_kinds, [('hrow', RWKV_HEAD, F32)], heads=n_rh, s=s, tm=tm)

    pos = positions.reshape(s, 1).astype(F32)
    rot, inv2 = _rot_matrix(), _inv_freq2()
    mla_args = [c_q, c_kv, k_pe, pos, w2d['mla_q_norm_g'], w2d['mla_kv_norm_g'], inv2, rot]
    mla_kinds = ['row', 'row', 'row', 'row', 'const', 'const', 'const', 'const']
    qn, kvn, kp_rot, cos2, sin2 = rowwise(
        "mla_pre", _mla_pre_fn, mla_args, mla_kinds,
        [('row', q_lora, BF16), ('row', kv_lora, BF16), ('row', QK_ROPE, F32), ('row', QK_ROPE, F32), ('row', QK_ROPE, F32)],
        heads=1, s=s, tm=tm)
    q = unpad_cols(mm_sh("proj_q", qn, gathered['mla_w_uq']), nb['mla_w_uq'], nbp['mla_w_uq'])
    kv = unpad_cols(mm_sh("proj_kv", kvn, gathered['mla_w_ukv']), nb['mla_w_ukv'], nbp['mla_w_ukv'])
    q_h = _to_heads(q, QK_NOPE + QK_ROPE)
    kv_h = _to_heads(kv, QK_NOPE + V_HEAD)
    q_nope, q_pe = q_h[..., :QK_NOPE], q_h[..., QK_NOPE:]
    k_nope, v_att = kv_h[..., :QK_NOPE], kv_h[..., QK_NOPE:]
    ropeq_args = [q_pe, cos2, sin2, rot]
    ropeq_kinds = ['hrow', 'row', 'row', 'const']
    (q_pe_rot,) = rowwise("rope_q", _rope_q_fn, ropeq_args, ropeq_kinds, [('hrow', QK_ROPE, F32)], heads=n_mh, s=s, tm=tm)
    o_att, lse = attn_fwd(q_nope, q_pe_rot, k_nope, kp_rot, v_att)
    ycat = jnp.concatenate([_from_heads(y_rwkv_h), _from_heads(o_att)], axis=-1).astype(BF16)
    x1 = mm("proj_out", ycat, gathered['w_out'], add=xs)
    (h2,) = rowwise("rms_ffn", _rms_fn, [x1, w2d['ffn_norm_g']], ['row', 'const'], [('row', d, BF16)], heads=1, s=s, tm=tm)
    gate_pre = mm_sh("ffn_gate", h2, gathered['ffn_w_gate'])
    up = mm_sh("ffn_up", h2, gathered['ffn_w_up'])
    act = ffn_act_fwd(gate_pre, up, conv_w_pad, conv_b_pad)
    x2 = mm("ffn_down", act, gathered['ffn_w_down'], add=x1)

    ones = jnp.ones((s, 1), F32)
    fin_g = w2d['final_norm_g']
    d_x2, dg_final_p, loss_rows = rowwise_vjp("loss_bwd", _loss_fn, [x2, fin_g, tgt], ['row', 'const', 'row'], [ones], ['row'],
                                              [0, 1], heads=1, s=s, tm=tm, primal=True)
    d_x2_b = d_x2.astype(BF16)
    d_act = mm_nt("d_act", d_x2_b, gathered['ffn_w_down'], out_dtype=BF16)
    gsh = {}
    gsh['ffn_w_down'] = mm("dw_down", act.T, d_x2_b).reshape(N_DEV, nbp['ffn_w_gate'], d)
    d_gate, d_up, dcw_p, dcb_p = ffn_act_bwd1(gate_pre, up, conv_w_pad, conv_b_pad, d_act)
    d_gp = ffn_act_bwd2(d_gate, conv_w_pad)
    d_h2 = mm_sh_nt("d_h2_up", d_up, gathered['ffn_w_up'], add=mm_sh_nt("d_h2_gate", d_gp, gathered['ffn_w_gate']))
    h2_t = h2.T
    gsh['ffn_w_gate'] = mm_sh_out("dw_gate", h2_t, d_gp)
    gsh['ffn_w_up'] = mm_sh_out("dw_up", h2_t, d_up)
    d_x1n, dg_ffn_p = rowwise_vjp("rms_ffn_bwd", _rms_fn, [x1, w2d['ffn_norm_g']], ['row', 'const'], [d_h2], ['row'], [0, 1],
                                  heads=1, s=s, tm=tm)
    d_x1 = add_slots("d_x1_add", d_x1n[None], d_x2[None])[0]
    d_x1_b = d_x1.astype(BF16)
    d_ycat = mm_nt("d_ycat", d_x1_b, gathered['w_out'])
    gsh['w_out'] = mm("dw_out", ycat.T, d_x1_b).reshape(N_DEV, d // N_DEV, d)
    d_yr_h = _to_heads(d_ycat[:, :c_rwkv], RWKV_HEAD)
    d_o_h = _to_heads(d_ycat[:, c_rwkv:], V_HEAD)

    d_qn_h, d_qpr_h, d_kn_h, d_v_h, d_kp_h = attn_bwd(q_nope, q_pe_rot, k_nope, kp_rot, v_att, o_att, lse, d_o_h)
    (d_qp_h,) = rowwise_vjp("rope_q_bwd", _rope_q_fn, ropeq_args, ropeq_kinds, [d_qpr_h], ['hrow'], [0], heads=n_mh, s=s, tm=tm)
    d_q = pad_cols(_from_heads(jnp.concatenate([d_qn_h, d_qp_h], axis=-1)).astype(BF16), nb['mla_w_uq'], nbp['mla_w_uq'])
    d_kv = pad_cols(_from_heads(jnp.concatenate([d_kn_h, d_v_h], axis=-1)).astype(BF16), nb['mla_w_ukv'], nbp['mla_w_ukv'])
    d_kp_rot = colsum("d_kpe_heads", d_kp_h.reshape(n_mh, s * QK_ROPE)).reshape(s, QK_ROPE)
    d_qn = mm_sh_nt("d_qn", d_q, gathered['mla_w_uq'])
    d_kvn = mm_sh_nt("d_kvn", d_kv, gathered['mla_w_ukv'])
    gsh['mla_w_uq'] = mm_sh_out("dw_uq", qn.T, d_q)
    gsh['mla_w_ukv'] = mm_sh_out("dw_ukv", kvn.T, d_kv)
    d_cq, d_ckv, d_kpe, dg_q_p, dg_kv_p = rowwise_vjp(
        "mla_pre_bwd", _mla_pre_grad_fn, mla_args, mla_kinds, [d_qn, d_kvn, d_kp_rot], ['row', 'row', 'row'], [0, 1, 2, 4, 5],
        heads=1, s=s, tm=tm)

    d_y, d_r_post, d_k_post, d_v_post, d_gate_r, dgnw_p, dgnb_p, drk_p = rowwise_vjp(
        "rwkv_post_bwd", _rwkv_post_fn, post_args, post_kinds, [d_yr_h], ['hrow'], list(range(8)), heads=n_rh, s=s, tm=tm)
    d_r_sc, d_w_sc, d_k_sc, d_v_sc, d_a_sc, d_b_sc = scan_bwd(hr, decay, kx, hv, a_sc, b_sc, ckpt, d_y)
    d_k_tot = add_slots("rwkv_dk_add", d_k_sc, d_k_post)
    d_hk, d_hw_p, d_ha_p, d_hg_p, dw0_p, dw2_p, da0_p, da2_p, dg2_p, dkk_p, dka_p = rowwise_vjp(
        "rwkv_pre_bwd", _rwkv_pre_fn, pre_args, pre_kinds, [d_w_sc, d_k_tot, d_a_sc, d_b_sc, d_gate_r], ['hrow'] * 5,
        list(range(11)), heads=n_rh, s=s, tm=tm)
    d_hr = add_slots("rwkv_dr_add", d_r_sc, d_r_post)
    d_hv = add_slots("rwkv_dv_add", d_v_sc, d_v_post)
    lora_w = decay_lora + aaa_lora + gate_lora
    d_lora = colsum("d_lora_heads", jnp.concatenate([d_hw_p, d_ha_p, d_hg_p], axis=-1).reshape(n_rh, s * lora_w)).reshape(s, lora_w)
    d_shifted = jnp.concatenate([_from_heads(d_hr), _from_heads(d_hk), _from_heads(d_hv), d_lora], axis=-1)
    d_p_rwkv, dmu_p = token_shift_bwd(p_rwkv, w2d['rwkv_mu'], d_shifted, tm_wide)
    d_proj = pad_cols(jnp.concatenate([d_p_rwkv, d_cq, d_ckv, d_kpe], axis=-1).astype(BF16), nb['w_in'], nbp['w_in'])
    d_h1 = mm_sh_nt("d_h1", d_proj, gathered['w_in'])
    gsh['w_in'] = mm_sh_out("dw_in", h1.T, d_proj)
    d_xn, dg_attn_p = rowwise_vjp("rms_attn_bwd", _rms_fn, [xs, w2d['attn_norm_g']], ['row', 'const'], [d_h1], ['row'], [0, 1],
                                  heads=1, s=s, tm=tm)
    grad_x = add_slots("grad_x_add", d_xn[None], d_x1[None])[0]

    def from_heads_lora(g):
        return jnp.transpose(g, (1, 0, 2)).reshape(g.shape[1], n_rh * RWKV_HEAD)

    gw = {}
    gw['rwkv_w2'] = from_heads_lora(sum_partials("sum_dw2", dw2_p, True))
    gw['rwkv_a2'] = from_heads_lora(sum_partials("sum_da2", da2_p, True))
    gw['rwkv_g2'] = from_heads_lora(sum_partials("sum_dg2", dg2_p, True))
    dcw_pad = colsum("sum_dconv_w", dcw_p.reshape(dcw_p.shape[0], CONV_W * f_pad)).reshape(CONV_W, f_pad)
    gw['ffn_conv_w'] = unpad_cols(dcw_pad, nb['ffn_w_gate'], nbp['ffn_w_gate'])

    rep = {
        'attn_norm_g': sum_partials("sum_dg_attn", dg_attn_p, False),
        'rwkv_mu': colsum("sum_dmu", dmu_p.reshape(dmu_p.shape[0], shift_dim)),
        'rwkv_w0': sum_partials("sum_dw0", dw0_p, True).reshape(1, c_rwkv),
        'rwkv_a0': sum_partials("sum_da0", da0_p, True).reshape(1, c_rwkv),
        'rwkv_k_k': sum_partials("sum_dkk", dkk_p, True).reshape(1, c_rwkv),
        'rwkv_k_a': sum_partials("sum_dka", dka_p, True).reshape(1, c_rwkv),
        'rwkv_r_k': sum_partials("sum_drk", drk_p, True).reshape(1, c_rwkv),
        'rwkv_gn_w': sum_partials("sum_dgnw", dgnw_p, True).reshape(1, c_rwkv),
        'rwkv_gn_b': sum_partials("sum_dgnb", dgnb_p, True).reshape(1, c_rwkv),
        'mla_q_norm_g': sum_partials("sum_dg_q", dg_q_p, False),
        'mla_kv_norm_g': sum_partials("sum_dg_kv", dg_kv_p, False),
        'ffn_norm_g': sum_partials("sum_dg_ffn", dg_ffn_p, False),
        'ffn_conv_b': unpad_cols(colsum("sum_dconv_b", dcb_p.reshape(dcb_p.shape[0], f_pad)), nb['ffn_w_gate'], nbp['ffn_w_gate']),
        'final_norm_g': sum_partials("sum_dg_final", dg_final_p, False),
        'loss': sum_all("sum_loss", loss_rows.reshape(s // SUBLANES, SUBLANES)),
    }
    rep_pack = Pack([(n, w2d[n].shape) for n in REPLICATED] + [('loss', (1, 1))], 8)
    rep_all = all_gather("gather_rep_grads", rep_pack.pack(rep, F32))
    rep_sum = colsum("sum_rep_grads", rep_all.reshape(N_DEV, rep_pack.total)).reshape(rep_pack.rows, PACK_W)
    rep_g = rep_pack.unpack(rep_sum)
    loss = rep_g.pop('loss').reshape(())

    grads, deltas, new_m, new_v = dict(rep_g), {}, {}, {}
    for n in BIG:
        a, b = w2d[n].shape
        grads[n] = reduce_scatter(n, gsh[n])[:a, :b]
        deltas[n], new_m[n], new_v[n] = rowwise(
            "adamw_" + n, _adamw_fn, [w2d[n], grads[n], m2d[n], v2d[n]], ['row'] * 4, [('row', b, F32)] * 3,
            heads=1, s=a, tm=_pick(a, 256, SUBLANES))
    sm_pack = Pack([(n, w2d[n].shape) for n in SMALL_SHARDED], 8)
    g_shards = {n: _full_to_shards(gw[n], SMALL_SHARDED[n]) for n in SMALL_SHARDED}
    grads.update(sm_pack.unpack(reduce_scatter("small", sm_pack.pack(g_shards, F32, lead=(N_DEV,)))))
    rest_pack = Pack([(n, w2d[n].shape) for n in WEIGHTS if n not in BIG], 8)
    d_r, m_r, v_r = rowwise(
        "adamw_small", _adamw_fn, [rest_pack.pack(w2d, F32), rest_pack.pack(grads, F32), rest_pack.pack(m2d, F32),
                                   rest_pack.pack(v2d, F32)],
        ['row'] * 4, [('row', PACK_W, F32)] * 3, heads=1, s=rest_pack.rows, tm=_pick(rest_pack.rows, 512, SUBLANES))
    deltas.update(rest_pack.unpack(d_r))
    new_m.update(rest_pack.unpack(m_r))
    new_v.update(rest_pack.unpack(v_r))

    def shaped(dct):
        return [dct[n].reshape(out_shapes[n]) for n in WEIGHTS]

    return (loss, grad_x.reshape(x.shape), *shaped(grads), *shaped(deltas), *shaped(new_m), *shaped(new_v))
```

```python
import functools
import math

import jax
import jax.numpy as jnp
import numpy as np
from jax import lax
from jax.experimental import pallas as pl
from jax.experimental.pallas import tpu as pltpu

F32 = jnp.float32
BF16 = jnp.bfloat16
HIGHEST = lax.Precision.HIGHEST
MESH = pl.DeviceIdType.MESH

N_DEV = 8
LANES = 128
SUBLANES = 8
VMEM_LIMIT = 48 * 1024 * 1024

NORM_EPS = 1e-6
GN_EPS = 64e-5
RWKV_HEAD = 64
QK_NOPE = 128
QK_ROPE = 64
V_HEAD = 128
ROPE_THETA = 10000.0
CONV_W = 3
NEG_INF = -1e30
SCAN_CHUNK = 64
SCAN_HEADS = 4

ADAM_LR = 0.001
ADAM_B1 = 0.9
ADAM_B2 = 0.999
ADAM_EPS = 1e-08
ADAM_WD = 0.01
ADAM_STEP = 10

WEIGHTS = ['attn_norm_g', 'w_in', 'rwkv_mu', 'rwkv_w0', 'rwkv_w2', 'rwkv_a0', 'rwkv_a2', 'rwkv_g2', 'rwkv_k_k',
           'rwkv_k_a', 'rwkv_r_k', 'rwkv_gn_w', 'rwkv_gn_b', 'mla_q_norm_g', 'mla_w_uq', 'mla_kv_norm_g', 'mla_w_ukv',
           'w_out', 'ffn_norm_g', 'ffn_w_gate', 'ffn_w_up', 'ffn_conv_w', 'ffn_conv_b', 'ffn_w_down', 'final_norm_g']
BIG = {'w_in': 'col', 'mla_w_uq': 'col', 'mla_w_ukv': 'col', 'w_out': 'row', 'ffn_w_gate': 'col', 'ffn_w_up': 'col',
       'ffn_w_down': 'row'}
SMALL_SHARDED = {'rwkv_w2': 'col', 'rwkv_a2': 'col', 'rwkv_g2': 'col', 'ffn_conv_w': 'col'}
SHARDED = {**BIG, **SMALL_SHARDED}
REPLICATED = [n for n in WEIGHTS if n not in SHARDED]


def _round_up(n, m):
    return (n + m - 1) // m * m


def _pick(n, cap, unit):
    if n <= cap:
        return n
    best = None
    for t in range(unit, cap + 1, unit):
        if n % t == 0:
            best = t
    assert best is not None, (n, cap, unit)
    return best


def _params(sem):
    return pltpu.CompilerParams(dimension_semantics=sem, vmem_limit_bytes=VMEM_LIMIT)


def mm(name, a, b, add=None, out_dtype=F32):
    m, k = a.shape
    k2, n = b.shape
    assert k == k2, (name, a.shape, b.shape)
    tm = _pick(m, 512, SUBLANES * 2)
    tn = _pick(n, 640, LANES)
    has_add = add is not None

    def body(a_ref, b_ref, *rest):
        o_ref = rest[-1]
        acc = jnp.dot(a_ref[...].astype(BF16), b_ref[...].astype(BF16), preferred_element_type=F32)
        if has_add:
            acc = acc + rest[0][...].astype(F32)
        o_ref[...] = acc.astype(o_ref.dtype)

    in_specs = [pl.BlockSpec((tm, k), lambda i, j: (i, 0)), pl.BlockSpec((k, tn), lambda i, j: (0, j))]
    ops = [a, b]
    if has_add:
        in_specs.append(pl.BlockSpec((tm, tn), lambda i, j: (i, j)))
        ops.append(add)
    return pl.pallas_call(
        body, name=name, grid=(m // tm, n // tn), in_specs=in_specs,
        out_specs=pl.BlockSpec((tm, tn), lambda i, j: (i, j)),
        out_shape=jax.ShapeDtypeStruct((m, n), out_dtype),
        compiler_params=_params(("parallel", "parallel")),
    )(*ops)


def mm_nt(name, a, b, out_dtype=F32):
    m, k = a.shape
    n, k2 = b.shape
    assert k == k2, (name, a.shape, b.shape)
    tm = _pick(m, 512, SUBLANES * 2)
    tn = _pick(n, 1024, LANES)

    def body(a_ref, b_ref, o_ref):
        acc = lax.dot_general(a_ref[...].astype(BF16), b_ref[...].astype(BF16), (((1,), (1,)), ((), ())),
                              preferred_element_type=F32)
        o_ref[...] = acc.astype(o_ref.dtype)

    return pl.pallas_call(
        body, name=name, grid=(m // tm, n // tn),
        in_specs=[pl.BlockSpec((tm, k), lambda i, j: (i, 0)), pl.BlockSpec((tn, k), lambda i, j: (j, 0))],
        out_specs=pl.BlockSpec((tm, tn), lambda i, j: (i, j)),
        out_shape=jax.ShapeDtypeStruct((m, n), out_dtype),
        compiler_params=_params(("parallel", "parallel")),
    )(a, b)


def mm_sh(name, a, g, out_dtype=F32):
    m, k = a.shape
    nd, k2, nbp = g.shape
    assert k == k2, (name, a.shape, g.shape)
    tm = _pick(m, 512, SUBLANES * 2)

    def body(a_ref, b_ref, o_ref):
        o_ref[...] = jnp.dot(a_ref[...].astype(BF16), b_ref[...].astype(BF16), preferred_element_type=F32).astype(o_ref.dtype)

    return pl.pallas_call(
        body, name=name, grid=(m // tm, nd),
        in_specs=[pl.BlockSpec((tm, k), lambda i, j: (i, 0)), pl.BlockSpec((None, k, nbp), lambda i, j: (j, 0, 0))],
        out_specs=pl.BlockSpec((tm, nbp), lambda i, j: (i, j)),
        out_shape=jax.ShapeDtypeStruct((m, nd * nbp), out_dtype),
        compiler_params=_params(("parallel", "parallel")),
    )(a, g)


def mm_sh_nt(name, a, g, add=None):
    m, n = a.shape
    nd, k, nbp = g.shape
    assert n == nd * nbp, (name, a.shape, g.shape)
    tm = _pick(m, 512, SUBLANES * 2)
    has_add = add is not None

    def body(a_ref, b_ref, *rest):
        o_ref = rest[-1]
        part = lax.dot_general(a_ref[...].astype(BF16), b_ref[...].astype(BF16), (((1,), (1,)), ((), ())),
                               preferred_element_type=F32)

        @pl.when(pl.program_id(1) == 0)
        def _():
            o_ref[...] = part + rest[0][...] if has_add else part

        @pl.when(pl.program_id(1) != 0)
        def _():
            o_ref[...] += part

    in_specs = [pl.BlockSpec((tm, nbp), lambda i, j: (i, j)), pl.BlockSpec((None, k, nbp), lambda i, j: (j, 0, 0))]
    ops = [a, g]
    if has_add:
        in_specs.append(pl.BlockSpec((tm, k), lambda i, j: (i, 0)))
        ops.append(add)
    return pl.pallas_call(
        body, name=name, grid=(m // tm, nd), in_specs=in_specs,
        out_specs=pl.BlockSpec((tm, k), lambda i, j: (i, 0)),
        out_shape=jax.ShapeDtypeStruct((m, k), F32),
        compiler_params=_params(("parallel", "arbitrary")),
    )(*ops)


def mm_sh_out(name, at, b):
    k, m = at.shape
    m2, n = b.shape
    assert m == m2 and n % N_DEV == 0, (name, at.shape, b.shape)
    nbp = n // N_DEV
    tk = _pick(k, 512, SUBLANES * 2)

    def body(a_ref, b_ref, o_ref):
        o_ref[...] = jnp.dot(a_ref[...].astype(BF16), b_ref[...].astype(BF16), preferred_element_type=F32)

    return pl.pallas_call(
        body, name=name, grid=(k // tk, N_DEV),
        in_specs=[pl.BlockSpec((tk, m), lambda i, j: (i, 0)), pl.BlockSpec((m, nbp), lambda i, j: (0, j))],
        out_specs=pl.BlockSpec((None, tk, nbp), lambda i, j: (j, i, 0)),
        out_shape=jax.ShapeDtypeStruct((N_DEV, k, nbp), F32),
        compiler_params=_params(("parallel", "parallel")),
    )(at, b)


def pad_cols(y, nb, nbp):
    m = y.shape[0]
    if nb == nbp:
        return y
    return jnp.pad(y.reshape(m, N_DEV, nb), ((0, 0), (0, 0), (0, nbp - nb))).reshape(m, N_DEV * nbp)


def unpad_cols(y, nb, nbp):
    m = y.shape[0]
    if nb == nbp:
        return y
    return y.reshape(m, N_DEV, nbp)[:, :, :nb].reshape(m, N_DEV * nb)


def _in_spec(kind, a, tm):
    if kind == 'row':
        return pl.BlockSpec((tm, a.shape[1]), lambda h, i: (i, 0))
    if kind == 'hrow':
        return pl.BlockSpec((None, tm, a.shape[2]), lambda h, i: (h, i, 0))
    if kind == 'const':
        return pl.BlockSpec(a.shape, lambda h, i: (0, 0))
    assert kind == 'hconst', kind
    return pl.BlockSpec((None,) + a.shape[1:], lambda h, i: (h, 0, 0))


def _row_out(kind, c, dtype, heads, s, tm):
    if kind == 'row':
        assert heads == 1
        return jax.ShapeDtypeStruct((s, c), dtype), pl.BlockSpec((tm, c), lambda h, i: (i, 0))
    return jax.ShapeDtypeStruct((heads, s, c), dtype), pl.BlockSpec((None, tm, c), lambda h, i: (h, i, 0))


def rowwise(name, fn, arrs, kinds, outs, *, heads, s, tm):
    n_in = len(arrs)

    def body(*refs):
        vals = fn(*[r[...] for r in refs[:n_in]])
        for o, v in zip(refs[n_in:], vals, strict=True):
            o[...] = v.astype(o.dtype)

    shapes, specs = zip(*[_row_out(k, c, dt, heads, s, tm) for k, c, dt in outs])
    return pl.pallas_call(
        body, name=name, grid=(heads, s // tm),
        in_specs=[_in_spec(k, a, tm) for k, a in zip(kinds, arrs, strict=True)],
        out_specs=list(specs), out_shape=list(shapes),
        compiler_params=_params(("parallel", "parallel")),
    )(*arrs)


def rowwise_vjp(name, fn, arrs, kinds, cots, cot_kinds, wrt, *, heads, s, tm, out_dtypes=None, primal=False):
    n_in, n_cot = len(arrs), len(cots)
    nb = s // tm
    out_dtypes = out_dtypes or [F32] * len(wrt)

    def body(*refs):
        vals = [r[...] for r in refs[:n_in]]
        cvals = tuple(r[...].astype(F32) for r in refs[n_in:n_in + n_cot])
        outs = refs[n_in + n_cot:]

        def f(*dv):
            full = list(vals)
            for j, i in enumerate(wrt):
                full[i] = dv[j]
            return tuple(fn(*full))

        prim, vjp_fn = jax.vjp(f, *[vals[i].astype(F32) for i in wrt])
        grads = vjp_fn(cvals)
        for o, g in zip(outs[:len(wrt)], grads, strict=True):
            o[...] = g.astype(o.dtype)
        if primal:
            for o, p in zip(outs[len(wrt):], prim, strict=True):
                o[...] = p.astype(o.dtype)

    shapes, specs = [], []
    for i, dt in zip(wrt, out_dtypes, strict=True):
        kind, a = kinds[i], arrs[i]
        if kind in ('row', 'hrow'):
            c = a.shape[-1]
            sh, sp = _row_out('row' if (kind == 'row' and heads == 1) else 'hrow', c, dt, heads, s, tm)
        else:
            r, c = a.shape[-2:]
            sh = jax.ShapeDtypeStruct((heads, nb, r, c), dt)
            sp = pl.BlockSpec((None, None, r, c), lambda h, i: (h, i, 0, 0))
        shapes.append(sh)
        specs.append(sp)
    if primal:
        for ck, c in zip(cot_kinds, cots, strict=True):
            sh, sp = _row_out(ck, c.shape[-1], F32, heads, s, tm)
            shapes.append(sh)
            specs.append(sp)
    in_specs = [_in_spec(k, a, tm) for k, a in zip(kinds, arrs, strict=True)]
    in_specs += [_in_spec(k, a, tm) for k, a in zip(cot_kinds, cots, strict=True)]
    return pl.pallas_call(
        body, name=name, grid=(heads, nb), in_specs=in_specs, out_specs=specs, out_shape=shapes,
        compiler_params=_params(("parallel", "parallel")),
    )(*arrs, *cots)


def colsum(name, x):
    n, m = x.shape
    tc = _pick(m, 32768, LANES) if m % LANES == 0 else m

    def body(x_ref, o_ref):
        acc = x_ref[0:1, :].astype(F32)
        for r in range(1, n):
            acc = acc + x_ref[r:r + 1, :].astype(F32)
        o_ref[...] = acc

    return pl.pallas_call(
        body, name=name, grid=(m // tc,), in_specs=[pl.BlockSpec((n, tc), lambda j: (0, j))],
        out_specs=pl.BlockSpec((1, tc), lambda j: (0, j)), out_shape=jax.ShapeDtypeStruct((1, m), F32),
        compiler_params=_params(("parallel",)),
    )(x)


def sum_all(name, x):
    def body(x_ref, o_ref):
        o_ref[...] = jnp.sum(x_ref[...], keepdims=True)

    return pl.pallas_call(body, name=name, out_shape=jax.ShapeDtypeStruct((1, 1), F32))(x)


def sum_partials(name, p, per_head):
    h, nb, r, c = p.shape
    if per_head:
        flat = jnp.transpose(p, (1, 0, 2, 3)).reshape(nb, h * r * c)
        if nb == 1:
            return flat.reshape(h, r, c)
        return colsum(name, flat).reshape(h, r, c)
    flat = p.reshape(h * nb, r * c)
    if h * nb == 1:
        return flat.reshape(r, c)
    return colsum(name, flat).reshape(r, c)


def _rms_fn(x, g):
    xf = x.astype(F32)
    return (xf * lax.rsqrt(jnp.mean(xf * xf, axis=-1, keepdims=True) + NORM_EPS) * g,)


def _softplus(z):
    return jnp.maximum(z, 0.0) + jnp.log(1.0 + jnp.exp(-jnp.abs(z)))


def _rwkv_pre_fn(hk, hw, ha, hg, w0, w2, a0, a2, g2, k_k, k_a):
    zw = w0 + jnp.dot(jnp.tanh(hw), w2, preferred_element_type=F32)
    w_log = -_softplus(-zw) - 0.5
    decay = jnp.exp(-jnp.exp(w_log))
    a = jax.nn.sigmoid(a0 + jnp.dot(ha, a2, preferred_element_type=F32))
    g = jnp.dot(jax.nn.sigmoid(hg), g2, preferred_element_type=F32)
    kk = hk * k_k
    kk = kk * lax.rsqrt(jnp.maximum(jnp.sum(kk * kk, axis=-1, keepdims=True), 1e-24))
    k = hk * (1.0 + (a - 1.0) * k_a)
    return decay, k, -kk, kk * a, g


def _rwkv_post_fn(y, r, k, v, g, gn_w, gn_b, r_k):
    mu = jnp.mean(y, axis=-1, keepdims=True)
    var = jnp.mean(jnp.square(y - mu), axis=-1, keepdims=True)
    yn = (y - mu) * lax.rsqrt(var + GN_EPS) * gn_w + gn_b
    bonus = jnp.sum(r * k * r_k, axis=-1, keepdims=True) * v
    return ((yn + bonus) * g,)


def _rope_tables(pos, inv_freq2):
    ang = pos * inv_freq2
    return jnp.cos(ang), jnp.sin(ang)


def _rope(t, cos2, sin2, rot):
    return t * cos2 + jnp.dot(t, rot, precision=HIGHEST, preferred_element_type=F32) * sin2


def _mla_pre_fn(c_q, c_kv, k_pe, pos, q_g, kv_g, inv_freq2, rot):
    cos2, sin2 = _rope_tables(pos, inv_freq2)
    return _rms_fn(c_q, q_g)[0], _rms_fn(c_kv, kv_g)[0], _rope(k_pe, cos2, sin2, rot), cos2, sin2


def _mla_pre_grad_fn(c_q, c_kv, k_pe, pos, q_g, kv_g, inv_freq2, rot):
    return _mla_pre_fn(c_q, c_kv, k_pe, pos, q_g, kv_g, inv_freq2, rot)[:3]


def _rope_q_fn(q_pe, cos2, sin2, rot):
    return (_rope(q_pe, cos2, sin2, rot),)


def _loss_fn(x2, g, target):
    y = _rms_fn(x2, g)[0]
    return (0.5 * jnp.mean(jnp.square(y - target), axis=-1, keepdims=True),)


def _adamw_fn(w, g, m, v):
    m = ADAM_B1 * m + (1.0 - ADAM_B1) * g
    v = ADAM_B2 * v + (1.0 - ADAM_B2) * jnp.square(g)
    m_hat = m / (1.0 - ADAM_B1 ** ADAM_STEP)
    v_hat = v / (1.0 - ADAM_B2 ** ADAM_STEP)
    delta = -ADAM_LR * (m_hat / (jnp.sqrt(v_hat) + ADAM_EPS) + ADAM_WD * w)
    return delta, m, v


def _prev_halo_spec(c, tm):
    return pl.BlockSpec((SUBLANES, c), lambda i: (jnp.maximum(i * (tm // SUBLANES) - 1, 0), 0))


def _next_halo_spec(c, tm, s):
    return pl.BlockSpec((SUBLANES, c), lambda i: (jnp.minimum((i + 1) * (tm // SUBLANES), s // SUBLANES - 1), 0))


def _shift_down(p, halo, first_block, n):
    out = pltpu.roll(p, n, 0)
    row = lax.broadcasted_iota(jnp.int32, p.shape, 0)
    for j in range(n):
        top = jnp.where(first_block, 0.0, halo[SUBLANES - n + j:SUBLANES - n + j + 1, :])
        out = jnp.where(row == j, top, out)
    return out


def _shift_up(p, halo, last_block, n):
    rows = p.shape[0]
    out = pltpu.roll(p, rows - n, 0)
    row = lax.broadcasted_iota(jnp.int32, p.shape, 0)
    for j in range(n):
        bot = jnp.where(last_block, 0.0, halo[j:j + 1, :])
        out = jnp.where(row == rows - n + j, bot, out)
    return out


def token_shift_fwd(p, mu, tm):
    s, c = p.shape

    def body(p_ref, halo_ref, mu_ref, o_ref):
        pv = p_ref[...]
        prev = _shift_down(pv, halo_ref[...], pl.program_id(0) == 0, 1)
        o_ref[...] = pv + (prev - pv) * mu_ref[...]

    return pl.pallas_call(
        body, name="token_shift_fwd", grid=(s // tm,),
        in_specs=[pl.BlockSpec((tm, c), lambda i: (i, 0)), _prev_halo_spec(c, tm), pl.BlockSpec((1, c), lambda i: (0, 0))],
        out_specs=pl.BlockSpec((tm, c), lambda i: (i, 0)), out_shape=jax.ShapeDtypeStruct((s, c), F32),
        compiler_params=_params(("parallel",)),
    )(p, p, mu)


def token_shift_bwd(p, mu, ds, tm):
    s, c = p.shape
    nb = s // tm

    def body(p_ref, halo_ref, mu_ref, ds_ref, dsn_ref, dp_ref, dmu_ref):
        i = pl.program_id(0)
        pv, dsv, muv = p_ref[...], ds_ref[...], mu_ref[...]
        prev = _shift_down(pv, halo_ref[...], i == 0, 1)
        nxt = _shift_up(dsv, dsn_ref[...], i == nb - 1, 1)
        dp_ref[...] = dsv * (1.0 - muv) + nxt * muv
        dmu_ref[...] = jnp.sum(dsv * (prev - pv), axis=0, keepdims=True)

    return pl.pallas_call(
        body, name="token_shift_bwd", grid=(nb,),
        in_specs=[pl.BlockSpec((tm, c), lambda i: (i, 0)), _prev_halo_spec(c, tm), pl.BlockSpec((1, c), lambda i: (0, 0)),
                  pl.BlockSpec((tm, c), lambda i: (i, 0)), _next_halo_spec(c, tm, s)],
        out_specs=[pl.BlockSpec((tm, c), lambda i: (i, 0)), pl.BlockSpec((None, 1, c), lambda i: (i, 0, 0))],
        out_shape=[jax.ShapeDtypeStruct((s, c), F32), jax.ShapeDtypeStruct((nb, 1, c), F32)],
        compiler_params=_params(("parallel",)),
    )(p, p, mu, ds, ds)


def _ffn_tiles(s, f):
    return _pick(s, 256, SUBLANES), _pick(f, 1408, LANES)


def _conv_gate(gp, halo, first_block, cw, cb):
    p1 = _shift_down(gp, halo, first_block, 1)
    p2 = _shift_down(gp, halo, first_block, 2)
    return cw[0:1, :] * p2 + cw[1:2, :] * p1 + cw[2:3, :] * gp + cb, p1, p2


def ffn_act_fwd(gate_pre, up, conv_w, conv_b):
    s, f = gate_pre.shape
    tm, tc = _ffn_tiles(s, f)

    def body(gp_ref, halo_ref, up_ref, cw_ref, cb_ref, o_ref):
        gate, _, _ = _conv_gate(gp_ref[...], halo_ref[...], pl.program_id(0) == 0, cw_ref[...], cb_ref[...])
        o_ref[...] = (gate * jax.nn.sigmoid(gate) * up_ref[...]).astype(o_ref.dtype)

    blk = pl.BlockSpec((tm, tc), lambda i, j: (i, j))
    return pl.pallas_call(
        body, name="ffn_act_fwd", grid=(s // tm, f // tc),
        in_specs=[blk, pl.BlockSpec((SUBLANES, tc), lambda i, j: (jnp.maximum(i * (tm // SUBLANES) - 1, 0), j)), blk,
                  pl.BlockSpec((CONV_W, tc), lambda i, j: (0, j)), pl.BlockSpec((1, tc), lambda i, j: (0, j))],
        out_specs=blk, out_shape=jax.ShapeDtypeStruct((s, f), BF16),
        compiler_params=_params(("parallel", "parallel")),
    )(gate_pre, gate_pre, up, conv_w, conv_b)


def ffn_act_bwd1(gate_pre, up, conv_w, conv_b, d_act):
    s, f = gate_pre.shape
    tm, tc = _ffn_tiles(s, f)
    nb = s // tm

    def body(gp_ref, halo_ref, up_ref, cw_ref, cb_ref, da_ref, dg_ref, du_ref, dcw_ref, dcb_ref):
        gp = gp_ref[...]
        gate, p1, p2 = _conv_gate(gp, halo_ref[...], pl.program_id(0) == 0, cw_ref[...], cb_ref[...])
        sig = jax.nn.sigmoid(gate)
        da = da_ref[...].astype(F32)
        du_ref[...] = (da * gate * sig).astype(du_ref.dtype)
        dg = da * up_ref[...] * (sig * (1.0 + gate * (1.0 - sig)))
        dg_ref[...] = dg
        dcb_ref[...] = jnp.sum(dg, axis=0, keepdims=True)
        dcw_ref[0:1, :] = jnp.sum(dg * p2, axis=0, keepdims=True)
        dcw_ref[1:2, :] = jnp.sum(dg * p1, axis=0, keepdims=True)
        dcw_ref[2:3, :] = jnp.sum(dg * gp, axis=0, keepdims=True)

    blk = pl.BlockSpec((tm, tc), lambda i, j: (i, j))
    return pl.pallas_call(
        body, name="ffn_act_bwd1", grid=(nb, f // tc),
        in_specs=[blk, pl.BlockSpec((SUBLANES, tc), lambda i, j: (jnp.maximum(i * (tm // SUBLANES) - 1, 0), j)), blk,
                  pl.BlockSpec((CONV_W, tc), lambda i, j: (0, j)), pl.BlockSpec((1, tc), lambda i, j: (0, j)), blk],
        out_specs=[blk, blk, pl.BlockSpec((None, CONV_W, tc), lambda i, j: (i, 0, j)),
                   pl.BlockSpec((None, 1, tc), lambda i, j: (i, 0, j))],
        out_shape=[jax.ShapeDtypeStruct((s, f), F32), jax.ShapeDtypeStruct((s, f), BF16),
                   jax.ShapeDtypeStruct((nb, CONV_W, f), F32), jax.ShapeDtypeStruct((nb, 1, f), F32)],
        compiler_params=_params(("parallel", "parallel")),
    )(gate_pre, gate_pre, up, conv_w, conv_b, d_act)


def ffn_act_bwd2(d_gate, conv_w):
    s, f = d_gate.shape
    tm, tc = _ffn_tiles(s, f)
    nb = s // tm

    def body(dg_ref, halo_ref, cw_ref, o_ref):
        dg, cw = dg_ref[...], cw_ref[...]
        last = pl.program_id(0) == nb - 1
        n1 = _shift_up(dg, halo_ref[...], last, 1)
        n2 = _shift_up(dg, halo_ref[...], last, 2)
        o_ref[...] = (cw[2:3, :] * dg + cw[1:2, :] * n1 + cw[0:1, :] * n2).astype(o_ref.dtype)

    blk = pl.BlockSpec((tm, tc), lambda i, j: (i, j))
    return pl.pallas_call(
        body, name="ffn_act_bwd2", grid=(nb, f // tc),
        in_specs=[blk, pl.BlockSpec((SUBLANES, tc), lambda i, j: (jnp.minimum((i + 1) * (tm // SUBLANES), s // SUBLANES - 1), j)),
                  pl.BlockSpec((CONV_W, tc), lambda i, j: (0, j))],
        out_specs=blk, out_shape=jax.ShapeDtypeStruct((s, f), BF16),
        compiler_params=_params(("parallel", "parallel")),
    )(d_gate, d_gate, conv_w)


def _mxu(x, y, cx, cy):
    if x.ndim == 3:
        return lax.dot_general(x, y, (((cx + 1,), (cy + 1,)), ((0,), (0,))), preferred_element_type=F32)
    return lax.dot_general(x, y, (((cx,), (cy,)), ((), ())), preferred_element_type=F32)


def _split(x):
    hi = x.astype(BF16)
    return hi, (x - hi.astype(F32)).astype(BF16)


def _make_dot3(cx, cy):
    @jax.custom_vjp
    def f(x, y):
        xh, xl = _split(x)
        yh, yl = _split(y)
        return _mxu(xh, yh, cx, cy) + (_mxu(xh, yl, cx, cy) + _mxu(xl, yh, cx, cy))

    def fwd(x, y):
        return f(x, y), (x, y)

    def bwd(res, g):
        x, y = res
        dx = dot3(g, y, 1, 1 - cy) if cx == 1 else dot3(y, g, 1 - cy, 1)
        dy = dot3(x, g, 1 - cx, 0) if cy == 0 else dot3(g, x, 0, 1 - cx)
        return dx, dy

    f.defvjp(fwd, bwd)
    return f


_DOT3 = {}


def dot3(x, y, cx, cy):
    if (cx, cy) not in _DOT3:
        _DOT3[(cx, cy)] = _make_dot3(cx, cy)
    return _DOT3[(cx, cy)](x, y)


def _dot(x, y):
    return dot3(x, y, 1, 0)


def _dot_nt(x, y):
    return dot3(x, y, 1, 1)


def _dot_tn(x, y):
    return dot3(x, y, 0, 0)


def _tri_sum(x, lower):
    t = x.shape[-2]
    row = lax.broadcasted_iota(jnp.int32, (t, t), 0)
    col = lax.broadcasted_iota(jnp.int32, (t, t), 1)
    tri = jnp.where((col <= row) if lower else (col >= row), 1.0, 0.0).astype(BF16)
    if x.ndim == 3:
        tri = jnp.broadcast_to(tri[None], (x.shape[0], t, t))
    hi = x.astype(BF16)
    rest = x - hi.astype(F32)
    mid = rest.astype(BF16)
    low = (rest - mid.astype(F32)).astype(BF16)
    return _mxu(tri, hi, 1, 0) + (_mxu(tri, mid, 1, 0) + _mxu(tri, low, 1, 0))


@jax.custom_vjp
def _cumsum_rows(x):
    return _tri_sum(x, True)


_cumsum_rows.defvjp(lambda x: (_tri_sum(x, True), None), lambda _, g: (_tri_sum(g, False),))


def _scan_chunk(s0, r, w, k, v, a, b):
    t = r.shape[1]
    row = lax.broadcasted_iota(jnp.int32, (1, t, t), 1)
    col = lax.broadcasted_iota(jnp.int32, (1, t, t), 2)
    strict, incl = col < row, col <= row
    logw = jnp.log(w)
    cum = _cumsum_rows(logw)
    w_in, w_ex, w_inv = jnp.exp(cum), jnp.exp(cum - logw), jnp.exp(-cum)
    w_all = jnp.exp(jnp.sum(logw, axis=1, keepdims=True))
    at, rt, kt, bt = a * w_ex, r * w_in, k * w_inv, b * w_inv
    a_ab = jnp.where(strict, _dot_nt(at, bt), 0.0)
    a_ak = jnp.where(strict, _dot_nt(at, kt), 0.0)
    a_rk = jnp.where(incl, _dot_nt(rt, kt), 0.0)
    a_rb = jnp.where(incl, _dot_nt(rt, bt), 0.0)
    u = _dot_nt(at, s0) + _dot(a_ak, v)
    p = a_ab
    steps = int(math.log2(t))
    assert 2 ** steps == t
    for j in range(steps):
        u = u + _dot(p, u)
        if j < steps - 1:
            p = _dot(p, p)
    y = _dot_nt(rt, s0) + _dot(a_rk, v) + _dot(a_rb, u)
    s_new = s0 * w_all + _dot_tn(v, kt * w_all) + _dot_tn(u, bt * w_all)
    return y, s_new


def scan_fwd(r, w, k, v, a, b):
    h, s, n = r.shape
    t = min(SCAN_CHUNK, s)
    nc = s // t

    hb = SCAN_HEADS if h % SCAN_HEADS == 0 else 1

    def body(r_ref, w_ref, k_ref, v_ref, a_ref, b_ref, y_ref, ck_ref, st_ref):
        @pl.when(pl.program_id(1) == 0)
        def _():
            st_ref[...] = jnp.zeros_like(st_ref)

        s0 = st_ref[...]
        ck_ref[...] = s0
        y, s_new = _scan_chunk(s0, r_ref[...], w_ref[...], k_ref[...], v_ref[...], a_ref[...], b_ref[...])
        y_ref[...] = y
        st_ref[...] = s_new

    blk = pl.BlockSpec((hb, t, n), lambda hh, c: (hh, c, 0))
    return pl.pallas_call(
        body, name="rwkv_scan_fwd", grid=(h // hb, nc), in_specs=[blk] * 6,
        out_specs=[blk, pl.BlockSpec((hb, None, n, n), lambda hh, c: (hh, c, 0, 0))],
        out_shape=[jax.ShapeDtypeStruct((h, s, n), F32), jax.ShapeDtypeStruct((h, nc, n, n), F32)],
        scratch_shapes=[pltpu.VMEM((hb, n, n), F32)],
        compiler_params=_params(("parallel", "arbitrary")),
    )(r, w, k, v, a, b)


def scan_bwd(r, w, k, v, a, b, ck, dy):
    h, s, n = r.shape
    t = min(SCAN_CHUNK, s)
    nc = s // t

    hb = SCAN_HEADS if h % SCAN_HEADS == 0 else 1

    def body(r_ref, w_ref, k_ref, v_ref, a_ref, b_ref, ck_ref, dy_ref, dr_ref, dw_ref, dk_ref, dv_ref, da_ref, db_ref, ds_ref):
        @pl.when(pl.program_id(1) == 0)
        def _():
            ds_ref[...] = jnp.zeros_like(ds_ref)

        _, vjp_fn = jax.vjp(_scan_chunk, ck_ref[...], r_ref[...], w_ref[...], k_ref[...], v_ref[...], a_ref[...], b_ref[...])
        ds0, dr, dw, dk, dv, da, db = vjp_fn((dy_ref[...], ds_ref[...]))
        ds_ref[...] = ds0
        dr_ref[...], dw_ref[...], dk_ref[...], dv_ref[...], da_ref[...], db_ref[...] = dr, dw, dk, dv, da, db

    blk = pl.BlockSpec((hb, t, n), lambda hh, c: (hh, nc - 1 - c, 0))
    return pl.pallas_call(
        body, name="rwkv_scan_bwd", grid=(h // hb, nc),
        in_specs=[blk] * 6 + [pl.BlockSpec((hb, None, n, n), lambda hh, c: (hh, nc - 1 - c, 0, 0)), blk],
        out_specs=[blk] * 6, out_shape=[jax.ShapeDtypeStruct((h, s, n), F32)] * 6,
        scratch_shapes=[pltpu.VMEM((hb, n, n), F32)],
        compiler_params=_params(("parallel", "arbitrary")),
    )(r, w, k, v, a, b, ck, dy)


def _attn_scores(qn, qp, kn, kp, q0):
    scale = (QK_NOPE + QK_ROPE) ** -0.5
    sc = lax.dot_general(qn.astype(BF16), kn.astype(BF16), (((1,), (1,)), ((), ())), preferred_element_type=F32)
    sc = sc + lax.dot_general(qp.astype(BF16), kp.astype(BF16), (((1,), (1,)), ((), ())), preferred_element_type=F32)
    row = q0 + lax.broadcasted_iota(jnp.int32, sc.shape, 0)
    col = lax.broadcasted_iota(jnp.int32, sc.shape, 1)
    return jnp.where(row >= col, sc * scale, NEG_INF), scale


def attn_fwd(qn, qp, kn, kp, v):
    h, s, _ = qn.shape
    tq = _pick(s, 256, SUBLANES)

    def body(qn_ref, qp_ref, kn_ref, kp_ref, v_ref, o_ref, lse_ref):
        sc, _ = _attn_scores(qn_ref[...], qp_ref[...], kn_ref[...], kp_ref[...], pl.program_id(1) * tq)
        mx = jnp.max(sc, axis=-1, keepdims=True)
        e = jnp.exp(sc - mx)
        den = jnp.sum(e, axis=-1, keepdims=True)
        p = e / den
        o_ref[...] = jnp.dot(p.astype(BF16), v_ref[...].astype(BF16), preferred_element_type=F32)
        lse_ref[...] = mx + jnp.log(den)

    qblk = lambda c: pl.BlockSpec((None, tq, c), lambda hh, i: (hh, i, 0))
    kblk = lambda c: pl.BlockSpec((None, s, c), lambda hh, i: (hh, 0, 0))
    return pl.pallas_call(
        body, name="mla_attn_fwd", grid=(h, s // tq),
        in_specs=[qblk(QK_NOPE), qblk(QK_ROPE), kblk(QK_NOPE), pl.BlockSpec((s, QK_ROPE), lambda hh, i: (0, 0)), kblk(V_HEAD)],
        out_specs=[qblk(V_HEAD), qblk(1)],
        out_shape=[jax.ShapeDtypeStruct((h, s, V_HEAD), F32), jax.ShapeDtypeStruct((h, s, 1), F32)],
        compiler_params=_params(("parallel", "parallel")),
    )(qn, qp, kn, kp, v)


def attn_bwd(qn, qp, kn, kp, v, o, lse, do):
    h, s, _ = qn.shape
    tq = _pick(s, 256, SUBLANES)

    def body(qn_ref, qp_ref, kn_ref, kp_ref, v_ref, o_ref, lse_ref, do_ref, dqn_ref, dqp_ref, dkn_ref, dv_ref, dkp_ref):
        @pl.when(pl.program_id(1) == 0)
        def _():
            dkn_ref[...] = jnp.zeros_like(dkn_ref)
            dv_ref[...] = jnp.zeros_like(dv_ref)
            dkp_ref[...] = jnp.zeros_like(dkp_ref)

        qn_b, qp_b = qn_ref[...].astype(BF16), qp_ref[...].astype(BF16)
        kn_b, kp_b, v_b = kn_ref[...].astype(BF16), kp_ref[...].astype(BF16), v_ref[...].astype(BF16)
        sc, scale = _attn_scores(qn_b, qp_b, kn_b, kp_b, pl.program_id(1) * tq)
        p = jnp.exp(sc - lse_ref[...])
        dov = do_ref[...]
        do_b = dov.astype(BF16)
        p_b = p.astype(BF16)
        dv_ref[...] += lax.dot_general(p_b, do_b, (((0,), (0,)), ((), ())), preferred_element_type=F32)
        dp = lax.dot_general(do_b, v_b, (((1,), (1,)), ((), ())), preferred_element_type=F32)
        delta = jnp.sum(dov * o_ref[...], axis=-1, keepdims=True)
        ds = (p * (dp - delta) * scale).astype(BF16)
        dqn_ref[...] = jnp.dot(ds, kn_b, preferred_element_type=F32)
        dqp_ref[...] = jnp.dot(ds, kp_b, preferred_element_type=F32)
        dkn_ref[...] += lax.dot_general(ds, qn_b, (((0,), (0,)), ((), ())), preferred_element_type=F32)
        dkp_ref[...] += lax.dot_general(ds, qp_b, (((0,), (0,)), ((), ())), preferred_element_type=F32)

    qblk = lambda c: pl.BlockSpec((None, tq, c), lambda hh, i: (hh, i, 0))
    kblk = lambda c: pl.BlockSpec((None, s, c), lambda hh, i: (hh, 0, 0))
    return pl.pallas_call(
        body, name="mla_attn_bwd", grid=(h, s // tq),
        in_specs=[qblk(QK_NOPE), qblk(QK_ROPE), kblk(QK_NOPE), pl.BlockSpec((s, QK_ROPE), lambda hh, i: (0, 0)), kblk(V_HEAD),
                  qblk(V_HEAD), qblk(1), qblk(V_HEAD)],
        out_specs=[qblk(QK_NOPE), qblk(QK_ROPE), kblk(QK_NOPE), kblk(V_HEAD), kblk(QK_ROPE)],
        out_shape=[jax.ShapeDtypeStruct((h, s, QK_NOPE), F32), jax.ShapeDtypeStruct((h, s, QK_ROPE), F32),
                   jax.ShapeDtypeStruct((h, s, QK_NOPE), F32), jax.ShapeDtypeStruct((h, s, V_HEAD), F32),
                   jax.ShapeDtypeStruct((h, s, QK_ROPE), F32)],
        compiler_params=_params(("parallel", "arbitrary")),
    )(qn, qp, kn, kp, v, o, lse, do)


def _my_pos():
    return lax.axis_index("x"), lax.axis_index("y"), lax.axis_index("c")


def _dev_index(px, py, pc):
    return 4 * px + 2 * py + pc


def all_gather(name, shard):
    r, c = shard.shape

    def body(x_ref, out_ref, send_sems, recv_sems, local_sem):
        x, y, cc = _my_pos()
        me, sibling = (x, y, cc), (x, y, 1 - cc)
        chips = [(1 - x, y), (x, 1 - y), (1 - x, 1 - y)]

        def rows(px, py, pc):
            return out_ref.at[_dev_index(px, py, pc)]

        def copy(kk, block, to, src=None):
            return pltpu.make_async_remote_copy(
                src_ref=rows(*block) if src is None else src, dst_ref=rows(*block),
                send_sem=send_sems.at[kk], recv_sem=recv_sems.at[kk], device_id=to, device_id_type=MESH)

        mine = pltpu.make_async_copy(x_ref, rows(*me), local_sem)
        mine.start()
        first = [copy(0, me, sibling, src=x_ref)]
        first += [copy(1 + j, me, (*chip, cc), src=x_ref) for j, chip in enumerate(chips)]
        for cp in first:
            cp.start()
        passed = [copy(4 + j, (*chip, cc), sibling) for j, chip in enumerate(chips)]
        for j, chip in enumerate(chips):
            copy(1 + j, (*chip, cc), me).wait_recv()
            passed[j].start()
        copy(0, sibling, me).wait_recv()
        for j, chip in enumerate(chips):
            copy(4 + j, (*chip, 1 - cc), me).wait_recv()
        for cp in first + passed:
            cp.wait_send()
        mine.wait()

    return pl.pallas_call(
        body, name=name, out_shape=jax.ShapeDtypeStruct((N_DEV, r, c), shard.dtype),
        in_specs=[pl.BlockSpec(memory_space=pl.ANY)], out_specs=pl.BlockSpec(memory_space=pl.ANY),
        scratch_shapes=[pltpu.SemaphoreType.DMA((7,)), pltpu.SemaphoreType.DMA((7,)), pltpu.SemaphoreType.DMA],
    )(shard)


def _flip(kind):
    x, y, c = _my_pos()
    return {'c': (x, y, 1 - c), 'x': (1 - x, y, c), 'y': (x, 1 - y, c), 'xy': (1 - x, 1 - y, c)}[kind]


def exchange_sibling(name, g):
    _, r, c = g.shape

    def body(g_ref, out_ref, send_sems, recv_sems):
        x, y, cc = _my_pos()
        copies = []
        for px in range(2):
            for py in range(2):
                slot = 2 * px + py
                copies.append(pltpu.make_async_remote_copy(
                    src_ref=g_ref.at[_dev_index(px, py, 1 - cc)], dst_ref=out_ref.at[slot],
                    send_sem=send_sems.at[slot], recv_sem=recv_sems.at[slot], device_id=(x, y, 1 - cc), device_id_type=MESH))
        for cp in copies:
            cp.start()
        for cp in copies:
            cp.wait()

    return pl.pallas_call(
        body, name=name, out_shape=jax.ShapeDtypeStruct((4, r, c), g.dtype),
        in_specs=[pl.BlockSpec(memory_space=pl.ANY)], out_specs=pl.BlockSpec(memory_space=pl.ANY),
        scratch_shapes=[pltpu.SemaphoreType.DMA((4,)), pltpu.SemaphoreType.DMA((4,))],
    )(g)


def exchange_chips(name, hsum):
    _, r, c = hsum.shape

    def body(h_ref, out_ref, send_sems, recv_sems):
        x, y, cc = _my_pos()
        copies = []
        for j, (px, py) in enumerate([(1 - x, y), (x, 1 - y), (1 - x, 1 - y)]):
            copies.append(pltpu.make_async_remote_copy(
                src_ref=h_ref.at[2 * px + py], dst_ref=out_ref.at[j],
                send_sem=send_sems.at[j], recv_sem=recv_sems.at[j], device_id=(px, py, cc), device_id_type=MESH))
        for cp in copies:
            cp.start()
        for cp in copies:
            cp.wait()

    return pl.pallas_call(
        body, name=name, out_shape=jax.ShapeDtypeStruct((3, r, c), hsum.dtype),
        in_specs=[pl.BlockSpec(memory_space=pl.ANY)], out_specs=pl.BlockSpec(memory_space=pl.ANY),
        scratch_shapes=[pltpu.SemaphoreType.DMA((3,)), pltpu.SemaphoreType.DMA((3,))],
    )(hsum)


def add_slots(name, *terms):
    n, r, c = terms[0].shape
    tr = _pick(r, 512, SUBLANES)

    def body(*refs):
        acc = refs[0][...]
        for t in refs[1:-1]:
            acc = acc + t[...]
        refs[-1][...] = acc

    blk = pl.BlockSpec((None, tr, c), lambda s_, i: (s_, i, 0))
    return pl.pallas_call(
        body, name=name, grid=(n, r // tr), in_specs=[blk] * len(terms), out_specs=blk,
        out_shape=jax.ShapeDtypeStruct((n, r, c), F32), compiler_params=_params(("parallel", "parallel")),
    )(*terms)


def _rs_add_sibling(name, g, from_sibling, cc):
    _, r, c = g.shape
    tr = _pick(r, 512, SUBLANES)

    def body(cc_ref, g_ref, s_ref, o_ref):
        o_ref[...] = g_ref[...] + s_ref[...]

    blk = pl.BlockSpec((None, tr, c), lambda s_, i, cc_ref: (s_, i, 0))
    return pl.pallas_call(
        body, name=name,
        grid_spec=pltpu.PrefetchScalarGridSpec(
            num_scalar_prefetch=1, grid=(4, r // tr),
            in_specs=[pl.BlockSpec((None, None, tr, c), lambda s_, i, cc_ref: (s_, cc_ref[0], i, 0)), blk], out_specs=blk),
        out_shape=jax.ShapeDtypeStruct((4, r, c), F32), compiler_params=_params(("parallel", "parallel")),
    )(cc.reshape(1).astype(jnp.int32), g.reshape(4, 2, r, c), from_sibling)


def _rs_add_chips(name, chip_sum, from_chips, slot):
    _, r, c = chip_sum.shape
    tr = _pick(r, 512, SUBLANES)

    def body(slot_ref, h_ref, f0_ref, f1_ref, f2_ref, o_ref):
        o_ref[...] = ((h_ref[...] + f0_ref[...]) + f1_ref[...]) + f2_ref[...]

    def from_blk(j):
        return pl.BlockSpec((None, tr, c), lambda i, slot_ref: (j, i, 0))

    return pl.pallas_call(
        body, name=name,
        grid_spec=pltpu.PrefetchScalarGridSpec(
            num_scalar_prefetch=1, grid=(r // tr,),
            in_specs=[pl.BlockSpec((None, tr, c), lambda i, slot_ref: (slot_ref[0], i, 0)), from_blk(0), from_blk(1), from_blk(2)],
            out_specs=pl.BlockSpec((tr, c), lambda i, slot_ref: (i, 0))),
        out_shape=jax.ShapeDtypeStruct((r, c), F32), compiler_params=_params(("parallel",)),
    )(slot.reshape(1).astype(jnp.int32), chip_sum, from_chips, from_chips, from_chips)


def reduce_scatter(tag, g):
    x, y, cc = _my_pos()
    from_sibling = exchange_sibling("rs_sibling_" + tag, g)
    chip_sum = _rs_add_sibling("rs_add_sibling_" + tag, g, from_sibling, cc)
    from_chips = exchange_chips("rs_chips_" + tag, chip_sum)
    return _rs_add_chips("rs_add_chips_" + tag, chip_sum, from_chips, 2 * x + y)


PACK_W = 1024


class Pack:
    def __init__(self, entries, row_unit):
        self.entries = entries
        self.sizes = [int(np.prod(sh)) for _, sh in entries]
        self.offsets = np.concatenate([[0], np.cumsum(self.sizes)]).tolist()
        self.total = _round_up(self.offsets[-1], PACK_W * row_unit)
        self.rows = self.total // PACK_W

    def pack(self, arrays, dtype, lead=()):
        flat = [arrays[n].astype(dtype).reshape(lead + (-1,)) for n, _ in self.entries]
        pad = self.total - self.offsets[-1]
        if pad:
            flat.append(jnp.zeros(lead + (pad,), dtype))
        return jnp.concatenate(flat, axis=-1).reshape(lead + (self.rows, PACK_W))

    def unpack(self, buf, lead=()):
        flat = buf.reshape(lead + (self.total,))
        out = {}
        for (n, sh), off, sz in zip(self.entries, self.offsets, self.sizes):
            out[n] = lax.slice_in_dim(flat, off, off + sz, axis=len(lead)).reshape(lead + tuple(sh))
        return out


def _gathered_to_full(g, how):
    _, a, b = g.shape
    if how == 'row':
        return g.reshape(N_DEV * a, b)
    return jnp.transpose(g, (1, 0, 2)).reshape(a, N_DEV * b)


def _full_to_shards(w, how):
    a, b = w.shape
    if how == 'row':
        return w.reshape(N_DEV, a // N_DEV, b)
    return jnp.transpose(w.reshape(a, N_DEV, b // N_DEV), (1, 0, 2))


def _to_heads(t, width):
    s, c = t.shape
    return jnp.transpose(t.reshape(s, c // width, width), (1, 0, 2))


def _from_heads(t):
    h, s, w = t.shape
    return jnp.transpose(t, (1, 0, 2)).reshape(s, h * w)


def _rot_matrix():
    half = QK_ROPE // 2
    rot = np.zeros((QK_ROPE, QK_ROPE), np.float32)
    for i in range(half):
        rot[i + half, i] = -1.0
        rot[i, i + half] = 1.0
    return jnp.asarray(rot)


def _inv_freq2():
    half = QK_ROPE // 2
    inv = ROPE_THETA ** (-np.arange(half, dtype=np.float32) / half)
    return jnp.asarray(np.concatenate([inv, inv])[None, :].astype(np.float32))


def kernel(x, positions, attn_norm_g, w_in, rwkv_mu, rwkv_w0, rwkv_w2, rwkv_a0, rwkv_a2, rwkv_g2, rwkv_k_k, rwkv_k_a, rwkv_r_k, rwkv_gn_w, rwkv_gn_b, mla_q_norm_g, mla_w_uq, mla_kv_norm_g, mla_w_ukv, w_out, ffn_norm_g, ffn_w_gate, ffn_w_up, ffn_conv_w, ffn_conv_b, ffn_w_down, final_norm_g, loss_target, m_attn_norm_g, m_w_in, m_rwkv_mu, m_rwkv_w0, m_rwkv_w2, m_rwkv_a0, m_rwkv_a2, m_rwkv_g2, m_rwkv_k_k, m_rwkv_k_a, m_rwkv_r_k, m_rwkv_gn_w, m_rwkv_gn_b, m_mla_q_norm_g, m_mla_w_uq, m_mla_kv_norm_g, m_mla_w_ukv, m_w_out, m_ffn_norm_g, m_ffn_w_gate, m_ffn_w_up, m_ffn_conv_w, m_ffn_conv_b, m_ffn_w_down, m_final_norm_g, v_attn_norm_g, v_w_in, v_rwkv_mu, v_rwkv_w0, v_rwkv_w2, v_rwkv_a0, v_rwkv_a2, v_rwkv_g2, v_rwkv_k_k, v_rwkv_k_a, v_rwkv_r_k, v_rwkv_gn_w, v_rwkv_gn_b, v_mla_q_norm_g, v_mla_w_uq, v_mla_kv_norm_g, v_mla_w_ukv, v_w_out, v_ffn_norm_g, v_ffn_w_gate, v_ffn_w_up, v_ffn_conv_w, v_ffn_conv_b, v_ffn_w_down, v_final_norm_g):
    given = dict(locals())
    wts = {n: given[n] for n in WEIGHTS}
    mom_m = {n: given["m_" + n] for n in WEIGHTS}
    mom_v = {n: given["v_" + n] for n in WEIGHTS}
    out_shapes = {n: wts[n].shape for n in WEIGHTS}

    def local2d(n, a):
        if n == 'rwkv_r_k' or a.ndim <= 2:
            return a.reshape(1, -1)
        return a.reshape(a.shape[1:])

    w2d = {n: local2d(n, wts[n]) for n in WEIGHTS}
    m2d = {n: local2d(n, mom_m[n]) for n in WEIGHTS}
    v2d = {n: local2d(n, mom_v[n]) for n in WEIGHTS}

    xs = x.reshape(x.shape[1:])
    tgt = loss_target.reshape(loss_target.shape[1:])
    s, d = xs.shape
    c_rwkv = w2d['rwkv_w0'].shape[1]
    n_rh = c_rwkv // RWKV_HEAD
    decay_lora, aaa_lora, gate_lora = w2d['rwkv_w2'].shape[0], w2d['rwkv_a2'].shape[0], w2d['rwkv_g2'].shape[0]
    q_lora, kv_lora = w2d['mla_q_norm_g'].shape[1], w2d['mla_kv_norm_g'].shape[1]
    shift_dim = w2d['rwkv_mu'].shape[1]
    d_in = w2d['w_in'].shape[1] * N_DEV
    d_in_pad = _round_up(d_in, LANES)
    n_mh = w2d['mla_w_uq'].shape[1] * N_DEV // (QK_NOPE + QK_ROPE)
    d_ff = w2d['ffn_conv_b'].shape[1]
    tm = _pick(s, 256, SUBLANES)
    tm_wide = _pick(s, 128, SUBLANES)

    nb = {n: w2d[n].shape[1] for n in BIG if BIG[n] == 'col'}
    nbp = {n: _round_up(v_, LANES) for n, v_ in nb.items()}
    gathered = {}
    for n in BIG:
        w = w2d[n].astype(BF16)
        if BIG[n] == 'col':
            w = jnp.pad(w, ((0, 0), (0, nbp[n] - nb[n])))
        elif n == 'ffn_w_down':
            w = jnp.pad(w, ((0, nbp['ffn_w_gate'] - w.shape[0]), (0, 0)))
        g = all_gather("gather_" + n, w)
        gathered[n] = g if BIG[n] == 'col' else g.reshape(N_DEV * g.shape[1], g.shape[2])
    f_pad = N_DEV * nbp['ffn_w_gate']
    small_pack = Pack([(n, w2d[n].shape) for n in SMALL_SHARDED], 8)
    small_all = all_gather("gather_small", small_pack.pack(w2d, F32))
    full = {}
    for n, g in small_pack.unpack(small_all, lead=(N_DEV,)).items():
        full[n] = _gathered_to_full(g, SMALL_SHARDED[n])
    conv_w_pad = pad_cols(full['ffn_conv_w'], nb['ffn_w_gate'], nbp['ffn_w_gate'])
    conv_b_pad = pad_cols(w2d['ffn_conv_b'], nb['ffn_w_gate'], nbp['ffn_w_gate'])

    (h1,) = rowwise("rms_attn", _rms_fn, [xs, w2d['attn_norm_g']], ['row', 'const'], [('row', d, BF16)], heads=1, s=s, tm=tm)
    proj = unpad_cols(mm_sh("proj_in", h1, gathered['w_in']), nb['w_in'], nbp['w_in'])
    p_rwkv = proj[:, :shift_dim]
    c_q = proj[:, shift_dim:shift_dim + q_lora]
    c_kv = proj[:, shift_dim + q_lora:shift_dim + q_lora + kv_lora]
    k_pe = proj[:, shift_dim + q_lora + kv_lora:d_in]
    shifted = token_shift_fwd(p_rwkv, w2d['rwkv_mu'], tm_wide)
    o1, o2, o3 = c_rwkv, 2 * c_rwkv, 3 * c_rwkv
    hr = _to_heads(shifted[:, :o1], RWKV_HEAD)
    hk = _to_heads(shifted[:, o1:o2], RWKV_HEAD)
    hv = _to_heads(shifted[:, o2:o3], RWKV_HEAD)
    hw = shifted[:, o3:o3 + decay_lora]
    ha = shifted[:, o3 + decay_lora:o3 + decay_lora + aaa_lora]
    hg = shifted[:, o3 + decay_lora + aaa_lora:]

    def per_head(vec):
        return vec.reshape(n_rh, 1, RWKV_HEAD)

    def lora_heads(w):
        return jnp.transpose(w.reshape(w.shape[0], n_rh, RWKV_HEAD), (1, 0, 2))

    pre_args = [hk, hw, ha, hg, per_head(w2d['rwkv_w0']), lora_heads(full['rwkv_w2']), per_head(w2d['rwkv_a0']),
                lora_heads(full['rwkv_a2']), lora_heads(full['rwkv_g2']), per_head(w2d['rwkv_k_k']), per_head(w2d['rwkv_k_a'])]
    pre_kinds = ['hrow', 'row', 'row', 'row', 'hconst', 'hconst', 'hconst', 'hconst', 'hconst', 'hconst', 'hconst']
    decay, kx, a_sc, b_sc, gate_r = rowwise("rwkv_pre", _rwkv_pre_fn, pre_args, pre_kinds,
                                            [('hrow', RWKV_HEAD, F32)] * 5, heads=n_rh, s=s, tm=tm)
    y_scan, ckpt = scan_fwd(hr, decay, kx, hv, a_sc, b_sc)
    post_args = [y_scan, hr, kx, hv, gate_r, per_head(w2d['rwkv_gn_w']), per_head(w2d['rwkv_gn_b']), per_head(w2d['rwkv_r_k'])]
    post_kinds = ['hrow'] * 5 + ['hconst'] * 3
    (y_rwkv_h,) = rowwise("rwkv_post", _rwkv_post_fn, post_args, post_kinds, [('hrow', RWKV_HEAD, F32)], heads=n_rh, s=s, tm=tm)

    pos = positions.reshape(s, 1).astype(F32)
    rot, inv2 = _rot_matrix(), _inv_freq2()
    mla_args = [c_q, c_kv, k_pe, pos, w2d['mla_q_norm_g'], w2d['mla_kv_norm_g'], inv2, rot]
    mla_kinds = ['row', 'row', 'row', 'row', 'const', 'const', 'const', 'const']
    qn, kvn, kp_rot, cos2, sin2 = rowwise(
        "mla_pre", _mla_pre_fn, mla_args, mla_kinds,
        [('row', q_lora, BF16), ('row', kv_lora, BF16), ('row', QK_ROPE, F32), ('row', QK_ROPE, F32), ('row', QK_ROPE, F32)],
        heads=1, s=s, tm=tm)
    q = unpad_cols(mm_sh("proj_q", qn, gathered['mla_w_uq']), nb['mla_w_uq'], nbp['mla_w_uq'])
    kv = unpad_cols(mm_sh("proj_kv", kvn, gathered['mla_w_ukv']), nb['mla_w_ukv'], nbp['mla_w_ukv'])
    q_h = _to_heads(q, QK_NOPE + QK_ROPE)
    kv_h = _to_heads(kv, QK_NOPE + V_HEAD)
    q_nope, q_pe = q_h[..., :QK_NOPE], q_h[..., QK_NOPE:]
    k_nope, v_att = kv_h[..., :QK_NOPE], kv_h[..., QK_NOPE:]
    ropeq_args = [q_pe, cos2, sin2, rot]
    ropeq_kinds = ['hrow', 'row', 'row', 'const']
    (q_pe_rot,) = rowwise("rope_q", _rope_q_fn, ropeq_args, ropeq_kinds, [('hrow', QK_ROPE, F32)], heads=n_mh, s=s, tm=tm)
    o_att, lse = attn_fwd(q_nope, q_pe_rot, k_nope, kp_rot, v_att)
    ycat = jnp.concatenate([_from_heads(y_rwkv_h), _from_heads(o_att)], axis=-1).astype(BF16)
    x1 = mm("proj_out", ycat, gathered['w_out'], add=xs)
    (h2,) = rowwise("rms_ffn", _rms_fn, [x1, w2d['ffn_norm_g']], ['row', 'const'], [('row', d, BF16)], heads=1, s=s, tm=tm)
    gate_pre = mm_sh("ffn_gate", h2, gathered['ffn_w_gate'])
    up = mm_sh("ffn_up", h2, gathered['ffn_w_up'])
    act = ffn_act_fwd(gate_pre, up, conv_w_pad, conv_b_pad)
    x2 = mm("ffn_down", act, gathered['ffn_w_down'], add=x1)

    ones = jnp.ones((s, 1), F32)
    fin_g = w2d['final_norm_g']
    d_x2, dg_final_p, loss_rows = rowwise_vjp("loss_bwd", _loss_fn, [x2, fin_g, tgt], ['row', 'const', 'row'], [ones], ['row'],
                                              [0, 1], heads=1, s=s, tm=tm, primal=True)
    d_x2_b = d_x2.astype(BF16)
    d_act = mm_nt("d_act", d_x2_b, gathered['ffn_w_down'], out_dtype=BF16)
    gsh = {}
    gsh['ffn_w_down'] = mm("dw_down", act.T, d_x2_b).reshape(N_DEV, nbp['ffn_w_gate'], d)
    d_gate, d_up, dcw_p, dcb_p = ffn_act_bwd1(gate_pre, up, conv_w_pad, conv_b_pad, d_act)
    d_gp = ffn_act_bwd2(d_gate, conv_w_pad)
    d_h2 = mm_sh_nt("d_h2_up", d_up, gathered['ffn_w_up'], add=mm_sh_nt("d_h2_gate", d_gp, gathered['ffn_w_gate']))
    h2_t = h2.T
    gsh['ffn_w_gate'] = mm_sh_out("dw_gate", h2_t, d_gp)
    gsh['ffn_w_up'] = mm_sh_out("dw_up", h2_t, d_up)
    d_x1n, dg_ffn_p = rowwise_vjp("rms_ffn_bwd", _rms_fn, [x1, w2d['ffn_norm_g']], ['row', 'const'], [d_h2], ['row'], [0, 1],
                                  heads=1, s=s, tm=tm)
    d_x1 = add_slots("d_x1_add", d_x1n[None], d_x2[None])[0]
    d_x1_b = d_x1.astype(BF16)
    d_ycat = mm_nt("d_ycat", d_x1_b, gathered['w_out'])
    gsh['w_out'] = mm("dw_out", ycat.T, d_x1_b).reshape(N_DEV, d // N_DEV, d)
    d_yr_h = _to_heads(d_ycat[:, :c_rwkv], RWKV_HEAD)
    d_o_h = _to_heads(d_ycat[:, c_rwkv:], V_HEAD)

    d_qn_h, d_qpr_h, d_kn_h, d_v_h, d_kp_h = attn_bwd(q_nope, q_pe_rot, k_nope, kp_rot, v_att, o_att, lse, d_o_h)
    (d_qp_h,) = rowwise_vjp("rope_q_bwd", _rope_q_fn, ropeq_args, ropeq_kinds, [d_qpr_h], ['hrow'], [0], heads=n_mh, s=s, tm=tm)
    d_q = pad_cols(_from_heads(jnp.concatenate([d_qn_h, d_qp_h], axis=-1)).astype(BF16), nb['mla_w_uq'], nbp['mla_w_uq'])
    d_kv = pad_cols(_from_heads(jnp.concatenate([d_kn_h, d_v_h], axis=-1)).astype(BF16), nb['mla_w_ukv'], nbp['mla_w_ukv'])
    d_kp_rot = colsum("d_kpe_heads", d_kp_h.reshape(n_mh, s * QK_ROPE)).reshape(s, QK_ROPE)
    d_qn = mm_sh_nt("d_qn", d_q, gathered['mla_w_uq'])
    d_kvn = mm_sh_nt("d_kvn", d_kv, gathered['mla_w_ukv'])
    gsh['mla_w_uq'] = mm_sh_out("dw_uq", qn.T, d_q)
    gsh['mla_w_ukv'] = mm_sh_out("dw_ukv", kvn.T, d_kv)
    d_cq, d_ckv, d_kpe, dg_q_p, dg_kv_p = rowwise_vjp(
        "mla_pre_bwd", _mla_pre_grad_fn, mla_args, mla_kinds, [d_qn, d_kvn, d_kp_rot], ['row', 'row', 'row'], [0, 1, 2, 4, 5],
        heads=1, s=s, tm=tm)

    d_y, d_r_post, d_k_post, d_v_post, d_gate_r, dgnw_p, dgnb_p, drk_p = rowwise_vjp(
        "rwkv_post_bwd", _rwkv_post_fn, post_args, post_kinds, [d_yr_h], ['hrow'], list(range(8)), heads=n_rh, s=s, tm=tm)
    d_r_sc, d_w_sc, d_k_sc, d_v_sc, d_a_sc, d_b_sc = scan_bwd(hr, decay, kx, hv, a_sc, b_sc, ckpt, d_y)
    d_k_tot = add_slots("rwkv_dk_add", d_k_sc, d_k_post)
    d_hk, d_hw_p, d_ha_p, d_hg_p, dw0_p, dw2_p, da0_p, da2_p, dg2_p, dkk_p, dka_p = rowwise_vjp(
        "rwkv_pre_bwd", _rwkv_pre_fn, pre_args, pre_kinds, [d_w_sc, d_k_tot, d_a_sc, d_b_sc, d_gate_r], ['hrow'] * 5,
        list(range(11)), heads=n_rh, s=s, tm=tm)
    d_hr = add_slots("rwkv_dr_add", d_r_sc, d_r_post)
    d_hv = add_slots("rwkv_dv_add", d_v_sc, d_v_post)
    lora_w = decay_lora + aaa_lora + gate_lora
    d_lora = colsum("d_lora_heads", jnp.concatenate([d_hw_p, d_ha_p, d_hg_p], axis=-1).reshape(n_rh, s * lora_w)).reshape(s, lora_w)
    d_shifted = jnp.concatenate([_from_heads(d_hr), _from_heads(d_hk), _from_heads(d_hv), d_lora], axis=-1)
    d_p_rwkv, dmu_p = token_shift_bwd(p_rwkv, w2d['rwkv_mu'], d_shifted, tm_wide)
    d_proj = pad_cols(jnp.concatenate([d_p_rwkv, d_cq, d_ckv, d_kpe], axis=-1).astype(BF16), nb['w_in'], nbp['w_in'])
    d_h1 = mm_sh_nt("d_h1", d_proj, gathered['w_in'])
    gsh['w_in'] = mm_sh_out("dw_in", h1.T, d_proj)
    d_xn, dg_attn_p = rowwise_vjp("rms_attn_bwd", _rms_fn, [xs, w2d['attn_norm_g']], ['row', 'const'], [d_h1], ['row'], [0, 1],
                                  heads=1, s=s, tm=tm)
    grad_x = add_slots("grad_x_add", d_xn[None], d_x1[None])[0]

    def from_heads_lora(g):
        return jnp.transpose(g, (1, 0, 2)).reshape(g.shape[1], n_rh * RWKV_HEAD)

    gw = {}
    gw['rwkv_w2'] = from_heads_lora(sum_partials("sum_dw2", dw2_p, True))
    gw['rwkv_a2'] = from_heads_lora(sum_partials("sum_da2", da2_p, True))
    gw['rwkv_g2'] = from_heads_lora(sum_partials("sum_dg2", dg2_p, True))
    dcw_pad = colsum("sum_dconv_w", dcw_p.reshape(dcw_p.shape[0], CONV_W * f_pad)).reshape(CONV_W, f_pad)
    gw['ffn_conv_w'] = unpad_cols(dcw_pad, nb['ffn_w_gate'], nbp['ffn_w_gate'])

    rep = {
        'attn_norm_g': sum_partials("sum_dg_attn", dg_attn_p, False),
        'rwkv_mu': colsum("sum_dmu", dmu_p.reshape(dmu_p.shape[0], shift_dim)),
        'rwkv_w0': sum_partials("sum_dw0", dw0_p, True).reshape(1, c_rwkv),
        'rwkv_a0': sum_partials("sum_da0", da0_p, True).reshape(1, c_rwkv),
        'rwkv_k_k': sum_partials("sum_dkk", dkk_p, True).reshape(1, c_rwkv),
        'rwkv_k_a': sum_partials("sum_dka", dka_p, True).reshape(1, c_rwkv),
        'rwkv_r_k': sum_partials("sum_drk", drk_p, True).reshape(1, c_rwkv),
        'rwkv_gn_w': sum_partials("sum_dgnw", dgnw_p, True).reshape(1, c_rwkv),
        'rwkv_gn_b': sum_partials("sum_dgnb", dgnb_p, True).reshape(1, c_rwkv),
        'mla_q_norm_g': sum_partials("sum_dg_q", dg_q_p, False),
        'mla_kv_norm_g': sum_partials("sum_dg_kv", dg_kv_p, False),
        'ffn_norm_g': sum_partials("sum_dg_ffn", dg_ffn_p, False),
        'ffn_conv_b': unpad_cols(colsum("sum_dconv_b", dcb_p.reshape(dcb_p.shape[0], f_pad)), nb['ffn_w_gate'], nbp['ffn_w_gate']),
        'final_norm_g': sum_partials("sum_dg_final", dg_final_p, False),
        'loss': sum_all("sum_loss", loss_rows.reshape(s // SUBLANES, SUBLANES)),
    }
    rep_pack = Pack([(n, w2d[n].shape) for n in REPLICATED] + [('loss', (1, 1))], 8)
    rep_all = all_gather("gather_rep_grads", rep_pack.pack(rep, F32))
    rep_sum = colsum("sum_rep_grads", rep_all.reshape(N_DEV, rep_pack.total)).reshape(rep_pack.rows, PACK_W)
    rep_g = rep_pack.unpack(rep_sum)
    loss = rep_g.pop('loss').reshape(())

    grads, deltas, new_m, new_v = dict(rep_g), {}, {}, {}
    for n in BIG:
        a, b = w2d[n].shape
        grads[n] = reduce_scatter(n, gsh[n])[:a, :b]
        deltas[n], new_m[n], new_v[n] = rowwise(
            "adamw_" + n, _adamw_fn, [w2d[n], grads[n], m2d[n], v2d[n]], ['row'] * 4, [('row', b, F32)] * 3,
            heads=1, s=a, tm=_pick(a, 256, SUBLANES))
    sm_pack = Pack([(n, w2d[n].shape) for n in SMALL_SHARDED], 8)
    g_shards = {n: _full_to_shards(gw[n], SMALL_SHARDED[n]) for n in SMALL_SHARDED}
    grads.update(sm_pack.unpack(reduce_scatter("small", sm_pack.pack(g_shards, F32, lead=(N_DEV,)))))
    rest_pack = Pack([(n, w2d[n].shape) for n in WEIGHTS if n not in BIG], 8)
    d_r, m_r, v_r = rowwise(
        "adamw_small", _adamw_fn, [rest_pack.pack(w2d, F32), rest_pack.pack(grads, F32), rest_pack.pack(m2d, F32),
                                   rest_pack.pack(v2d, F32)],
        ['row'] * 4, [('row', PACK_W, F32)] * 3, heads=1, s=rest_pack.rows, tm=_pick(rest_pack.rows, 512, SUBLANES))
    deltas.update(rest_pack.unpack(d_r))
    new_m.update(rest_pack.unpack(m_r))
    new_v.update(rest_pack.unpack(v_r))

    def shaped(dct):
        return [dct[n].reshape(out_shapes[n]) for n in WEIGHTS]

    return (loss, grad_x.reshape(x.shape), *shaped(grads), *shaped(deltas), *shaped(new_m), *shaped(new_v))
```

```python
import functools
import math

import jax
import jax.numpy as jnp
import numpy as np
from jax import lax
from jax.experimental import pallas as pl
from jax.experimental.pallas import tpu as pltpu

F32 = jnp.float32
BF16 = jnp.bfloat16
HIGHEST = lax.Precision.HIGHEST
MESH = pl.DeviceIdType.MESH

N_DEV = 8
LANES = 128
SUBLANES = 8
VMEM_LIMIT = 48 * 1024 * 1024

NORM_EPS = 1e-6
GN_EPS = 64e-5
RWKV_HEAD = 64
QK_NOPE = 128
QK_ROPE = 64
V_HEAD = 128
ROPE_THETA = 10000.0
CONV_W = 3
NEG_INF = -1e30
SCAN_CHUNK = 64
SCAN_HEADS = 4

ADAM_LR = 0.001
ADAM_B1 = 0.9
ADAM_B2 = 0.999
ADAM_EPS = 1e-08
ADAM_WD = 0.01
ADAM_STEP = 10

WEIGHTS = ['attn_norm_g', 'w_in', 'rwkv_mu', 'rwkv_w0', 'rwkv_w2', 'rwkv_a0', 'rwkv_a2', 'rwkv_g2', 'rwkv_k_k',
           'rwkv_k_a', 'rwkv_r_k', 'rwkv_gn_w', 'rwkv_gn_b', 'mla_q_norm_g', 'mla_w_uq', 'mla_kv_norm_g', 'mla_w_ukv',
           'w_out', 'ffn_norm_g', 'ffn_w_gate', 'ffn_w_up', 'ffn_conv_w', 'ffn_conv_b', 'ffn_w_down', 'final_norm_g']
BIG = {'w_in': 'col', 'mla_w_uq': 'col', 'mla_w_ukv': 'col', 'w_out': 'row', 'ffn_w_gate': 'col', 'ffn_w_up': 'col',
       'ffn_w_down': 'row'}
SMALL_SHARDED = {'rwkv_w2': 'col', 'rwkv_a2': 'col', 'rwkv_g2': 'col', 'ffn_conv_w': 'col'}
SHARDED = {**BIG, **SMALL_SHARDED}
REPLICATED = [n for n in WEIGHTS if n not in SHARDED]


def _round_up(n, m):
    return (n + m - 1) // m * m


def _pick(n, cap, unit):
    if n <= cap:
        return n
    best = None
    for t in range(unit, cap + 1, unit):
        if n % t == 0:
            best = t
    assert best is not None, (n, cap, unit)
    return best


def _params(sem):
    return pltpu.CompilerParams(dimension_semantics=sem, vmem_limit_bytes=VMEM_LIMIT)


def mm(name, a, b, add=None, out_dtype=F32):
    m, k = a.shape
    k2, n = b.shape
    assert k == k2, (name, a.shape, b.shape)
    tm = _pick(m, 512, SUBLANES * 2)
    tn = _pick(n, 640, LANES)
    has_add = add is not None

    def body(a_ref, b_ref, *rest):
        o_ref = rest[-1]
        acc = jnp.dot(a_ref[...].astype(BF16), b_ref[...].astype(BF16), preferred_element_type=F32)
        if has_add:
            acc = acc + rest[0][...].astype(F32)
        o_ref[...] = acc.astype(o_ref.dtype)

    in_specs = [pl.BlockSpec((tm, k), lambda i, j: (i, 0)), pl.BlockSpec((k, tn), lambda i, j: (0, j))]
    ops = [a, b]
    if has_add:
        in_specs.append(pl.BlockSpec((tm, tn), lambda i, j: (i, j)))
        ops.append(add)
    return pl.pallas_call(
        body, name=name, grid=(m // tm, n // tn), in_specs=in_specs,
        out_specs=pl.BlockSpec((tm, tn), lambda i, j: (i, j)),
        out_shape=jax.ShapeDtypeStruct((m, n), out_dtype),
        compiler_params=_params(("parallel", "parallel")),
    )(*ops)


def mm_nt(name, a, b, out_dtype=F32):
    m, k = a.shape
    n, k2 = b.shape
    assert k == k2, (name, a.shape, b.shape)
    tm = _pick(m, 512, SUBLANES * 2)
    tn = _pick(n, 1024, LANES)

    def body(a_ref, b_ref, o_ref):
        acc = lax.dot_general(a_ref[...].astype(BF16), b_ref[...].astype(BF16), (((1,), (1,)), ((), ())),
                              preferred_element_type=F32)
        o_ref[...] = acc.astype(o_ref.dtype)

    return pl.pallas_call(
        body, name=name, grid=(m // tm, n // tn),
        in_specs=[pl.BlockSpec((tm, k), lambda i, j: (i, 0)), pl.BlockSpec((tn, k), lambda i, j: (j, 0))],
        out_specs=pl.BlockSpec((tm, tn), lambda i, j: (i, j)),
        out_shape=jax.ShapeDtypeStruct((m, n), out_dtype),
        compiler_params=_params(("parallel", "parallel")),
    )(a, b)


def mm_sh(name, a, g, out_dtype=F32):
    m, k = a.shape
    nd, k2, nbp = g.shape
    assert k == k2, (name, a.shape, g.shape)
    tm = _pick(m, 512, SUBLANES * 2)

    def body(a_ref, b_ref, o_ref):
        o_ref[...] = jnp.dot(a_ref[...].astype(BF16), b_ref[...].astype(BF16), preferred_element_type=F32).astype(o_ref.dtype)

    return pl.pallas_call(
        body, name=name, grid=(m // tm, nd),
        in_specs=[pl.BlockSpec((tm, k), lambda i, j: (i, 0)), pl.BlockSpec((None, k, nbp), lambda i, j: (j, 0, 0))],
        out_specs=pl.BlockSpec((tm, nbp), lambda i, j: (i, j)),
        out_shape=jax.ShapeDtypeStruct((m, nd * nbp), out_dtype),
        compiler_params=_params(("parallel", "parallel")),
    )(a, g)


def mm_sh_nt(name, a, g, add=None, comm=None):
    m, n = a.shape
    nd, k, nbp = g.shape
    assert n == nd * nbp, (name, a.shape, g.shape)
    tm = _pick(m, 512, SUBLANES * 2)
    has_add = add is not None

    def body(a_ref, b_ref, *rest):
        o_ref = rest[-1]
        part = lax.dot_general(a_ref[...].astype(BF16), b_ref[...].astype(BF16), (((1,), (1,)), ((), ())),
                               preferred_element_type=F32)

        @pl.when(pl.program_id(1) == 0)
        def _():
            o_ref[...] = part + rest[0][...] if has_add else part

        @pl.when(pl.program_id(1) != 0)
        def _():
            o_ref[...] += part

    in_specs = [pl.BlockSpec((tm, nbp), lambda i, j: (i, j)), pl.BlockSpec((None, k, nbp), lambda i, j: (j, 0, 0))]
    ops = [a, g]
    if has_add:
        in_specs.append(pl.BlockSpec((tm, k), lambda i, j: (i, 0)))
        ops.append(add)
    res = _pallas(
        body, name=name, grid=(m // tm, nd), in_specs=in_specs,
        out_specs=[pl.BlockSpec((tm, k), lambda i, j: (i, 0))],
        out_shape=[jax.ShapeDtypeStruct((m, k), F32)], sem=("parallel", "arbitrary"), comm=comm,
    )(*ops)
    return res[0] if comm is None else res


def mm_sh_out(name, at, b):
    k, m = at.shape
    m2, n = b.shape
    assert m == m2 and n % N_DEV == 0, (name, at.shape, b.shape)
    nbp = n // N_DEV
    tk = _pick(k, 512, SUBLANES * 2)

    def body(a_ref, b_ref, o_ref):
        o_ref[...] = jnp.dot(a_ref[...].astype(BF16), b_ref[...].astype(BF16), preferred_element_type=F32)

    return pl.pallas_call(
        body, name=name, grid=(k // tk, N_DEV),
        in_specs=[pl.BlockSpec((tk, m), lambda i, j: (i, 0)), pl.BlockSpec((m, nbp), lambda i, j: (0, j))],
        out_specs=pl.BlockSpec((None, tk, nbp), lambda i, j: (j, i, 0)),
        out_shape=jax.ShapeDtypeStruct((N_DEV, k, nbp), F32),
        compiler_params=_params(("parallel", "parallel")),
    )(at, b)


def pad_cols(y, nb, nbp):
    m = y.shape[0]
    if nb == nbp:
        return y
    return jnp.pad(y.reshape(m, N_DEV, nb), ((0, 0), (0, 0), (0, nbp - nb))).reshape(m, N_DEV * nbp)


def unpad_cols(y, nb, nbp):
    m = y.shape[0]
    if nb == nbp:
        return y
    return y.reshape(m, N_DEV, nbp)[:, :, :nb].reshape(m, N_DEV * nb)


def _in_spec(kind, a, tm):
    if kind == 'row':
        return pl.BlockSpec((tm, a.shape[1]), lambda h, i: (i, 0))
    if kind == 'hrow':
        return pl.BlockSpec((None, tm, a.shape[2]), lambda h, i: (h, i, 0))
    if kind == 'const':
        return pl.BlockSpec(a.shape, lambda h, i: (0, 0))
    assert kind == 'hconst', kind
    return pl.BlockSpec((None,) + a.shape[1:], lambda h, i: (h, 0, 0))


def _row_out(kind, c, dtype, heads, s, tm):
    if kind == 'row':
        assert heads == 1
        return jax.ShapeDtypeStruct((s, c), dtype), pl.BlockSpec((tm, c), lambda h, i: (i, 0))
    return jax.ShapeDtypeStruct((heads, s, c), dtype), pl.BlockSpec((None, tm, c), lambda h, i: (h, i, 0))


def rowwise(name, fn, arrs, kinds, outs, *, heads, s, tm):
    n_in = len(arrs)

    def body(*refs):
        vals = fn(*[r[...] for r in refs[:n_in]])
        for o, v in zip(refs[n_in:], vals, strict=True):
            o[...] = v.astype(o.dtype)

    shapes, specs = zip(*[_row_out(k, c, dt, heads, s, tm) for k, c, dt in outs])
    return pl.pallas_call(
        body, name=name, grid=(heads, s // tm),
        in_specs=[_in_spec(k, a, tm) for k, a in zip(kinds, arrs, strict=True)],
        out_specs=list(specs), out_shape=list(shapes),
        compiler_params=_params(("parallel", "parallel")),
    )(*arrs)


def rowwise_vjp(name, fn, arrs, kinds, cots, cot_kinds, wrt, *, heads, s, tm, out_dtypes=None, primal=False):
    n_in, n_cot = len(arrs), len(cots)
    nb = s // tm
    out_dtypes = out_dtypes or [F32] * len(wrt)

    def body(*refs):
        vals = [r[...] for r in refs[:n_in]]
        cvals = tuple(r[...].astype(F32) for r in refs[n_in:n_in + n_cot])
        outs = refs[n_in + n_cot:]

        def f(*dv):
            full = list(vals)
            for j, i in enumerate(wrt):
                full[i] = dv[j]
            return tuple(fn(*full))

        prim, vjp_fn = jax.vjp(f, *[vals[i].astype(F32) for i in wrt])
        grads = vjp_fn(cvals)
        for o, g in zip(outs[:len(wrt)], grads, strict=True):
            o[...] = g.astype(o.dtype)
        if primal:
            for o, p in zip(outs[len(wrt):], prim, strict=True):
                o[...] = p.astype(o.dtype)

    shapes, specs = [], []
    for i, dt in zip(wrt, out_dtypes, strict=True):
        kind, a = kinds[i], arrs[i]
        if kind in ('row', 'hrow'):
            c = a.shape[-1]
            sh, sp = _row_out('row' if (kind == 'row' and heads == 1) else 'hrow', c, dt, heads, s, tm)
        else:
            r, c = a.shape[-2:]
            sh = jax.ShapeDtypeStruct((heads, nb, r, c), dt)
            sp = pl.BlockSpec((None, None, r, c), lambda h, i: (h, i, 0, 0))
        shapes.append(sh)
        specs.append(sp)
    if primal:
        for ck, c in zip(cot_kinds, cots, strict=True):
            sh, sp = _row_out(ck, c.shape[-1], F32, heads, s, tm)
            shapes.append(sh)
            specs.append(sp)
    in_specs = [_in_spec(k, a, tm) for k, a in zip(kinds, arrs, strict=True)]
    in_specs += [_in_spec(k, a, tm) for k, a in zip(cot_kinds, cots, strict=True)]
    return pl.pallas_call(
        body, name=name, grid=(heads, nb), in_specs=in_specs, out_specs=specs, out_shape=shapes,
        compiler_params=_params(("parallel", "parallel")),
    )(*arrs, *cots)


def colsum(name, x):
    n, m = x.shape
    tc = _pick(m, 32768, LANES) if m % LANES == 0 else m

    def body(x_ref, o_ref):
        acc = x_ref[0:1, :].astype(F32)
        for r in range(1, n):
            acc = acc + x_ref[r:r + 1, :].astype(F32)
        o_ref[...] = acc

    return pl.pallas_call(
        body, name=name, grid=(m // tc,), in_specs=[pl.BlockSpec((n, tc), lambda j: (0, j))],
        out_specs=pl.BlockSpec((1, tc), lambda j: (0, j)), out_shape=jax.ShapeDtypeStruct((1, m), F32),
        compiler_params=_params(("parallel",)),
    )(x)


def sum_all(name, x):
    def body(x_ref, o_ref):
        o_ref[...] = jnp.sum(x_ref[...], keepdims=True)

    return pl.pallas_call(body, name=name, out_shape=jax.ShapeDtypeStruct((1, 1), F32))(x)


def sum_partials(name, p, per_head):
    h, nb, r, c = p.shape
    if per_head:
        flat = jnp.transpose(p, (1, 0, 2, 3)).reshape(nb, h * r * c)
        if nb == 1:
            return flat.reshape(h, r, c)
        return colsum(name, flat).reshape(h, r, c)
    flat = p.reshape(h * nb, r * c)
    if h * nb == 1:
        return flat.reshape(r, c)
    return colsum(name, flat).reshape(r, c)


def _rms_fn(x, g):
    xf = x.astype(F32)
    return (xf * lax.rsqrt(jnp.mean(xf * xf, axis=-1, keepdims=True) + NORM_EPS) * g,)


def _softplus(z):
    return jnp.maximum(z, 0.0) + jnp.log(1.0 + jnp.exp(-jnp.abs(z)))


def _rwkv_pre_fn(hk, hw, ha, hg, w0, w2, a0, a2, g2, k_k, k_a):
    zw = w0 + jnp.dot(jnp.tanh(hw), w2, preferred_element_type=F32)
    w_log = -_softplus(-zw) - 0.5
    decay = jnp.exp(-jnp.exp(w_log))
    a = jax.nn.sigmoid(a0 + jnp.dot(ha, a2, preferred_element_type=F32))
    g = jnp.dot(jax.nn.sigmoid(hg), g2, preferred_element_type=F32)
    kk = hk * k_k
    kk = kk * lax.rsqrt(jnp.maximum(jnp.sum(kk * kk, axis=-1, keepdims=True), 1e-24))
    k = hk * (1.0 + (a - 1.0) * k_a)
    return decay, k, -kk, kk * a, g


def _rwkv_post_fn(y, r, k, v, g, gn_w, gn_b, r_k):
    mu = jnp.mean(y, axis=-1, keepdims=True)
    var = jnp.mean(jnp.square(y - mu), axis=-1, keepdims=True)
    yn = (y - mu) * lax.rsqrt(var + GN_EPS) * gn_w + gn_b
    bonus = jnp.sum(r * k * r_k, axis=-1, keepdims=True) * v
    return ((yn + bonus) * g,)


def _rope_tables(pos, inv_freq2):
    ang = pos * inv_freq2
    return jnp.cos(ang), jnp.sin(ang)


def _rope(t, cos2, sin2, rot):
    return t * cos2 + jnp.dot(t, rot, precision=HIGHEST, preferred_element_type=F32) * sin2


def _mla_pre_fn(c_q, c_kv, k_pe, pos, q_g, kv_g, inv_freq2, rot):
    cos2, sin2 = _rope_tables(pos, inv_freq2)
    return _rms_fn(c_q, q_g)[0], _rms_fn(c_kv, kv_g)[0], _rope(k_pe, cos2, sin2, rot), cos2, sin2


def _mla_pre_grad_fn(c_q, c_kv, k_pe, pos, q_g, kv_g, inv_freq2, rot):
    return _mla_pre_fn(c_q, c_kv, k_pe, pos, q_g, kv_g, inv_freq2, rot)[:3]


def _rope_q_fn(q_pe, cos2, sin2, rot):
    return (_rope(q_pe, cos2, sin2, rot),)


def _loss_fn(x2, g, target):
    y = _rms_fn(x2, g)[0]
    return (0.5 * jnp.mean(jnp.square(y - target), axis=-1, keepdims=True),)


def _adamw_fn(w, g, m, v):
    m = ADAM_B1 * m + (1.0 - ADAM_B1) * g
    v = ADAM_B2 * v + (1.0 - ADAM_B2) * jnp.square(g)
    m_hat = m / (1.0 - ADAM_B1 ** ADAM_STEP)
    v_hat = v / (1.0 - ADAM_B2 ** ADAM_STEP)
    delta = -ADAM_LR * (m_hat / (jnp.sqrt(v_hat) + ADAM_EPS) + ADAM_WD * w)
    return delta, m, v


def _prev_halo_spec(c, tm):
    return pl.BlockSpec((SUBLANES, c), lambda i: (jnp.maximum(i * (tm // SUBLANES) - 1, 0), 0))


def _next_halo_spec(c, tm, s):
    return pl.BlockSpec((SUBLANES, c), lambda i: (jnp.minimum((i + 1) * (tm // SUBLANES), s // SUBLANES - 1), 0))


def _shift_down(p, halo, first_block, n):
    out = pltpu.roll(p, n, 0)
    row = lax.broadcasted_iota(jnp.int32, p.shape, 0)
    for j in range(n):
        top = jnp.where(first_block, 0.0, halo[SUBLANES - n + j:SUBLANES - n + j + 1, :])
        out = jnp.where(row == j, top, out)
    return out


def _shift_up(p, halo, last_block, n):
    rows = p.shape[0]
    out = pltpu.roll(p, rows - n, 0)
    row = lax.broadcasted_iota(jnp.int32, p.shape, 0)
    for j in range(n):
        bot = jnp.where(last_block, 0.0, halo[j:j + 1, :])
        out = jnp.where(row == rows - n + j, bot, out)
    return out


def token_shift_fwd(p, mu, tm):
    s, c = p.shape

    def body(p_ref, halo_ref, mu_ref, o_ref):
        pv = p_ref[...]
        prev = _shift_down(pv, halo_ref[...], pl.program_id(0) == 0, 1)
        o_ref[...] = pv + (prev - pv) * mu_ref[...]

    return pl.pallas_call(
        body, name="token_shift_fwd", grid=(s // tm,),
        in_specs=[pl.BlockSpec((tm, c), lambda i: (i, 0)), _prev_halo_spec(c, tm), pl.BlockSpec((1, c), lambda i: (0, 0))],
        out_specs=pl.BlockSpec((tm, c), lambda i: (i, 0)), out_shape=jax.ShapeDtypeStruct((s, c), F32),
        compiler_params=_params(("parallel",)),
    )(p, p, mu)


def token_shift_bwd(p, mu, ds, tm):
    s, c = p.shape
    nb = s // tm

    def body(p_ref, halo_ref, mu_ref, ds_ref, dsn_ref, dp_ref, dmu_ref):
        i = pl.program_id(0)
        pv, dsv, muv = p_ref[...], ds_ref[...], mu_ref[...]
        prev = _shift_down(pv, halo_ref[...], i == 0, 1)
        nxt = _shift_up(dsv, dsn_ref[...], i == nb - 1, 1)
        dp_ref[...] = dsv * (1.0 - muv) + nxt * muv
        dmu_ref[...] = jnp.sum(dsv * (prev - pv), axis=0, keepdims=True)

    return pl.pallas_call(
        body, name="token_shift_bwd", grid=(nb,),
        in_specs=[pl.BlockSpec((tm, c), lambda i: (i, 0)), _prev_halo_spec(c, tm), pl.BlockSpec((1, c), lambda i: (0, 0)),
                  pl.BlockSpec((tm, c), lambda i: (i, 0)), _next_halo_spec(c, tm, s)],
        out_specs=[pl.BlockSpec((tm, c), lambda i: (i, 0)), pl.BlockSpec((None, 1, c), lambda i: (i, 0, 0))],
        out_shape=[jax.ShapeDtypeStruct((s, c), F32), jax.ShapeDtypeStruct((nb, 1, c), F32)],
        compiler_params=_params(("parallel",)),
    )(p, p, mu, ds, ds)


def _ffn_tiles(s, f):
    return _pick(s, 256, SUBLANES), _pick(f, 1408, LANES)


def _conv_gate(gp, halo, first_block, cw, cb):
    p1 = _shift_down(gp, halo, first_block, 1)
    p2 = _shift_down(gp, halo, first_block, 2)
    return cw[0:1, :] * p2 + cw[1:2, :] * p1 + cw[2:3, :] * gp + cb, p1, p2


def ffn_act_fwd(gate_pre, up, conv_w, conv_b):
    s, f = gate_pre.shape
    tm, tc = _ffn_tiles(s, f)

    def body(gp_ref, halo_ref, up_ref, cw_ref, cb_ref, o_ref):
        gate, _, _ = _conv_gate(gp_ref[...], halo_ref[...], pl.program_id(0) == 0, cw_ref[...], cb_ref[...])
        o_ref[...] = (gate * jax.nn.sigmoid(gate) * up_ref[...]).astype(o_ref.dtype)

    blk = pl.BlockSpec((tm, tc), lambda i, j: (i, j))
    return pl.pallas_call(
        body, name="ffn_act_fwd", grid=(s // tm, f // tc),
        in_specs=[blk, pl.BlockSpec((SUBLANES, tc), lambda i, j: (jnp.maximum(i * (tm // SUBLANES) - 1, 0), j)), blk,
                  pl.BlockSpec((CONV_W, tc), lambda i, j: (0, j)), pl.BlockSpec((1, tc), lambda i, j: (0, j))],
        out_specs=blk, out_shape=jax.ShapeDtypeStruct((s, f), BF16),
        compiler_params=_params(("parallel", "parallel")),
    )(gate_pre, gate_pre, up, conv_w, conv_b)


def ffn_act_bwd1(gate_pre, up, conv_w, conv_b, d_act):
    s, f = gate_pre.shape
    tm, tc = _ffn_tiles(s, f)
    nb = s // tm

    def body(gp_ref, halo_ref, up_ref, cw_ref, cb_ref, da_ref, dg_ref, du_ref, dcw_ref, dcb_ref):
        gp = gp_ref[...]
        gate, p1, p2 = _conv_gate(gp, halo_ref[...], pl.program_id(0) == 0, cw_ref[...], cb_ref[...])
        sig = jax.nn.sigmoid(gate)
        da = da_ref[...].astype(F32)
        du_ref[...] = (da * gate * sig).astype(du_ref.dtype)
        dg = da * up_ref[...] * (sig * (1.0 + gate * (1.0 - sig)))
        dg_ref[...] = dg
        dcb_ref[...] = jnp.sum(dg, axis=0, keepdims=True)
        dcw_ref[0:1, :] = jnp.sum(dg * p2, axis=0, keepdims=True)
        dcw_ref[1:2, :] = jnp.sum(dg * p1, axis=0, keepdims=True)
        dcw_ref[2:3, :] = jnp.sum(dg * gp, axis=0, keepdims=True)

    blk = pl.BlockSpec((tm, tc), lambda i, j: (i, j))
    return pl.pallas_call(
        body, name="ffn_act_bwd1", grid=(nb, f // tc),
        in_specs=[blk, pl.BlockSpec((SUBLANES, tc), lambda i, j: (jnp.maximum(i * (tm // SUBLANES) - 1, 0), j)), blk,
                  pl.BlockSpec((CONV_W, tc), lambda i, j: (0, j)), pl.BlockSpec((1, tc), lambda i, j: (0, j)), blk],
        out_specs=[blk, blk, pl.BlockSpec((None, CONV_W, tc), lambda i, j: (i, 0, j)),
                   pl.BlockSpec((None, 1, tc), lambda i, j: (i, 0, j))],
        out_shape=[jax.ShapeDtypeStruct((s, f), F32), jax.ShapeDtypeStruct((s, f), BF16),
                   jax.ShapeDtypeStruct((nb, CONV_W, f), F32), jax.ShapeDtypeStruct((nb, 1, f), F32)],
        compiler_params=_params(("parallel", "parallel")),
    )(gate_pre, gate_pre, up, conv_w, conv_b, d_act)


def ffn_act_bwd2(d_gate, conv_w):
    s, f = d_gate.shape
    tm, tc = _ffn_tiles(s, f)
    nb = s // tm

    def body(dg_ref, halo_ref, cw_ref, o_ref):
        dg, cw = dg_ref[...], cw_ref[...]
        last = pl.program_id(0) == nb - 1
        n1 = _shift_up(dg, halo_ref[...], last, 1)
        n2 = _shift_up(dg, halo_ref[...], last, 2)
        o_ref[...] = (cw[2:3, :] * dg + cw[1:2, :] * n1 + cw[0:1, :] * n2).astype(o_ref.dtype)

    blk = pl.BlockSpec((tm, tc), lambda i, j: (i, j))
    return pl.pallas_call(
        body, name="ffn_act_bwd2", grid=(nb, f // tc),
        in_specs=[blk, pl.BlockSpec((SUBLANES, tc), lambda i, j: (jnp.minimum((i + 1) * (tm // SUBLANES), s // SUBLANES - 1), j)),
                  pl.BlockSpec((CONV_W, tc), lambda i, j: (0, j))],
        out_specs=blk, out_shape=jax.ShapeDtypeStruct((s, f), BF16),
        compiler_params=_params(("parallel", "parallel")),
    )(d_gate, d_gate, conv_w)


def _mxu(x, y, cx, cy):
    if x.ndim == 3:
        return lax.dot_general(x, y, (((cx + 1,), (cy + 1,)), ((0,), (0,))), preferred_element_type=F32)
    return lax.dot_general(x, y, (((cx,), (cy,)), ((), ())), preferred_element_type=F32)


def _split(x):
    hi = x.astype(BF16)
    return hi, (x - hi.astype(F32)).astype(BF16)


def _make_dot3(cx, cy):
    @jax.custom_vjp
    def f(x, y):
        xh, xl = _split(x)
        yh, yl = _split(y)
        return _mxu(xh, yh, cx, cy) + (_mxu(xh, yl, cx, cy) + _mxu(xl, yh, cx, cy))

    def fwd(x, y):
        return f(x, y), (x, y)

    def bwd(res, g):
        x, y = res
        dx = dot3(g, y, 1, 1 - cy) if cx == 1 else dot3(y, g, 1 - cy, 1)
        dy = dot3(x, g, 1 - cx, 0) if cy == 0 else dot3(g, x, 0, 1 - cx)
        return dx, dy

    f.defvjp(fwd, bwd)
    return f


_DOT3 = {}


def dot3(x, y, cx, cy):
    if (cx, cy) not in _DOT3:
        _DOT3[(cx, cy)] = _make_dot3(cx, cy)
    return _DOT3[(cx, cy)](x, y)


def _dot(x, y):
    return dot3(x, y, 1, 0)


def _dot_nt(x, y):
    return dot3(x, y, 1, 1)


def _dot_tn(x, y):
    return dot3(x, y, 0, 0)


def _tri_sum(x, lower):
    t = x.shape[-2]
    row = lax.broadcasted_iota(jnp.int32, (t, t), 0)
    col = lax.broadcasted_iota(jnp.int32, (t, t), 1)
    tri = jnp.where((col <= row) if lower else (col >= row), 1.0, 0.0).astype(BF16)
    if x.ndim == 3:
        tri = jnp.broadcast_to(tri[None], (x.shape[0], t, t))
    hi = x.astype(BF16)
    rest = x - hi.astype(F32)
    mid = rest.astype(BF16)
    low = (rest - mid.astype(F32)).astype(BF16)
    return _mxu(tri, hi, 1, 0) + (_mxu(tri, mid, 1, 0) + _mxu(tri, low, 1, 0))


@jax.custom_vjp
def _cumsum_rows(x):
    return _tri_sum(x, True)


_cumsum_rows.defvjp(lambda x: (_tri_sum(x, True), None), lambda _, g: (_tri_sum(g, False),))


def _scan_chunk(s0, r, w, k, v, a, b):
    t = r.shape[1]
    row = lax.broadcasted_iota(jnp.int32, (1, t, t), 1)
    col = lax.broadcasted_iota(jnp.int32, (1, t, t), 2)
    strict, incl = col < row, col <= row
    logw = jnp.log(w)
    cum = _cumsum_rows(logw)
    w_in, w_ex, w_inv = jnp.exp(cum), jnp.exp(cum - logw), jnp.exp(-cum)
    w_all = jnp.exp(jnp.sum(logw, axis=1, keepdims=True))
    at, rt, kt, bt = a * w_ex, r * w_in, k * w_inv, b * w_inv
    a_ab = jnp.where(strict, _dot_nt(at, bt), 0.0)
    a_ak = jnp.where(strict, _dot_nt(at, kt), 0.0)
    a_rk = jnp.where(incl, _dot_nt(rt, kt), 0.0)
    a_rb = jnp.where(incl, _dot_nt(rt, bt), 0.0)
    u = _dot_nt(at, s0) + _dot(a_ak, v)
    p = a_ab
    steps = int(math.log2(t))
    assert 2 ** steps == t
    for j in range(steps):
        u = u + _dot(p, u)
        if j < steps - 1:
            p = _dot(p, p)
    y = _dot_nt(rt, s0) + _dot(a_rk, v) + _dot(a_rb, u)
    s_new = s0 * w_all + _dot_tn(v, kt * w_all) + _dot_tn(u, bt * w_all)
    return y, s_new


def scan_fwd(r, w, k, v, a, b, comm=None):
    h, s, n = r.shape
    t = min(SCAN_CHUNK, s)
    nc = s // t

    hb = SCAN_HEADS if h % SCAN_HEADS == 0 else 1

    def body(r_ref, w_ref, k_ref, v_ref, a_ref, b_ref, y_ref, ck_ref, st_ref):
        @pl.when(pl.program_id(1) == 0)
        def _():
            st_ref[...] = jnp.zeros_like(st_ref)

        s0 = st_ref[...]
        ck_ref[...] = s0
        y, s_new = _scan_chunk(s0, r_ref[...], w_ref[...], k_ref[...], v_ref[...], a_ref[...], b_ref[...])
        y_ref[...] = y
        st_ref[...] = s_new

    blk = pl.BlockSpec((hb, t, n), lambda hh, c: (hh, c, 0))
    return _pallas(
        body, name="rwkv_scan_fwd", grid=(h // hb, nc), in_specs=[blk] * 6,
        out_specs=[blk, pl.BlockSpec((hb, None, n, n), lambda hh, c: (hh, c, 0, 0))],
        out_shape=[jax.ShapeDtypeStruct((h, s, n), F32), jax.ShapeDtypeStruct((h, nc, n, n), F32)],
        scratch_shapes=[pltpu.VMEM((hb, n, n), F32)], sem=("parallel", "arbitrary"), comm=comm,
    )(r, w, k, v, a, b)


def scan_bwd(r, w, k, v, a, b, ck, dy, comm=None):
    h, s, n = r.shape
    t = min(SCAN_CHUNK, s)
    nc = s // t

    hb = SCAN_HEADS if h % SCAN_HEADS == 0 else 1

    def body(r_ref, w_ref, k_ref, v_ref, a_ref, b_ref, ck_ref, dy_ref, dr_ref, dw_ref, dk_ref, dv_ref, da_ref, db_ref, ds_ref):
        @pl.when(pl.program_id(1) == 0)
        def _():
            ds_ref[...] = jnp.zeros_like(ds_ref)

        _, vjp_fn = jax.vjp(_scan_chunk, ck_ref[...], r_ref[...], w_ref[...], k_ref[...], v_ref[...], a_ref[...], b_ref[...])
        ds0, dr, dw, dk, dv, da, db = vjp_fn((dy_ref[...], ds_ref[...]))
        ds_ref[...] = ds0
        dr_ref[...], dw_ref[...], dk_ref[...], dv_ref[...], da_ref[...], db_ref[...] = dr, dw, dk, dv, da, db

    blk = pl.BlockSpec((hb, t, n), lambda hh, c: (hh, nc - 1 - c, 0))
    return _pallas(
        body, name="rwkv_scan_bwd", grid=(h // hb, nc),
        in_specs=[blk] * 6 + [pl.BlockSpec((hb, None, n, n), lambda hh, c: (hh, nc - 1 - c, 0, 0)), blk],
        out_specs=[blk] * 6, out_shape=[jax.ShapeDtypeStruct((h, s, n), F32)] * 6,
        scratch_shapes=[pltpu.VMEM((hb, n, n), F32)], sem=("parallel", "arbitrary"), comm=comm,
    )(r, w, k, v, a, b, ck, dy)


def _attn_scores(qn, qp, kn, kp, q0):
    scale = (QK_NOPE + QK_ROPE) ** -0.5
    sc = lax.dot_general(qn.astype(BF16), kn.astype(BF16), (((1,), (1,)), ((), ())), preferred_element_type=F32)
    sc = sc + lax.dot_general(qp.astype(BF16), kp.astype(BF16), (((1,), (1,)), ((), ())), preferred_element_type=F32)
    row = q0 + lax.broadcasted_iota(jnp.int32, sc.shape, 0)
    col = lax.broadcasted_iota(jnp.int32, sc.shape, 1)
    return jnp.where(row >= col, sc * scale, NEG_INF), scale


def attn_fwd(qn, qp, kn, kp, v):
    h, s, _ = qn.shape
    tq = _pick(s, 256, SUBLANES)

    def body(qn_ref, qp_ref, kn_ref, kp_ref, v_ref, o_ref, lse_ref):
        sc, _ = _attn_scores(qn_ref[...], qp_ref[...], kn_ref[...], kp_ref[...], pl.program_id(1) * tq)
        mx = jnp.max(sc, axis=-1, keepdims=True)
        e = jnp.exp(sc - mx)
        den = jnp.sum(e, axis=-1, keepdims=True)
        p = e / den
        o_ref[...] = jnp.dot(p.astype(BF16), v_ref[...].astype(BF16), preferred_element_type=F32)
        lse_ref[...] = mx + jnp.log(den)

    qblk = lambda c: pl.BlockSpec((None, tq, c), lambda hh, i: (hh, i, 0))
    kblk = lambda c: pl.BlockSpec((None, s, c), lambda hh, i: (hh, 0, 0))
    return pl.pallas_call(
        body, name="mla_attn_fwd", grid=(h, s // tq),
        in_specs=[qblk(QK_NOPE), qblk(QK_ROPE), kblk(QK_NOPE), pl.BlockSpec((s, QK_ROPE), lambda hh, i: (0, 0)), kblk(V_HEAD)],
        out_specs=[qblk(V_HEAD), qblk(1)],
        out_shape=[jax.ShapeDtypeStruct((h, s, V_HEAD), F32), jax.ShapeDtypeStruct((h, s, 1), F32)],
        compiler_params=_params(("parallel", "parallel")),
    )(qn, qp, kn, kp, v)


def attn_bwd(qn, qp, kn, kp, v, o, lse, do):
    h, s, _ = qn.shape
    tq = _pick(s, 256, SUBLANES)

    def body(qn_ref, qp_ref, kn_ref, kp_ref, v_ref, o_ref, lse_ref, do_ref, dqn_ref, dqp_ref, dkn_ref, dv_ref, dkp_ref):
        @pl.when(pl.program_id(1) == 0)
        def _():
            dkn_ref[...] = jnp.zeros_like(dkn_ref)
            dv_ref[...] = jnp.zeros_like(dv_ref)
            dkp_ref[...] = jnp.zeros_like(dkp_ref)

        qn_b, qp_b = qn_ref[...].astype(BF16), qp_ref[...].astype(BF16)
        kn_b, kp_b, v_b = kn_ref[...].astype(BF16), kp_ref[...].astype(BF16), v_ref[...].astype(BF16)
        sc, scale = _attn_scores(qn_b, qp_b, kn_b, kp_b, pl.program_id(1) * tq)
        p = jnp.exp(sc - lse_ref[...])
        dov = do_ref[...]
        do_b = dov.astype(BF16)
        p_b = p.astype(BF16)
        dv_ref[...] += lax.dot_general(p_b, do_b, (((0,), (0,)), ((), ())), preferred_element_type=F32)
        dp = lax.dot_general(do_b, v_b, (((1,), (1,)), ((), ())), preferred_element_type=F32)
        delta = jnp.sum(dov * o_ref[...], axis=-1, keepdims=True)
        ds = (p * (dp - delta) * scale).astype(BF16)
        dqn_ref[...] = jnp.dot(ds, kn_b, preferred_element_type=F32)
        dqp_ref[...] = jnp.dot(ds, kp_b, preferred_element_type=F32)
        dkn_ref[...] += lax.dot_general(ds, qn_b, (((0,), (0,)), ((), ())), preferred_element_type=F32)
        dkp_ref[...] += lax.dot_general(ds, qp_b, (((0,), (0,)), ((), ())), preferred_element_type=F32)

    qblk = lambda c: pl.BlockSpec((None, tq, c), lambda hh, i: (hh, i, 0))
    kblk = lambda c: pl.BlockSpec((None, s, c), lambda hh, i: (hh, 0, 0))
    return pl.pallas_call(
        body, name="mla_attn_bwd", grid=(h, s // tq),
        in_specs=[qblk(QK_NOPE), qblk(QK_ROPE), kblk(QK_NOPE), pl.BlockSpec((s, QK_ROPE), lambda hh, i: (0, 0)), kblk(V_HEAD),
                  qblk(V_HEAD), qblk(1), qblk(V_HEAD)],
        out_specs=[qblk(QK_NOPE), qblk(QK_ROPE), kblk(QK_NOPE), kblk(V_HEAD), kblk(QK_ROPE)],
        out_shape=[jax.ShapeDtypeStruct((h, s, QK_NOPE), F32), jax.ShapeDtypeStruct((h, s, QK_ROPE), F32),
                   jax.ShapeDtypeStruct((h, s, QK_NOPE), F32), jax.ShapeDtypeStruct((h, s, V_HEAD), F32),
                   jax.ShapeDtypeStruct((h, s, QK_ROPE), F32)],
        compiler_params=_params(("parallel", "arbitrary")),
    )(qn, qp, kn, kp, v, o, lse, do)


def _my_pos():
    return lax.axis_index("x"), lax.axis_index("y"), lax.axis_index("c")


def _dev_index(px, py, pc):
    return 4 * px + 2 * py + pc


def all_gather(name, shard):
    r, c = shard.shape

    def body(x_ref, out_ref, send_sems, recv_sems, local_sem):
        x, y, cc = _my_pos()
        me, sibling = (x, y, cc), (x, y, 1 - cc)
        chips = [(1 - x, y), (x, 1 - y), (1 - x, 1 - y)]

        def rows(px, py, pc):
            return out_ref.at[_dev_index(px, py, pc)]

        def copy(kk, block, to, src=None):
            return pltpu.make_async_remote_copy(
                src_ref=rows(*block) if src is None else src, dst_ref=rows(*block),
                send_sem=send_sems.at[kk], recv_sem=recv_sems.at[kk], device_id=to, device_id_type=MESH)

        mine = pltpu.make_async_copy(x_ref, rows(*me), local_sem)
        mine.start()
        first = [copy(0, me, sibling, src=x_ref)]
        first += [copy(1 + j, me, (*chip, cc), src=x_ref) for j, chip in enumerate(chips)]
        for cp in first:
            cp.start()
        passed = [copy(4 + j, (*chip, cc), sibling) for j, chip in enumerate(chips)]
        for j, chip in enumerate(chips):
            copy(1 + j, (*chip, cc), me).wait_recv()
            passed[j].start()
        copy(0, sibling, me).wait_recv()
        for j, chip in enumerate(chips):
            copy(4 + j, (*chip, 1 - cc), me).wait_recv()
        for cp in first + passed:
            cp.wait_send()
        mine.wait()

    return pl.pallas_call(
        body, name=name, out_shape=jax.ShapeDtypeStruct((N_DEV, r, c), shard.dtype),
        in_specs=[pl.BlockSpec(memory_space=pl.ANY)], out_specs=pl.BlockSpec(memory_space=pl.ANY),
        scratch_shapes=[pltpu.SemaphoreType.DMA((7,)), pltpu.SemaphoreType.DMA((7,)), pltpu.SemaphoreType.DMA],
    )(shard)


def _flip(kind):
    x, y, c = _my_pos()
    return {'c': (x, y, 1 - c), 'x': (1 - x, y, c), 'y': (x, 1 - y, c), 'xy': (1 - x, 1 - y, c)}[kind]


def exchange_sibling(name, g):
    _, r, c = g.shape

    def body(g_ref, out_ref, send_sems, recv_sems):
        x, y, cc = _my_pos()
        copies = []
        for px in range(2):
            for py in range(2):
                slot = 2 * px + py
                copies.append(pltpu.make_async_remote_copy(
                    src_ref=g_ref.at[_dev_index(px, py, 1 - cc)], dst_ref=out_ref.at[slot],
                    send_sem=send_sems.at[slot], recv_sem=recv_sems.at[slot], device_id=(x, y, 1 - cc), device_id_type=MESH))
        for cp in copies:
            cp.start()
        for cp in copies:
            cp.wait()

    return pl.pallas_call(
        body, name=name, out_shape=jax.ShapeDtypeStruct((4, r, c), g.dtype),
        in_specs=[pl.BlockSpec(memory_space=pl.ANY)], out_specs=pl.BlockSpec(memory_space=pl.ANY),
        scratch_shapes=[pltpu.SemaphoreType.DMA((4,)), pltpu.SemaphoreType.DMA((4,))],
    )(g)


def exchange_chips(name, hsum):
    _, r, c = hsum.shape

    def body(h_ref, out_ref, send_sems, recv_sems):
        x, y, cc = _my_pos()
        copies = []
        for j, (px, py) in enumerate([(1 - x, y), (x, 1 - y), (1 - x, 1 - y)]):
            copies.append(pltpu.make_async_remote_copy(
                src_ref=h_ref.at[2 * px + py], dst_ref=out_ref.at[j],
                send_sem=send_sems.at[j], recv_sem=recv_sems.at[j], device_id=(px, py, cc), device_id_type=MESH))
        for cp in copies:
            cp.start()
        for cp in copies:
            cp.wait()

    return pl.pallas_call(
        body, name=name, out_shape=jax.ShapeDtypeStruct((3, r, c), hsum.dtype),
        in_specs=[pl.BlockSpec(memory_space=pl.ANY)], out_specs=pl.BlockSpec(memory_space=pl.ANY),
        scratch_shapes=[pltpu.SemaphoreType.DMA((3,)), pltpu.SemaphoreType.DMA((3,))],
    )(hsum)


def add_slots(name, *terms):
    n, r, c = terms[0].shape
    tr = _pick(r, 512, SUBLANES)

    def body(*refs):
        acc = refs[0][...]
        for t in refs[1:-1]:
            acc = acc + t[...]
        refs[-1][...] = acc

    blk = pl.BlockSpec((None, tr, c), lambda s_, i: (s_, i, 0))
    return pl.pallas_call(
        body, name=name, grid=(n, r // tr), in_specs=[blk] * len(terms), out_specs=blk,
        out_shape=jax.ShapeDtypeStruct((n, r, c), F32), compiler_params=_params(("parallel", "parallel")),
    )(*terms)


def _rs_add_sibling(name, g, from_sibling, cc):
    _, r, c = g.shape
    tr = _pick(r, 512, SUBLANES * 2)

    def body(cc_ref, g_ref, s_ref, o_ref, ob_ref):
        tot = g_ref[...] + s_ref[...]
        o_ref[...] = tot
        ob_ref[...] = tot.astype(BF16)

    blk = pl.BlockSpec((None, tr, c), lambda s_, i, cc_ref: (s_, i, 0))
    return pl.pallas_call(
        body, name=name,
        grid_spec=pltpu.PrefetchScalarGridSpec(
            num_scalar_prefetch=1, grid=(4, r // tr),
            in_specs=[pl.BlockSpec((None, None, tr, c), lambda s_, i, cc_ref: (s_, cc_ref[0], i, 0)), blk], out_specs=[blk, blk]),
        out_shape=[jax.ShapeDtypeStruct((4, r, c), F32), jax.ShapeDtypeStruct((4, r, c), BF16)],
        compiler_params=_params(("parallel", "parallel")),
    )(cc.reshape(1).astype(jnp.int32), g.reshape(4, 2, r, c), from_sibling)


def _rs_add_chips(name, chip_sum, from_chips, slot):
    _, r, c = chip_sum.shape
    tr = _pick(r, 512, SUBLANES * 2)

    def body(slot_ref, h_ref, f0_ref, f1_ref, f2_ref, o_ref):
        o_ref[...] = ((h_ref[...] + f0_ref[...].astype(F32)) + f1_ref[...].astype(F32)) + f2_ref[...].astype(F32)

    def from_blk(j):
        return pl.BlockSpec((None, tr, c), lambda i, slot_ref: (j, i, 0))

    return pl.pallas_call(
        body, name=name,
        grid_spec=pltpu.PrefetchScalarGridSpec(
            num_scalar_prefetch=1, grid=(r // tr,),
            in_specs=[pl.BlockSpec((None, tr, c), lambda i, slot_ref: (slot_ref[0], i, 0)), from_blk(0), from_blk(1), from_blk(2)],
            out_specs=pl.BlockSpec((tr, c), lambda i, slot_ref: (i, 0))),
        out_shape=jax.ShapeDtypeStruct((r, c), F32), compiler_params=_params(("parallel",)),
    )(slot.reshape(1).astype(jnp.int32), chip_sum, from_chips, from_chips, from_chips)


def rs_chip_sum(tag, g):
    _, _, cc = _my_pos()
    from_sibling = exchange_sibling("rs_sibling_" + tag, g)
    return _rs_add_sibling("rs_add_sibling_" + tag, g, from_sibling, cc)


def rs_finish(tag, chip_sum, from_chips):
    x, y, _ = _my_pos()
    return _rs_add_chips("rs_add_chips_" + tag, chip_sum, from_chips, 2 * x + y)


def reduce_scatter(tag, g):
    chip_sum, chip_sum_b = rs_chip_sum(tag, g)
    return rs_finish(tag, chip_sum, exchange_chips("rs_chips_" + tag, chip_sum_b))


class GatherIci:
    def __init__(self, shards):
        self.inputs = list(shards)
        self.out_shapes = [jax.ShapeDtypeStruct((N_DEV,) + s.shape, s.dtype) for s in shards]
        self.n_remote, self.n_local = 3 * len(shards), len(shards)

    def make(self, cins, couts, send, recv, local):
        x, y, cc = _my_pos()
        me = _dev_index(x, y, cc)
        copies = []
        for w, (src, out) in enumerate(zip(cins, couts, strict=True)):
            copies.append(pltpu.make_async_copy(src, out.at[me], local.at[w]))
            for j, (px, py) in enumerate([(1 - x, y), (x, 1 - y), (1 - x, 1 - y)]):
                copies.append(pltpu.make_async_remote_copy(
                    src_ref=src, dst_ref=out.at[me], send_sem=send.at[3 * w + j], recv_sem=recv.at[3 * w + j],
                    device_id=(px, py, cc), device_id_type=MESH))
        return copies


class RsChips:
    def __init__(self, chip_sums):
        self.inputs = list(chip_sums)
        self.out_shapes = [jax.ShapeDtypeStruct((3,) + h.shape[1:], h.dtype) for h in chip_sums]
        self.n_remote, self.n_local = 3 * len(chip_sums), 0

    def make(self, cins, couts, send, recv, local):
        x, y, cc = _my_pos()
        copies = []
        for w, (h_ref, out) in enumerate(zip(cins, couts, strict=True)):
            for j, (px, py) in enumerate([(1 - x, y), (x, 1 - y), (1 - x, 1 - y)]):
                copies.append(pltpu.make_async_remote_copy(
                    src_ref=h_ref.at[2 * px + py], dst_ref=out.at[j], send_sem=send.at[3 * w + j], recv_sem=recv.at[3 * w + j],
                    device_id=(px, py, cc), device_id_type=MESH))
        return copies


def gather_d2d(name, arrays):
    n = len(arrays)

    def body(*refs):
        outs, send, recv = refs[n:2 * n], refs[2 * n], refs[2 * n + 1]
        x, y, cc = _my_pos()
        copies = []
        for w, out in enumerate(outs):
            for px in range(2):
                for py in range(2):
                    q = 4 * w + 2 * px + py
                    slab = out.at[_dev_index(px, py, cc)]
                    copies.append(pltpu.make_async_remote_copy(
                        src_ref=slab, dst_ref=slab, send_sem=send.at[q], recv_sem=recv.at[q],
                        device_id=(x, y, 1 - cc), device_id_type=MESH))
        for cp in copies:
            cp.start()
        for cp in copies:
            cp.wait()

    any_spec = pl.BlockSpec(memory_space=pl.ANY)
    return pl.pallas_call(
        body, name=name, out_shape=[jax.ShapeDtypeStruct(a.shape, a.dtype) for a in arrays],
        in_specs=[any_spec] * n, out_specs=[any_spec] * n, input_output_aliases={i: i for i in range(n)},
        scratch_shapes=[pltpu.SemaphoreType.DMA((4 * n,)), pltpu.SemaphoreType.DMA((4 * n,))],
    )(*arrays)


def _pallas(body, *, name, grid, in_specs, out_specs, out_shape, scratch_shapes=(), sem, comm=None):
    in_specs, out_specs, out_shape, scratch_shapes = list(in_specs), list(out_specs), list(out_shape), list(scratch_shapes)
    if comm is None:
        return pl.pallas_call(body, name=name, grid=grid, in_specs=in_specs, out_specs=out_specs, out_shape=out_shape,
                              scratch_shapes=scratch_shapes, compiler_params=_params(sem))
    n_in, n_out, n_scr = len(in_specs), len(out_specs), len(scratch_shapes)
    nci, nco = len(comm.inputs), len(comm.out_shapes)

    def body2(*refs):
        ins, cins = refs[:n_in], refs[n_in:n_in + nci]
        o0 = n_in + nci
        outs, couts = refs[o0:o0 + n_out], refs[o0 + n_out:o0 + n_out + nco]
        s0 = o0 + n_out + nco
        scr = refs[s0:s0 + n_scr]
        send, recv, local = refs[s0 + n_scr:]
        pids = [pl.program_id(k) for k in range(len(grid))]
        first = functools.reduce(jnp.logical_and, [p == 0 for p in pids])
        last = functools.reduce(jnp.logical_and, [p == g - 1 for p, g in zip(pids, grid)])

        @pl.when(first)
        def _():
            for cp in comm.make(cins, couts, send, recv, local):
                cp.start()

        body(*ins, *outs, *scr)

        @pl.when(last)
        def _():
            for cp in comm.make(cins, couts, send, recv, local):
                cp.wait()

    any_spec = pl.BlockSpec(memory_space=pl.ANY)
    call = pl.pallas_call(
        body2, name=name, grid=grid, in_specs=in_specs + [any_spec] * nci, out_specs=out_specs + [any_spec] * nco,
        out_shape=out_shape + list(comm.out_shapes),
        scratch_shapes=scratch_shapes + [pltpu.SemaphoreType.DMA((comm.n_remote,)), pltpu.SemaphoreType.DMA((comm.n_remote,)),
                                         pltpu.SemaphoreType.DMA((max(comm.n_local, 1),))],
        compiler_params=_params(tuple("arbitrary" for _ in grid)))
    return lambda *args: call(*args, *comm.inputs)


PACK_W = 1024


class Pack:
    def __init__(self, entries, row_unit):
        self.entries = entries
        self.sizes = [int(np.prod(sh)) for _, sh in entries]
        self.offsets = np.concatenate([[0], np.cumsum(self.sizes)]).tolist()
        self.total = _round_up(self.offsets[-1], PACK_W * row_unit)
        self.rows = self.total // PACK_W

    def pack(self, arrays, dtype, lead=()):
        flat = [arrays[n].astype(dtype).reshape(lead + (-1,)) for n, _ in self.entries]
        pad = self.total - self.offsets[-1]
        if pad:
            flat.append(jnp.zeros(lead + (pad,), dtype))
        return jnp.concatenate(flat, axis=-1).reshape(lead + (self.rows, PACK_W))

    def unpack(self, buf, lead=()):
        flat = buf.reshape(lead + (self.total,))
        out = {}
        for (n, sh), off, sz in zip(self.entries, self.offsets, self.sizes):
            out[n] = lax.slice_in_dim(flat, off, off + sz, axis=len(lead)).reshape(lead + tuple(sh))
        return out


def _gathered_to_full(g, how):
    _, a, b = g.shape
    if how == 'row':
        return g.reshape(N_DEV * a, b)
    return jnp.transpose(g, (1, 0, 2)).reshape(a, N_DEV * b)


def _full_to_shards(w, how):
    a, b = w.shape
    if how == 'row':
        return w.reshape(N_DEV, a // N_DEV, b)
    return jnp.transpose(w.reshape(a, N_DEV, b // N_DEV), (1, 0, 2))


def _to_heads(t, width):
    s, c = t.shape
    return jnp.transpose(t.reshape(s, c // width, width), (1, 0, 2))


def _from_heads(t):
    h, s, w = t.shape
    return jnp.transpose(t, (1, 0, 2)).reshape(s, h * w)


def _rot_matrix():
    half = QK_ROPE // 2
    rot = np.zeros((QK_ROPE, QK_ROPE), np.float32)
    for i in range(half):
        rot[i + half, i] = -1.0
        rot[i, i + half] = 1.0
    return jnp.asarray(rot)


def _inv_freq2():
    half = QK_ROPE // 2
    inv = ROPE_THETA ** (-np.arange(half, dtype=np.float32) / half)
    return jnp.asarray(np.concatenate([inv, inv])[None, :].astype(np.float32))


def kernel(x, positions, attn_norm_g, w_in, rwkv_mu, rwkv_w0, rwkv_w2, rwkv_a0, rwkv_a2, rwkv_g2, rwkv_k_k, rwkv_k_a, rwkv_r_k, rwkv_gn_w, rwkv_gn_b, mla_q_norm_g, mla_w_uq, mla_kv_norm_g, mla_w_ukv, w_out, ffn_norm_g, ffn_w_gate, ffn_w_up, ffn_conv_w, ffn_conv_b, ffn_w_down, final_norm_g, loss_target, m_attn_norm_g, m_w_in, m_rwkv_mu, m_rwkv_w0, m_rwkv_w2, m_rwkv_a0, m_rwkv_a2, m_rwkv_g2, m_rwkv_k_k, m_rwkv_k_a, m_rwkv_r_k, m_rwkv_gn_w, m_rwkv_gn_b, m_mla_q_norm_g, m_mla_w_uq, m_mla_kv_norm_g, m_mla_w_ukv, m_w_out, m_ffn_norm_g, m_ffn_w_gate, m_ffn_w_up, m_ffn_conv_w, m_ffn_conv_b, m_ffn_w_down, m_final_norm_g, v_attn_norm_g, v_w_in, v_rwkv_mu, v_rwkv_w0, v_rwkv_w2, v_rwkv_a0, v_rwkv_a2, v_rwkv_g2, v_rwkv_k_k, v_rwkv_k_a, v_rwkv_r_k, v_rwkv_gn_w, v_rwkv_gn_b, v_mla_q_norm_g, v_mla_w_uq, v_mla_kv_norm_g, v_mla_w_ukv, v_w_out, v_ffn_norm_g, v_ffn_w_gate, v_ffn_w_up, v_ffn_conv_w, v_ffn_conv_b, v_ffn_w_down, v_final_norm_g):
    given = dict(locals())
    wts = {n: given[n] for n in WEIGHTS}
    mom_m = {n: given["m_" + n] for n in WEIGHTS}
    mom_v = {n: given["v_" + n] for n in WEIGHTS}
    out_shapes = {n: wts[n].shape for n in WEIGHTS}

    def local2d(n, a):
        if n == 'rwkv_r_k' or a.ndim <= 2:
            return a.reshape(1, -1)
        return a.reshape(a.shape[1:])

    w2d = {n: local2d(n, wts[n]) for n in WEIGHTS}
    m2d = {n: local2d(n, mom_m[n]) for n in WEIGHTS}
    v2d = {n: local2d(n, mom_v[n]) for n in WEIGHTS}

    xs = x.reshape(x.shape[1:])
    tgt = loss_target.reshape(loss_target.shape[1:])
    s, d = xs.shape
    c_rwkv = w2d['rwkv_w0'].shape[1]
    n_rh = c_rwkv // RWKV_HEAD
    decay_lora, aaa_lora, gate_lora = w2d['rwkv_w2'].shape[0], w2d['rwkv_a2'].shape[0], w2d['rwkv_g2'].shape[0]
    q_lora, kv_lora = w2d['mla_q_norm_g'].shape[1], w2d['mla_kv_norm_g'].shape[1]
    shift_dim = w2d['rwkv_mu'].shape[1]
    d_in = w2d['w_in'].shape[1] * N_DEV
    d_in_pad = _round_up(d_in, LANES)
    n_mh = w2d['mla_w_uq'].shape[1] * N_DEV // (QK_NOPE + QK_ROPE)
    d_ff = w2d['ffn_conv_b'].shape[1]
    tm = _pick(s, 256, SUBLANES)
    tm_wide = _pick(s, 128, SUBLANES)

    nb = {n: w2d[n].shape[1] for n in BIG if BIG[n] == 'col'}
    nbp = {n: _round_up(v_, LANES) for n, v_ in nb.items()}
    shards = {}
    for n in BIG:
        w = w2d[n].astype(BF16)
        if BIG[n] == 'col':
            w = jnp.pad(w, ((0, 0), (0, nbp[n] - nb[n])))
        elif n == 'ffn_w_down':
            w = jnp.pad(w, ((0, nbp['ffn_w_gate'] - w.shape[0]), (0, 0)))
        shards[n] = w

    def as_used(n, g):
        return g if BIG[n] == 'col' else g.reshape(N_DEV * g.shape[1], g.shape[2])

    gathered = {'w_in': as_used('w_in', all_gather("gather_w_in", shards['w_in']))}
    later = [n for n in BIG if n != 'w_in']
    f_pad = N_DEV * nbp['ffn_w_gate']
    small_pack = Pack([(n, w2d[n].shape) for n in SMALL_SHARDED], 8)
    small_all = all_gather("gather_small", small_pack.pack(w2d, F32))
    full = {}
    for n, g in small_pack.unpack(small_all, lead=(N_DEV,)).items():
        full[n] = _gathered_to_full(g, SMALL_SHARDED[n])
    conv_w_pad = pad_cols(full['ffn_conv_w'], nb['ffn_w_gate'], nbp['ffn_w_gate'])
    conv_b_pad = pad_cols(w2d['ffn_conv_b'], nb['ffn_w_gate'], nbp['ffn_w_gate'])

    (h1,) = rowwise("rms_attn", _rms_fn, [xs, w2d['attn_norm_g']], ['row', 'const'], [('row', d, BF16)], heads=1, s=s, tm=tm)
    proj = unpad_cols(mm_sh("proj_in", h1, gathered['w_in']), nb['w_in'], nbp['w_in'])
    p_rwkv = proj[:, :shift_dim]
    c_q = proj[:, shift_dim:shift_dim + q_lora]
    c_kv = proj[:, shift_dim + q_lora:shift_dim + q_lora + kv_lora]
    k_pe = proj[:, shift_dim + q_lora + kv_lora:d_in]
    shifted = token_shift_fwd(p_rwkv, w2d['rwkv_mu'], tm_wide)
    o1, o2, o3 = c_rwkv, 2 * c_rwkv, 3 * c_rwkv
    hr = _to_heads(shifted[:, :o1], RWKV_HEAD)
    hk = _to_heads(shifted[:, o1:o2], RWKV_HEAD)
    hv = _to_heads(shifted[:, o2:o3], RWKV_HEAD)
    hw = shifted[:, o3:o3 + decay_lora]
    ha = shifted[:, o3 + decay_lora:o3 + decay_lora + aaa_lora]
    hg = shifted[:, o3 + decay_lora + aaa_lora:]

    def per_head(vec):
        return vec.reshape(n_rh, 1, RWKV_HEAD)

    def lora_heads(w):
        return jnp.transpose(w.reshape(w.shape[0], n_rh, RWKV_HEAD), (1, 0, 2))

    pre_args = [hk, hw, ha, hg, per_head(w2d['rwkv_w0']), lora_heads(full['rwkv_w2']), per_head(w2d['rwkv_a0']),
                lora_heads(full['rwkv_a2']), lora_heads(full['rwkv_g2']), per_head(w2d['rwkv_k_k']), per_head(w2d['rwkv_k_a'])]
    pre_kinds = ['hrow', 'row', 'row', 'row', 'hconst', 'hconst', 'hconst', 'hconst', 'hconst', 'hconst', 'hconst']
    decay, kx, a_sc, b_sc, gate_r = rowwise("rwkv_pre", _rwkv_pre_fn, pre_args, pre_kinds,
                                            [('hrow', RWKV_HEAD, F32)] * 5, heads=n_rh, s=s, tm=tm)
    y_scan, ckpt, *landed = scan_fwd(hr, decay, kx, hv, a_sc, b_sc, comm=GatherIci([shards[n] for n in later]))
    for n, g in zip(later, gather_d2d("gather_d2d", landed), strict=True):
        gathered[n] = as_used(n, g)
    post_args = [y_scan, hr, kx, hv, gate_r, per_head(w2d['rwkv_gn_w']), per_head(w2d['rwkv_gn_b']), per_head(w2d['rwkv_r_k'])]
    post_kinds = ['hrow'] * 5 + ['hconst'] * 3
    (y_rwkv_h,) = rowwise("rwkv_post", _rwkv_post_fn, post_args, post_kinds, [('hrow', RWKV_HEAD, F32)], heads=n_rh, s=s, tm=tm)

    pos = positions.reshape(s, 1).astype(F32)
    rot, inv2 = _rot_matrix(), _inv_freq2()
    mla_args = [c_q, c_kv, k_pe, pos, w2d['mla_q_norm_g'], w2d['mla_kv_norm_g'], inv2, rot]
    mla_kinds = ['row', 'row', 'row', 'row', 'const', 'const', 'const', 'const']
    qn, kvn, kp_rot, cos2, sin2 = rowwise(
        "mla_pre", _mla_pre_fn, mla_args, mla_kinds,
        [('row', q_lora, BF16), ('row', kv_lora, BF16), ('row', QK_ROPE, F32), ('row', QK_ROPE, F32), ('row', QK_ROPE, F32)],
        heads=1, s=s, tm=tm)
    q = unpad_cols(mm_sh("proj_q", qn, gathered['mla_w_uq']), nb['mla_w_uq'], nbp['mla_w_uq'])
    kv = unpad_cols(mm_sh("proj_kv", kvn, gathered['mla_w_ukv']), nb['mla_w_ukv'], nbp['mla_w_ukv'])
    q_h = _to_heads(q, QK_NOPE + QK_ROPE)
    kv_h = _to_heads(kv, QK_NOPE + V_HEAD)
    q_nope, q_pe = q_h[..., :QK_NOPE], q_h[..., QK_NOPE:]
    k_nope, v_att = kv_h[..., :QK_NOPE], kv_h[..., QK_NOPE:]
    ropeq_args = [q_pe, cos2, sin2, rot]
    ropeq_kinds = ['hrow', 'row', 'row', 'const']
    (q_pe_rot,) = rowwise("rope_q", _rope_q_fn, ropeq_args, ropeq_kinds, [('hrow', QK_ROPE, F32)], heads=n_mh, s=s, tm=tm)
    o_att, lse = attn_fwd(q_nope, q_pe_rot, k_nope, kp_rot, v_att)
    ycat = jnp.concatenate([_from_heads(y_rwkv_h), _from_heads(o_att)], axis=-1).astype(BF16)
    x1 = mm("proj_out", ycat, gathered['w_out'], add=xs)
    (h2,) = rowwise("rms_ffn", _rms_fn, [x1, w2d['ffn_norm_g']], ['row', 'const'], [('row', d, BF16)], heads=1, s=s, tm=tm)
    gate_pre = mm_sh("ffn_gate", h2, gathered['ffn_w_gate'])
    up = mm_sh("ffn_up", h2, gathered['ffn_w_up'])
    act = ffn_act_fwd(gate_pre, up, conv_w_pad, conv_b_pad)
    x2 = mm("ffn_down", act, gathered['ffn_w_down'], add=x1)

    ones = jnp.ones((s, 1), F32)
    fin_g = w2d['final_norm_g']
    d_x2, dg_final_p, loss_rows = rowwise_vjp("loss_bwd", _loss_fn, [x2, fin_g, tgt], ['row', 'const', 'row'], [ones], ['row'],
                                              [0, 1], heads=1, s=s, tm=tm, primal=True)
    d_x2_b = d_x2.astype(BF16)
    d_act = mm_nt("d_act", d_x2_b, gathered['ffn_w_down'], out_dtype=BF16)
    gsh = {}
    gsh['ffn_w_down'] = mm("dw_down", act.T, d_x2_b).reshape(N_DEV, nbp['ffn_w_gate'], d)
    d_gate, d_up, dcw_p, dcb_p = ffn_act_bwd1(gate_pre, up, conv_w_pad, conv_b_pad, d_act)
    d_gp = ffn_act_bwd2(d_gate, conv_w_pad)
    d_h2 = mm_sh_nt("d_h2_up", d_up, gathered['ffn_w_up'], add=mm_sh_nt("d_h2_gate", d_gp, gathered['ffn_w_gate']))
    h2_t = h2.T
    gsh['ffn_w_gate'] = mm_sh_out("dw_gate", h2_t, d_gp)
    gsh['ffn_w_up'] = mm_sh_out("dw_up", h2_t, d_up)
    d_x1n, dg_ffn_p = rowwise_vjp("rms_ffn_bwd", _rms_fn, [x1, w2d['ffn_norm_g']], ['row', 'const'], [d_h2], ['row'], [0, 1],
                                  heads=1, s=s, tm=tm)
    d_x1 = add_slots("d_x1_add", d_x1n[None], d_x2[None])[0]
    d_x1_b = d_x1.astype(BF16)
    d_ycat = mm_nt("d_ycat", d_x1_b, gathered['w_out'])
    gsh['w_out'] = mm("dw_out", ycat.T, d_x1_b).reshape(N_DEV, d // N_DEV, d)
    d_yr_h = _to_heads(d_ycat[:, :c_rwkv], RWKV_HEAD)
    d_o_h = _to_heads(d_ycat[:, c_rwkv:], V_HEAD)

    d_qn_h, d_qpr_h, d_kn_h, d_v_h, d_kp_h = attn_bwd(q_nope, q_pe_rot, k_nope, kp_rot, v_att, o_att, lse, d_o_h)
    (d_qp_h,) = rowwise_vjp("rope_q_bwd", _rope_q_fn, ropeq_args, ropeq_kinds, [d_qpr_h], ['hrow'], [0], heads=n_mh, s=s, tm=tm)
    d_q = pad_cols(_from_heads(jnp.concatenate([d_qn_h, d_qp_h], axis=-1)).astype(BF16), nb['mla_w_uq'], nbp['mla_w_uq'])
    d_kv = pad_cols(_from_heads(jnp.concatenate([d_kn_h, d_v_h], axis=-1)).astype(BF16), nb['mla_w_ukv'], nbp['mla_w_ukv'])
    d_kp_rot = colsum("d_kpe_heads", d_kp_h.reshape(n_mh, s * QK_ROPE)).reshape(s, QK_ROPE)
    d_qn = mm_sh_nt("d_qn", d_q, gathered['mla_w_uq'])
    d_kvn = mm_sh_nt("d_kvn", d_kv, gathered['mla_w_ukv'])
    gsh['mla_w_uq'] = mm_sh_out("dw_uq", qn.T, d_q)
    gsh['mla_w_ukv'] = mm_sh_out("dw_ukv", kvn.T, d_kv)
    d_cq, d_ckv, d_kpe, dg_q_p, dg_kv_p = rowwise_vjp(
        "mla_pre_bwd", _mla_pre_grad_fn, mla_args, mla_kinds, [d_qn, d_kvn, d_kp_rot], ['row', 'row', 'row'], [0, 1, 2, 4, 5],
        heads=1, s=s, tm=tm)

    d_y, d_r_post, d_k_post, d_v_post, d_gate_r, dgnw_p, dgnb_p, drk_p = rowwise_vjp(
        "rwkv_post_bwd", _rwkv_post_fn, post_args, post_kinds, [d_yr_h], ['hrow'], list(range(8)), heads=n_rh, s=s, tm=tm)
    chip_sums = {n: rs_chip_sum(n, gsh[n]) for n in later}
    d_r_sc, d_w_sc, d_k_sc, d_v_sc, d_a_sc, d_b_sc, *arrived = scan_bwd(
        hr, decay, kx, hv, a_sc, b_sc, ckpt, d_y, comm=RsChips([chip_sums[n][1] for n in later]))
    from_chips = dict(zip(later, arrived, strict=True))
    d_k_tot = add_slots("rwkv_dk_add", d_k_sc, d_k_post)
    d_hk, d_hw_p, d_ha_p, d_hg_p, dw0_p, dw2_p, da0_p, da2_p, dg2_p, dkk_p, dka_p = rowwise_vjp(
        "rwkv_pre_bwd", _rwkv_pre_fn, pre_args, pre_kinds, [d_w_sc, d_k_tot, d_a_sc, d_b_sc, d_gate_r], ['hrow'] * 5,
        list(range(11)), heads=n_rh, s=s, tm=tm)
    d_hr = add_slots("rwkv_dr_add", d_r_sc, d_r_post)
    d_hv = add_slots("rwkv_dv_add", d_v_sc, d_v_post)
    lora_w = decay_lora + aaa_lora + gate_lora
    d_lora = colsum("d_lora_heads", jnp.concatenate([d_hw_p, d_ha_p, d_hg_p], axis=-1).reshape(n_rh, s * lora_w)).reshape(s, lora_w)
    d_shifted = jnp.concatenate([_from_heads(d_hr), _from_heads(d_hk), _from_heads(d_hv), d_lora], axis=-1)
    d_p_rwkv, dmu_p = token_shift_bwd(p_rwkv, w2d['rwkv_mu'], d_shifted, tm_wide)
    d_proj = pad_cols(jnp.concatenate([d_p_rwkv, d_cq, d_ckv, d_kpe], axis=-1).astype(BF16), nb['w_in'], nbp['w_in'])
    gsh['w_in'] = mm_sh_out("dw_in", h1.T, d_proj)
    chip_sums['w_in'] = rs_chip_sum('w_in', gsh['w_in'])
    d_h1, from_chips['w_in'] = mm_sh_nt("d_h1", d_proj, gathered['w_in'], comm=RsChips([chip_sums['w_in'][1]]))
    d_xn, dg_attn_p = rowwise_vjp("rms_attn_bwd", _rms_fn, [xs, w2d['attn_norm_g']], ['row', 'const'], [d_h1], ['row'], [0, 1],
                                  heads=1, s=s, tm=tm)
    grad_x = add_slots("grad_x_add", d_xn[None], d_x1[None])[0]

    def from_heads_lora(g):
        return jnp.transpose(g, (1, 0, 2)).reshape(g.shape[1], n_rh * RWKV_HEAD)

    gw = {}
    gw['rwkv_w2'] = from_heads_lora(sum_partials("sum_dw2", dw2_p, True))
    gw['rwkv_a2'] = from_heads_lora(sum_partials("sum_da2", da2_p, True))
    gw['rwkv_g2'] = from_heads_lora(sum_partials("sum_dg2", dg2_p, True))
    dcw_pad = colsum("sum_dconv_w", dcw_p.reshape(dcw_p.shape[0], CONV_W * f_pad)).reshape(CONV_W, f_pad)
    gw['ffn_conv_w'] = unpad_cols(dcw_pad, nb['ffn_w_gate'], nbp['ffn_w_gate'])

    rep = {
        'attn_norm_g': sum_partials("sum_dg_attn", dg_attn_p, False),
        'rwkv_mu': colsum("sum_dmu", dmu_p.reshape(dmu_p.shape[0], shift_dim)),
        'rwkv_w0': sum_partials("sum_dw0", dw0_p, True).reshape(1, c_rwkv),
        'rwkv_a0': sum_partials("sum_da0", da0_p, True).reshape(1, c_rwkv),
        'rwkv_k_k': sum_partials("sum_dkk", dkk_p, True).reshape(1, c_rwkv),
        'rwkv_k_a': sum_partials("sum_dka", dka_p, True).reshape(1, c_rwkv),
        'rwkv_r_k': sum_partials("sum_drk", drk_p, True).reshape(1, c_rwkv),
        'rwkv_gn_w': sum_partials("sum_dgnw", dgnw_p, True).reshape(1, c_rwkv),
        'rwkv_gn_b': sum_partials("sum_dgnb", dgnb_p, True).reshape(1, c_rwkv),
        'mla_q_norm_g': sum_partials("sum_dg_q", dg_q_p, False),
        'mla_kv_norm_g': sum_partials("sum_dg_kv", dg_kv_p, False),
        'ffn_norm_g': sum_partials("sum_dg_ffn", dg_ffn_p, False),
        'ffn_conv_b': unpad_cols(colsum("sum_dconv_b", dcb_p.reshape(dcb_p.shape[0], f_pad)), nb['ffn_w_gate'], nbp['ffn_w_gate']),
        'final_norm_g': sum_partials("sum_dg_final", dg_final_p, False),
        'loss': sum_all("sum_loss", loss_rows.reshape(s // SUBLANES, SUBLANES)),
    }
    rep_pack = Pack([(n, w2d[n].shape) for n in REPLICATED] + [('loss', (1, 1))], 8)
    rep_all = all_gather("gather_rep_grads", rep_pack.pack(rep, F32))
    rep_sum = colsum("sum_rep_grads", rep_all.reshape(N_DEV, rep_pack.total)).reshape(rep_pack.rows, PACK_W)
    rep_g = rep_pack.unpack(rep_sum)
    loss = rep_g.pop('loss').reshape(())

    grads, deltas, new_m, new_v = dict(rep_g), {}, {}, {}
    for n in BIG:
        a, b = w2d[n].shape
        grads[n] = rs_finish(n, chip_sums[n][0], from_chips[n])[:a, :b]
        deltas[n], new_m[n], new_v[n] = rowwise(
            "adamw_" + n, _adamw_fn, [w2d[n], grads[n], m2d[n], v2d[n]], ['row'] * 4, [('row', b, F32)] * 3,
            heads=1, s=a, tm=_pick(a, 256, SUBLANES))
    sm_pack = Pack([(n, w2d[n].shape) for n in SMALL_SHARDED], 8)
    g_shards = {n: _full_to_shards(gw[n], SMALL_SHARDED[n]) for n in SMALL_SHARDED}
    grads.update(sm_pack.unpack(reduce_scatter("small", sm_pack.pack(g_shards, F32, lead=(N_DEV,)))))
    rest_pack = Pack([(n, w2d[n].shape) for n in WEIGHTS if n not in BIG], 8)
    d_r, m_r, v_r = rowwise(
        "adamw_small", _adamw_fn, [rest_pack.pack(w2d, F32), rest_pack.pack(grads, F32), rest_pack.pack(m2d, F32),
                                   rest_pack.pack(v2d, F32)],
        ['row'] * 4, [('row', PACK_W, F32)] * 3, heads=1, s=rest_pack.rows, tm=_pick(rest_pack.rows, 512, SUBLANES))
    deltas.update(rest_pack.unpack(d_r))
    new_m.update(rest_pack.unpack(m_r))
    new_v.update(rest_pack.unpack(v_r))

    def shaped(dct):
        return [dct[n].reshape(out_shapes[n]) for n in WEIGHTS]

    return (loss, grad_x.reshape(x.shape), *shaped(grads), *shaped(deltas), *shaped(new_m), *shaped(new_v))
```

```python
import functools
import math

import jax
import jax.numpy as jnp
import numpy as np
from jax import lax
from jax.experimental import pallas as pl
from jax.experimental.pallas import tpu as pltpu

F32 = jnp.float32
BF16 = jnp.bfloat16
HIGHEST = lax.Precision.HIGHEST
MESH = pl.DeviceIdType.MESH

N_DEV = 8
LANES = 128
SUBLANES = 8
VMEM_LIMIT = 48 * 1024 * 1024
RESIDENT_BYTES = 8 * 1024 * 1024

NORM_EPS = 1e-6
GN_EPS = 64e-5
RWKV_HEAD = 64
QK_NOPE = 128
QK_ROPE = 64
V_HEAD = 128
ROPE_THETA = 10000.0
CONV_W = 3
NEG_INF = -1e30
SCAN_CHUNK = 64
SCAN_HEADS = 8
SCAN_PASSES_SOLVE = 1
SCAN_PASSES_OUT = 1

ADAM_LR = 0.001
ADAM_B1 = 0.9
ADAM_B2 = 0.999
ADAM_EPS = 1e-08
ADAM_WD = 0.01
ADAM_STEP = 10

WEIGHTS = ['attn_norm_g', 'w_in', 'rwkv_mu', 'rwkv_w0', 'rwkv_w2', 'rwkv_a0', 'rwkv_a2', 'rwkv_g2', 'rwkv_k_k',
           'rwkv_k_a', 'rwkv_r_k', 'rwkv_gn_w', 'rwkv_gn_b', 'mla_q_norm_g', 'mla_w_uq', 'mla_kv_norm_g', 'mla_w_ukv',
           'w_out', 'ffn_norm_g', 'ffn_w_gate', 'ffn_w_up', 'ffn_conv_w', 'ffn_conv_b', 'ffn_w_down', 'final_norm_g']
BIG = {'w_in': 'col', 'mla_w_uq': 'col', 'mla_w_ukv': 'col', 'w_out': 'row', 'ffn_w_gate': 'col', 'ffn_w_up': 'col',
       'ffn_w_down': 'row'}
SMALL_SHARDED = {'rwkv_w2': 'col', 'rwkv_a2': 'col', 'rwkv_g2': 'col', 'ffn_conv_w': 'col'}
SHARDED = {**BIG, **SMALL_SHARDED}
REPLICATED = [n for n in WEIGHTS if n not in SHARDED]


def _round_up(n, m):
    return (n + m - 1) // m * m


def _pick(n, cap, unit):
    if n <= cap:
        return n
    best = None
    for t in range(unit, cap + 1, unit):
        if n % t == 0:
            best = t
    assert best is not None, (n, cap, unit)
    return best


def _params(sem):
    return pltpu.CompilerParams(dimension_semantics=sem, vmem_limit_bytes=VMEM_LIMIT)


def mm(name, a, b, add=None, out_dtype=F32):
    m, k = a.shape
    k2, n = b.shape
    assert k == k2, (name, a.shape, b.shape)
    tm = _pick(m, 512, SUBLANES * 2)
    tn = n if k * n * 2 <= RESIDENT_BYTES else _pick(n, 640, LANES)
    has_add = add is not None

    def body(a_ref, b_ref, *rest):
        o_ref = rest[-1]
        acc = jnp.dot(a_ref[...].astype(BF16), b_ref[...].astype(BF16), preferred_element_type=F32)
        if has_add:
            acc = acc + rest[0][...].astype(F32)
        o_ref[...] = acc.astype(o_ref.dtype)

    in_specs = [pl.BlockSpec((tm, k), lambda i, j: (i, 0)), pl.BlockSpec((k, tn), lambda i, j: (0, j))]
    ops = [a, b]
    if has_add:
        in_specs.append(pl.BlockSpec((tm, tn), lambda i, j: (i, j)))
        ops.append(add)
    return pl.pallas_call(
        body, name=name, grid=(m // tm, n // tn), in_specs=in_specs,
        out_specs=pl.BlockSpec((tm, tn), lambda i, j: (i, j)),
        out_shape=jax.ShapeDtypeStruct((m, n), out_dtype),
        compiler_params=_params(("parallel", "parallel")),
    )(*ops)


def mm_nt(name, a, b, out_dtype=F32):
    m, k = a.shape
    n, k2 = b.shape
    assert k == k2, (name, a.shape, b.shape)
    tm = _pick(m, 2048 if m * k * 2 <= RESIDENT_BYTES else 512, SUBLANES * 2)
    tn = _pick(n, 1024, LANES)

    def body(a_ref, b_ref, o_ref):
        acc = lax.dot_general(a_ref[...].astype(BF16), b_ref[...].astype(BF16), (((1,), (1,)), ((), ())),
                              preferred_element_type=F32)
        o_ref[...] = acc.astype(o_ref.dtype)

    return pl.pallas_call(
        body, name=name, grid=(m // tm, n // tn),
        in_specs=[pl.BlockSpec((tm, k), lambda i, j: (i, 0)), pl.BlockSpec((tn, k), lambda i, j: (j, 0))],
        out_specs=pl.BlockSpec((tm, tn), lambda i, j: (i, j)),
        out_shape=jax.ShapeDtypeStruct((m, n), out_dtype),
        compiler_params=_params(("parallel", "parallel")),
    )(a, b)


def mm_sh(name, a, g, out_dtype=F32):
    m, k = a.shape
    nd, k2, nbp = g.shape
    assert k == k2, (name, a.shape, g.shape)
    tm = _pick(m, 2048 if m * k * 2 <= RESIDENT_BYTES else 512, SUBLANES * 2)

    def body(a_ref, b_ref, o_ref):
        o_ref[...] = jnp.dot(a_ref[...].astype(BF16), b_ref[...].astype(BF16), preferred_element_type=F32).astype(o_ref.dtype)

    return pl.pallas_call(
        body, name=name, grid=(m // tm, nd),
        in_specs=[pl.BlockSpec((tm, k), lambda i, j: (i, 0)), pl.BlockSpec((None, k, nbp), lambda i, j: (j, 0, 0))],
        out_specs=pl.BlockSpec((tm, nbp), lambda i, j: (i, j)),
        out_shape=jax.ShapeDtypeStruct((m, nd * nbp), out_dtype),
        compiler_params=_params(("parallel", "parallel")),
    )(a, g)


def mm_sh_nt(name, a, g, add=None, comm=None):
    m, n = a.shape
    nd, k, nbp = g.shape
    assert n == nd * nbp, (name, a.shape, g.shape)
    tm = _pick(m, 512, SUBLANES * 2)
    has_add = add is not None

    def body(a_ref, b_ref, *rest):
        o_ref = rest[-1]
        part = lax.dot_general(a_ref[...].astype(BF16), b_ref[...].astype(BF16), (((1,), (1,)), ((), ())),
                               preferred_element_type=F32)

        @pl.when(pl.program_id(1) == 0)
        def _():
            o_ref[...] = part + rest[0][...] if has_add else part

        @pl.when(pl.program_id(1) != 0)
        def _():
            o_ref[...] += part

    in_specs = [pl.BlockSpec((tm, nbp), lambda i, j: (i, j)), pl.BlockSpec((None, k, nbp), lambda i, j: (j, 0, 0))]
    ops = [a, g]
    if has_add:
        in_specs.append(pl.BlockSpec((tm, k), lambda i, j: (i, 0)))
        ops.append(add)
    res = _pallas(
        body, name=name, grid=(m // tm, nd), in_specs=in_specs,
        out_specs=[pl.BlockSpec((tm, k), lambda i, j: (i, 0))],
        out_shape=[jax.ShapeDtypeStruct((m, k), F32)], sem=("parallel", "arbitrary"), comm=comm,
    )(*ops)
    return res[0] if comm is None else res


def mm_sh_out(name, at, b):
    k, m = at.shape
    m2, n = b.shape
    assert m == m2 and n % N_DEV == 0, (name, at.shape, b.shape)
    nbp = n // N_DEV
    tk = _pick(k, 2048 if k * m * 2 <= RESIDENT_BYTES else 512, SUBLANES * 2)

    def body(a_ref, b_ref, o_ref):
        o_ref[...] = jnp.dot(a_ref[...].astype(BF16), b_ref[...].astype(BF16), preferred_element_type=F32)

    return pl.pallas_call(
        body, name=name, grid=(k // tk, N_DEV),
        in_specs=[pl.BlockSpec((tk, m), lambda i, j: (i, 0)), pl.BlockSpec((m, nbp), lambda i, j: (0, j))],
        out_specs=pl.BlockSpec((None, tk, nbp), lambda i, j: (j, i, 0)),
        out_shape=jax.ShapeDtypeStruct((N_DEV, k, nbp), F32),
        compiler_params=_params(("parallel", "parallel")),
    )(at, b)


def pad_cols(y, nb, nbp):
    m = y.shape[0]
    if nb == nbp:
        return y
    return jnp.pad(y.reshape(m, N_DEV, nb), ((0, 0), (0, 0), (0, nbp - nb))).reshape(m, N_DEV * nbp)


def unpad_cols(y, nb, nbp):
    m = y.shape[0]
    if nb == nbp:
        return y
    return y.reshape(m, N_DEV, nbp)[:, :, :nb].reshape(m, N_DEV * nb)


def _in_spec(kind, a, tm):
    if kind == 'row':
        return pl.BlockSpec((tm, a.shape[1]), lambda h, i: (i, 0))
    if kind == 'hrow':
        return pl.BlockSpec((None, tm, a.shape[2]), lambda h, i: (h, i, 0))
    if kind == 'const':
        return pl.BlockSpec(a.shape, lambda h, i: (0, 0))
    assert kind == 'hconst', kind
    return pl.BlockSpec((None,) + a.shape[1:], lambda h, i: (h, 0, 0))


def _row_out(kind, c, dtype, heads, s, tm):
    if kind == 'row':
        assert heads == 1
        return jax.ShapeDtypeStruct((s, c), dtype), pl.BlockSpec((tm, c), lambda h, i: (i, 0))
    return jax.ShapeDtypeStruct((heads, s, c), dtype), pl.BlockSpec((None, tm, c), lambda h, i: (h, i, 0))


def rowwise(name, fn, arrs, kinds, outs, *, heads, s, tm):
    n_in = len(arrs)

    def body(*refs):
        vals = fn(*[r[...] for r in refs[:n_in]])
        for o, v in zip(refs[n_in:], vals, strict=True):
            o[...] = v.astype(o.dtype)

    shapes, specs = zip(*[_row_out(k, c, dt, heads, s, tm) for k, c, dt in outs])
    return pl.pallas_call(
        body, name=name, grid=(heads, s // tm),
        in_specs=[_in_spec(k, a, tm) for k, a in zip(kinds, arrs, strict=True)],
        out_specs=list(specs), out_shape=list(shapes),
        compiler_params=_params(("parallel", "parallel")),
    )(*arrs)


def rowwise_vjp(name, fn, arrs, kinds, cots, cot_kinds, wrt, *, heads, s, tm, out_dtypes=None, primal=False):
    n_in, n_cot = len(arrs), len(cots)
    nb = s // tm
    out_dtypes = out_dtypes or [F32] * len(wrt)

    def body(*refs):
        vals = [r[...] for r in refs[:n_in]]
        cvals = tuple(r[...].astype(F32) for r in refs[n_in:n_in + n_cot])
        outs = refs[n_in + n_cot:]

        def f(*dv):
            full = list(vals)
            for j, i in enumerate(wrt):
                full[i] = dv[j]
            return tuple(fn(*full))

        prim, vjp_fn = jax.vjp(f, *[vals[i].astype(F32) for i in wrt])
        grads = vjp_fn(cvals)
        for o, g in zip(outs[:len(wrt)], grads, strict=True):
            o[...] = g.astype(o.dtype)
        if primal:
            for o, p in zip(outs[len(wrt):], prim, strict=True):
                o[...] = p.astype(o.dtype)

    shapes, specs = [], []
    for i, dt in zip(wrt, out_dtypes, strict=True):
        kind, a = kinds[i], arrs[i]
        if kind in ('row', 'hrow'):
            c = a.shape[-1]
            sh, sp = _row_out('row' if (kind == 'row' and heads == 1) else 'hrow', c, dt, heads, s, tm)
        else:
            r, c = a.shape[-2:]
            sh = jax.ShapeDtypeStruct((heads, nb, r, c), dt)
            sp = pl.BlockSpec((None, None, r, c), lambda h, i: (h, i, 0, 0))
        shapes.append(sh)
        specs.append(sp)
    if primal:
        for ck, c in zip(cot_kinds, cots, strict=True):
            sh, sp = _row_out(ck, c.shape[-1], F32, heads, s, tm)
            shapes.append(sh)
            specs.append(sp)
    in_specs = [_in_spec(k, a, tm) for k, a in zip(kinds, arrs, strict=True)]
    in_specs += [_in_spec(k, a, tm) for k, a in zip(cot_kinds, cots, strict=True)]
    return pl.pallas_call(
        body, name=name, grid=(heads, nb), in_specs=in_specs, out_specs=specs, out_shape=shapes,
        compiler_params=_params(("parallel", "parallel")),
    )(*arrs, *cots)


def colsum(name, x):
    n, m = x.shape
    tc = _pick(m, 32768, LANES) if m % LANES == 0 else m

    def body(x_ref, o_ref):
        acc = x_ref[0:1, :].astype(F32)
        for r in range(1, n):
            acc = acc + x_ref[r:r + 1, :].astype(F32)
        o_ref[...] = acc

    return pl.pallas_call(
        body, name=name, grid=(m // tc,), in_specs=[pl.BlockSpec((n, tc), lambda j: (0, j))],
        out_specs=pl.BlockSpec((1, tc), lambda j: (0, j)), out_shape=jax.ShapeDtypeStruct((1, m), F32),
        compiler_params=_params(("parallel",)),
    )(x)


def headsum(name, x):
    h, s, c = x.shape
    tm = _pick(s, 256, SUBLANES)

    def body(x_ref, o_ref):
        acc = x_ref[0]
        for j in range(1, h):
            acc = acc + x_ref[j]
        o_ref[...] = acc

    return pl.pallas_call(
        body, name=name, grid=(s // tm,), in_specs=[pl.BlockSpec((h, tm, c), lambda i: (0, i, 0))],
        out_specs=pl.BlockSpec((tm, c), lambda i: (i, 0)), out_shape=jax.ShapeDtypeStruct((s, c), F32),
        compiler_params=_params(("parallel",)),
    )(x)


def sum_all(name, x):
    def body(x_ref, o_ref):
        o_ref[...] = jnp.sum(x_ref[...], keepdims=True)

    return pl.pallas_call(body, name=name, out_shape=jax.ShapeDtypeStruct((1, 1), F32))(x)


def sum_partials(name, p, per_head):
    h, nb, r, c = p.shape
    if per_head:
        flat = jnp.transpose(p, (1, 0, 2, 3)).reshape(nb, h * r * c)
        if nb == 1:
            return flat.reshape(h, r, c)
        return colsum(name, flat).reshape(h, r, c)
    flat = p.reshape(h * nb, r * c)
    if h * nb == 1:
        return flat.reshape(r, c)
    return colsum(name, flat).reshape(r, c)


def _rms_fn(x, g):
    xf = x.astype(F32)
    return (xf * lax.rsqrt(jnp.mean(xf * xf, axis=-1, keepdims=True) + NORM_EPS) * g,)


def _softplus(z):
    return jnp.maximum(z, 0.0) + jnp.log(1.0 + jnp.exp(-jnp.abs(z)))


def _rwkv_pre_fn(hk, hw, ha, hg, w0, w2, a0, a2, g2, k_k, k_a):
    zw = w0 + jnp.dot(jnp.tanh(hw), w2, preferred_element_type=F32)
    w_log = -_softplus(-zw) - 0.5
    decay = jnp.exp(-jnp.exp(w_log))
    a = jax.nn.sigmoid(a0 + jnp.dot(ha, a2, preferred_element_type=F32))
    g = jnp.dot(jax.nn.sigmoid(hg), g2, preferred_element_type=F32)
    kk = hk * k_k
    kk = kk * lax.rsqrt(jnp.maximum(jnp.sum(kk * kk, axis=-1, keepdims=True), 1e-24))
    k = hk * (1.0 + (a - 1.0) * k_a)
    return decay, k, -kk, kk * a, g


def _rwkv_pre_grad_fn(hk, hw, ha, hg, w0, w2, a0, a2, g2, k_k, k_a, hr, hv):
    decay, k, a_sc, b_sc, g = _rwkv_pre_fn(hk, hw, ha, hg, w0, w2, a0, a2, g2, k_k, k_a)
    return decay, k, k, a_sc, b_sc, g, hr, hr, hv, hv


def _rwkv_post_fn(y, r, k, v, g, gn_w, gn_b, r_k):
    mu = jnp.mean(y, axis=-1, keepdims=True)
    var = jnp.mean(jnp.square(y - mu), axis=-1, keepdims=True)
    yn = (y - mu) * lax.rsqrt(var + GN_EPS) * gn_w + gn_b
    bonus = jnp.sum(r * k * r_k, axis=-1, keepdims=True) * v
    return ((yn + bonus) * g,)


def _rope_tables(pos, inv_freq2):
    ang = pos * inv_freq2
    return jnp.cos(ang), jnp.sin(ang)


def _rope(t, cos2, sin2, rot):
    return t * cos2 + jnp.dot(t, rot, precision=HIGHEST, preferred_element_type=F32) * sin2


def _mla_pre_fn(c_q, c_kv, k_pe, pos, q_g, kv_g, inv_freq2, rot):
    cos2, sin2 = _rope_tables(pos, inv_freq2)
    return _rms_fn(c_q, q_g)[0], _rms_fn(c_kv, kv_g)[0], _rope(k_pe, cos2, sin2, rot), cos2, sin2


def _mla_pre_grad_fn(c_q, c_kv, k_pe, pos, q_g, kv_g, inv_freq2, rot):
    return _mla_pre_fn(c_q, c_kv, k_pe, pos, q_g, kv_g, inv_freq2, rot)[:3]


def _rope_q_fn(q_pe, cos2, sin2, rot):
    return (_rope(q_pe, cos2, sin2, rot),)


def _loss_fn(x2, g, target):
    y = _rms_fn(x2, g)[0]
    return (0.5 * jnp.mean(jnp.square(y - target), axis=-1, keepdims=True),)


def _adamw_fn(w, g, m, v):
    m = ADAM_B1 * m + (1.0 - ADAM_B1) * g
    v = ADAM_B2 * v + (1.0 - ADAM_B2) * jnp.square(g)
    m_hat = m / (1.0 - ADAM_B1 ** ADAM_STEP)
    v_hat = v / (1.0 - ADAM_B2 ** ADAM_STEP)
    delta = -ADAM_LR * (m_hat / (jnp.sqrt(v_hat) + ADAM_EPS) + ADAM_WD * w)
    return delta, m, v


def _prev_halo_spec(c, tm):
    return pl.BlockSpec((SUBLANES, c), lambda i: (jnp.maximum(i * (tm // SUBLANES) - 1, 0), 0))


def _next_halo_spec(c, tm, s):
    return pl.BlockSpec((SUBLANES, c), lambda i: (jnp.minimum((i + 1) * (tm // SUBLANES), s // SUBLANES - 1), 0))


def _shift_down(p, halo, first_block, n):
    out = pltpu.roll(p, n, 0)
    row = lax.broadcasted_iota(jnp.int32, p.shape, 0)
    for j in range(n):
        top = jnp.where(first_block, 0.0, halo[SUBLANES - n + j:SUBLANES - n + j + 1, :])
        out = jnp.where(row == j, top, out)
    return out


def _shift_up(p, halo, last_block, n):
    rows = p.shape[0]
    out = pltpu.roll(p, rows - n, 0)
    row = lax.broadcasted_iota(jnp.int32, p.shape, 0)
    for j in range(n):
        bot = jnp.where(last_block, 0.0, halo[j:j + 1, :])
        out = jnp.where(row == rows - n + j, bot, out)
    return out


def token_shift_fwd(p, mu, tm):
    s, c = p.shape

    def body(p_ref, halo_ref, mu_ref, o_ref):
        pv = p_ref[...]
        prev = _shift_down(pv, halo_ref[...], pl.program_id(0) == 0, 1)
        o_ref[...] = pv + (prev - pv) * mu_ref[...]

    return pl.pallas_call(
        body, name="token_shift_fwd", grid=(s // tm,),
        in_specs=[pl.BlockSpec((tm, c), lambda i: (i, 0)), _prev_halo_spec(c, tm), pl.BlockSpec((1, c), lambda i: (0, 0))],
        out_specs=pl.BlockSpec((tm, c), lambda i: (i, 0)), out_shape=jax.ShapeDtypeStruct((s, c), F32),
        compiler_params=_params(("parallel",)),
    )(p, p, mu)


def token_shift_bwd(p, mu, ds, tm):
    s, c = p.shape
    nb = s // tm

    def body(p_ref, halo_ref, mu_ref, ds_ref, dsn_ref, dp_ref, dmu_ref):
        i = pl.program_id(0)
        pv, dsv, muv = p_ref[...], ds_ref[...], mu_ref[...]
        prev = _shift_down(pv, halo_ref[...], i == 0, 1)
        nxt = _shift_up(dsv, dsn_ref[...], i == nb - 1, 1)
        dp_ref[...] = dsv * (1.0 - muv) + nxt * muv
        dmu_ref[...] = jnp.sum(dsv * (prev - pv), axis=0, keepdims=True)

    return pl.pallas_call(
        body, name="token_shift_bwd", grid=(nb,),
        in_specs=[pl.BlockSpec((tm, c), lambda i: (i, 0)), _prev_halo_spec(c, tm), pl.BlockSpec((1, c), lambda i: (0, 0)),
                  pl.BlockSpec((tm, c), lambda i: (i, 0)), _next_halo_spec(c, tm, s)],
        out_specs=[pl.BlockSpec((tm, c), lambda i: (i, 0)), pl.BlockSpec((None, 1, c), lambda i: (i, 0, 0))],
        out_shape=[jax.ShapeDtypeStruct((s, c), F32), jax.ShapeDtypeStruct((nb, 1, c), F32)],
        compiler_params=_params(("parallel",)),
    )(p, p, mu, ds, ds)


def _ffn_tiles(s, f):
    return _pick(s, 256, SUBLANES), _pick(f, 1408, LANES)


def _conv_gate(gp, halo, first_block, cw, cb):
    p1 = _shift_down(gp, halo, first_block, 1)
    p2 = _shift_down(gp, halo, first_block, 2)
    return cw[0:1, :] * p2 + cw[1:2, :] * p1 + cw[2:3, :] * gp + cb, p1, p2


def ffn_act_fwd(gate_pre, up, conv_w, conv_b):
    s, f = gate_pre.shape
    tm, tc = _ffn_tiles(s, f)

    def body(gp_ref, halo_ref, up_ref, cw_ref, cb_ref, o_ref):
        gate, _, _ = _conv_gate(gp_ref[...], halo_ref[...], pl.program_id(0) == 0, cw_ref[...], cb_ref[...])
        o_ref[...] = (gate * jax.nn.sigmoid(gate) * up_ref[...]).astype(o_ref.dtype)

    blk = pl.BlockSpec((tm, tc), lambda i, j: (i, j))
    return pl.pallas_call(
        body, name="ffn_act_fwd", grid=(s // tm, f // tc),
        in_specs=[blk, pl.BlockSpec((SUBLANES, tc), lambda i, j: (jnp.maximum(i * (tm // SUBLANES) - 1, 0), j)), blk,
                  pl.BlockSpec((CONV_W, tc), lambda i, j: (0, j)), pl.BlockSpec((1, tc), lambda i, j: (0, j))],
        out_specs=blk, out_shape=jax.ShapeDtypeStruct((s, f), BF16),
        compiler_params=_params(("parallel", "parallel")),
    )(gate_pre, gate_pre, up, conv_w, conv_b)


def ffn_act_bwd1(gate_pre, up, conv_w, conv_b, d_act):
    s, f = gate_pre.shape
    tm, tc = _ffn_tiles(s, f)
    nb = s // tm

    def body(gp_ref, halo_ref, up_ref, cw_ref, cb_ref, da_ref, dg_ref, du_ref, dcw_ref, dcb_ref):
        gp = gp_ref[...]
        gate, p1, p2 = _conv_gate(gp, halo_ref[...], pl.program_id(0) == 0, cw_ref[...], cb_ref[...])
        sig = jax.nn.sigmoid(gate)
        da = da_ref[...].astype(F32)
        du_ref[...] = (da * gate * sig).astype(du_ref.dtype)
        dg = da * up_ref[...] * (sig * (1.0 + gate * (1.0 - sig)))
        dg_ref[...] = dg
        dcb_ref[...] = jnp.sum(dg, axis=0, keepdims=True)
        dcw_ref[0:1, :] = jnp.sum(dg * p2, axis=0, keepdims=True)
        dcw_ref[1:2, :] = jnp.sum(dg * p1, axis=0, keepdims=True)
        dcw_ref[2:3, :] = jnp.sum(dg * gp, axis=0, keepdims=True)

    blk = pl.BlockSpec((tm, tc), lambda i, j: (i, j))
    return pl.pallas_call(
        body, name="ffn_act_bwd1", grid=(nb, f // tc),
        in_specs=[blk, pl.BlockSpec((SUBLANES, tc), lambda i, j: (jnp.maximum(i * (tm // SUBLANES) - 1, 0), j)), blk,
                  pl.BlockSpec((CONV_W, tc), lambda i, j: (0, j)), pl.BlockSpec((1, tc), lambda i, j: (0, j)), blk],
        out_specs=[blk, blk, pl.BlockSpec((None, CONV_W, tc), lambda i, j: (i, 0, j)),
                   pl.BlockSpec((None, 1, tc), lambda i, j: (i, 0, j))],
        out_shape=[jax.ShapeDtypeStruct((s, f), F32), jax.ShapeDtypeStruct((s, f), BF16),
                   jax.ShapeDtypeStruct((nb, CONV_W, f), F32), jax.ShapeDtypeStruct((nb, 1, f), F32)],
        compiler_params=_params(("parallel", "parallel")),
    )(gate_pre, gate_pre, up, conv_w, conv_b, d_act)


def ffn_act_bwd2(d_gate, conv_w):
    s, f = d_gate.shape
    tm, tc = _ffn_tiles(s, f)
    nb = s // tm

    def body(dg_ref, halo_ref, cw_ref, o_ref):
        dg, cw = dg_ref[...], cw_ref[...]
        last = pl.program_id(0) == nb - 1
        n1 = _shift_up(dg, halo_ref[...], last, 1)
        n2 = _shift_up(dg, halo_ref[...], last, 2)
        o_ref[...] = (cw[2:3, :] * dg + cw[1:2, :] * n1 + cw[0:1, :] * n2).astype(o_ref.dtype)

    blk = pl.BlockSpec((tm, tc), lambda i, j: (i, j))
    return pl.pallas_call(
        body, name="ffn_act_bwd2", grid=(nb, f // tc),
        in_specs=[blk, pl.BlockSpec((SUBLANES, tc), lambda i, j: (jnp.minimum((i + 1) * (tm // SUBLANES), s // SUBLANES - 1), j)),
                  pl.BlockSpec((CONV_W, tc), lambda i, j: (0, j))],
        out_specs=blk, out_shape=jax.ShapeDtypeStruct((s, f), BF16),
        compiler_params=_params(("parallel", "parallel")),
    )(d_gate, d_gate, conv_w)


def _mxu(x, y, cx, cy):
    if x.ndim == 3:
        return lax.dot_general(x, y, (((cx + 1,), (cy + 1,)), ((0,), (0,))), preferred_element_type=F32)
    return lax.dot_general(x, y, (((cx,), (cy,)), ((), ())), preferred_element_type=F32)


def _split(x):
    hi = x.astype(BF16)
    return hi, (x - hi.astype(F32)).astype(BF16)


def _make_dot3(cx, cy, passes):
    @jax.custom_vjp
    def f(x, y):
        if passes == 1:
            return _mxu(x.astype(BF16), y.astype(BF16), cx, cy)
        xh, xl = _split(x)
        yh, yl = _split(y)
        return _mxu(xh, yh, cx, cy) + (_mxu(xh, yl, cx, cy) + _mxu(xl, yh, cx, cy))

    def fwd(x, y):
        return f(x, y), (x, y)

    def bwd(res, g):
        x, y = res
        dx = dot3(g, y, 1, 1 - cy, passes) if cx == 1 else dot3(y, g, 1 - cy, 1, passes)
        dy = dot3(x, g, 1 - cx, 0, passes) if cy == 0 else dot3(g, x, 0, 1 - cx, passes)
        return dx, dy

    f.defvjp(fwd, bwd)
    return f


_DOT3 = {}


def dot3(x, y, cx, cy, passes=3):
    if (cx, cy, passes) not in _DOT3:
        _DOT3[(cx, cy, passes)] = _make_dot3(cx, cy, passes)
    return _DOT3[(cx, cy, passes)](x, y)


def _dot(x, y, passes=3):
    return dot3(x, y, 1, 0, passes)


def _dot_nt(x, y, passes=3):
    return dot3(x, y, 1, 1, passes)


def _dot_tn(x, y, passes=3):
    return dot3(x, y, 0, 0, passes)


def _tri_sum(x, lower):
    t = x.shape[-2]
    row = lax.broadcasted_iota(jnp.int32, (t, t), 0)
    col = lax.broadcasted_iota(jnp.int32, (t, t), 1)
    tri = jnp.where((col <= row) if lower else (col >= row), 1.0, 0.0).astype(BF16)
    if x.ndim == 3:
        tri = jnp.broadcast_to(tri[None], (x.shape[0], t, t))
    hi = x.astype(BF16)
    rest = x - hi.astype(F32)
    mid = rest.astype(BF16)
    low = (rest - mid.astype(F32)).astype(BF16)
    return _mxu(tri, hi, 1, 0) + (_mxu(tri, mid, 1, 0) + _mxu(tri, low, 1, 0))


@jax.custom_vjp
def _cumsum_rows(x):
    return _tri_sum(x, True)


_cumsum_rows.defvjp(lambda x: (_tri_sum(x, True), None), lambda _, g: (_tri_sum(g, False),))


def _scan_chunk(s0, r, w, k, v, a, b):
    t = r.shape[1]
    row = lax.broadcasted_iota(jnp.int32, (1, t, t), 1)
    col = lax.broadcasted_iota(jnp.int32, (1, t, t), 2)
    strict, incl = col < row, col <= row
    logw = jnp.log(w)
    cum = _cumsum_rows(logw)
    w_in, w_ex, w_inv = jnp.exp(cum), jnp.exp(cum - logw), jnp.exp(-cum)
    w_all = jnp.exp(jnp.sum(logw, axis=1, keepdims=True))
    at, rt, kt, bt = a * w_ex, r * w_in, k * w_inv, b * w_inv
    ps, po = SCAN_PASSES_SOLVE, SCAN_PASSES_OUT
    a_ab = jnp.where(strict, _dot_nt(at, bt, ps), 0.0)
    a_ak = jnp.where(strict, _dot_nt(at, kt, ps), 0.0)
    a_rk = jnp.where(incl, _dot_nt(rt, kt, po), 0.0)
    a_rb = jnp.where(incl, _dot_nt(rt, bt, po), 0.0)
    u = _dot_nt(at, s0, ps) + _dot(a_ak, v, ps)
    p = a_ab
    steps = int(math.log2(t))
    assert 2 ** steps == t
    for j in range(steps):
        u = u + _dot(p, u, ps)
        if j < steps - 1:
            p = _dot(p, p, ps)
    y = _dot_nt(rt, s0, po) + _dot(a_rk, v, po) + _dot(a_rb, u, po)
    s_new = s0 * w_all + _dot_tn(v, kt * w_all, po) + _dot_tn(u, bt * w_all, po)
    return y, s_new


def scan_fwd(r, w, k, v, a, b, comm=None):
    h, s, n = r.shape
    t = min(SCAN_CHUNK, s)
    nc = s // t

    hb = SCAN_HEADS if h % SCAN_HEADS == 0 else 1

    def body(r_ref, w_ref, k_ref, v_ref, a_ref, b_ref, y_ref, ck_ref, st_ref):
        @pl.when(pl.program_id(1) == 0)
        def _():
            st_ref[...] = jnp.zeros_like(st_ref)

        s0 = st_ref[...]
        ck_ref[...] = s0
        y, s_new = _scan_chunk(s0, r_ref[...], w_ref[...], k_ref[...], v_ref[...], a_ref[...], b_ref[...])
        y_ref[...] = y
        st_ref[...] = s_new

    blk = pl.BlockSpec((hb, t, n), lambda hh, c: (hh, c, 0))
    return _pallas(
        body, name="rwkv_scan_fwd", grid=(h // hb, nc), in_specs=[blk] * 6,
        out_specs=[blk, pl.BlockSpec((hb, None, n, n), lambda hh, c: (hh, c, 0, 0))],
        out_shape=[jax.ShapeDtypeStruct((h, s, n), F32), jax.ShapeDtypeStruct((h, nc, n, n), F32)],
        scratch_shapes=[pltpu.VMEM((hb, n, n), F32)], sem=("parallel", "arbitrary"), comm=comm,
    )(r, w, k, v, a, b)


def scan_bwd(r, w, k, v, a, b, ck, dy, comm=None):
    h, s, n = r.shape
    t = min(SCAN_CHUNK, s)
    nc = s // t

    hb = SCAN_HEADS if h % SCAN_HEADS == 0 else 1

    def body(r_ref, w_ref, k_ref, v_ref, a_ref, b_ref, ck_ref, dy_ref, dr_ref, dw_ref, dk_ref, dv_ref, da_ref, db_ref, ds_ref):
        @pl.when(pl.program_id(1) == 0)
        def _():
            ds_ref[...] = jnp.zeros_like(ds_ref)

        _, vjp_fn = jax.vjp(_scan_chunk, ck_ref[...], r_ref[...], w_ref[...], k_ref[...], v_ref[...], a_ref[...], b_ref[...])
        ds0, dr, dw, dk, dv, da, db = vjp_fn((dy_ref[...], ds_ref[...]))
        ds_ref[...] = ds0
        dr_ref[...], dw_ref[...], dk_ref[...], dv_ref[...], da_ref[...], db_ref[...] = dr, dw, dk, dv, da, db

    blk = pl.BlockSpec((hb, t, n), lambda hh, c: (hh, nc - 1 - c, 0))
    return _pallas(
        body, name="rwkv_scan_bwd", grid=(h // hb, nc),
        in_specs=[blk] * 6 + [pl.BlockSpec((hb, None, n, n), lambda hh, c: (hh, nc - 1 - c, 0, 0)), blk],
        out_specs=[blk] * 6, out_shape=[jax.ShapeDtypeStruct((h, s, n), F32)] * 6,
        scratch_shapes=[pltpu.VMEM((hb, n, n), F32)], sem=("parallel", "arbitrary"), comm=comm,
    )(r, w, k, v, a, b, ck, dy)


def _attn_scores(qn, qp, kn, kp, q0):
    scale = (QK_NOPE + QK_ROPE) ** -0.5
    sc = lax.dot_general(qn.astype(BF16), kn.astype(BF16), (((1,), (1,)), ((), ())), preferred_element_type=F32)
    sc = sc + lax.dot_general(qp.astype(BF16), kp.astype(BF16), (((1,), (1,)), ((), ())), preferred_element_type=F32)
    row = q0 + lax.broadcasted_iota(jnp.int32, sc.shape, 0)
    col = lax.broadcasted_iota(jnp.int32, sc.shape, 1)
    return jnp.where(row >= col, sc * scale, NEG_INF), scale


def attn_fwd(qn, qp, kn, kp, v):
    h, s, _ = qn.shape
    tq = _pick(s, 256, SUBLANES)

    def body(qn_ref, qp_ref, kn_ref, kp_ref, v_ref, o_ref, lse_ref):
        sc, _ = _attn_scores(qn_ref[...], qp_ref[...], kn_ref[...], kp_ref[...], pl.program_id(1) * tq)
        mx = jnp.max(sc, axis=-1, keepdims=True)
        e = jnp.exp(sc - mx)
        den = jnp.sum(e, axis=-1, keepdims=True)
        p = e / den
        o_ref[...] = jnp.dot(p.astype(BF16), v_ref[...].astype(BF16), preferred_element_type=F32)
        lse_ref[...] = mx + jnp.log(den)

    qblk = lambda c: pl.BlockSpec((None, tq, c), lambda hh, i: (hh, i, 0))
    kblk = lambda c: pl.BlockSpec((None, s, c), lambda hh, i: (hh, 0, 0))
    return pl.pallas_call(
        body, name="mla_attn_fwd", grid=(h, s // tq),
        in_specs=[qblk(QK_NOPE), qblk(QK_ROPE), kblk(QK_NOPE), pl.BlockSpec((s, QK_ROPE), lambda hh, i: (0, 0)), kblk(V_HEAD)],
        out_specs=[qblk(V_HEAD), qblk(1)],
        out_shape=[jax.ShapeDtypeStruct((h, s, V_HEAD), F32), jax.ShapeDtypeStruct((h, s, 1), F32)],
        compiler_params=_params(("parallel", "parallel")),
    )(qn, qp, kn, kp, v)


def attn_bwd(qn, qp, kn, kp, v, o, lse, do):
    h, s, _ = qn.shape
    tq = _pick(s, 256, SUBLANES)

    def body(qn_ref, qp_ref, kn_ref, kp_ref, v_ref, o_ref, lse_ref, do_ref, dqn_ref, dqp_ref, dkn_ref, dv_ref, dkp_ref):
        @pl.when(pl.program_id(1) == 0)
        def _():
            dkn_ref[...] = jnp.zeros_like(dkn_ref)
            dv_ref[...] = jnp.zeros_like(dv_ref)
            dkp_ref[...] = jnp.zeros_like(dkp_ref)

        qn_b, qp_b = qn_ref[...].astype(BF16), qp_ref[...].astype(BF16)
        kn_b, kp_b, v_b = kn_ref[...].astype(BF16), kp_ref[...].astype(BF16), v_ref[...].astype(BF16)
        sc, scale = _attn_scores(qn_b, qp_b, kn_b, kp_b, pl.program_id(1) * tq)
        p = jnp.exp(sc - lse_ref[...])
        dov = do_ref[...]
        do_b = dov.astype(BF16)
        p_b = p.astype(BF16)
        dv_ref[...] += lax.dot_general(p_b, do_b, (((0,), (0,)), ((), ())), preferred_element_type=F32)
        dp = lax.dot_general(do_b, v_b, (((1,), (1,)), ((), ())), preferred_element_type=F32)
        delta = jnp.sum(dov * o_ref[...], axis=-1, keepdims=True)
        ds = (p * (dp - delta) * scale).astype(BF16)
        dqn_ref[...] = jnp.dot(ds, kn_b, preferred_element_type=F32)
        dqp_ref[...] = jnp.dot(ds, kp_b, preferred_element_type=F32)
        dkn_ref[...] += lax.dot_general(ds, qn_b, (((0,), (0,)), ((), ())), preferred_element_type=F32)
        dkp_ref[...] += lax.dot_general(ds, qp_b, (((0,), (0,)), ((), ())), preferred_element_type=F32)

    qblk = lambda c: pl.BlockSpec((None, tq, c), lambda hh, i: (hh, i, 0))
    kblk = lambda c: pl.BlockSpec((None, s, c), lambda hh, i: (hh, 0, 0))
    return pl.pallas_call(
        body, name="mla_attn_bwd", grid=(h, s // tq),
        in_specs=[qblk(QK_NOPE), qblk(QK_ROPE), kblk(QK_NOPE), pl.BlockSpec((s, QK_ROPE), lambda hh, i: (0, 0)), kblk(V_HEAD),
                  qblk(V_HEAD), qblk(1), qblk(V_HEAD)],
        out_specs=[qblk(QK_NOPE), qblk(QK_ROPE), kblk(QK_NOPE), kblk(V_HEAD), kblk(QK_ROPE)],
        out_shape=[jax.ShapeDtypeStruct((h, s, QK_NOPE), F32), jax.ShapeDtypeStruct((h, s, QK_ROPE), F32),
                   jax.ShapeDtypeStruct((h, s, QK_NOPE), F32), jax.ShapeDtypeStruct((h, s, V_HEAD), F32),
                   jax.ShapeDtypeStruct((h, s, QK_ROPE), F32)],
        compiler_params=_params(("parallel", "arbitrary")),
    )(qn, qp, kn, kp, v, o, lse, do)


def _my_pos():
    return lax.axis_index("x"), lax.axis_index("y"), lax.axis_index("c")


def _dev_index(px, py, pc):
    return 4 * px + 2 * py + pc


def all_gather(name, shard):
    r, c = shard.shape

    def body(x_ref, out_ref, send_sems, recv_sems, local_sem):
        x, y, cc = _my_pos()
        me, sibling = (x, y, cc), (x, y, 1 - cc)
        chips = [(1 - x, y), (x, 1 - y), (1 - x, 1 - y)]

        def rows(px, py, pc):
            return out_ref.at[_dev_index(px, py, pc)]

        def copy(kk, block, to, src=None):
            return pltpu.make_async_remote_copy(
                src_ref=rows(*block) if src is None else src, dst_ref=rows(*block),
                send_sem=send_sems.at[kk], recv_sem=recv_sems.at[kk], device_id=to, device_id_type=MESH)

        mine = pltpu.make_async_copy(x_ref, rows(*me), local_sem)
        mine.start()
        first = [copy(0, me, sibling, src=x_ref)]
        first += [copy(1 + j, me, (*chip, cc), src=x_ref) for j, chip in enumerate(chips)]
        for cp in first:
            cp.start()
        passed = [copy(4 + j, (*chip, cc), sibling) for j, chip in enumerate(chips)]
        for j, chip in enumerate(chips):
            copy(1 + j, (*chip, cc), me).wait_recv()
            passed[j].start()
        copy(0, sibling, me).wait_recv()
        for j, chip in enumerate(chips):
            copy(4 + j, (*chip, 1 - cc), me).wait_recv()
        for cp in first + passed:
            cp.wait_send()
        mine.wait()

    return pl.pallas_call(
        body, name=name, out_shape=jax.ShapeDtypeStruct((N_DEV, r, c), shard.dtype),
        in_specs=[pl.BlockSpec(memory_space=pl.ANY)], out_specs=pl.BlockSpec(memory_space=pl.ANY),
        scratch_shapes=[pltpu.SemaphoreType.DMA((7,)), pltpu.SemaphoreType.DMA((7,)), pltpu.SemaphoreType.DMA],
    )(shard)


def _flip(kind):
    x, y, c = _my_pos()
    return {'c': (x, y, 1 - c), 'x': (1 - x, y, c), 'y': (x, 1 - y, c), 'xy': (1 - x, 1 - y, c)}[kind]


def exchange_sibling(name, g):
    _, r, c = g.shape

    def body(g_ref, out_ref, send_sems, recv_sems):
        x, y, cc = _my_pos()
        copies = []
        for px in range(2):
            for py in range(2):
                slot = 2 * px + py
                copies.append(pltpu.make_async_remote_copy(
                    src_ref=g_ref.at[_dev_index(px, py, 1 - cc)], dst_ref=out_ref.at[slot],
                    send_sem=send_sems.at[slot], recv_sem=recv_sems.at[slot], device_id=(x, y, 1 - cc), device_id_type=MESH))
        for cp in copies:
            cp.start()
        for cp in copies:
            cp.wait()

    return pl.pallas_call(
        body, name=name, out_shape=jax.ShapeDtypeStruct((4, r, c), g.dtype),
        in_specs=[pl.BlockSpec(memory_space=pl.ANY)], out_specs=pl.BlockSpec(memory_space=pl.ANY),
        scratch_shapes=[pltpu.SemaphoreType.DMA((4,)), pltpu.SemaphoreType.DMA((4,))],
    )(g)


def exchange_chips(name, hsum):
    _, r, c = hsum.shape

    def body(h_ref, out_ref, send_sems, recv_sems):
        x, y, cc = _my_pos()
        copies = []
        for j, (px, py) in enumerate([(1 - x, y), (x, 1 - y), (1 - x, 1 - y)]):
            copies.append(pltpu.make_async_remote_copy(
                src_ref=h_ref.at[2 * px + py], dst_ref=out_ref.at[j],
                send_sem=send_sems.at[j], recv_sem=recv_sems.at[j], device_id=(px, py, cc), device_id_type=MESH))
        for cp in copies:
            cp.start()
        for cp in copies:
            cp.wait()

    return pl.pallas_call(
        body, name=name, out_shape=jax.ShapeDtypeStruct((3, r, c), hsum.dtype),
        in_specs=[pl.BlockSpec(memory_space=pl.ANY)], out_specs=pl.BlockSpec(memory_space=pl.ANY),
        scratch_shapes=[pltpu.SemaphoreType.DMA((3,)), pltpu.SemaphoreType.DMA((3,))],
    )(hsum)


def add_slots(name, *terms):
    n, r, c = terms[0].shape
    tr = _pick(r, 512, SUBLANES)

    def body(*refs):
        acc = refs[0][...]
        for t in refs[1:-1]:
            acc = acc + t[...]
        refs[-1][...] = acc

    blk = pl.BlockSpec((None, tr, c), lambda s_, i: (s_, i, 0))
    return pl.pallas_call(
        body, name=name, grid=(n, r // tr), in_specs=[blk] * len(terms), out_specs=blk,
        out_shape=jax.ShapeDtypeStruct((n, r, c), F32), compiler_params=_params(("parallel", "parallel")),
    )(*terms)


def _rs_add_sibling(name, g, from_sibling, cc):
    _, r, c = g.shape
    tr = _pick(r, 512, SUBLANES * 2)

    def body(cc_ref, g_ref, s_ref, o_ref, ob_ref):
        tot = g_ref[...] + s_ref[...]
        o_ref[...] = tot
        ob_ref[...] = tot.astype(BF16)

    blk = pl.BlockSpec((None, tr, c), lambda s_, i, cc_ref: (s_, i, 0))
    return pl.pallas_call(
        body, name=name,
        grid_spec=pltpu.PrefetchScalarGridSpec(
            num_scalar_prefetch=1, grid=(4, r // tr),
            in_specs=[pl.BlockSpec((None, None, tr, c), lambda s_, i, cc_ref: (s_, cc_ref[0], i, 0)), blk], out_specs=[blk, blk]),
        out_shape=[jax.ShapeDtypeStruct((4, r, c), F32), jax.ShapeDtypeStruct((4, r, c), BF16)],
        compiler_params=_params(("parallel", "parallel")),
    )(cc.reshape(1).astype(jnp.int32), g.reshape(4, 2, r, c), from_sibling)


def _rs_add_chips(name, chip_sum, from_chips, slot):
    _, r, c = chip_sum.shape
    tr = _pick(r, 512, SUBLANES * 2)

    def body(slot_ref, h_ref, f0_ref, f1_ref, f2_ref, o_ref):
        o_ref[...] = ((h_ref[...] + f0_ref[...].astype(F32)) + f1_ref[...].astype(F32)) + f2_ref[...].astype(F32)

    def from_blk(j):
        return pl.BlockSpec((None, tr, c), lambda i, slot_ref: (j, i, 0))

    return pl.pallas_call(
        body, name=name,
        grid_spec=pltpu.PrefetchScalarGridSpec(
            num_scalar_prefetch=1, grid=(r // tr,),
            in_specs=[pl.BlockSpec((None, tr, c), lambda i, slot_ref: (slot_ref[0], i, 0)), from_blk(0), from_blk(1), from_blk(2)],
            out_specs=pl.BlockSpec((tr, c), lambda i, slot_ref: (i, 0))),
        out_shape=jax.ShapeDtypeStruct((r, c), F32), compiler_params=_params(("parallel",)),
    )(slot.reshape(1).astype(jnp.int32), chip_sum, from_chips, from_chips, from_chips)


def rs_chip_sum(tag, g):
    _, _, cc = _my_pos()
    from_sibling = exchange_sibling("rs_sibling_" + tag, g)
    return _rs_add_sibling("rs_add_sibling_" + tag, g, from_sibling, cc)


def rs_finish(tag, chip_sum, from_chips):
    x, y, _ = _my_pos()
    return _rs_add_chips("rs_add_chips_" + tag, chip_sum, from_chips, 2 * x + y)


def reduce_scatter(tag, g):
    chip_sum, chip_sum_b = rs_chip_sum(tag, g)
    return rs_finish(tag, chip_sum, exchange_chips("rs_chips_" + tag, chip_sum_b))


class GatherIci:
    def __init__(self, shards):
        self.inputs = list(shards)
        self.out_shapes = [jax.ShapeDtypeStruct((N_DEV,) + s.shape, s.dtype) for s in shards]
        self.n_remote, self.n_local = 3 * len(shards), len(shards)

    def make(self, cins, couts, send, recv, local):
        x, y, cc = _my_pos()
        me = _dev_index(x, y, cc)
        copies = []
        for w, (src, out) in enumerate(zip(cins, couts, strict=True)):
            copies.append(pltpu.make_async_copy(src, out.at[me], local.at[w]))
            for j, (px, py) in enumerate([(1 - x, y), (x, 1 - y), (1 - x, 1 - y)]):
                copies.append(pltpu.make_async_remote_copy(
                    src_ref=src, dst_ref=out.at[me], send_sem=send.at[3 * w + j], recv_sem=recv.at[3 * w + j],
                    device_id=(px, py, cc), device_id_type=MESH))
        return copies


class RsChips:
    def __init__(self, chip_sums):
        self.inputs = list(chip_sums)
        self.out_shapes = [jax.ShapeDtypeStruct((3,) + h.shape[1:], h.dtype) for h in chip_sums]
        self.n_remote, self.n_local = 3 * len(chip_sums), 0

    def make(self, cins, couts, send, recv, local):
        x, y, cc = _my_pos()
        copies = []
        for w, (h_ref, out) in enumerate(zip(cins, couts, strict=True)):
            for j, (px, py) in enumerate([(1 - x, y), (x, 1 - y), (1 - x, 1 - y)]):
                copies.append(pltpu.make_async_remote_copy(
                    src_ref=h_ref.at[2 * px + py], dst_ref=out.at[j], send_sem=send.at[3 * w + j], recv_sem=recv.at[3 * w + j],
                    device_id=(px, py, cc), device_id_type=MESH))
        return copies


def gather_d2d(name, arrays):
    n = len(arrays)

    def body(*refs):
        outs, send, recv = refs[n:2 * n], refs[2 * n], refs[2 * n + 1]
        x, y, cc = _my_pos()
        copies = []
        for w, out in enumerate(outs):
            for px in range(2):
                for py in range(2):
                    q = 4 * w + 2 * px + py
                    slab = out.at[_dev_index(px, py, cc)]
                    copies.append(pltpu.make_async_remote_copy(
                        src_ref=slab, dst_ref=slab, send_sem=send.at[q], recv_sem=recv.at[q],
                        device_id=(x, y, 1 - cc), device_id_type=MESH))
        for cp in copies:
            cp.start()
        for cp in copies:
            cp.wait()

    any_spec = pl.BlockSpec(memory_space=pl.ANY)
    return pl.pallas_call(
        body, name=name, out_shape=[jax.ShapeDtypeStruct(a.shape, a.dtype) for a in arrays],
        in_specs=[any_spec] * n, out_specs=[any_spec] * n, input_output_aliases={i: i for i in range(n)},
        scratch_shapes=[pltpu.SemaphoreType.DMA((4 * n,)), pltpu.SemaphoreType.DMA((4 * n,))],
    )(*arrays)


def _pallas(body, *, name, grid, in_specs, out_specs, out_shape, scratch_shapes=(), sem, comm=None):
    in_specs, out_specs, out_shape, scratch_shapes = list(in_specs), list(out_specs), list(out_shape), list(scratch_shapes)
    if comm is None:
        return pl.pallas_call(body, name=name, grid=grid, in_specs=in_specs, out_specs=out_specs, out_shape=out_shape,
                              scratch_shapes=scratch_shapes, compiler_params=_params(sem))
    n_in, n_out, n_scr = len(in_specs), len(out_specs), len(scratch_shapes)
    nci, nco = len(comm.inputs), len(comm.out_shapes)

    def body2(*refs):
        ins, cins = refs[:n_in], refs[n_in:n_in + nci]
        o0 = n_in + nci
        outs, couts = refs[o0:o0 + n_out], refs[o0 + n_out:o0 + n_out + nco]
        s0 = o0 + n_out + nco
        scr = refs[s0:s0 + n_scr]
        send, recv, local = refs[s0 + n_scr:]
        pids = [pl.program_id(k) for k in range(len(grid))]
        first = functools.reduce(jnp.logical_and, [p == 0 for p in pids])
        last = functools.reduce(jnp.logical_and, [p == g - 1 for p, g in zip(pids, grid)])

        @pl.when(first)
        def _():
            for cp in comm.make(cins, couts, send, recv, local):
                cp.start()

        body(*ins, *outs, *scr)

        @pl.when(last)
        def _():
            for cp in comm.make(cins, couts, send, recv, local):
                cp.wait()

    any_spec = pl.BlockSpec(memory_space=pl.ANY)
    call = pl.pallas_call(
        body2, name=name, grid=grid, in_specs=in_specs + [any_spec] * nci, out_specs=out_specs + [any_spec] * nco,
        out_shape=out_shape + list(comm.out_shapes),
        scratch_shapes=scratch_shapes + [pltpu.SemaphoreType.DMA((comm.n_remote,)), pltpu.SemaphoreType.DMA((comm.n_remote,)),
                                         pltpu.SemaphoreType.DMA((max(comm.n_local, 1),))],
        compiler_params=_params(tuple("arbitrary" for _ in grid)))
    return lambda *args: call(*args, *comm.inputs)


PACK_W = 1024


class Pack:
    def __init__(self, entries, row_unit):
        self.entries = entries
        self.sizes = [int(np.prod(sh)) for _, sh in entries]
        self.offsets = np.concatenate([[0], np.cumsum(self.sizes)]).tolist()
        self.total = _round_up(self.offsets[-1], PACK_W * row_unit)
        self.rows = self.total // PACK_W

    def pack(self, arrays, dtype, lead=()):
        flat = [arrays[n].astype(dtype).reshape(lead + (-1,)) for n, _ in self.entries]
        pad = self.total - self.offsets[-1]
        if pad:
            flat.append(jnp.zeros(lead + (pad,), dtype))
        return jnp.concatenate(flat, axis=-1).reshape(lead + (self.rows, PACK_W))

    def unpack(self, buf, lead=()):
        flat = buf.reshape(lead + (self.total,))
        out = {}
        for (n, sh), off, sz in zip(self.entries, self.offsets, self.sizes):
            out[n] = lax.slice_in_dim(flat, off, off + sz, axis=len(lead)).reshape(lead + tuple(sh))
        return out


def _gathered_to_full(g, how):
    _, a, b = g.shape
    if how == 'row':
        return g.reshape(N_DEV * a, b)
    return jnp.transpose(g, (1, 0, 2)).reshape(a, N_DEV * b)


def _full_to_shards(w, how):
    a, b = w.shape
    if how == 'row':
        return w.reshape(N_DEV, a // N_DEV, b)
    return jnp.transpose(w.reshape(a, N_DEV, b // N_DEV), (1, 0, 2))


def _to_heads(t, width):
    s, c = t.shape
    return jnp.transpose(t.reshape(s, c // width, width), (1, 0, 2))


def _from_heads(t):
    h, s, w = t.shape
    return jnp.transpose(t, (1, 0, 2)).reshape(s, h * w)


def _rot_matrix():
    half = QK_ROPE // 2
    rot = np.zeros((QK_ROPE, QK_ROPE), np.float32)
    for i in range(half):
        rot[i + half, i] = -1.0
        rot[i, i + half] = 1.0
    return jnp.asarray(rot)


def _inv_freq2():
    half = QK_ROPE // 2
    inv = ROPE_THETA ** (-np.arange(half, dtype=np.float32) / half)
    return jnp.asarray(np.concatenate([inv, inv])[None, :].astype(np.float32))


def kernel(x, positions, attn_norm_g, w_in, rwkv_mu, rwkv_w0, rwkv_w2, rwkv_a0, rwkv_a2, rwkv_g2, rwkv_k_k, rwkv_k_a, rwkv_r_k, rwkv_gn_w, rwkv_gn_b, mla_q_norm_g, mla_w_uq, mla_kv_norm_g, mla_w_ukv, w_out, ffn_norm_g, ffn_w_gate, ffn_w_up, ffn_conv_w, ffn_conv_b, ffn_w_down, final_norm_g, loss_target, m_attn_norm_g, m_w_in, m_rwkv_mu, m_rwkv_w0, m_rwkv_w2, m_rwkv_a0, m_rwkv_a2, m_rwkv_g2, m_rwkv_k_k, m_rwkv_k_a, m_rwkv_r_k, m_rwkv_gn_w, m_rwkv_gn_b, m_mla_q_norm_g, m_mla_w_uq, m_mla_kv_norm_g, m_mla_w_ukv, m_w_out, m_ffn_norm_g, m_ffn_w_gate, m_ffn_w_up, m_ffn_conv_w, m_ffn_conv_b, m_ffn_w_down, m_final_norm_g, v_attn_norm_g, v_w_in, v_rwkv_mu, v_rwkv_w0, v_rwkv_w2, v_rwkv_a0, v_rwkv_a2, v_rwkv_g2, v_rwkv_k_k, v_rwkv_k_a, v_rwkv_r_k, v_rwkv_gn_w, v_rwkv_gn_b, v_mla_q_norm_g, v_mla_w_uq, v_mla_kv_norm_g, v_mla_w_ukv, v_w_out, v_ffn_norm_g, v_ffn_w_gate, v_ffn_w_up, v_ffn_conv_w, v_ffn_conv_b, v_ffn_w_down, v_final_norm_g):
    given = dict(locals())
    wts = {n: given[n] for n in WEIGHTS}
    mom_m = {n: given["m_" + n] for n in WEIGHTS}
    mom_v = {n: given["v_" + n] for n in WEIGHTS}
    out_shapes = {n: wts[n].shape for n in WEIGHTS}

    def local2d(n, a):
        if n == 'rwkv_r_k' or a.ndim <= 2:
            return a.reshape(1, -1)
        return a.reshape(a.shape[1:])

    w2d = {n: local2d(n, wts[n]) for n in WEIGHTS}
    m2d = {n: local2d(n, mom_m[n]) for n in WEIGHTS}
    v2d = {n: local2d(n, mom_v[n]) for n in WEIGHTS}

    xs = x.reshape(x.shape[1:])
    tgt = loss_target.reshape(loss_target.shape[1:])
    s, d = xs.shape
    c_rwkv = w2d['rwkv_w0'].shape[1]
    n_rh = c_rwkv // RWKV_HEAD
    decay_lora, aaa_lora, gate_lora = w2d['rwkv_w2'].shape[0], w2d['rwkv_a2'].shape[0], w2d['rwkv_g2'].shape[0]
    q_lora, kv_lora = w2d['mla_q_norm_g'].shape[1], w2d['mla_kv_norm_g'].shape[1]
    shift_dim = w2d['rwkv_mu'].shape[1]
    d_in = w2d['w_in'].shape[1] * N_DEV
    d_in_pad = _round_up(d_in, LANES)
    n_mh = w2d['mla_w_uq'].shape[1] * N_DEV // (QK_NOPE + QK_ROPE)
    d_ff = w2d['ffn_conv_b'].shape[1]
    tm = _pick(s, 256, SUBLANES)
    tm_wide = _pick(s, 128, SUBLANES)

    nb = {n: w2d[n].shape[1] for n in BIG if BIG[n] == 'col'}
    nbp = {n: _round_up(v_, LANES) for n, v_ in nb.items()}
    shards = {}
    for n in BIG:
        w = w2d[n].astype(BF16)
        if BIG[n] == 'col':
            w = jnp.pad(w, ((0, 0), (0, nbp[n] - nb[n])))
        elif n == 'ffn_w_down':
            w = jnp.pad(w, ((0, nbp['ffn_w_gate'] - w.shape[0]), (0, 0)))
        shards[n] = w

    def as_used(n, g):
        return g if BIG[n] == 'col' else g.reshape(N_DEV * g.shape[1], g.shape[2])

    gathered = {'w_in': as_used('w_in', all_gather("gather_w_in", shards['w_in']))}
    later = [n for n in BIG if n != 'w_in']
    f_pad = N_DEV * nbp['ffn_w_gate']
    small_pack = Pack([(n, w2d[n].shape) for n in SMALL_SHARDED], 8)
    small_all = all_gather("gather_small", small_pack.pack(w2d, F32))
    full = {}
    for n, g in small_pack.unpack(small_all, lead=(N_DEV,)).items():
        full[n] = _gathered_to_full(g, SMALL_SHARDED[n])
    conv_w_pad = pad_cols(full['ffn_conv_w'], nb['ffn_w_gate'], nbp['ffn_w_gate'])
    conv_b_pad = pad_cols(w2d['ffn_conv_b'], nb['ffn_w_gate'], nbp['ffn_w_gate'])

    (h1,) = rowwise("rms_attn", _rms_fn, [xs, w2d['attn_norm_g']], ['row', 'const'], [('row', d, BF16)], heads=1, s=s, tm=tm)
    proj = unpad_cols(mm_sh("proj_in", h1, gathered['w_in']), nb['w_in'], nbp['w_in'])
    p_rwkv = proj[:, :shift_dim]
    c_q = proj[:, shift_dim:shift_dim + q_lora]
    c_kv = proj[:, shift_dim + q_lora:shift_dim + q_lora + kv_lora]
    k_pe = proj[:, shift_dim + q_lora + kv_lora:d_in]
    shifted = token_shift_fwd(p_rwkv, w2d['rwkv_mu'], tm_wide)
    o1, o2, o3 = c_rwkv, 2 * c_rwkv, 3 * c_rwkv
    hr = _to_heads(shifted[:, :o1], RWKV_HEAD)
    hk = _to_heads(shifted[:, o1:o2], RWKV_HEAD)
    hv = _to_heads(shifted[:, o2:o3], RWKV_HEAD)
    hw = shifted[:, o3:o3 + decay_lora]
    ha = shifted[:, o3 + decay_lora:o3 + decay_lora + aaa_lora]
    hg = shifted[:, o3 + decay_lora + aaa_lora:]

    def per_head(vec):
        return vec.reshape(n_rh, 1, RWKV_HEAD)

    def lora_heads(w):
        return jnp.transpose(w.reshape(w.shape[0], n_rh, RWKV_HEAD), (1, 0, 2))

    pre_args = [hk, hw, ha, hg, per_head(w2d['rwkv_w0']), lora_heads(full['rwkv_w2']), per_head(w2d['rwkv_a0']),
                lora_heads(full['rwkv_a2']), lora_heads(full['rwkv_g2']), per_head(w2d['rwkv_k_k']), per_head(w2d['rwkv_k_a'])]
    pre_kinds = ['hrow', 'row', 'row', 'row', 'hconst', 'hconst', 'hconst', 'hconst', 'hconst', 'hconst', 'hconst']
    decay, kx, a_sc, b_sc, gate_r = rowwise("rwkv_pre", _rwkv_pre_fn, pre_args, pre_kinds,
                                            [('hrow', RWKV_HEAD, F32)] * 5, heads=n_rh, s=s, tm=tm)
    y_scan, ckpt, *landed = scan_fwd(hr, decay, kx, hv, a_sc, b_sc, comm=GatherIci([shards[n] for n in later]))
    for n, g in zip(later, gather_d2d("gather_d2d", landed), strict=True):
        gathered[n] = as_used(n, g)
    post_args = [y_scan, hr, kx, hv, gate_r, per_head(w2d['rwkv_gn_w']), per_head(w2d['rwkv_gn_b']), per_head(w2d['rwkv_r_k'])]
    post_kinds = ['hrow'] * 5 + ['hconst'] * 3
    (y_rwkv_h,) = rowwise("rwkv_post", _rwkv_post_fn, post_args, post_kinds, [('hrow', RWKV_HEAD, F32)], heads=n_rh, s=s, tm=tm)

    pos = positions.reshape(s, 1).astype(F32)
    rot, inv2 = _rot_matrix(), _inv_freq2()
    mla_args = [c_q, c_kv, k_pe, pos, w2d['mla_q_norm_g'], w2d['mla_kv_norm_g'], inv2, rot]
    mla_kinds = ['row', 'row', 'row', 'row', 'const', 'const', 'const', 'const']
    qn, kvn, kp_rot, cos2, sin2 = rowwise(
        "mla_pre", _mla_pre_fn, mla_args, mla_kinds,
        [('row', q_lora, BF16), ('row', kv_lora, BF16), ('row', QK_ROPE, F32), ('row', QK_ROPE, F32), ('row', QK_ROPE, F32)],
        heads=1, s=s, tm=tm)
    q = unpad_cols(mm_sh("proj_q", qn, gathered['mla_w_uq']), nb['mla_w_uq'], nbp['mla_w_uq'])
    kv = unpad_cols(mm_sh("proj_kv", kvn, gathered['mla_w_ukv']), nb['mla_w_ukv'], nbp['mla_w_ukv'])
    q_h = _to_heads(q, QK_NOPE + QK_ROPE)
    kv_h = _to_heads(kv, QK_NOPE + V_HEAD)
    q_nope, q_pe = q_h[..., :QK_NOPE], q_h[..., QK_NOPE:]
    k_nope, v_att = kv_h[..., :QK_NOPE], kv_h[..., QK_NOPE:]
    ropeq_args = [q_pe, cos2, sin2, rot]
    ropeq_kinds = ['hrow', 'row', 'row', 'const']
    (q_pe_rot,) = rowwise("rope_q", _rope_q_fn, ropeq_args, ropeq_kinds, [('hrow', QK_ROPE, F32)], heads=n_mh, s=s, tm=tm)
    o_att, lse = attn_fwd(q_nope, q_pe_rot, k_nope, kp_rot, v_att)
    ycat = jnp.concatenate([_from_heads(y_rwkv_h), _from_heads(o_att)], axis=-1).astype(BF16)
    x1 = mm("proj_out", ycat, gathered['w_out'], add=xs)
    (h2,) = rowwise("rms_ffn", _rms_fn, [x1, w2d['ffn_norm_g']], ['row', 'const'], [('row', d, BF16)], heads=1, s=s, tm=tm)
    gate_pre = mm_sh("ffn_gate", h2, gathered['ffn_w_gate'])
    up = mm_sh("ffn_up", h2, gathered['ffn_w_up'])
    act = ffn_act_fwd(gate_pre, up, conv_w_pad, conv_b_pad)
    x2 = mm("ffn_down", act, gathered['ffn_w_down'], add=x1)

    ones = jnp.ones((s, 1), F32)
    fin_g = w2d['final_norm_g']
    d_x2, dg_final_p, loss_rows = rowwise_vjp("loss_bwd", _loss_fn, [x2, fin_g, tgt], ['row', 'const', 'row'], [ones], ['row'],
                                              [0, 1], heads=1, s=s, tm=tm, primal=True)
    d_x2_b = d_x2.astype(BF16)
    d_act = mm_nt("d_act", d_x2_b, gathered['ffn_w_down'], out_dtype=BF16)
    gsh = {}
    gsh['ffn_w_down'] = mm("dw_down", act.T, d_x2_b).reshape(N_DEV, nbp['ffn_w_gate'], d)
    d_gate, d_up, dcw_p, dcb_p = ffn_act_bwd1(gate_pre, up, conv_w_pad, conv_b_pad, d_act)
    d_gp = ffn_act_bwd2(d_gate, conv_w_pad)
    d_h2 = mm_sh_nt("d_h2_up", d_up, gathered['ffn_w_up'], add=mm_sh_nt("d_h2_gate", d_gp, gathered['ffn_w_gate']))
    h2_t = h2.T
    gsh['ffn_w_gate'] = mm_sh_out("dw_gate", h2_t, d_gp)
    gsh['ffn_w_up'] = mm_sh_out("dw_up", h2_t, d_up)
    d_x1n, dg_ffn_p = rowwise_vjp("rms_ffn_bwd", _rms_fn, [x1, w2d['ffn_norm_g']], ['row', 'const'], [d_h2], ['row'], [0, 1],
                                  heads=1, s=s, tm=tm)
    d_x1 = add_slots("d_x1_add", d_x1n[None], d_x2[None])[0]
    d_x1_b = d_x1.astype(BF16)
    d_ycat = mm_nt("d_ycat", d_x1_b, gathered['w_out'])
    gsh['w_out'] = mm("dw_out", ycat.T, d_x1_b).reshape(N_DEV, d // N_DEV, d)
    d_yr_h = _to_heads(d_ycat[:, :c_rwkv], RWKV_HEAD)
    d_o_h = _to_heads(d_ycat[:, c_rwkv:], V_HEAD)

    d_qn_h, d_qpr_h, d_kn_h, d_v_h, d_kp_h = attn_bwd(q_nope, q_pe_rot, k_nope, kp_rot, v_att, o_att, lse, d_o_h)
    (d_qp_h,) = rowwise_vjp("rope_q_bwd", _rope_q_fn, ropeq_args, ropeq_kinds, [d_qpr_h], ['hrow'], [0], heads=n_mh, s=s, tm=tm)
    d_q = pad_cols(_from_heads(jnp.concatenate([d_qn_h, d_qp_h], axis=-1)).astype(BF16), nb['mla_w_uq'], nbp['mla_w_uq'])
    d_kv = pad_cols(_from_heads(jnp.concatenate([d_kn_h, d_v_h], axis=-1)).astype(BF16), nb['mla_w_ukv'], nbp['mla_w_ukv'])
    d_kp_rot = headsum("d_kpe_heads", d_kp_h)
    d_qn = mm_sh_nt("d_qn", d_q, gathered['mla_w_uq'])
    d_kvn = mm_sh_nt("d_kvn", d_kv, gathered['mla_w_ukv'])
    gsh['mla_w_uq'] = mm_sh_out("dw_uq", qn.T, d_q)
    gsh['mla_w_ukv'] = mm_sh_out("dw_ukv", kvn.T, d_kv)
    d_cq, d_ckv, d_kpe, dg_q_p, dg_kv_p = rowwise_vjp(
        "mla_pre_bwd", _mla_pre_grad_fn, mla_args, mla_kinds, [d_qn, d_kvn, d_kp_rot], ['row', 'row', 'row'], [0, 1, 2, 4, 5],
        heads=1, s=s, tm=tm)

    d_y, d_r_post, d_k_post, d_v_post, d_gate_r, dgnw_p, dgnb_p, drk_p = rowwise_vjp(
        "rwkv_post_bwd", _rwkv_post_fn, post_args, post_kinds, [d_yr_h], ['hrow'], list(range(8)), heads=n_rh, s=s, tm=tm)
    chip_sums = {n: rs_chip_sum(n, gsh[n]) for n in later}
    d_r_sc, d_w_sc, d_k_sc, d_v_sc, d_a_sc, d_b_sc, *arrived = scan_bwd(
        hr, decay, kx, hv, a_sc, b_sc, ckpt, d_y, comm=RsChips([chip_sums[n][1] for n in later]))
    from_chips = dict(zip(later, arrived, strict=True))
    d_hk, d_hw_p, d_ha_p, d_hg_p, dw0_p, dw2_p, da0_p, da2_p, dg2_p, dkk_p, dka_p, d_hr, d_hv = rowwise_vjp(
        "rwkv_pre_bwd", _rwkv_pre_grad_fn, pre_args + [hr, hv], pre_kinds + ['hrow', 'hrow'],
        [d_w_sc, d_k_sc, d_k_post, d_a_sc, d_b_sc, d_gate_r, d_r_sc, d_r_post, d_v_sc, d_v_post], ['hrow'] * 10,
        list(range(13)), heads=n_rh, s=s, tm=tm)
    d_shifted = jnp.concatenate([_from_heads(d_hr), _from_heads(d_hk), _from_heads(d_hv), headsum("d_hw_heads", d_hw_p),
                                 headsum("d_ha_heads", d_ha_p), headsum("d_hg_heads", d_hg_p)], axis=-1)
    d_p_rwkv, dmu_p = token_shift_bwd(p_rwkv, w2d['rwkv_mu'], d_shifted, tm_wide)
    d_proj = pad_cols(jnp.concatenate([d_p_rwkv, d_cq, d_ckv, d_kpe], axis=-1).astype(BF16), nb['w_in'], nbp['w_in'])
    gsh['w_in'] = mm_sh_out("dw_in", h1.T, d_proj)
    chip_sums['w_in'] = rs_chip_sum('w_in', gsh['w_in'])
    d_h1, from_chips['w_in'] = mm_sh_nt("d_h1", d_proj, gathered['w_in'], comm=RsChips([chip_sums['w_in'][1]]))
    d_xn, dg_attn_p = rowwise_vjp("rms_attn_bwd", _rms_fn, [xs, w2d['attn_norm_g']], ['row', 'const'], [d_h1], ['row'], [0, 1],
                                  heads=1, s=s, tm=tm)
    grad_x = add_slots("grad_x_add", d_xn[None], d_x1[None])[0]

    def from_heads_lora(g):
        return jnp.transpose(g, (1, 0, 2)).reshape(g.shape[1], n_rh * RWKV_HEAD)

    gw = {}
    gw['rwkv_w2'] = from_heads_lora(sum_partials("sum_dw2", dw2_p, True))
    gw['rwkv_a2'] = from_heads_lora(sum_partials("sum_da2", da2_p, True))
    gw['rwkv_g2'] = from_heads_lora(sum_partials("sum_dg2", dg2_p, True))
    dcw_pad = colsum("sum_dconv_w", dcw_p.reshape(dcw_p.shape[0], CONV_W * f_pad)).reshape(CONV_W, f_pad)
    gw['ffn_conv_w'] = unpad_cols(dcw_pad, nb['ffn_w_gate'], nbp['ffn_w_gate'])

    rep = {
        'attn_norm_g': sum_partials("sum_dg_attn", dg_attn_p, False),
        'rwkv_mu': colsum("sum_dmu", dmu_p.reshape(dmu_p.shape[0], shift_dim)),
        'rwkv_w0': sum_partials("sum_dw0", dw0_p, True).reshape(1, c_rwkv),
        'rwkv_a0': sum_partials("sum_da0", da0_p, True).reshape(1, c_rwkv),
        'rwkv_k_k': sum_partials("sum_dkk", dkk_p, True).reshape(1, c_rwkv),
        'rwkv_k_a': sum_partials("sum_dka", dka_p, True).reshape(1, c_rwkv),
        'rwkv_r_k': sum_partials("sum_drk", drk_p, True).reshape(1, c_rwkv),
        'rwkv_gn_w': sum_partials("sum_dgnw", dgnw_p, True).reshape(1, c_rwkv),
        'rwkv_gn_b': sum_partials("sum_dgnb", dgnb_p, True).reshape(1, c_rwkv),
        'mla_q_norm_g': sum_partials("sum_dg_q", dg_q_p, False),
        'mla_kv_norm_g': sum_partials("sum_dg_kv", dg_kv_p, False),
        'ffn_norm_g': sum_partials("sum_dg_ffn", dg_ffn_p, False),
        'ffn_conv_b': unpad_cols(colsum("sum_dconv_b", dcb_p.reshape(dcb_p.shape[0], f_pad)), nb['ffn_w_gate'], nbp['ffn_w_gate']),
        'final_norm_g': sum_partials("sum_dg_final", dg_final_p, False),
        'loss': sum_all("sum_loss", loss_rows.reshape(s // SUBLANES, SUBLANES)),
    }
    rep_pack = Pack([(n, w2d[n].shape) for n in REPLICATED] + [('loss', (1, 1))], 8)
    rep_all = all_gather("gather_rep_grads", rep_pack.pack(rep, F32))
    rep_sum = colsum("sum_rep_grads", rep_all.reshape(N_DEV, rep_pack.total)).reshape(rep_pack.rows, PACK_W)
    rep_g = rep_pack.unpack(rep_sum)
    loss = rep_g.pop('loss').reshape(())

    grads, deltas, new_m, new_v = dict(rep_g), {}, {}, {}
    for n in BIG:
        a, b = w2d[n].shape
        grads[n] = rs_finish(n, chip_sums[n][0], from_chips[n])[:a, :b]
        deltas[n], new_m[n], new_v[n] = rowwise(
            "adamw_" + n, _adamw_fn, [w2d[n], grads[n], m2d[n], v2d[n]], ['row'] * 4, [('row', b, F32)] * 3,
            heads=1, s=a, tm=_pick(a, 256, SUBLANES))
    sm_pack = Pack([(n, w2d[n].shape) for n in SMALL_SHARDED], 8)
    g_shards = {n: _full_to_shards(gw[n], SMALL_SHARDED[n]) for n in SMALL_SHARDED}
    grads.update(sm_pack.unpack(reduce_scatter("small", sm_pack.pack(g_shards, F32, lead=(N_DEV,)))))
    rest_pack = Pack([(n, w2d[n].shape) for n in WEIGHTS if n not in BIG], 8)
    d_r, m_r, v_r = rowwise(
        "adamw_small", _adamw_fn, [rest_pack.pack(w2d, F32), rest_pack.pack(grads, F32), rest_pack.pack(m2d, F32),
                                   rest_pack.pack(v2d, F32)],
        ['row'] * 4, [('row', PACK_W, F32)] * 3, heads=1, s=rest_pack.rows, tm=_pick(rest_pack.rows, 512, SUBLANES))
    deltas.update(rest_pack.unpack(d_r))
    new_m.update(rest_pack.unpack(m_r))
    new_v.update(rest_pack.unpack(v_r))

    def shaped(dct):
        return [dct[n].reshape(out_shapes[n]) for n in WEIGHTS]

    return (loss, grad_x.reshape(x.shape), *shaped(grads), *shaped(deltas), *shaped(new_m), *shaped(new_v))
```

```python
import functools
import math

import jax
import jax.numpy as jnp
import numpy as np
from jax import lax
from jax.experimental import pallas as pl
from jax.experimental.pallas import tpu as pltpu

F32 = jnp.float32
BF16 = jnp.bfloat16
HIGHEST = lax.Precision.HIGHEST
MESH = pl.DeviceIdType.MESH

N_DEV = 8
LANES = 128
SUBLANES = 8
VMEM_LIMIT = 48 * 1024 * 1024
RESIDENT_BYTES = 8 * 1024 * 1024

NORM_EPS = 1e-6
GN_EPS = 64e-5
RWKV_HEAD = 64
QK_NOPE = 128
QK_ROPE = 64
V_HEAD = 128
ROPE_THETA = 10000.0
CONV_W = 3
NEG_INF = -1e30
SCAN_CHUNK = 64
SCAN_HEADS = 8
SCAN_PASSES_SOLVE = 1
SCAN_PASSES_OUT = 1

ADAM_LR = 0.001
ADAM_B1 = 0.9
ADAM_B2 = 0.999
ADAM_EPS = 1e-08
ADAM_WD = 0.01
ADAM_STEP = 10

WEIGHTS = ['attn_norm_g', 'w_in', 'rwkv_mu', 'rwkv_w0', 'rwkv_w2', 'rwkv_a0', 'rwkv_a2', 'rwkv_g2', 'rwkv_k_k',
           'rwkv_k_a', 'rwkv_r_k', 'rwkv_gn_w', 'rwkv_gn_b', 'mla_q_norm_g', 'mla_w_uq', 'mla_kv_norm_g', 'mla_w_ukv',
           'w_out', 'ffn_norm_g', 'ffn_w_gate', 'ffn_w_up', 'ffn_conv_w', 'ffn_conv_b', 'ffn_w_down', 'final_norm_g']
BIG = {'w_in': 'col', 'mla_w_uq': 'col', 'mla_w_ukv': 'col', 'w_out': 'row', 'ffn_w_gate': 'col', 'ffn_w_up': 'col',
       'ffn_w_down': 'row'}
SMALL_SHARDED = {'rwkv_w2': 'col', 'rwkv_a2': 'col', 'rwkv_g2': 'col', 'ffn_conv_w': 'col'}
SHARDED = {**BIG, **SMALL_SHARDED}
REPLICATED = [n for n in WEIGHTS if n not in SHARDED]


def _round_up(n, m):
    return (n + m - 1) // m * m


def _pick(n, cap, unit):
    if n <= cap:
        return n
    best = None
    for t in range(unit, cap + 1, unit):
        if n % t == 0:
            best = t
    assert best is not None, (n, cap, unit)
    return best


def _params(sem):
    return pltpu.CompilerParams(dimension_semantics=sem, vmem_limit_bytes=VMEM_LIMIT)


def mm(name, a, b, add=None, out_dtype=F32):
    m, k = a.shape
    k2, n = b.shape
    assert k == k2, (name, a.shape, b.shape)
    tm = _pick(m, 512, SUBLANES * 2)
    tn = n if k * n * 2 <= RESIDENT_BYTES else _pick(n, 640, LANES)
    has_add = add is not None

    def body(a_ref, b_ref, *rest):
        o_ref = rest[-1]
        acc = jnp.dot(a_ref[...].astype(BF16), b_ref[...].astype(BF16), preferred_element_type=F32)
        if has_add:
            acc = acc + rest[0][...].astype(F32)
        o_ref[...] = acc.astype(o_ref.dtype)

    in_specs = [pl.BlockSpec((tm, k), lambda i, j: (i, 0)), pl.BlockSpec((k, tn), lambda i, j: (0, j))]
    ops = [a, b]
    if has_add:
        in_specs.append(pl.BlockSpec((tm, tn), lambda i, j: (i, j)))
        ops.append(add)
    return pl.pallas_call(
        body, name=name, grid=(m // tm, n // tn), in_specs=in_specs,
        out_specs=pl.BlockSpec((tm, tn), lambda i, j: (i, j)),
        out_shape=jax.ShapeDtypeStruct((m, n), out_dtype),
        compiler_params=_params(("parallel", "parallel")),
    )(*ops)


def mm_nt(name, a, b, out_dtype=F32):
    m, k = a.shape
    n, k2 = b.shape
    assert k == k2, (name, a.shape, b.shape)
    tm = _pick(m, 2048 if m * k * 2 <= RESIDENT_BYTES else 512, SUBLANES * 2)
    tn = _pick(n, 1024, LANES)

    def body(a_ref, b_ref, o_ref):
        acc = lax.dot_general(a_ref[...].astype(BF16), b_ref[...].astype(BF16), (((1,), (1,)), ((), ())),
                              preferred_element_type=F32)
        o_ref[...] = acc.astype(o_ref.dtype)

    return pl.pallas_call(
        body, name=name, grid=(m // tm, n // tn),
        in_specs=[pl.BlockSpec((tm, k), lambda i, j: (i, 0)), pl.BlockSpec((tn, k), lambda i, j: (j, 0))],
        out_specs=pl.BlockSpec((tm, tn), lambda i, j: (i, j)),
        out_shape=jax.ShapeDtypeStruct((m, n), out_dtype),
        compiler_params=_params(("parallel", "parallel")),
    )(a, b)


def mm_sh(name, a, g, out_dtype=F32, comm=None):
    m, k = a.shape
    nd, k2, nbp = g.shape
    assert k == k2, (name, a.shape, g.shape)
    tm = _pick(m, 2048 if m * k * 2 <= RESIDENT_BYTES else 512, SUBLANES * 2)

    def body(a_ref, b_ref, o_ref):
        o_ref[...] = jnp.dot(a_ref[...].astype(BF16), b_ref[...].astype(BF16), preferred_element_type=F32).astype(o_ref.dtype)

    res = _pallas(
        body, name=name, grid=(m // tm, nd),
        in_specs=[pl.BlockSpec((tm, k), lambda i, j: (i, 0)), pl.BlockSpec((None, k, nbp), lambda i, j: (j, 0, 0))],
        out_specs=[pl.BlockSpec((tm, nbp), lambda i, j: (i, j))],
        out_shape=[jax.ShapeDtypeStruct((m, nd * nbp), out_dtype)], sem=("parallel", "parallel"), comm=comm,
    )(a, g)
    return res[0] if comm is None else res


def mm_sh_nt(name, a, g, add=None, comm=None):
    m, n = a.shape
    nd, k, nbp = g.shape
    assert n == nd * nbp, (name, a.shape, g.shape)
    tm = _pick(m, 512, SUBLANES * 2)
    has_add = add is not None

    def body(a_ref, b_ref, *rest):
        o_ref = rest[-1]
        part = lax.dot_general(a_ref[...].astype(BF16), b_ref[...].astype(BF16), (((1,), (1,)), ((), ())),
                               preferred_element_type=F32)

        @pl.when(pl.program_id(1) == 0)
        def _():
            o_ref[...] = part + rest[0][...] if has_add else part

        @pl.when(pl.program_id(1) != 0)
        def _():
            o_ref[...] += part

    in_specs = [pl.BlockSpec((tm, nbp), lambda i, j: (i, j)), pl.BlockSpec((None, k, nbp), lambda i, j: (j, 0, 0))]
    ops = [a, g]
    if has_add:
        in_specs.append(pl.BlockSpec((tm, k), lambda i, j: (i, 0)))
        ops.append(add)
    res = _pallas(
        body, name=name, grid=(m // tm, nd), in_specs=in_specs,
        out_specs=[pl.BlockSpec((tm, k), lambda i, j: (i, 0))],
        out_shape=[jax.ShapeDtypeStruct((m, k), F32)], sem=("parallel", "arbitrary"), comm=comm,
    )(*ops)
    return res[0] if comm is None else res


def mm_sh_out(name, at, b):
    k, m = at.shape
    m2, n = b.shape
    assert m == m2 and n % N_DEV == 0, (name, at.shape, b.shape)
    nbp = n // N_DEV
    tk = _pick(k, 2048 if k * m * 2 <= RESIDENT_BYTES else 512, SUBLANES * 2)

    def body(a_ref, b_ref, o_ref):
        o_ref[...] = jnp.dot(a_ref[...].astype(BF16), b_ref[...].astype(BF16), preferred_element_type=F32)

    return pl.pallas_call(
        body, name=name, grid=(k // tk, N_DEV),
        in_specs=[pl.BlockSpec((tk, m), lambda i, j: (i, 0)), pl.BlockSpec((m, nbp), lambda i, j: (0, j))],
        out_specs=pl.BlockSpec((None, tk, nbp), lambda i, j: (j, i, 0)),
        out_shape=jax.ShapeDtypeStruct((N_DEV, k, nbp), F32),
        compiler_params=_params(("parallel", "parallel")),
    )(at, b)


def pad_cols(y, nb, nbp):
    m = y.shape[0]
    if nb == nbp:
        return y
    return jnp.pad(y.reshape(m, N_DEV, nb), ((0, 0), (0, 0), (0, nbp - nb))).reshape(m, N_DEV * nbp)


def unpad_cols(y, nb, nbp):
    m = y.shape[0]
    if nb == nbp:
        return y
    return y.reshape(m, N_DEV, nbp)[:, :, :nb].reshape(m, N_DEV * nb)


def _in_spec(kind, a, tm):
    if kind == 'row':
        return pl.BlockSpec((tm, a.shape[1]), lambda h, i: (i, 0))
    if kind == 'hrow':
        return pl.BlockSpec((None, tm, a.shape[2]), lambda h, i: (h, i, 0))
    if kind == 'const':
        return pl.BlockSpec(a.shape, lambda h, i: (0, 0))
    assert kind == 'hconst', kind
    return pl.BlockSpec((None,) + a.shape[1:], lambda h, i: (h, 0, 0))


def _row_out(kind, c, dtype, heads, s, tm):
    if kind == 'row':
        assert heads == 1
        return jax.ShapeDtypeStruct((s, c), dtype), pl.BlockSpec((tm, c), lambda h, i: (i, 0))
    return jax.ShapeDtypeStruct((heads, s, c), dtype), pl.BlockSpec((None, tm, c), lambda h, i: (h, i, 0))


def rowwise(name, fn, arrs, kinds, outs, *, heads, s, tm, comm=None):
    n_in = len(arrs)

    def body(*refs):
        vals = fn(*[r[...] for r in refs[:n_in]])
        for o, v in zip(refs[n_in:], vals, strict=True):
            o[...] = v.astype(o.dtype)

    shapes, specs = zip(*[_row_out(k, c, dt, heads, s, tm) for k, c, dt in outs])
    return _pallas(
        body, name=name, grid=(heads, s // tm),
        in_specs=[_in_spec(k, a, tm) for k, a in zip(kinds, arrs, strict=True)],
        out_specs=list(specs), out_shape=list(shapes), sem=("parallel", "parallel"), comm=comm,
    )(*arrs)


def rowwise_vjp(name, fn, arrs, kinds, cots, cot_kinds, wrt, *, heads, s, tm, out_dtypes=None, primal=False, comm=None):
    n_in, n_cot = len(arrs), len(cots)
    nb = s // tm
    out_dtypes = out_dtypes or [F32] * len(wrt)

    def body(*refs):
        vals = [r[...] for r in refs[:n_in]]
        cvals = tuple(r[...].astype(F32) for r in refs[n_in:n_in + n_cot])
        outs = refs[n_in + n_cot:]

        def f(*dv):
            full = list(vals)
            for j, i in enumerate(wrt):
                full[i] = dv[j]
            return tuple(fn(*full))

        prim, vjp_fn = jax.vjp(f, *[vals[i].astype(F32) for i in wrt])
        grads = vjp_fn(cvals)
        for o, g in zip(outs[:len(wrt)], grads, strict=True):
            o[...] = g.astype(o.dtype)
        if primal:
            for o, p in zip(outs[len(wrt):], prim, strict=True):
                o[...] = p.astype(o.dtype)

    shapes, specs = [], []
    for i, dt in zip(wrt, out_dtypes, strict=True):
        kind, a = kinds[i], arrs[i]
        if kind in ('row', 'hrow'):
            c = a.shape[-1]
            sh, sp = _row_out('row' if (kind == 'row' and heads == 1) else 'hrow', c, dt, heads, s, tm)
        else:
            r, c = a.shape[-2:]
            sh = jax.ShapeDtypeStruct((heads, nb, r, c), dt)
            sp = pl.BlockSpec((None, None, r, c), lambda h, i: (h, i, 0, 0))
        shapes.append(sh)
        specs.append(sp)
    if primal:
        for ck, c in zip(cot_kinds, cots, strict=True):
            sh, sp = _row_out(ck, c.shape[-1], F32, heads, s, tm)
            shapes.append(sh)
            specs.append(sp)
    in_specs = [_in_spec(k, a, tm) for k, a in zip(kinds, arrs, strict=True)]
    in_specs += [_in_spec(k, a, tm) for k, a in zip(cot_kinds, cots, strict=True)]
    return _pallas(
        body, name=name, grid=(heads, nb), in_specs=in_specs, out_specs=specs, out_shape=shapes,
        sem=("parallel", "parallel"), comm=comm,
    )(*arrs, *cots)


def colsum(name, x):
    n, m = x.shape
    tc = _pick(m, 32768, LANES) if m % LANES == 0 else m

    def body(x_ref, o_ref):
        acc = x_ref[0:1, :].astype(F32)
        for r in range(1, n):
            acc = acc + x_ref[r:r + 1, :].astype(F32)
        o_ref[...] = acc

    return pl.pallas_call(
        body, name=name, grid=(m // tc,), in_specs=[pl.BlockSpec((n, tc), lambda j: (0, j))],
        out_specs=pl.BlockSpec((1, tc), lambda j: (0, j)), out_shape=jax.ShapeDtypeStruct((1, m), F32),
        compiler_params=_params(("parallel",)),
    )(x)


def headsum(name, x):
    h, s, c = x.shape
    tm = _pick(s, 256, SUBLANES)

    def body(x_ref, o_ref):
        acc = x_ref[0]
        for j in range(1, h):
            acc = acc + x_ref[j]
        o_ref[...] = acc

    return pl.pallas_call(
        body, name=name, grid=(s // tm,), in_specs=[pl.BlockSpec((h, tm, c), lambda i: (0, i, 0))],
        out_specs=pl.BlockSpec((tm, c), lambda i: (i, 0)), out_shape=jax.ShapeDtypeStruct((s, c), F32),
        compiler_params=_params(("parallel",)),
    )(x)


def sum_all(name, x):
    def body(x_ref, o_ref):
        o_ref[...] = jnp.sum(x_ref[...], keepdims=True)

    return pl.pallas_call(body, name=name, out_shape=jax.ShapeDtypeStruct((1, 1), F32))(x)


def sum_partials(name, p, per_head):
    h, nb, r, c = p.shape
    if per_head:
        flat = jnp.transpose(p, (1, 0, 2, 3)).reshape(nb, h * r * c)
        if nb == 1:
            return flat.reshape(h, r, c)
        return colsum(name, flat).reshape(h, r, c)
    flat = p.reshape(h * nb, r * c)
    if h * nb == 1:
        return flat.reshape(r, c)
    return colsum(name, flat).reshape(r, c)


def _rms_fn(x, g):
    xf = x.astype(F32)
    return (xf * lax.rsqrt(jnp.mean(xf * xf, axis=-1, keepdims=True) + NORM_EPS) * g,)


def _softplus(z):
    return jnp.maximum(z, 0.0) + jnp.log(1.0 + jnp.exp(-jnp.abs(z)))


def _rwkv_pre_fn(hk, hw, ha, hg, w0, w2, a0, a2, g2, k_k, k_a):
    zw = w0 + jnp.dot(jnp.tanh(hw), w2, preferred_element_type=F32)
    w_log = -_softplus(-zw) - 0.5
    decay = jnp.exp(-jnp.exp(w_log))
    a = jax.nn.sigmoid(a0 + jnp.dot(ha, a2, preferred_element_type=F32))
    g = jnp.dot(jax.nn.sigmoid(hg), g2, preferred_element_type=F32)
    kk = hk * k_k
    kk = kk * lax.rsqrt(jnp.maximum(jnp.sum(kk * kk, axis=-1, keepdims=True), 1e-24))
    k = hk * (1.0 + (a - 1.0) * k_a)
    return decay, k, -kk, kk * a, g


def _rwkv_pre_grad_fn(hk, hw, ha, hg, w0, w2, a0, a2, g2, k_k, k_a, hr, hv):
    decay, k, a_sc, b_sc, g = _rwkv_pre_fn(hk, hw, ha, hg, w0, w2, a0, a2, g2, k_k, k_a)
    return decay, k, k, a_sc, b_sc, g, hr, hr, hv, hv


def _rwkv_post_fn(y, r, k, v, g, gn_w, gn_b, r_k):
    mu = jnp.mean(y, axis=-1, keepdims=True)
    var = jnp.mean(jnp.square(y - mu), axis=-1, keepdims=True)
    yn = (y - mu) * lax.rsqrt(var + GN_EPS) * gn_w + gn_b
    bonus = jnp.sum(r * k * r_k, axis=-1, keepdims=True) * v
    return ((yn + bonus) * g,)


def _rope_tables(pos, inv_freq2):
    ang = pos * inv_freq2
    return jnp.cos(ang), jnp.sin(ang)


def _rope(t, cos2, sin2, rot):
    return t * cos2 + jnp.dot(t, rot, precision=HIGHEST, preferred_element_type=F32) * sin2


def _mla_pre_fn(c_q, c_kv, k_pe, pos, q_g, kv_g, inv_freq2, rot):
    cos2, sin2 = _rope_tables(pos, inv_freq2)
    return _rms_fn(c_q, q_g)[0], _rms_fn(c_kv, kv_g)[0], _rope(k_pe, cos2, sin2, rot), cos2, sin2


def _mla_pre_grad_fn(c_q, c_kv, k_pe, pos, q_g, kv_g, inv_freq2, rot):
    return _mla_pre_fn(c_q, c_kv, k_pe, pos, q_g, kv_g, inv_freq2, rot)[:3]


def _rope_q_fn(q_pe, cos2, sin2, rot):
    return (_rope(q_pe, cos2, sin2, rot),)


def _loss_fn(x2, g, target):
    y = _rms_fn(x2, g)[0]
    return (0.5 * jnp.mean(jnp.square(y - target), axis=-1, keepdims=True),)


def _adamw_fn(w, g, m, v):
    m = ADAM_B1 * m + (1.0 - ADAM_B1) * g
    v = ADAM_B2 * v + (1.0 - ADAM_B2) * jnp.square(g)
    m_hat = m / (1.0 - ADAM_B1 ** ADAM_STEP)
    v_hat = v / (1.0 - ADAM_B2 ** ADAM_STEP)
    delta = -ADAM_LR * (m_hat / (jnp.sqrt(v_hat) + ADAM_EPS) + ADAM_WD * w)
    return delta, m, v


def _prev_halo_spec(c, tm):
    return pl.BlockSpec((SUBLANES, c), lambda i: (jnp.maximum(i * (tm // SUBLANES) - 1, 0), 0))


def _next_halo_spec(c, tm, s):
    return pl.BlockSpec((SUBLANES, c), lambda i: (jnp.minimum((i + 1) * (tm // SUBLANES), s // SUBLANES - 1), 0))


def _shift_down(p, halo, first_block, n):
    out = pltpu.roll(p, n, 0)
    row = lax.broadcasted_iota(jnp.int32, p.shape, 0)
    for j in range(n):
        top = jnp.where(first_block, 0.0, halo[SUBLANES - n + j:SUBLANES - n + j + 1, :])
        out = jnp.where(row == j, top, out)
    return out


def _shift_up(p, halo, last_block, n):
    rows = p.shape[0]
    out = pltpu.roll(p, rows - n, 0)
    row = lax.broadcasted_iota(jnp.int32, p.shape, 0)
    for j in range(n):
        bot = jnp.where(last_block, 0.0, halo[j:j + 1, :])
        out = jnp.where(row == rows - n + j, bot, out)
    return out


def token_shift_fwd(p, mu, tm):
    s, c = p.shape

    def body(p_ref, halo_ref, mu_ref, o_ref):
        pv = p_ref[...]
        prev = _shift_down(pv, halo_ref[...], pl.program_id(0) == 0, 1)
        o_ref[...] = pv + (prev - pv) * mu_ref[...]

    return pl.pallas_call(
        body, name="token_shift_fwd", grid=(s // tm,),
        in_specs=[pl.BlockSpec((tm, c), lambda i: (i, 0)), _prev_halo_spec(c, tm), pl.BlockSpec((1, c), lambda i: (0, 0))],
        out_specs=pl.BlockSpec((tm, c), lambda i: (i, 0)), out_shape=jax.ShapeDtypeStruct((s, c), F32),
        compiler_params=_params(("parallel",)),
    )(p, p, mu)


def token_shift_bwd(p, mu, ds, tm):
    s, c = p.shape
    nb = s // tm

    def body(p_ref, halo_ref, mu_ref, ds_ref, dsn_ref, dp_ref, dmu_ref):
        i = pl.program_id(0)
        pv, dsv, muv = p_ref[...], ds_ref[...], mu_ref[...]
        prev = _shift_down(pv, halo_ref[...], i == 0, 1)
        nxt = _shift_up(dsv, dsn_ref[...], i == nb - 1, 1)
        dp_ref[...] = dsv * (1.0 - muv) + nxt * muv
        dmu_ref[...] = jnp.sum(dsv * (prev - pv), axis=0, keepdims=True)

    return pl.pallas_call(
        body, name="token_shift_bwd", grid=(nb,),
        in_specs=[pl.BlockSpec((tm, c), lambda i: (i, 0)), _prev_halo_spec(c, tm), pl.BlockSpec((1, c), lambda i: (0, 0)),
                  pl.BlockSpec((tm, c), lambda i: (i, 0)), _next_halo_spec(c, tm, s)],
        out_specs=[pl.BlockSpec((tm, c), lambda i: (i, 0)), pl.BlockSpec((None, 1, c), lambda i: (i, 0, 0))],
        out_shape=[jax.ShapeDtypeStruct((s, c), F32), jax.ShapeDtypeStruct((nb, 1, c), F32)],
        compiler_params=_params(("parallel",)),
    )(p, p, mu, ds, ds)


def _ffn_tiles(s, f):
    return _pick(s, 256, SUBLANES), _pick(f, 1408, LANES)


def _conv_gate(gp, halo, first_block, cw, cb):
    p1 = _shift_down(gp, halo, first_block, 1)
    p2 = _shift_down(gp, halo, first_block, 2)
    return cw[0:1, :] * p2 + cw[1:2, :] * p1 + cw[2:3, :] * gp + cb, p1, p2


def ffn_act_fwd(gate_pre, up, conv_w, conv_b):
    s, f = gate_pre.shape
    tm, tc = _ffn_tiles(s, f)

    def body(gp_ref, halo_ref, up_ref, cw_ref, cb_ref, o_ref):
        gate, _, _ = _conv_gate(gp_ref[...], halo_ref[...], pl.program_id(0) == 0, cw_ref[...], cb_ref[...])
        o_ref[...] = (gate * jax.nn.sigmoid(gate) * up_ref[...]).astype(o_ref.dtype)

    blk = pl.BlockSpec((tm, tc), lambda i, j: (i, j))
    return pl.pallas_call(
        body, name="ffn_act_fwd", grid=(s // tm, f // tc),
        in_specs=[blk, pl.BlockSpec((SUBLANES, tc), lambda i, j: (jnp.maximum(i * (tm // SUBLANES) - 1, 0), j)), blk,
                  pl.BlockSpec((CONV_W, tc), lambda i, j: (0, j)), pl.BlockSpec((1, tc), lambda i, j: (0, j))],
        out_specs=blk, out_shape=jax.ShapeDtypeStruct((s, f), BF16),
        compiler_params=_params(("parallel", "parallel")),
    )(gate_pre, gate_pre, up, conv_w, conv_b)


def ffn_act_bwd1(gate_pre, up, conv_w, conv_b, d_act):
    s, f = gate_pre.shape
    tm, tc = _ffn_tiles(s, f)
    nb = s // tm

    def body(gp_ref, halo_ref, up_ref, cw_ref, cb_ref, da_ref, dg_ref, du_ref, dcw_ref, dcb_ref):
        gp = gp_ref[...]
        gate, p1, p2 = _conv_gate(gp, halo_ref[...], pl.program_id(0) == 0, cw_ref[...], cb_ref[...])
        sig = jax.nn.sigmoid(gate)
        da = da_ref[...].astype(F32)
        du_ref[...] = (da * gate * sig).astype(du_ref.dtype)
        dg = da * up_ref[...] * (sig * (1.0 + gate * (1.0 - sig)))
        dg_ref[...] = dg
        dcb_ref[...] = jnp.sum(dg, axis=0, keepdims=True)
        dcw_ref[0:1, :] = jnp.sum(dg * p2, axis=0, keepdims=True)
        dcw_ref[1:2, :] = jnp.sum(dg * p1, axis=0, keepdims=True)
        dcw_ref[2:3, :] = jnp.sum(dg * gp, axis=0, keepdims=True)

    blk = pl.BlockSpec((tm, tc), lambda i, j: (i, j))
    return pl.pallas_call(
        body, name="ffn_act_bwd1", grid=(nb, f // tc),
        in_specs=[blk, pl.BlockSpec((SUBLANES, tc), lambda i, j: (jnp.maximum(i * (tm // SUBLANES) - 1, 0), j)), blk,
                  pl.BlockSpec((CONV_W, tc), lambda i, j: (0, j)), pl.BlockSpec((1, tc), lambda i, j: (0, j)), blk],
        out_specs=[blk, blk, pl.BlockSpec((None, CONV_W, tc), lambda i, j: (i, 0, j)),
                   pl.BlockSpec((None, 1, tc), lambda i, j: (i, 0, j))],
        out_shape=[jax.ShapeDtypeStruct((s, f), F32), jax.ShapeDtypeStruct((s, f), BF16),
                   jax.ShapeDtypeStruct((nb, CONV_W, f), F32), jax.ShapeDtypeStruct((nb, 1, f), F32)],
        compiler_params=_params(("parallel", "parallel")),
    )(gate_pre, gate_pre, up, conv_w, conv_b, d_act)


def ffn_act_bwd2(d_gate, conv_w):
    s, f = d_gate.shape
    tm, tc = _ffn_tiles(s, f)
    nb = s // tm

    def body(dg_ref, halo_ref, cw_ref, o_ref):
        dg, cw = dg_ref[...], cw_ref[...]
        last = pl.program_id(0) == nb - 1
        n1 = _shift_up(dg, halo_ref[...], last, 1)
        n2 = _shift_up(dg, halo_ref[...], last, 2)
        o_ref[...] = (cw[2:3, :] * dg + cw[1:2, :] * n1 + cw[0:1, :] * n2).astype(o_ref.dtype)

    blk = pl.BlockSpec((tm, tc), lambda i, j: (i, j))
    return pl.pallas_call(
        body, name="ffn_act_bwd2", grid=(nb, f // tc),
        in_specs=[blk, pl.BlockSpec((SUBLANES, tc), lambda i, j: (jnp.minimum((i + 1) * (tm // SUBLANES), s // SUBLANES - 1), j)),
                  pl.BlockSpec((CONV_W, tc), lambda i, j: (0, j))],
        out_specs=blk, out_shape=jax.ShapeDtypeStruct((s, f), BF16),
        compiler_params=_params(("parallel", "parallel")),
    )(d_gate, d_gate, conv_w)


def _mxu(x, y, cx, cy):
    if x.ndim == 3:
        return lax.dot_general(x, y, (((cx + 1,), (cy + 1,)), ((0,), (0,))), preferred_element_type=F32)
    return lax.dot_general(x, y, (((cx,), (cy,)), ((), ())), preferred_element_type=F32)


def _split(x):
    hi = x.astype(BF16)
    return hi, (x - hi.astype(F32)).astype(BF16)


def _make_dot3(cx, cy, passes):
    @jax.custom_vjp
    def f(x, y):
        if passes == 1:
            return _mxu(x.astype(BF16), y.astype(BF16), cx, cy)
        xh, xl = _split(x)
        yh, yl = _split(y)
        return _mxu(xh, yh, cx, cy) + (_mxu(xh, yl, cx, cy) + _mxu(xl, yh, cx, cy))

    def fwd(x, y):
        return f(x, y), (x, y)

    def bwd(res, g):
        x, y = res
        dx = dot3(g, y, 1, 1 - cy, passes) if cx == 1 else dot3(y, g, 1 - cy, 1, passes)
        dy = dot3(x, g, 1 - cx, 0, passes) if cy == 0 else dot3(g, x, 0, 1 - cx, passes)
        return dx, dy

    f.defvjp(fwd, bwd)
    return f


_DOT3 = {}


def dot3(x, y, cx, cy, passes=3):
    if (cx, cy, passes) not in _DOT3:
        _DOT3[(cx, cy, passes)] = _make_dot3(cx, cy, passes)
    return _DOT3[(cx, cy, passes)](x, y)


def _dot(x, y, passes=3):
    return dot3(x, y, 1, 0, passes)


def _dot_nt(x, y, passes=3):
    return dot3(x, y, 1, 1, passes)


def _dot_tn(x, y, passes=3):
    return dot3(x, y, 0, 0, passes)


def _tri_sum(x, lower):
    t = x.shape[-2]
    row = lax.broadcasted_iota(jnp.int32, (t, t), 0)
    col = lax.broadcasted_iota(jnp.int32, (t, t), 1)
    tri = jnp.where((col <= row) if lower else (col >= row), 1.0, 0.0).astype(BF16)
    if x.ndim == 3:
        tri = jnp.broadcast_to(tri[None], (x.shape[0], t, t))
    hi = x.astype(BF16)
    rest = x - hi.astype(F32)
    mid = rest.astype(BF16)
    low = (rest - mid.astype(F32)).astype(BF16)
    return _mxu(tri, hi, 1, 0) + (_mxu(tri, mid, 1, 0) + _mxu(tri, low, 1, 0))


@jax.custom_vjp
def _cumsum_rows(x):
    return _tri_sum(x, True)


_cumsum_rows.defvjp(lambda x: (_tri_sum(x, True), None), lambda _, g: (_tri_sum(g, False),))


def _scan_chunk(s0, r, w, k, v, a, b):
    t = r.shape[1]
    row = lax.broadcasted_iota(jnp.int32, (1, t, t), 1)
    col = lax.broadcasted_iota(jnp.int32, (1, t, t), 2)
    strict, incl = col < row, col <= row
    logw = jnp.log(w)
    cum = _cumsum_rows(logw)
    w_in, w_ex, w_inv = jnp.exp(cum), jnp.exp(cum - logw), jnp.exp(-cum)
    w_all = jnp.exp(jnp.sum(logw, axis=1, keepdims=True))
    at, rt, kt, bt = a * w_ex, r * w_in, k * w_inv, b * w_inv
    ps, po = SCAN_PASSES_SOLVE, SCAN_PASSES_OUT
    a_ab = jnp.where(strict, _dot_nt(at, bt, ps), 0.0)
    a_ak = jnp.where(strict, _dot_nt(at, kt, ps), 0.0)
    a_rk = jnp.where(incl, _dot_nt(rt, kt, po), 0.0)
    a_rb = jnp.where(incl, _dot_nt(rt, bt, po), 0.0)
    u = _dot_nt(at, s0, ps) + _dot(a_ak, v, ps)
    p = a_ab
    steps = int(math.log2(t))
    assert 2 ** steps == t
    for j in range(steps):
        u = u + _dot(p, u, ps)
        if j < steps - 1:
            p = _dot(p, p, ps)
    y = _dot_nt(rt, s0, po) + _dot(a_rk, v, po) + _dot(a_rb, u, po)
    s_new = s0 * w_all + _dot_tn(v, kt * w_all, po) + _dot_tn(u, bt * w_all, po)
    return y, s_new


def scan_fwd(r, w, k, v, a, b, comm=None):
    h, s, n = r.shape
    t = min(SCAN_CHUNK, s)
    nc = s // t

    hb = SCAN_HEADS if h % SCAN_HEADS == 0 else 1

    def body(r_ref, w_ref, k_ref, v_ref, a_ref, b_ref, y_ref, ck_ref, st_ref):
        @pl.when(pl.program_id(1) == 0)
        def _():
            st_ref[...] = jnp.zeros_like(st_ref)

        s0 = st_ref[...]
        ck_ref[...] = s0
        y, s_new = _scan_chunk(s0, r_ref[...], w_ref[...], k_ref[...], v_ref[...], a_ref[...], b_ref[...])
        y_ref[...] = y
        st_ref[...] = s_new

    blk = pl.BlockSpec((hb, t, n), lambda hh, c: (hh, c, 0))
    return _pallas(
        body, name="rwkv_scan_fwd", grid=(h // hb, nc), in_specs=[blk] * 6,
        out_specs=[blk, pl.BlockSpec((hb, None, n, n), lambda hh, c: (hh, c, 0, 0))],
        out_shape=[jax.ShapeDtypeStruct((h, s, n), F32), jax.ShapeDtypeStruct((h, nc, n, n), F32)],
        scratch_shapes=[pltpu.VMEM((hb, n, n), F32)], sem=("parallel", "arbitrary"), comm=comm,
    )(r, w, k, v, a, b)


def scan_bwd(r, w, k, v, a, b, ck, dy, comm=None):
    h, s, n = r.shape
    t = min(SCAN_CHUNK, s)
    nc = s // t

    hb = SCAN_HEADS if h % SCAN_HEADS == 0 else 1

    def body(r_ref, w_ref, k_ref, v_ref, a_ref, b_ref, ck_ref, dy_ref, dr_ref, dw_ref, dk_ref, dv_ref, da_ref, db_ref, ds_ref):
        @pl.when(pl.program_id(1) == 0)
        def _():
            ds_ref[...] = jnp.zeros_like(ds_ref)

        _, vjp_fn = jax.vjp(_scan_chunk, ck_ref[...], r_ref[...], w_ref[...], k_ref[...], v_ref[...], a_ref[...], b_ref[...])
        ds0, dr, dw, dk, dv, da, db = vjp_fn((dy_ref[...], ds_ref[...]))
        ds_ref[...] = ds0
        dr_ref[...], dw_ref[...], dk_ref[...], dv_ref[...], da_ref[...], db_ref[...] = dr, dw, dk, dv, da, db

    blk = pl.BlockSpec((hb, t, n), lambda hh, c: (hh, nc - 1 - c, 0))
    return _pallas(
        body, name="rwkv_scan_bwd", grid=(h // hb, nc),
        in_specs=[blk] * 6 + [pl.BlockSpec((hb, None, n, n), lambda hh, c: (hh, nc - 1 - c, 0, 0)), blk],
        out_specs=[blk] * 6, out_shape=[jax.ShapeDtypeStruct((h, s, n), F32)] * 6,
        scratch_shapes=[pltpu.VMEM((hb, n, n), F32)], sem=("parallel", "arbitrary"), comm=comm,
    )(r, w, k, v, a, b, ck, dy)


def _attn_scores(qn, qp, kn, kp, q0):
    scale = (QK_NOPE + QK_ROPE) ** -0.5
    sc = lax.dot_general(qn.astype(BF16), kn.astype(BF16), (((1,), (1,)), ((), ())), preferred_element_type=F32)
    sc = sc + lax.dot_general(qp.astype(BF16), kp.astype(BF16), (((1,), (1,)), ((), ())), preferred_element_type=F32)
    row = q0 + lax.broadcasted_iota(jnp.int32, sc.shape, 0)
    col = lax.broadcasted_iota(jnp.int32, sc.shape, 1)
    return jnp.where(row >= col, sc * scale, NEG_INF), scale


def attn_fwd(qn, qp, kn, kp, v, comm=None):
    h, s, _ = qn.shape
    tq = _pick(s, 256, SUBLANES)

    def body(qn_ref, qp_ref, kn_ref, kp_ref, v_ref, o_ref, lse_ref):
        sc, _ = _attn_scores(qn_ref[...], qp_ref[...], kn_ref[...], kp_ref[...], pl.program_id(1) * tq)
        mx = jnp.max(sc, axis=-1, keepdims=True)
        e = jnp.exp(sc - mx)
        den = jnp.sum(e, axis=-1, keepdims=True)
        p = e / den
        o_ref[...] = jnp.dot(p.astype(BF16), v_ref[...].astype(BF16), preferred_element_type=F32)
        lse_ref[...] = mx + jnp.log(den)

    qblk = lambda c: pl.BlockSpec((None, tq, c), lambda hh, i: (hh, i, 0))
    kblk = lambda c: pl.BlockSpec((None, s, c), lambda hh, i: (hh, 0, 0))
    return _pallas(
        body, name="mla_attn_fwd", grid=(h, s // tq),
        in_specs=[qblk(QK_NOPE), qblk(QK_ROPE), kblk(QK_NOPE), pl.BlockSpec((s, QK_ROPE), lambda hh, i: (0, 0)), kblk(V_HEAD)],
        out_specs=[qblk(V_HEAD), qblk(1)],
        out_shape=[jax.ShapeDtypeStruct((h, s, V_HEAD), F32), jax.ShapeDtypeStruct((h, s, 1), F32)],
        sem=("parallel", "parallel"), comm=comm,
    )(qn, qp, kn, kp, v)


def attn_bwd(qn, qp, kn, kp, v, o, lse, do, comm=None):
    h, s, _ = qn.shape
    tq = _pick(s, 256, SUBLANES)

    def body(qn_ref, qp_ref, kn_ref, kp_ref, v_ref, o_ref, lse_ref, do_ref, dqn_ref, dqp_ref, dkn_ref, dv_ref, dkp_ref):
        @pl.when(pl.program_id(1) == 0)
        def _():
            dkn_ref[...] = jnp.zeros_like(dkn_ref)
            dv_ref[...] = jnp.zeros_like(dv_ref)
            dkp_ref[...] = jnp.zeros_like(dkp_ref)

        qn_b, qp_b = qn_ref[...].astype(BF16), qp_ref[...].astype(BF16)
        kn_b, kp_b, v_b = kn_ref[...].astype(BF16), kp_ref[...].astype(BF16), v_ref[...].astype(BF16)
        sc, scale = _attn_scores(qn_b, qp_b, kn_b, kp_b, pl.program_id(1) * tq)
        p = jnp.exp(sc - lse_ref[...])
        dov = do_ref[...]
        do_b = dov.astype(BF16)
        p_b = p.astype(BF16)
        dv_ref[...] += lax.dot_general(p_b, do_b, (((0,), (0,)), ((), ())), preferred_element_type=F32)
        dp = lax.dot_general(do_b, v_b, (((1,), (1,)), ((), ())), preferred_element_type=F32)
        delta = jnp.sum(dov * o_ref[...], axis=-1, keepdims=True)
        ds = (p * (dp - delta) * scale).astype(BF16)
        dqn_ref[...] = jnp.dot(ds, kn_b, preferred_element_type=F32)
        dqp_ref[...] = jnp.dot(ds, kp_b, preferred_element_type=F32)
        dkn_ref[...] += lax.dot_general(ds, qn_b, (((0,), (0,)), ((), ())), preferred_element_type=F32)
        dkp_ref[...] += lax.dot_general(ds, qp_b, (((0,), (0,)), ((), ())), preferred_element_type=F32)

    qblk = lambda c: pl.BlockSpec((None, tq, c), lambda hh, i: (hh, i, 0))
    kblk = lambda c: pl.BlockSpec((None, s, c), lambda hh, i: (hh, 0, 0))
    return _pallas(
        body, name="mla_attn_bwd", grid=(h, s // tq),
        in_specs=[qblk(QK_NOPE), qblk(QK_ROPE), kblk(QK_NOPE), pl.BlockSpec((s, QK_ROPE), lambda hh, i: (0, 0)), kblk(V_HEAD),
                  qblk(V_HEAD), qblk(1), qblk(V_HEAD)],
        out_specs=[qblk(QK_NOPE), qblk(QK_ROPE), kblk(QK_NOPE), kblk(V_HEAD), kblk(QK_ROPE)],
        out_shape=[jax.ShapeDtypeStruct((h, s, QK_NOPE), F32), jax.ShapeDtypeStruct((h, s, QK_ROPE), F32),
                   jax.ShapeDtypeStruct((h, s, QK_NOPE), F32), jax.ShapeDtypeStruct((h, s, V_HEAD), F32),
                   jax.ShapeDtypeStruct((h, s, QK_ROPE), F32)],
        sem=("parallel", "arbitrary"), comm=comm,
    )(qn, qp, kn, kp, v, o, lse, do)


def _my_pos():
    return lax.axis_index("x"), lax.axis_index("y"), lax.axis_index("c")


def _dev_index(px, py, pc):
    return 4 * px + 2 * py + pc


def all_gather(name, shard):
    r, c = shard.shape

    def body(x_ref, out_ref, send_sems, recv_sems, local_sem):
        x, y, cc = _my_pos()
        me, sibling = (x, y, cc), (x, y, 1 - cc)
        chips = [(1 - x, y), (x, 1 - y), (1 - x, 1 - y)]

        def rows(px, py, pc):
            return out_ref.at[_dev_index(px, py, pc)]

        def copy(kk, block, to, src=None):
            return pltpu.make_async_remote_copy(
                src_ref=rows(*block) if src is None else src, dst_ref=rows(*block),
                send_sem=send_sems.at[kk], recv_sem=recv_sems.at[kk], device_id=to, device_id_type=MESH)

        mine = pltpu.make_async_copy(x_ref, rows(*me), local_sem)
        mine.start()
        first = [copy(0, me, sibling, src=x_ref)]
        first += [copy(1 + j, me, (*chip, cc), src=x_ref) for j, chip in enumerate(chips)]
        for cp in first:
            cp.start()
        passed = [copy(4 + j, (*chip, cc), sibling) for j, chip in enumerate(chips)]
        for j, chip in enumerate(chips):
            copy(1 + j, (*chip, cc), me).wait_recv()
            passed[j].start()
        copy(0, sibling, me).wait_recv()
        for j, chip in enumerate(chips):
            copy(4 + j, (*chip, 1 - cc), me).wait_recv()
        for cp in first + passed:
            cp.wait_send()
        mine.wait()

    return pl.pallas_call(
        body, name=name, out_shape=jax.ShapeDtypeStruct((N_DEV, r, c), shard.dtype),
        in_specs=[pl.BlockSpec(memory_space=pl.ANY)], out_specs=pl.BlockSpec(memory_space=pl.ANY),
        scratch_shapes=[pltpu.SemaphoreType.DMA((7,)), pltpu.SemaphoreType.DMA((7,)), pltpu.SemaphoreType.DMA],
    )(shard)


def _flip(kind):
    x, y, c = _my_pos()
    return {'c': (x, y, 1 - c), 'x': (1 - x, y, c), 'y': (x, 1 - y, c), 'xy': (1 - x, 1 - y, c)}[kind]


def exchange_sibling(name, g):
    _, r, c = g.shape

    def body(g_ref, out_ref, send_sems, recv_sems):
        x, y, cc = _my_pos()
        copies = []
        for px in range(2):
            for py in range(2):
                slot = 2 * px + py
                copies.append(pltpu.make_async_remote_copy(
                    src_ref=g_ref.at[_dev_index(px, py, 1 - cc)], dst_ref=out_ref.at[slot],
                    send_sem=send_sems.at[slot], recv_sem=recv_sems.at[slot], device_id=(x, y, 1 - cc), device_id_type=MESH))
        for cp in copies:
            cp.start()
        for cp in copies:
            cp.wait()

    return pl.pallas_call(
        body, name=name, out_shape=jax.ShapeDtypeStruct((4, r, c), g.dtype),
        in_specs=[pl.BlockSpec(memory_space=pl.ANY)], out_specs=pl.BlockSpec(memory_space=pl.ANY),
        scratch_shapes=[pltpu.SemaphoreType.DMA((4,)), pltpu.SemaphoreType.DMA((4,))],
    )(g)


def exchange_chips(name, hsum):
    _, r, c = hsum.shape

    def body(h_ref, out_ref, send_sems, recv_sems):
        x, y, cc = _my_pos()
        copies = []
        for j, (px, py) in enumerate([(1 - x, y), (x, 1 - y), (1 - x, 1 - y)]):
            copies.append(pltpu.make_async_remote_copy(
                src_ref=h_ref.at[2 * px + py], dst_ref=out_ref.at[j],
                send_sem=send_sems.at[j], recv_sem=recv_sems.at[j], device_id=(px, py, cc), device_id_type=MESH))
        for cp in copies:
            cp.start()
        for cp in copies:
            cp.wait()

    return pl.pallas_call(
        body, name=name, out_shape=jax.ShapeDtypeStruct((3, r, c), hsum.dtype),
        in_specs=[pl.BlockSpec(memory_space=pl.ANY)], out_specs=pl.BlockSpec(memory_space=pl.ANY),
        scratch_shapes=[pltpu.SemaphoreType.DMA((3,)), pltpu.SemaphoreType.DMA((3,))],
    )(hsum)


def add_slots(name, *terms):
    n, r, c = terms[0].shape
    tr = _pick(r, 512, SUBLANES)

    def body(*refs):
        acc = refs[0][...]
        for t in refs[1:-1]:
            acc = acc + t[...]
        refs[-1][...] = acc

    blk = pl.BlockSpec((None, tr, c), lambda s_, i: (s_, i, 0))
    return pl.pallas_call(
        body, name=name, grid=(n, r // tr), in_specs=[blk] * len(terms), out_specs=blk,
        out_shape=jax.ShapeDtypeStruct((n, r, c), F32), compiler_params=_params(("parallel", "parallel")),
    )(*terms)


def _rs_add_sibling(name, g, from_sibling, cc):
    _, r, c = g.shape
    tr = _pick(r, 512, SUBLANES * 2)

    def body(cc_ref, g_ref, s_ref, o_ref, ob_ref):
        tot = g_ref[...] + s_ref[...]
        o_ref[...] = tot
        ob_ref[...] = tot.astype(BF16)

    blk = pl.BlockSpec((None, tr, c), lambda s_, i, cc_ref: (s_, i, 0))
    return pl.pallas_call(
        body, name=name,
        grid_spec=pltpu.PrefetchScalarGridSpec(
            num_scalar_prefetch=1, grid=(4, r // tr),
            in_specs=[pl.BlockSpec((None, None, tr, c), lambda s_, i, cc_ref: (s_, cc_ref[0], i, 0)), blk], out_specs=[blk, blk]),
        out_shape=[jax.ShapeDtypeStruct((4, r, c), F32), jax.ShapeDtypeStruct((4, r, c), BF16)],
        compiler_params=_params(("parallel", "parallel")),
    )(cc.reshape(1).astype(jnp.int32), g.reshape(4, 2, r, c), from_sibling)


def _rs_add_chips(name, chip_sum, from_chips, slot):
    _, r, c = chip_sum.shape
    tr = _pick(r, 512, SUBLANES * 2)

    def body(slot_ref, h_ref, f0_ref, f1_ref, f2_ref, o_ref):
        o_ref[...] = ((h_ref[...] + f0_ref[...].astype(F32)) + f1_ref[...].astype(F32)) + f2_ref[...].astype(F32)

    def from_blk(j):
        return pl.BlockSpec((None, tr, c), lambda i, slot_ref: (j, i, 0))

    return pl.pallas_call(
        body, name=name,
        grid_spec=pltpu.PrefetchScalarGridSpec(
            num_scalar_prefetch=1, grid=(r // tr,),
            in_specs=[pl.BlockSpec((None, tr, c), lambda i, slot_ref: (slot_ref[0], i, 0)), from_blk(0), from_blk(1), from_blk(2)],
            out_specs=pl.BlockSpec((tr, c), lambda i, slot_ref: (i, 0))),
        out_shape=jax.ShapeDtypeStruct((r, c), F32), compiler_params=_params(("parallel",)),
    )(slot.reshape(1).astype(jnp.int32), chip_sum, from_chips, from_chips, from_chips)


def rs_chip_sum(tag, g):
    _, _, cc = _my_pos()
    from_sibling = exchange_sibling("rs_sibling_" + tag, g)
    return _rs_add_sibling("rs_add_sibling_" + tag, g, from_sibling, cc)


def rs_finish(tag, chip_sum, from_chips):
    x, y, _ = _my_pos()
    return _rs_add_chips("rs_add_chips_" + tag, chip_sum, from_chips, 2 * x + y)


def reduce_scatter(tag, g):
    chip_sum, chip_sum_b = rs_chip_sum(tag, g)
    return rs_finish(tag, chip_sum, exchange_chips("rs_chips_" + tag, chip_sum_b))


class GatherIci:
    def __init__(self, shards):
        self.inputs = list(shards)
        self.out_shapes = [jax.ShapeDtypeStruct((N_DEV,) + s.shape, s.dtype) for s in shards]
        self.n_remote, self.n_local = 3 * len(shards), len(shards)

    def make(self, cins, couts, send, recv, local):
        x, y, cc = _my_pos()
        me = _dev_index(x, y, cc)
        copies = []
        for w, (src, out) in enumerate(zip(cins, couts, strict=True)):
            copies.append(pltpu.make_async_copy(src, out.at[me], local.at[w]))
            for j, (px, py) in enumerate([(1 - x, y), (x, 1 - y), (1 - x, 1 - y)]):
                copies.append(pltpu.make_async_remote_copy(
                    src_ref=src, dst_ref=out.at[me], send_sem=send.at[3 * w + j], recv_sem=recv.at[3 * w + j],
                    device_id=(px, py, cc), device_id_type=MESH))
        return copies


class RsChips:
    def __init__(self, chip_sums):
        self.inputs = list(chip_sums)
        self.out_shapes = [jax.ShapeDtypeStruct((3,) + h.shape[1:], h.dtype) for h in chip_sums]
        self.n_remote, self.n_local = 3 * len(chip_sums), 0

    def make(self, cins, couts, send, recv, local):
        x, y, cc = _my_pos()
        copies = []
        for w, (h_ref, out) in enumerate(zip(cins, couts, strict=True)):
            for j, (px, py) in enumerate([(1 - x, y), (x, 1 - y), (1 - x, 1 - y)]):
                copies.append(pltpu.make_async_remote_copy(
                    src_ref=h_ref.at[2 * px + py], dst_ref=out.at[j], send_sem=send.at[3 * w + j], recv_sem=recv.at[3 * w + j],
                    device_id=(px, py, cc), device_id_type=MESH))
        return copies


def gather_d2d(name, arrays):
    n = len(arrays)

    def body(*refs):
        outs, send, recv = refs[n:2 * n], refs[2 * n], refs[2 * n + 1]
        x, y, cc = _my_pos()
        copies = []
        for w, out in enumerate(outs):
            for px in range(2):
                for py in range(2):
                    q = 4 * w + 2 * px + py
                    slab = out.at[_dev_index(px, py, cc)]
                    copies.append(pltpu.make_async_remote_copy(
                        src_ref=slab, dst_ref=slab, send_sem=send.at[q], recv_sem=recv.at[q],
                        device_id=(x, y, 1 - cc), device_id_type=MESH))
        for cp in copies:
            cp.start()
        for cp in copies:
            cp.wait()

    any_spec = pl.BlockSpec(memory_space=pl.ANY)
    return pl.pallas_call(
        body, name=name, out_shape=[jax.ShapeDtypeStruct(a.shape, a.dtype) for a in arrays],
        in_specs=[any_spec] * n, out_specs=[any_spec] * n, input_output_aliases={i: i for i in range(n)},
        scratch_shapes=[pltpu.SemaphoreType.DMA((4 * n,)), pltpu.SemaphoreType.DMA((4 * n,))],
    )(*arrays)


def _pallas(body, *, name, grid, in_specs, out_specs, out_shape, scratch_shapes=(), sem, comm=None):
    in_specs, out_specs, out_shape, scratch_shapes = list(in_specs), list(out_specs), list(out_shape), list(scratch_shapes)
    if comm is None:
        return pl.pallas_call(body, name=name, grid=grid, in_specs=in_specs, out_specs=out_specs, out_shape=out_shape,
                              scratch_shapes=scratch_shapes, compiler_params=_params(sem))
    n_in, n_out, n_scr = len(in_specs), len(out_specs), len(scratch_shapes)
    nci, nco = len(comm.inputs), len(comm.out_shapes)

    def body2(*refs):
        ins, cins = refs[:n_in], refs[n_in:n_in + nci]
        o0 = n_in + nci
        outs, couts = refs[o0:o0 + n_out], refs[o0 + n_out:o0 + n_out + nco]
        s0 = o0 + n_out + nco
        scr = refs[s0:s0 + n_scr]
        send, recv, local = refs[s0 + n_scr:]
        pids = [pl.program_id(k) for k in range(len(grid))]
        first = functools.reduce(jnp.logical_and, [p == 0 for p in pids])
        last = functools.reduce(jnp.logical_and, [p == g - 1 for p, g in zip(pids, grid)])

        @pl.when(first)
        def _():
            for cp in comm.make(cins, couts, send, recv, local):
                cp.start()

        body(*ins, *outs, *scr)

        @pl.when(last)
        def _():
            for cp in comm.make(cins, couts, send, recv, local):
                cp.wait()

    any_spec = pl.BlockSpec(memory_space=pl.ANY)
    call = pl.pallas_call(
        body2, name=name, grid=grid, in_specs=in_specs + [any_spec] * nci, out_specs=out_specs + [any_spec] * nco,
        out_shape=out_shape + list(comm.out_shapes),
        scratch_shapes=scratch_shapes + [pltpu.SemaphoreType.DMA((comm.n_remote,)), pltpu.SemaphoreType.DMA((comm.n_remote,)),
                                         pltpu.SemaphoreType.DMA((max(comm.n_local, 1),))],
        compiler_params=_params(tuple("arbitrary" for _ in grid)))
    return lambda *args: call(*args, *comm.inputs)


PACK_W = 1024


class Pack:
    def __init__(self, entries, row_unit):
        self.entries = entries
        self.sizes = [int(np.prod(sh)) for _, sh in entries]
        self.offsets = np.concatenate([[0], np.cumsum(self.sizes)]).tolist()
        self.total = _round_up(self.offsets[-1], PACK_W * row_unit)
        self.rows = self.total // PACK_W

    def pack(self, arrays, dtype, lead=()):
        flat = [arrays[n].astype(dtype).reshape(lead + (-1,)) for n, _ in self.entries]
        pad = self.total - self.offsets[-1]
        if pad:
            flat.append(jnp.zeros(lead + (pad,), dtype))
        return jnp.concatenate(flat, axis=-1).reshape(lead + (self.rows, PACK_W))

    def unpack(self, buf, lead=()):
        flat = buf.reshape(lead + (self.total,))
        out = {}
        for (n, sh), off, sz in zip(self.entries, self.offsets, self.sizes):
            out[n] = lax.slice_in_dim(flat, off, off + sz, axis=len(lead)).reshape(lead + tuple(sh))
        return out


def _gathered_to_full(g, how):
    _, a, b = g.shape
    if how == 'row':
        return g.reshape(N_DEV * a, b)
    return jnp.transpose(g, (1, 0, 2)).reshape(a, N_DEV * b)


def _full_to_shards(w, how):
    a, b = w.shape
    if how == 'row':
        return w.reshape(N_DEV, a // N_DEV, b)
    return jnp.transpose(w.reshape(a, N_DEV, b // N_DEV), (1, 0, 2))


def _to_heads(t, width):
    s, c = t.shape
    return jnp.transpose(t.reshape(s, c // width, width), (1, 0, 2))


def _from_heads(t):
    h, s, w = t.shape
    return jnp.transpose(t, (1, 0, 2)).reshape(s, h * w)


def _rot_matrix():
    half = QK_ROPE // 2
    rot = np.zeros((QK_ROPE, QK_ROPE), np.float32)
    for i in range(half):
        rot[i + half, i] = -1.0
        rot[i, i + half] = 1.0
    return jnp.asarray(rot)


def _inv_freq2():
    half = QK_ROPE // 2
    inv = ROPE_THETA ** (-np.arange(half, dtype=np.float32) / half)
    return jnp.asarray(np.concatenate([inv, inv])[None, :].astype(np.float32))


def kernel(x, positions, attn_norm_g, w_in, rwkv_mu, rwkv_w0, rwkv_w2, rwkv_a0, rwkv_a2, rwkv_g2, rwkv_k_k, rwkv_k_a, rwkv_r_k, rwkv_gn_w, rwkv_gn_b, mla_q_norm_g, mla_w_uq, mla_kv_norm_g, mla_w_ukv, w_out, ffn_norm_g, ffn_w_gate, ffn_w_up, ffn_conv_w, ffn_conv_b, ffn_w_down, final_norm_g, loss_target, m_attn_norm_g, m_w_in, m_rwkv_mu, m_rwkv_w0, m_rwkv_w2, m_rwkv_a0, m_rwkv_a2, m_rwkv_g2, m_rwkv_k_k, m_rwkv_k_a, m_rwkv_r_k, m_rwkv_gn_w, m_rwkv_gn_b, m_mla_q_norm_g, m_mla_w_uq, m_mla_kv_norm_g, m_mla_w_ukv, m_w_out, m_ffn_norm_g, m_ffn_w_gate, m_ffn_w_up, m_ffn_conv_w, m_ffn_conv_b, m_ffn_w_down, m_final_norm_g, v_attn_norm_g, v_w_in, v_rwkv_mu, v_rwkv_w0, v_rwkv_w2, v_rwkv_a0, v_rwkv_a2, v_rwkv_g2, v_rwkv_k_k, v_rwkv_k_a, v_rwkv_r_k, v_rwkv_gn_w, v_rwkv_gn_b, v_mla_q_norm_g, v_mla_w_uq, v_mla_kv_norm_g, v_mla_w_ukv, v_w_out, v_ffn_norm_g, v_ffn_w_gate, v_ffn_w_up, v_ffn_conv_w, v_ffn_conv_b, v_ffn_w_down, v_final_norm_g):
    given = dict(locals())
    wts = {n: given[n] for n in WEIGHTS}
    mom_m = {n: given["m_" + n] for n in WEIGHTS}
    mom_v = {n: given["v_" + n] for n in WEIGHTS}
    out_shapes = {n: wts[n].shape for n in WEIGHTS}

    def local2d(n, a):
        if n == 'rwkv_r_k' or a.ndim <= 2:
            return a.reshape(1, -1)
        return a.reshape(a.shape[1:])

    w2d = {n: local2d(n, wts[n]) for n in WEIGHTS}
    m2d = {n: local2d(n, mom_m[n]) for n in WEIGHTS}
    v2d = {n: local2d(n, mom_v[n]) for n in WEIGHTS}

    xs = x.reshape(x.shape[1:])
    tgt = loss_target.reshape(loss_target.shape[1:])
    s, d = xs.shape
    c_rwkv = w2d['rwkv_w0'].shape[1]
    n_rh = c_rwkv // RWKV_HEAD
    decay_lora, aaa_lora, gate_lora = w2d['rwkv_w2'].shape[0], w2d['rwkv_a2'].shape[0], w2d['rwkv_g2'].shape[0]
    q_lora, kv_lora = w2d['mla_q_norm_g'].shape[1], w2d['mla_kv_norm_g'].shape[1]
    shift_dim = w2d['rwkv_mu'].shape[1]
    d_in = w2d['w_in'].shape[1] * N_DEV
    d_in_pad = _round_up(d_in, LANES)
    n_mh = w2d['mla_w_uq'].shape[1] * N_DEV // (QK_NOPE + QK_ROPE)
    d_ff = w2d['ffn_conv_b'].shape[1]
    tm = _pick(s, 256, SUBLANES)
    tm_wide = _pick(s, 128, SUBLANES)

    nb = {n: w2d[n].shape[1] for n in BIG if BIG[n] == 'col'}
    nbp = {n: _round_up(v_, LANES) for n, v_ in nb.items()}
    shards = {}
    for n in BIG:
        w = w2d[n].astype(BF16)
        if BIG[n] == 'col':
            w = jnp.pad(w, ((0, 0), (0, nbp[n] - nb[n])))
        elif n == 'ffn_w_down':
            w = jnp.pad(w, ((0, nbp['ffn_w_gate'] - w.shape[0]), (0, 0)))
        shards[n] = w

    def as_used(n, g):
        return g if BIG[n] == 'col' else g.reshape(N_DEV * g.shape[1], g.shape[2])

    gathered = {'w_in': as_used('w_in', all_gather("gather_w_in", shards['w_in']))}
    later = [n for n in BIG if n != 'w_in']
    f_pad = N_DEV * nbp['ffn_w_gate']
    small_pack = Pack([(n, w2d[n].shape) for n in SMALL_SHARDED], 8)
    small_all = all_gather("gather_small", small_pack.pack(w2d, F32))
    full = {}
    for n, g in small_pack.unpack(small_all, lead=(N_DEV,)).items():
        full[n] = _gathered_to_full(g, SMALL_SHARDED[n])
    conv_w_pad = pad_cols(full['ffn_conv_w'], nb['ffn_w_gate'], nbp['ffn_w_gate'])
    conv_b_pad = pad_cols(w2d['ffn_conv_b'], nb['ffn_w_gate'], nbp['ffn_w_gate'])

    (h1,) = rowwise("rms_attn", _rms_fn, [xs, w2d['attn_norm_g']], ['row', 'const'], [('row', d, BF16)], heads=1, s=s, tm=tm)
    def gather_behind(names, run):
        *res, = run(GatherIci([shards[n] for n in names]))
        landed = res[len(res) - len(names):]
        for n, g in zip(names, gather_d2d("gather_d2d_" + names[0], landed), strict=True):
            gathered[n] = as_used(n, g)
        return res[:len(res) - len(names)]

    (proj_pad,) = gather_behind(['mla_w_uq', 'mla_w_ukv', 'w_out'], lambda c: mm_sh("proj_in", h1, gathered['w_in'], comm=c))
    proj = unpad_cols(proj_pad, nb['w_in'], nbp['w_in'])
    p_rwkv = proj[:, :shift_dim]
    c_q = proj[:, shift_dim:shift_dim + q_lora]
    c_kv = proj[:, shift_dim + q_lora:shift_dim + q_lora + kv_lora]
    k_pe = proj[:, shift_dim + q_lora + kv_lora:d_in]
    shifted = token_shift_fwd(p_rwkv, w2d['rwkv_mu'], tm_wide)
    o1, o2, o3 = c_rwkv, 2 * c_rwkv, 3 * c_rwkv
    hr = _to_heads(shifted[:, :o1], RWKV_HEAD)
    hk = _to_heads(shifted[:, o1:o2], RWKV_HEAD)
    hv = _to_heads(shifted[:, o2:o3], RWKV_HEAD)
    hw = shifted[:, o3:o3 + decay_lora]
    ha = shifted[:, o3 + decay_lora:o3 + decay_lora + aaa_lora]
    hg = shifted[:, o3 + decay_lora + aaa_lora:]

    def per_head(vec):
        return vec.reshape(n_rh, 1, RWKV_HEAD)

    def lora_heads(w):
        return jnp.transpose(w.reshape(w.shape[0], n_rh, RWKV_HEAD), (1, 0, 2))

    pre_args = [hk, hw, ha, hg, per_head(w2d['rwkv_w0']), lora_heads(full['rwkv_w2']), per_head(w2d['rwkv_a0']),
                lora_heads(full['rwkv_a2']), lora_heads(full['rwkv_g2']), per_head(w2d['rwkv_k_k']), per_head(w2d['rwkv_k_a'])]
    pre_kinds = ['hrow', 'row', 'row', 'row', 'hconst', 'hconst', 'hconst', 'hconst', 'hconst', 'hconst', 'hconst']
    decay, kx, a_sc, b_sc, gate_r = gather_behind(['ffn_w_gate'], lambda c: rowwise(
        "rwkv_pre", _rwkv_pre_fn, pre_args, pre_kinds, [('hrow', RWKV_HEAD, F32)] * 5, heads=n_rh, s=s, tm=tm, comm=c))
    y_scan, ckpt = gather_behind(['ffn_w_up'], lambda c: scan_fwd(hr, decay, kx, hv, a_sc, b_sc, comm=c))
    post_args = [y_scan, hr, kx, hv, gate_r, per_head(w2d['rwkv_gn_w']), per_head(w2d['rwkv_gn_b']), per_head(w2d['rwkv_r_k'])]
    post_kinds = ['hrow'] * 5 + ['hconst'] * 3
    (y_rwkv_h,) = rowwise("rwkv_post", _rwkv_post_fn, post_args, post_kinds, [('hrow', RWKV_HEAD, F32)], heads=n_rh, s=s, tm=tm)

    pos = positions.reshape(s, 1).astype(F32)
    rot, inv2 = _rot_matrix(), _inv_freq2()
    mla_args = [c_q, c_kv, k_pe, pos, w2d['mla_q_norm_g'], w2d['mla_kv_norm_g'], inv2, rot]
    mla_kinds = ['row', 'row', 'row', 'row', 'const', 'const', 'const', 'const']
    qn, kvn, kp_rot, cos2, sin2 = rowwise(
        "mla_pre", _mla_pre_fn, mla_args, mla_kinds,
        [('row', q_lora, BF16), ('row', kv_lora, BF16), ('row', QK_ROPE, F32), ('row', QK_ROPE, F32), ('row', QK_ROPE, F32)],
        heads=1, s=s, tm=tm)
    q = unpad_cols(mm_sh("proj_q", qn, gathered['mla_w_uq']), nb['mla_w_uq'], nbp['mla_w_uq'])
    kv = unpad_cols(mm_sh("proj_kv", kvn, gathered['mla_w_ukv']), nb['mla_w_ukv'], nbp['mla_w_ukv'])
    q_h = _to_heads(q, QK_NOPE + QK_ROPE)
    kv_h = _to_heads(kv, QK_NOPE + V_HEAD)
    q_nope, q_pe = q_h[..., :QK_NOPE], q_h[..., QK_NOPE:]
    k_nope, v_att = kv_h[..., :QK_NOPE], kv_h[..., QK_NOPE:]
    ropeq_args = [q_pe, cos2, sin2, rot]
    ropeq_kinds = ['hrow', 'row', 'row', 'const']
    (q_pe_rot,) = rowwise("rope_q", _rope_q_fn, ropeq_args, ropeq_kinds, [('hrow', QK_ROPE, F32)], heads=n_mh, s=s, tm=tm)
    o_att, lse = gather_behind(['ffn_w_down'], lambda c: attn_fwd(q_nope, q_pe_rot, k_nope, kp_rot, v_att, comm=c))
    ycat = jnp.concatenate([_from_heads(y_rwkv_h), _from_heads(o_att)], axis=-1).astype(BF16)
    x1 = mm("proj_out", ycat, gathered['w_out'], add=xs)
    (h2,) = rowwise("rms_ffn", _rms_fn, [x1, w2d['ffn_norm_g']], ['row', 'const'], [('row', d, BF16)], heads=1, s=s, tm=tm)
    gate_pre = mm_sh("ffn_gate", h2, gathered['ffn_w_gate'])
    up = mm_sh("ffn_up", h2, gathered['ffn_w_up'])
    act = ffn_act_fwd(gate_pre, up, conv_w_pad, conv_b_pad)
    x2 = mm("ffn_down", act, gathered['ffn_w_down'], add=x1)

    ones = jnp.ones((s, 1), F32)
    fin_g = w2d['final_norm_g']
    d_x2, dg_final_p, loss_rows = rowwise_vjp("loss_bwd", _loss_fn, [x2, fin_g, tgt], ['row', 'const', 'row'], [ones], ['row'],
                                              [0, 1], heads=1, s=s, tm=tm, primal=True)
    d_x2_b = d_x2.astype(BF16)
    d_act = mm_nt("d_act", d_x2_b, gathered['ffn_w_down'], out_dtype=BF16)
    gsh = {}
    gsh['ffn_w_down'] = mm("dw_down", act.T, d_x2_b).reshape(N_DEV, nbp['ffn_w_gate'], d)
    d_gate, d_up, dcw_p, dcb_p = ffn_act_bwd1(gate_pre, up, conv_w_pad, conv_b_pad, d_act)
    d_gp = ffn_act_bwd2(d_gate, conv_w_pad)
    d_h2 = mm_sh_nt("d_h2_up", d_up, gathered['ffn_w_up'], add=mm_sh_nt("d_h2_gate", d_gp, gathered['ffn_w_gate']))
    h2_t = h2.T
    gsh['ffn_w_gate'] = mm_sh_out("dw_gate", h2_t, d_gp)
    gsh['ffn_w_up'] = mm_sh_out("dw_up", h2_t, d_up)
    d_x1n, dg_ffn_p = rowwise_vjp("rms_ffn_bwd", _rms_fn, [x1, w2d['ffn_norm_g']], ['row', 'const'], [d_h2], ['row'], [0, 1],
                                  heads=1, s=s, tm=tm)
    d_x1 = add_slots("d_x1_add", d_x1n[None], d_x2[None])[0]
    d_x1_b = d_x1.astype(BF16)
    d_ycat = mm_nt("d_ycat", d_x1_b, gathered['w_out'])
    gsh['w_out'] = mm("dw_out", ycat.T, d_x1_b).reshape(N_DEV, d // N_DEV, d)
    d_yr_h = _to_heads(d_ycat[:, :c_rwkv], RWKV_HEAD)
    d_o_h = _to_heads(d_ycat[:, c_rwkv:], V_HEAD)

    chip_sums, from_chips = {}, {}

    def scatter_behind(names, run):
        for n in names:
            chip_sums[n] = rs_chip_sum(n, gsh[n])
        *res, = run(RsChips([chip_sums[n][1] for n in names]))
        from_chips.update(zip(names, res[len(res) - len(names):], strict=True))
        return res[:len(res) - len(names)]

    d_qn_h, d_qpr_h, d_kn_h, d_v_h, d_kp_h = scatter_behind(['ffn_w_down'], lambda c: attn_bwd(
        q_nope, q_pe_rot, k_nope, kp_rot, v_att, o_att, lse, d_o_h, comm=c))
    (d_qp_h,) = rowwise_vjp("rope_q_bwd", _rope_q_fn, ropeq_args, ropeq_kinds, [d_qpr_h], ['hrow'], [0], heads=n_mh, s=s, tm=tm)
    d_q = pad_cols(_from_heads(jnp.concatenate([d_qn_h, d_qp_h], axis=-1)).astype(BF16), nb['mla_w_uq'], nbp['mla_w_uq'])
    d_kv = pad_cols(_from_heads(jnp.concatenate([d_kn_h, d_v_h], axis=-1)).astype(BF16), nb['mla_w_ukv'], nbp['mla_w_ukv'])
    d_kp_rot = headsum("d_kpe_heads", d_kp_h)
    d_qn = mm_sh_nt("d_qn", d_q, gathered['mla_w_uq'])
    d_kvn = mm_sh_nt("d_kvn", d_kv, gathered['mla_w_ukv'])
    gsh['mla_w_uq'] = mm_sh_out("dw_uq", qn.T, d_q)
    gsh['mla_w_ukv'] = mm_sh_out("dw_ukv", kvn.T, d_kv)
    d_cq, d_ckv, d_kpe, dg_q_p, dg_kv_p = rowwise_vjp(
        "mla_pre_bwd", _mla_pre_grad_fn, mla_args, mla_kinds, [d_qn, d_kvn, d_kp_rot], ['row', 'row', 'row'], [0, 1, 2, 4, 5],
        heads=1, s=s, tm=tm)

    d_y, d_r_post, d_k_post, d_v_post, d_gate_r, dgnw_p, dgnb_p, drk_p = scatter_behind(['ffn_w_gate'], lambda c: rowwise_vjp(
        "rwkv_post_bwd", _rwkv_post_fn, post_args, post_kinds, [d_yr_h], ['hrow'], list(range(8)), heads=n_rh, s=s, tm=tm, comm=c))
    d_r_sc, d_w_sc, d_k_sc, d_v_sc, d_a_sc, d_b_sc = scatter_behind(['ffn_w_up', 'w_out'], lambda c: scan_bwd(
        hr, decay, kx, hv, a_sc, b_sc, ckpt, d_y, comm=c))
    d_hk, d_hw_p, d_ha_p, d_hg_p, dw0_p, dw2_p, da0_p, da2_p, dg2_p, dkk_p, dka_p, d_hr, d_hv = scatter_behind(
        ['mla_w_uq', 'mla_w_ukv'], lambda c: rowwise_vjp(
            "rwkv_pre_bwd", _rwkv_pre_grad_fn, pre_args + [hr, hv], pre_kinds + ['hrow', 'hrow'],
            [d_w_sc, d_k_sc, d_k_post, d_a_sc, d_b_sc, d_gate_r, d_r_sc, d_r_post, d_v_sc, d_v_post], ['hrow'] * 10,
            list(range(13)), heads=n_rh, s=s, tm=tm, comm=c))
    d_shifted = jnp.concatenate([_from_heads(d_hr), _from_heads(d_hk), _from_heads(d_hv), headsum("d_hw_heads", d_hw_p),
                                 headsum("d_ha_heads", d_ha_p), headsum("d_hg_heads", d_hg_p)], axis=-1)
    d_p_rwkv, dmu_p = token_shift_bwd(p_rwkv, w2d['rwkv_mu'], d_shifted, tm_wide)
    d_proj = pad_cols(jnp.concatenate([d_p_rwkv, d_cq, d_ckv, d_kpe], axis=-1).astype(BF16), nb['w_in'], nbp['w_in'])
    gsh['w_in'] = mm_sh_out("dw_in", h1.T, d_proj)
    (d_h1,) = scatter_behind(['w_in'], lambda c: mm_sh_nt("d_h1", d_proj, gathered['w_in'], comm=c))
    d_xn, dg_attn_p = rowwise_vjp("rms_attn_bwd", _rms_fn, [xs, w2d['attn_norm_g']], ['row', 'const'], [d_h1], ['row'], [0, 1],
                                  heads=1, s=s, tm=tm)
    grad_x = add_slots("grad_x_add", d_xn[None], d_x1[None])[0]

    def from_heads_lora(g):
        return jnp.transpose(g, (1, 0, 2)).reshape(g.shape[1], n_rh * RWKV_HEAD)

    gw = {}
    gw['rwkv_w2'] = from_heads_lora(sum_partials("sum_dw2", dw2_p, True))
    gw['rwkv_a2'] = from_heads_lora(sum_partials("sum_da2", da2_p, True))
    gw['rwkv_g2'] = from_heads_lora(sum_partials("sum_dg2", dg2_p, True))
    dcw_pad = colsum("sum_dconv_w", dcw_p.reshape(dcw_p.shape[0], CONV_W * f_pad)).reshape(CONV_W, f_pad)
    gw['ffn_conv_w'] = unpad_cols(dcw_pad, nb['ffn_w_gate'], nbp['ffn_w_gate'])

    rep = {
        'attn_norm_g': sum_partials("sum_dg_attn", dg_attn_p, False),
        'rwkv_mu': colsum("sum_dmu", dmu_p.reshape(dmu_p.shape[0], shift_dim)),
        'rwkv_w0': sum_partials("sum_dw0", dw0_p, True).reshape(1, c_rwkv),
        'rwkv_a0': sum_partials("sum_da0", da0_p, True).reshape(1, c_rwkv),
        'rwkv_k_k': sum_partials("sum_dkk", dkk_p, True).reshape(1, c_rwkv),
        'rwkv_k_a': sum_partials("sum_dka", dka_p, True).reshape(1, c_rwkv),
        'rwkv_r_k': sum_partials("sum_drk", drk_p, True).reshape(1, c_rwkv),
        'rwkv_gn_w': sum_partials("sum_dgnw", dgnw_p, True).reshape(1, c_rwkv),
        'rwkv_gn_b': sum_partials("sum_dgnb", dgnb_p, True).reshape(1, c_rwkv),
        'mla_q_norm_g': sum_partials("sum_dg_q", dg_q_p, False),
        'mla_kv_norm_g': sum_partials("sum_dg_kv", dg_kv_p, False),
        'ffn_norm_g': sum_partials("sum_dg_ffn", dg_ffn_p, False),
        'ffn_conv_b': unpad_cols(colsum("sum_dconv_b", dcb_p.reshape(dcb_p.shape[0], f_pad)), nb['ffn_w_gate'], nbp['ffn_w_gate']),
        'final_norm_g': sum_partials("sum_dg_final", dg_final_p, False),
        'loss': sum_all("sum_loss", loss_rows.reshape(s // SUBLANES, SUBLANES)),
    }
    rep_pack = Pack([(n, w2d[n].shape) for n in REPLICATED] + [('loss', (1, 1))], 8)
    rep_all = all_gather("gather_rep_grads", rep_pack.pack(rep, F32))
    rep_sum = colsum("sum_rep_grads", rep_all.reshape(N_DEV, rep_pack.total)).reshape(rep_pack.rows, PACK_W)
    rep_g = rep_pack.unpack(rep_sum)
    loss = rep_g.pop('loss').reshape(())

    grads, deltas, new_m, new_v = dict(rep_g), {}, {}, {}
    for n in BIG:
        a, b = w2d[n].shape
        grads[n] = rs_finish(n, chip_sums[n][0], from_chips[n])[:a, :b]
        deltas[n], new_m[n], new_v[n] = rowwise(
            "adamw_" + n, _adamw_fn, [w2d[n], grads[n], m2d[n], v2d[n]], ['row'] * 4, [('row', b, F32)] * 3,
            heads=1, s=a, tm=_pick(a, 256, SUBLANES))
    sm_pack = Pack([(n, w2d[n].shape) for n in SMALL_SHARDED], 8)
    g_shards = {n: _full_to_shards(gw[n], SMALL_SHARDED[n]) for n in SMALL_SHARDED}
    grads.update(sm_pack.unpack(reduce_scatter("small", sm_pack.pack(g_shards, F32, lead=(N_DEV,)))))
    rest_pack = Pack([(n, w2d[n].shape) for n in WEIGHTS if n not in BIG], 8)
    d_r, m_r, v_r = rowwise(
        "adamw_small", _adamw_fn, [rest_pack.pack(w2d, F32), rest_pack.pack(grads, F32), rest_pack.pack(m2d, F32),
                                   rest_pack.pack(v2d, F32)],
        ['row'] * 4, [('row', PACK_W, F32)] * 3, heads=1, s=rest_pack.rows, tm=_pick(rest_pack.rows, 512, SUBLANES))
    deltas.update(rest_pack.unpack(d_r))
    new_m.update(rest_pack.unpack(m_r))
    new_v.update(rest_pack.unpack(v_r))

    def shaped(dct):
        return [dct[n].reshape(out_shapes[n]) for n in WEIGHTS]

    return (loss, grad_x.reshape(x.shape), *shaped(grads), *shaped(deltas), *shaped(new_m), *shaped(new_v))
```

```python
import functools
import math

import jax
import jax.numpy as jnp
import numpy as np
from jax import lax
from jax.experimental import pallas as pl
from jax.experimental.pallas import tpu as pltpu

F32 = jnp.float32
BF16 = jnp.bfloat16
HIGHEST = lax.Precision.HIGHEST
MESH = pl.DeviceIdType.MESH

N_DEV = 8
LANES = 128
SUBLANES = 8
VMEM_LIMIT = 48 * 1024 * 1024
RESIDENT_BYTES = 8 * 1024 * 1024

NORM_EPS = 1e-6
GN_EPS = 64e-5
RWKV_HEAD = 64
QK_NOPE = 128
QK_ROPE = 64
V_HEAD = 128
ROPE_THETA = 10000.0
CONV_W = 3
NEG_INF = -1e30
SCAN_CHUNK = 64
SCAN_HEADS = 8
SCAN_PASSES_SOLVE = 1
SCAN_PASSES_OUT = 1

ADAM_LR = 0.001
ADAM_B1 = 0.9
ADAM_B2 = 0.999
ADAM_EPS = 1e-08
ADAM_WD = 0.01
ADAM_STEP = 10

WEIGHTS = ['attn_norm_g', 'w_in', 'rwkv_mu', 'rwkv_w0', 'rwkv_w2', 'rwkv_a0', 'rwkv_a2', 'rwkv_g2', 'rwkv_k_k',
           'rwkv_k_a', 'rwkv_r_k', 'rwkv_gn_w', 'rwkv_gn_b', 'mla_q_norm_g', 'mla_w_uq', 'mla_kv_norm_g', 'mla_w_ukv',
           'w_out', 'ffn_norm_g', 'ffn_w_gate', 'ffn_w_up', 'ffn_conv_w', 'ffn_conv_b', 'ffn_w_down', 'final_norm_g']
BIG = {'w_in': 'col', 'mla_w_uq': 'col', 'mla_w_ukv': 'col', 'w_out': 'row', 'ffn_w_gate': 'col', 'ffn_w_up': 'col',
       'ffn_w_down': 'row'}
SMALL_SHARDED = {'rwkv_w2': 'col', 'rwkv_a2': 'col', 'rwkv_g2': 'col', 'ffn_conv_w': 'col'}
SHARDED = {**BIG, **SMALL_SHARDED}
REPLICATED = [n for n in WEIGHTS if n not in SHARDED]


def _round_up(n, m):
    return (n + m - 1) // m * m


def _pick(n, cap, unit):
    if n <= cap:
        return n
    best = None
    for t in range(unit, cap + 1, unit):
        if n % t == 0:
            best = t
    assert best is not None, (n, cap, unit)
    return best


def _params(sem):
    return pltpu.CompilerParams(dimension_semantics=sem, vmem_limit_bytes=VMEM_LIMIT)


def mm(name, a, b, add=None, out_dtype=F32):
    m, k = a.shape
    k2, n = b.shape
    assert k == k2, (name, a.shape, b.shape)
    tm = _pick(m, 512, SUBLANES * 2)
    tn = n if k * n * 2 <= RESIDENT_BYTES else _pick(n, 640, LANES)
    has_add = add is not None

    def body(a_ref, b_ref, *rest):
        o_ref = rest[-1]
        acc = jnp.dot(a_ref[...].astype(BF16), b_ref[...].astype(BF16), preferred_element_type=F32)
        if has_add:
            acc = acc + rest[0][...].astype(F32)
        o_ref[...] = acc.astype(o_ref.dtype)

    in_specs = [pl.BlockSpec((tm, k), lambda i, j: (i, 0)), pl.BlockSpec((k, tn), lambda i, j: (0, j))]
    ops = [a, b]
    if has_add:
        in_specs.append(pl.BlockSpec((tm, tn), lambda i, j: (i, j)))
        ops.append(add)
    return pl.pallas_call(
        body, name=name, grid=(m // tm, n // tn), in_specs=in_specs,
        out_specs=pl.BlockSpec((tm, tn), lambda i, j: (i, j)),
        out_shape=jax.ShapeDtypeStruct((m, n), out_dtype),
        compiler_params=_params(("parallel", "parallel")),
    )(*ops)


def mm_nt(name, a, b, out_dtype=F32):
    m, k = a.shape
    n, k2 = b.shape
    assert k == k2, (name, a.shape, b.shape)
    tm = _pick(m, 2048 if m * k * 2 <= RESIDENT_BYTES else 512, SUBLANES * 2)
    tn = _pick(n, 1024, LANES)

    def body(a_ref, b_ref, o_ref):
        acc = lax.dot_general(a_ref[...].astype(BF16), b_ref[...].astype(BF16), (((1,), (1,)), ((), ())),
                              preferred_element_type=F32)
        o_ref[...] = acc.astype(o_ref.dtype)

    return pl.pallas_call(
        body, name=name, grid=(m // tm, n // tn),
        in_specs=[pl.BlockSpec((tm, k), lambda i, j: (i, 0)), pl.BlockSpec((tn, k), lambda i, j: (j, 0))],
        out_specs=pl.BlockSpec((tm, tn), lambda i, j: (i, j)),
        out_shape=jax.ShapeDtypeStruct((m, n), out_dtype),
        compiler_params=_params(("parallel", "parallel")),
    )(a, b)


def mm_sh(name, a, g, out_dtype=F32, comm=None):
    m, k = a.shape
    nd, k2, nbp = g.shape
    assert k == k2, (name, a.shape, g.shape)
    tm = _pick(m, 2048 if m * k * 2 <= RESIDENT_BYTES else 512, SUBLANES * 2)

    def body(a_ref, b_ref, o_ref):
        o_ref[...] = jnp.dot(a_ref[...].astype(BF16), b_ref[...].astype(BF16), preferred_element_type=F32).astype(o_ref.dtype)

    res = _pallas(
        body, name=name, grid=(m // tm, nd),
        in_specs=[pl.BlockSpec((tm, k), lambda i, j: (i, 0)), pl.BlockSpec((None, k, nbp), lambda i, j: (j, 0, 0))],
        out_specs=[pl.BlockSpec((tm, nbp), lambda i, j: (i, j))],
        out_shape=[jax.ShapeDtypeStruct((m, nd * nbp), out_dtype)], sem=("parallel", "parallel"), comm=comm,
    )(a, g)
    return res[0] if comm is None else res


def mm_sh_nt(name, a, g, add=None, comm=None):
    m, n = a.shape
    nd, k, nbp = g.shape
    assert n == nd * nbp, (name, a.shape, g.shape)
    tm = _pick(m, 512, SUBLANES * 2)
    has_add = add is not None

    def body(a_ref, b_ref, *rest):
        o_ref = rest[-1]
        part = lax.dot_general(a_ref[...].astype(BF16), b_ref[...].astype(BF16), (((1,), (1,)), ((), ())),
                               preferred_element_type=F32)

        @pl.when(pl.program_id(1) == 0)
        def _():
            o_ref[...] = part + rest[0][...] if has_add else part

        @pl.when(pl.program_id(1) != 0)
        def _():
            o_ref[...] += part

    in_specs = [pl.BlockSpec((tm, nbp), lambda i, j: (i, j)), pl.BlockSpec((None, k, nbp), lambda i, j: (j, 0, 0))]
    ops = [a, g]
    if has_add:
        in_specs.append(pl.BlockSpec((tm, k), lambda i, j: (i, 0)))
        ops.append(add)
    res = _pallas(
        body, name=name, grid=(m // tm, nd), in_specs=in_specs,
        out_specs=[pl.BlockSpec((tm, k), lambda i, j: (i, 0))],
        out_shape=[jax.ShapeDtypeStruct((m, k), F32)], sem=("parallel", "arbitrary"), comm=comm,
    )(*ops)
    return res[0] if comm is None else res


def mm_tn(name, a, b):
    m, k = a.shape
    m2, n = b.shape
    assert m == m2, (name, a.shape, b.shape)
    tk = _pick(k, 512, LANES)
    tn = n if m * n * 2 <= RESIDENT_BYTES else _pick(n, 640, LANES)

    def body(a_ref, b_ref, o_ref):
        o_ref[...] = lax.dot_general(a_ref[...].astype(BF16), b_ref[...].astype(BF16), (((0,), (0,)), ((), ())),
                                     preferred_element_type=F32)

    return pl.pallas_call(
        body, name=name, grid=(k // tk, n // tn),
        in_specs=[pl.BlockSpec((m, tk), lambda i, j: (0, i)), pl.BlockSpec((m, tn), lambda i, j: (0, j))],
        out_specs=pl.BlockSpec((tk, tn), lambda i, j: (i, j)),
        out_shape=jax.ShapeDtypeStruct((k, n), F32),
        compiler_params=_params(("parallel", "parallel")),
    )(a, b)


def mm_sh_out(name, a, b, comm=None):
    m, k = a.shape
    m2, n = b.shape
    assert m == m2 and n % N_DEV == 0, (name, a.shape, b.shape)
    nbp = n // N_DEV
    tk = _pick(k, 2048 if k * m * 2 <= RESIDENT_BYTES else 512, LANES)

    def body(a_ref, b_ref, o_ref):
        o_ref[...] = lax.dot_general(a_ref[...].astype(BF16), b_ref[...].astype(BF16), (((0,), (0,)), ((), ())),
                                     preferred_element_type=F32)

    res = _pallas(
        body, name=name, grid=(k // tk, N_DEV),
        in_specs=[pl.BlockSpec((m, tk), lambda i, j: (0, i)), pl.BlockSpec((m, nbp), lambda i, j: (0, j))],
        out_specs=[pl.BlockSpec((None, tk, nbp), lambda i, j: (j, i, 0))],
        out_shape=[jax.ShapeDtypeStruct((N_DEV, k, nbp), F32)], sem=("parallel", "parallel"), comm=comm,
    )(a, b)
    return res[0] if comm is None else res


def pad_cols(y, nb, nbp):
    m = y.shape[0]
    if nb == nbp:
        return y
    return jnp.pad(y.reshape(m, N_DEV, nb), ((0, 0), (0, 0), (0, nbp - nb))).reshape(m, N_DEV * nbp)


def unpad_cols(y, nb, nbp):
    m = y.shape[0]
    if nb == nbp:
        return y
    return y.reshape(m, N_DEV, nbp)[:, :, :nb].reshape(m, N_DEV * nb)


def _in_spec(kind, a, tm):
    if kind == 'row':
        return pl.BlockSpec((tm, a.shape[1]), lambda h, i: (i, 0))
    if kind == 'hrow':
        return pl.BlockSpec((None, tm, a.shape[2]), lambda h, i: (h, i, 0))
    if kind == 'const':
        return pl.BlockSpec(a.shape, lambda h, i: (0, 0))
    assert kind == 'hconst', kind
    return pl.BlockSpec((None,) + a.shape[1:], lambda h, i: (h, 0, 0))


def _row_out(kind, c, dtype, heads, s, tm):
    if kind == 'row':
        assert heads == 1
        return jax.ShapeDtypeStruct((s, c), dtype), pl.BlockSpec((tm, c), lambda h, i: (i, 0))
    return jax.ShapeDtypeStruct((heads, s, c), dtype), pl.BlockSpec((None, tm, c), lambda h, i: (h, i, 0))


def rowwise(name, fn, arrs, kinds, outs, *, heads, s, tm, comm=None):
    n_in = len(arrs)

    def body(*refs):
        vals = fn(*[r[...] for r in refs[:n_in]])
        for o, v in zip(refs[n_in:], vals, strict=True):
            o[...] = v.astype(o.dtype)

    shapes, specs = zip(*[_row_out(k, c, dt, heads, s, tm) for k, c, dt in outs])
    return _pallas(
        body, name=name, grid=(heads, s // tm),
        in_specs=[_in_spec(k, a, tm) for k, a in zip(kinds, arrs, strict=True)],
        out_specs=list(specs), out_shape=list(shapes), sem=("parallel", "parallel"), comm=comm,
    )(*arrs)


def rowwise_vjp(name, fn, arrs, kinds, cots, cot_kinds, wrt, *, heads, s, tm, out_dtypes=None, primal=False, comm=None):
    n_in, n_cot = len(arrs), len(cots)
    nb = s // tm
    out_dtypes = out_dtypes or [F32] * len(wrt)

    def body(*refs):
        vals = [r[...] for r in refs[:n_in]]
        cvals = tuple(r[...].astype(F32) for r in refs[n_in:n_in + n_cot])
        outs = refs[n_in + n_cot:]

        def f(*dv):
            full = list(vals)
            for j, i in enumerate(wrt):
                full[i] = dv[j]
            return tuple(fn(*full))

        prim, vjp_fn = jax.vjp(f, *[vals[i].astype(F32) for i in wrt])
        grads = vjp_fn(cvals)
        for o, g in zip(outs[:len(wrt)], grads, strict=True):
            o[...] = g.astype(o.dtype)
        if primal:
            for o, p in zip(outs[len(wrt):], prim, strict=True):
                o[...] = p.astype(o.dtype)

    shapes, specs = [], []
    for i, dt in zip(wrt, out_dtypes, strict=True):
        kind, a = kinds[i], arrs[i]
        if kind in ('row', 'hrow'):
            c = a.shape[-1]
            sh, sp = _row_out('row' if (kind == 'row' and heads == 1) else 'hrow', c, dt, heads, s, tm)
        else:
            r, c = a.shape[-2:]
            sh = jax.ShapeDtypeStruct((heads, nb, r, c), dt)
            sp = pl.BlockSpec((None, None, r, c), lambda h, i: (h, i, 0, 0))
        shapes.append(sh)
        specs.append(sp)
    if primal:
        for ck, c in zip(cot_kinds, cots, strict=True):
            sh, sp = _row_out(ck, c.shape[-1], F32, heads, s, tm)
            shapes.append(sh)
            specs.append(sp)
    in_specs = [_in_spec(k, a, tm) for k, a in zip(kinds, arrs, strict=True)]
    in_specs += [_in_spec(k, a, tm) for k, a in zip(cot_kinds, cots, strict=True)]
    return _pallas(
        body, name=name, grid=(heads, nb), in_specs=in_specs, out_specs=specs, out_shape=shapes,
        sem=("parallel", "parallel"), comm=comm,
    )(*arrs, *cots)


def colsum(name, x):
    n, m = x.shape
    tc = _pick(m, 32768, LANES) if m % LANES == 0 else m

    def body(x_ref, o_ref):
        acc = x_ref[0:1, :].astype(F32)
        for r in range(1, n):
            acc = acc + x_ref[r:r + 1, :].astype(F32)
        o_ref[...] = acc

    return pl.pallas_call(
        body, name=name, grid=(m // tc,), in_specs=[pl.BlockSpec((n, tc), lambda j: (0, j))],
        out_specs=pl.BlockSpec((1, tc), lambda j: (0, j)), out_shape=jax.ShapeDtypeStruct((1, m), F32),
        compiler_params=_params(("parallel",)),
    )(x)


def headsum(name, x):
    h, s, c = x.shape
    tm = _pick(s, 256, SUBLANES)

    def body(x_ref, o_ref):
        acc = x_ref[0]
        for j in range(1, h):
            acc = acc + x_ref[j]
        o_ref[...] = acc

    return pl.pallas_call(
        body, name=name, grid=(s // tm,), in_specs=[pl.BlockSpec((h, tm, c), lambda i: (0, i, 0))],
        out_specs=pl.BlockSpec((tm, c), lambda i: (i, 0)), out_shape=jax.ShapeDtypeStruct((s, c), F32),
        compiler_params=_params(("parallel",)),
    )(x)


def sum_all(name, x):
    def body(x_ref, o_ref):
        o_ref[...] = jnp.sum(x_ref[...], keepdims=True)

    return pl.pallas_call(body, name=name, out_shape=jax.ShapeDtypeStruct((1, 1), F32))(x)


def sum_partials(name, p, per_head):
    h, nb, r, c = p.shape
    if per_head:
        flat = jnp.transpose(p, (1, 0, 2, 3)).reshape(nb, h * r * c)
        if nb == 1:
            return flat.reshape(h, r, c)
        return colsum(name, flat).reshape(h, r, c)
    flat = p.reshape(h * nb, r * c)
    if h * nb == 1:
        return flat.reshape(r, c)
    return colsum(name, flat).reshape(r, c)


def _rms_fn(x, g):
    xf = x.astype(F32)
    return (xf * lax.rsqrt(jnp.mean(xf * xf, axis=-1, keepdims=True) + NORM_EPS) * g,)


def _softplus(z):
    return jnp.maximum(z, 0.0) + jnp.log(1.0 + jnp.exp(-jnp.abs(z)))


def _rwkv_pre_fn(hk, hw, ha, hg, w0, w2, a0, a2, g2, k_k, k_a):
    zw = w0 + jnp.dot(jnp.tanh(hw), w2, preferred_element_type=F32)
    w_log = -_softplus(-zw) - 0.5
    decay = jnp.exp(-jnp.exp(w_log))
    a = jax.nn.sigmoid(a0 + jnp.dot(ha, a2, preferred_element_type=F32))
    g = jnp.dot(jax.nn.sigmoid(hg), g2, preferred_element_type=F32)
    kk = hk * k_k
    kk = kk * lax.rsqrt(jnp.maximum(jnp.sum(kk * kk, axis=-1, keepdims=True), 1e-24))
    k = hk * (1.0 + (a - 1.0) * k_a)
    return decay, k, -kk, kk * a, g


def _rwkv_pre_grad_fn(hk, hw, ha, hg, w0, w2, a0, a2, g2, k_k, k_a, hr, hv):
    decay, k, a_sc, b_sc, g = _rwkv_pre_fn(hk, hw, ha, hg, w0, w2, a0, a2, g2, k_k, k_a)
    return decay, k, k, a_sc, b_sc, g, hr, hr, hv, hv


def _rwkv_post_fn(y, r, k, v, g, gn_w, gn_b, r_k):
    mu = jnp.mean(y, axis=-1, keepdims=True)
    var = jnp.mean(jnp.square(y - mu), axis=-1, keepdims=True)
    yn = (y - mu) * lax.rsqrt(var + GN_EPS) * gn_w + gn_b
    bonus = jnp.sum(r * k * r_k, axis=-1, keepdims=True) * v
    return ((yn + bonus) * g,)


def _rope_tables(pos, inv_freq2):
    ang = pos * inv_freq2
    return jnp.cos(ang), jnp.sin(ang)


def _rope(t, cos2, sin2, rot):
    return t * cos2 + jnp.dot(t, rot, precision=HIGHEST, preferred_element_type=F32) * sin2


def _mla_pre_fn(c_q, c_kv, k_pe, pos, q_g, kv_g, inv_freq2, rot):
    cos2, sin2 = _rope_tables(pos, inv_freq2)
    return _rms_fn(c_q, q_g)[0], _rms_fn(c_kv, kv_g)[0], _rope(k_pe, cos2, sin2, rot), cos2, sin2


def _mla_pre_grad_fn(c_q, c_kv, k_pe, pos, q_g, kv_g, inv_freq2, rot):
    return _mla_pre_fn(c_q, c_kv, k_pe, pos, q_g, kv_g, inv_freq2, rot)[:3]


def _rope_q_fn(q_pe, cos2, sin2, rot):
    return (_rope(q_pe, cos2, sin2, rot),)


def _loss_fn(x2, g, target):
    y = _rms_fn(x2, g)[0]
    return (0.5 * jnp.mean(jnp.square(y - target), axis=-1, keepdims=True),)


def _adamw_fn(w, g, m, v):
    m = ADAM_B1 * m + (1.0 - ADAM_B1) * g
    v = ADAM_B2 * v + (1.0 - ADAM_B2) * jnp.square(g)
    m_hat = m / (1.0 - ADAM_B1 ** ADAM_STEP)
    v_hat = v / (1.0 - ADAM_B2 ** ADAM_STEP)
    delta = -ADAM_LR * (m_hat / (jnp.sqrt(v_hat) + ADAM_EPS) + ADAM_WD * w)
    return delta, m, v


def _prev_halo_spec(c, tm):
    return pl.BlockSpec((SUBLANES, c), lambda i: (jnp.maximum(i * (tm // SUBLANES) - 1, 0), 0))


def _next_halo_spec(c, tm, s):
    return pl.BlockSpec((SUBLANES, c), lambda i: (jnp.minimum((i + 1) * (tm // SUBLANES), s // SUBLANES - 1), 0))


def _shift_down(p, halo, first_block, n):
    out = pltpu.roll(p, n, 0)
    row = lax.broadcasted_iota(jnp.int32, p.shape, 0)
    for j in range(n):
        top = jnp.where(first_block, 0.0, halo[SUBLANES - n + j:SUBLANES - n + j + 1, :])
        out = jnp.where(row == j, top, out)
    return out


def _shift_up(p, halo, last_block, n):
    rows = p.shape[0]
    out = pltpu.roll(p, rows - n, 0)
    row = lax.broadcasted_iota(jnp.int32, p.shape, 0)
    for j in range(n):
        bot = jnp.where(last_block, 0.0, halo[j:j + 1, :])
        out = jnp.where(row == rows - n + j, bot, out)
    return out


def token_shift_fwd(p, mu, tm):
    s, c = p.shape

    def body(p_ref, halo_ref, mu_ref, o_ref):
        pv = p_ref[...]
        prev = _shift_down(pv, halo_ref[...], pl.program_id(0) == 0, 1)
        o_ref[...] = pv + (prev - pv) * mu_ref[...]

    return pl.pallas_call(
        body, name="token_shift_fwd", grid=(s // tm,),
        in_specs=[pl.BlockSpec((tm, c), lambda i: (i, 0)), _prev_halo_spec(c, tm), pl.BlockSpec((1, c), lambda i: (0, 0))],
        out_specs=pl.BlockSpec((tm, c), lambda i: (i, 0)), out_shape=jax.ShapeDtypeStruct((s, c), F32),
        compiler_params=_params(("parallel",)),
    )(p, p, mu)


def token_shift_bwd(p, mu, ds, tm):
    s, c = p.shape
    nb = s // tm

    def body(p_ref, halo_ref, mu_ref, ds_ref, dsn_ref, dp_ref, dmu_ref):
        i = pl.program_id(0)
        pv, dsv, muv = p_ref[...], ds_ref[...], mu_ref[...]
        prev = _shift_down(pv, halo_ref[...], i == 0, 1)
        nxt = _shift_up(dsv, dsn_ref[...], i == nb - 1, 1)
        dp_ref[...] = dsv * (1.0 - muv) + nxt * muv
        dmu_ref[...] = jnp.sum(dsv * (prev - pv), axis=0, keepdims=True)

    return pl.pallas_call(
        body, name="token_shift_bwd", grid=(nb,),
        in_specs=[pl.BlockSpec((tm, c), lambda i: (i, 0)), _prev_halo_spec(c, tm), pl.BlockSpec((1, c), lambda i: (0, 0)),
                  pl.BlockSpec((tm, c), lambda i: (i, 0)), _next_halo_spec(c, tm, s)],
        out_specs=[pl.BlockSpec((tm, c), lambda i: (i, 0)), pl.BlockSpec((None, 1, c), lambda i: (i, 0, 0))],
        out_shape=[jax.ShapeDtypeStruct((s, c), F32), jax.ShapeDtypeStruct((nb, 1, c), F32)],
        compiler_params=_params(("parallel",)),
    )(p, p, mu, ds, ds)


def _ffn_tiles(s, f):
    return _pick(s, 256, SUBLANES), _pick(f, 1408, LANES)


def _conv_gate(gp, halo, first_block, cw, cb):
    p1 = _shift_down(gp, halo, first_block, 1)
    p2 = _shift_down(gp, halo, first_block, 2)
    return cw[0:1, :] * p2 + cw[1:2, :] * p1 + cw[2:3, :] * gp + cb, p1, p2


def ffn_act_fwd(gate_pre, up, conv_w, conv_b):
    s, f = gate_pre.shape
    tm, tc = _ffn_tiles(s, f)

    def body(gp_ref, halo_ref, up_ref, cw_ref, cb_ref, o_ref):
        gate, _, _ = _conv_gate(gp_ref[...], halo_ref[...], pl.program_id(0) == 0, cw_ref[...], cb_ref[...])
        o_ref[...] = (gate * jax.nn.sigmoid(gate) * up_ref[...]).astype(o_ref.dtype)

    blk = pl.BlockSpec((tm, tc), lambda i, j: (i, j))
    return pl.pallas_call(
        body, name="ffn_act_fwd", grid=(s // tm, f // tc),
        in_specs=[blk, pl.BlockSpec((SUBLANES, tc), lambda i, j: (jnp.maximum(i * (tm // SUBLANES) - 1, 0), j)), blk,
                  pl.BlockSpec((CONV_W, tc), lambda i, j: (0, j)), pl.BlockSpec((1, tc), lambda i, j: (0, j))],
        out_specs=blk, out_shape=jax.ShapeDtypeStruct((s, f), BF16),
        compiler_params=_params(("parallel", "parallel")),
    )(gate_pre, gate_pre, up, conv_w, conv_b)


def ffn_act_bwd1(gate_pre, up, conv_w, conv_b, d_act):
    s, f = gate_pre.shape
    tm, tc = _ffn_tiles(s, f)
    nb = s // tm

    def body(gp_ref, halo_ref, up_ref, cw_ref, cb_ref, da_ref, dg_ref, du_ref, dcw_ref, dcb_ref):
        gp = gp_ref[...]
        gate, p1, p2 = _conv_gate(gp, halo_ref[...], pl.program_id(0) == 0, cw_ref[...], cb_ref[...])
        sig = jax.nn.sigmoid(gate)
        da = da_ref[...].astype(F32)
        du_ref[...] = (da * gate * sig).astype(du_ref.dtype)
        dg = da * up_ref[...] * (sig * (1.0 + gate * (1.0 - sig)))
        dg_ref[...] = dg
        dcb_ref[...] = jnp.sum(dg, axis=0, keepdims=True)
        dcw_ref[0:1, :] = jnp.sum(dg * p2, axis=0, keepdims=True)
        dcw_ref[1:2, :] = jnp.sum(dg * p1, axis=0, keepdims=True)
        dcw_ref[2:3, :] = jnp.sum(dg * gp, axis=0, keepdims=True)

    blk = pl.BlockSpec((tm, tc), lambda i, j: (i, j))
    return pl.pallas_call(
        body, name="ffn_act_bwd1", grid=(nb, f // tc),
        in_specs=[blk, pl.BlockSpec((SUBLANES, tc), lambda i, j: (jnp.maximum(i * (tm // SUBLANES) - 1, 0), j)), blk,
                  pl.BlockSpec((CONV_W, tc), lambda i, j: (0, j)), pl.BlockSpec((1, tc), lambda i, j: (0, j)), blk],
        out_specs=[blk, blk, pl.BlockSpec((None, CONV_W, tc), lambda i, j: (i, 0, j)),
                   pl.BlockSpec((None, 1, tc), lambda i, j: (i, 0, j))],
        out_shape=[jax.ShapeDtypeStruct((s, f), F32), jax.ShapeDtypeStruct((s, f), BF16),
                   jax.ShapeDtypeStruct((nb, CONV_W, f), F32), jax.ShapeDtypeStruct((nb, 1, f), F32)],
        compiler_params=_params(("parallel", "parallel")),
    )(gate_pre, gate_pre, up, conv_w, conv_b, d_act)


def ffn_act_bwd2(d_gate, conv_w):
    s, f = d_gate.shape
    tm, tc = _ffn_tiles(s, f)
    nb = s // tm

    def body(dg_ref, halo_ref, cw_ref, o_ref):
        dg, cw = dg_ref[...], cw_ref[...]
        last = pl.program_id(0) == nb - 1
        n1 = _shift_up(dg, halo_ref[...], last, 1)
        n2 = _shift_up(dg, halo_ref[...], last, 2)
        o_ref[...] = (cw[2:3, :] * dg + cw[1:2, :] * n1 + cw[0:1, :] * n2).astype(o_ref.dtype)

    blk = pl.BlockSpec((tm, tc), lambda i, j: (i, j))
    return pl.pallas_call(
        body, name="ffn_act_bwd2", grid=(nb, f // tc),
        in_specs=[blk, pl.BlockSpec((SUBLANES, tc), lambda i, j: (jnp.minimum((i + 1) * (tm // SUBLANES), s // SUBLANES - 1), j)),
                  pl.BlockSpec((CONV_W, tc), lambda i, j: (0, j))],
        out_specs=blk, out_shape=jax.ShapeDtypeStruct((s, f), BF16),
        compiler_params=_params(("parallel", "parallel")),
    )(d_gate, d_gate, conv_w)


def _mxu(x, y, cx, cy):
    if x.ndim == 3:
        return lax.dot_general(x, y, (((cx + 1,), (cy + 1,)), ((0,), (0,))), preferred_element_type=F32)
    return lax.dot_general(x, y, (((cx,), (cy,)), ((), ())), preferred_element_type=F32)


def _split(x):
    hi = x.astype(BF16)
    return hi, (x - hi.astype(F32)).astype(BF16)


def _make_dot3(cx, cy, passes):
    @jax.custom_vjp
    def f(x, y):
        if passes == 1:
            return _mxu(x.astype(BF16), y.astype(BF16), cx, cy)
        xh, xl = _split(x)
        yh, yl = _split(y)
        return _mxu(xh, yh, cx, cy) + (_mxu(xh, yl, cx, cy) + _mxu(xl, yh, cx, cy))

    def fwd(x, y):
        return f(x, y), (x, y)

    def bwd(res, g):
        x, y = res
        dx = dot3(g, y, 1, 1 - cy, passes) if cx == 1 else dot3(y, g, 1 - cy, 1, passes)
        dy = dot3(x, g, 1 - cx, 0, passes) if cy == 0 else dot3(g, x, 0, 1 - cx, passes)
        return dx, dy

    f.defvjp(fwd, bwd)
    return f


_DOT3 = {}


def dot3(x, y, cx, cy, passes=3):
    if (cx, cy, passes) not in _DOT3:
        _DOT3[(cx, cy, passes)] = _make_dot3(cx, cy, passes)
    return _DOT3[(cx, cy, passes)](x, y)


def _dot(x, y, passes=3):
    return dot3(x, y, 1, 0, passes)


def _dot_nt(x, y, passes=3):
    return dot3(x, y, 1, 1, passes)


def _dot_tn(x, y, passes=3):
    return dot3(x, y, 0, 0, passes)


def _tri_sum(x, lower):
    t = x.shape[-2]
    row = lax.broadcasted_iota(jnp.int32, (t, t), 0)
    col = lax.broadcasted_iota(jnp.int32, (t, t), 1)
    tri = jnp.where((col <= row) if lower else (col >= row), 1.0, 0.0).astype(BF16)
    if x.ndim == 3:
        tri = jnp.broadcast_to(tri[None], (x.shape[0], t, t))
    hi = x.astype(BF16)
    rest = x - hi.astype(F32)
    mid = rest.astype(BF16)
    low = (rest - mid.astype(F32)).astype(BF16)
    return _mxu(tri, hi, 1, 0) + (_mxu(tri, mid, 1, 0) + _mxu(tri, low, 1, 0))


@jax.custom_vjp
def _cumsum_rows(x):
    return _tri_sum(x, True)


_cumsum_rows.defvjp(lambda x: (_tri_sum(x, True), None), lambda _, g: (_tri_sum(g, False),))


def _scan_chunk(s0, r, w, k, v, a, b):
    t = r.shape[1]
    row = lax.broadcasted_iota(jnp.int32, (1, t, t), 1)
    col = lax.broadcasted_iota(jnp.int32, (1, t, t), 2)
    strict, incl = col < row, col <= row
    logw = jnp.log(w)
    cum = _cumsum_rows(logw)
    w_in, w_ex, w_inv = jnp.exp(cum), jnp.exp(cum - logw), jnp.exp(-cum)
    w_all = jnp.exp(jnp.sum(logw, axis=1, keepdims=True))
    at, rt, kt, bt = a * w_ex, r * w_in, k * w_inv, b * w_inv
    ps, po = SCAN_PASSES_SOLVE, SCAN_PASSES_OUT
    a_ab = jnp.where(strict, _dot_nt(at, bt, ps), 0.0)
    a_ak = jnp.where(strict, _dot_nt(at, kt, ps), 0.0)
    a_rk = jnp.where(incl, _dot_nt(rt, kt, po), 0.0)
    a_rb = jnp.where(incl, _dot_nt(rt, bt, po), 0.0)
    u = _dot_nt(at, s0, ps) + _dot(a_ak, v, ps)
    p = a_ab
    steps = int(math.log2(t))
    assert 2 ** steps == t
    for j in range(steps):
        u = u + _dot(p, u, ps)
        if j < steps - 1:
            p = _dot(p, p, ps)
    y = _dot_nt(rt, s0, po) + _dot(a_rk, v, po) + _dot(a_rb, u, po)
    s_new = s0 * w_all + _dot_tn(v, kt * w_all, po) + _dot_tn(u, bt * w_all, po)
    return y, s_new


def scan_fwd(r, w, k, v, a, b, comm=None):
    h, s, n = r.shape
    t = min(SCAN_CHUNK, s)
    nc = s // t

    hb = SCAN_HEADS if h % SCAN_HEADS == 0 else 1

    def body(r_ref, w_ref, k_ref, v_ref, a_ref, b_ref, y_ref, ck_ref, st_ref):
        @pl.when(pl.program_id(1) == 0)
        def _():
            st_ref[...] = jnp.zeros_like(st_ref)

        s0 = st_ref[...]
        ck_ref[...] = s0
        y, s_new = _scan_chunk(s0, r_ref[...], w_ref[...], k_ref[...], v_ref[...], a_ref[...], b_ref[...])
        y_ref[...] = y
        st_ref[...] = s_new

    blk = pl.BlockSpec((hb, t, n), lambda hh, c: (hh, c, 0))
    return _pallas(
        body, name="rwkv_scan_fwd", grid=(h // hb, nc), in_specs=[blk] * 6,
        out_specs=[blk, pl.BlockSpec((hb, None, n, n), lambda hh, c: (hh, c, 0, 0))],
        out_shape=[jax.ShapeDtypeStruct((h, s, n), F32), jax.ShapeDtypeStruct((h, nc, n, n), F32)],
        scratch_shapes=[pltpu.VMEM((hb, n, n), F32)], sem=("parallel", "arbitrary"), comm=comm,
    )(r, w, k, v, a, b)


def scan_bwd(r, w, k, v, a, b, ck, dy, comm=None):
    h, s, n = r.shape
    t = min(SCAN_CHUNK, s)
    nc = s // t

    hb = SCAN_HEADS if h % SCAN_HEADS == 0 else 1

    def body(r_ref, w_ref, k_ref, v_ref, a_ref, b_ref, ck_ref, dy_ref, dr_ref, dw_ref, dk_ref, dv_ref, da_ref, db_ref, ds_ref):
        @pl.when(pl.program_id(1) == 0)
        def _():
            ds_ref[...] = jnp.zeros_like(ds_ref)

        _, vjp_fn = jax.vjp(_scan_chunk, ck_ref[...], r_ref[...], w_ref[...], k_ref[...], v_ref[...], a_ref[...], b_ref[...])
        ds0, dr, dw, dk, dv, da, db = vjp_fn((dy_ref[...], ds_ref[...]))
        ds_ref[...] = ds0
        dr_ref[...], dw_ref[...], dk_ref[...], dv_ref[...], da_ref[...], db_ref[...] = dr, dw, dk, dv, da, db

    blk = pl.BlockSpec((hb, t, n), lambda hh, c: (hh, nc - 1 - c, 0))
    return _pallas(
        body, name="rwkv_scan_bwd", grid=(h // hb, nc),
        in_specs=[blk] * 6 + [pl.BlockSpec((hb, None, n, n), lambda hh, c: (hh, nc - 1 - c, 0, 0)), blk],
        out_specs=[blk] * 6, out_shape=[jax.ShapeDtypeStruct((h, s, n), F32)] * 6,
        scratch_shapes=[pltpu.VMEM((hb, n, n), F32)], sem=("parallel", "arbitrary"), comm=comm,
    )(r, w, k, v, a, b, ck, dy)


def _attn_scores(qn, qp, kn, kp, q0):
    scale = (QK_NOPE + QK_ROPE) ** -0.5
    sc = lax.dot_general(qn.astype(BF16), kn.astype(BF16), (((1,), (1,)), ((), ())), preferred_element_type=F32)
    sc = sc + lax.dot_general(qp.astype(BF16), kp.astype(BF16), (((1,), (1,)), ((), ())), preferred_element_type=F32)
    row = q0 + lax.broadcasted_iota(jnp.int32, sc.shape, 0)
    col = lax.broadcasted_iota(jnp.int32, sc.shape, 1)
    return jnp.where(row >= col, sc * scale, NEG_INF), scale


def attn_fwd(qn, qp, kn, kp, v, comm=None):
    h, s, _ = qn.shape
    tq = _pick(s, 256, SUBLANES)

    def body(qn_ref, qp_ref, kn_ref, kp_ref, v_ref, o_ref, lse_ref):
        sc, _ = _attn_scores(qn_ref[...], qp_ref[...], kn_ref[...], kp_ref[...], pl.program_id(1) * tq)
        mx = jnp.max(sc, axis=-1, keepdims=True)
        e = jnp.exp(sc - mx)
        den = jnp.sum(e, axis=-1, keepdims=True)
        p = e / den
        o_ref[...] = jnp.dot(p.astype(BF16), v_ref[...].astype(BF16), preferred_element_type=F32)
        lse_ref[...] = mx + jnp.log(den)

    qblk = lambda c: pl.BlockSpec((None, tq, c), lambda hh, i: (hh, i, 0))
    kblk = lambda c: pl.BlockSpec((None, s, c), lambda hh, i: (hh, 0, 0))
    return _pallas(
        body, name="mla_attn_fwd", grid=(h, s // tq),
        in_specs=[qblk(QK_NOPE), qblk(QK_ROPE), kblk(QK_NOPE), pl.BlockSpec((s, QK_ROPE), lambda hh, i: (0, 0)), kblk(V_HEAD)],
        out_specs=[qblk(V_HEAD), qblk(1)],
        out_shape=[jax.ShapeDtypeStruct((h, s, V_HEAD), F32), jax.ShapeDtypeStruct((h, s, 1), F32)],
        sem=("parallel", "parallel"), comm=comm,
    )(qn, qp, kn, kp, v)


def attn_bwd(qn, qp, kn, kp, v, o, lse, do, comm=None):
    h, s, _ = qn.shape
    tq = _pick(s, 256, SUBLANES)

    def body(qn_ref, qp_ref, kn_ref, kp_ref, v_ref, o_ref, lse_ref, do_ref, dqn_ref, dqp_ref, dkn_ref, dv_ref, dkp_ref):
        @pl.when(pl.program_id(1) == 0)
        def _():
            dkn_ref[...] = jnp.zeros_like(dkn_ref)
            dv_ref[...] = jnp.zeros_like(dv_ref)
            dkp_ref[...] = jnp.zeros_like(dkp_ref)

        qn_b, qp_b = qn_ref[...].astype(BF16), qp_ref[...].astype(BF16)
        kn_b, kp_b, v_b = kn_ref[...].astype(BF16), kp_ref[...].astype(BF16), v_ref[...].astype(BF16)
        sc, scale = _attn_scores(qn_b, qp_b, kn_b, kp_b, pl.program_id(1) * tq)
        p = jnp.exp(sc - lse_ref[...])
        dov = do_ref[...]
        do_b = dov.astype(BF16)
        p_b = p.astype(BF16)
        dv_ref[...] += lax.dot_general(p_b, do_b, (((0,), (0,)), ((), ())), preferred_element_type=F32)
        dp = lax.dot_general(do_b, v_b, (((1,), (1,)), ((), ())), preferred_element_type=F32)
        delta = jnp.sum(dov * o_ref[...], axis=-1, keepdims=True)
        ds = (p * (dp - delta) * scale).astype(BF16)
        dqn_ref[...] = jnp.dot(ds, kn_b, preferred_element_type=F32)
        dqp_ref[...] = jnp.dot(ds, kp_b, preferred_element_type=F32)
        dkn_ref[...] += lax.dot_general(ds, qn_b, (((0,), (0,)), ((), ())), preferred_element_type=F32)
        dkp_ref[...] += lax.dot_general(ds, qp_b, (((0,), (0,)), ((), ())), preferred_element_type=F32)

    qblk = lambda c: pl.BlockSpec((None, tq, c), lambda hh, i: (hh, i, 0))
    kblk = lambda c: pl.BlockSpec((None, s, c), lambda hh, i: (hh, 0, 0))
    return _pallas(
        body, name="mla_attn_bwd", grid=(h, s // tq),
        in_specs=[qblk(QK_NOPE), qblk(QK_ROPE), kblk(QK_NOPE), pl.BlockSpec((s, QK_ROPE), lambda hh, i: (0, 0)), kblk(V_HEAD),
                  qblk(V_HEAD), qblk(1), qblk(V_HEAD)],
        out_specs=[qblk(QK_NOPE), qblk(QK_ROPE), kblk(QK_NOPE), kblk(V_HEAD), kblk(QK_ROPE)],
        out_shape=[jax.ShapeDtypeStruct((h, s, QK_NOPE), F32), jax.ShapeDtypeStruct((h, s, QK_ROPE), F32),
                   jax.ShapeDtypeStruct((h, s, QK_NOPE), F32), jax.ShapeDtypeStruct((h, s, V_HEAD), F32),
                   jax.ShapeDtypeStruct((h, s, QK_ROPE), F32)],
        sem=("parallel", "arbitrary"), comm=comm,
    )(qn, qp, kn, kp, v, o, lse, do)


def _my_pos():
    return lax.axis_index("x"), lax.axis_index("y"), lax.axis_index("c")


def _dev_index(px, py, pc):
    return 4 * px + 2 * py + pc


def all_gather(name, shard):
    r, c = shard.shape

    def body(x_ref, out_ref, send_sems, recv_sems, local_sem):
        x, y, cc = _my_pos()
        me, sibling = (x, y, cc), (x, y, 1 - cc)
        chips = [(1 - x, y), (x, 1 - y), (1 - x, 1 - y)]

        def rows(px, py, pc):
            return out_ref.at[_dev_index(px, py, pc)]

        def copy(kk, block, to, src=None):
            return pltpu.make_async_remote_copy(
                src_ref=rows(*block) if src is None else src, dst_ref=rows(*block),
                send_sem=send_sems.at[kk], recv_sem=recv_sems.at[kk], device_id=to, device_id_type=MESH)

        mine = pltpu.make_async_copy(x_ref, rows(*me), local_sem)
        mine.start()
        first = [copy(0, me, sibling, src=x_ref)]
        first += [copy(1 + j, me, (*chip, cc), src=x_ref) for j, chip in enumerate(chips)]
        for cp in first:
            cp.start()
        passed = [copy(4 + j, (*chip, cc), sibling) for j, chip in enumerate(chips)]
        for j, chip in enumerate(chips):
            copy(1 + j, (*chip, cc), me).wait_recv()
            passed[j].start()
        copy(0, sibling, me).wait_recv()
        for j, chip in enumerate(chips):
            copy(4 + j, (*chip, 1 - cc), me).wait_recv()
        for cp in first + passed:
            cp.wait_send()
        mine.wait()

    return pl.pallas_call(
        body, name=name, out_shape=jax.ShapeDtypeStruct((N_DEV, r, c), shard.dtype),
        in_specs=[pl.BlockSpec(memory_space=pl.ANY)], out_specs=pl.BlockSpec(memory_space=pl.ANY),
        scratch_shapes=[pltpu.SemaphoreType.DMA((7,)), pltpu.SemaphoreType.DMA((7,)), pltpu.SemaphoreType.DMA],
    )(shard)


def _flip(kind):
    x, y, c = _my_pos()
    return {'c': (x, y, 1 - c), 'x': (1 - x, y, c), 'y': (x, 1 - y, c), 'xy': (1 - x, 1 - y, c)}[kind]


def exchange_sibling(name, g):
    _, r, c = g.shape

    def body(g_ref, out_ref, send_sems, recv_sems):
        x, y, cc = _my_pos()
        copies = []
        for px in range(2):
            for py in range(2):
                slot = 2 * px + py
                copies.append(pltpu.make_async_remote_copy(
                    src_ref=g_ref.at[_dev_index(px, py, 1 - cc)], dst_ref=out_ref.at[slot],
                    send_sem=send_sems.at[slot], recv_sem=recv_sems.at[slot], device_id=(x, y, 1 - cc), device_id_type=MESH))
        for cp in copies:
            cp.start()
        for cp in copies:
            cp.wait()

    return pl.pallas_call(
        body, name=name, out_shape=jax.ShapeDtypeStruct((4, r, c), g.dtype),
        in_specs=[pl.BlockSpec(memory_space=pl.ANY)], out_specs=pl.BlockSpec(memory_space=pl.ANY),
        scratch_shapes=[pltpu.SemaphoreType.DMA((4,)), pltpu.SemaphoreType.DMA((4,))],
    )(g)


def exchange_chips(name, hsum):
    _, r, c = hsum.shape

    def body(h_ref, out_ref, send_sems, recv_sems):
        x, y, cc = _my_pos()
        copies = []
        for j, (px, py) in enumerate([(1 - x, y), (x, 1 - y), (1 - x, 1 - y)]):
            copies.append(pltpu.make_async_remote_copy(
                src_ref=h_ref.at[2 * px + py], dst_ref=out_ref.at[j],
                send_sem=send_sems.at[j], recv_sem=recv_sems.at[j], device_id=(px, py, cc), device_id_type=MESH))
        for cp in copies:
            cp.start()
        for cp in copies:
            cp.wait()

    return pl.pallas_call(
        body, name=name, out_shape=jax.ShapeDtypeStruct((3, r, c), hsum.dtype),
        in_specs=[pl.BlockSpec(memory_space=pl.ANY)], out_specs=pl.BlockSpec(memory_space=pl.ANY),
        scratch_shapes=[pltpu.SemaphoreType.DMA((3,)), pltpu.SemaphoreType.DMA((3,))],
    )(hsum)


def add_slots(name, *terms):
    n, r, c = terms[0].shape
    tr = _pick(r, 512, SUBLANES)

    def body(*refs):
        acc = refs[0][...]
        for t in refs[1:-1]:
            acc = acc + t[...]
        refs[-1][...] = acc

    blk = pl.BlockSpec((None, tr, c), lambda s_, i: (s_, i, 0))
    return pl.pallas_call(
        body, name=name, grid=(n, r // tr), in_specs=[blk] * len(terms), out_specs=blk,
        out_shape=jax.ShapeDtypeStruct((n, r, c), F32), compiler_params=_params(("parallel", "parallel")),
    )(*terms)


def _rs_add_sibling(name, g, from_sibling, cc):
    _, r, c = g.shape
    tr = _pick(r, 512, SUBLANES * 2)

    def body(cc_ref, g_ref, s_ref, o_ref, ob_ref):
        tot = g_ref[...] + s_ref[...]
        o_ref[...] = tot
        ob_ref[...] = tot.astype(BF16)

    blk = pl.BlockSpec((None, tr, c), lambda s_, i, cc_ref: (s_, i, 0))
    return pl.pallas_call(
        body, name=name,
        grid_spec=pltpu.PrefetchScalarGridSpec(
            num_scalar_prefetch=1, grid=(4, r // tr),
            in_specs=[pl.BlockSpec((None, None, tr, c), lambda s_, i, cc_ref: (s_, cc_ref[0], i, 0)), blk], out_specs=[blk, blk]),
        out_shape=[jax.ShapeDtypeStruct((4, r, c), F32), jax.ShapeDtypeStruct((4, r, c), BF16)],
        compiler_params=_params(("parallel", "parallel")),
    )(cc.reshape(1).astype(jnp.int32), g.reshape(4, 2, r, c), from_sibling)


def _rs_add_chips(name, chip_sum, from_chips, slot):
    _, r, c = chip_sum.shape
    tr = _pick(r, 512, SUBLANES * 2)

    def body(slot_ref, h_ref, f0_ref, f1_ref, f2_ref, o_ref):
        o_ref[...] = ((h_ref[...] + f0_ref[...].astype(F32)) + f1_ref[...].astype(F32)) + f2_ref[...].astype(F32)

    def from_blk(j):
        return pl.BlockSpec((None, tr, c), lambda i, slot_ref: (j, i, 0))

    return pl.pallas_call(
        body, name=name,
        grid_spec=pltpu.PrefetchScalarGridSpec(
            num_scalar_prefetch=1, grid=(r // tr,),
            in_specs=[pl.BlockSpec((None, tr, c), lambda i, slot_ref: (slot_ref[0], i, 0)), from_blk(0), from_blk(1), from_blk(2)],
            out_specs=pl.BlockSpec((tr, c), lambda i, slot_ref: (i, 0))),
        out_shape=jax.ShapeDtypeStruct((r, c), F32), compiler_params=_params(("parallel",)),
    )(slot.reshape(1).astype(jnp.int32), chip_sum, from_chips, from_chips, from_chips)


def rs_chip_sum(tag, g):
    _, _, cc = _my_pos()
    from_sibling = exchange_sibling("rs_sibling_" + tag, g)
    return _rs_add_sibling("rs_add_sibling_" + tag, g, from_sibling, cc)


def rs_finish(tag, chip_sum, from_chips):
    x, y, _ = _my_pos()
    return _rs_add_chips("rs_add_chips_" + tag, chip_sum, from_chips, 2 * x + y)


def reduce_scatter(tag, g):
    chip_sum, chip_sum_b = rs_chip_sum(tag, g)
    return rs_finish(tag, chip_sum, exchange_chips("rs_chips_" + tag, chip_sum_b))


class GatherIci:
    def __init__(self, shards):
        self.inputs = list(shards)
        self.out_shapes = [jax.ShapeDtypeStruct((N_DEV,) + s.shape, s.dtype) for s in shards]
        self.n_remote, self.n_local = 3 * len(shards), len(shards)

    def make(self, cins, couts, send, recv, local):
        x, y, cc = _my_pos()
        me = _dev_index(x, y, cc)
        copies = []
        for w, (src, out) in enumerate(zip(cins, couts, strict=True)):
            copies.append(pltpu.make_async_copy(src, out.at[me], local.at[w]))
            for j, (px, py) in enumerate([(1 - x, y), (x, 1 - y), (1 - x, 1 - y)]):
                copies.append(pltpu.make_async_remote_copy(
                    src_ref=src, dst_ref=out.at[me], send_sem=send.at[3 * w + j], recv_sem=recv.at[3 * w + j],
                    device_id=(px, py, cc), device_id_type=MESH))
        return copies


class RsChips:
    def __init__(self, chip_sums):
        self.inputs = list(chip_sums)
        self.out_shapes = [jax.ShapeDtypeStruct((3,) + h.shape[1:], h.dtype) for h in chip_sums]
        self.n_remote, self.n_local = 3 * len(chip_sums), 0

    def make(self, cins, couts, send, recv, local):
        x, y, cc = _my_pos()
        copies = []
        for w, (h_ref, out) in enumerate(zip(cins, couts, strict=True)):
            for j, (px, py) in enumerate([(1 - x, y), (x, 1 - y), (1 - x, 1 - y)]):
                copies.append(pltpu.make_async_remote_copy(
                    src_ref=h_ref.at[2 * px + py], dst_ref=out.at[j], send_sem=send.at[3 * w + j], recv_sem=recv.at[3 * w + j],
                    device_id=(px, py, cc), device_id_type=MESH))
        return copies


class SiblingSwap:
    def __init__(self, gs):
        self.inputs = list(gs)
        self.out_shapes = [jax.ShapeDtypeStruct((4,) + g.shape[1:], g.dtype) for g in gs]
        self.n_remote, self.n_local = 4 * len(gs), 0

    def make(self, cins, couts, send, recv, local):
        x, y, cc = _my_pos()
        copies = []
        for w, (g_ref, out) in enumerate(zip(cins, couts, strict=True)):
            for px in range(2):
                for py in range(2):
                    q = 4 * w + 2 * px + py
                    copies.append(pltpu.make_async_remote_copy(
                        src_ref=g_ref.at[_dev_index(px, py, 1 - cc)], dst_ref=out.at[2 * px + py],
                        send_sem=send.at[q], recv_sem=recv.at[q], device_id=(x, y, 1 - cc), device_id_type=MESH))
        return copies


class _SemSlice:
    def __init__(self, base, start):
        self.base, self.start = base, start

    @property
    def at(self):
        return self

    def __getitem__(self, k):
        return self.base.at[self.start + k]


class CommGroup:
    def __init__(self, plans):
        self.plans = [p for p in plans if p.inputs]
        self.inputs = [a for p in self.plans for a in p.inputs]
        self.out_shapes = [s_ for p in self.plans for s_ in p.out_shapes]
        self.n_remote = sum(p.n_remote for p in self.plans)
        self.n_local = sum(p.n_local for p in self.plans)

    def make(self, cins, couts, send, recv, local):
        copies, i0, o0, r0, l0 = [], 0, 0, 0, 0
        for p in self.plans:
            ni, no = len(p.inputs), len(p.out_shapes)
            copies += p.make(cins[i0:i0 + ni], couts[o0:o0 + no], _SemSlice(send, r0), _SemSlice(recv, r0), _SemSlice(local, l0))
            i0, o0, r0, l0 = i0 + ni, o0 + no, r0 + p.n_remote, l0 + p.n_local
        return copies


def gather_d2d(name, arrays):
    n = len(arrays)

    def body(*refs):
        outs, send, recv = refs[n:2 * n], refs[2 * n], refs[2 * n + 1]
        x, y, cc = _my_pos()
        copies = []
        for w, out in enumerate(outs):
            for px in range(2):
                for py in range(2):
                    q = 4 * w + 2 * px + py
                    slab = out.at[_dev_index(px, py, cc)]
                    copies.append(pltpu.make_async_remote_copy(
                        src_ref=slab, dst_ref=slab, send_sem=send.at[q], recv_sem=recv.at[q],
                        device_id=(x, y, 1 - cc), device_id_type=MESH))
        for cp in copies:
            cp.start()
        for cp in copies:
            cp.wait()

    any_spec = pl.BlockSpec(memory_space=pl.ANY)
    return pl.pallas_call(
        body, name=name, out_shape=[jax.ShapeDtypeStruct(a.shape, a.dtype) for a in arrays],
        in_specs=[any_spec] * n, out_specs=[any_spec] * n, input_output_aliases={i: i for i in range(n)},
        scratch_shapes=[pltpu.SemaphoreType.DMA((4 * n,)), pltpu.SemaphoreType.DMA((4 * n,))],
    )(*arrays)


def _pallas(body, *, name, grid, in_specs, out_specs, out_shape, scratch_shapes=(), sem, comm=None):
    in_specs, out_specs, out_shape, scratch_shapes = list(in_specs), list(out_specs), list(out_shape), list(scratch_shapes)
    if comm is None:
        return pl.pallas_call(body, name=name, grid=grid, in_specs=in_specs, out_specs=out_specs, out_shape=out_shape,
                              scratch_shapes=scratch_shapes, compiler_params=_params(sem))
    n_in, n_out, n_scr = len(in_specs), len(out_specs), len(scratch_shapes)
    nci, nco = len(comm.inputs), len(comm.out_shapes)

    def body2(*refs):
        ins, cins = refs[:n_in], refs[n_in:n_in + nci]
        o0 = n_in + nci
        outs, couts = refs[o0:o0 + n_out], refs[o0 + n_out:o0 + n_out + nco]
        s0 = o0 + n_out + nco
        scr = refs[s0:s0 + n_scr]
        send, recv, local = refs[s0 + n_scr:]
        pids = [pl.program_id(k) for k in range(len(grid))]
        first = functools.reduce(jnp.logical_and, [p == 0 for p in pids])
        last = functools.reduce(jnp.logical_and, [p == g - 1 for p, g in zip(pids, grid)])

        @pl.when(first)
        def _():
            for cp in comm.make(cins, couts, send, recv, local):
                cp.start()

        body(*ins, *outs, *scr)

        @pl.when(last)
        def _():
            for cp in comm.make(cins, couts, send, recv, local):
                cp.wait()

    any_spec = pl.BlockSpec(memory_space=pl.ANY)
    call = pl.pallas_call(
        body2, name=name, grid=grid, in_specs=in_specs + [any_spec] * nci, out_specs=out_specs + [any_spec] * nco,
        out_shape=out_shape + list(comm.out_shapes),
        scratch_shapes=scratch_shapes + [pltpu.SemaphoreType.DMA((comm.n_remote,)), pltpu.SemaphoreType.DMA((comm.n_remote,)),
                                         pltpu.SemaphoreType.DMA((max(comm.n_local, 1),))],
        compiler_params=_params(tuple("arbitrary" for _ in grid)))
    return lambda *args: call(*args, *comm.inputs)


PACK_W = 1024


class Pack:
    def __init__(self, entries, row_unit):
        self.entries = entries
        self.sizes = [int(np.prod(sh)) for _, sh in entries]
        self.offsets = np.concatenate([[0], np.cumsum(self.sizes)]).tolist()
        self.total = _round_up(self.offsets[-1], PACK_W * row_unit)
        self.rows = self.total // PACK_W

    def pack(self, arrays, dtype, lead=()):
        flat = [arrays[n].astype(dtype).reshape(lead + (-1,)) for n, _ in self.entries]
        pad = self.total - self.offsets[-1]
        if pad:
            flat.append(jnp.zeros(lead + (pad,), dtype))
        return jnp.concatenate(flat, axis=-1).reshape(lead + (self.rows, PACK_W))

    def unpack(self, buf, lead=()):
        flat = buf.reshape(lead + (self.total,))
        out = {}
        for (n, sh), off, sz in zip(self.entries, self.offsets, self.sizes):
            out[n] = lax.slice_in_dim(flat, off, off + sz, axis=len(lead)).reshape(lead + tuple(sh))
        return out


def _gathered_to_full(g, how):
    _, a, b = g.shape
    if how == 'row':
        return g.reshape(N_DEV * a, b)
    return jnp.transpose(g, (1, 0, 2)).reshape(a, N_DEV * b)


def _full_to_shards(w, how):
    a, b = w.shape
    if how == 'row':
        return w.reshape(N_DEV, a // N_DEV, b)
    return jnp.transpose(w.reshape(a, N_DEV, b // N_DEV), (1, 0, 2))


def _to_heads(t, width):
    s, c = t.shape
    return jnp.transpose(t.reshape(s, c // width, width), (1, 0, 2))


def _from_heads(t):
    h, s, w = t.shape
    return jnp.transpose(t, (1, 0, 2)).reshape(s, h * w)


def _rot_matrix():
    half = QK_ROPE // 2
    rot = np.zeros((QK_ROPE, QK_ROPE), np.float32)
    for i in range(half):
        rot[i + half, i] = -1.0
        rot[i, i + half] = 1.0
    return jnp.asarray(rot)


def _inv_freq2():
    half = QK_ROPE // 2
    inv = ROPE_THETA ** (-np.arange(half, dtype=np.float32) / half)
    return jnp.asarray(np.concatenate([inv, inv])[None, :].astype(np.float32))


def kernel(x, positions, attn_norm_g, w_in, rwkv_mu, rwkv_w0, rwkv_w2, rwkv_a0, rwkv_a2, rwkv_g2, rwkv_k_k, rwkv_k_a, rwkv_r_k, rwkv_gn_w, rwkv_gn_b, mla_q_norm_g, mla_w_uq, mla_kv_norm_g, mla_w_ukv, w_out, ffn_norm_g, ffn_w_gate, ffn_w_up, ffn_conv_w, ffn_conv_b, ffn_w_down, final_norm_g, loss_target, m_attn_norm_g, m_w_in, m_rwkv_mu, m_rwkv_w0, m_rwkv_w2, m_rwkv_a0, m_rwkv_a2, m_rwkv_g2, m_rwkv_k_k, m_rwkv_k_a, m_rwkv_r_k, m_rwkv_gn_w, m_rwkv_gn_b, m_mla_q_norm_g, m_mla_w_uq, m_mla_kv_norm_g, m_mla_w_ukv, m_w_out, m_ffn_norm_g, m_ffn_w_gate, m_ffn_w_up, m_ffn_conv_w, m_ffn_conv_b, m_ffn_w_down, m_final_norm_g, v_attn_norm_g, v_w_in, v_rwkv_mu, v_rwkv_w0, v_rwkv_w2, v_rwkv_a0, v_rwkv_a2, v_rwkv_g2, v_rwkv_k_k, v_rwkv_k_a, v_rwkv_r_k, v_rwkv_gn_w, v_rwkv_gn_b, v_mla_q_norm_g, v_mla_w_uq, v_mla_kv_norm_g, v_mla_w_ukv, v_w_out, v_ffn_norm_g, v_ffn_w_gate, v_ffn_w_up, v_ffn_conv_w, v_ffn_conv_b, v_ffn_w_down, v_final_norm_g):
    given = dict(locals())
    wts = {n: given[n] for n in WEIGHTS}
    mom_m = {n: given["m_" + n] for n in WEIGHTS}
    mom_v = {n: given["v_" + n] for n in WEIGHTS}
    out_shapes = {n: wts[n].shape for n in WEIGHTS}

    def local2d(n, a):
        if n == 'rwkv_r_k' or a.ndim <= 2:
            return a.reshape(1, -1)
        return a.reshape(a.shape[1:])

    w2d = {n: local2d(n, wts[n]) for n in WEIGHTS}
    m2d = {n: local2d(n, mom_m[n]) for n in WEIGHTS}
    v2d = {n: local2d(n, mom_v[n]) for n in WEIGHTS}

    xs = x.reshape(x.shape[1:])
    tgt = loss_target.reshape(loss_target.shape[1:])
    s, d = xs.shape
    c_rwkv = w2d['rwkv_w0'].shape[1]
    n_rh = c_rwkv // RWKV_HEAD
    decay_lora, aaa_lora, gate_lora = w2d['rwkv_w2'].shape[0], w2d['rwkv_a2'].shape[0], w2d['rwkv_g2'].shape[0]
    q_lora, kv_lora = w2d['mla_q_norm_g'].shape[1], w2d['mla_kv_norm_g'].shape[1]
    shift_dim = w2d['rwkv_mu'].shape[1]
    d_in = w2d['w_in'].shape[1] * N_DEV
    d_in_pad = _round_up(d_in, LANES)
    n_mh = w2d['mla_w_uq'].shape[1] * N_DEV // (QK_NOPE + QK_ROPE)
    d_ff = w2d['ffn_conv_b'].shape[1]
    tm = _pick(s, 256, SUBLANES)
    tm_wide = _pick(s, 128, SUBLANES)

    nb = {n: w2d[n].shape[1] for n in BIG if BIG[n] == 'col'}
    nbp = {n: _round_up(v_, LANES) for n, v_ in nb.items()}
    shards = {}
    for n in BIG:
        w = w2d[n].astype(BF16)
        if BIG[n] == 'col':
            w = jnp.pad(w, ((0, 0), (0, nbp[n] - nb[n])))
        elif n == 'ffn_w_down':
            w = jnp.pad(w, ((0, nbp['ffn_w_gate'] - w.shape[0]), (0, 0)))
        shards[n] = w

    def as_used(n, g):
        return g if BIG[n] == 'col' else g.reshape(N_DEV * g.shape[1], g.shape[2])

    gathered = {'w_in': as_used('w_in', all_gather("gather_w_in", shards['w_in']))}
    later = [n for n in BIG if n != 'w_in']
    f_pad = N_DEV * nbp['ffn_w_gate']
    small_pack = Pack([(n, w2d[n].shape) for n in SMALL_SHARDED], 8)
    small_all = all_gather("gather_small", small_pack.pack(w2d, F32))
    full = {}
    for n, g in small_pack.unpack(small_all, lead=(N_DEV,)).items():
        full[n] = _gathered_to_full(g, SMALL_SHARDED[n])
    conv_w_pad = pad_cols(full['ffn_conv_w'], nb['ffn_w_gate'], nbp['ffn_w_gate'])
    conv_b_pad = pad_cols(w2d['ffn_conv_b'], nb['ffn_w_gate'], nbp['ffn_w_gate'])

    (h1,) = rowwise("rms_attn", _rms_fn, [xs, w2d['attn_norm_g']], ['row', 'const'], [('row', d, BF16)], heads=1, s=s, tm=tm)
    def gather_behind(names, run):
        *res, = run(GatherIci([shards[n] for n in names]))
        landed = res[len(res) - len(names):]
        for n, g in zip(names, gather_d2d("gather_d2d_" + names[0], landed), strict=True):
            gathered[n] = as_used(n, g)
        return res[:len(res) - len(names)]

    (proj_pad,) = gather_behind(['mla_w_uq', 'mla_w_ukv', 'w_out'], lambda c: mm_sh("proj_in", h1, gathered['w_in'], comm=c))
    proj = unpad_cols(proj_pad, nb['w_in'], nbp['w_in'])
    p_rwkv = proj[:, :shift_dim]
    c_q = proj[:, shift_dim:shift_dim + q_lora]
    c_kv = proj[:, shift_dim + q_lora:shift_dim + q_lora + kv_lora]
    k_pe = proj[:, shift_dim + q_lora + kv_lora:d_in]
    shifted = token_shift_fwd(p_rwkv, w2d['rwkv_mu'], tm_wide)
    o1, o2, o3 = c_rwkv, 2 * c_rwkv, 3 * c_rwkv
    hr = _to_heads(shifted[:, :o1], RWKV_HEAD)
    hk = _to_heads(shifted[:, o1:o2], RWKV_HEAD)
    hv = _to_heads(shifted[:, o2:o3], RWKV_HEAD)
    hw = shifted[:, o3:o3 + decay_lora]
    ha = shifted[:, o3 + decay_lora:o3 + decay_lora + aaa_lora]
    hg = shifted[:, o3 + decay_lora + aaa_lora:]

    def per_head(vec):
        return vec.reshape(n_rh, 1, RWKV_HEAD)

    def lora_heads(w):
        return jnp.transpose(w.reshape(w.shape[0], n_rh, RWKV_HEAD), (1, 0, 2))

    pre_args = [hk, hw, ha, hg, per_head(w2d['rwkv_w0']), lora_heads(full['rwkv_w2']), per_head(w2d['rwkv_a0']),
                lora_heads(full['rwkv_a2']), lora_heads(full['rwkv_g2']), per_head(w2d['rwkv_k_k']), per_head(w2d['rwkv_k_a'])]
    pre_kinds = ['hrow', 'row', 'row', 'row', 'hconst', 'hconst', 'hconst', 'hconst', 'hconst', 'hconst', 'hconst']
    decay, kx, a_sc, b_sc, gate_r = gather_behind(['ffn_w_gate'], lambda c: rowwise(
        "rwkv_pre", _rwkv_pre_fn, pre_args, pre_kinds, [('hrow', RWKV_HEAD, F32)] * 5, heads=n_rh, s=s, tm=tm, comm=c))
    y_scan, ckpt = gather_behind(['ffn_w_up'], lambda c: scan_fwd(hr, decay, kx, hv, a_sc, b_sc, comm=c))
    post_args = [y_scan, hr, kx, hv, gate_r, per_head(w2d['rwkv_gn_w']), per_head(w2d['rwkv_gn_b']), per_head(w2d['rwkv_r_k'])]
    post_kinds = ['hrow'] * 5 + ['hconst'] * 3
    (y_rwkv_h,) = rowwise("rwkv_post", _rwkv_post_fn, post_args, post_kinds, [('hrow', RWKV_HEAD, F32)], heads=n_rh, s=s, tm=tm)

    pos = positions.reshape(s, 1).astype(F32)
    rot, inv2 = _rot_matrix(), _inv_freq2()
    mla_args = [c_q, c_kv, k_pe, pos, w2d['mla_q_norm_g'], w2d['mla_kv_norm_g'], inv2, rot]
    mla_kinds = ['row', 'row', 'row', 'row', 'const', 'const', 'const', 'const']
    qn, kvn, kp_rot, cos2, sin2 = rowwise(
        "mla_pre", _mla_pre_fn, mla_args, mla_kinds,
        [('row', q_lora, BF16), ('row', kv_lora, BF16), ('row', QK_ROPE, F32), ('row', QK_ROPE, F32), ('row', QK_ROPE, F32)],
        heads=1, s=s, tm=tm)
    q = unpad_cols(mm_sh("proj_q", qn, gathered['mla_w_uq']), nb['mla_w_uq'], nbp['mla_w_uq'])
    kv = unpad_cols(mm_sh("proj_kv", kvn, gathered['mla_w_ukv']), nb['mla_w_ukv'], nbp['mla_w_ukv'])
    q_h = _to_heads(q, QK_NOPE + QK_ROPE)
    kv_h = _to_heads(kv, QK_NOPE + V_HEAD)
    q_nope, q_pe = q_h[..., :QK_NOPE], q_h[..., QK_NOPE:]
    k_nope, v_att = kv_h[..., :QK_NOPE], kv_h[..., QK_NOPE:]
    ropeq_args = [q_pe, cos2, sin2, rot]
    ropeq_kinds = ['hrow', 'row', 'row', 'const']
    (q_pe_rot,) = rowwise("rope_q", _rope_q_fn, ropeq_args, ropeq_kinds, [('hrow', QK_ROPE, F32)], heads=n_mh, s=s, tm=tm)
    o_att, lse = gather_behind(['ffn_w_down'], lambda c: attn_fwd(q_nope, q_pe_rot, k_nope, kp_rot, v_att, comm=c))
    ycat = jnp.concatenate([_from_heads(y_rwkv_h), _from_heads(o_att)], axis=-1).astype(BF16)
    x1 = mm("proj_out", ycat, gathered['w_out'], add=xs)
    (h2,) = rowwise("rms_ffn", _rms_fn, [x1, w2d['ffn_norm_g']], ['row', 'const'], [('row', d, BF16)], heads=1, s=s, tm=tm)
    gate_pre = mm_sh("ffn_gate", h2, gathered['ffn_w_gate'])
    up = mm_sh("ffn_up", h2, gathered['ffn_w_up'])
    act = ffn_act_fwd(gate_pre, up, conv_w_pad, conv_b_pad)
    x2 = mm("ffn_down", act, gathered['ffn_w_down'], add=x1)

    ones = jnp.ones((s, 1), F32)
    fin_g = w2d['final_norm_g']
    d_x2, dg_final_p, loss_rows = rowwise_vjp("loss_bwd", _loss_fn, [x2, fin_g, tgt], ['row', 'const', 'row'], [ones], ['row'],
                                              [0, 1], heads=1, s=s, tm=tm, primal=True)
    d_x2_b = d_x2.astype(BF16)
    d_act = mm_nt("d_act", d_x2_b, gathered['ffn_w_down'], out_dtype=BF16)
    gsh, chip_sums, from_chips = {}, {}, {}

    def scatter_behind(run, ici=(), swap=()):
        *res, = run(CommGroup([RsChips([chip_sums[n][1] for n in ici]), SiblingSwap([gsh[n] for n in swap])]))
        n_own = len(res) - len(ici) - len(swap)
        from_chips.update(zip(ici, res[n_own:n_own + len(ici)], strict=True))
        _, _, cc = _my_pos()
        for n, from_sibling in zip(swap, res[n_own + len(ici):], strict=True):
            chip_sums[n] = _rs_add_sibling("rs_add_sibling_" + n, gsh[n], from_sibling, cc)
        return res[:n_own]

    gsh['ffn_w_down'] = mm_tn("dw_down", act, d_x2_b).reshape(N_DEV, nbp['ffn_w_gate'], d)
    d_gate, d_up, dcw_p, dcb_p = ffn_act_bwd1(gate_pre, up, conv_w_pad, conv_b_pad, d_act)
    d_gp = ffn_act_bwd2(d_gate, conv_w_pad)
    (d_h2_g,) = scatter_behind(lambda c: mm_sh_nt("d_h2_gate", d_gp, gathered['ffn_w_gate'], comm=c), swap=['ffn_w_down'])
    d_h2 = mm_sh_nt("d_h2_up", d_up, gathered['ffn_w_up'], add=d_h2_g)
    gsh['ffn_w_gate'] = mm_sh_out("dw_gate", h2, d_gp)
    (gsh['ffn_w_up'],) = scatter_behind(lambda c: mm_sh_out("dw_up", h2, d_up, comm=c), swap=['ffn_w_gate'])
    d_x1n, dg_ffn_p = rowwise_vjp("rms_ffn_bwd", _rms_fn, [x1, w2d['ffn_norm_g']], ['row', 'const'], [d_h2], ['row'], [0, 1],
                                  heads=1, s=s, tm=tm)
    d_x1 = add_slots("d_x1_add", d_x1n[None], d_x2[None])[0]
    d_x1_b = d_x1.astype(BF16)
    d_ycat = mm_nt("d_ycat", d_x1_b, gathered['w_out'])
    gsh['w_out'] = mm_tn("dw_out", ycat, d_x1_b).reshape(N_DEV, d // N_DEV, d)
    d_yr_h = _to_heads(d_ycat[:, :c_rwkv], RWKV_HEAD)
    d_o_h = _to_heads(d_ycat[:, c_rwkv:], V_HEAD)

    d_qn_h, d_qpr_h, d_kn_h, d_v_h, d_kp_h = scatter_behind(
        lambda c: attn_bwd(q_nope, q_pe_rot, k_nope, kp_rot, v_att, o_att, lse, d_o_h, comm=c),
        ici=['ffn_w_down'], swap=['ffn_w_up', 'w_out'])
    (d_qp_h,) = rowwise_vjp("rope_q_bwd", _rope_q_fn, ropeq_args, ropeq_kinds, [d_qpr_h], ['hrow'], [0], heads=n_mh, s=s, tm=tm)
    d_q = pad_cols(_from_heads(jnp.concatenate([d_qn_h, d_qp_h], axis=-1)).astype(BF16), nb['mla_w_uq'], nbp['mla_w_uq'])
    d_kv = pad_cols(_from_heads(jnp.concatenate([d_kn_h, d_v_h], axis=-1)).astype(BF16), nb['mla_w_ukv'], nbp['mla_w_ukv'])
    d_kp_rot = headsum("d_kpe_heads", d_kp_h)
    d_qn = mm_sh_nt("d_qn", d_q, gathered['mla_w_uq'])
    d_kvn = mm_sh_nt("d_kvn", d_kv, gathered['mla_w_ukv'])
    gsh['mla_w_uq'] = mm_sh_out("dw_uq", qn, d_q)
    gsh['mla_w_ukv'] = mm_sh_out("dw_ukv", kvn, d_kv)
    d_cq, d_ckv, d_kpe, dg_q_p, dg_kv_p = rowwise_vjp(
        "mla_pre_bwd", _mla_pre_grad_fn, mla_args, mla_kinds, [d_qn, d_kvn, d_kp_rot], ['row', 'row', 'row'], [0, 1, 2, 4, 5],
        heads=1, s=s, tm=tm)

    d_y, d_r_post, d_k_post, d_v_post, d_gate_r, dgnw_p, dgnb_p, drk_p = scatter_behind(
        lambda c: rowwise_vjp("rwkv_post_bwd", _rwkv_post_fn, post_args, post_kinds, [d_yr_h], ['hrow'], list(range(8)),
                              heads=n_rh, s=s, tm=tm, comm=c),
        ici=['ffn_w_gate'], swap=['mla_w_uq', 'mla_w_ukv'])
    d_r_sc, d_w_sc, d_k_sc, d_v_sc, d_a_sc, d_b_sc = scatter_behind(
        lambda c: scan_bwd(hr, decay, kx, hv, a_sc, b_sc, ckpt, d_y, comm=c), ici=['ffn_w_up', 'w_out'])
    d_hk, d_hw_p, d_ha_p, d_hg_p, dw0_p, dw2_p, da0_p, da2_p, dg2_p, dkk_p, dka_p, d_hr, d_hv = scatter_behind(
        lambda c: rowwise_vjp(
            "rwkv_pre_bwd", _rwkv_pre_grad_fn, pre_args + [hr, hv], pre_kinds + ['hrow', 'hrow'],
            [d_w_sc, d_k_sc, d_k_post, d_a_sc, d_b_sc, d_gate_r, d_r_sc, d_r_post, d_v_sc, d_v_post], ['hrow'] * 10,
            list(range(13)), heads=n_rh, s=s, tm=tm, comm=c),
        ici=['mla_w_uq', 'mla_w_ukv'])
    d_shifted = jnp.concatenate([_from_heads(d_hr), _from_heads(d_hk), _from_heads(d_hv), headsum("d_hw_heads", d_hw_p),
                                 headsum("d_ha_heads", d_ha_p), headsum("d_hg_heads", d_hg_p)], axis=-1)
    d_p_rwkv, dmu_p = token_shift_bwd(p_rwkv, w2d['rwkv_mu'], d_shifted, tm_wide)
    d_proj = pad_cols(jnp.concatenate([d_p_rwkv, d_cq, d_ckv, d_kpe], axis=-1).astype(BF16), nb['w_in'], nbp['w_in'])
    gsh['w_in'] = mm_sh_out("dw_in", h1, d_proj)
    chip_sums['w_in'] = rs_chip_sum('w_in', gsh['w_in'])
    (d_h1,) = scatter_behind(lambda c: mm_sh_nt("d_h1", d_proj, gathered['w_in'], comm=c), ici=['w_in'])
    d_xn, dg_attn_p = rowwise_vjp("rms_attn_bwd", _rms_fn, [xs, w2d['attn_norm_g']], ['row', 'const'], [d_h1], ['row'], [0, 1],
                                  heads=1, s=s, tm=tm)
    grad_x = add_slots("grad_x_add", d_xn[None], d_x1[None])[0]

    def from_heads_lora(g):
        return jnp.transpose(g, (1, 0, 2)).reshape(g.shape[1], n_rh * RWKV_HEAD)

    gw = {}
    gw['rwkv_w2'] = from_heads_lora(sum_partials("sum_dw2", dw2_p, True))
    gw['rwkv_a2'] = from_heads_lora(sum_partials("sum_da2", da2_p, True))
    gw['rwkv_g2'] = from_heads_lora(sum_partials("sum_dg2", dg2_p, True))
    dcw_pad = colsum("sum_dconv_w", dcw_p.reshape(dcw_p.shape[0], CONV_W * f_pad)).reshape(CONV_W, f_pad)
    gw['ffn_conv_w'] = unpad_cols(dcw_pad, nb['ffn_w_gate'], nbp['ffn_w_gate'])

    rep = {
        'attn_norm_g': sum_partials("sum_dg_attn", dg_attn_p, False),
        'rwkv_mu': colsum("sum_dmu", dmu_p.reshape(dmu_p.shape[0], shift_dim)),
        'rwkv_w0': sum_partials("sum_dw0", dw0_p, True).reshape(1, c_rwkv),
        'rwkv_a0': sum_partials("sum_da0", da0_p, True).reshape(1, c_rwkv),
        'rwkv_k_k': sum_partials("sum_dkk", dkk_p, True).reshape(1, c_rwkv),
        'rwkv_k_a': sum_partials("sum_dka", dka_p, True).reshape(1, c_rwkv),
        'rwkv_r_k': sum_partials("sum_drk", drk_p, True).reshape(1, c_rwkv),
        'rwkv_gn_w': sum_partials("sum_dgnw", dgnw_p, True).reshape(1, c_rwkv),
        'rwkv_gn_b': sum_partials("sum_dgnb", dgnb_p, True).reshape(1, c_rwkv),
        'mla_q_norm_g': sum_partials("sum_dg_q", dg_q_p, False),
        'mla_kv_norm_g': sum_partials("sum_dg_kv", dg_kv_p, False),
        'ffn_norm_g': sum_partials("sum_dg_ffn", dg_ffn_p, False),
        'ffn_conv_b': unpad_cols(colsum("sum_dconv_b", dcb_p.reshape(dcb_p.shape[0], f_pad)), nb['ffn_w_gate'], nbp['ffn_w_gate']),
        'final_norm_g': sum_partials("sum_dg_final", dg_final_p, False),
        'loss': sum_all("sum_loss", loss_rows.reshape(s // SUBLANES, SUBLANES)),
    }
    rep_pack = Pack([(n, w2d[n].shape) for n in REPLICATED] + [('loss', (1, 1))], 8)
    rep_all = all_gather("gather_rep_grads", rep_pack.pack(rep, F32))
    rep_sum = colsum("sum_rep_grads", rep_all.reshape(N_DEV, rep_pack.total)).reshape(rep_pack.rows, PACK_W)
    rep_g = rep_pack.unpack(rep_sum)
    loss = rep_g.pop('loss').reshape(())

    grads, deltas, new_m, new_v = dict(rep_g), {}, {}, {}
    for n in BIG:
        a, b = w2d[n].shape
        grads[n] = rs_finish(n, chip_sums[n][0], from_chips[n])[:a, :b]
        deltas[n], new_m[n], new_v[n] = rowwise(
            "adamw_" + n, _adamw_fn, [w2d[n], grads[n], m2d[n], v2d[n]], ['row'] * 4, [('row', b, F32)] * 3,
            heads=1, s=a, tm=_pick(a, 256, SUBLANES))
    sm_pack = Pack([(n, w2d[n].shape) for n in SMALL_SHARDED], 8)
    g_shards = {n: _full_to_shards(gw[n], SMALL_SHARDED[n]) for n in SMALL_SHARDED}
    grads.update(sm_pack.unpack(reduce_scatter("small", sm_pack.pack(g_shards, F32, lead=(N_DEV,)))))
    rest_pack = Pack([(n, w2d[n].shape) for n in WEIGHTS if n not in BIG], 8)
    d_r, m_r, v_r = rowwise(
        "adamw_small", _adamw_fn, [rest_pack.pack(w2d, F32), rest_pack.pack(grads, F32), rest_pack.pack(m2d, F32),
                                   rest_pack.pack(v2d, F32)],
        ['row'] * 4, [('row', PACK_W, F32)] * 3, heads=1, s=rest_pack.rows, tm=_pick(rest_pack.rows, 512, SUBLANES))
    deltas.update(rest_pack.unpack(d_r))
    new_m.update(rest_pack.unpack(m_r))
    new_v.update(rest_pack.unpack(v_r))

    def shaped(dct):
        return [dct[n].reshape(out_shapes[n]) for n in WEIGHTS]

    return (loss, grad_x.reshape(x.shape), *shaped(grads), *shaped(deltas), *shaped(new_m), *shaped(new_v))
```

```python
import functools
import math

import jax
import jax.numpy as jnp
import numpy as np
from jax import lax
from jax.experimental import pallas as pl
from jax.experimental.pallas import tpu as pltpu

F32 = jnp.float32
BF16 = jnp.bfloat16
HIGHEST = lax.Precision.HIGHEST
MESH = pl.DeviceIdType.MESH

N_DEV = 8
LANES = 128
SUBLANES = 8
VMEM_LIMIT = 48 * 1024 * 1024
RESIDENT_BYTES = 8 * 1024 * 1024

NORM_EPS = 1e-6
GN_EPS = 64e-5
RWKV_HEAD = 64
QK_NOPE = 128
QK_ROPE = 64
V_HEAD = 128
ROPE_THETA = 10000.0
CONV_W = 3
NEG_INF = -1e30
SCAN_CHUNK = 64
SCAN_HEADS = 8
SCAN_PASSES_SOLVE = 1
SCAN_PASSES_OUT = 1

ADAM_LR = 0.001
ADAM_B1 = 0.9
ADAM_B2 = 0.999
ADAM_EPS = 1e-08
ADAM_WD = 0.01
ADAM_STEP = 10

WEIGHTS = ['attn_norm_g', 'w_in', 'rwkv_mu', 'rwkv_w0', 'rwkv_w2', 'rwkv_a0', 'rwkv_a2', 'rwkv_g2', 'rwkv_k_k',
           'rwkv_k_a', 'rwkv_r_k', 'rwkv_gn_w', 'rwkv_gn_b', 'mla_q_norm_g', 'mla_w_uq', 'mla_kv_norm_g', 'mla_w_ukv',
           'w_out', 'ffn_norm_g', 'ffn_w_gate', 'ffn_w_up', 'ffn_conv_w', 'ffn_conv_b', 'ffn_w_down', 'final_norm_g']
BIG = {'w_in': 'col', 'mla_w_uq': 'col', 'mla_w_ukv': 'col', 'w_out': 'row', 'ffn_w_gate': 'col', 'ffn_w_up': 'col',
       'ffn_w_down': 'row'}
SMALL_SHARDED = {'rwkv_w2': 'col', 'rwkv_a2': 'col', 'rwkv_g2': 'col', 'ffn_conv_w': 'col'}
SHARDED = {**BIG, **SMALL_SHARDED}
REPLICATED = [n for n in WEIGHTS if n not in SHARDED]


def _round_up(n, m):
    return (n + m - 1) // m * m


def _pick(n, cap, unit):
    if n <= cap:
        return n
    best = None
    for t in range(unit, cap + 1, unit):
        if n % t == 0:
            best = t
    assert best is not None, (n, cap, unit)
    return best


def _params(sem):
    return pltpu.CompilerParams(dimension_semantics=sem, vmem_limit_bytes=VMEM_LIMIT)


def mm(name, a, b, add=None, out_dtype=F32):
    m, k = a.shape
    k2, n = b.shape
    assert k == k2, (name, a.shape, b.shape)
    tm = _pick(m, 512, SUBLANES * 2)
    tn = n if k * n * 2 <= RESIDENT_BYTES else _pick(n, 640, LANES)
    has_add = add is not None

    def body(a_ref, b_ref, *rest):
        o_ref = rest[-1]
        acc = jnp.dot(a_ref[...].astype(BF16), b_ref[...].astype(BF16), preferred_element_type=F32)
        if has_add:
            acc = acc + rest[0][...].astype(F32)
        o_ref[...] = acc.astype(o_ref.dtype)

    in_specs = [pl.BlockSpec((tm, k), lambda i, j: (i, 0)), pl.BlockSpec((k, tn), lambda i, j: (0, j))]
    ops = [a, b]
    if has_add:
        in_specs.append(pl.BlockSpec((tm, tn), lambda i, j: (i, j)))
        ops.append(add)
    return pl.pallas_call(
        body, name=name, grid=(m // tm, n // tn), in_specs=in_specs,
        out_specs=pl.BlockSpec((tm, tn), lambda i, j: (i, j)),
        out_shape=jax.ShapeDtypeStruct((m, n), out_dtype),
        compiler_params=_params(("parallel", "parallel")),
    )(*ops)


def mm_nt(name, a, b, out_dtype=F32):
    m, k = a.shape
    n, k2 = b.shape
    assert k == k2, (name, a.shape, b.shape)
    tm = _pick(m, 2048 if m * k * 2 <= RESIDENT_BYTES else 512, SUBLANES * 2)
    tn = _pick(n, 1024, LANES)

    def body(a_ref, b_ref, o_ref):
        acc = lax.dot_general(a_ref[...].astype(BF16), b_ref[...].astype(BF16), (((1,), (1,)), ((), ())),
                              preferred_element_type=F32)
        o_ref[...] = acc.astype(o_ref.dtype)

    return pl.pallas_call(
        body, name=name, grid=(m // tm, n // tn),
        in_specs=[pl.BlockSpec((tm, k), lambda i, j: (i, 0)), pl.BlockSpec((tn, k), lambda i, j: (j, 0))],
        out_specs=pl.BlockSpec((tm, tn), lambda i, j: (i, j)),
        out_shape=jax.ShapeDtypeStruct((m, n), out_dtype),
        compiler_params=_params(("parallel", "parallel")),
    )(a, b)


def mm_sh(name, a, g, out_dtype=F32, comm=None, slabs=False):
    m, k = a.shape
    nd, k2, nbp = g.shape
    assert k == k2, (name, a.shape, g.shape)
    tm = _pick(m, 2048 if m * k * 2 <= RESIDENT_BYTES else 512, SUBLANES * 2)

    def body(a_ref, b_ref, o_ref):
        o_ref[...] = jnp.dot(a_ref[...].astype(BF16), b_ref[...].astype(BF16), preferred_element_type=F32).astype(o_ref.dtype)

    if slabs:
        out_spec, out_shape = pl.BlockSpec((None, tm, nbp), lambda i, j: (j, i, 0)), (nd, m, nbp)
    else:
        out_spec, out_shape = pl.BlockSpec((tm, nbp), lambda i, j: (i, j)), (m, nd * nbp)
    res = _pallas(
        body, name=name, grid=(m // tm, nd),
        in_specs=[pl.BlockSpec((tm, k), lambda i, j: (i, 0)), pl.BlockSpec((None, k, nbp), lambda i, j: (j, 0, 0))],
        out_specs=[out_spec], out_shape=[jax.ShapeDtypeStruct(out_shape, out_dtype)], sem=("parallel", "parallel"), comm=comm,
    )(a, g)
    return res[0] if comm is None else res


def mm_sh_nt(name, a, g, add=None, comm=None):
    nd, k, nbp = g.shape
    slabs = a.ndim == 3
    m = a.shape[1] if slabs else a.shape[0]
    assert a.shape == ((nd, m, nbp) if slabs else (m, nd * nbp)), (name, a.shape, g.shape)
    tm = _pick(m, 512, SUBLANES * 2)
    has_add = add is not None

    def body(a_ref, b_ref, *rest):
        o_ref = rest[-1]
        part = lax.dot_general(a_ref[...].astype(BF16), b_ref[...].astype(BF16), (((1,), (1,)), ((), ())),
                               preferred_element_type=F32)

        @pl.when(pl.program_id(1) == 0)
        def _():
            o_ref[...] = part + rest[0][...] if has_add else part

        @pl.when(pl.program_id(1) != 0)
        def _():
            o_ref[...] += part

    a_spec = pl.BlockSpec((None, tm, nbp), lambda i, j: (j, i, 0)) if slabs else pl.BlockSpec((tm, nbp), lambda i, j: (i, j))
    in_specs = [a_spec, pl.BlockSpec((None, k, nbp), lambda i, j: (j, 0, 0))]
    ops = [a, g]
    if has_add:
        in_specs.append(pl.BlockSpec((tm, k), lambda i, j: (i, 0)))
        ops.append(add)
    res = _pallas(
        body, name=name, grid=(m // tm, nd), in_specs=in_specs,
        out_specs=[pl.BlockSpec((tm, k), lambda i, j: (i, 0))],
        out_shape=[jax.ShapeDtypeStruct((m, k), F32)], sem=("parallel", "arbitrary"), comm=comm,
    )(*ops)
    return res[0] if comm is None else res


def mm_tn(name, a, b):
    m, k = a.shape
    m2, n = b.shape
    assert m == m2, (name, a.shape, b.shape)
    tk = _pick(k, 512, LANES)
    tn = n if m * n * 2 <= RESIDENT_BYTES else _pick(n, 640, LANES)

    def body(a_ref, b_ref, o_ref):
        o_ref[...] = lax.dot_general(a_ref[...].astype(BF16), b_ref[...].astype(BF16), (((0,), (0,)), ((), ())),
                                     preferred_element_type=F32)

    return pl.pallas_call(
        body, name=name, grid=(k // tk, n // tn),
        in_specs=[pl.BlockSpec((m, tk), lambda i, j: (0, i)), pl.BlockSpec((m, tn), lambda i, j: (0, j))],
        out_specs=pl.BlockSpec((tk, tn), lambda i, j: (i, j)),
        out_shape=jax.ShapeDtypeStruct((k, n), F32),
        compiler_params=_params(("parallel", "parallel")),
    )(a, b)


def mm_sh_out(name, a, b, comm=None):
    m, k = a.shape
    slabs = b.ndim == 3
    nbp = b.shape[2] if slabs else b.shape[1] // N_DEV
    assert b.shape == ((N_DEV, m, nbp) if slabs else (m, N_DEV * nbp)), (name, a.shape, b.shape)
    tk = _pick(k, 2048 if k * m * 2 <= RESIDENT_BYTES else 512, LANES)
    b_spec = pl.BlockSpec((None, m, nbp), lambda i, j: (j, 0, 0)) if slabs else pl.BlockSpec((m, nbp), lambda i, j: (0, j))

    def body(a_ref, b_ref, o_ref):
        o_ref[...] = lax.dot_general(a_ref[...].astype(BF16), b_ref[...].astype(BF16), (((0,), (0,)), ((), ())),
                                     preferred_element_type=F32)

    res = _pallas(
        body, name=name, grid=(k // tk, N_DEV),
        in_specs=[pl.BlockSpec((m, tk), lambda i, j: (0, i)), b_spec],
        out_specs=[pl.BlockSpec((None, tk, nbp), lambda i, j: (j, i, 0))],
        out_shape=[jax.ShapeDtypeStruct((N_DEV, k, nbp), F32)], sem=("parallel", "parallel"), comm=comm,
    )(a, b)
    return res[0] if comm is None else res


def pad_cols(y, nb, nbp):
    m = y.shape[0]
    if nb == nbp:
        return y
    return jnp.pad(y.reshape(m, N_DEV, nb), ((0, 0), (0, 0), (0, nbp - nb))).reshape(m, N_DEV * nbp)


def unpad_cols(y, nb, nbp):
    m = y.shape[0]
    if nb == nbp:
        return y
    return y.reshape(m, N_DEV, nbp)[:, :, :nb].reshape(m, N_DEV * nb)


def _in_spec(kind, a, tm):
    if kind == 'row':
        return pl.BlockSpec((tm, a.shape[1]), lambda h, i: (i, 0))
    if kind == 'hrow':
        return pl.BlockSpec((None, tm, a.shape[2]), lambda h, i: (h, i, 0))
    if kind == 'const':
        return pl.BlockSpec(a.shape, lambda h, i: (0, 0))
    assert kind == 'hconst', kind
    return pl.BlockSpec((None,) + a.shape[1:], lambda h, i: (h, 0, 0))


def _row_out(kind, c, dtype, heads, s, tm):
    if kind == 'row':
        assert heads == 1
        return jax.ShapeDtypeStruct((s, c), dtype), pl.BlockSpec((tm, c), lambda h, i: (i, 0))
    return jax.ShapeDtypeStruct((heads, s, c), dtype), pl.BlockSpec((None, tm, c), lambda h, i: (h, i, 0))


def rowwise(name, fn, arrs, kinds, outs, *, heads, s, tm, comm=None):
    n_in = len(arrs)

    def body(*refs):
        vals = fn(*[r[...] for r in refs[:n_in]])
        for o, v in zip(refs[n_in:], vals, strict=True):
            o[...] = v.astype(o.dtype)

    shapes, specs = zip(*[_row_out(k, c, dt, heads, s, tm) for k, c, dt in outs])
    return _pallas(
        body, name=name, grid=(heads, s // tm),
        in_specs=[_in_spec(k, a, tm) for k, a in zip(kinds, arrs, strict=True)],
        out_specs=list(specs), out_shape=list(shapes), sem=("parallel", "parallel"), comm=comm,
    )(*arrs)


def rowwise_vjp(name, fn, arrs, kinds, cots, cot_kinds, wrt, *, heads, s, tm, out_dtypes=None, primal=False, comm=None):
    n_in, n_cot = len(arrs), len(cots)
    nb = s // tm
    out_dtypes = out_dtypes or [F32] * len(wrt)

    def body(*refs):
        vals = [r[...] for r in refs[:n_in]]
        cvals = tuple(r[...].astype(F32) for r in refs[n_in:n_in + n_cot])
        outs = refs[n_in + n_cot:]

        def f(*dv):
            full = list(vals)
            for j, i in enumerate(wrt):
                full[i] = dv[j]
            return tuple(fn(*full))

        prim, vjp_fn = jax.vjp(f, *[vals[i].astype(F32) for i in wrt])
        grads = vjp_fn(cvals)
        for o, g in zip(outs[:len(wrt)], grads, strict=True):
            o[...] = g.astype(o.dtype)
        if primal:
            for o, p in zip(outs[len(wrt):], prim, strict=True):
                o[...] = p.astype(o.dtype)

    shapes, specs = [], []
    for i, dt in zip(wrt, out_dtypes, strict=True):
        kind, a = kinds[i], arrs[i]
        if kind in ('row', 'hrow'):
            c = a.shape[-1]
            sh, sp = _row_out('row' if (kind == 'row' and heads == 1) else 'hrow', c, dt, heads, s, tm)
        else:
            r, c = a.shape[-2:]
            sh = jax.ShapeDtypeStruct((heads, nb, r, c), dt)
            sp = pl.BlockSpec((None, None, r, c), lambda h, i: (h, i, 0, 0))
        shapes.append(sh)
        specs.append(sp)
    if primal:
        for ck, c in zip(cot_kinds, cots, strict=True):
            sh, sp = _row_out(ck, c.shape[-1], F32, heads, s, tm)
            shapes.append(sh)
            specs.append(sp)
    in_specs = [_in_spec(k, a, tm) for k, a in zip(kinds, arrs, strict=True)]
    in_specs += [_in_spec(k, a, tm) for k, a in zip(cot_kinds, cots, strict=True)]
    return _pallas(
        body, name=name, grid=(heads, nb), in_specs=in_specs, out_specs=specs, out_shape=shapes,
        sem=("parallel", "parallel"), comm=comm,
    )(*arrs, *cots)


def colsum(name, x):
    n, m = x.shape
    tc = _pick(m, 32768, LANES) if m % LANES == 0 else m

    def body(x_ref, o_ref):
        acc = x_ref[0:1, :].astype(F32)
        for r in range(1, n):
            acc = acc + x_ref[r:r + 1, :].astype(F32)
        o_ref[...] = acc

    return pl.pallas_call(
        body, name=name, grid=(m // tc,), in_specs=[pl.BlockSpec((n, tc), lambda j: (0, j))],
        out_specs=pl.BlockSpec((1, tc), lambda j: (0, j)), out_shape=jax.ShapeDtypeStruct((1, m), F32),
        compiler_params=_params(("parallel",)),
    )(x)


def headsum(name, x):
    h, s, c = x.shape
    tm = _pick(s, 256, SUBLANES)

    def body(x_ref, o_ref):
        acc = x_ref[0]
        for j in range(1, h):
            acc = acc + x_ref[j]
        o_ref[...] = acc

    return pl.pallas_call(
        body, name=name, grid=(s // tm,), in_specs=[pl.BlockSpec((h, tm, c), lambda i: (0, i, 0))],
        out_specs=pl.BlockSpec((tm, c), lambda i: (i, 0)), out_shape=jax.ShapeDtypeStruct((s, c), F32),
        compiler_params=_params(("parallel",)),
    )(x)


def sum_all(name, x):
    def body(x_ref, o_ref):
        o_ref[...] = jnp.sum(x_ref[...], keepdims=True)

    return pl.pallas_call(body, name=name, out_shape=jax.ShapeDtypeStruct((1, 1), F32))(x)


def sum_partials(name, p, per_head):
    h, nb, r, c = p.shape
    if per_head:
        flat = jnp.transpose(p, (1, 0, 2, 3)).reshape(nb, h * r * c)
        if nb == 1:
            return flat.reshape(h, r, c)
        return colsum(name, flat).reshape(h, r, c)
    flat = p.reshape(h * nb, r * c)
    if h * nb == 1:
        return flat.reshape(r, c)
    return colsum(name, flat).reshape(r, c)


def _rms_fn(x, g):
    xf = x.astype(F32)
    return (xf * lax.rsqrt(jnp.mean(xf * xf, axis=-1, keepdims=True) + NORM_EPS) * g,)


def _softplus(z):
    return jnp.maximum(z, 0.0) + jnp.log(1.0 + jnp.exp(-jnp.abs(z)))


def _rwkv_pre_fn(hk, hw, ha, hg, w0, w2, a0, a2, g2, k_k, k_a):
    zw = w0 + jnp.dot(jnp.tanh(hw), w2, preferred_element_type=F32)
    w_log = -_softplus(-zw) - 0.5
    decay = jnp.exp(-jnp.exp(w_log))
    a = jax.nn.sigmoid(a0 + jnp.dot(ha, a2, preferred_element_type=F32))
    g = jnp.dot(jax.nn.sigmoid(hg), g2, preferred_element_type=F32)
    kk = hk * k_k
    kk = kk * lax.rsqrt(jnp.maximum(jnp.sum(kk * kk, axis=-1, keepdims=True), 1e-24))
    k = hk * (1.0 + (a - 1.0) * k_a)
    return decay, k, -kk, kk * a, g


def _rwkv_pre_grad_fn(hk, hw, ha, hg, w0, w2, a0, a2, g2, k_k, k_a, hr, hv):
    decay, k, a_sc, b_sc, g = _rwkv_pre_fn(hk, hw, ha, hg, w0, w2, a0, a2, g2, k_k, k_a)
    return decay, k, k, a_sc, b_sc, g, hr, hr, hv, hv


def _rwkv_post_fn(y, r, k, v, g, gn_w, gn_b, r_k):
    mu = jnp.mean(y, axis=-1, keepdims=True)
    var = jnp.mean(jnp.square(y - mu), axis=-1, keepdims=True)
    yn = (y - mu) * lax.rsqrt(var + GN_EPS) * gn_w + gn_b
    bonus = jnp.sum(r * k * r_k, axis=-1, keepdims=True) * v
    return ((yn + bonus) * g,)


def _rope_tables(pos, inv_freq2):
    ang = pos * inv_freq2
    return jnp.cos(ang), jnp.sin(ang)


def _rope(t, cos2, sin2, rot):
    return t * cos2 + jnp.dot(t, rot, precision=HIGHEST, preferred_element_type=F32) * sin2


def _mla_pre_fn(c_q, c_kv, k_pe, pos, q_g, kv_g, inv_freq2, rot):
    cos2, sin2 = _rope_tables(pos, inv_freq2)
    return _rms_fn(c_q, q_g)[0], _rms_fn(c_kv, kv_g)[0], _rope(k_pe, cos2, sin2, rot), cos2, sin2


def _mla_pre_grad_fn(c_q, c_kv, k_pe, pos, q_g, kv_g, inv_freq2, rot):
    return _mla_pre_fn(c_q, c_kv, k_pe, pos, q_g, kv_g, inv_freq2, rot)[:3]


def _rope_q_fn(q_pe, cos2, sin2, rot):
    return (_rope(q_pe, cos2, sin2, rot),)


def _loss_fn(x2, g, target):
    y = _rms_fn(x2, g)[0]
    return (0.5 * jnp.mean(jnp.square(y - target), axis=-1, keepdims=True),)


def _adamw_fn(w, g, m, v):
    m = ADAM_B1 * m + (1.0 - ADAM_B1) * g
    v = ADAM_B2 * v + (1.0 - ADAM_B2) * jnp.square(g)
    m_hat = m / (1.0 - ADAM_B1 ** ADAM_STEP)
    v_hat = v / (1.0 - ADAM_B2 ** ADAM_STEP)
    delta = -ADAM_LR * (m_hat / (jnp.sqrt(v_hat) + ADAM_EPS) + ADAM_WD * w)
    return delta, m, v


def _prev_halo_spec(c, tm):
    return pl.BlockSpec((SUBLANES, c), lambda i: (jnp.maximum(i * (tm // SUBLANES) - 1, 0), 0))


def _next_halo_spec(c, tm, s):
    return pl.BlockSpec((SUBLANES, c), lambda i: (jnp.minimum((i + 1) * (tm // SUBLANES), s // SUBLANES - 1), 0))


def _shift_down(p, halo, first_block, n):
    out = pltpu.roll(p, n, 0)
    row = lax.broadcasted_iota(jnp.int32, p.shape, 0)
    for j in range(n):
        top = jnp.where(first_block, 0.0, halo[SUBLANES - n + j:SUBLANES - n + j + 1, :])
        out = jnp.where(row == j, top, out)
    return out


def _shift_up(p, halo, last_block, n):
    rows = p.shape[0]
    out = pltpu.roll(p, rows - n, 0)
    row = lax.broadcasted_iota(jnp.int32, p.shape, 0)
    for j in range(n):
        bot = jnp.where(last_block, 0.0, halo[j:j + 1, :])
        out = jnp.where(row == rows - n + j, bot, out)
    return out


def token_shift_fwd(p, mu, tm):
    s, c = p.shape

    def body(p_ref, halo_ref, mu_ref, o_ref):
        pv = p_ref[...]
        prev = _shift_down(pv, halo_ref[...], pl.program_id(0) == 0, 1)
        o_ref[...] = pv + (prev - pv) * mu_ref[...]

    return pl.pallas_call(
        body, name="token_shift_fwd", grid=(s // tm,),
        in_specs=[pl.BlockSpec((tm, c), lambda i: (i, 0)), _prev_halo_spec(c, tm), pl.BlockSpec((1, c), lambda i: (0, 0))],
        out_specs=pl.BlockSpec((tm, c), lambda i: (i, 0)), out_shape=jax.ShapeDtypeStruct((s, c), F32),
        compiler_params=_params(("parallel",)),
    )(p, p, mu)


def token_shift_bwd(p, mu, ds, tm):
    s, c = p.shape
    nb = s // tm

    def body(p_ref, halo_ref, mu_ref, ds_ref, dsn_ref, dp_ref, dmu_ref):
        i = pl.program_id(0)
        pv, dsv, muv = p_ref[...], ds_ref[...], mu_ref[...]
        prev = _shift_down(pv, halo_ref[...], i == 0, 1)
        nxt = _shift_up(dsv, dsn_ref[...], i == nb - 1, 1)
        dp_ref[...] = dsv * (1.0 - muv) + nxt * muv
        dmu_ref[...] = jnp.sum(dsv * (prev - pv), axis=0, keepdims=True)

    return pl.pallas_call(
        body, name="token_shift_bwd", grid=(nb,),
        in_specs=[pl.BlockSpec((tm, c), lambda i: (i, 0)), _prev_halo_spec(c, tm), pl.BlockSpec((1, c), lambda i: (0, 0)),
                  pl.BlockSpec((tm, c), lambda i: (i, 0)), _next_halo_spec(c, tm, s)],
        out_specs=[pl.BlockSpec((tm, c), lambda i: (i, 0)), pl.BlockSpec((None, 1, c), lambda i: (i, 0, 0))],
        out_shape=[jax.ShapeDtypeStruct((s, c), F32), jax.ShapeDtypeStruct((nb, 1, c), F32)],
        compiler_params=_params(("parallel",)),
    )(p, p, mu, ds, ds)


def _ffn_tiles(s, f):
    return _pick(s, 256, SUBLANES), _pick(f, 1408, LANES)


def _conv_gate(gp, halo, first_block, cw, cb):
    p1 = _shift_down(gp, halo, first_block, 1)
    p2 = _shift_down(gp, halo, first_block, 2)
    return cw[0:1, :] * p2 + cw[1:2, :] * p1 + cw[2:3, :] * gp + cb, p1, p2


def ffn_act_fwd(gate_pre, up, conv_w, conv_b):
    s, f = gate_pre.shape
    tm, tc = _ffn_tiles(s, f)

    def body(gp_ref, halo_ref, up_ref, cw_ref, cb_ref, o_ref):
        gate, _, _ = _conv_gate(gp_ref[...], halo_ref[...], pl.program_id(0) == 0, cw_ref[...], cb_ref[...])
        o_ref[...] = (gate * jax.nn.sigmoid(gate) * up_ref[...]).astype(o_ref.dtype)

    blk = pl.BlockSpec((tm, tc), lambda i, j: (i, j))
    return pl.pallas_call(
        body, name="ffn_act_fwd", grid=(s // tm, f // tc),
        in_specs=[blk, pl.BlockSpec((SUBLANES, tc), lambda i, j: (jnp.maximum(i * (tm // SUBLANES) - 1, 0), j)), blk,
                  pl.BlockSpec((CONV_W, tc), lambda i, j: (0, j)), pl.BlockSpec((1, tc), lambda i, j: (0, j))],
        out_specs=blk, out_shape=jax.ShapeDtypeStruct((s, f), BF16),
        compiler_params=_params(("parallel", "parallel")),
    )(gate_pre, gate_pre, up, conv_w, conv_b)


def ffn_act_bwd1(gate_pre, up, conv_w, conv_b, d_act):
    s, f = gate_pre.shape
    tm, tc = _ffn_tiles(s, f)
    nb = s // tm

    def body(gp_ref, halo_ref, up_ref, cw_ref, cb_ref, da_ref, dg_ref, du_ref, dcw_ref, dcb_ref):
        gp = gp_ref[...]
        gate, p1, p2 = _conv_gate(gp, halo_ref[...], pl.program_id(0) == 0, cw_ref[...], cb_ref[...])
        sig = jax.nn.sigmoid(gate)
        da = da_ref[...].astype(F32)
        du_ref[...] = (da * gate * sig).astype(du_ref.dtype)
        dg = da * up_ref[...] * (sig * (1.0 + gate * (1.0 - sig)))
        dg_ref[...] = dg
        dcb_ref[...] = jnp.sum(dg, axis=0, keepdims=True)
        dcw_ref[0:1, :] = jnp.sum(dg * p2, axis=0, keepdims=True)
        dcw_ref[1:2, :] = jnp.sum(dg * p1, axis=0, keepdims=True)
        dcw_ref[2:3, :] = jnp.sum(dg * gp, axis=0, keepdims=True)

    blk = pl.BlockSpec((tm, tc), lambda i, j: (i, j))
    return pl.pallas_call(
        body, name="ffn_act_bwd1", grid=(nb, f // tc),
        in_specs=[blk, pl.BlockSpec((SUBLANES, tc), lambda i, j: (jnp.maximum(i * (tm // SUBLANES) - 1, 0), j)), blk,
                  pl.BlockSpec((CONV_W, tc), lambda i, j: (0, j)), pl.BlockSpec((1, tc), lambda i, j: (0, j)), blk],
        out_specs=[blk, blk, pl.BlockSpec((None, CONV_W, tc), lambda i, j: (i, 0, j)),
                   pl.BlockSpec((None, 1, tc), lambda i, j: (i, 0, j))],
        out_shape=[jax.ShapeDtypeStruct((s, f), F32), jax.ShapeDtypeStruct((s, f), BF16),
                   jax.ShapeDtypeStruct((nb, CONV_W, f), F32), jax.ShapeDtypeStruct((nb, 1, f), F32)],
        compiler_params=_params(("parallel", "parallel")),
    )(gate_pre, gate_pre, up, conv_w, conv_b, d_act)


def ffn_act_bwd2(d_gate, conv_w):
    s, f = d_gate.shape
    tm, tc = _ffn_tiles(s, f)
    nb = s // tm

    def body(dg_ref, halo_ref, cw_ref, o_ref):
        dg, cw = dg_ref[...], cw_ref[...]
        last = pl.program_id(0) == nb - 1
        n1 = _shift_up(dg, halo_ref[...], last, 1)
        n2 = _shift_up(dg, halo_ref[...], last, 2)
        o_ref[...] = (cw[2:3, :] * dg + cw[1:2, :] * n1 + cw[0:1, :] * n2).astype(o_ref.dtype)

    blk = pl.BlockSpec((tm, tc), lambda i, j: (i, j))
    return pl.pallas_call(
        body, name="ffn_act_bwd2", grid=(nb, f // tc),
        in_specs=[blk, pl.BlockSpec((SUBLANES, tc), lambda i, j: (jnp.minimum((i + 1) * (tm // SUBLANES), s // SUBLANES - 1), j)),
                  pl.BlockSpec((CONV_W, tc), lambda i, j: (0, j))],
        out_specs=blk, out_shape=jax.ShapeDtypeStruct((s, f), BF16),
        compiler_params=_params(("parallel", "parallel")),
    )(d_gate, d_gate, conv_w)


def _mxu(x, y, cx, cy):
    if x.ndim == 3:
        return lax.dot_general(x, y, (((cx + 1,), (cy + 1,)), ((0,), (0,))), preferred_element_type=F32)
    return lax.dot_general(x, y, (((cx,), (cy,)), ((), ())), preferred_element_type=F32)


def _split(x):
    hi = x.astype(BF16)
    return hi, (x - hi.astype(F32)).astype(BF16)


def _make_dot3(cx, cy, passes):
    @jax.custom_vjp
    def f(x, y):
        if passes == 1:
            return _mxu(x.astype(BF16), y.astype(BF16), cx, cy)
        xh, xl = _split(x)
        yh, yl = _split(y)
        return _mxu(xh, yh, cx, cy) + (_mxu(xh, yl, cx, cy) + _mxu(xl, yh, cx, cy))

    def fwd(x, y):
        return f(x, y), (x, y)

    def bwd(res, g):
        x, y = res
        dx = dot3(g, y, 1, 1 - cy, passes) if cx == 1 else dot3(y, g, 1 - cy, 1, passes)
        dy = dot3(x, g, 1 - cx, 0, passes) if cy == 0 else dot3(g, x, 0, 1 - cx, passes)
        return dx, dy

    f.defvjp(fwd, bwd)
    return f


_DOT3 = {}


def dot3(x, y, cx, cy, passes=3):
    if (cx, cy, passes) not in _DOT3:
        _DOT3[(cx, cy, passes)] = _make_dot3(cx, cy, passes)
    return _DOT3[(cx, cy, passes)](x, y)


def _dot(x, y, passes=3):
    return dot3(x, y, 1, 0, passes)


def _dot_nt(x, y, passes=3):
    return dot3(x, y, 1, 1, passes)


def _dot_tn(x, y, passes=3):
    return dot3(x, y, 0, 0, passes)


def _tri_sum(x, lower):
    t = x.shape[-2]
    row = lax.broadcasted_iota(jnp.int32, (t, t), 0)
    col = lax.broadcasted_iota(jnp.int32, (t, t), 1)
    tri = jnp.where((col <= row) if lower else (col >= row), 1.0, 0.0).astype(BF16)
    if x.ndim == 3:
        tri = jnp.broadcast_to(tri[None], (x.shape[0], t, t))
    hi = x.astype(BF16)
    rest = x - hi.astype(F32)
    mid = rest.astype(BF16)
    low = (rest - mid.astype(F32)).astype(BF16)
    return _mxu(tri, hi, 1, 0) + (_mxu(tri, mid, 1, 0) + _mxu(tri, low, 1, 0))


@jax.custom_vjp
def _cumsum_rows(x):
    return _tri_sum(x, True)


_cumsum_rows.defvjp(lambda x: (_tri_sum(x, True), None), lambda _, g: (_tri_sum(g, False),))


def _scan_chunk(s0, r, w, k, v, a, b):
    t = r.shape[1]
    row = lax.broadcasted_iota(jnp.int32, (1, t, t), 1)
    col = lax.broadcasted_iota(jnp.int32, (1, t, t), 2)
    strict, incl = col < row, col <= row
    logw = jnp.log(w)
    cum = _cumsum_rows(logw)
    w_in, w_ex, w_inv = jnp.exp(cum), jnp.exp(cum - logw), jnp.exp(-cum)
    w_all = jnp.exp(jnp.sum(logw, axis=1, keepdims=True))
    at, rt, kt, bt = a * w_ex, r * w_in, k * w_inv, b * w_inv
    ps, po = SCAN_PASSES_SOLVE, SCAN_PASSES_OUT
    a_ab = jnp.where(strict, _dot_nt(at, bt, ps), 0.0)
    a_ak = jnp.where(strict, _dot_nt(at, kt, ps), 0.0)
    a_rk = jnp.where(incl, _dot_nt(rt, kt, po), 0.0)
    a_rb = jnp.where(incl, _dot_nt(rt, bt, po), 0.0)
    u = _dot_nt(at, s0, ps) + _dot(a_ak, v, ps)
    p = a_ab
    steps = int(math.log2(t))
    assert 2 ** steps == t
    for j in range(steps):
        u = u + _dot(p, u, ps)
        if j < steps - 1:
            p = _dot(p, p, ps)
    y = _dot_nt(rt, s0, po) + _dot(a_rk, v, po) + _dot(a_rb, u, po)
    s_new = s0 * w_all + _dot_tn(v, kt * w_all, po) + _dot_tn(u, bt * w_all, po)
    return y, s_new


def scan_fwd(r, w, k, v, a, b, comm=None):
    h, s, n = r.shape
    t = min(SCAN_CHUNK, s)
    nc = s // t

    hb = SCAN_HEADS if h % SCAN_HEADS == 0 else 1

    def body(r_ref, w_ref, k_ref, v_ref, a_ref, b_ref, y_ref, ck_ref, st_ref):
        @pl.when(pl.program_id(1) == 0)
        def _():
            st_ref[...] = jnp.zeros_like(st_ref)

        s0 = st_ref[...]
        ck_ref[...] = s0
        y, s_new = _scan_chunk(s0, r_ref[...], w_ref[...], k_ref[...], v_ref[...], a_ref[...], b_ref[...])
        y_ref[...] = y
        st_ref[...] = s_new

    blk = pl.BlockSpec((hb, t, n), lambda hh, c: (hh, c, 0))
    return _pallas(
        body, name="rwkv_scan_fwd", grid=(h // hb, nc), in_specs=[blk] * 6,
        out_specs=[blk, pl.BlockSpec((hb, None, n, n), lambda hh, c: (hh, c, 0, 0))],
        out_shape=[jax.ShapeDtypeStruct((h, s, n), F32), jax.ShapeDtypeStruct((h, nc, n, n), F32)],
        scratch_shapes=[pltpu.VMEM((hb, n, n), F32)], sem=("parallel", "arbitrary"), comm=comm,
    )(r, w, k, v, a, b)


def scan_bwd(r, w, k, v, a, b, ck, dy, comm=None):
    h, s, n = r.shape
    t = min(SCAN_CHUNK, s)
    nc = s // t

    hb = SCAN_HEADS if h % SCAN_HEADS == 0 else 1

    def body(r_ref, w_ref, k_ref, v_ref, a_ref, b_ref, ck_ref, dy_ref, dr_ref, dw_ref, dk_ref, dv_ref, da_ref, db_ref, ds_ref):
        @pl.when(pl.program_id(1) == 0)
        def _():
            ds_ref[...] = jnp.zeros_like(ds_ref)

        _, vjp_fn = jax.vjp(_scan_chunk, ck_ref[...], r_ref[...], w_ref[...], k_ref[...], v_ref[...], a_ref[...], b_ref[...])
        ds0, dr, dw, dk, dv, da, db = vjp_fn((dy_ref[...], ds_ref[...]))
        ds_ref[...] = ds0
        dr_ref[...], dw_ref[...], dk_ref[...], dv_ref[...], da_ref[...], db_ref[...] = dr, dw, dk, dv, da, db

    blk = pl.BlockSpec((hb, t, n), lambda hh, c: (hh, nc - 1 - c, 0))
    return _pallas(
        body, name="rwkv_scan_bwd", grid=(h // hb, nc),
        in_specs=[blk] * 6 + [pl.BlockSpec((hb, None, n, n), lambda hh, c: (hh, nc - 1 - c, 0, 0)), blk],
        out_specs=[blk] * 6, out_shape=[jax.ShapeDtypeStruct((h, s, n), F32)] * 6,
        scratch_shapes=[pltpu.VMEM((hb, n, n), F32)], sem=("parallel", "arbitrary"), comm=comm,
    )(r, w, k, v, a, b, ck, dy)


ATTN_BLOCK = 256
ATTN_SLAB = 2 * LANES


def _attn_specs(h, s, tq):
    qblk = lambda c, part: pl.BlockSpec((None, tq, c), lambda hh, i: (hh, i, part))
    kblk = lambda part: pl.BlockSpec((None, s, LANES), lambda hh, i: (hh, 0, part))
    row64 = pl.BlockSpec((tq, QK_ROPE), lambda hh, i: (i, 0))
    return [qblk(LANES, 0), qblk(LANES, 1), kblk(0), kblk(1), pl.BlockSpec((s, QK_ROPE), lambda hh, i: (0, 0)),
            row64, row64, pl.BlockSpec((QK_ROPE, QK_ROPE), lambda hh, i: (0, 0))]


def _attn_block_scores(qn_b, qp_b, kn_ref, kp_ref, kb, i, tq):
    scale = (QK_NOPE + QK_ROPE) ** -0.5
    k0 = pl.multiple_of(kb * tq, tq)
    kn_b = kn_ref[pl.ds(k0, tq), :].astype(BF16)
    kp_b = kp_ref[pl.ds(k0, tq), :].astype(BF16)
    sc = lax.dot_general(qn_b, kn_b, (((1,), (1,)), ((), ())), preferred_element_type=F32)
    sc = sc + lax.dot_general(qp_b, kp_b, (((1,), (1,)), ((), ())), preferred_element_type=F32)
    row = i * tq + lax.broadcasted_iota(jnp.int32, sc.shape, 0)
    col = kb * tq + lax.broadcasted_iota(jnp.int32, sc.shape, 1)
    return jnp.where(row >= col, sc * scale, NEG_INF), scale, kn_b, kp_b, k0


def attn_fwd(q_h, kv_h, kp, cos2, sin2, rot, comm=None):
    h, s, _ = q_h.shape
    tq = _pick(s, ATTN_BLOCK, SUBLANES)

    def body(qn_ref, qp_ref, kn_ref, v_ref, kp_ref, cos_ref, sin_ref, rot_ref, o_ref, lse_ref):
        i = pl.program_id(1)
        qn_b = qn_ref[...].astype(BF16)
        qp_b = _rope(qp_ref[:, :QK_ROPE], cos_ref[...], sin_ref[...], rot_ref[...]).astype(BF16)

        def step(kb, carry):
            mx, den, acc = carry
            sc, _, _, _, k0 = _attn_block_scores(qn_b, qp_b, kn_ref, kp_ref, kb, i, tq)
            mx_new = jnp.maximum(mx, jnp.max(sc, axis=-1, keepdims=True))
            corr = jnp.exp(mx - mx_new)
            e = jnp.exp(sc - mx_new)
            v_b = v_ref[pl.ds(k0, tq), :].astype(BF16)
            acc = acc * corr + jnp.dot(e.astype(BF16), v_b, preferred_element_type=F32)
            return mx_new, den * corr + jnp.sum(e, axis=-1, keepdims=True), acc

        init = (jnp.full((tq, 1), NEG_INF, F32), jnp.zeros((tq, 1), F32), jnp.zeros((tq, V_HEAD), F32))
        mx, den, acc = lax.fori_loop(0, i + 1, step, init)
        o_ref[...] = acc / den
        lse_ref[...] = mx + jnp.log(den)

    oblk = lambda c: pl.BlockSpec((None, tq, c), lambda hh, i: (hh, i, 0))
    return _pallas(
        body, name="mla_attn_fwd", grid=(h, s // tq), in_specs=_attn_specs(h, s, tq),
        out_specs=[oblk(V_HEAD), oblk(1)],
        out_shape=[jax.ShapeDtypeStruct((h, s, V_HEAD), F32), jax.ShapeDtypeStruct((h, s, 1), F32)],
        sem=("parallel", "parallel"), comm=comm,
    )(q_h, q_h, kv_h, kv_h, kp, cos2, sin2, rot)


def attn_bwd(q_h, kv_h, kp, cos2, sin2, rot, o, lse, do, comm=None):
    h, s, _ = q_h.shape
    tq = _pick(s, ATTN_BLOCK, SUBLANES)
    nq = s // tq

    def body(qn_ref, qp_ref, kn_ref, v_ref, kp_ref, cos_ref, sin_ref, rot_ref, o_ref, lse_ref, do_ref,
             dq_ref, dkv_ref, dkp_ref, dkv_acc, dkp_acc):
        i = pl.program_id(1)

        @pl.when(i == 0)
        def _():
            dkv_acc[...] = jnp.zeros_like(dkv_acc)
            dkp_acc[...] = jnp.zeros_like(dkp_acc)

        cosv, sinv, rotv = cos_ref[...], sin_ref[...], rot_ref[...]
        qn_b = qn_ref[...].astype(BF16)
        qp_b = _rope(qp_ref[:, :QK_ROPE], cosv, sinv, rotv).astype(BF16)
        dov = do_ref[...]
        do_b = dov.astype(BF16)
        delta = jnp.sum(dov * o_ref[...], axis=-1, keepdims=True)
        lsev = lse_ref[...]

        def step(kb, carry):
            dqn, dqp = carry
            sc, scale, kn_b, kp_b, k0 = _attn_block_scores(qn_b, qp_b, kn_ref, kp_ref, kb, i, tq)
            p = jnp.exp(sc - lsev)
            v_b = v_ref[pl.ds(k0, tq), :].astype(BF16)
            dp = lax.dot_general(do_b, v_b, (((1,), (1,)), ((), ())), preferred_element_type=F32)
            ds = (p * (dp - delta) * scale).astype(BF16)
            dkv_acc[pl.ds(k0, tq), :LANES] += lax.dot_general(ds, qn_b, (((0,), (0,)), ((), ())), preferred_element_type=F32)
            dkv_acc[pl.ds(k0, tq), LANES:] += lax.dot_general(p.astype(BF16), do_b, (((0,), (0,)), ((), ())),
                                                            preferred_element_type=F32)
            dkp_acc[pl.ds(k0, tq), :] += lax.dot_general(ds, qp_b, (((0,), (0,)), ((), ())), preferred_element_type=F32)
            return (dqn + jnp.dot(ds, kn_b, preferred_element_type=F32), dqp + jnp.dot(ds, kp_b, preferred_element_type=F32))

        dqn, dqp = lax.fori_loop(0, i + 1, step, (jnp.zeros((tq, QK_NOPE), F32), jnp.zeros((tq, QK_ROPE), F32)))
        dqp_raw = dqp * cosv + lax.dot_general(dqp * sinv, rotv, (((1,), (1,)), ((), ())), precision=HIGHEST,
                                               preferred_element_type=F32)
        dq_ref[:, :QK_NOPE] = dqn.astype(dq_ref.dtype)
        dq_ref[:, QK_NOPE:QK_NOPE + QK_ROPE] = dqp_raw.astype(dq_ref.dtype)
        dq_ref[:, QK_NOPE + QK_ROPE:] = jnp.zeros((tq, ATTN_SLAB - QK_NOPE - QK_ROPE), dq_ref.dtype)

        @pl.when(i == nq - 1)
        def _():
            dkv_ref[...] = dkv_acc[...].astype(dkv_ref.dtype)
            dkp_ref[...] = dkp_acc[...]

    rblk = lambda c: pl.BlockSpec((None, tq, c), lambda hh, i: (hh, i, 0))
    sblk = lambda c: pl.BlockSpec((None, s, c), lambda hh, i: (hh, 0, 0))
    return _pallas(
        body, name="mla_attn_bwd", grid=(h, nq),
        in_specs=_attn_specs(h, s, tq) + [rblk(V_HEAD), rblk(1), rblk(V_HEAD)],
        out_specs=[rblk(ATTN_SLAB), sblk(ATTN_SLAB), sblk(QK_ROPE)],
        out_shape=[jax.ShapeDtypeStruct((h, s, ATTN_SLAB), BF16), jax.ShapeDtypeStruct((h, s, ATTN_SLAB), BF16),
                   jax.ShapeDtypeStruct((h, s, QK_ROPE), F32)],
        scratch_shapes=[pltpu.VMEM((s, ATTN_SLAB), F32), pltpu.VMEM((s, QK_ROPE), F32)],
        sem=("parallel", "arbitrary"), comm=comm,
    )(q_h, q_h, kv_h, kv_h, kp, cos2, sin2, rot, o, lse, do)


def _my_pos():
    return lax.axis_index("x"), lax.axis_index("y"), lax.axis_index("c")


def _dev_index(px, py, pc):
    return 4 * px + 2 * py + pc


def all_gather(name, shard):
    r, c = shard.shape

    def body(x_ref, out_ref, send_sems, recv_sems, local_sem):
        x, y, cc = _my_pos()
        me, sibling = (x, y, cc), (x, y, 1 - cc)
        chips = [(1 - x, y), (x, 1 - y), (1 - x, 1 - y)]

        def rows(px, py, pc):
            return out_ref.at[_dev_index(px, py, pc)]

        def copy(kk, block, to, src=None):
            return pltpu.make_async_remote_copy(
                src_ref=rows(*block) if src is None else src, dst_ref=rows(*block),
                send_sem=send_sems.at[kk], recv_sem=recv_sems.at[kk], device_id=to, device_id_type=MESH)

        mine = pltpu.make_async_copy(x_ref, rows(*me), local_sem)
        mine.start()
        first = [copy(0, me, sibling, src=x_ref)]
        first += [copy(1 + j, me, (*chip, cc), src=x_ref) for j, chip in enumerate(chips)]
        for cp in first:
            cp.start()
        passed = [copy(4 + j, (*chip, cc), sibling) for j, chip in enumerate(chips)]
        for j, chip in enumerate(chips):
            copy(1 + j, (*chip, cc), me).wait_recv()
            passed[j].start()
        copy(0, sibling, me).wait_recv()
        for j, chip in enumerate(chips):
            copy(4 + j, (*chip, 1 - cc), me).wait_recv()
        for cp in first + passed:
            cp.wait_send()
        mine.wait()

    return pl.pallas_call(
        body, name=name, out_shape=jax.ShapeDtypeStruct((N_DEV, r, c), shard.dtype),
        in_specs=[pl.BlockSpec(memory_space=pl.ANY)], out_specs=pl.BlockSpec(memory_space=pl.ANY),
        scratch_shapes=[pltpu.SemaphoreType.DMA((7,)), pltpu.SemaphoreType.DMA((7,)), pltpu.SemaphoreType.DMA],
    )(shard)


def _flip(kind):
    x, y, c = _my_pos()
    return {'c': (x, y, 1 - c), 'x': (1 - x, y, c), 'y': (x, 1 - y, c), 'xy': (1 - x, 1 - y, c)}[kind]


def exchange_sibling(name, g):
    _, r, c = g.shape

    def body(g_ref, out_ref, send_sems, recv_sems):
        x, y, cc = _my_pos()
        copies = []
        for px in range(2):
            for py in range(2):
                slot = 2 * px + py
                copies.append(pltpu.make_async_remote_copy(
                    src_ref=g_ref.at[_dev_index(px, py, 1 - cc)], dst_ref=out_ref.at[slot],
                    send_sem=send_sems.at[slot], recv_sem=recv_sems.at[slot], device_id=(x, y, 1 - cc), device_id_type=MESH))
        for cp in copies:
            cp.start()
        for cp in copies:
            cp.wait()

    return pl.pallas_call(
        body, name=name, out_shape=jax.ShapeDtypeStruct((4, r, c), g.dtype),
        in_specs=[pl.BlockSpec(memory_space=pl.ANY)], out_specs=pl.BlockSpec(memory_space=pl.ANY),
        scratch_shapes=[pltpu.SemaphoreType.DMA((4,)), pltpu.SemaphoreType.DMA((4,))],
    )(g)


def exchange_chips(name, hsum):
    _, r, c = hsum.shape

    def body(h_ref, out_ref, send_sems, recv_sems):
        x, y, cc = _my_pos()
        copies = []
        for j, (px, py) in enumerate([(1 - x, y), (x, 1 - y), (1 - x, 1 - y)]):
            copies.append(pltpu.make_async_remote_copy(
                src_ref=h_ref.at[2 * px + py], dst_ref=out_ref.at[j],
                send_sem=send_sems.at[j], recv_sem=recv_sems.at[j], device_id=(px, py, cc), device_id_type=MESH))
        for cp in copies:
            cp.start()
        for cp in copies:
            cp.wait()

    return pl.pallas_call(
        body, name=name, out_shape=jax.ShapeDtypeStruct((3, r, c), hsum.dtype),
        in_specs=[pl.BlockSpec(memory_space=pl.ANY)], out_specs=pl.BlockSpec(memory_space=pl.ANY),
        scratch_shapes=[pltpu.SemaphoreType.DMA((3,)), pltpu.SemaphoreType.DMA((3,))],
    )(hsum)


def add_slots(name, *terms):
    n, r, c = terms[0].shape
    tr = _pick(r, 512, SUBLANES)

    def body(*refs):
        acc = refs[0][...]
        for t in refs[1:-1]:
            acc = acc + t[...]
        refs[-1][...] = acc

    blk = pl.BlockSpec((None, tr, c), lambda s_, i: (s_, i, 0))
    return pl.pallas_call(
        body, name=name, grid=(n, r // tr), in_specs=[blk] * len(terms), out_specs=blk,
        out_shape=jax.ShapeDtypeStruct((n, r, c), F32), compiler_params=_params(("parallel", "parallel")),
    )(*terms)


def _rs_add_sibling(name, g, from_sibling, cc):
    _, r, c = g.shape
    tr = _pick(r, 512, SUBLANES * 2)

    def body(cc_ref, g_ref, s_ref, o_ref, ob_ref):
        tot = g_ref[...] + s_ref[...]
        o_ref[...] = tot
        ob_ref[...] = tot.astype(BF16)

    blk = pl.BlockSpec((None, tr, c), lambda s_, i, cc_ref: (s_, i, 0))
    return pl.pallas_call(
        body, name=name,
        grid_spec=pltpu.PrefetchScalarGridSpec(
            num_scalar_prefetch=1, grid=(4, r // tr),
            in_specs=[pl.BlockSpec((None, None, tr, c), lambda s_, i, cc_ref: (s_, cc_ref[0], i, 0)), blk], out_specs=[blk, blk]),
        out_shape=[jax.ShapeDtypeStruct((4, r, c), F32), jax.ShapeDtypeStruct((4, r, c), BF16)],
        compiler_params=_params(("parallel", "parallel")),
    )(cc.reshape(1).astype(jnp.int32), g.reshape(4, 2, r, c), from_sibling)


def _rs_add_chips(name, chip_sum, from_chips, slot):
    _, r, c = chip_sum.shape
    tr = _pick(r, 512, SUBLANES * 2)

    def body(slot_ref, h_ref, f0_ref, f1_ref, f2_ref, o_ref):
        o_ref[...] = ((h_ref[...] + f0_ref[...].astype(F32)) + f1_ref[...].astype(F32)) + f2_ref[...].astype(F32)

    def from_blk(j):
        return pl.BlockSpec((None, tr, c), lambda i, slot_ref: (j, i, 0))

    return pl.pallas_call(
        body, name=name,
        grid_spec=pltpu.PrefetchScalarGridSpec(
            num_scalar_prefetch=1, grid=(r // tr,),
            in_specs=[pl.BlockSpec((None, tr, c), lambda i, slot_ref: (slot_ref[0], i, 0)), from_blk(0), from_blk(1), from_blk(2)],
            out_specs=pl.BlockSpec((tr, c), lambda i, slot_ref: (i, 0))),
        out_shape=jax.ShapeDtypeStruct((r, c), F32), compiler_params=_params(("parallel",)),
    )(slot.reshape(1).astype(jnp.int32), chip_sum, from_chips, from_chips, from_chips)


def rs_chip_sum(tag, g):
    _, _, cc = _my_pos()
    from_sibling = exchange_sibling("rs_sibling_" + tag, g)
    return _rs_add_sibling("rs_add_sibling_" + tag, g, from_sibling, cc)


def rs_finish(tag, chip_sum, from_chips):
    x, y, _ = _my_pos()
    return _rs_add_chips("rs_add_chips_" + tag, chip_sum, from_chips, 2 * x + y)


def reduce_scatter(tag, g):
    chip_sum, chip_sum_b = rs_chip_sum(tag, g)
    return rs_finish(tag, chip_sum, exchange_chips("rs_chips_" + tag, chip_sum_b))


class GatherIci:
    def __init__(self, shards):
        self.inputs = list(shards)
        self.out_shapes = [jax.ShapeDtypeStruct((N_DEV,) + s.shape, s.dtype) for s in shards]
        self.n_remote, self.n_local = 3 * len(shards), len(shards)

    def make(self, cins, couts, send, recv, local):
        x, y, cc = _my_pos()
        me = _dev_index(x, y, cc)
        copies = []
        for w, (src, out) in enumerate(zip(cins, couts, strict=True)):
            copies.append(pltpu.make_async_copy(src, out.at[me], local.at[w]))
            for j, (px, py) in enumerate([(1 - x, y), (x, 1 - y), (1 - x, 1 - y)]):
                copies.append(pltpu.make_async_remote_copy(
                    src_ref=src, dst_ref=out.at[me], send_sem=send.at[3 * w + j], recv_sem=recv.at[3 * w + j],
                    device_id=(px, py, cc), device_id_type=MESH))
        return copies


class RsChips:
    def __init__(self, chip_sums):
        self.inputs = list(chip_sums)
        self.out_shapes = [jax.ShapeDtypeStruct((3,) + h.shape[1:], h.dtype) for h in chip_sums]
        self.n_remote, self.n_local = 3 * len(chip_sums), 0

    def make(self, cins, couts, send, recv, local):
        x, y, cc = _my_pos()
        copies = []
        for w, (h_ref, out) in enumerate(zip(cins, couts, strict=True)):
            for j, (px, py) in enumerate([(1 - x, y), (x, 1 - y), (1 - x, 1 - y)]):
                copies.append(pltpu.make_async_remote_copy(
                    src_ref=h_ref.at[2 * px + py], dst_ref=out.at[j], send_sem=send.at[3 * w + j], recv_sem=recv.at[3 * w + j],
                    device_id=(px, py, cc), device_id_type=MESH))
        return copies


class SiblingSwap:
    def __init__(self, gs):
        self.inputs = list(gs)
        self.out_shapes = [jax.ShapeDtypeStruct((4,) + g.shape[1:], g.dtype) for g in gs]
        self.n_remote, self.n_local = 4 * len(gs), 0

    def make(self, cins, couts, send, recv, local):
        x, y, cc = _my_pos()
        copies = []
        for w, (g_ref, out) in enumerate(zip(cins, couts, strict=True)):
            for px in range(2):
                for py in range(2):
                    q = 4 * w + 2 * px + py
                    copies.append(pltpu.make_async_remote_copy(
                        src_ref=g_ref.at[_dev_index(px, py, 1 - cc)], dst_ref=out.at[2 * px + py],
                        send_sem=send.at[q], recv_sem=recv.at[q], device_id=(x, y, 1 - cc), device_id_type=MESH))
        return copies


class _SemSlice:
    def __init__(self, base, start):
        self.base, self.start = base, start

    @property
    def at(self):
        return self

    def __getitem__(self, k):
        return self.base.at[self.start + k]


class CommGroup:
    def __init__(self, plans):
        self.plans = [p for p in plans if p.inputs]
        self.inputs = [a for p in self.plans for a in p.inputs]
        self.out_shapes = [s_ for p in self.plans for s_ in p.out_shapes]
        self.n_remote = sum(p.n_remote for p in self.plans)
        self.n_local = sum(p.n_local for p in self.plans)

    def make(self, cins, couts, send, recv, local):
        copies, i0, o0, r0, l0 = [], 0, 0, 0, 0
        for p in self.plans:
            ni, no = len(p.inputs), len(p.out_shapes)
            copies += p.make(cins[i0:i0 + ni], couts[o0:o0 + no], _SemSlice(send, r0), _SemSlice(recv, r0), _SemSlice(local, l0))
            i0, o0, r0, l0 = i0 + ni, o0 + no, r0 + p.n_remote, l0 + p.n_local
        return copies


def gather_d2d(name, arrays):
    n = len(arrays)

    def body(*refs):
        outs, send, recv = refs[n:2 * n], refs[2 * n], refs[2 * n + 1]
        x, y, cc = _my_pos()
        copies = []
        for w, out in enumerate(outs):
            for px in range(2):
                for py in range(2):
                    q = 4 * w + 2 * px + py
                    slab = out.at[_dev_index(px, py, cc)]
                    copies.append(pltpu.make_async_remote_copy(
                        src_ref=slab, dst_ref=slab, send_sem=send.at[q], recv_sem=recv.at[q],
                        device_id=(x, y, 1 - cc), device_id_type=MESH))
        for cp in copies:
            cp.start()
        for cp in copies:
            cp.wait()

    any_spec = pl.BlockSpec(memory_space=pl.ANY)
    return pl.pallas_call(
        body, name=name, out_shape=[jax.ShapeDtypeStruct(a.shape, a.dtype) for a in arrays],
        in_specs=[any_spec] * n, out_specs=[any_spec] * n, input_output_aliases={i: i for i in range(n)},
        scratch_shapes=[pltpu.SemaphoreType.DMA((4 * n,)), pltpu.SemaphoreType.DMA((4 * n,))],
    )(*arrays)


def _pallas(body, *, name, grid, in_specs, out_specs, out_shape, scratch_shapes=(), sem, comm=None):
    in_specs, out_specs, out_shape, scratch_shapes = list(in_specs), list(out_specs), list(out_shape), list(scratch_shapes)
    if comm is None:
        return pl.pallas_call(body, name=name, grid=grid, in_specs=in_specs, out_specs=out_specs, out_shape=out_shape,
                              scratch_shapes=scratch_shapes, compiler_params=_params(sem))
    n_in, n_out, n_scr = len(in_specs), len(out_specs), len(scratch_shapes)
    nci, nco = len(comm.inputs), len(comm.out_shapes)

    def body2(*refs):
        ins, cins = refs[:n_in], refs[n_in:n_in + nci]
        o0 = n_in + nci
        outs, couts = refs[o0:o0 + n_out], refs[o0 + n_out:o0 + n_out + nco]
        s0 = o0 + n_out + nco
        scr = refs[s0:s0 + n_scr]
        send, recv, local = refs[s0 + n_scr:]
        pids = [pl.program_id(k) for k in range(len(grid))]
        first = functools.reduce(jnp.logical_and, [p == 0 for p in pids])
        last = functools.reduce(jnp.logical_and, [p == g - 1 for p, g in zip(pids, grid)])

        @pl.when(first)
        def _():
            for cp in comm.make(cins, couts, send, recv, local):
                cp.start()

        body(*ins, *outs, *scr)

        @pl.when(last)
        def _():
            for cp in comm.make(cins, couts, send, recv, local):
                cp.wait()

    any_spec = pl.BlockSpec(memory_space=pl.ANY)
    call = pl.pallas_call(
        body2, name=name, grid=grid, in_specs=in_specs + [any_spec] * nci, out_specs=out_specs + [any_spec] * nco,
        out_shape=out_shape + list(comm.out_shapes),
        scratch_shapes=scratch_shapes + [pltpu.SemaphoreType.DMA((comm.n_remote,)), pltpu.SemaphoreType.DMA((comm.n_remote,)),
                                         pltpu.SemaphoreType.DMA((max(comm.n_local, 1),))],
        compiler_params=_params(tuple("arbitrary" for _ in grid)))
    return lambda *args: call(*args, *comm.inputs)


PACK_W = 1024


class Pack:
    def __init__(self, entries, row_unit):
        self.entries = entries
        self.sizes = [int(np.prod(sh)) for _, sh in entries]
        self.offsets = np.concatenate([[0], np.cumsum(self.sizes)]).tolist()
        self.total = _round_up(self.offsets[-1], PACK_W * row_unit)
        self.rows = self.total // PACK_W

    def pack(self, arrays, dtype, lead=()):
        flat = [arrays[n].astype(dtype).reshape(lead + (-1,)) for n, _ in self.entries]
        pad = self.total - self.offsets[-1]
        if pad:
            flat.append(jnp.zeros(lead + (pad,), dtype))
        return jnp.concatenate(flat, axis=-1).reshape(lead + (self.rows, PACK_W))

    def unpack(self, buf, lead=()):
        flat = buf.reshape(lead + (self.total,))
        out = {}
        for (n, sh), off, sz in zip(self.entries, self.offsets, self.sizes):
            out[n] = lax.slice_in_dim(flat, off, off + sz, axis=len(lead)).reshape(lead + tuple(sh))
        return out


def _gathered_to_full(g, how):
    _, a, b = g.shape
    if how == 'row':
        return g.reshape(N_DEV * a, b)
    return jnp.transpose(g, (1, 0, 2)).reshape(a, N_DEV * b)


def _full_to_shards(w, how):
    a, b = w.shape
    if how == 'row':
        return w.reshape(N_DEV, a // N_DEV, b)
    return jnp.transpose(w.reshape(a, N_DEV, b // N_DEV), (1, 0, 2))


def _to_heads(t, width):
    s, c = t.shape
    return jnp.transpose(t.reshape(s, c // width, width), (1, 0, 2))


def _from_heads(t):
    h, s, w = t.shape
    return jnp.transpose(t, (1, 0, 2)).reshape(s, h * w)


def _rot_matrix():
    half = QK_ROPE // 2
    rot = np.zeros((QK_ROPE, QK_ROPE), np.float32)
    for i in range(half):
        rot[i + half, i] = -1.0
        rot[i, i + half] = 1.0
    return jnp.asarray(rot)


def _inv_freq2():
    half = QK_ROPE // 2
    inv = ROPE_THETA ** (-np.arange(half, dtype=np.float32) / half)
    return jnp.asarray(np.concatenate([inv, inv])[None, :].astype(np.float32))


def kernel(x, positions, attn_norm_g, w_in, rwkv_mu, rwkv_w0, rwkv_w2, rwkv_a0, rwkv_a2, rwkv_g2, rwkv_k_k, rwkv_k_a, rwkv_r_k, rwkv_gn_w, rwkv_gn_b, mla_q_norm_g, mla_w_uq, mla_kv_norm_g, mla_w_ukv, w_out, ffn_norm_g, ffn_w_gate, ffn_w_up, ffn_conv_w, ffn_conv_b, ffn_w_down, final_norm_g, loss_target, m_attn_norm_g, m_w_in, m_rwkv_mu, m_rwkv_w0, m_rwkv_w2, m_rwkv_a0, m_rwkv_a2, m_rwkv_g2, m_rwkv_k_k, m_rwkv_k_a, m_rwkv_r_k, m_rwkv_gn_w, m_rwkv_gn_b, m_mla_q_norm_g, m_mla_w_uq, m_mla_kv_norm_g, m_mla_w_ukv, m_w_out, m_ffn_norm_g, m_ffn_w_gate, m_ffn_w_up, m_ffn_conv_w, m_ffn_conv_b, m_ffn_w_down, m_final_norm_g, v_attn_norm_g, v_w_in, v_rwkv_mu, v_rwkv_w0, v_rwkv_w2, v_rwkv_a0, v_rwkv_a2, v_rwkv_g2, v_rwkv_k_k, v_rwkv_k_a, v_rwkv_r_k, v_rwkv_gn_w, v_rwkv_gn_b, v_mla_q_norm_g, v_mla_w_uq, v_mla_kv_norm_g, v_mla_w_ukv, v_w_out, v_ffn_norm_g, v_ffn_w_gate, v_ffn_w_up, v_ffn_conv_w, v_ffn_conv_b, v_ffn_w_down, v_final_norm_g):
    given = dict(locals())
    wts = {n: given[n] for n in WEIGHTS}
    mom_m = {n: given["m_" + n] for n in WEIGHTS}
    mom_v = {n: given["v_" + n] for n in WEIGHTS}
    out_shapes = {n: wts[n].shape for n in WEIGHTS}

    def local2d(n, a):
        if n == 'rwkv_r_k' or a.ndim <= 2:
            return a.reshape(1, -1)
        return a.reshape(a.shape[1:])

    w2d = {n: local2d(n, wts[n]) for n in WEIGHTS}
    m2d = {n: local2d(n, mom_m[n]) for n in WEIGHTS}
    v2d = {n: local2d(n, mom_v[n]) for n in WEIGHTS}

    xs = x.reshape(x.shape[1:])
    tgt = loss_target.reshape(loss_target.shape[1:])
    s, d = xs.shape
    c_rwkv = w2d['rwkv_w0'].shape[1]
    n_rh = c_rwkv // RWKV_HEAD
    decay_lora, aaa_lora, gate_lora = w2d['rwkv_w2'].shape[0], w2d['rwkv_a2'].shape[0], w2d['rwkv_g2'].shape[0]
    q_lora, kv_lora = w2d['mla_q_norm_g'].shape[1], w2d['mla_kv_norm_g'].shape[1]
    shift_dim = w2d['rwkv_mu'].shape[1]
    d_in = w2d['w_in'].shape[1] * N_DEV
    d_in_pad = _round_up(d_in, LANES)
    n_mh = w2d['mla_w_uq'].shape[1] * N_DEV // (QK_NOPE + QK_ROPE)
    d_ff = w2d['ffn_conv_b'].shape[1]
    tm = _pick(s, 256, SUBLANES)
    tm_wide = _pick(s, 128, SUBLANES)
    tm_heads = _pick(s, 512, SUBLANES)

    nb = {n: w2d[n].shape[1] for n in BIG if BIG[n] == 'col'}
    nbp = {n: _round_up(v_, LANES) for n, v_ in nb.items()}
    shards = {}
    for n in BIG:
        w = w2d[n].astype(BF16)
        if BIG[n] == 'col':
            w = jnp.pad(w, ((0, 0), (0, nbp[n] - nb[n])))
        elif n == 'ffn_w_down':
            w = jnp.pad(w, ((0, nbp['ffn_w_gate'] - w.shape[0]), (0, 0)))
        shards[n] = w

    def as_used(n, g):
        return g if BIG[n] == 'col' else g.reshape(N_DEV * g.shape[1], g.shape[2])

    gathered = {'w_in': as_used('w_in', all_gather("gather_w_in", shards['w_in']))}
    later = [n for n in BIG if n != 'w_in']
    f_pad = N_DEV * nbp['ffn_w_gate']
    small_pack = Pack([(n, w2d[n].shape) for n in SMALL_SHARDED], 8)
    small_all = all_gather("gather_small", small_pack.pack(w2d, F32))
    full = {}
    for n, g in small_pack.unpack(small_all, lead=(N_DEV,)).items():
        full[n] = _gathered_to_full(g, SMALL_SHARDED[n])
    conv_w_pad = pad_cols(full['ffn_conv_w'], nb['ffn_w_gate'], nbp['ffn_w_gate'])
    conv_b_pad = pad_cols(w2d['ffn_conv_b'], nb['ffn_w_gate'], nbp['ffn_w_gate'])

    (h1,) = rowwise("rms_attn", _rms_fn, [xs, w2d['attn_norm_g']], ['row', 'const'], [('row', d, BF16)], heads=1, s=s, tm=tm)
    def gather_behind(names, run):
        *res, = run(GatherIci([shards[n] for n in names]))
        landed = res[len(res) - len(names):]
        for n, g in zip(names, gather_d2d("gather_d2d_" + names[0], landed), strict=True):
            gathered[n] = as_used(n, g)
        return res[:len(res) - len(names)]

    (proj_pad,) = gather_behind(['mla_w_uq', 'mla_w_ukv', 'w_out'], lambda c: mm_sh("proj_in", h1, gathered['w_in'], comm=c))
    proj = unpad_cols(proj_pad, nb['w_in'], nbp['w_in'])
    p_rwkv = proj[:, :shift_dim]
    c_q = proj[:, shift_dim:shift_dim + q_lora]
    c_kv = proj[:, shift_dim + q_lora:shift_dim + q_lora + kv_lora]
    k_pe = proj[:, shift_dim + q_lora + kv_lora:d_in]
    shifted = token_shift_fwd(p_rwkv, w2d['rwkv_mu'], tm_wide)
    o1, o2, o3 = c_rwkv, 2 * c_rwkv, 3 * c_rwkv
    hr = _to_heads(shifted[:, :o1], RWKV_HEAD)
    hk = _to_heads(shifted[:, o1:o2], RWKV_HEAD)
    hv = _to_heads(shifted[:, o2:o3], RWKV_HEAD)
    hw = shifted[:, o3:o3 + decay_lora]
    ha = shifted[:, o3 + decay_lora:o3 + decay_lora + aaa_lora]
    hg = shifted[:, o3 + decay_lora + aaa_lora:]

    def per_head(vec):
        return vec.reshape(n_rh, 1, RWKV_HEAD)

    def lora_heads(w):
        return jnp.transpose(w.reshape(w.shape[0], n_rh, RWKV_HEAD), (1, 0, 2))

    pre_args = [hk, hw, ha, hg, per_head(w2d['rwkv_w0']), lora_heads(full['rwkv_w2']), per_head(w2d['rwkv_a0']),
                lora_heads(full['rwkv_a2']), lora_heads(full['rwkv_g2']), per_head(w2d['rwkv_k_k']), per_head(w2d['rwkv_k_a'])]
    pre_kinds = ['hrow', 'row', 'row', 'row', 'hconst', 'hconst', 'hconst', 'hconst', 'hconst', 'hconst', 'hconst']
    decay, kx, a_sc, b_sc, gate_r = gather_behind(['ffn_w_gate'], lambda c: rowwise(
        "rwkv_pre", _rwkv_pre_fn, pre_args, pre_kinds, [('hrow', RWKV_HEAD, F32)] * 5, heads=n_rh, s=s, tm=tm_heads, comm=c))
    y_scan, ckpt = gather_behind(['ffn_w_up'], lambda c: scan_fwd(hr, decay, kx, hv, a_sc, b_sc, comm=c))
    post_args = [y_scan, hr, kx, hv, gate_r, per_head(w2d['rwkv_gn_w']), per_head(w2d['rwkv_gn_b']), per_head(w2d['rwkv_r_k'])]
    post_kinds = ['hrow'] * 5 + ['hconst'] * 3
    (y_rwkv_h,) = rowwise("rwkv_post", _rwkv_post_fn, post_args, post_kinds, [('hrow', RWKV_HEAD, F32)], heads=n_rh, s=s, tm=tm_heads)

    pos = positions.reshape(s, 1).astype(F32)
    rot, inv2 = _rot_matrix(), _inv_freq2()
    mla_args = [c_q, c_kv, k_pe, pos, w2d['mla_q_norm_g'], w2d['mla_kv_norm_g'], inv2, rot]
    mla_kinds = ['row', 'row', 'row', 'row', 'const', 'const', 'const', 'const']
    qn, kvn, kp_rot, cos2, sin2 = rowwise(
        "mla_pre", _mla_pre_fn, mla_args, mla_kinds,
        [('row', q_lora, BF16), ('row', kv_lora, BF16), ('row', QK_ROPE, F32), ('row', QK_ROPE, F32), ('row', QK_ROPE, F32)],
        heads=1, s=s, tm=tm)
    assert n_mh == N_DEV and nb['mla_w_uq'] == QK_NOPE + QK_ROPE and nb['mla_w_ukv'] == QK_NOPE + V_HEAD
    assert nbp['mla_w_uq'] == ATTN_SLAB and nbp['mla_w_ukv'] == ATTN_SLAB
    q_h = mm_sh("proj_q", qn, gathered['mla_w_uq'], slabs=True)
    kv_h = mm_sh("proj_kv", kvn, gathered['mla_w_ukv'], slabs=True)
    o_att, lse = gather_behind(['ffn_w_down'], lambda c: attn_fwd(q_h, kv_h, kp_rot, cos2, sin2, rot, comm=c))
    ycat = jnp.concatenate([_from_heads(y_rwkv_h), _from_heads(o_att)], axis=-1).astype(BF16)
    x1 = mm("proj_out", ycat, gathered['w_out'], add=xs)
    (h2,) = rowwise("rms_ffn", _rms_fn, [x1, w2d['ffn_norm_g']], ['row', 'const'], [('row', d, BF16)], heads=1, s=s, tm=tm)
    gate_pre = mm_sh("ffn_gate", h2, gathered['ffn_w_gate'])
    up = mm_sh("ffn_up", h2, gathered['ffn_w_up'])
    act = ffn_act_fwd(gate_pre, up, conv_w_pad, conv_b_pad)
    x2 = mm("ffn_down", act, gathered['ffn_w_down'], add=x1)

    ones = jnp.ones((s, 1), F32)
    fin_g = w2d['final_norm_g']
    d_x2, dg_final_p, loss_rows = rowwise_vjp("loss_bwd", _loss_fn, [x2, fin_g, tgt], ['row', 'const', 'row'], [ones], ['row'],
                                              [0, 1], heads=1, s=s, tm=tm, primal=True)
    d_x2_b = d_x2.astype(BF16)
    d_act = mm_nt("d_act", d_x2_b, gathered['ffn_w_down'], out_dtype=BF16)
    gsh, chip_sums, from_chips = {}, {}, {}

    def scatter_behind(run, ici=(), swap=()):
        *res, = run(CommGroup([RsChips([chip_sums[n][1] for n in ici]), SiblingSwap([gsh[n] for n in swap])]))
        n_own = len(res) - len(ici) - len(swap)
        from_chips.update(zip(ici, res[n_own:n_own + len(ici)], strict=True))
        _, _, cc = _my_pos()
        for n, from_sibling in zip(swap, res[n_own + len(ici):], strict=True):
            chip_sums[n] = _rs_add_sibling("rs_add_sibling_" + n, gsh[n], from_sibling, cc)
        return res[:n_own]

    gsh['ffn_w_down'] = mm_tn("dw_down", act, d_x2_b).reshape(N_DEV, nbp['ffn_w_gate'], d)
    d_gate, d_up, dcw_p, dcb_p = ffn_act_bwd1(gate_pre, up, conv_w_pad, conv_b_pad, d_act)
    d_gp = ffn_act_bwd2(d_gate, conv_w_pad)
    (d_h2_g,) = scatter_behind(lambda c: mm_sh_nt("d_h2_gate", d_gp, gathered['ffn_w_gate'], comm=c), swap=['ffn_w_down'])
    d_h2 = mm_sh_nt("d_h2_up", d_up, gathered['ffn_w_up'], add=d_h2_g)
    gsh['ffn_w_gate'] = mm_sh_out("dw_gate", h2, d_gp)
    (gsh['ffn_w_up'],) = scatter_behind(lambda c: mm_sh_out("dw_up", h2, d_up, comm=c), swap=['ffn_w_gate'])
    d_x1n, dg_ffn_p = rowwise_vjp("rms_ffn_bwd", _rms_fn, [x1, w2d['ffn_norm_g']], ['row', 'const'], [d_h2], ['row'], [0, 1],
                                  heads=1, s=s, tm=tm)
    d_x1 = add_slots("d_x1_add", d_x1n[None], d_x2[None])[0]
    d_x1_b = d_x1.astype(BF16)
    d_ycat = mm_nt("d_ycat", d_x1_b, gathered['w_out'])
    gsh['w_out'] = mm_tn("dw_out", ycat, d_x1_b).reshape((N_DEV,) + w2d['w_out'].shape)
    d_yr_h = _to_heads(d_ycat[:, :c_rwkv], RWKV_HEAD)
    d_o_h = _to_heads(d_ycat[:, c_rwkv:], V_HEAD)

    d_q, d_kv, d_kp_h = scatter_behind(
        lambda c: attn_bwd(q_h, kv_h, kp_rot, cos2, sin2, rot, o_att, lse, d_o_h, comm=c),
        ici=['ffn_w_down'], swap=['ffn_w_up', 'w_out'])
    d_kp_rot = headsum("d_kpe_heads", d_kp_h)
    d_qn = mm_sh_nt("d_qn", d_q, gathered['mla_w_uq'])
    d_kvn = mm_sh_nt("d_kvn", d_kv, gathered['mla_w_ukv'])
    gsh['mla_w_uq'] = mm_sh_out("dw_uq", qn, d_q)
    gsh['mla_w_ukv'] = mm_sh_out("dw_ukv", kvn, d_kv)
    d_cq, d_ckv, d_kpe, dg_q_p, dg_kv_p = rowwise_vjp(
        "mla_pre_bwd", _mla_pre_grad_fn, mla_args, mla_kinds, [d_qn, d_kvn, d_kp_rot], ['row', 'row', 'row'], [0, 1, 2, 4, 5],
        heads=1, s=s, tm=tm)

    d_y, d_r_post, d_k_post, d_v_post, d_gate_r, dgnw_p, dgnb_p, drk_p = scatter_behind(
        lambda c: rowwise_vjp("rwkv_post_bwd", _rwkv_post_fn, post_args, post_kinds, [d_yr_h], ['hrow'], list(range(8)),
                              heads=n_rh, s=s, tm=tm_heads, comm=c),
        ici=['ffn_w_gate'], swap=['mla_w_uq', 'mla_w_ukv'])
    d_r_sc, d_w_sc, d_k_sc, d_v_sc, d_a_sc, d_b_sc = scatter_behind(
        lambda c: scan_bwd(hr, decay, kx, hv, a_sc, b_sc, ckpt, d_y, comm=c), ici=['ffn_w_up', 'w_out'])
    d_hk, d_hw_p, d_ha_p, d_hg_p, dw0_p, dw2_p, da0_p, da2_p, dg2_p, dkk_p, dka_p, d_hr, d_hv = scatter_behind(
        lambda c: rowwise_vjp(
            "rwkv_pre_bwd", _rwkv_pre_grad_fn, pre_args + [hr, hv], pre_kinds + ['hrow', 'hrow'],
            [d_w_sc, d_k_sc, d_k_post, d_a_sc, d_b_sc, d_gate_r, d_r_sc, d_r_post, d_v_sc, d_v_post], ['hrow'] * 10,
            list(range(13)), heads=n_rh, s=s, tm=tm_heads, comm=c),
        ici=['mla_w_uq', 'mla_w_ukv'])
    d_shifted = jnp.concatenate([_from_heads(d_hr), _from_heads(d_hk), _from_heads(d_hv), headsum("d_hw_heads", d_hw_p),
                                 headsum("d_ha_heads", d_ha_p), headsum("d_hg_heads", d_hg_p)], axis=-1)
    d_p_rwkv, dmu_p = token_shift_bwd(p_rwkv, w2d['rwkv_mu'], d_shifted, tm_wide)
    d_proj = pad_cols(jnp.concatenate([d_p_rwkv, d_cq, d_ckv, d_kpe], axis=-1).astype(BF16), nb['w_in'], nbp['w_in'])
    gsh['w_in'] = mm_sh_out("dw_in", h1, d_proj)
    chip_sums['w_in'] = rs_chip_sum('w_in', gsh['w_in'])
    (d_h1,) = scatter_behind(lambda c: mm_sh_nt("d_h1", d_proj, gathered['w_in'], comm=c), ici=['w_in'])
    d_xn, dg_attn_p = rowwise_vjp("rms_attn_bwd", _rms_fn, [xs, w2d['attn_norm_g']], ['row', 'const'], [d_h1], ['row'], [0, 1],
                                  heads=1, s=s, tm=tm)
    grad_x = add_slots("grad_x_add", d_xn[None], d_x1[None])[0]

    def from_heads_lora(g):
        return jnp.transpose(g, (1, 0, 2)).reshape(g.shape[1], n_rh * RWKV_HEAD)

    gw = {}
    gw['rwkv_w2'] = from_heads_lora(sum_partials("sum_dw2", dw2_p, True))
    gw['rwkv_a2'] = from_heads_lora(sum_partials("sum_da2", da2_p, True))
    gw['rwkv_g2'] = from_heads_lora(sum_partials("sum_dg2", dg2_p, True))
    dcw_pad = colsum("sum_dconv_w", dcw_p.reshape(dcw_p.shape[0], CONV_W * f_pad)).reshape(CONV_W, f_pad)
    gw['ffn_conv_w'] = unpad_cols(dcw_pad, nb['ffn_w_gate'], nbp['ffn_w_gate'])

    rep = {
        'attn_norm_g': sum_partials("sum_dg_attn", dg_attn_p, False),
        'rwkv_mu': colsum("sum_dmu", dmu_p.reshape(dmu_p.shape[0], shift_dim)),
        'rwkv_w0': sum_partials("sum_dw0", dw0_p, True).reshape(1, c_rwkv),
        'rwkv_a0': sum_partials("sum_da0", da0_p, True).reshape(1, c_rwkv),
        'rwkv_k_k': sum_partials("sum_dkk", dkk_p, True).reshape(1, c_rwkv),
        'rwkv_k_a': sum_partials("sum_dka", dka_p, True).reshape(1, c_rwkv),
        'rwkv_r_k': sum_partials("sum_drk", drk_p, True).reshape(1, c_rwkv),
        'rwkv_gn_w': sum_partials("sum_dgnw", dgnw_p, True).reshape(1, c_rwkv),
        'rwkv_gn_b': sum_partials("sum_dgnb", dgnb_p, True).reshape(1, c_rwkv),
        'mla_q_norm_g': sum_partials("sum_dg_q", dg_q_p, False),
        'mla_kv_norm_g': sum_partials("sum_dg_kv", dg_kv_p, False),
        'ffn_norm_g': sum_partials("sum_dg_ffn", dg_ffn_p, False),
        'ffn_conv_b': unpad_cols(colsum("sum_dconv_b", dcb_p.reshape(dcb_p.shape[0], f_pad)), nb['ffn_w_gate'], nbp['ffn_w_gate']),
        'final_norm_g': sum_partials("sum_dg_final", dg_final_p, False),
        'loss': sum_all("sum_loss", loss_rows.reshape(s // SUBLANES, SUBLANES)),
    }
    rep_pack = Pack([(n, w2d[n].shape) for n in REPLICATED] + [('loss', (1, 1))], 8)
    rep_all = all_gather("gather_rep_grads", rep_pack.pack(rep, F32))
    rep_sum = colsum("sum_rep_grads", rep_all.reshape(N_DEV, rep_pack.total)).reshape(rep_pack.rows, PACK_W)
    rep_g = rep_pack.unpack(rep_sum)
    loss = rep_g.pop('loss').reshape(())

    grads, deltas, new_m, new_v = dict(rep_g), {}, {}, {}
    for n in BIG:
        a, b = w2d[n].shape
        grads[n] = rs_finish(n, chip_sums[n][0], from_chips[n])[:a, :b]
        deltas[n], new_m[n], new_v[n] = rowwise(
            "adamw_" + n, _adamw_fn, [w2d[n], grads[n], m2d[n], v2d[n]], ['row'] * 4, [('row', b, F32)] * 3,
            heads=1, s=a, tm=_pick(a, 256, SUBLANES))
    sm_pack = Pack([(n, w2d[n].shape) for n in SMALL_SHARDED], 8)
    g_shards = {n: _full_to_shards(gw[n], SMALL_SHARDED[n]) for n in SMALL_SHARDED}
    grads.update(sm_pack.unpack(reduce_scatter("small", sm_pack.pack(g_shards, F32, lead=(N_DEV,)))))
    rest_pack = Pack([(n, w2d[n].shape) for n in WEIGHTS if n not in BIG], 8)
    d_r, m_r, v_r = rowwise(
        "adamw_small", _adamw_fn, [rest_pack.pack(w2d, F32), rest_pack.pack(grads, F32), rest_pack.pack(m2d, F32),
                                   rest_pack.pack(v2d, F32)],
        ['row'] * 4, [('row', PACK_W, F32)] * 3, heads=1, s=rest_pack.rows, tm=_pick(rest_pack.rows, 512, SUBLANES))
    deltas.update(rest_pack.unpack(d_r))
    new_m.update(rest_pack.unpack(m_r))
    new_v.update(rest_pack.unpack(v_r))

    def shaped(dct):
        return [dct[n].reshape(out_shapes[n]) for n in WEIGHTS]

    return (loss, grad_x.reshape(x.shape), *shaped(grads), *shaped(deltas), *shaped(new_m), *shaped(new_v))
```

```python
import functools
import math

import jax
import jax.numpy as jnp
import numpy as np
from jax import lax
from jax.experimental import pallas as pl
from jax.experimental.pallas import tpu as pltpu

F32 = jnp.float32
BF16 = jnp.bfloat16
HIGHEST = lax.Precision.HIGHEST
MESH = pl.DeviceIdType.MESH

N_DEV = 8
LANES = 128
SUBLANES = 8
VMEM_LIMIT = 48 * 1024 * 1024
RESIDENT_BYTES = 8 * 1024 * 1024

NORM_EPS = 1e-6
GN_EPS = 64e-5
RWKV_HEAD = 64
QK_NOPE = 128
QK_ROPE = 64
V_HEAD = 128
ROPE_THETA = 10000.0
CONV_W = 3
NEG_INF = -1e30
SCAN_CHUNK = 64
SCAN_HEADS = 16
SCAN_PASSES_SOLVE = 1
SCAN_PASSES_OUT = 1

ADAM_LR = 0.001
ADAM_B1 = 0.9
ADAM_B2 = 0.999
ADAM_EPS = 1e-08
ADAM_WD = 0.01
ADAM_STEP = 10

WEIGHTS = ['attn_norm_g', 'w_in', 'rwkv_mu', 'rwkv_w0', 'rwkv_w2', 'rwkv_a0', 'rwkv_a2', 'rwkv_g2', 'rwkv_k_k',
           'rwkv_k_a', 'rwkv_r_k', 'rwkv_gn_w', 'rwkv_gn_b', 'mla_q_norm_g', 'mla_w_uq', 'mla_kv_norm_g', 'mla_w_ukv',
           'w_out', 'ffn_norm_g', 'ffn_w_gate', 'ffn_w_up', 'ffn_conv_w', 'ffn_conv_b', 'ffn_w_down', 'final_norm_g']
BIG = {'w_in': 'col', 'mla_w_uq': 'col', 'mla_w_ukv': 'col', 'w_out': 'row', 'ffn_w_gate': 'col', 'ffn_w_up': 'col',
       'ffn_w_down': 'row'}
SMALL_SHARDED = {'rwkv_w2': 'col', 'rwkv_a2': 'col', 'rwkv_g2': 'col', 'ffn_conv_w': 'col'}
SHARDED = {**BIG, **SMALL_SHARDED}
REPLICATED = [n for n in WEIGHTS if n not in SHARDED]


def _round_up(n, m):
    return (n + m - 1) // m * m


def _pick(n, cap, unit):
    if n <= cap:
        return n
    best = None
    for t in range(unit, cap + 1, unit):
        if n % t == 0:
            best = t
    assert best is not None, (n, cap, unit)
    return best


def _params(sem):
    return pltpu.CompilerParams(dimension_semantics=sem, vmem_limit_bytes=VMEM_LIMIT)


def mm(name, a, b, add=None, out_dtype=F32):
    m, k = a.shape
    k2, n = b.shape
    assert k == k2, (name, a.shape, b.shape)
    tm = _pick(m, 512, SUBLANES * 2)
    tn = n if k * n * 2 <= RESIDENT_BYTES else _pick(n, 640, LANES)
    has_add = add is not None

    def body(a_ref, b_ref, *rest):
        o_ref = rest[-1]
        acc = jnp.dot(a_ref[...].astype(BF16), b_ref[...].astype(BF16), preferred_element_type=F32)
        if has_add:
            acc = acc + rest[0][...].astype(F32)
        o_ref[...] = acc.astype(o_ref.dtype)

    in_specs = [pl.BlockSpec((tm, k), lambda i, j: (i, 0)), pl.BlockSpec((k, tn), lambda i, j: (0, j))]
    ops = [a, b]
    if has_add:
        in_specs.append(pl.BlockSpec((tm, tn), lambda i, j: (i, j)))
        ops.append(add)
    return pl.pallas_call(
        body, name=name, grid=(m // tm, n // tn), in_specs=in_specs,
        out_specs=pl.BlockSpec((tm, tn), lambda i, j: (i, j)),
        out_shape=jax.ShapeDtypeStruct((m, n), out_dtype),
        compiler_params=_params(("parallel", "parallel")),
    )(*ops)


def mm_nt(name, a, b, out_dtype=F32):
    m, k = a.shape
    n, k2 = b.shape
    assert k == k2, (name, a.shape, b.shape)
    tm = _pick(m, 2048 if m * k * 2 <= RESIDENT_BYTES else 512, SUBLANES * 2)
    tn = _pick(n, 1024, LANES)

    def body(a_ref, b_ref, o_ref):
        acc = lax.dot_general(a_ref[...].astype(BF16), b_ref[...].astype(BF16), (((1,), (1,)), ((), ())),
                              preferred_element_type=F32)
        o_ref[...] = acc.astype(o_ref.dtype)

    return pl.pallas_call(
        body, name=name, grid=(m // tm, n // tn),
        in_specs=[pl.BlockSpec((tm, k), lambda i, j: (i, 0)), pl.BlockSpec((tn, k), lambda i, j: (j, 0))],
        out_specs=pl.BlockSpec((tm, tn), lambda i, j: (i, j)),
        out_shape=jax.ShapeDtypeStruct((m, n), out_dtype),
        compiler_params=_params(("parallel", "parallel")),
    )(a, b)


def mm_sh(name, a, g, out_dtype=F32, comm=None, slabs=False):
    m, k = a.shape
    nd, k2, nbp = g.shape
    assert k == k2, (name, a.shape, g.shape)
    tm = _pick(m, 2048 if m * k * 2 <= RESIDENT_BYTES else 512, SUBLANES * 2)

    def body(a_ref, b_ref, o_ref):
        o_ref[...] = jnp.dot(a_ref[...].astype(BF16), b_ref[...].astype(BF16), preferred_element_type=F32).astype(o_ref.dtype)

    if slabs:
        out_spec, out_shape = pl.BlockSpec((None, tm, nbp), lambda i, j: (j, i, 0)), (nd, m, nbp)
    else:
        out_spec, out_shape = pl.BlockSpec((tm, nbp), lambda i, j: (i, j)), (m, nd * nbp)
    res = _pallas(
        body, name=name, grid=(m // tm, nd),
        in_specs=[pl.BlockSpec((tm, k), lambda i, j: (i, 0)), pl.BlockSpec((None, k, nbp), lambda i, j: (j, 0, 0))],
        out_specs=[out_spec], out_shape=[jax.ShapeDtypeStruct(out_shape, out_dtype)], sem=("parallel", "parallel"), comm=comm,
    )(a, g)
    return res[0] if comm is None else res


def mm_sh_nt(name, a, g, add=None, comm=None):
    nd, k, nbp = g.shape
    slabs = a.ndim == 3
    m = a.shape[1] if slabs else a.shape[0]
    assert a.shape == ((nd, m, nbp) if slabs else (m, nd * nbp)), (name, a.shape, g.shape)
    tm = _pick(m, 512, SUBLANES * 2)
    has_add = add is not None

    def body(a_ref, b_ref, *rest):
        o_ref = rest[-1]
        part = lax.dot_general(a_ref[...].astype(BF16), b_ref[...].astype(BF16), (((1,), (1,)), ((), ())),
                               preferred_element_type=F32)

        @pl.when(pl.program_id(1) == 0)
        def _():
            o_ref[...] = part + rest[0][...] if has_add else part

        @pl.when(pl.program_id(1) != 0)
        def _():
            o_ref[...] += part

    a_spec = pl.BlockSpec((None, tm, nbp), lambda i, j: (j, i, 0)) if slabs else pl.BlockSpec((tm, nbp), lambda i, j: (i, j))
    in_specs = [a_spec, pl.BlockSpec((None, k, nbp), lambda i, j: (j, 0, 0))]
    ops = [a, g]
    if has_add:
        in_specs.append(pl.BlockSpec((tm, k), lambda i, j: (i, 0)))
        ops.append(add)
    res = _pallas(
        body, name=name, grid=(m // tm, nd), in_specs=in_specs,
        out_specs=[pl.BlockSpec((tm, k), lambda i, j: (i, 0))],
        out_shape=[jax.ShapeDtypeStruct((m, k), F32)], sem=("parallel", "arbitrary"), comm=comm,
    )(*ops)
    return res[0] if comm is None else res


def mm_tn(name, a, b):
    m, k = a.shape
    m2, n = b.shape
    assert m == m2, (name, a.shape, b.shape)
    tk = _pick(k, 512, LANES)
    tn = n if m * n * 2 <= RESIDENT_BYTES else _pick(n, 640, LANES)

    def body(a_ref, b_ref, o_ref):
        o_ref[...] = lax.dot_general(a_ref[...].astype(BF16), b_ref[...].astype(BF16), (((0,), (0,)), ((), ())),
                                     preferred_element_type=F32)

    return pl.pallas_call(
        body, name=name, grid=(k // tk, n // tn),
        in_specs=[pl.BlockSpec((m, tk), lambda i, j: (0, i)), pl.BlockSpec((m, tn), lambda i, j: (0, j))],
        out_specs=pl.BlockSpec((tk, tn), lambda i, j: (i, j)),
        out_shape=jax.ShapeDtypeStruct((k, n), F32),
        compiler_params=_params(("parallel", "parallel")),
    )(a, b)


def mm_sh_out(name, a, b, comm=None):
    m, k = a.shape
    slabs = b.ndim == 3
    nbp = b.shape[2] if slabs else b.shape[1] // N_DEV
    assert b.shape == ((N_DEV, m, nbp) if slabs else (m, N_DEV * nbp)), (name, a.shape, b.shape)
    tk = _pick(k, 2048 if k * m * 2 <= RESIDENT_BYTES else 512, LANES)
    b_spec = pl.BlockSpec((None, m, nbp), lambda i, j: (j, 0, 0)) if slabs else pl.BlockSpec((m, nbp), lambda i, j: (0, j))

    def body(a_ref, b_ref, o_ref):
        o_ref[...] = lax.dot_general(a_ref[...].astype(BF16), b_ref[...].astype(BF16), (((0,), (0,)), ((), ())),
                                     preferred_element_type=F32)

    res = _pallas(
        body, name=name, grid=(k // tk, N_DEV),
        in_specs=[pl.BlockSpec((m, tk), lambda i, j: (0, i)), b_spec],
        out_specs=[pl.BlockSpec((None, tk, nbp), lambda i, j: (j, i, 0))],
        out_shape=[jax.ShapeDtypeStruct((N_DEV, k, nbp), F32)], sem=("parallel", "parallel"), comm=comm,
    )(a, b)
    return res[0] if comm is None else res


def pad_cols(y, nb, nbp):
    m = y.shape[0]
    if nb == nbp:
        return y
    return jnp.pad(y.reshape(m, N_DEV, nb), ((0, 0), (0, 0), (0, nbp - nb))).reshape(m, N_DEV * nbp)


def unpad_cols(y, nb, nbp):
    m = y.shape[0]
    if nb == nbp:
        return y
    return y.reshape(m, N_DEV, nbp)[:, :, :nb].reshape(m, N_DEV * nb)


def _in_spec(kind, a, tm):
    if kind == 'row':
        return pl.BlockSpec((tm, a.shape[1]), lambda h, i: (i, 0))
    if kind == 'hrow':
        return pl.BlockSpec((None, tm, a.shape[2]), lambda h, i: (h, i, 0))
    if kind == 'const':
        return pl.BlockSpec(a.shape, lambda h, i: (0, 0))
    assert kind == 'hconst', kind
    return pl.BlockSpec((None,) + a.shape[1:], lambda h, i: (h, 0, 0))


def _row_out(kind, c, dtype, heads, s, tm):
    if kind == 'row':
        assert heads == 1
        return jax.ShapeDtypeStruct((s, c), dtype), pl.BlockSpec((tm, c), lambda h, i: (i, 0))
    return jax.ShapeDtypeStruct((heads, s, c), dtype), pl.BlockSpec((None, tm, c), lambda h, i: (h, i, 0))


def rowwise(name, fn, arrs, kinds, outs, *, heads, s, tm, comm=None):
    n_in = len(arrs)

    def body(*refs):
        vals = fn(*[r[...] for r in refs[:n_in]])
        for o, v in zip(refs[n_in:], vals, strict=True):
            o[...] = v.astype(o.dtype)

    shapes, specs = zip(*[_row_out(k, c, dt, heads, s, tm) for k, c, dt in outs])
    return _pallas(
        body, name=name, grid=(heads, s // tm),
        in_specs=[_in_spec(k, a, tm) for k, a in zip(kinds, arrs, strict=True)],
        out_specs=list(specs), out_shape=list(shapes), sem=("parallel", "parallel"), comm=comm,
    )(*arrs)


def rowwise_vjp(name, fn, arrs, kinds, cots, cot_kinds, wrt, *, heads, s, tm, out_dtypes=None, primal=False, comm=None):
    n_in, n_cot = len(arrs), len(cots)
    nb = s // tm
    out_dtypes = out_dtypes or [F32] * len(wrt)

    def body(*refs):
        vals = [r[...] for r in refs[:n_in]]
        cvals = tuple(r[...].astype(F32) for r in refs[n_in:n_in + n_cot])
        outs = refs[n_in + n_cot:]

        def f(*dv):
            full = list(vals)
            for j, i in enumerate(wrt):
                full[i] = dv[j]
            return tuple(fn(*full))

        prim, vjp_fn = jax.vjp(f, *[vals[i].astype(F32) for i in wrt])
        grads = vjp_fn(cvals)
        for o, g in zip(outs[:len(wrt)], grads, strict=True):
            o[...] = g.astype(o.dtype)
        if primal:
            for o, p in zip(outs[len(wrt):], prim, strict=True):
                o[...] = p.astype(o.dtype)

    shapes, specs = [], []
    for i, dt in zip(wrt, out_dtypes, strict=True):
        kind, a = kinds[i], arrs[i]
        if kind in ('row', 'hrow'):
            c = a.shape[-1]
            sh, sp = _row_out('row' if (kind == 'row' and heads == 1) else 'hrow', c, dt, heads, s, tm)
        else:
            r, c = a.shape[-2:]
            sh = jax.ShapeDtypeStruct((heads, nb, r, c), dt)
            sp = pl.BlockSpec((None, None, r, c), lambda h, i: (h, i, 0, 0))
        shapes.append(sh)
        specs.append(sp)
    if primal:
        for ck, c in zip(cot_kinds, cots, strict=True):
            sh, sp = _row_out(ck, c.shape[-1], F32, heads, s, tm)
            shapes.append(sh)
            specs.append(sp)
    in_specs = [_in_spec(k, a, tm) for k, a in zip(kinds, arrs, strict=True)]
    in_specs += [_in_spec(k, a, tm) for k, a in zip(cot_kinds, cots, strict=True)]
    return _pallas(
        body, name=name, grid=(heads, nb), in_specs=in_specs, out_specs=specs, out_shape=shapes,
        sem=("parallel", "parallel"), comm=comm,
    )(*arrs, *cots)


def colsum(name, x):
    n, m = x.shape
    tc = _pick(m, 32768, LANES) if m % LANES == 0 else m

    def body(x_ref, o_ref):
        acc = x_ref[0:1, :].astype(F32)
        for r in range(1, n):
            acc = acc + x_ref[r:r + 1, :].astype(F32)
        o_ref[...] = acc

    return pl.pallas_call(
        body, name=name, grid=(m // tc,), in_specs=[pl.BlockSpec((n, tc), lambda j: (0, j))],
        out_specs=pl.BlockSpec((1, tc), lambda j: (0, j)), out_shape=jax.ShapeDtypeStruct((1, m), F32),
        compiler_params=_params(("parallel",)),
    )(x)


def headsum(name, x):
    h, s, c = x.shape
    tm = _pick(s, 256, SUBLANES)

    def body(x_ref, o_ref):
        acc = x_ref[0]
        for j in range(1, h):
            acc = acc + x_ref[j]
        o_ref[...] = acc

    return pl.pallas_call(
        body, name=name, grid=(s // tm,), in_specs=[pl.BlockSpec((h, tm, c), lambda i: (0, i, 0))],
        out_specs=pl.BlockSpec((tm, c), lambda i: (i, 0)), out_shape=jax.ShapeDtypeStruct((s, c), F32),
        compiler_params=_params(("parallel",)),
    )(x)


def sum_all(name, x):
    def body(x_ref, o_ref):
        o_ref[...] = jnp.sum(x_ref[...], keepdims=True)

    return pl.pallas_call(body, name=name, out_shape=jax.ShapeDtypeStruct((1, 1), F32))(x)


def sum_partials(name, p, per_head):
    h, nb, r, c = p.shape
    if per_head:
        flat = jnp.transpose(p, (1, 0, 2, 3)).reshape(nb, h * r * c)
        if nb == 1:
            return flat.reshape(h, r, c)
        return colsum(name, flat).reshape(h, r, c)
    flat = p.reshape(h * nb, r * c)
    if h * nb == 1:
        return flat.reshape(r, c)
    return colsum(name, flat).reshape(r, c)


def _rms_fn(x, g):
    xf = x.astype(F32)
    return (xf * lax.rsqrt(jnp.mean(xf * xf, axis=-1, keepdims=True) + NORM_EPS) * g,)


def _softplus(z):
    return jnp.maximum(z, 0.0) + jnp.log(1.0 + jnp.exp(-jnp.abs(z)))


def _rwkv_pre_fn(hk, hw, ha, hg, w0, w2, a0, a2, g2, k_k, k_a):
    zw = w0 + jnp.dot(jnp.tanh(hw), w2, preferred_element_type=F32)
    w_log = -_softplus(-zw) - 0.5
    decay = jnp.exp(-jnp.exp(w_log))
    a = jax.nn.sigmoid(a0 + jnp.dot(ha, a2, preferred_element_type=F32))
    g = jnp.dot(jax.nn.sigmoid(hg), g2, preferred_element_type=F32)
    kk = hk * k_k
    kk = kk * lax.rsqrt(jnp.maximum(jnp.sum(kk * kk, axis=-1, keepdims=True), 1e-24))
    k = hk * (1.0 + (a - 1.0) * k_a)
    return decay, k, -kk, kk * a, g


def _rwkv_pre_grad_fn(hk, hw, ha, hg, w0, w2, a0, a2, g2, k_k, k_a, hr, hv):
    decay, k, a_sc, b_sc, g = _rwkv_pre_fn(hk, hw, ha, hg, w0, w2, a0, a2, g2, k_k, k_a)
    return decay, k, k, a_sc, b_sc, g, hr, hr, hv, hv


def _rwkv_post_fn(y, r, k, v, g, gn_w, gn_b, r_k):
    mu = jnp.mean(y, axis=-1, keepdims=True)
    var = jnp.mean(jnp.square(y - mu), axis=-1, keepdims=True)
    yn = (y - mu) * lax.rsqrt(var + GN_EPS) * gn_w + gn_b
    bonus = jnp.sum(r * k * r_k, axis=-1, keepdims=True) * v
    return ((yn + bonus) * g,)


def _rope_tables(pos, inv_freq2):
    ang = pos * inv_freq2
    return jnp.cos(ang), jnp.sin(ang)


def _rope(t, cos2, sin2, rot):
    return t * cos2 + jnp.dot(t, rot, precision=HIGHEST, preferred_element_type=F32) * sin2


def _mla_pre_fn(c_q, c_kv, k_pe, pos, q_g, kv_g, inv_freq2, rot):
    cos2, sin2 = _rope_tables(pos, inv_freq2)
    return _rms_fn(c_q, q_g)[0], _rms_fn(c_kv, kv_g)[0], _rope(k_pe, cos2, sin2, rot), cos2, sin2


def _mla_pre_grad_fn(c_q, c_kv, k_pe, pos, q_g, kv_g, inv_freq2, rot):
    return _mla_pre_fn(c_q, c_kv, k_pe, pos, q_g, kv_g, inv_freq2, rot)[:3]


def _rope_q_fn(q_pe, cos2, sin2, rot):
    return (_rope(q_pe, cos2, sin2, rot),)


def _loss_fn(x2, g, target):
    y = _rms_fn(x2, g)[0]
    return (0.5 * jnp.mean(jnp.square(y - target), axis=-1, keepdims=True),)


def _adamw_fn(w, g, m, v):
    m = ADAM_B1 * m + (1.0 - ADAM_B1) * g
    v = ADAM_B2 * v + (1.0 - ADAM_B2) * jnp.square(g)
    m_hat = m / (1.0 - ADAM_B1 ** ADAM_STEP)
    v_hat = v / (1.0 - ADAM_B2 ** ADAM_STEP)
    delta = -ADAM_LR * (m_hat / (jnp.sqrt(v_hat) + ADAM_EPS) + ADAM_WD * w)
    return delta, m, v


def _prev_halo_spec(c, tm):
    return pl.BlockSpec((SUBLANES, c), lambda i: (jnp.maximum(i * (tm // SUBLANES) - 1, 0), 0))


def _next_halo_spec(c, tm, s):
    return pl.BlockSpec((SUBLANES, c), lambda i: (jnp.minimum((i + 1) * (tm // SUBLANES), s // SUBLANES - 1), 0))


def _shift_down(p, halo, first_block, n):
    out = pltpu.roll(p, n, 0)
    row = lax.broadcasted_iota(jnp.int32, p.shape, 0)
    for j in range(n):
        top = jnp.where(first_block, 0.0, halo[SUBLANES - n + j:SUBLANES - n + j + 1, :])
        out = jnp.where(row == j, top, out)
    return out


def _shift_up(p, halo, last_block, n):
    rows = p.shape[0]
    out = pltpu.roll(p, rows - n, 0)
    row = lax.broadcasted_iota(jnp.int32, p.shape, 0)
    for j in range(n):
        bot = jnp.where(last_block, 0.0, halo[j:j + 1, :])
        out = jnp.where(row == rows - n + j, bot, out)
    return out


def token_shift_fwd(p, mu, tm):
    s, c = p.shape

    def body(p_ref, halo_ref, mu_ref, o_ref):
        pv = p_ref[...]
        prev = _shift_down(pv, halo_ref[...], pl.program_id(0) == 0, 1)
        o_ref[...] = pv + (prev - pv) * mu_ref[...]

    return pl.pallas_call(
        body, name="token_shift_fwd", grid=(s // tm,),
        in_specs=[pl.BlockSpec((tm, c), lambda i: (i, 0)), _prev_halo_spec(c, tm), pl.BlockSpec((1, c), lambda i: (0, 0))],
        out_specs=pl.BlockSpec((tm, c), lambda i: (i, 0)), out_shape=jax.ShapeDtypeStruct((s, c), F32),
        compiler_params=_params(("parallel",)),
    )(p, p, mu)


def token_shift_bwd(p, mu, ds, tm):
    s, c = p.shape
    nb = s // tm

    def body(p_ref, halo_ref, mu_ref, ds_ref, dsn_ref, dp_ref, dmu_ref):
        i = pl.program_id(0)
        pv, dsv, muv = p_ref[...], ds_ref[...], mu_ref[...]
        prev = _shift_down(pv, halo_ref[...], i == 0, 1)
        nxt = _shift_up(dsv, dsn_ref[...], i == nb - 1, 1)
        dp_ref[...] = dsv * (1.0 - muv) + nxt * muv
        dmu_ref[...] = jnp.sum(dsv * (prev - pv), axis=0, keepdims=True)

    return pl.pallas_call(
        body, name="token_shift_bwd", grid=(nb,),
        in_specs=[pl.BlockSpec((tm, c), lambda i: (i, 0)), _prev_halo_spec(c, tm), pl.BlockSpec((1, c), lambda i: (0, 0)),
                  pl.BlockSpec((tm, c), lambda i: (i, 0)), _next_halo_spec(c, tm, s)],
        out_specs=[pl.BlockSpec((tm, c), lambda i: (i, 0)), pl.BlockSpec((None, 1, c), lambda i: (i, 0, 0))],
        out_shape=[jax.ShapeDtypeStruct((s, c), F32), jax.ShapeDtypeStruct((nb, 1, c), F32)],
        compiler_params=_params(("parallel",)),
    )(p, p, mu, ds, ds)


def _ffn_tiles(s, f):
    return _pick(s, 256, SUBLANES), _pick(f, 1408, LANES)


def _conv_gate(gp, halo, first_block, cw, cb):
    p1 = _shift_down(gp, halo, first_block, 1)
    p2 = _shift_down(gp, halo, first_block, 2)
    return cw[0:1, :] * p2 + cw[1:2, :] * p1 + cw[2:3, :] * gp + cb, p1, p2


def ffn_act_fwd(gate_pre, up, conv_w, conv_b):
    s, f = gate_pre.shape
    tm, tc = _ffn_tiles(s, f)

    def body(gp_ref, halo_ref, up_ref, cw_ref, cb_ref, o_ref):
        gate, _, _ = _conv_gate(gp_ref[...], halo_ref[...], pl.program_id(0) == 0, cw_ref[...], cb_ref[...])
        o_ref[...] = (gate * jax.nn.sigmoid(gate) * up_ref[...]).astype(o_ref.dtype)

    blk = pl.BlockSpec((tm, tc), lambda i, j: (i, j))
    return pl.pallas_call(
        body, name="ffn_act_fwd", grid=(s // tm, f // tc),
        in_specs=[blk, pl.BlockSpec((SUBLANES, tc), lambda i, j: (jnp.maximum(i * (tm // SUBLANES) - 1, 0), j)), blk,
                  pl.BlockSpec((CONV_W, tc), lambda i, j: (0, j)), pl.BlockSpec((1, tc), lambda i, j: (0, j))],
        out_specs=blk, out_shape=jax.ShapeDtypeStruct((s, f), BF16),
        compiler_params=_params(("parallel", "parallel")),
    )(gate_pre, gate_pre, up, conv_w, conv_b)


def ffn_act_bwd1(gate_pre, up, conv_w, conv_b, d_act):
    s, f = gate_pre.shape
    tm, tc = _ffn_tiles(s, f)
    nb = s // tm

    def body(gp_ref, halo_ref, up_ref, cw_ref, cb_ref, da_ref, dg_ref, du_ref, dcw_ref, dcb_ref):
        gp = gp_ref[...]
        gate, p1, p2 = _conv_gate(gp, halo_ref[...], pl.program_id(0) == 0, cw_ref[...], cb_ref[...])
        sig = jax.nn.sigmoid(gate)
        da = da_ref[...].astype(F32)
        du_ref[...] = (da * gate * sig).astype(du_ref.dtype)
        dg = da * up_ref[...] * (sig * (1.0 + gate * (1.0 - sig)))
        dg_ref[...] = dg
        dcb_ref[...] = jnp.sum(dg, axis=0, keepdims=True)
        dcw_ref[0:1, :] = jnp.sum(dg * p2, axis=0, keepdims=True)
        dcw_ref[1:2, :] = jnp.sum(dg * p1, axis=0, keepdims=True)
        dcw_ref[2:3, :] = jnp.sum(dg * gp, axis=0, keepdims=True)

    blk = pl.BlockSpec((tm, tc), lambda i, j: (i, j))
    return pl.pallas_call(
        body, name="ffn_act_bwd1", grid=(nb, f // tc),
        in_specs=[blk, pl.BlockSpec((SUBLANES, tc), lambda i, j: (jnp.maximum(i * (tm // SUBLANES) - 1, 0), j)), blk,
                  pl.BlockSpec((CONV_W, tc), lambda i, j: (0, j)), pl.BlockSpec((1, tc), lambda i, j: (0, j)), blk],
        out_specs=[blk, blk, pl.BlockSpec((None, CONV_W, tc), lambda i, j: (i, 0, j)),
                   pl.BlockSpec((None, 1, tc), lambda i, j: (i, 0, j))],
        out_shape=[jax.ShapeDtypeStruct((s, f), F32), jax.ShapeDtypeStruct((s, f), BF16),
                   jax.ShapeDtypeStruct((nb, CONV_W, f), F32), jax.ShapeDtypeStruct((nb, 1, f), F32)],
        compiler_params=_params(("parallel", "parallel")),
    )(gate_pre, gate_pre, up, conv_w, conv_b, d_act)


def ffn_act_bwd2(d_gate, conv_w):
    s, f = d_gate.shape
    tm, tc = _ffn_tiles(s, f)
    nb = s // tm

    def body(dg_ref, halo_ref, cw_ref, o_ref):
        dg, cw = dg_ref[...], cw_ref[...]
        last = pl.program_id(0) == nb - 1
        n1 = _shift_up(dg, halo_ref[...], last, 1)
        n2 = _shift_up(dg, halo_ref[...], last, 2)
        o_ref[...] = (cw[2:3, :] * dg + cw[1:2, :] * n1 + cw[0:1, :] * n2).astype(o_ref.dtype)

    blk = pl.BlockSpec((tm, tc), lambda i, j: (i, j))
    return pl.pallas_call(
        body, name="ffn_act_bwd2", grid=(nb, f // tc),
        in_specs=[blk, pl.BlockSpec((SUBLANES, tc), lambda i, j: (jnp.minimum((i + 1) * (tm // SUBLANES), s // SUBLANES - 1), j)),
                  pl.BlockSpec((CONV_W, tc), lambda i, j: (0, j))],
        out_specs=blk, out_shape=jax.ShapeDtypeStruct((s, f), BF16),
        compiler_params=_params(("parallel", "parallel")),
    )(d_gate, d_gate, conv_w)


def _mxu(x, y, cx, cy):
    if x.ndim == 3:
        return lax.dot_general(x, y, (((cx + 1,), (cy + 1,)), ((0,), (0,))), preferred_element_type=F32)
    return lax.dot_general(x, y, (((cx,), (cy,)), ((), ())), preferred_element_type=F32)


def _split(x):
    hi = x.astype(BF16)
    return hi, (x - hi.astype(F32)).astype(BF16)


def _make_dot3(cx, cy, passes):
    @jax.custom_vjp
    def f(x, y):
        if passes == 1:
            return _mxu(x.astype(BF16), y.astype(BF16), cx, cy)
        xh, xl = _split(x)
        yh, yl = _split(y)
        return _mxu(xh, yh, cx, cy) + (_mxu(xh, yl, cx, cy) + _mxu(xl, yh, cx, cy))

    def fwd(x, y):
        return f(x, y), (x, y)

    def bwd(res, g):
        x, y = res
        dx = dot3(g, y, 1, 1 - cy, passes) if cx == 1 else dot3(y, g, 1 - cy, 1, passes)
        dy = dot3(x, g, 1 - cx, 0, passes) if cy == 0 else dot3(g, x, 0, 1 - cx, passes)
        return dx, dy

    f.defvjp(fwd, bwd)
    return f


_DOT3 = {}


def dot3(x, y, cx, cy, passes=3):
    if (cx, cy, passes) not in _DOT3:
        _DOT3[(cx, cy, passes)] = _make_dot3(cx, cy, passes)
    return _DOT3[(cx, cy, passes)](x, y)


def _dot(x, y, passes=3):
    return dot3(x, y, 1, 0, passes)


def _dot_nt(x, y, passes=3):
    return dot3(x, y, 1, 1, passes)


def _dot_tn(x, y, passes=3):
    return dot3(x, y, 0, 0, passes)


def _tri_sum(x, lower):
    t = x.shape[-2]
    row = lax.broadcasted_iota(jnp.int32, (t, t), 0)
    col = lax.broadcasted_iota(jnp.int32, (t, t), 1)
    tri = jnp.where((col <= row) if lower else (col >= row), 1.0, 0.0).astype(BF16)
    if x.ndim == 3:
        tri = jnp.broadcast_to(tri[None], (x.shape[0], t, t))
    hi = x.astype(BF16)
    rest = x - hi.astype(F32)
    mid = rest.astype(BF16)
    low = (rest - mid.astype(F32)).astype(BF16)
    return _mxu(tri, hi, 1, 0) + (_mxu(tri, mid, 1, 0) + _mxu(tri, low, 1, 0))


@jax.custom_vjp
def _cumsum_rows(x):
    return _tri_sum(x, True)


_cumsum_rows.defvjp(lambda x: (_tri_sum(x, True), None), lambda _, g: (_tri_sum(g, False),))


def _scan_chunk(s0, r, w, k, v, a, b):
    t = r.shape[1]
    row = lax.broadcasted_iota(jnp.int32, (1, t, t), 1)
    col = lax.broadcasted_iota(jnp.int32, (1, t, t), 2)
    strict, incl = col < row, col <= row
    logw = jnp.log(w)
    cum = _cumsum_rows(logw)
    w_in, w_ex, w_inv = jnp.exp(cum), jnp.exp(cum - logw), jnp.exp(-cum)
    w_all = jnp.exp(jnp.sum(logw, axis=1, keepdims=True))
    at, rt, kt, bt = a * w_ex, r * w_in, k * w_inv, b * w_inv
    ps, po = SCAN_PASSES_SOLVE, SCAN_PASSES_OUT
    a_ab = jnp.where(strict, _dot_nt(at, bt, ps), 0.0)
    a_ak = jnp.where(strict, _dot_nt(at, kt, ps), 0.0)
    a_rk = jnp.where(incl, _dot_nt(rt, kt, po), 0.0)
    a_rb = jnp.where(incl, _dot_nt(rt, bt, po), 0.0)
    u = _dot_nt(at, s0, ps) + _dot(a_ak, v, ps)
    p = a_ab
    steps = int(math.log2(t))
    assert 2 ** steps == t
    for j in range(steps):
        u = u + _dot(p, u, ps)
        if j < steps - 1:
            p = _dot(p, p, ps)
    y = _dot_nt(rt, s0, po) + _dot(a_rk, v, po) + _dot(a_rb, u, po)
    s_new = s0 * w_all + _dot_tn(v, kt * w_all, po) + _dot_tn(u, bt * w_all, po)
    return y, s_new


def scan_fwd(r, w, k, v, a, b, comm=None):
    h, s, n = r.shape
    t = min(SCAN_CHUNK, s)
    nc = s // t

    hb = SCAN_HEADS if h % SCAN_HEADS == 0 else 1

    def body(r_ref, w_ref, k_ref, v_ref, a_ref, b_ref, y_ref, ck_ref, st_ref):
        @pl.when(pl.program_id(1) == 0)
        def _():
            st_ref[...] = jnp.zeros_like(st_ref)

        s0 = st_ref[...]
        ck_ref[...] = s0
        y, s_new = _scan_chunk(s0, r_ref[...], w_ref[...], k_ref[...], v_ref[...], a_ref[...], b_ref[...])
        y_ref[...] = y
        st_ref[...] = s_new

    blk = pl.BlockSpec((hb, t, n), lambda hh, c: (hh, c, 0))
    return _pallas(
        body, name="rwkv_scan_fwd", grid=(h // hb, nc), in_specs=[blk] * 6,
        out_specs=[blk, pl.BlockSpec((hb, None, n, n), lambda hh, c: (hh, c, 0, 0))],
        out_shape=[jax.ShapeDtypeStruct((h, s, n), F32), jax.ShapeDtypeStruct((h, nc, n, n), F32)],
        scratch_shapes=[pltpu.VMEM((hb, n, n), F32)], sem=("parallel", "arbitrary"), comm=comm,
    )(r, w, k, v, a, b)


def scan_bwd(r, w, k, v, a, b, ck, dy, comm=None):
    h, s, n = r.shape
    t = min(SCAN_CHUNK, s)
    nc = s // t

    hb = SCAN_HEADS if h % SCAN_HEADS == 0 else 1

    def body(r_ref, w_ref, k_ref, v_ref, a_ref, b_ref, ck_ref, dy_ref, dr_ref, dw_ref, dk_ref, dv_ref, da_ref, db_ref, ds_ref):
        @pl.when(pl.program_id(1) == 0)
        def _():
            ds_ref[...] = jnp.zeros_like(ds_ref)

        _, vjp_fn = jax.vjp(_scan_chunk, ck_ref[...], r_ref[...], w_ref[...], k_ref[...], v_ref[...], a_ref[...], b_ref[...])
        ds0, dr, dw, dk, dv, da, db = vjp_fn((dy_ref[...], ds_ref[...]))
        ds_ref[...] = ds0
        dr_ref[...], dw_ref[...], dk_ref[...], dv_ref[...], da_ref[...], db_ref[...] = dr, dw, dk, dv, da, db

    blk = pl.BlockSpec((hb, t, n), lambda hh, c: (hh, nc - 1 - c, 0))
    return _pallas(
        body, name="rwkv_scan_bwd", grid=(h // hb, nc),
        in_specs=[blk] * 6 + [pl.BlockSpec((hb, None, n, n), lambda hh, c: (hh, nc - 1 - c, 0, 0)), blk],
        out_specs=[blk] * 6, out_shape=[jax.ShapeDtypeStruct((h, s, n), F32)] * 6,
        scratch_shapes=[pltpu.VMEM((hb, n, n), F32)], sem=("parallel", "arbitrary"), comm=comm,
    )(r, w, k, v, a, b, ck, dy)


ATTN_BLOCK = 256
ATTN_LEVELS = 4
ATTN_SLAB = 2 * LANES


def _attn_specs(h, s, tq):
    qblk = lambda c, part: pl.BlockSpec((None, tq, c), lambda hh, i: (hh, i, part))
    kblk = lambda part: pl.BlockSpec((None, s, LANES), lambda hh, i: (hh, 0, part))
    row64 = pl.BlockSpec((tq, QK_ROPE), lambda hh, i: (i, 0))
    return [qblk(LANES, 0), qblk(LANES, 1), kblk(0), kblk(1), pl.BlockSpec((s, QK_ROPE), lambda hh, i: (0, 0)),
            row64, row64, pl.BlockSpec((QK_ROPE, QK_ROPE), lambda hh, i: (0, 0))]


def _attn_levels(s, tq):
    nq = s // tq
    n_lev = min(ATTN_LEVELS, nq)
    per = nq // n_lev
    return [(lv * per, (lv + 1) * per, (lv + 1) * per * tq) for lv in range(n_lev)]


def _attn_scores(qn_b, qp_b, kn_ref, kp_ref, klen, i, tq):
    scale = (QK_NOPE + QK_ROPE) ** -0.5
    kn_b = kn_ref[0:klen, :].astype(BF16)
    kp_b = kp_ref[0:klen, :].astype(BF16)
    sc = lax.dot_general(qn_b, kn_b, (((1,), (1,)), ((), ())), preferred_element_type=F32)
    sc = sc + lax.dot_general(qp_b, kp_b, (((1,), (1,)), ((), ())), preferred_element_type=F32)
    row = i * tq + lax.broadcasted_iota(jnp.int32, sc.shape, 0)
    col = lax.broadcasted_iota(jnp.int32, sc.shape, 1)
    return jnp.where(row >= col, sc * scale, NEG_INF), scale, kn_b, kp_b


def attn_fwd(q_h, kv_h, kp, cos2, sin2, rot, comm=None):
    h, s, _ = q_h.shape
    tq = _pick(s, ATTN_BLOCK, SUBLANES)

    def body(qn_ref, qp_ref, kn_ref, v_ref, kp_ref, cos_ref, sin_ref, rot_ref, o_ref, lse_ref):
        i = pl.program_id(1)
        qn_b = qn_ref[...].astype(BF16)
        qp_b = _rope(qp_ref[:, :QK_ROPE], cos_ref[...], sin_ref[...], rot_ref[...]).astype(BF16)

        def level(klen):
            sc, _, _, _ = _attn_scores(qn_b, qp_b, kn_ref, kp_ref, klen, i, tq)
            mx = jnp.max(sc, axis=-1, keepdims=True)
            e = jnp.exp(sc - mx)
            den = jnp.sum(e, axis=-1, keepdims=True)
            o_ref[...] = jnp.dot((e / den).astype(BF16), v_ref[0:klen, :].astype(BF16), preferred_element_type=F32)
            lse_ref[...] = mx + jnp.log(den)

        for lo, hi, klen in _attn_levels(s, tq):
            pl.when((i >= lo) & (i < hi))(functools.partial(level, klen))

    oblk = lambda c: pl.BlockSpec((None, tq, c), lambda hh, i: (hh, i, 0))
    return _pallas(
        body, name="mla_attn_fwd", grid=(h, s // tq), in_specs=_attn_specs(h, s, tq),
        out_specs=[oblk(V_HEAD), oblk(1)],
        out_shape=[jax.ShapeDtypeStruct((h, s, V_HEAD), F32), jax.ShapeDtypeStruct((h, s, 1), F32)],
        sem=("parallel", "parallel"), comm=comm,
    )(q_h, q_h, kv_h, kv_h, kp, cos2, sin2, rot)


def attn_bwd(q_h, kv_h, kp, cos2, sin2, rot, o, lse, do, comm=None):
    h, s, _ = q_h.shape
    tq = _pick(s, ATTN_BLOCK, SUBLANES)
    nq = s // tq

    def body(qn_ref, qp_ref, kn_ref, v_ref, kp_ref, cos_ref, sin_ref, rot_ref, o_ref, lse_ref, do_ref,
             dq_ref, dkv_ref, dkp_ref, dkv_acc, dkp_acc):
        i = pl.program_id(1)

        @pl.when(i == 0)
        def _():
            dkv_acc[...] = jnp.zeros_like(dkv_acc)
            dkp_acc[...] = jnp.zeros_like(dkp_acc)

        cosv, sinv, rotv = cos_ref[...], sin_ref[...], rot_ref[...]
        qn_b = qn_ref[...].astype(BF16)
        qp_b = _rope(qp_ref[:, :QK_ROPE], cosv, sinv, rotv).astype(BF16)
        dov = do_ref[...]
        do_b = dov.astype(BF16)
        delta = jnp.sum(dov * o_ref[...], axis=-1, keepdims=True)
        lsev = lse_ref[...]

        def level(klen):
            sc, scale, kn_b, kp_b = _attn_scores(qn_b, qp_b, kn_ref, kp_ref, klen, i, tq)
            p = jnp.exp(sc - lsev)
            dp = lax.dot_general(do_b, v_ref[0:klen, :].astype(BF16), (((1,), (1,)), ((), ())), preferred_element_type=F32)
            ds = (p * (dp - delta) * scale).astype(BF16)
            dkv_acc[0:klen, :LANES] += lax.dot_general(ds, qn_b, (((0,), (0,)), ((), ())), preferred_element_type=F32)
            dkv_acc[0:klen, LANES:] += lax.dot_general(p.astype(BF16), do_b, (((0,), (0,)), ((), ())), preferred_element_type=F32)
            dkp_acc[0:klen, :] += lax.dot_general(ds, qp_b, (((0,), (0,)), ((), ())), preferred_element_type=F32)
            dqp = jnp.dot(ds, kp_b, preferred_element_type=F32)
            dqp_raw = dqp * cosv + lax.dot_general(dqp * sinv, rotv, (((1,), (1,)), ((), ())), precision=HIGHEST,
                                                   preferred_element_type=F32)
            dq_ref[:, :QK_NOPE] = jnp.dot(ds, kn_b, preferred_element_type=F32).astype(dq_ref.dtype)
            dq_ref[:, QK_NOPE:QK_NOPE + QK_ROPE] = dqp_raw.astype(dq_ref.dtype)
            dq_ref[:, QK_NOPE + QK_ROPE:] = jnp.zeros((tq, ATTN_SLAB - QK_NOPE - QK_ROPE), dq_ref.dtype)

        for lo, hi, klen in _attn_levels(s, tq):
            pl.when((i >= lo) & (i < hi))(functools.partial(level, klen))

        @pl.when(i == nq - 1)
        def _():
            dkv_ref[...] = dkv_acc[...].astype(dkv_ref.dtype)
            dkp_ref[...] = dkp_acc[...]

    rblk = lambda c: pl.BlockSpec((None, tq, c), lambda hh, i: (hh, i, 0))
    sblk = lambda c: pl.BlockSpec((None, s, c), lambda hh, i: (hh, 0, 0))
    return _pallas(
        body, name="mla_attn_bwd", grid=(h, nq),
        in_specs=_attn_specs(h, s, tq) + [rblk(V_HEAD), rblk(1), rblk(V_HEAD)],
        out_specs=[rblk(ATTN_SLAB), sblk(ATTN_SLAB), sblk(QK_ROPE)],
        out_shape=[jax.ShapeDtypeStruct((h, s, ATTN_SLAB), BF16), jax.ShapeDtypeStruct((h, s, ATTN_SLAB), BF16),
                   jax.ShapeDtypeStruct((h, s, QK_ROPE), F32)],
        scratch_shapes=[pltpu.VMEM((s, ATTN_SLAB), F32), pltpu.VMEM((s, QK_ROPE), F32)],
        sem=("parallel", "arbitrary"), comm=comm,
    )(q_h, q_h, kv_h, kv_h, kp, cos2, sin2, rot, o, lse, do)


def _my_pos():
    return lax.axis_index("x"), lax.axis_index("y"), lax.axis_index("c")


def _dev_index(px, py, pc):
    return 4 * px + 2 * py + pc


def all_gather(name, shard):
    r, c = shard.shape

    def body(x_ref, out_ref, send_sems, recv_sems, local_sem):
        x, y, cc = _my_pos()
        me, sibling = (x, y, cc), (x, y, 1 - cc)
        chips = [(1 - x, y), (x, 1 - y), (1 - x, 1 - y)]

        def rows(px, py, pc):
            return out_ref.at[_dev_index(px, py, pc)]

        def copy(kk, block, to, src=None):
            return pltpu.make_async_remote_copy(
                src_ref=rows(*block) if src is None else src, dst_ref=rows(*block),
                send_sem=send_sems.at[kk], recv_sem=recv_sems.at[kk], device_id=to, device_id_type=MESH)

        mine = pltpu.make_async_copy(x_ref, rows(*me), local_sem)
        mine.start()
        first = [copy(0, me, sibling, src=x_ref)]
        first += [copy(1 + j, me, (*chip, cc), src=x_ref) for j, chip in enumerate(chips)]
        for cp in first:
            cp.start()
        passed = [copy(4 + j, (*chip, cc), sibling) for j, chip in enumerate(chips)]
        for j, chip in enumerate(chips):
            copy(1 + j, (*chip, cc), me).wait_recv()
            passed[j].start()
        copy(0, sibling, me).wait_recv()
        for j, chip in enumerate(chips):
            copy(4 + j, (*chip, 1 - cc), me).wait_recv()
        for cp in first + passed:
            cp.wait_send()
        mine.wait()

    return pl.pallas_call(
        body, name=name, out_shape=jax.ShapeDtypeStruct((N_DEV, r, c), shard.dtype),
        in_specs=[pl.BlockSpec(memory_space=pl.ANY)], out_specs=pl.BlockSpec(memory_space=pl.ANY),
        scratch_shapes=[pltpu.SemaphoreType.DMA((7,)), pltpu.SemaphoreType.DMA((7,)), pltpu.SemaphoreType.DMA],
    )(shard)


def _flip(kind):
    x, y, c = _my_pos()
    return {'c': (x, y, 1 - c), 'x': (1 - x, y, c), 'y': (x, 1 - y, c), 'xy': (1 - x, 1 - y, c)}[kind]


def exchange_sibling(name, g):
    _, r, c = g.shape

    def body(g_ref, out_ref, send_sems, recv_sems):
        x, y, cc = _my_pos()
        copies = []
        for px in range(2):
            for py in range(2):
                slot = 2 * px + py
                copies.append(pltpu.make_async_remote_copy(
                    src_ref=g_ref.at[_dev_index(px, py, 1 - cc)], dst_ref=out_ref.at[slot],
                    send_sem=send_sems.at[slot], recv_sem=recv_sems.at[slot], device_id=(x, y, 1 - cc), device_id_type=MESH))
        for cp in copies:
            cp.start()
        for cp in copies:
            cp.wait()

    return pl.pallas_call(
        body, name=name, out_shape=jax.ShapeDtypeStruct((4, r, c), g.dtype),
        in_specs=[pl.BlockSpec(memory_space=pl.ANY)], out_specs=pl.BlockSpec(memory_space=pl.ANY),
        scratch_shapes=[pltpu.SemaphoreType.DMA((4,)), pltpu.SemaphoreType.DMA((4,))],
    )(g)


def exchange_chips(name, hsum):
    _, r, c = hsum.shape

    def body(h_ref, out_ref, send_sems, recv_sems):
        x, y, cc = _my_pos()
        copies = []
        for j, (px, py) in enumerate([(1 - x, y), (x, 1 - y), (1 - x, 1 - y)]):
            copies.append(pltpu.make_async_remote_copy(
                src_ref=h_ref.at[2 * px + py], dst_ref=out_ref.at[j],
                send_sem=send_sems.at[j], recv_sem=recv_sems.at[j], device_id=(px, py, cc), device_id_type=MESH))
        for cp in copies:
            cp.start()
        for cp in copies:
            cp.wait()

    return pl.pallas_call(
        body, name=name, out_shape=jax.ShapeDtypeStruct((3, r, c), hsum.dtype),
        in_specs=[pl.BlockSpec(memory_space=pl.ANY)], out_specs=pl.BlockSpec(memory_space=pl.ANY),
        scratch_shapes=[pltpu.SemaphoreType.DMA((3,)), pltpu.SemaphoreType.DMA((3,))],
    )(hsum)


def add_slots(name, *terms):
    n, r, c = terms[0].shape
    tr = _pick(r, 512, SUBLANES)

    def body(*refs):
        acc = refs[0][...]
        for t in refs[1:-1]:
            acc = acc + t[...]
        refs[-1][...] = acc

    blk = pl.BlockSpec((None, tr, c), lambda s_, i: (s_, i, 0))
    return pl.pallas_call(
        body, name=name, grid=(n, r // tr), in_specs=[blk] * len(terms), out_specs=blk,
        out_shape=jax.ShapeDtypeStruct((n, r, c), F32), compiler_params=_params(("parallel", "parallel")),
    )(*terms)


def _rs_add_sibling(name, g, from_sibling, cc):
    _, r, c = g.shape
    tr = _pick(r, 512, SUBLANES * 2)

    def body(cc_ref, g_ref, s_ref, o_ref, ob_ref):
        tot = g_ref[...] + s_ref[...]
        o_ref[...] = tot
        ob_ref[...] = tot.astype(BF16)

    blk = pl.BlockSpec((None, tr, c), lambda s_, i, cc_ref: (s_, i, 0))
    return pl.pallas_call(
        body, name=name,
        grid_spec=pltpu.PrefetchScalarGridSpec(
            num_scalar_prefetch=1, grid=(4, r // tr),
            in_specs=[pl.BlockSpec((None, None, tr, c), lambda s_, i, cc_ref: (s_, cc_ref[0], i, 0)), blk], out_specs=[blk, blk]),
        out_shape=[jax.ShapeDtypeStruct((4, r, c), F32), jax.ShapeDtypeStruct((4, r, c), BF16)],
        compiler_params=_params(("parallel", "parallel")),
    )(cc.reshape(1).astype(jnp.int32), g.reshape(4, 2, r, c), from_sibling)


def _rs_add_chips(name, chip_sum, from_chips, slot):
    _, r, c = chip_sum.shape
    tr = _pick(r, 512, SUBLANES * 2)

    def body(slot_ref, h_ref, f0_ref, f1_ref, f2_ref, o_ref):
        o_ref[...] = ((h_ref[...] + f0_ref[...].astype(F32)) + f1_ref[...].astype(F32)) + f2_ref[...].astype(F32)

    def from_blk(j):
        return pl.BlockSpec((None, tr, c), lambda i, slot_ref: (j, i, 0))

    return pl.pallas_call(
        body, name=name,
        grid_spec=pltpu.PrefetchScalarGridSpec(
            num_scalar_prefetch=1, grid=(r // tr,),
            in_specs=[pl.BlockSpec((None, tr, c), lambda i, slot_ref: (slot_ref[0], i, 0)), from_blk(0), from_blk(1), from_blk(2)],
            out_specs=pl.BlockSpec((tr, c), lambda i, slot_ref: (i, 0))),
        out_shape=jax.ShapeDtypeStruct((r, c), F32), compiler_params=_params(("parallel",)),
    )(slot.reshape(1).astype(jnp.int32), chip_sum, from_chips, from_chips, from_chips)


def rs_chip_sum(tag, g):
    _, _, cc = _my_pos()
    from_sibling = exchange_sibling("rs_sibling_" + tag, g)
    return _rs_add_sibling("rs_add_sibling_" + tag, g, from_sibling, cc)


def rs_finish(tag, chip_sum, from_chips):
    x, y, _ = _my_pos()
    return _rs_add_chips("rs_add_chips_" + tag, chip_sum, from_chips, 2 * x + y)


def reduce_scatter(tag, g):
    chip_sum, chip_sum_b = rs_chip_sum(tag, g)
    return rs_finish(tag, chip_sum, exchange_chips("rs_chips_" + tag, chip_sum_b))


class GatherIci:
    def __init__(self, shards):
        self.inputs = list(shards)
        self.out_shapes = [jax.ShapeDtypeStruct((N_DEV,) + s.shape, s.dtype) for s in shards]
        self.n_remote, self.n_local = 3 * len(shards), len(shards)

    def make(self, cins, couts, send, recv, local):
        x, y, cc = _my_pos()
        me = _dev_index(x, y, cc)
        copies = []
        for w, (src, out) in enumerate(zip(cins, couts, strict=True)):
            copies.append(pltpu.make_async_copy(src, out.at[me], local.at[w]))
            for j, (px, py) in enumerate([(1 - x, y), (x, 1 - y), (1 - x, 1 - y)]):
                copies.append(pltpu.make_async_remote_copy(
                    src_ref=src, dst_ref=out.at[me], send_sem=send.at[3 * w + j], recv_sem=recv.at[3 * w + j],
                    device_id=(px, py, cc), device_id_type=MESH))
        return copies


class RsChips:
    def __init__(self, chip_sums):
        self.inputs = list(chip_sums)
        self.out_shapes = [jax.ShapeDtypeStruct((3,) + h.shape[1:], h.dtype) for h in chip_sums]
        self.n_remote, self.n_local = 3 * len(chip_sums), 0

    def make(self, cins, couts, send, recv, local):
        x, y, cc = _my_pos()
        copies = []
        for w, (h_ref, out) in enumerate(zip(cins, couts, strict=True)):
            for j, (px, py) in enumerate([(1 - x, y), (x, 1 - y), (1 - x, 1 - y)]):
                copies.append(pltpu.make_async_remote_copy(
                    src_ref=h_ref.at[2 * px + py], dst_ref=out.at[j], send_sem=send.at[3 * w + j], recv_sem=recv.at[3 * w + j],
                    device_id=(px, py, cc), device_id_type=MESH))
        return copies


class SiblingSwap:
    def __init__(self, gs):
        self.inputs = list(gs)
        self.out_shapes = [jax.ShapeDtypeStruct((4,) + g.shape[1:], g.dtype) for g in gs]
        self.n_remote, self.n_local = 4 * len(gs), 0

    def make(self, cins, couts, send, recv, local):
        x, y, cc = _my_pos()
        copies = []
        for w, (g_ref, out) in enumerate(zip(cins, couts, strict=True)):
            for px in range(2):
                for py in range(2):
                    q = 4 * w + 2 * px + py
                    copies.append(pltpu.make_async_remote_copy(
                        src_ref=g_ref.at[_dev_index(px, py, 1 - cc)], dst_ref=out.at[2 * px + py],
                        send_sem=send.at[q], recv_sem=recv.at[q], device_id=(x, y, 1 - cc), device_id_type=MESH))
        return copies


class _SemSlice:
    def __init__(self, base, start):
        self.base, self.start = base, start

    @property
    def at(self):
        return self

    def __getitem__(self, k):
        return self.base.at[self.start + k]


class CommGroup:
    def __init__(self, plans):
        self.plans = [p for p in plans if p.inputs]
        self.inputs = [a for p in self.plans for a in p.inputs]
        self.out_shapes = [s_ for p in self.plans for s_ in p.out_shapes]
        self.n_remote = sum(p.n_remote for p in self.plans)
        self.n_local = sum(p.n_local for p in self.plans)

    def make(self, cins, couts, send, recv, local):
        copies, i0, o0, r0, l0 = [], 0, 0, 0, 0
        for p in self.plans:
            ni, no = len(p.inputs), len(p.out_shapes)
            copies += p.make(cins[i0:i0 + ni], couts[o0:o0 + no], _SemSlice(send, r0), _SemSlice(recv, r0), _SemSlice(local, l0))
            i0, o0, r0, l0 = i0 + ni, o0 + no, r0 + p.n_remote, l0 + p.n_local
        return copies


def gather_d2d(name, arrays):
    n = len(arrays)

    def body(*refs):
        outs, send, recv = refs[n:2 * n], refs[2 * n], refs[2 * n + 1]
        x, y, cc = _my_pos()
        copies = []
        for w, out in enumerate(outs):
            for px in range(2):
                for py in range(2):
                    q = 4 * w + 2 * px + py
                    slab = out.at[_dev_index(px, py, cc)]
                    copies.append(pltpu.make_async_remote_copy(
                        src_ref=slab, dst_ref=slab, send_sem=send.at[q], recv_sem=recv.at[q],
                        device_id=(x, y, 1 - cc), device_id_type=MESH))
        for cp in copies:
            cp.start()
        for cp in copies:
            cp.wait()

    any_spec = pl.BlockSpec(memory_space=pl.ANY)
    return pl.pallas_call(
        body, name=name, out_shape=[jax.ShapeDtypeStruct(a.shape, a.dtype) for a in arrays],
        in_specs=[any_spec] * n, out_specs=[any_spec] * n, input_output_aliases={i: i for i in range(n)},
        scratch_shapes=[pltpu.SemaphoreType.DMA((4 * n,)), pltpu.SemaphoreType.DMA((4 * n,))],
    )(*arrays)


def _pallas(body, *, name, grid, in_specs, out_specs, out_shape, scratch_shapes=(), sem, comm=None):
    in_specs, out_specs, out_shape, scratch_shapes = list(in_specs), list(out_specs), list(out_shape), list(scratch_shapes)
    if comm is None:
        return pl.pallas_call(body, name=name, grid=grid, in_specs=in_specs, out_specs=out_specs, out_shape=out_shape,
                              scratch_shapes=scratch_shapes, compiler_params=_params(sem))
    n_in, n_out, n_scr = len(in_specs), len(out_specs), len(scratch_shapes)
    nci, nco = len(comm.inputs), len(comm.out_shapes)

    def body2(*refs):
        ins, cins = refs[:n_in], refs[n_in:n_in + nci]
        o0 = n_in + nci
        outs, couts = refs[o0:o0 + n_out], refs[o0 + n_out:o0 + n_out + nco]
        s0 = o0 + n_out + nco
        scr = refs[s0:s0 + n_scr]
        send, recv, local = refs[s0 + n_scr:]
        pids = [pl.program_id(k) for k in range(len(grid))]
        first = functools.reduce(jnp.logical_and, [p == 0 for p in pids])
        last = functools.reduce(jnp.logical_and, [p == g - 1 for p, g in zip(pids, grid)])

        @pl.when(first)
        def _():
            for cp in comm.make(cins, couts, send, recv, local):
                cp.start()

        body(*ins, *outs, *scr)

        @pl.when(last)
        def _():
            for cp in comm.make(cins, couts, send, recv, local):
                cp.wait()

    any_spec = pl.BlockSpec(memory_space=pl.ANY)
    call = pl.pallas_call(
        body2, name=name, grid=grid, in_specs=in_specs + [any_spec] * nci, out_specs=out_specs + [any_spec] * nco,
        out_shape=out_shape + list(comm.out_shapes),
        scratch_shapes=scratch_shapes + [pltpu.SemaphoreType.DMA((comm.n_remote,)), pltpu.SemaphoreType.DMA((comm.n_remote,)),
                                         pltpu.SemaphoreType.DMA((max(comm.n_local, 1),))],
        compiler_params=_params(tuple("arbitrary" for _ in grid)))
    return lambda *args: call(*args, *comm.inputs)


PACK_W = 1024


class Pack:
    def __init__(self, entries, row_unit):
        self.entries = entries
        self.sizes = [int(np.prod(sh)) for _, sh in entries]
        self.offsets = np.concatenate([[0], np.cumsum(self.sizes)]).tolist()
        self.total = _round_up(self.offsets[-1], PACK_W * row_unit)
        self.rows = self.total // PACK_W

    def pack(self, arrays, dtype, lead=()):
        flat = [arrays[n].astype(dtype).reshape(lead + (-1,)) for n, _ in self.entries]
        pad = self.total - self.offsets[-1]
        if pad:
            flat.append(jnp.zeros(lead + (pad,), dtype))
        return jnp.concatenate(flat, axis=-1).reshape(lead + (self.rows, PACK_W))

    def unpack(self, buf, lead=()):
        flat = buf.reshape(lead + (self.total,))
        out = {}
        for (n, sh), off, sz in zip(self.entries, self.offsets, self.sizes):
            out[n] = lax.slice_in_dim(flat, off, off + sz, axis=len(lead)).reshape(lead + tuple(sh))
        return out


def _gathered_to_full(g, how):
    _, a, b = g.shape
    if how == 'row':
        return g.reshape(N_DEV * a, b)
    return jnp.transpose(g, (1, 0, 2)).reshape(a, N_DEV * b)


def _full_to_shards(w, how):
    a, b = w.shape
    if how == 'row':
        return w.reshape(N_DEV, a // N_DEV, b)
    return jnp.transpose(w.reshape(a, N_DEV, b // N_DEV), (1, 0, 2))


def _to_heads(t, width):
    s, c = t.shape
    return jnp.transpose(t.reshape(s, c // width, width), (1, 0, 2))


def _from_heads(t):
    h, s, w = t.shape
    return jnp.transpose(t, (1, 0, 2)).reshape(s, h * w)


def _rot_matrix():
    half = QK_ROPE // 2
    rot = np.zeros((QK_ROPE, QK_ROPE), np.float32)
    for i in range(half):
        rot[i + half, i] = -1.0
        rot[i, i + half] = 1.0
    return jnp.asarray(rot)


def _inv_freq2():
    half = QK_ROPE // 2
    inv = ROPE_THETA ** (-np.arange(half, dtype=np.float32) / half)
    return jnp.asarray(np.concatenate([inv, inv])[None, :].astype(np.float32))


def kernel(x, positions, attn_norm_g, w_in, rwkv_mu, rwkv_w0, rwkv_w2, rwkv_a0, rwkv_a2, rwkv_g2, rwkv_k_k, rwkv_k_a, rwkv_r_k, rwkv_gn_w, rwkv_gn_b, mla_q_norm_g, mla_w_uq, mla_kv_norm_g, mla_w_ukv, w_out, ffn_norm_g, ffn_w_gate, ffn_w_up, ffn_conv_w, ffn_conv_b, ffn_w_down, final_norm_g, loss_target, m_attn_norm_g, m_w_in, m_rwkv_mu, m_rwkv_w0, m_rwkv_w2, m_rwkv_a0, m_rwkv_a2, m_rwkv_g2, m_rwkv_k_k, m_rwkv_k_a, m_rwkv_r_k, m_rwkv_gn_w, m_rwkv_gn_b, m_mla_q_norm_g, m_mla_w_uq, m_mla_kv_norm_g, m_mla_w_ukv, m_w_out, m_ffn_norm_g, m_ffn_w_gate, m_ffn_w_up, m_ffn_conv_w, m_ffn_conv_b, m_ffn_w_down, m_final_norm_g, v_attn_norm_g, v_w_in, v_rwkv_mu, v_rwkv_w0, v_rwkv_w2, v_rwkv_a0, v_rwkv_a2, v_rwkv_g2, v_rwkv_k_k, v_rwkv_k_a, v_rwkv_r_k, v_rwkv_gn_w, v_rwkv_gn_b, v_mla_q_norm_g, v_mla_w_uq, v_mla_kv_norm_g, v_mla_w_ukv, v_w_out, v_ffn_norm_g, v_ffn_w_gate, v_ffn_w_up, v_ffn_conv_w, v_ffn_conv_b, v_ffn_w_down, v_final_norm_g):
    given = dict(locals())
    wts = {n: given[n] for n in WEIGHTS}
    mom_m = {n: given["m_" + n] for n in WEIGHTS}
    mom_v = {n: given["v_" + n] for n in WEIGHTS}
    out_shapes = {n: wts[n].shape for n in WEIGHTS}

    def local2d(n, a):
        if n == 'rwkv_r_k' or a.ndim <= 2:
            return a.reshape(1, -1)
        return a.reshape(a.shape[1:])

    w2d = {n: local2d(n, wts[n]) for n in WEIGHTS}
    m2d = {n: local2d(n, mom_m[n]) for n in WEIGHTS}
    v2d = {n: local2d(n, mom_v[n]) for n in WEIGHTS}

    xs = x.reshape(x.shape[1:])
    tgt = loss_target.reshape(loss_target.shape[1:])
    s, d = xs.shape
    c_rwkv = w2d['rwkv_w0'].shape[1]
    n_rh = c_rwkv // RWKV_HEAD
    decay_lora, aaa_lora, gate_lora = w2d['rwkv_w2'].shape[0], w2d['rwkv_a2'].shape[0], w2d['rwkv_g2'].shape[0]
    q_lora, kv_lora = w2d['mla_q_norm_g'].shape[1], w2d['mla_kv_norm_g'].shape[1]
    shift_dim = w2d['rwkv_mu'].shape[1]
    d_in = w2d['w_in'].shape[1] * N_DEV
    d_in_pad = _round_up(d_in, LANES)
    n_mh = w2d['mla_w_uq'].shape[1] * N_DEV // (QK_NOPE + QK_ROPE)
    d_ff = w2d['ffn_conv_b'].shape[1]
    tm = _pick(s, 256, SUBLANES)
    tm_wide = _pick(s, 128, SUBLANES)
    tm_heads = _pick(s, 512, SUBLANES)

    nb = {n: w2d[n].shape[1] for n in BIG if BIG[n] == 'col'}
    nbp = {n: _round_up(v_, LANES) for n, v_ in nb.items()}
    shards = {}
    for n in BIG:
        w = w2d[n].astype(BF16)
        if BIG[n] == 'col':
            w = jnp.pad(w, ((0, 0), (0, nbp[n] - nb[n])))
        elif n == 'ffn_w_down':
            w = jnp.pad(w, ((0, nbp['ffn_w_gate'] - w.shape[0]), (0, 0)))
        shards[n] = w

    def as_used(n, g):
        return g if BIG[n] == 'col' else g.reshape(N_DEV * g.shape[1], g.shape[2])

    gathered = {'w_in': as_used('w_in', all_gather("gather_w_in", shards['w_in']))}
    later = [n for n in BIG if n != 'w_in']
    f_pad = N_DEV * nbp['ffn_w_gate']
    small_pack = Pack([(n, w2d[n].shape) for n in SMALL_SHARDED], 8)
    small_all = all_gather("gather_small", small_pack.pack(w2d, F32))
    full = {}
    for n, g in small_pack.unpack(small_all, lead=(N_DEV,)).items():
        full[n] = _gathered_to_full(g, SMALL_SHARDED[n])
    conv_w_pad = pad_cols(full['ffn_conv_w'], nb['ffn_w_gate'], nbp['ffn_w_gate'])
    conv_b_pad = pad_cols(w2d['ffn_conv_b'], nb['ffn_w_gate'], nbp['ffn_w_gate'])

    (h1,) = rowwise("rms_attn", _rms_fn, [xs, w2d['attn_norm_g']], ['row', 'const'], [('row', d, BF16)], heads=1, s=s, tm=tm)
    def gather_behind(names, run):
        *res, = run(GatherIci([shards[n] for n in names]))
        landed = res[len(res) - len(names):]
        for n, g in zip(names, gather_d2d("gather_d2d_" + names[0], landed), strict=True):
            gathered[n] = as_used(n, g)
        return res[:len(res) - len(names)]

    (proj_pad,) = gather_behind(['mla_w_uq', 'mla_w_ukv', 'w_out'], lambda c: mm_sh("proj_in", h1, gathered['w_in'], comm=c))
    proj = unpad_cols(proj_pad, nb['w_in'], nbp['w_in'])
    p_rwkv = proj[:, :shift_dim]
    c_q = proj[:, shift_dim:shift_dim + q_lora]
    c_kv = proj[:, shift_dim + q_lora:shift_dim + q_lora + kv_lora]
    k_pe = proj[:, shift_dim + q_lora + kv_lora:d_in]
    shifted = token_shift_fwd(p_rwkv, w2d['rwkv_mu'], tm_wide)
    o1, o2, o3 = c_rwkv, 2 * c_rwkv, 3 * c_rwkv
    hr = _to_heads(shifted[:, :o1], RWKV_HEAD)
    hk = _to_heads(shifted[:, o1:o2], RWKV_HEAD)
    hv = _to_heads(shifted[:, o2:o3], RWKV_HEAD)
    hw = shifted[:, o3:o3 + decay_lora]
    ha = shifted[:, o3 + decay_lora:o3 + decay_lora + aaa_lora]
    hg = shifted[:, o3 + decay_lora + aaa_lora:]

    def per_head(vec):
        return vec.reshape(n_rh, 1, RWKV_HEAD)

    def lora_heads(w):
        return jnp.transpose(w.reshape(w.shape[0], n_rh, RWKV_HEAD), (1, 0, 2))

    pre_args = [hk, hw, ha, hg, per_head(w2d['rwkv_w0']), lora_heads(full['rwkv_w2']), per_head(w2d['rwkv_a0']),
                lora_heads(full['rwkv_a2']), lora_heads(full['rwkv_g2']), per_head(w2d['rwkv_k_k']), per_head(w2d['rwkv_k_a'])]
    pre_kinds = ['hrow', 'row', 'row', 'row', 'hconst', 'hconst', 'hconst', 'hconst', 'hconst', 'hconst', 'hconst']
    decay, kx, a_sc, b_sc, gate_r = gather_behind(['ffn_w_gate'], lambda c: rowwise(
        "rwkv_pre", _rwkv_pre_fn, pre_args, pre_kinds, [('hrow', RWKV_HEAD, F32)] * 5, heads=n_rh, s=s, tm=tm_heads, comm=c))
    y_scan, ckpt = gather_behind(['ffn_w_up'], lambda c: scan_fwd(hr, decay, kx, hv, a_sc, b_sc, comm=c))
    post_args = [y_scan, hr, kx, hv, gate_r, per_head(w2d['rwkv_gn_w']), per_head(w2d['rwkv_gn_b']), per_head(w2d['rwkv_r_k'])]
    post_kinds = ['hrow'] * 5 + ['hconst'] * 3
    (y_rwkv_h,) = rowwise("rwkv_post", _rwkv_post_fn, post_args, post_kinds, [('hrow', RWKV_HEAD, F32)], heads=n_rh, s=s, tm=tm_heads)

    pos = positions.reshape(s, 1).astype(F32)
    rot, inv2 = _rot_matrix(), _inv_freq2()
    mla_args = [c_q, c_kv, k_pe, pos, w2d['mla_q_norm_g'], w2d['mla_kv_norm_g'], inv2, rot]
    mla_kinds = ['row', 'row', 'row', 'row', 'const', 'const', 'const', 'const']
    qn, kvn, kp_rot, cos2, sin2 = rowwise(
        "mla_pre", _mla_pre_fn, mla_args, mla_kinds,
        [('row', q_lora, BF16), ('row', kv_lora, BF16), ('row', QK_ROPE, F32), ('row', QK_ROPE, F32), ('row', QK_ROPE, F32)],
        heads=1, s=s, tm=tm)
    assert n_mh == N_DEV and nb['mla_w_uq'] == QK_NOPE + QK_ROPE and nb['mla_w_ukv'] == QK_NOPE + V_HEAD
    assert nbp['mla_w_uq'] == ATTN_SLAB and nbp['mla_w_ukv'] == ATTN_SLAB
    q_h = mm_sh("proj_q", qn, gathered['mla_w_uq'], slabs=True)
    kv_h = mm_sh("proj_kv", kvn, gathered['mla_w_ukv'], slabs=True)
    o_att, lse = gather_behind(['ffn_w_down'], lambda c: attn_fwd(q_h, kv_h, kp_rot, cos2, sin2, rot, comm=c))
    ycat = jnp.concatenate([_from_heads(y_rwkv_h), _from_heads(o_att)], axis=-1).astype(BF16)
    x1 = mm("proj_out", ycat, gathered['w_out'], add=xs)
    (h2,) = rowwise("rms_ffn", _rms_fn, [x1, w2d['ffn_norm_g']], ['row', 'const'], [('row', d, BF16)], heads=1, s=s, tm=tm)
    gate_pre = mm_sh("ffn_gate", h2, gathered['ffn_w_gate'])
    up = mm_sh("ffn_up", h2, gathered['ffn_w_up'])
    act = ffn_act_fwd(gate_pre, up, conv_w_pad, conv_b_pad)
    x2 = mm("ffn_down", act, gathered['ffn_w_down'], add=x1)

    ones = jnp.ones((s, 1), F32)
    fin_g = w2d['final_norm_g']
    d_x2, dg_final_p, loss_rows = rowwise_vjp("loss_bwd", _loss_fn, [x2, fin_g, tgt], ['row', 'const', 'row'], [ones], ['row'],
                                              [0, 1], heads=1, s=s, tm=tm, primal=True)
    d_x2_b = d_x2.astype(BF16)
    d_act = mm_nt("d_act", d_x2_b, gathered['ffn_w_down'], out_dtype=BF16)
    gsh, chip_sums, from_chips = {}, {}, {}

    def scatter_behind(run, ici=(), swap=()):
        *res, = run(CommGroup([RsChips([chip_sums[n][1] for n in ici]), SiblingSwap([gsh[n] for n in swap])]))
        n_own = len(res) - len(ici) - len(swap)
        from_chips.update(zip(ici, res[n_own:n_own + len(ici)], strict=True))
        _, _, cc = _my_pos()
        for n, from_sibling in zip(swap, res[n_own + len(ici):], strict=True):
            chip_sums[n] = _rs_add_sibling("rs_add_sibling_" + n, gsh[n], from_sibling, cc)
        return res[:n_own]

    gsh['ffn_w_down'] = mm_tn("dw_down", act, d_x2_b).reshape(N_DEV, nbp['ffn_w_gate'], d)
    d_gate, d_up, dcw_p, dcb_p = ffn_act_bwd1(gate_pre, up, conv_w_pad, conv_b_pad, d_act)
    d_gp = ffn_act_bwd2(d_gate, conv_w_pad)
    (d_h2_g,) = scatter_behind(lambda c: mm_sh_nt("d_h2_gate", d_gp, gathered['ffn_w_gate'], comm=c), swap=['ffn_w_down'])
    d_h2 = mm_sh_nt("d_h2_up", d_up, gathered['ffn_w_up'], add=d_h2_g)
    gsh['ffn_w_gate'] = mm_sh_out("dw_gate", h2, d_gp)
    (gsh['ffn_w_up'],) = scatter_behind(lambda c: mm_sh_out("dw_up", h2, d_up, comm=c), swap=['ffn_w_gate'])
    d_x1n, dg_ffn_p = rowwise_vjp("rms_ffn_bwd", _rms_fn, [x1, w2d['ffn_norm_g']], ['row', 'const'], [d_h2], ['row'], [0, 1],
                                  heads=1, s=s, tm=tm)
    d_x1 = add_slots("d_x1_add", d_x1n[None], d_x2[None])[0]
    d_x1_b = d_x1.astype(BF16)
    d_ycat = mm_nt("d_ycat", d_x1_b, gathered['w_out'])
    gsh['w_out'] = mm_tn("dw_out", ycat, d_x1_b).reshape((N_DEV,) + w2d['w_out'].shape)
    d_yr_h = _to_heads(d_ycat[:, :c_rwkv], RWKV_HEAD)
    d_o_h = _to_heads(d_ycat[:, c_rwkv:], V_HEAD)

    d_q, d_kv, d_kp_h = scatter_behind(
        lambda c: attn_bwd(q_h, kv_h, kp_rot, cos2, sin2, rot, o_att, lse, d_o_h, comm=c),
        ici=['ffn_w_down'], swap=['ffn_w_up', 'w_out'])
    d_kp_rot = headsum("d_kpe_heads", d_kp_h)
    d_qn = mm_sh_nt("d_qn", d_q, gathered['mla_w_uq'])
    d_kvn = mm_sh_nt("d_kvn", d_kv, gathered['mla_w_ukv'])
    gsh['mla_w_uq'] = mm_sh_out("dw_uq", qn, d_q)
    gsh['mla_w_ukv'] = mm_sh_out("dw_ukv", kvn, d_kv)
    d_cq, d_ckv, d_kpe, dg_q_p, dg_kv_p = rowwise_vjp(
        "mla_pre_bwd", _mla_pre_grad_fn, mla_args, mla_kinds, [d_qn, d_kvn, d_kp_rot], ['row', 'row', 'row'], [0, 1, 2, 4, 5],
        heads=1, s=s, tm=tm)

    d_y, d_r_post, d_k_post, d_v_post, d_gate_r, dgnw_p, dgnb_p, drk_p = scatter_behind(
        lambda c: rowwise_vjp("rwkv_post_bwd", _rwkv_post_fn, post_args, post_kinds, [d_yr_h], ['hrow'], list(range(8)),
                              heads=n_rh, s=s, tm=tm_heads, comm=c),
        ici=['ffn_w_gate'], swap=['mla_w_uq', 'mla_w_ukv'])
    d_r_sc, d_w_sc, d_k_sc, d_v_sc, d_a_sc, d_b_sc = scatter_behind(
        lambda c: scan_bwd(hr, decay, kx, hv, a_sc, b_sc, ckpt, d_y, comm=c), ici=['ffn_w_up', 'w_out'])
    d_hk, d_hw_p, d_ha_p, d_hg_p, dw0_p, dw2_p, da0_p, da2_p, dg2_p, dkk_p, dka_p, d_hr, d_hv = scatter_behind(
        lambda c: rowwise_vjp(
            "rwkv_pre_bwd", _rwkv_pre_grad_fn, pre_args + [hr, hv], pre_kinds + ['hrow', 'hrow'],
            [d_w_sc, d_k_sc, d_k_post, d_a_sc, d_b_sc, d_gate_r, d_r_sc, d_r_post, d_v_sc, d_v_post], ['hrow'] * 10,
            list(range(13)), heads=n_rh, s=s, tm=tm_heads, comm=c),
        ici=['mla_w_uq', 'mla_w_ukv'])
    d_shifted = jnp.concatenate([_from_heads(d_hr), _from_heads(d_hk), _from_heads(d_hv), headsum("d_hw_heads", d_hw_p),
                                 headsum("d_ha_heads", d_ha_p), headsum("d_hg_heads", d_hg_p)], axis=-1)
    d_p_rwkv, dmu_p = token_shift_bwd(p_rwkv, w2d['rwkv_mu'], d_shifted, tm_wide)
    d_proj = pad_cols(jnp.concatenate([d_p_rwkv, d_cq, d_ckv, d_kpe], axis=-1).astype(BF16), nb['w_in'], nbp['w_in'])
    gsh['w_in'] = mm_sh_out("dw_in", h1, d_proj)
    chip_sums['w_in'] = rs_chip_sum('w_in', gsh['w_in'])
    (d_h1,) = scatter_behind(lambda c: mm_sh_nt("d_h1", d_proj, gathered['w_in'], comm=c), ici=['w_in'])
    d_xn, dg_attn_p = rowwise_vjp("rms_attn_bwd", _rms_fn, [xs, w2d['attn_norm_g']], ['row', 'const'], [d_h1], ['row'], [0, 1],
                                  heads=1, s=s, tm=tm)
    grad_x = add_slots("grad_x_add", d_xn[None], d_x1[None])[0]

    def from_heads_lora(g):
        return jnp.transpose(g, (1, 0, 2)).reshape(g.shape[1], n_rh * RWKV_HEAD)

    gw = {}
    gw['rwkv_w2'] = from_heads_lora(sum_partials("sum_dw2", dw2_p, True))
    gw['rwkv_a2'] = from_heads_lora(sum_partials("sum_da2", da2_p, True))
    gw['rwkv_g2'] = from_heads_lora(sum_partials("sum_dg2", dg2_p, True))
    dcw_pad = colsum("sum_dconv_w", dcw_p.reshape(dcw_p.shape[0], CONV_W * f_pad)).reshape(CONV_W, f_pad)
    gw['ffn_conv_w'] = unpad_cols(dcw_pad, nb['ffn_w_gate'], nbp['ffn_w_gate'])

    rep = {
        'attn_norm_g': sum_partials("sum_dg_attn", dg_attn_p, False),
        'rwkv_mu': colsum("sum_dmu", dmu_p.reshape(dmu_p.shape[0], shift_dim)),
        'rwkv_w0': sum_partials("sum_dw0", dw0_p, True).reshape(1, c_rwkv),
        'rwkv_a0': sum_partials("sum_da0", da0_p, True).reshape(1, c_rwkv),
        'rwkv_k_k': sum_partials("sum_dkk", dkk_p, True).reshape(1, c_rwkv),
        'rwkv_k_a': sum_partials("sum_dka", dka_p, True).reshape(1, c_rwkv),
        'rwkv_r_k': sum_partials("sum_drk", drk_p, True).reshape(1, c_rwkv),
        'rwkv_gn_w': sum_partials("sum_dgnw", dgnw_p, True).reshape(1, c_rwkv),
        'rwkv_gn_b': sum_partials("sum_dgnb", dgnb_p, True).reshape(1, c_rwkv),
        'mla_q_norm_g': sum_partials("sum_dg_q", dg_q_p, False),
        'mla_kv_norm_g': sum_partials("sum_dg_kv", dg_kv_p, False),
        'ffn_norm_g': sum_partials("sum_dg_ffn", dg_ffn_p, False),
        'ffn_conv_b': unpad_cols(colsum("sum_dconv_b", dcb_p.reshape(dcb_p.shape[0], f_pad)), nb['ffn_w_gate'], nbp['ffn_w_gate']),
        'final_norm_g': sum_partials("sum_dg_final", dg_final_p, False),
        'loss': sum_all("sum_loss", loss_rows.reshape(s // SUBLANES, SUBLANES)),
    }
    rep_pack = Pack([(n, w2d[n].shape) for n in REPLICATED] + [('loss', (1, 1))], 8)
    rep_all = all_gather("gather_rep_grads", rep_pack.pack(rep, F32))
    rep_sum = colsum("sum_rep_grads", rep_all.reshape(N_DEV, rep_pack.total)).reshape(rep_pack.rows, PACK_W)
    rep_g = rep_pack.unpack(rep_sum)
    loss = rep_g.pop('loss').reshape(())

    grads, deltas, new_m, new_v = dict(rep_g), {}, {}, {}
    for n in BIG:
        a, b = w2d[n].shape
        grads[n] = rs_finish(n, chip_sums[n][0], from_chips[n])[:a, :b]
        deltas[n], new_m[n], new_v[n] = rowwise(
            "adamw_" + n, _adamw_fn, [w2d[n], grads[n], m2d[n], v2d[n]], ['row'] * 4, [('row', b, F32)] * 3,
            heads=1, s=a, tm=_pick(a, 256, SUBLANES))
    sm_pack = Pack([(n, w2d[n].shape) for n in SMALL_SHARDED], 8)
    g_shards = {n: _full_to_shards(gw[n], SMALL_SHARDED[n]) for n in SMALL_SHARDED}
    grads.update(sm_pack.unpack(reduce_scatter("small", sm_pack.pack(g_shards, F32, lead=(N_DEV,)))))
    rest_pack = Pack([(n, w2d[n].shape) for n in WEIGHTS if n not in BIG], 8)
    d_r, m_r, v_r = rowwise(
        "adamw_small", _adamw_fn, [rest_pack.pack(w2d, F32), rest_pack.pack(grads, F32), rest_pack.pack(m2d, F32),
                                   rest_pack.pack(v2d, F32)],
        ['row'] * 4, [('row', PACK_W, F32)] * 3, heads=1, s=rest_pack.rows, tm=_pick(rest_pack.rows, 512, SUBLANES))
    deltas.update(rest_pack.unpack(d_r))
    new_m.update(rest_pack.unpack(m_r))
    new_v.update(rest_pack.unpack(v_r))

    def shaped(dct):
        return [dct[n].reshape(out_shapes[n]) for n in WEIGHTS]

    return (loss, grad_x.reshape(x.shape), *shaped(grads), *shaped(deltas), *shaped(new_m), *shaped(new_v))
```

```python
import functools
import math

import jax
import jax.numpy as jnp
import numpy as np
from jax import lax
from jax.experimental import pallas as pl
from jax.experimental.pallas import tpu as pltpu

F32 = jnp.float32
BF16 = jnp.bfloat16
HIGHEST = lax.Precision.HIGHEST
MESH = pl.DeviceIdType.MESH

N_DEV = 8
LANES = 128
SUBLANES = 8
VMEM_LIMIT = 48 * 1024 * 1024
RESIDENT_BYTES = 8 * 1024 * 1024

NORM_EPS = 1e-6
GN_EPS = 64e-5
RWKV_HEAD = 64
QK_NOPE = 128
QK_ROPE = 64
V_HEAD = 128
ROPE_THETA = 10000.0
CONV_W = 3
NEG_INF = -1e30
SCAN_CHUNK = 64
SCAN_HEADS = 16
SCAN_PASSES_SOLVE = 1
SCAN_PASSES_OUT = 1

ADAM_LR = 0.001
ADAM_B1 = 0.9
ADAM_B2 = 0.999
ADAM_EPS = 1e-08
ADAM_WD = 0.01
ADAM_STEP = 10

WEIGHTS = ['attn_norm_g', 'w_in', 'rwkv_mu', 'rwkv_w0', 'rwkv_w2', 'rwkv_a0', 'rwkv_a2', 'rwkv_g2', 'rwkv_k_k',
           'rwkv_k_a', 'rwkv_r_k', 'rwkv_gn_w', 'rwkv_gn_b', 'mla_q_norm_g', 'mla_w_uq', 'mla_kv_norm_g', 'mla_w_ukv',
           'w_out', 'ffn_norm_g', 'ffn_w_gate', 'ffn_w_up', 'ffn_conv_w', 'ffn_conv_b', 'ffn_w_down', 'final_norm_g']
BIG = {'w_in': 'col', 'mla_w_uq': 'col', 'mla_w_ukv': 'col', 'w_out': 'row', 'ffn_w_gate': 'col', 'ffn_w_up': 'col',
       'ffn_w_down': 'row'}
SMALL_SHARDED = {'rwkv_w2': 'col', 'rwkv_a2': 'col', 'rwkv_g2': 'col', 'ffn_conv_w': 'col'}
SHARDED = {**BIG, **SMALL_SHARDED}
REPLICATED = [n for n in WEIGHTS if n not in SHARDED]


def _round_up(n, m):
    return (n + m - 1) // m * m


def _pick(n, cap, unit):
    if n <= cap:
        return n
    best = None
    for t in range(unit, cap + 1, unit):
        if n % t == 0:
            best = t
    assert best is not None, (n, cap, unit)
    return best


def _params(sem):
    return pltpu.CompilerParams(dimension_semantics=sem, vmem_limit_bytes=VMEM_LIMIT)


def mm(name, a, b, add=None, out_dtype=F32):
    m, k = a.shape
    k2, n = b.shape
    assert k == k2, (name, a.shape, b.shape)
    tm = _pick(m, 512, SUBLANES * 2)
    tn = n if k * n * 2 <= RESIDENT_BYTES else _pick(n, 640, LANES)
    has_add = add is not None

    def body(a_ref, b_ref, *rest):
        o_ref = rest[-1]
        acc = jnp.dot(a_ref[...].astype(BF16), b_ref[...].astype(BF16), preferred_element_type=F32)
        if has_add:
            acc = acc + rest[0][...].astype(F32)
        o_ref[...] = acc.astype(o_ref.dtype)

    in_specs = [pl.BlockSpec((tm, k), lambda i, j: (i, 0)), pl.BlockSpec((k, tn), lambda i, j: (0, j))]
    ops = [a, b]
    if has_add:
        in_specs.append(pl.BlockSpec((tm, tn), lambda i, j: (i, j)))
        ops.append(add)
    return pl.pallas_call(
        body, name=name, grid=(m // tm, n // tn), in_specs=in_specs,
        out_specs=pl.BlockSpec((tm, tn), lambda i, j: (i, j)),
        out_shape=jax.ShapeDtypeStruct((m, n), out_dtype),
        compiler_params=_params(("parallel", "parallel")),
    )(*ops)


def mm_nt(name, a, b, out_dtype=F32):
    m, k = a.shape
    n, k2 = b.shape
    assert k == k2, (name, a.shape, b.shape)
    tm = _pick(m, 2048 if m * k * 2 <= RESIDENT_BYTES else 512, SUBLANES * 2)
    tn = _pick(n, 1024, LANES)

    def body(a_ref, b_ref, o_ref):
        acc = lax.dot_general(a_ref[...].astype(BF16), b_ref[...].astype(BF16), (((1,), (1,)), ((), ())),
                              preferred_element_type=F32)
        o_ref[...] = acc.astype(o_ref.dtype)

    return pl.pallas_call(
        body, name=name, grid=(m // tm, n // tn),
        in_specs=[pl.BlockSpec((tm, k), lambda i, j: (i, 0)), pl.BlockSpec((tn, k), lambda i, j: (j, 0))],
        out_specs=pl.BlockSpec((tm, tn), lambda i, j: (i, j)),
        out_shape=jax.ShapeDtypeStruct((m, n), out_dtype),
        compiler_params=_params(("parallel", "parallel")),
    )(a, b)


def mm_sh(name, a, g, out_dtype=F32, comm=None, slabs=False):
    m, k = a.shape
    nd, k2, nbp = g.shape
    assert k == k2, (name, a.shape, g.shape)
    tm = _pick(m, 2048 if m * k * 2 <= RESIDENT_BYTES else 512, SUBLANES * 2)

    def body(a_ref, b_ref, o_ref):
        o_ref[...] = jnp.dot(a_ref[...].astype(BF16), b_ref[...].astype(BF16), preferred_element_type=F32).astype(o_ref.dtype)

    if slabs:
        out_spec, out_shape = pl.BlockSpec((None, tm, nbp), lambda i, j: (j, i, 0)), (nd, m, nbp)
    else:
        out_spec, out_shape = pl.BlockSpec((tm, nbp), lambda i, j: (i, j)), (m, nd * nbp)
    res = _pallas(
        body, name=name, grid=(m // tm, nd),
        in_specs=[pl.BlockSpec((tm, k), lambda i, j: (i, 0)), pl.BlockSpec((None, k, nbp), lambda i, j: (j, 0, 0))],
        out_specs=[out_spec], out_shape=[jax.ShapeDtypeStruct(out_shape, out_dtype)], sem=("parallel", "parallel"), comm=comm,
    )(a, g)
    return res[0] if comm is None else res


def mm_sh_nt(name, a, g, add=None, comm=None):
    nd, k, nbp = g.shape
    slabs = a.ndim == 3
    m = a.shape[1] if slabs else a.shape[0]
    assert a.shape == ((nd, m, nbp) if slabs else (m, nd * nbp)), (name, a.shape, g.shape)
    tm = _pick(m, 512, SUBLANES * 2)
    has_add = add is not None

    def body(a_ref, b_ref, *rest):
        o_ref = rest[-1]
        part = lax.dot_general(a_ref[...].astype(BF16), b_ref[...].astype(BF16), (((1,), (1,)), ((), ())),
                               preferred_element_type=F32)

        @pl.when(pl.program_id(1) == 0)
        def _():
            o_ref[...] = part + rest[0][...] if has_add else part

        @pl.when(pl.program_id(1) != 0)
        def _():
            o_ref[...] += part

    a_spec = pl.BlockSpec((None, tm, nbp), lambda i, j: (j, i, 0)) if slabs else pl.BlockSpec((tm, nbp), lambda i, j: (i, j))
    in_specs = [a_spec, pl.BlockSpec((None, k, nbp), lambda i, j: (j, 0, 0))]
    ops = [a, g]
    if has_add:
        in_specs.append(pl.BlockSpec((tm, k), lambda i, j: (i, 0)))
        ops.append(add)
    res = _pallas(
        body, name=name, grid=(m // tm, nd), in_specs=in_specs,
        out_specs=[pl.BlockSpec((tm, k), lambda i, j: (i, 0))],
        out_shape=[jax.ShapeDtypeStruct((m, k), F32)], sem=("parallel", "arbitrary"), comm=comm,
    )(*ops)
    return res[0] if comm is None else res


def mm_tn(name, a, b):
    m, k = a.shape
    m2, n = b.shape
    assert m == m2, (name, a.shape, b.shape)
    tk = _pick(k, 512, LANES)
    tn = n if m * n * 2 <= RESIDENT_BYTES else _pick(n, 640, LANES)

    def body(a_ref, b_ref, o_ref):
        o_ref[...] = lax.dot_general(a_ref[...].astype(BF16), b_ref[...].astype(BF16), (((0,), (0,)), ((), ())),
                                     preferred_element_type=F32)

    return pl.pallas_call(
        body, name=name, grid=(k // tk, n // tn),
        in_specs=[pl.BlockSpec((m, tk), lambda i, j: (0, i)), pl.BlockSpec((m, tn), lambda i, j: (0, j))],
        out_specs=pl.BlockSpec((tk, tn), lambda i, j: (i, j)),
        out_shape=jax.ShapeDtypeStruct((k, n), F32),
        compiler_params=_params(("parallel", "parallel")),
    )(a, b)


def mm_sh_out(name, a, b, comm=None):
    m, k = a.shape
    slabs = b.ndim == 3
    nbp = b.shape[2] if slabs else b.shape[1] // N_DEV
    assert b.shape == ((N_DEV, m, nbp) if slabs else (m, N_DEV * nbp)), (name, a.shape, b.shape)
    tk = _pick(k, 2048 if k * m * 2 <= RESIDENT_BYTES else 512, LANES)
    b_spec = pl.BlockSpec((None, m, nbp), lambda i, j: (j, 0, 0)) if slabs else pl.BlockSpec((m, nbp), lambda i, j: (0, j))

    def body(a_ref, b_ref, o_ref):
        o_ref[...] = lax.dot_general(a_ref[...].astype(BF16), b_ref[...].astype(BF16), (((0,), (0,)), ((), ())),
                                     preferred_element_type=F32)

    res = _pallas(
        body, name=name, grid=(k // tk, N_DEV),
        in_specs=[pl.BlockSpec((m, tk), lambda i, j: (0, i)), b_spec],
        out_specs=[pl.BlockSpec((None, tk, nbp), lambda i, j: (j, i, 0))],
        out_shape=[jax.ShapeDtypeStruct((N_DEV, k, nbp), F32)], sem=("parallel", "parallel"), comm=comm,
    )(a, b)
    return res[0] if comm is None else res


def pad_cols(y, nb, nbp):
    m = y.shape[0]
    if nb == nbp:
        return y
    return jnp.pad(y.reshape(m, N_DEV, nb), ((0, 0), (0, 0), (0, nbp - nb))).reshape(m, N_DEV * nbp)


def unpad_cols(y, nb, nbp):
    m = y.shape[0]
    if nb == nbp:
        return y
    return y.reshape(m, N_DEV, nbp)[:, :, :nb].reshape(m, N_DEV * nb)


def _in_spec(kind, a, tm):
    if kind == 'row':
        return pl.BlockSpec((tm, a.shape[1]), lambda h, i: (i, 0))
    if kind == 'hrow':
        return pl.BlockSpec((None, tm, a.shape[2]), lambda h, i: (h, i, 0))
    if kind == 'const':
        return pl.BlockSpec(a.shape, lambda h, i: (0, 0))
    assert kind == 'hconst', kind
    return pl.BlockSpec((None,) + a.shape[1:], lambda h, i: (h, 0, 0))


def _row_out(kind, c, dtype, heads, s, tm):
    if kind == 'row':
        assert heads == 1
        return jax.ShapeDtypeStruct((s, c), dtype), pl.BlockSpec((tm, c), lambda h, i: (i, 0))
    return jax.ShapeDtypeStruct((heads, s, c), dtype), pl.BlockSpec((None, tm, c), lambda h, i: (h, i, 0))


def rowwise(name, fn, arrs, kinds, outs, *, heads, s, tm, comm=None):
    n_in = len(arrs)

    def body(*refs):
        vals = fn(*[r[...] for r in refs[:n_in]])
        for o, v in zip(refs[n_in:], vals, strict=True):
            o[...] = v.astype(o.dtype)

    shapes, specs = zip(*[_row_out(k, c, dt, heads, s, tm) for k, c, dt in outs])
    return _pallas(
        body, name=name, grid=(heads, s // tm),
        in_specs=[_in_spec(k, a, tm) for k, a in zip(kinds, arrs, strict=True)],
        out_specs=list(specs), out_shape=list(shapes), sem=("parallel", "parallel"), comm=comm,
    )(*arrs)


def rowwise_vjp(name, fn, arrs, kinds, cots, cot_kinds, wrt, *, heads, s, tm, out_dtypes=None, primal=False, comm=None,
                plus=None):
    n_in, n_cot = len(arrs), len(cots)
    nb = s // tm
    out_dtypes = out_dtypes or [F32] * len(wrt)
    extra = [] if plus is None else [plus]

    def body(*refs):
        vals = [r[...] for r in refs[:n_in]]
        cvals = tuple(r[...].astype(F32) for r in refs[n_in:n_in + n_cot])
        outs = refs[n_in + n_cot + len(extra):]

        def f(*dv):
            full = list(vals)
            for j, i in enumerate(wrt):
                full[i] = dv[j]
            return tuple(fn(*full))

        prim, vjp_fn = jax.vjp(f, *[vals[i].astype(F32) for i in wrt])
        grads = list(vjp_fn(cvals))
        if plus is not None:
            grads[0] = grads[0] + refs[n_in + n_cot][...]
        for o, g in zip(outs[:len(wrt)], grads, strict=True):
            o[...] = g.astype(o.dtype)
        if primal:
            for o, p in zip(outs[len(wrt):], prim, strict=True):
                o[...] = p.astype(o.dtype)

    shapes, specs = [], []
    for i, dt in zip(wrt, out_dtypes, strict=True):
        kind, a = kinds[i], arrs[i]
        if kind in ('row', 'hrow'):
            c = a.shape[-1]
            sh, sp = _row_out('row' if (kind == 'row' and heads == 1) else 'hrow', c, dt, heads, s, tm)
        else:
            r, c = a.shape[-2:]
            sh = jax.ShapeDtypeStruct((heads, nb, r, c), dt)
            sp = pl.BlockSpec((None, None, r, c), lambda h, i: (h, i, 0, 0))
        shapes.append(sh)
        specs.append(sp)
    if primal:
        for ck, c in zip(cot_kinds, cots, strict=True):
            sh, sp = _row_out(ck, c.shape[-1], F32, heads, s, tm)
            shapes.append(sh)
            specs.append(sp)
    in_specs = [_in_spec(k, a, tm) for k, a in zip(kinds, arrs, strict=True)]
    in_specs += [_in_spec(k, a, tm) for k, a in zip(cot_kinds, cots, strict=True)]
    in_specs += [_in_spec('row', a, tm) for a in extra]
    return _pallas(
        body, name=name, grid=(heads, nb), in_specs=in_specs, out_specs=specs, out_shape=shapes,
        sem=("parallel", "parallel"), comm=comm,
    )(*arrs, *cots, *extra)


def colsum(name, x):
    n, m = x.shape
    tc = _pick(m, 32768, LANES) if m % LANES == 0 else m

    def body(x_ref, o_ref):
        acc = x_ref[0:1, :].astype(F32)
        for r in range(1, n):
            acc = acc + x_ref[r:r + 1, :].astype(F32)
        o_ref[...] = acc

    return pl.pallas_call(
        body, name=name, grid=(m // tc,), in_specs=[pl.BlockSpec((n, tc), lambda j: (0, j))],
        out_specs=pl.BlockSpec((1, tc), lambda j: (0, j)), out_shape=jax.ShapeDtypeStruct((1, m), F32),
        compiler_params=_params(("parallel",)),
    )(x)


def headsum(name, x):
    h, s, c = x.shape
    tm = _pick(s, 256, SUBLANES)

    def body(x_ref, o_ref):
        acc = x_ref[0]
        for j in range(1, h):
            acc = acc + x_ref[j]
        o_ref[...] = acc

    return pl.pallas_call(
        body, name=name, grid=(s // tm,), in_specs=[pl.BlockSpec((h, tm, c), lambda i: (0, i, 0))],
        out_specs=pl.BlockSpec((tm, c), lambda i: (i, 0)), out_shape=jax.ShapeDtypeStruct((s, c), F32),
        compiler_params=_params(("parallel",)),
    )(x)


def sum_all(name, x):
    def body(x_ref, o_ref):
        o_ref[...] = jnp.sum(x_ref[...], keepdims=True)

    return pl.pallas_call(body, name=name, out_shape=jax.ShapeDtypeStruct((1, 1), F32))(x)


def sum_partials(name, p, per_head):
    h, nb, r, c = p.shape
    if per_head:
        flat = jnp.transpose(p, (1, 0, 2, 3)).reshape(nb, h * r * c)
        if nb == 1:
            return flat.reshape(h, r, c)
        return colsum(name, flat).reshape(h, r, c)
    flat = p.reshape(h * nb, r * c)
    if h * nb == 1:
        return flat.reshape(r, c)
    return colsum(name, flat).reshape(r, c)


def _rms_fn(x, g):
    xf = x.astype(F32)
    return (xf * lax.rsqrt(jnp.mean(xf * xf, axis=-1, keepdims=True) + NORM_EPS) * g,)


def _softplus(z):
    return jnp.maximum(z, 0.0) + jnp.log(1.0 + jnp.exp(-jnp.abs(z)))


def _rwkv_pre_fn(hk, hw, ha, hg, w0, w2, a0, a2, g2, k_k, k_a):
    zw = w0 + jnp.dot(jnp.tanh(hw), w2, preferred_element_type=F32)
    w_log = -_softplus(-zw) - 0.5
    decay = jnp.exp(-jnp.exp(w_log))
    a = jax.nn.sigmoid(a0 + jnp.dot(ha, a2, preferred_element_type=F32))
    g = jnp.dot(jax.nn.sigmoid(hg), g2, preferred_element_type=F32)
    kk = hk * k_k
    kk = kk * lax.rsqrt(jnp.maximum(jnp.sum(kk * kk, axis=-1, keepdims=True), 1e-24))
    k = hk * (1.0 + (a - 1.0) * k_a)
    return decay, k, -kk, kk * a, g


def _rwkv_pre_grad_fn(hk, hw, ha, hg, w0, w2, a0, a2, g2, k_k, k_a, hr, hv):
    decay, k, a_sc, b_sc, g = _rwkv_pre_fn(hk, hw, ha, hg, w0, w2, a0, a2, g2, k_k, k_a)
    return decay, k, k, a_sc, b_sc, g, hr, hr, hv, hv


def _rwkv_post_fn(y, r, k, v, g, gn_w, gn_b, r_k):
    mu = jnp.mean(y, axis=-1, keepdims=True)
    var = jnp.mean(jnp.square(y - mu), axis=-1, keepdims=True)
    yn = (y - mu) * lax.rsqrt(var + GN_EPS) * gn_w + gn_b
    bonus = jnp.sum(r * k * r_k, axis=-1, keepdims=True) * v
    return ((yn + bonus) * g,)


def _rope_tables(pos, inv_freq2):
    ang = pos * inv_freq2
    return jnp.cos(ang), jnp.sin(ang)


def _rope(t, cos2, sin2, rot):
    return t * cos2 + jnp.dot(t, rot, precision=HIGHEST, preferred_element_type=F32) * sin2


def _mla_pre_fn(c_q, c_kv, k_pe, pos, q_g, kv_g, inv_freq2, rot):
    cos2, sin2 = _rope_tables(pos, inv_freq2)
    return _rms_fn(c_q, q_g)[0], _rms_fn(c_kv, kv_g)[0], _rope(k_pe, cos2, sin2, rot), cos2, sin2


def _mla_pre_grad_fn(c_q, c_kv, k_pe, pos, q_g, kv_g, inv_freq2, rot):
    return _mla_pre_fn(c_q, c_kv, k_pe, pos, q_g, kv_g, inv_freq2, rot)[:3]


def _loss_fn(x2, g, target):
    y = _rms_fn(x2, g)[0]
    return (0.5 * jnp.mean(jnp.square(y - target), axis=-1, keepdims=True),)


def _adamw_fn(w, g, m, v):
    m = ADAM_B1 * m + (1.0 - ADAM_B1) * g
    v = ADAM_B2 * v + (1.0 - ADAM_B2) * jnp.square(g)
    m_hat = m / (1.0 - ADAM_B1 ** ADAM_STEP)
    v_hat = v / (1.0 - ADAM_B2 ** ADAM_STEP)
    delta = -ADAM_LR * (m_hat / (jnp.sqrt(v_hat) + ADAM_EPS) + ADAM_WD * w)
    return delta, m, v


def _prev_halo_spec(c, tm):
    return pl.BlockSpec((SUBLANES, c), lambda i: (jnp.maximum(i * (tm // SUBLANES) - 1, 0), 0))


def _next_halo_spec(c, tm, s):
    return pl.BlockSpec((SUBLANES, c), lambda i: (jnp.minimum((i + 1) * (tm // SUBLANES), s // SUBLANES - 1), 0))


def _shift_down(p, halo, first_block, n):
    out = pltpu.roll(p, n, 0)
    row = lax.broadcasted_iota(jnp.int32, p.shape, 0)
    for j in range(n):
        top = jnp.where(first_block, 0.0, halo[SUBLANES - n + j:SUBLANES - n + j + 1, :])
        out = jnp.where(row == j, top, out)
    return out


def _shift_up(p, halo, last_block, n):
    rows = p.shape[0]
    out = pltpu.roll(p, rows - n, 0)
    row = lax.broadcasted_iota(jnp.int32, p.shape, 0)
    for j in range(n):
        bot = jnp.where(last_block, 0.0, halo[j:j + 1, :])
        out = jnp.where(row == rows - n + j, bot, out)
    return out


def token_shift_fwd(p, mu, tm):
    s, c = p.shape

    def body(p_ref, halo_ref, mu_ref, o_ref):
        pv = p_ref[...]
        prev = _shift_down(pv, halo_ref[...], pl.program_id(0) == 0, 1)
        o_ref[...] = pv + (prev - pv) * mu_ref[...]

    return pl.pallas_call(
        body, name="token_shift_fwd", grid=(s // tm,),
        in_specs=[pl.BlockSpec((tm, c), lambda i: (i, 0)), _prev_halo_spec(c, tm), pl.BlockSpec((1, c), lambda i: (0, 0))],
        out_specs=pl.BlockSpec((tm, c), lambda i: (i, 0)), out_shape=jax.ShapeDtypeStruct((s, c), F32),
        compiler_params=_params(("parallel",)),
    )(p, p, mu)


def token_shift_bwd(p, mu, ds, tm):
    s, c = p.shape
    nb = s // tm

    def body(p_ref, halo_ref, mu_ref, ds_ref, dsn_ref, dp_ref, dmu_ref):
        i = pl.program_id(0)
        pv, dsv, muv = p_ref[...], ds_ref[...], mu_ref[...]
        prev = _shift_down(pv, halo_ref[...], i == 0, 1)
        nxt = _shift_up(dsv, dsn_ref[...], i == nb - 1, 1)
        dp_ref[...] = dsv * (1.0 - muv) + nxt * muv
        dmu_ref[...] = jnp.sum(dsv * (prev - pv), axis=0, keepdims=True)

    return pl.pallas_call(
        body, name="token_shift_bwd", grid=(nb,),
        in_specs=[pl.BlockSpec((tm, c), lambda i: (i, 0)), _prev_halo_spec(c, tm), pl.BlockSpec((1, c), lambda i: (0, 0)),
                  pl.BlockSpec((tm, c), lambda i: (i, 0)), _next_halo_spec(c, tm, s)],
        out_specs=[pl.BlockSpec((tm, c), lambda i: (i, 0)), pl.BlockSpec((None, 1, c), lambda i: (i, 0, 0))],
        out_shape=[jax.ShapeDtypeStruct((s, c), F32), jax.ShapeDtypeStruct((nb, 1, c), F32)],
        compiler_params=_params(("parallel",)),
    )(p, p, mu, ds, ds)


def _ffn_tiles(s, f):
    return _pick(s, 256, SUBLANES), _pick(f, 1408, LANES)


def _conv_gate(gp, halo, first_block, cw, cb):
    p1 = _shift_down(gp, halo, first_block, 1)
    p2 = _shift_down(gp, halo, first_block, 2)
    return cw[0:1, :] * p2 + cw[1:2, :] * p1 + cw[2:3, :] * gp + cb, p1, p2


def ffn_act_fwd(gate_pre, up, conv_w, conv_b):
    s, f = gate_pre.shape
    tm, tc = _ffn_tiles(s, f)

    def body(gp_ref, halo_ref, up_ref, cw_ref, cb_ref, o_ref):
        gate, _, _ = _conv_gate(gp_ref[...], halo_ref[...], pl.program_id(0) == 0, cw_ref[...], cb_ref[...])
        o_ref[...] = (gate * jax.nn.sigmoid(gate) * up_ref[...]).astype(o_ref.dtype)

    blk = pl.BlockSpec((tm, tc), lambda i, j: (i, j))
    return pl.pallas_call(
        body, name="ffn_act_fwd", grid=(s // tm, f // tc),
        in_specs=[blk, pl.BlockSpec((SUBLANES, tc), lambda i, j: (jnp.maximum(i * (tm // SUBLANES) - 1, 0), j)), blk,
                  pl.BlockSpec((CONV_W, tc), lambda i, j: (0, j)), pl.BlockSpec((1, tc), lambda i, j: (0, j))],
        out_specs=blk, out_shape=jax.ShapeDtypeStruct((s, f), BF16),
        compiler_params=_params(("parallel", "parallel")),
    )(gate_pre, gate_pre, up, conv_w, conv_b)


def ffn_act_bwd1(gate_pre, up, conv_w, conv_b, d_act):
    s, f = gate_pre.shape
    tm, tc = _ffn_tiles(s, f)
    nb = s // tm

    def body(gp_ref, halo_ref, up_ref, cw_ref, cb_ref, da_ref, dg_ref, du_ref, dcw_ref, dcb_ref):
        gp = gp_ref[...]
        gate, p1, p2 = _conv_gate(gp, halo_ref[...], pl.program_id(0) == 0, cw_ref[...], cb_ref[...])
        sig = jax.nn.sigmoid(gate)
        da = da_ref[...].astype(F32)
        du_ref[...] = (da * gate * sig).astype(du_ref.dtype)
        dg = da * up_ref[...] * (sig * (1.0 + gate * (1.0 - sig)))
        dg_ref[...] = dg
        dcb_ref[...] = jnp.sum(dg, axis=0, keepdims=True)
        dcw_ref[0:1, :] = jnp.sum(dg * p2, axis=0, keepdims=True)
        dcw_ref[1:2, :] = jnp.sum(dg * p1, axis=0, keepdims=True)
        dcw_ref[2:3, :] = jnp.sum(dg * gp, axis=0, keepdims=True)

    blk = pl.BlockSpec((tm, tc), lambda i, j: (i, j))
    return pl.pallas_call(
        body, name="ffn_act_bwd1", grid=(nb, f // tc),
        in_specs=[blk, pl.BlockSpec((SUBLANES, tc), lambda i, j: (jnp.maximum(i * (tm // SUBLANES) - 1, 0), j)), blk,
                  pl.BlockSpec((CONV_W, tc), lambda i, j: (0, j)), pl.BlockSpec((1, tc), lambda i, j: (0, j)), blk],
        out_specs=[blk, blk, pl.BlockSpec((None, CONV_W, tc), lambda i, j: (i, 0, j)),
                   pl.BlockSpec((None, 1, tc), lambda i, j: (i, 0, j))],
        out_shape=[jax.ShapeDtypeStruct((s, f), F32), jax.ShapeDtypeStruct((s, f), BF16),
                   jax.ShapeDtypeStruct((nb, CONV_W, f), F32), jax.ShapeDtypeStruct((nb, 1, f), F32)],
        compiler_params=_params(("parallel", "parallel")),
    )(gate_pre, gate_pre, up, conv_w, conv_b, d_act)


def ffn_act_bwd2(d_gate, conv_w):
    s, f = d_gate.shape
    tm, tc = _ffn_tiles(s, f)
    nb = s // tm

    def body(dg_ref, halo_ref, cw_ref, o_ref):
        dg, cw = dg_ref[...], cw_ref[...]
        last = pl.program_id(0) == nb - 1
        n1 = _shift_up(dg, halo_ref[...], last, 1)
        n2 = _shift_up(dg, halo_ref[...], last, 2)
        o_ref[...] = (cw[2:3, :] * dg + cw[1:2, :] * n1 + cw[0:1, :] * n2).astype(o_ref.dtype)

    blk = pl.BlockSpec((tm, tc), lambda i, j: (i, j))
    return pl.pallas_call(
        body, name="ffn_act_bwd2", grid=(nb, f // tc),
        in_specs=[blk, pl.BlockSpec((SUBLANES, tc), lambda i, j: (jnp.minimum((i + 1) * (tm // SUBLANES), s // SUBLANES - 1), j)),
                  pl.BlockSpec((CONV_W, tc), lambda i, j: (0, j))],
        out_specs=blk, out_shape=jax.ShapeDtypeStruct((s, f), BF16),
        compiler_params=_params(("parallel", "parallel")),
    )(d_gate, d_gate, conv_w)


def _mxu(x, y, cx, cy):
    if x.ndim == 3:
        return lax.dot_general(x, y, (((cx + 1,), (cy + 1,)), ((0,), (0,))), preferred_element_type=F32)
    return lax.dot_general(x, y, (((cx,), (cy,)), ((), ())), preferred_element_type=F32)


def _split(x):
    hi = x.astype(BF16)
    return hi, (x - hi.astype(F32)).astype(BF16)


def _make_dot3(cx, cy, passes):
    @jax.custom_vjp
    def f(x, y):
        if passes == 1:
            return _mxu(x.astype(BF16), y.astype(BF16), cx, cy)
        xh, xl = _split(x)
        yh, yl = _split(y)
        return _mxu(xh, yh, cx, cy) + (_mxu(xh, yl, cx, cy) + _mxu(xl, yh, cx, cy))

    def fwd(x, y):
        return f(x, y), (x, y)

    def bwd(res, g):
        x, y = res
        dx = dot3(g, y, 1, 1 - cy, passes) if cx == 1 else dot3(y, g, 1 - cy, 1, passes)
        dy = dot3(x, g, 1 - cx, 0, passes) if cy == 0 else dot3(g, x, 0, 1 - cx, passes)
        return dx, dy

    f.defvjp(fwd, bwd)
    return f


_DOT3 = {}


def dot3(x, y, cx, cy, passes=3):
    if (cx, cy, passes) not in _DOT3:
        _DOT3[(cx, cy, passes)] = _make_dot3(cx, cy, passes)
    return _DOT3[(cx, cy, passes)](x, y)


def _dot(x, y, passes=3):
    return dot3(x, y, 1, 0, passes)


def _dot_nt(x, y, passes=3):
    return dot3(x, y, 1, 1, passes)


def _dot_tn(x, y, passes=3):
    return dot3(x, y, 0, 0, passes)


def _tri_sum(x, lower):
    t = x.shape[-2]
    row = lax.broadcasted_iota(jnp.int32, (t, t), 0)
    col = lax.broadcasted_iota(jnp.int32, (t, t), 1)
    tri = jnp.where((col <= row) if lower else (col >= row), 1.0, 0.0).astype(BF16)
    if x.ndim == 3:
        tri = jnp.broadcast_to(tri[None], (x.shape[0], t, t))
    hi = x.astype(BF16)
    rest = x - hi.astype(F32)
    mid = rest.astype(BF16)
    low = (rest - mid.astype(F32)).astype(BF16)
    return _mxu(tri, hi, 1, 0) + (_mxu(tri, mid, 1, 0) + _mxu(tri, low, 1, 0))


@jax.custom_vjp
def _cumsum_rows(x):
    return _tri_sum(x, True)


_cumsum_rows.defvjp(lambda x: (_tri_sum(x, True), None), lambda _, g: (_tri_sum(g, False),))


def _scan_chunk(s0, r, w, k, v, a, b):
    t = r.shape[1]
    row = lax.broadcasted_iota(jnp.int32, (1, t, t), 1)
    col = lax.broadcasted_iota(jnp.int32, (1, t, t), 2)
    strict, incl = col < row, col <= row
    logw = jnp.log(w)
    cum = _cumsum_rows(logw)
    w_in, w_ex, w_inv = jnp.exp(cum), jnp.exp(cum - logw), jnp.exp(-cum)
    w_all = jnp.exp(jnp.sum(logw, axis=1, keepdims=True))
    at, rt, kt, bt = a * w_ex, r * w_in, k * w_inv, b * w_inv
    ps, po = SCAN_PASSES_SOLVE, SCAN_PASSES_OUT
    a_ab = jnp.where(strict, _dot_nt(at, bt, ps), 0.0)
    a_ak = jnp.where(strict, _dot_nt(at, kt, ps), 0.0)
    a_rk = jnp.where(incl, _dot_nt(rt, kt, po), 0.0)
    a_rb = jnp.where(incl, _dot_nt(rt, bt, po), 0.0)
    u = _dot_nt(at, s0, ps) + _dot(a_ak, v, ps)
    p = a_ab
    steps = int(math.log2(t))
    assert 2 ** steps == t
    for j in range(steps):
        u = u + _dot(p, u, ps)
        if j < steps - 1:
            p = _dot(p, p, ps)
    y = _dot_nt(rt, s0, po) + _dot(a_rk, v, po) + _dot(a_rb, u, po)
    s_new = s0 * w_all + _dot_tn(v, kt * w_all, po) + _dot_tn(u, bt * w_all, po)
    return y, s_new


def scan_fwd(r, w, k, v, a, b, comm=None):
    h, s, n = r.shape
    t = min(SCAN_CHUNK, s)
    nc = s // t

    hb = SCAN_HEADS if h % SCAN_HEADS == 0 else 1

    def body(r_ref, w_ref, k_ref, v_ref, a_ref, b_ref, y_ref, ck_ref, st_ref):
        @pl.when(pl.program_id(1) == 0)
        def _():
            st_ref[...] = jnp.zeros_like(st_ref)

        s0 = st_ref[...]
        ck_ref[...] = s0
        y, s_new = _scan_chunk(s0, r_ref[...], w_ref[...], k_ref[...], v_ref[...], a_ref[...], b_ref[...])
        y_ref[...] = y
        st_ref[...] = s_new

    blk = pl.BlockSpec((hb, t, n), lambda hh, c: (hh, c, 0))
    return _pallas(
        body, name="rwkv_scan_fwd", grid=(h // hb, nc), in_specs=[blk] * 6,
        out_specs=[blk, pl.BlockSpec((hb, None, n, n), lambda hh, c: (hh, c, 0, 0))],
        out_shape=[jax.ShapeDtypeStruct((h, s, n), F32), jax.ShapeDtypeStruct((h, nc, n, n), F32)],
        scratch_shapes=[pltpu.VMEM((hb, n, n), F32)], sem=("parallel", "arbitrary"), comm=comm,
    )(r, w, k, v, a, b)


def scan_bwd(r, w, k, v, a, b, ck, dy, comm=None):
    h, s, n = r.shape
    t = min(SCAN_CHUNK, s)
    nc = s // t

    hb = SCAN_HEADS if h % SCAN_HEADS == 0 else 1

    def body(r_ref, w_ref, k_ref, v_ref, a_ref, b_ref, ck_ref, dy_ref, dr_ref, dw_ref, dk_ref, dv_ref, da_ref, db_ref, ds_ref):
        @pl.when(pl.program_id(1) == 0)
        def _():
            ds_ref[...] = jnp.zeros_like(ds_ref)

        _, vjp_fn = jax.vjp(_scan_chunk, ck_ref[...], r_ref[...], w_ref[...], k_ref[...], v_ref[...], a_ref[...], b_ref[...])
        ds0, dr, dw, dk, dv, da, db = vjp_fn((dy_ref[...], ds_ref[...]))
        ds_ref[...] = ds0
        dr_ref[...], dw_ref[...], dk_ref[...], dv_ref[...], da_ref[...], db_ref[...] = dr, dw, dk, dv, da, db

    blk = pl.BlockSpec((hb, t, n), lambda hh, c: (hh, nc - 1 - c, 0))
    return _pallas(
        body, name="rwkv_scan_bwd", grid=(h // hb, nc),
        in_specs=[blk] * 6 + [pl.BlockSpec((hb, None, n, n), lambda hh, c: (hh, nc - 1 - c, 0, 0)), blk],
        out_specs=[blk] * 6, out_shape=[jax.ShapeDtypeStruct((h, s, n), F32)] * 6,
        scratch_shapes=[pltpu.VMEM((hb, n, n), F32)], sem=("parallel", "arbitrary"), comm=comm,
    )(r, w, k, v, a, b, ck, dy)


ATTN_BLOCK = 256
ATTN_LEVELS = 4
ATTN_SLAB = 2 * LANES


def _attn_specs(h, s, tq):
    qblk = lambda c, part: pl.BlockSpec((None, tq, c), lambda hh, i: (hh, i, part))
    kblk = lambda part: pl.BlockSpec((None, s, LANES), lambda hh, i: (hh, 0, part))
    row64 = pl.BlockSpec((tq, QK_ROPE), lambda hh, i: (i, 0))
    return [qblk(LANES, 0), qblk(LANES, 1), kblk(0), kblk(1), pl.BlockSpec((s, QK_ROPE), lambda hh, i: (0, 0)),
            row64, row64, pl.BlockSpec((QK_ROPE, QK_ROPE), lambda hh, i: (0, 0))]


def _attn_levels(s, tq):
    nq = s // tq
    n_lev = min(ATTN_LEVELS, nq)
    per = nq // n_lev
    return [(lv * per, (lv + 1) * per, (lv + 1) * per * tq) for lv in range(n_lev)]


def _attn_scores(qn_b, qp_b, kn_ref, kp_ref, klen, i, tq):
    scale = (QK_NOPE + QK_ROPE) ** -0.5
    kn_b = kn_ref[0:klen, :].astype(BF16)
    kp_b = kp_ref[0:klen, :].astype(BF16)
    sc = lax.dot_general(qn_b, kn_b, (((1,), (1,)), ((), ())), preferred_element_type=F32)
    sc = sc + lax.dot_general(qp_b, kp_b, (((1,), (1,)), ((), ())), preferred_element_type=F32)
    row = i * tq + lax.broadcasted_iota(jnp.int32, sc.shape, 0)
    col = lax.broadcasted_iota(jnp.int32, sc.shape, 1)
    return jnp.where(row >= col, sc * scale, NEG_INF), scale, kn_b, kp_b


def attn_fwd(q_h, kv_h, kp, cos2, sin2, rot, comm=None):
    h, s, _ = q_h.shape
    tq = _pick(s, ATTN_BLOCK, SUBLANES)

    def body(qn_ref, qp_ref, kn_ref, v_ref, kp_ref, cos_ref, sin_ref, rot_ref, o_ref, lse_ref):
        i = pl.program_id(1)
        qn_b = qn_ref[...].astype(BF16)
        qp_b = _rope(qp_ref[:, :QK_ROPE], cos_ref[...], sin_ref[...], rot_ref[...]).astype(BF16)

        def level(klen):
            sc, _, _, _ = _attn_scores(qn_b, qp_b, kn_ref, kp_ref, klen, i, tq)
            mx = jnp.max(sc, axis=-1, keepdims=True)
            e = jnp.exp(sc - mx)
            den = jnp.sum(e, axis=-1, keepdims=True)
            o_ref[...] = jnp.dot((e / den).astype(BF16), v_ref[0:klen, :].astype(BF16), preferred_element_type=F32)
            lse_ref[...] = mx + jnp.log(den)

        for lo, hi, klen in _attn_levels(s, tq):
            pl.when((i >= lo) & (i < hi))(functools.partial(level, klen))

    oblk = lambda c: pl.BlockSpec((None, tq, c), lambda hh, i: (hh, i, 0))
    return _pallas(
        body, name="mla_attn_fwd", grid=(h, s // tq), in_specs=_attn_specs(h, s, tq),
        out_specs=[oblk(V_HEAD), oblk(1)],
        out_shape=[jax.ShapeDtypeStruct((h, s, V_HEAD), F32), jax.ShapeDtypeStruct((h, s, 1), F32)],
        sem=("parallel", "parallel"), comm=comm,
    )(q_h, q_h, kv_h, kv_h, kp, cos2, sin2, rot)


def attn_bwd(q_h, kv_h, kp, cos2, sin2, rot, o, lse, do, comm=None):
    h, s, _ = q_h.shape
    tq = _pick(s, ATTN_BLOCK, SUBLANES)
    nq = s // tq

    def body(qn_ref, qp_ref, kn_ref, v_ref, kp_ref, cos_ref, sin_ref, rot_ref, o_ref, lse_ref, do_ref,
             dq_ref, dkv_ref, dkp_ref, dkv_acc, dkp_acc):
        i = pl.program_id(1)

        @pl.when(i == 0)
        def _():
            dkv_acc[...] = jnp.zeros_like(dkv_acc)
            dkp_acc[...] = jnp.zeros_like(dkp_acc)

        cosv, sinv, rotv = cos_ref[...], sin_ref[...], rot_ref[...]
        qn_b = qn_ref[...].astype(BF16)
        qp_b = _rope(qp_ref[:, :QK_ROPE], cosv, sinv, rotv).astype(BF16)
        dov = do_ref[...]
        do_b = dov.astype(BF16)
        delta = jnp.sum(dov * o_ref[...], axis=-1, keepdims=True)
        lsev = lse_ref[...]

        def level(klen):
            sc, scale, kn_b, kp_b = _attn_scores(qn_b, qp_b, kn_ref, kp_ref, klen, i, tq)
            p = jnp.exp(sc - lsev)
            dp = lax.dot_general(do_b, v_ref[0:klen, :].astype(BF16), (((1,), (1,)), ((), ())), preferred_element_type=F32)
            ds = (p * (dp - delta) * scale).astype(BF16)
            dkv_acc[0:klen, :LANES] += lax.dot_general(ds, qn_b, (((0,), (0,)), ((), ())), preferred_element_type=F32)
            dkv_acc[0:klen, LANES:] += lax.dot_general(p.astype(BF16), do_b, (((0,), (0,)), ((), ())), preferred_element_type=F32)
            dkp_acc[0:klen, :] += lax.dot_general(ds, qp_b, (((0,), (0,)), ((), ())), preferred_element_type=F32)
            dqp = jnp.dot(ds, kp_b, preferred_element_type=F32)
            dqp_raw = dqp * cosv + lax.dot_general(dqp * sinv, rotv, (((1,), (1,)), ((), ())), precision=HIGHEST,
                                                   preferred_element_type=F32)
            dq_ref[:, :QK_NOPE] = jnp.dot(ds, kn_b, preferred_element_type=F32).astype(dq_ref.dtype)
            dq_ref[:, QK_NOPE:QK_NOPE + QK_ROPE] = dqp_raw.astype(dq_ref.dtype)
            dq_ref[:, QK_NOPE + QK_ROPE:] = jnp.zeros((tq, ATTN_SLAB - QK_NOPE - QK_ROPE), dq_ref.dtype)

        for lo, hi, klen in _attn_levels(s, tq):
            pl.when((i >= lo) & (i < hi))(functools.partial(level, klen))

        @pl.when(i == nq - 1)
        def _():
            dkv_ref[...] = dkv_acc[...].astype(dkv_ref.dtype)
            dkp_ref[...] = dkp_acc[...]

    rblk = lambda c: pl.BlockSpec((None, tq, c), lambda hh, i: (hh, i, 0))
    sblk = lambda c: pl.BlockSpec((None, s, c), lambda hh, i: (hh, 0, 0))
    return _pallas(
        body, name="mla_attn_bwd", grid=(h, nq),
        in_specs=_attn_specs(h, s, tq) + [rblk(V_HEAD), rblk(1), rblk(V_HEAD)],
        out_specs=[rblk(ATTN_SLAB), sblk(ATTN_SLAB), sblk(QK_ROPE)],
        out_shape=[jax.ShapeDtypeStruct((h, s, ATTN_SLAB), BF16), jax.ShapeDtypeStruct((h, s, ATTN_SLAB), BF16),
                   jax.ShapeDtypeStruct((h, s, QK_ROPE), F32)],
        scratch_shapes=[pltpu.VMEM((s, ATTN_SLAB), F32), pltpu.VMEM((s, QK_ROPE), F32)],
        sem=("parallel", "arbitrary"), comm=comm,
    )(q_h, q_h, kv_h, kv_h, kp, cos2, sin2, rot, o, lse, do)


def _my_pos():
    return lax.axis_index("x"), lax.axis_index("y"), lax.axis_index("c")


def _dev_index(px, py, pc):
    return 4 * px + 2 * py + pc


def all_gather(name, shard):
    r, c = shard.shape

    def body(x_ref, out_ref, send_sems, recv_sems, local_sem):
        x, y, cc = _my_pos()
        me, sibling = (x, y, cc), (x, y, 1 - cc)
        chips = [(1 - x, y), (x, 1 - y), (1 - x, 1 - y)]

        def rows(px, py, pc):
            return out_ref.at[_dev_index(px, py, pc)]

        def copy(kk, block, to, src=None):
            return pltpu.make_async_remote_copy(
                src_ref=rows(*block) if src is None else src, dst_ref=rows(*block),
                send_sem=send_sems.at[kk], recv_sem=recv_sems.at[kk], device_id=to, device_id_type=MESH)

        mine = pltpu.make_async_copy(x_ref, rows(*me), local_sem)
        mine.start()
        first = [copy(0, me, sibling, src=x_ref)]
        first += [copy(1 + j, me, (*chip, cc), src=x_ref) for j, chip in enumerate(chips)]
        for cp in first:
            cp.start()
        passed = [copy(4 + j, (*chip, cc), sibling) for j, chip in enumerate(chips)]
        for j, chip in enumerate(chips):
            copy(1 + j, (*chip, cc), me).wait_recv()
            passed[j].start()
        copy(0, sibling, me).wait_recv()
        for j, chip in enumerate(chips):
            copy(4 + j, (*chip, 1 - cc), me).wait_recv()
        for cp in first + passed:
            cp.wait_send()
        mine.wait()

    return pl.pallas_call(
        body, name=name, out_shape=jax.ShapeDtypeStruct((N_DEV, r, c), shard.dtype),
        in_specs=[pl.BlockSpec(memory_space=pl.ANY)], out_specs=pl.BlockSpec(memory_space=pl.ANY),
        scratch_shapes=[pltpu.SemaphoreType.DMA((7,)), pltpu.SemaphoreType.DMA((7,)), pltpu.SemaphoreType.DMA],
    )(shard)


def exchange_sibling(name, g):
    _, r, c = g.shape

    def body(g_ref, out_ref, send_sems, recv_sems):
        x, y, cc = _my_pos()
        copies = []
        for px in range(2):
            for py in range(2):
                slot = 2 * px + py
                copies.append(pltpu.make_async_remote_copy(
                    src_ref=g_ref.at[_dev_index(px, py, 1 - cc)], dst_ref=out_ref.at[slot],
                    send_sem=send_sems.at[slot], recv_sem=recv_sems.at[slot], device_id=(x, y, 1 - cc), device_id_type=MESH))
        for cp in copies:
            cp.start()
        for cp in copies:
            cp.wait()

    return pl.pallas_call(
        body, name=name, out_shape=jax.ShapeDtypeStruct((4, r, c), g.dtype),
        in_specs=[pl.BlockSpec(memory_space=pl.ANY)], out_specs=pl.BlockSpec(memory_space=pl.ANY),
        scratch_shapes=[pltpu.SemaphoreType.DMA((4,)), pltpu.SemaphoreType.DMA((4,))],
    )(g)


def exchange_chips(name, hsum):
    _, r, c = hsum.shape

    def body(h_ref, out_ref, send_sems, recv_sems):
        x, y, cc = _my_pos()
        copies = []
        for j, (px, py) in enumerate([(1 - x, y), (x, 1 - y), (1 - x, 1 - y)]):
            copies.append(pltpu.make_async_remote_copy(
                src_ref=h_ref.at[2 * px + py], dst_ref=out_ref.at[j],
                send_sem=send_sems.at[j], recv_sem=recv_sems.at[j], device_id=(px, py, cc), device_id_type=MESH))
        for cp in copies:
            cp.start()
        for cp in copies:
            cp.wait()

    return pl.pallas_call(
        body, name=name, out_shape=jax.ShapeDtypeStruct((3, r, c), hsum.dtype),
        in_specs=[pl.BlockSpec(memory_space=pl.ANY)], out_specs=pl.BlockSpec(memory_space=pl.ANY),
        scratch_shapes=[pltpu.SemaphoreType.DMA((3,)), pltpu.SemaphoreType.DMA((3,))],
    )(hsum)


def _rs_add_sibling(name, g, from_sibling, cc):
    _, r, c = g.shape
    tr = _pick(r, 512, SUBLANES * 2)

    def body(cc_ref, g_ref, s_ref, o_ref, ob_ref):
        tot = g_ref[...] + s_ref[...]
        o_ref[...] = tot
        ob_ref[...] = tot.astype(BF16)

    blk = pl.BlockSpec((None, tr, c), lambda s_, i, cc_ref: (s_, i, 0))
    return pl.pallas_call(
        body, name=name,
        grid_spec=pltpu.PrefetchScalarGridSpec(
            num_scalar_prefetch=1, grid=(4, r // tr),
            in_specs=[pl.BlockSpec((None, None, tr, c), lambda s_, i, cc_ref: (s_, cc_ref[0], i, 0)), blk], out_specs=[blk, blk]),
        out_shape=[jax.ShapeDtypeStruct((4, r, c), F32), jax.ShapeDtypeStruct((4, r, c), BF16)],
        compiler_params=_params(("parallel", "parallel")),
    )(cc.reshape(1).astype(jnp.int32), g.reshape(4, 2, r, c), from_sibling)


def _rs_add_chips(name, chip_sum, from_chips, slot):
    _, r, c = chip_sum.shape
    tr = _pick(r, 512, SUBLANES * 2)

    def body(slot_ref, h_ref, f0_ref, f1_ref, f2_ref, o_ref):
        o_ref[...] = ((h_ref[...] + f0_ref[...].astype(F32)) + f1_ref[...].astype(F32)) + f2_ref[...].astype(F32)

    def from_blk(j):
        return pl.BlockSpec((None, tr, c), lambda i, slot_ref: (j, i, 0))

    return pl.pallas_call(
        body, name=name,
        grid_spec=pltpu.PrefetchScalarGridSpec(
            num_scalar_prefetch=1, grid=(r // tr,),
            in_specs=[pl.BlockSpec((None, tr, c), lambda i, slot_ref: (slot_ref[0], i, 0)), from_blk(0), from_blk(1), from_blk(2)],
            out_specs=pl.BlockSpec((tr, c), lambda i, slot_ref: (i, 0))),
        out_shape=jax.ShapeDtypeStruct((r, c), F32), compiler_params=_params(("parallel",)),
    )(slot.reshape(1).astype(jnp.int32), chip_sum, from_chips, from_chips, from_chips)


def adamw_scatter(name, w, m, v, chip_sum, from_chips, slot):
    _, a, b = w.shape
    _, r, c = chip_sum.shape
    assert a <= r and b <= c, (name, w.shape, chip_sum.shape)
    tr = _pick(a, 256, SUBLANES * 2)

    def body(slot_ref, h_ref, f0_ref, f1_ref, f2_ref, w_ref, m_ref, v_ref, g_out, d_out, m_out, v_out):
        g = ((h_ref[...] + f0_ref[...].astype(F32)) + f1_ref[...].astype(F32)) + f2_ref[...].astype(F32)
        g = g[:, :b]
        g_out[...] = g
        d_out[...], m_out[...], v_out[...] = _adamw_fn(w_ref[...], g, m_ref[...], v_ref[...])

    def from_blk(j):
        return pl.BlockSpec((None, tr, c), lambda i, slot_ref: (j, i, 0))

    mine = pl.BlockSpec((None, tr, b), lambda i, slot_ref: (0, i, 0))
    return pl.pallas_call(
        body, name=name,
        grid_spec=pltpu.PrefetchScalarGridSpec(
            num_scalar_prefetch=1, grid=(a // tr,),
            in_specs=[pl.BlockSpec((None, tr, c), lambda i, slot_ref: (slot_ref[0], i, 0)), from_blk(0), from_blk(1), from_blk(2),
                      mine, mine, mine],
            out_specs=[mine] * 4),
        out_shape=[jax.ShapeDtypeStruct((1, a, b), F32)] * 4, compiler_params=_params(("parallel",)),
    )(slot.reshape(1).astype(jnp.int32), chip_sum, from_chips, from_chips, from_chips, w, m, v)


def rs_chip_sum(tag, g):
    _, _, cc = _my_pos()
    from_sibling = exchange_sibling("rs_sibling_" + tag, g)
    return _rs_add_sibling("rs_add_sibling_" + tag, g, from_sibling, cc)


def rs_finish(tag, chip_sum, from_chips):
    x, y, _ = _my_pos()
    return _rs_add_chips("rs_add_chips_" + tag, chip_sum, from_chips, 2 * x + y)


def reduce_scatter(tag, g):
    chip_sum, chip_sum_b = rs_chip_sum(tag, g)
    return rs_finish(tag, chip_sum, exchange_chips("rs_chips_" + tag, chip_sum_b))


class GatherIci:
    def __init__(self, shards):
        self.inputs = list(shards)
        self.out_shapes = [jax.ShapeDtypeStruct((N_DEV,) + s.shape, s.dtype) for s in shards]
        self.n_remote, self.n_local = 3 * len(shards), len(shards)

    def make(self, cins, couts, send, recv, local):
        x, y, cc = _my_pos()
        me = _dev_index(x, y, cc)
        copies = []
        for w, (src, out) in enumerate(zip(cins, couts, strict=True)):
            copies.append(pltpu.make_async_copy(src, out.at[me], local.at[w]))
            for j, (px, py) in enumerate([(1 - x, y), (x, 1 - y), (1 - x, 1 - y)]):
                copies.append(pltpu.make_async_remote_copy(
                    src_ref=src, dst_ref=out.at[me], send_sem=send.at[3 * w + j], recv_sem=recv.at[3 * w + j],
                    device_id=(px, py, cc), device_id_type=MESH))
        return copies


class RsChips:
    def __init__(self, chip_sums):
        self.inputs = list(chip_sums)
        self.out_shapes = [jax.ShapeDtypeStruct((3,) + h.shape[1:], h.dtype) for h in chip_sums]
        self.n_remote, self.n_local = 3 * len(chip_sums), 0

    def make(self, cins, couts, send, recv, local):
        x, y, cc = _my_pos()
        copies = []
        for w, (h_ref, out) in enumerate(zip(cins, couts, strict=True)):
            for j, (px, py) in enumerate([(1 - x, y), (x, 1 - y), (1 - x, 1 - y)]):
                copies.append(pltpu.make_async_remote_copy(
                    src_ref=h_ref.at[2 * px + py], dst_ref=out.at[j], send_sem=send.at[3 * w + j], recv_sem=recv.at[3 * w + j],
                    device_id=(px, py, cc), device_id_type=MESH))
        return copies


class SiblingSwap:
    def __init__(self, gs):
        self.inputs = list(gs)
        self.out_shapes = [jax.ShapeDtypeStruct((4,) + g.shape[1:], g.dtype) for g in gs]
        self.n_remote, self.n_local = 4 * len(gs), 0

    def make(self, cins, couts, send, recv, local):
        x, y, cc = _my_pos()
        copies = []
        for w, (g_ref, out) in enumerate(zip(cins, couts, strict=True)):
            for px in range(2):
                for py in range(2):
                    q = 4 * w + 2 * px + py
                    copies.append(pltpu.make_async_remote_copy(
                        src_ref=g_ref.at[_dev_index(px, py, 1 - cc)], dst_ref=out.at[2 * px + py],
                        send_sem=send.at[q], recv_sem=recv.at[q], device_id=(x, y, 1 - cc), device_id_type=MESH))
        return copies


class _SemSlice:
    def __init__(self, base, start):
        self.base, self.start = base, start

    @property
    def at(self):
        return self

    def __getitem__(self, k):
        return self.base.at[self.start + k]


class CommGroup:
    def __init__(self, plans):
        self.plans = [p for p in plans if p.inputs]
        self.inputs = [a for p in self.plans for a in p.inputs]
        self.out_shapes = [s_ for p in self.plans for s_ in p.out_shapes]
        self.n_remote = sum(p.n_remote for p in self.plans)
        self.n_local = sum(p.n_local for p in self.plans)

    def make(self, cins, couts, send, recv, local):
        copies, i0, o0, r0, l0 = [], 0, 0, 0, 0
        for p in self.plans:
            ni, no = len(p.inputs), len(p.out_shapes)
            copies += p.make(cins[i0:i0 + ni], couts[o0:o0 + no], _SemSlice(send, r0), _SemSlice(recv, r0), _SemSlice(local, l0))
            i0, o0, r0, l0 = i0 + ni, o0 + no, r0 + p.n_remote, l0 + p.n_local
        return copies


def gather_d2d(name, arrays):
    n = len(arrays)

    def body(*refs):
        outs, send, recv = refs[n:2 * n], refs[2 * n], refs[2 * n + 1]
        x, y, cc = _my_pos()
        copies = []
        for w, out in enumerate(outs):
            for px in range(2):
                for py in range(2):
                    q = 4 * w + 2 * px + py
                    slab = out.at[_dev_index(px, py, cc)]
                    copies.append(pltpu.make_async_remote_copy(
                        src_ref=slab, dst_ref=slab, send_sem=send.at[q], recv_sem=recv.at[q],
                        device_id=(x, y, 1 - cc), device_id_type=MESH))
        for cp in copies:
            cp.start()
        for cp in copies:
            cp.wait()

    any_spec = pl.BlockSpec(memory_space=pl.ANY)
    return pl.pallas_call(
        body, name=name, out_shape=[jax.ShapeDtypeStruct(a.shape, a.dtype) for a in arrays],
        in_specs=[any_spec] * n, out_specs=[any_spec] * n, input_output_aliases={i: i for i in range(n)},
        scratch_shapes=[pltpu.SemaphoreType.DMA((4 * n,)), pltpu.SemaphoreType.DMA((4 * n,))],
    )(*arrays)


def _pallas(body, *, name, grid, in_specs, out_specs, out_shape, scratch_shapes=(), sem, comm=None):
    in_specs, out_specs, out_shape, scratch_shapes = list(in_specs), list(out_specs), list(out_shape), list(scratch_shapes)
    if comm is None:
        return pl.pallas_call(body, name=name, grid=grid, in_specs=in_specs, out_specs=out_specs, out_shape=out_shape,
                              scratch_shapes=scratch_shapes, compiler_params=_params(sem))
    n_in, n_out, n_scr = len(in_specs), len(out_specs), len(scratch_shapes)
    nci, nco = len(comm.inputs), len(comm.out_shapes)

    def body2(*refs):
        ins, cins = refs[:n_in], refs[n_in:n_in + nci]
        o0 = n_in + nci
        outs, couts = refs[o0:o0 + n_out], refs[o0 + n_out:o0 + n_out + nco]
        s0 = o0 + n_out + nco
        scr = refs[s0:s0 + n_scr]
        send, recv, local = refs[s0 + n_scr:]
        pids = [pl.program_id(k) for k in range(len(grid))]
        first = functools.reduce(jnp.logical_and, [p == 0 for p in pids])
        last = functools.reduce(jnp.logical_and, [p == g - 1 for p, g in zip(pids, grid)])

        @pl.when(first)
        def _():
            for cp in comm.make(cins, couts, send, recv, local):
                cp.start()

        body(*ins, *outs, *scr)

        @pl.when(last)
        def _():
            for cp in comm.make(cins, couts, send, recv, local):
                cp.wait()

    any_spec = pl.BlockSpec(memory_space=pl.ANY)
    call = pl.pallas_call(
        body2, name=name, grid=grid, in_specs=in_specs + [any_spec] * nci, out_specs=out_specs + [any_spec] * nco,
        out_shape=out_shape + list(comm.out_shapes),
        scratch_shapes=scratch_shapes + [pltpu.SemaphoreType.DMA((comm.n_remote,)), pltpu.SemaphoreType.DMA((comm.n_remote,)),
                                         pltpu.SemaphoreType.DMA((max(comm.n_local, 1),))],
        compiler_params=_params(tuple("arbitrary" for _ in grid)))
    return lambda *args: call(*args, *comm.inputs)


PACK_W = 1024


class Pack:
    def __init__(self, entries, row_unit):
        self.entries = entries
        self.sizes = [int(np.prod(sh)) for _, sh in entries]
        self.offsets = np.concatenate([[0], np.cumsum(self.sizes)]).tolist()
        self.total = _round_up(self.offsets[-1], PACK_W * row_unit)
        self.rows = self.total // PACK_W

    def pack(self, arrays, dtype, lead=()):
        flat = [arrays[n].astype(dtype).reshape(lead + (-1,)) for n, _ in self.entries]
        pad = self.total - self.offsets[-1]
        if pad:
            flat.append(jnp.zeros(lead + (pad,), dtype))
        return jnp.concatenate(flat, axis=-1).reshape(lead + (self.rows, PACK_W))

    def unpack(self, buf, lead=()):
        flat = buf.reshape(lead + (self.total,))
        out = {}
        for (n, sh), off, sz in zip(self.entries, self.offsets, self.sizes):
            out[n] = lax.slice_in_dim(flat, off, off + sz, axis=len(lead)).reshape(lead + tuple(sh))
        return out


def _gathered_to_full(g, how):
    _, a, b = g.shape
    if how == 'row':
        return g.reshape(N_DEV * a, b)
    return jnp.transpose(g, (1, 0, 2)).reshape(a, N_DEV * b)


def _full_to_shards(w, how):
    a, b = w.shape
    if how == 'row':
        return w.reshape(N_DEV, a // N_DEV, b)
    return jnp.transpose(w.reshape(a, N_DEV, b // N_DEV), (1, 0, 2))


def _to_heads(t, width):
    s, c = t.shape
    return jnp.transpose(t.reshape(s, c // width, width), (1, 0, 2))


def _from_heads(t):
    h, s, w = t.shape
    return jnp.transpose(t, (1, 0, 2)).reshape(s, h * w)


def _rot_matrix():
    half = QK_ROPE // 2
    rot = np.zeros((QK_ROPE, QK_ROPE), np.float32)
    for i in range(half):
        rot[i + half, i] = -1.0
        rot[i, i + half] = 1.0
    return jnp.asarray(rot)


def _inv_freq2():
    half = QK_ROPE // 2
    inv = ROPE_THETA ** (-np.arange(half, dtype=np.float32) / half)
    return jnp.asarray(np.concatenate([inv, inv])[None, :].astype(np.float32))


def kernel(x, positions, attn_norm_g, w_in, rwkv_mu, rwkv_w0, rwkv_w2, rwkv_a0, rwkv_a2, rwkv_g2, rwkv_k_k, rwkv_k_a, rwkv_r_k, rwkv_gn_w, rwkv_gn_b, mla_q_norm_g, mla_w_uq, mla_kv_norm_g, mla_w_ukv, w_out, ffn_norm_g, ffn_w_gate, ffn_w_up, ffn_conv_w, ffn_conv_b, ffn_w_down, final_norm_g, loss_target, m_attn_norm_g, m_w_in, m_rwkv_mu, m_rwkv_w0, m_rwkv_w2, m_rwkv_a0, m_rwkv_a2, m_rwkv_g2, m_rwkv_k_k, m_rwkv_k_a, m_rwkv_r_k, m_rwkv_gn_w, m_rwkv_gn_b, m_mla_q_norm_g, m_mla_w_uq, m_mla_kv_norm_g, m_mla_w_ukv, m_w_out, m_ffn_norm_g, m_ffn_w_gate, m_ffn_w_up, m_ffn_conv_w, m_ffn_conv_b, m_ffn_w_down, m_final_norm_g, v_attn_norm_g, v_w_in, v_rwkv_mu, v_rwkv_w0, v_rwkv_w2, v_rwkv_a0, v_rwkv_a2, v_rwkv_g2, v_rwkv_k_k, v_rwkv_k_a, v_rwkv_r_k, v_rwkv_gn_w, v_rwkv_gn_b, v_mla_q_norm_g, v_mla_w_uq, v_mla_kv_norm_g, v_mla_w_ukv, v_w_out, v_ffn_norm_g, v_ffn_w_gate, v_ffn_w_up, v_ffn_conv_w, v_ffn_conv_b, v_ffn_w_down, v_final_norm_g):
    given = dict(locals())
    wts = {n: given[n] for n in WEIGHTS}
    mom_m = {n: given["m_" + n] for n in WEIGHTS}
    mom_v = {n: given["v_" + n] for n in WEIGHTS}
    out_shapes = {n: wts[n].shape for n in WEIGHTS}

    def local2d(n, a):
        if n == 'rwkv_r_k' or a.ndim <= 2:
            return a.reshape(1, -1)
        return a.reshape(a.shape[1:])

    w2d = {n: local2d(n, wts[n]) for n in WEIGHTS}
    m2d = {n: local2d(n, mom_m[n]) for n in WEIGHTS}
    v2d = {n: local2d(n, mom_v[n]) for n in WEIGHTS}

    xs = x.reshape(x.shape[1:])
    tgt = loss_target.reshape(loss_target.shape[1:])
    s, d = xs.shape
    c_rwkv = w2d['rwkv_w0'].shape[1]
    n_rh = c_rwkv // RWKV_HEAD
    decay_lora, aaa_lora, gate_lora = w2d['rwkv_w2'].shape[0], w2d['rwkv_a2'].shape[0], w2d['rwkv_g2'].shape[0]
    q_lora, kv_lora = w2d['mla_q_norm_g'].shape[1], w2d['mla_kv_norm_g'].shape[1]
    shift_dim = w2d['rwkv_mu'].shape[1]
    d_in = w2d['w_in'].shape[1] * N_DEV
    n_mh = w2d['mla_w_uq'].shape[1] * N_DEV // (QK_NOPE + QK_ROPE)
    tm = _pick(s, 256, SUBLANES)
    tm_wide = _pick(s, 128, SUBLANES)
    tm_heads = _pick(s, 512, SUBLANES)

    nb = {n: w2d[n].shape[1] for n in BIG if BIG[n] == 'col'}
    nbp = {n: _round_up(v_, LANES) for n, v_ in nb.items()}
    shards = {}
    for n in BIG:
        w = w2d[n].astype(BF16)
        if BIG[n] == 'col':
            w = jnp.pad(w, ((0, 0), (0, nbp[n] - nb[n])))
        elif n == 'ffn_w_down':
            w = jnp.pad(w, ((0, nbp['ffn_w_gate'] - w.shape[0]), (0, 0)))
        shards[n] = w

    def as_used(n, g):
        return g if BIG[n] == 'col' else g.reshape(N_DEV * g.shape[1], g.shape[2])

    gathered = {'w_in': as_used('w_in', all_gather("gather_w_in", shards['w_in']))}
    f_pad = N_DEV * nbp['ffn_w_gate']
    small_pack = Pack([(n, w2d[n].shape) for n in SMALL_SHARDED], 8)
    small_all = all_gather("gather_small", small_pack.pack(w2d, F32))
    full = {}
    for n, g in small_pack.unpack(small_all, lead=(N_DEV,)).items():
        full[n] = _gathered_to_full(g, SMALL_SHARDED[n])
    conv_w_pad = pad_cols(full['ffn_conv_w'], nb['ffn_w_gate'], nbp['ffn_w_gate'])
    conv_b_pad = pad_cols(w2d['ffn_conv_b'], nb['ffn_w_gate'], nbp['ffn_w_gate'])

    (h1,) = rowwise("rms_attn", _rms_fn, [xs, w2d['attn_norm_g']], ['row', 'const'], [('row', d, BF16)], heads=1, s=s, tm=tm)
    def gather_behind(names, run):
        *res, = run(GatherIci([shards[n] for n in names]))
        landed = res[len(res) - len(names):]
        for n, g in zip(names, gather_d2d("gather_d2d_" + names[0], landed), strict=True):
            gathered[n] = as_used(n, g)
        return res[:len(res) - len(names)]

    (proj_pad,) = gather_behind(['mla_w_uq', 'mla_w_ukv', 'w_out'], lambda c: mm_sh("proj_in", h1, gathered['w_in'], comm=c))
    proj = unpad_cols(proj_pad, nb['w_in'], nbp['w_in'])
    p_rwkv = proj[:, :shift_dim]
    c_q = proj[:, shift_dim:shift_dim + q_lora]
    c_kv = proj[:, shift_dim + q_lora:shift_dim + q_lora + kv_lora]
    k_pe = proj[:, shift_dim + q_lora + kv_lora:d_in]
    shifted = token_shift_fwd(p_rwkv, w2d['rwkv_mu'], tm_wide)
    o1, o2, o3 = c_rwkv, 2 * c_rwkv, 3 * c_rwkv
    hr = _to_heads(shifted[:, :o1], RWKV_HEAD)
    hk = _to_heads(shifted[:, o1:o2], RWKV_HEAD)
    hv = _to_heads(shifted[:, o2:o3], RWKV_HEAD)
    hw = shifted[:, o3:o3 + decay_lora]
    ha = shifted[:, o3 + decay_lora:o3 + decay_lora + aaa_lora]
    hg = shifted[:, o3 + decay_lora + aaa_lora:]

    def per_head(vec):
        return vec.reshape(n_rh, 1, RWKV_HEAD)

    def lora_heads(w):
        return jnp.transpose(w.reshape(w.shape[0], n_rh, RWKV_HEAD), (1, 0, 2))

    pre_args = [hk, hw, ha, hg, per_head(w2d['rwkv_w0']), lora_heads(full['rwkv_w2']), per_head(w2d['rwkv_a0']),
                lora_heads(full['rwkv_a2']), lora_heads(full['rwkv_g2']), per_head(w2d['rwkv_k_k']), per_head(w2d['rwkv_k_a'])]
    pre_kinds = ['hrow', 'row', 'row', 'row', 'hconst', 'hconst', 'hconst', 'hconst', 'hconst', 'hconst', 'hconst']
    decay, kx, a_sc, b_sc, gate_r = gather_behind(['ffn_w_gate'], lambda c: rowwise(
        "rwkv_pre", _rwkv_pre_fn, pre_args, pre_kinds, [('hrow', RWKV_HEAD, F32)] * 5, heads=n_rh, s=s, tm=tm_heads, comm=c))
    y_scan, ckpt = gather_behind(['ffn_w_up'], lambda c: scan_fwd(hr, decay, kx, hv, a_sc, b_sc, comm=c))
    post_args = [y_scan, hr, kx, hv, gate_r, per_head(w2d['rwkv_gn_w']), per_head(w2d['rwkv_gn_b']), per_head(w2d['rwkv_r_k'])]
    post_kinds = ['hrow'] * 5 + ['hconst'] * 3
    (y_rwkv_h,) = rowwise("rwkv_post", _rwkv_post_fn, post_args, post_kinds, [('hrow', RWKV_HEAD, F32)], heads=n_rh, s=s, tm=tm_heads)

    pos = positions.reshape(s, 1).astype(F32)
    rot, inv2 = _rot_matrix(), _inv_freq2()
    mla_args = [c_q, c_kv, k_pe, pos, w2d['mla_q_norm_g'], w2d['mla_kv_norm_g'], inv2, rot]
    mla_kinds = ['row', 'row', 'row', 'row', 'const', 'const', 'const', 'const']
    qn, kvn, kp_rot, cos2, sin2 = rowwise(
        "mla_pre", _mla_pre_fn, mla_args, mla_kinds,
        [('row', q_lora, BF16), ('row', kv_lora, BF16), ('row', QK_ROPE, F32), ('row', QK_ROPE, F32), ('row', QK_ROPE, F32)],
        heads=1, s=s, tm=tm)
    assert n_mh == N_DEV and nb['mla_w_uq'] == QK_NOPE + QK_ROPE and nb['mla_w_ukv'] == QK_NOPE + V_HEAD
    assert nbp['mla_w_uq'] == ATTN_SLAB and nbp['mla_w_ukv'] == ATTN_SLAB
    q_h = mm_sh("proj_q", qn, gathered['mla_w_uq'], slabs=True)
    kv_h = mm_sh("proj_kv", kvn, gathered['mla_w_ukv'], slabs=True)
    o_att, lse = gather_behind(['ffn_w_down'], lambda c: attn_fwd(q_h, kv_h, kp_rot, cos2, sin2, rot, comm=c))
    ycat = jnp.concatenate([_from_heads(y_rwkv_h), _from_heads(o_att)], axis=-1).astype(BF16)
    x1 = mm("proj_out", ycat, gathered['w_out'], add=xs)
    (h2,) = rowwise("rms_ffn", _rms_fn, [x1, w2d['ffn_norm_g']], ['row', 'const'], [('row', d, BF16)], heads=1, s=s, tm=tm)
    gate_pre = mm_sh("ffn_gate", h2, gathered['ffn_w_gate'])
    up = mm_sh("ffn_up", h2, gathered['ffn_w_up'])
    act = ffn_act_fwd(gate_pre, up, conv_w_pad, conv_b_pad)
    x2 = mm("ffn_down", act, gathered['ffn_w_down'], add=x1)

    ones = jnp.ones((s, 1), F32)
    fin_g = w2d['final_norm_g']
    d_x2, dg_final_p, loss_rows = rowwise_vjp("loss_bwd", _loss_fn, [x2, fin_g, tgt], ['row', 'const', 'row'], [ones], ['row'],
                                              [0, 1], heads=1, s=s, tm=tm, primal=True)
    d_x2_b = d_x2.astype(BF16)
    d_act = mm_nt("d_act", d_x2_b, gathered['ffn_w_down'], out_dtype=BF16)
    gsh, chip_sums, from_chips = {}, {}, {}

    def scatter_behind(run, ici=(), swap=()):
        *res, = run(CommGroup([RsChips([chip_sums[n][1] for n in ici]), SiblingSwap([gsh[n] for n in swap])]))
        n_own = len(res) - len(ici) - len(swap)
        from_chips.update(zip(ici, res[n_own:n_own + len(ici)], strict=True))
        _, _, cc = _my_pos()
        for n, from_sibling in zip(swap, res[n_own + len(ici):], strict=True):
            chip_sums[n] = _rs_add_sibling("rs_add_sibling_" + n, gsh[n], from_sibling, cc)
        return res[:n_own]

    gsh['ffn_w_down'] = mm_tn("dw_down", act, d_x2_b).reshape(N_DEV, nbp['ffn_w_gate'], d)
    d_gate, d_up, dcw_p, dcb_p = ffn_act_bwd1(gate_pre, up, conv_w_pad, conv_b_pad, d_act)
    d_gp = ffn_act_bwd2(d_gate, conv_w_pad)
    (d_h2_g,) = scatter_behind(lambda c: mm_sh_nt("d_h2_gate", d_gp, gathered['ffn_w_gate'], comm=c), swap=['ffn_w_down'])
    d_h2 = mm_sh_nt("d_h2_up", d_up, gathered['ffn_w_up'], add=d_h2_g)
    gsh['ffn_w_gate'] = mm_sh_out("dw_gate", h2, d_gp)
    (gsh['ffn_w_up'],) = scatter_behind(lambda c: mm_sh_out("dw_up", h2, d_up, comm=c), swap=['ffn_w_gate'])
    d_x1, dg_ffn_p = rowwise_vjp("rms_ffn_bwd", _rms_fn, [x1, w2d['ffn_norm_g']], ['row', 'const'], [d_h2], ['row'], [0, 1],
                                 heads=1, s=s, tm=tm, plus=d_x2)
    d_x1_b = d_x1.astype(BF16)
    d_ycat = mm_nt("d_ycat", d_x1_b, gathered['w_out'])
    gsh['w_out'] = mm_tn("dw_out", ycat, d_x1_b).reshape((N_DEV,) + w2d['w_out'].shape)
    d_yr_h = _to_heads(d_ycat[:, :c_rwkv], RWKV_HEAD)
    d_o_h = _to_heads(d_ycat[:, c_rwkv:], V_HEAD)

    d_q, d_kv, d_kp_h = scatter_behind(
        lambda c: attn_bwd(q_h, kv_h, kp_rot, cos2, sin2, rot, o_att, lse, d_o_h, comm=c),
        ici=['ffn_w_down'], swap=['ffn_w_up', 'w_out'])
    d_kp_rot = headsum("d_kpe_heads", d_kp_h)
    d_qn = mm_sh_nt("d_qn", d_q, gathered['mla_w_uq'])
    d_kvn = mm_sh_nt("d_kvn", d_kv, gathered['mla_w_ukv'])
    gsh['mla_w_uq'] = mm_sh_out("dw_uq", qn, d_q)
    gsh['mla_w_ukv'] = mm_sh_out("dw_ukv", kvn, d_kv)
    d_cq, d_ckv, d_kpe, dg_q_p, dg_kv_p = rowwise_vjp(
        "mla_pre_bwd", _mla_pre_grad_fn, mla_args, mla_kinds, [d_qn, d_kvn, d_kp_rot], ['row', 'row', 'row'], [0, 1, 2, 4, 5],
        heads=1, s=s, tm=tm)

    d_y, d_r_post, d_k_post, d_v_post, d_gate_r, dgnw_p, dgnb_p, drk_p = scatter_behind(
        lambda c: rowwise_vjp("rwkv_post_bwd", _rwkv_post_fn, post_args, post_kinds, [d_yr_h], ['hrow'], list(range(8)),
                              heads=n_rh, s=s, tm=tm_heads, comm=c),
        ici=['ffn_w_gate'], swap=['mla_w_uq', 'mla_w_ukv'])
    d_r_sc, d_w_sc, d_k_sc, d_v_sc, d_a_sc, d_b_sc = scatter_behind(
        lambda c: scan_bwd(hr, decay, kx, hv, a_sc, b_sc, ckpt, d_y, comm=c), ici=['ffn_w_up', 'w_out'])
    d_hk, d_hw_p, d_ha_p, d_hg_p, dw0_p, dw2_p, da0_p, da2_p, dg2_p, dkk_p, dka_p, d_hr, d_hv = scatter_behind(
        lambda c: rowwise_vjp(
            "rwkv_pre_bwd", _rwkv_pre_grad_fn, pre_args + [hr, hv], pre_kinds + ['hrow', 'hrow'],
            [d_w_sc, d_k_sc, d_k_post, d_a_sc, d_b_sc, d_gate_r, d_r_sc, d_r_post, d_v_sc, d_v_post], ['hrow'] * 10,
            list(range(13)), heads=n_rh, s=s, tm=tm_heads, comm=c),
        ici=['mla_w_uq', 'mla_w_ukv'])
    d_shifted = jnp.concatenate([_from_heads(d_hr), _from_heads(d_hk), _from_heads(d_hv), headsum("d_hw_heads", d_hw_p),
                                 headsum("d_ha_heads", d_ha_p), headsum("d_hg_heads", d_hg_p)], axis=-1)
    d_p_rwkv, dmu_p = token_shift_bwd(p_rwkv, w2d['rwkv_mu'], d_shifted, tm_wide)
    d_proj = pad_cols(jnp.concatenate([d_p_rwkv, d_cq, d_ckv, d_kpe], axis=-1).astype(BF16), nb['w_in'], nbp['w_in'])
    gsh['w_in'] = mm_sh_out("dw_in", h1, d_proj)
    chip_sums['w_in'] = rs_chip_sum('w_in', gsh['w_in'])
    (d_h1,) = scatter_behind(lambda c: mm_sh_nt("d_h1", d_proj, gathered['w_in'], comm=c), ici=['w_in'])
    grad_x, dg_attn_p = rowwise_vjp("rms_attn_bwd", _rms_fn, [xs, w2d['attn_norm_g']], ['row', 'const'], [d_h1], ['row'], [0, 1],
                                    heads=1, s=s, tm=tm, plus=d_x1)

    def from_heads_lora(g):
        return jnp.transpose(g, (1, 0, 2)).reshape(g.shape[1], n_rh * RWKV_HEAD)

    gw = {}
    gw['rwkv_w2'] = from_heads_lora(sum_partials("sum_dw2", dw2_p, True))
    gw['rwkv_a2'] = from_heads_lora(sum_partials("sum_da2", da2_p, True))
    gw['rwkv_g2'] = from_heads_lora(sum_partials("sum_dg2", dg2_p, True))
    dcw_pad = colsum("sum_dconv_w", dcw_p.reshape(dcw_p.shape[0], CONV_W * f_pad)).reshape(CONV_W, f_pad)
    gw['ffn_conv_w'] = unpad_cols(dcw_pad, nb['ffn_w_gate'], nbp['ffn_w_gate'])

    rep = {
        'attn_norm_g': sum_partials("sum_dg_attn", dg_attn_p, False),
        'rwkv_mu': colsum("sum_dmu", dmu_p.reshape(dmu_p.shape[0], shift_dim)),
        'rwkv_w0': sum_partials("sum_dw0", dw0_p, True).reshape(1, c_rwkv),
        'rwkv_a0': sum_partials("sum_da0", da0_p, True).reshape(1, c_rwkv),
        'rwkv_k_k': sum_partials("sum_dkk", dkk_p, True).reshape(1, c_rwkv),
        'rwkv_k_a': sum_partials("sum_dka", dka_p, True).reshape(1, c_rwkv),
        'rwkv_r_k': sum_partials("sum_drk", drk_p, True).reshape(1, c_rwkv),
        'rwkv_gn_w': sum_partials("sum_dgnw", dgnw_p, True).reshape(1, c_rwkv),
        'rwkv_gn_b': sum_partials("sum_dgnb", dgnb_p, True).reshape(1, c_rwkv),
        'mla_q_norm_g': sum_partials("sum_dg_q", dg_q_p, False),
        'mla_kv_norm_g': sum_partials("sum_dg_kv", dg_kv_p, False),
        'ffn_norm_g': sum_partials("sum_dg_ffn", dg_ffn_p, False),
        'ffn_conv_b': unpad_cols(colsum("sum_dconv_b", dcb_p.reshape(dcb_p.shape[0], f_pad)), nb['ffn_w_gate'], nbp['ffn_w_gate']),
        'final_norm_g': sum_partials("sum_dg_final", dg_final_p, False),
        'loss': sum_all("sum_loss", loss_rows.reshape(s // SUBLANES, SUBLANES)),
    }
    rep_pack = Pack([(n, w2d[n].shape) for n in REPLICATED] + [('loss', (1, 1))], 8)
    rep_all = all_gather("gather_rep_grads", rep_pack.pack(rep, F32))
    rep_sum = colsum("sum_rep_grads", rep_all.reshape(N_DEV, rep_pack.total)).reshape(rep_pack.rows, PACK_W)
    rep_g = rep_pack.unpack(rep_sum)
    loss = rep_g.pop('loss').reshape(())

    grads, deltas, new_m, new_v = dict(rep_g), {}, {}, {}
    my_x, my_y, _ = _my_pos()
    for n in BIG:
        grads[n], deltas[n], new_m[n], new_v[n] = adamw_scatter(
            "adamw_" + n, wts[n], mom_m[n], mom_v[n], chip_sums[n][0], from_chips[n], 2 * my_x + my_y)
    sm_pack = Pack([(n, w2d[n].shape) for n in SMALL_SHARDED], 8)
    g_shards = {n: _full_to_shards(gw[n], SMALL_SHARDED[n]) for n in SMALL_SHARDED}
    grads.update(sm_pack.unpack(reduce_scatter("small", sm_pack.pack(g_shards, F32, lead=(N_DEV,)))))
    rest_pack = Pack([(n, w2d[n].shape) for n in WEIGHTS if n not in BIG], 8)
    d_r, m_r, v_r = rowwise(
        "adamw_small", _adamw_fn, [rest_pack.pack(w2d, F32), rest_pack.pack(grads, F32), rest_pack.pack(m2d, F32),
                                   rest_pack.pack(v2d, F32)],
        ['row'] * 4, [('row', PACK_W, F32)] * 3, heads=1, s=rest_pack.rows, tm=_pick(rest_pack.rows, 512, SUBLANES))
    deltas.update(rest_pack.unpack(d_r))
    new_m.update(rest_pack.unpack(m_r))
    new_v.update(rest_pack.unpack(v_r))

    def shaped(dct):
        return [dct[n].reshape(out_shapes[n]) for n in WEIGHTS]

    return (loss, grad_x.reshape(x.shape), *shaped(grads), *shaped(deltas), *shaped(new_m), *shaped(new_v))
```

```python
import functools
import math

import jax
import jax.numpy as jnp
import numpy as np
from jax import lax
from jax.experimental import pallas as pl
from jax.experimental.pallas import tpu as pltpu

F32 = jnp.float32
BF16 = jnp.bfloat16
HIGHEST = lax.Precision.HIGHEST
MESH = pl.DeviceIdType.MESH

N_DEV = 8
LANES = 128
SUBLANES = 8
VMEM_LIMIT = 48 * 1024 * 1024
RESIDENT_BYTES = 8 * 1024 * 1024

NORM_EPS = 1e-6
GN_EPS = 64e-5
RWKV_HEAD = 64
QK_NOPE = 128
QK_ROPE = 64
V_HEAD = 128
ROPE_THETA = 10000.0
CONV_W = 3
NEG_INF = -1e30
SCAN_CHUNK = 64
SCAN_HEADS = 16
SCAN_PASSES_SOLVE = 1
SCAN_PASSES_OUT = 1

ADAM_LR = 0.001
ADAM_B1 = 0.9
ADAM_B2 = 0.999
ADAM_EPS = 1e-08
ADAM_WD = 0.01
ADAM_STEP = 10

WEIGHTS = ['attn_norm_g', 'w_in', 'rwkv_mu', 'rwkv_w0', 'rwkv_w2', 'rwkv_a0', 'rwkv_a2', 'rwkv_g2', 'rwkv_k_k',
           'rwkv_k_a', 'rwkv_r_k', 'rwkv_gn_w', 'rwkv_gn_b', 'mla_q_norm_g', 'mla_w_uq', 'mla_kv_norm_g', 'mla_w_ukv',
           'w_out', 'ffn_norm_g', 'ffn_w_gate', 'ffn_w_up', 'ffn_conv_w', 'ffn_conv_b', 'ffn_w_down', 'final_norm_g']
BIG = {'w_in': 'col', 'mla_w_uq': 'col', 'mla_w_ukv': 'col', 'w_out': 'row', 'ffn_w_gate': 'col', 'ffn_w_up': 'col',
       'ffn_w_down': 'row'}
SMALL_SHARDED = {'rwkv_w2': 'col', 'rwkv_a2': 'col', 'rwkv_g2': 'col', 'ffn_conv_w': 'col'}
SHARDED = {**BIG, **SMALL_SHARDED}
REPLICATED = [n for n in WEIGHTS if n not in SHARDED]


def _round_up(n, m):
    return (n + m - 1) // m * m


def _pick(n, cap, unit):
    if n <= cap:
        return n
    best = None
    for t in range(unit, cap + 1, unit):
        if n % t == 0:
            best = t
    assert best is not None, (n, cap, unit)
    return best


def _params(sem):
    return pltpu.CompilerParams(dimension_semantics=sem, vmem_limit_bytes=VMEM_LIMIT)


def mm(name, a, b, add=None, out_dtype=F32, comm=None):
    m, k = a.shape
    k2, n = b.shape
    assert k == k2, (name, a.shape, b.shape)
    tn = n if k * n * 2 <= RESIDENT_BYTES else _pick(n, 640, LANES)
    tm = _pick(m, 2048 if (tn < n and m * k * 2 <= RESIDENT_BYTES) else 512, SUBLANES * 2)
    has_add = add is not None

    def body(a_ref, b_ref, *rest):
        o_ref = rest[-1]
        acc = jnp.dot(a_ref[...].astype(BF16), b_ref[...].astype(BF16), preferred_element_type=F32)
        if has_add:
            acc = acc + rest[0][...].astype(F32)
        o_ref[...] = acc.astype(o_ref.dtype)

    in_specs = [pl.BlockSpec((tm, k), lambda i, j: (i, 0)), pl.BlockSpec((k, tn), lambda i, j: (0, j))]
    ops = [a, b]
    if has_add:
        in_specs.append(pl.BlockSpec((tm, tn), lambda i, j: (i, j)))
        ops.append(add)
    res = _pallas(
        body, name=name, grid=(m // tm, n // tn), in_specs=in_specs,
        out_specs=[pl.BlockSpec((tm, tn), lambda i, j: (i, j))],
        out_shape=[jax.ShapeDtypeStruct((m, n), out_dtype)], sem=("parallel", "parallel"), comm=comm,
    )(*ops)
    return res[0] if comm is None else res


def mm_nt(name, a, b, out_dtype=F32):
    m, k = a.shape
    n, k2 = b.shape
    assert k == k2, (name, a.shape, b.shape)
    tm = _pick(m, 2048 if m * k * 2 <= RESIDENT_BYTES else 512, SUBLANES * 2)
    tn = _pick(n, 1024, LANES)

    def body(a_ref, b_ref, o_ref):
        acc = lax.dot_general(a_ref[...].astype(BF16), b_ref[...].astype(BF16), (((1,), (1,)), ((), ())),
                              preferred_element_type=F32)
        o_ref[...] = acc.astype(o_ref.dtype)

    return pl.pallas_call(
        body, name=name, grid=(m // tm, n // tn),
        in_specs=[pl.BlockSpec((tm, k), lambda i, j: (i, 0)), pl.BlockSpec((tn, k), lambda i, j: (j, 0))],
        out_specs=pl.BlockSpec((tm, tn), lambda i, j: (i, j)),
        out_shape=jax.ShapeDtypeStruct((m, n), out_dtype),
        compiler_params=_params(("parallel", "parallel")),
    )(a, b)


def mm_sh(name, a, g, out_dtype=F32, comm=None, slabs=False):
    m, k = a.shape
    nd, k2, nbp = g.shape
    assert k == k2, (name, a.shape, g.shape)
    tm = _pick(m, 2048 if m * k * 2 <= RESIDENT_BYTES else 512, SUBLANES * 2)

    def body(a_ref, b_ref, o_ref):
        o_ref[...] = jnp.dot(a_ref[...].astype(BF16), b_ref[...].astype(BF16), preferred_element_type=F32).astype(o_ref.dtype)

    if slabs:
        out_spec, out_shape = pl.BlockSpec((None, tm, nbp), lambda i, j: (j, i, 0)), (nd, m, nbp)
    else:
        out_spec, out_shape = pl.BlockSpec((tm, nbp), lambda i, j: (i, j)), (m, nd * nbp)
    res = _pallas(
        body, name=name, grid=(m // tm, nd),
        in_specs=[pl.BlockSpec((tm, k), lambda i, j: (i, 0)), pl.BlockSpec((None, k, nbp), lambda i, j: (j, 0, 0))],
        out_specs=[out_spec], out_shape=[jax.ShapeDtypeStruct(out_shape, out_dtype)], sem=("parallel", "parallel"), comm=comm,
    )(a, g)
    return res[0] if comm is None else res


def mm_sh_nt(name, a, g, add=None, comm=None):
    nd, k, nbp = g.shape
    slabs = a.ndim == 3
    m = a.shape[1] if slabs else a.shape[0]
    assert a.shape == ((nd, m, nbp) if slabs else (m, nd * nbp)), (name, a.shape, g.shape)
    has_add = add is not None
    tm = _pick(m, 512 if has_add else 1024, SUBLANES * 2)

    def body(a_ref, b_ref, *rest):
        o_ref = rest[-1]
        part = lax.dot_general(a_ref[...].astype(BF16), b_ref[...].astype(BF16), (((1,), (1,)), ((), ())),
                               preferred_element_type=F32)

        @pl.when(pl.program_id(1) == 0)
        def _():
            o_ref[...] = part + rest[0][...] if has_add else part

        @pl.when(pl.program_id(1) != 0)
        def _():
            o_ref[...] += part

    a_spec = pl.BlockSpec((None, tm, nbp), lambda i, j: (j, i, 0)) if slabs else pl.BlockSpec((tm, nbp), lambda i, j: (i, j))
    in_specs = [a_spec, pl.BlockSpec((None, k, nbp), lambda i, j: (j, 0, 0))]
    ops = [a, g]
    if has_add:
        in_specs.append(pl.BlockSpec((tm, k), lambda i, j: (i, 0)))
        ops.append(add)
    res = _pallas(
        body, name=name, grid=(m // tm, nd), in_specs=in_specs,
        out_specs=[pl.BlockSpec((tm, k), lambda i, j: (i, 0))],
        out_shape=[jax.ShapeDtypeStruct((m, k), F32)], sem=("parallel", "arbitrary"), comm=comm,
    )(*ops)
    return res[0] if comm is None else res


def mm_tn(name, a, b):
    m, k = a.shape
    m2, n = b.shape
    assert m == m2, (name, a.shape, b.shape)
    tk = _pick(k, 512, LANES)
    tn = n if m * n * 2 <= RESIDENT_BYTES else _pick(n, 640, LANES)

    def body(a_ref, b_ref, o_ref):
        o_ref[...] = lax.dot_general(a_ref[...].astype(BF16), b_ref[...].astype(BF16), (((0,), (0,)), ((), ())),
                                     preferred_element_type=F32)

    return pl.pallas_call(
        body, name=name, grid=(k // tk, n // tn),
        in_specs=[pl.BlockSpec((m, tk), lambda i, j: (0, i)), pl.BlockSpec((m, tn), lambda i, j: (0, j))],
        out_specs=pl.BlockSpec((tk, tn), lambda i, j: (i, j)),
        out_shape=jax.ShapeDtypeStruct((k, n), F32),
        compiler_params=_params(("parallel", "parallel")),
    )(a, b)


def mm_sh_out(name, a, b, comm=None):
    m, k = a.shape
    slabs = b.ndim == 3
    nbp = b.shape[2] if slabs else b.shape[1] // N_DEV
    assert b.shape == ((N_DEV, m, nbp) if slabs else (m, N_DEV * nbp)), (name, a.shape, b.shape)
    tk = _pick(k, 2048 if k * m * 2 <= RESIDENT_BYTES else 512, LANES)
    b_spec = pl.BlockSpec((None, m, nbp), lambda i, j: (j, 0, 0)) if slabs else pl.BlockSpec((m, nbp), lambda i, j: (0, j))

    def body(a_ref, b_ref, o_ref):
        o_ref[...] = lax.dot_general(a_ref[...].astype(BF16), b_ref[...].astype(BF16), (((0,), (0,)), ((), ())),
                                     preferred_element_type=F32)

    res = _pallas(
        body, name=name, grid=(k // tk, N_DEV),
        in_specs=[pl.BlockSpec((m, tk), lambda i, j: (0, i)), b_spec],
        out_specs=[pl.BlockSpec((None, tk, nbp), lambda i, j: (j, i, 0))],
        out_shape=[jax.ShapeDtypeStruct((N_DEV, k, nbp), F32)], sem=("parallel", "parallel"), comm=comm,
    )(a, b)
    return res[0] if comm is None else res


def pad_cols(y, nb, nbp):
    m = y.shape[0]
    if nb == nbp:
        return y
    return jnp.pad(y.reshape(m, N_DEV, nb), ((0, 0), (0, 0), (0, nbp - nb))).reshape(m, N_DEV * nbp)


def unpad_cols(y, nb, nbp):
    m = y.shape[0]
    if nb == nbp:
        return y
    return y.reshape(m, N_DEV, nbp)[:, :, :nb].reshape(m, N_DEV * nb)


def _in_spec(kind, a, tm):
    if kind == 'row':
        return pl.BlockSpec((tm, a.shape[1]), lambda h, i: (i, 0))
    if kind == 'hrow':
        return pl.BlockSpec((None, tm, a.shape[2]), lambda h, i: (h, i, 0))
    if kind == 'const':
        return pl.BlockSpec(a.shape, lambda h, i: (0, 0))
    assert kind == 'hconst', kind
    return pl.BlockSpec((None,) + a.shape[1:], lambda h, i: (h, 0, 0))


def _row_out(kind, c, dtype, heads, s, tm):
    if kind == 'row':
        assert heads == 1
        return jax.ShapeDtypeStruct((s, c), dtype), pl.BlockSpec((tm, c), lambda h, i: (i, 0))
    return jax.ShapeDtypeStruct((heads, s, c), dtype), pl.BlockSpec((None, tm, c), lambda h, i: (h, i, 0))


def rowwise(name, fn, arrs, kinds, outs, *, heads, s, tm, comm=None):
    n_in = len(arrs)

    def body(*refs):
        vals = fn(*[r[...] for r in refs[:n_in]])
        for o, v in zip(refs[n_in:], vals, strict=True):
            o[...] = v.astype(o.dtype)

    shapes, specs = zip(*[_row_out(k, c, dt, heads, s, tm) for k, c, dt in outs])
    return _pallas(
        body, name=name, grid=(heads, s // tm),
        in_specs=[_in_spec(k, a, tm) for k, a in zip(kinds, arrs, strict=True)],
        out_specs=list(specs), out_shape=list(shapes), sem=("parallel", "parallel"), comm=comm,
    )(*arrs)


def rowwise_vjp(name, fn, arrs, kinds, cots, cot_kinds, wrt, *, heads, s, tm, out_dtypes=None, primal=False, comm=None,
                plus=None):
    n_in, n_cot = len(arrs), len(cots)
    nb = s // tm
    out_dtypes = out_dtypes or [F32] * len(wrt)
    extra = [] if plus is None else [plus]

    def body(*refs):
        vals = [r[...] for r in refs[:n_in]]
        cvals = tuple(r[...].astype(F32) for r in refs[n_in:n_in + n_cot])
        outs = refs[n_in + n_cot + len(extra):]

        def f(*dv):
            full = list(vals)
            for j, i in enumerate(wrt):
                full[i] = dv[j]
            return tuple(fn(*full))

        prim, vjp_fn = jax.vjp(f, *[vals[i].astype(F32) for i in wrt])
        grads = list(vjp_fn(cvals))
        if plus is not None:
            grads[0] = grads[0] + refs[n_in + n_cot][...]
        for o, g in zip(outs[:len(wrt)], grads, strict=True):
            o[...] = g.astype(o.dtype)
        if primal:
            for o, p in zip(outs[len(wrt):], prim, strict=True):
                o[...] = p.astype(o.dtype)

    shapes, specs = [], []
    for i, dt in zip(wrt, out_dtypes, strict=True):
        kind, a = kinds[i], arrs[i]
        if kind in ('row', 'hrow'):
            c = a.shape[-1]
            sh, sp = _row_out('row' if (kind == 'row' and heads == 1) else 'hrow', c, dt, heads, s, tm)
        else:
            r, c = a.shape[-2:]
            sh = jax.ShapeDtypeStruct((heads, nb, r, c), dt)
            sp = pl.BlockSpec((None, None, r, c), lambda h, i: (h, i, 0, 0))
        shapes.append(sh)
        specs.append(sp)
    if primal:
        for ck, c in zip(cot_kinds, cots, strict=True):
            sh, sp = _row_out(ck, c.shape[-1], F32, heads, s, tm)
            shapes.append(sh)
            specs.append(sp)
    in_specs = [_in_spec(k, a, tm) for k, a in zip(kinds, arrs, strict=True)]
    in_specs += [_in_spec(k, a, tm) for k, a in zip(cot_kinds, cots, strict=True)]
    in_specs += [_in_spec('row', a, tm) for a in extra]
    return _pallas(
        body, name=name, grid=(heads, nb), in_specs=in_specs, out_specs=specs, out_shape=shapes,
        sem=("parallel", "parallel"), comm=comm,
    )(*arrs, *cots, *extra)


def colsum(name, x):
    n, m = x.shape
    tc = _pick(m, 32768, LANES) if m % LANES == 0 else m

    def body(x_ref, o_ref):
        acc = x_ref[0:1, :].astype(F32)
        for r in range(1, n):
            acc = acc + x_ref[r:r + 1, :].astype(F32)
        o_ref[...] = acc

    return pl.pallas_call(
        body, name=name, grid=(m // tc,), in_specs=[pl.BlockSpec((n, tc), lambda j: (0, j))],
        out_specs=pl.BlockSpec((1, tc), lambda j: (0, j)), out_shape=jax.ShapeDtypeStruct((1, m), F32),
        compiler_params=_params(("parallel",)),
    )(x)


def headsum(name, x):
    h, s, c = x.shape
    tm = _pick(s, 256, SUBLANES)

    def body(x_ref, o_ref):
        acc = x_ref[0]
        for j in range(1, h):
            acc = acc + x_ref[j]
        o_ref[...] = acc

    return pl.pallas_call(
        body, name=name, grid=(s // tm,), in_specs=[pl.BlockSpec((h, tm, c), lambda i: (0, i, 0))],
        out_specs=pl.BlockSpec((tm, c), lambda i: (i, 0)), out_shape=jax.ShapeDtypeStruct((s, c), F32),
        compiler_params=_params(("parallel",)),
    )(x)


def sum_all(name, x):
    def body(x_ref, o_ref):
        o_ref[...] = jnp.sum(x_ref[...], keepdims=True)

    return pl.pallas_call(body, name=name, out_shape=jax.ShapeDtypeStruct((1, 1), F32))(x)


def sum_partials(name, p, per_head):
    h, nb, r, c = p.shape
    if per_head:
        flat = jnp.transpose(p, (1, 0, 2, 3)).reshape(nb, h * r * c)
        if nb == 1:
            return flat.reshape(h, r, c)
        return colsum(name, flat).reshape(h, r, c)
    flat = p.reshape(h * nb, r * c)
    if h * nb == 1:
        return flat.reshape(r, c)
    return colsum(name, flat).reshape(r, c)


def _rms_fn(x, g):
    xf = x.astype(F32)
    return (xf * lax.rsqrt(jnp.mean(xf * xf, axis=-1, keepdims=True) + NORM_EPS) * g,)


def _softplus(z):
    return jnp.maximum(z, 0.0) + jnp.log(1.0 + jnp.exp(-jnp.abs(z)))


def _rwkv_pre_fn(hk, hw, ha, hg, w0, w2, a0, a2, g2, k_k, k_a):
    zw = w0 + jnp.dot(jnp.tanh(hw), w2, preferred_element_type=F32)
    w_log = -_softplus(-zw) - 0.5
    decay = jnp.exp(-jnp.exp(w_log))
    a = jax.nn.sigmoid(a0 + jnp.dot(ha, a2, preferred_element_type=F32))
    g = jnp.dot(jax.nn.sigmoid(hg), g2, preferred_element_type=F32)
    kk = hk * k_k
    kk = kk * lax.rsqrt(jnp.maximum(jnp.sum(kk * kk, axis=-1, keepdims=True), 1e-24))
    k = hk * (1.0 + (a - 1.0) * k_a)
    return decay, k, -kk, kk * a, g


def _rwkv_pre_grad_fn(hk, hw, ha, hg, w0, w2, a0, a2, g2, k_k, k_a, hr, hv):
    decay, k, a_sc, b_sc, g = _rwkv_pre_fn(hk, hw, ha, hg, w0, w2, a0, a2, g2, k_k, k_a)
    return decay, k, k, a_sc, b_sc, g, hr, hr, hv, hv


def _rwkv_post_fn(y, r, k, v, g, gn_w, gn_b, r_k):
    mu = jnp.mean(y, axis=-1, keepdims=True)
    var = jnp.mean(jnp.square(y - mu), axis=-1, keepdims=True)
    yn = (y - mu) * lax.rsqrt(var + GN_EPS) * gn_w + gn_b
    bonus = jnp.sum(r * k * r_k, axis=-1, keepdims=True) * v
    return ((yn + bonus) * g,)


def _rope_tables(pos, inv_freq2):
    ang = pos * inv_freq2
    return jnp.cos(ang), jnp.sin(ang)


def _rope(t, cos2, sin2, rot):
    return t * cos2 + jnp.dot(t, rot, precision=HIGHEST, preferred_element_type=F32) * sin2


def _mla_pre_fn(c_q, c_kv, k_pe, pos, q_g, kv_g, inv_freq2, rot):
    cos2, sin2 = _rope_tables(pos, inv_freq2)
    return _rms_fn(c_q, q_g)[0], _rms_fn(c_kv, kv_g)[0], _rope(k_pe, cos2, sin2, rot), cos2, sin2


def _mla_pre_grad_fn(c_q, c_kv, k_pe, pos, q_g, kv_g, inv_freq2, rot):
    return _mla_pre_fn(c_q, c_kv, k_pe, pos, q_g, kv_g, inv_freq2, rot)[:3]


def _loss_fn(x2, g, target):
    y = _rms_fn(x2, g)[0]
    return (0.5 * jnp.mean(jnp.square(y - target), axis=-1, keepdims=True),)


def _adamw_fn(w, g, m, v):
    m = ADAM_B1 * m + (1.0 - ADAM_B1) * g
    v = ADAM_B2 * v + (1.0 - ADAM_B2) * jnp.square(g)
    m_hat = m / (1.0 - ADAM_B1 ** ADAM_STEP)
    v_hat = v / (1.0 - ADAM_B2 ** ADAM_STEP)
    delta = -ADAM_LR * (m_hat / (jnp.sqrt(v_hat) + ADAM_EPS) + ADAM_WD * w)
    return delta, m, v


def _prev_halo_spec(c, tm):
    return pl.BlockSpec((SUBLANES, c), lambda i: (jnp.maximum(i * (tm // SUBLANES) - 1, 0), 0))


def _next_halo_spec(c, tm, s):
    return pl.BlockSpec((SUBLANES, c), lambda i: (jnp.minimum((i + 1) * (tm // SUBLANES), s // SUBLANES - 1), 0))


def _shift_down(p, halo, first_block, n):
    out = pltpu.roll(p, n, 0)
    row = lax.broadcasted_iota(jnp.int32, p.shape, 0)
    for j in range(n):
        top = jnp.where(first_block, 0.0, halo[SUBLANES - n + j:SUBLANES - n + j + 1, :])
        out = jnp.where(row == j, top, out)
    return out


def _shift_up(p, halo, last_block, n):
    rows = p.shape[0]
    out = pltpu.roll(p, rows - n, 0)
    row = lax.broadcasted_iota(jnp.int32, p.shape, 0)
    for j in range(n):
        bot = jnp.where(last_block, 0.0, halo[j:j + 1, :])
        out = jnp.where(row == rows - n + j, bot, out)
    return out


def token_shift_fwd(p, mu, tm):
    s, c = p.shape

    def body(p_ref, halo_ref, mu_ref, o_ref):
        pv = p_ref[...]
        prev = _shift_down(pv, halo_ref[...], pl.program_id(0) == 0, 1)
        o_ref[...] = pv + (prev - pv) * mu_ref[...]

    return pl.pallas_call(
        body, name="token_shift_fwd", grid=(s // tm,),
        in_specs=[pl.BlockSpec((tm, c), lambda i: (i, 0)), _prev_halo_spec(c, tm), pl.BlockSpec((1, c), lambda i: (0, 0))],
        out_specs=pl.BlockSpec((tm, c), lambda i: (i, 0)), out_shape=jax.ShapeDtypeStruct((s, c), F32),
        compiler_params=_params(("parallel",)),
    )(p, p, mu)


def token_shift_bwd(p, mu, ds, tm):
    s, c = p.shape
    nb = s // tm

    def body(p_ref, halo_ref, mu_ref, ds_ref, dsn_ref, dp_ref, dmu_ref):
        i = pl.program_id(0)
        pv, dsv, muv = p_ref[...], ds_ref[...], mu_ref[...]
        prev = _shift_down(pv, halo_ref[...], i == 0, 1)
        nxt = _shift_up(dsv, dsn_ref[...], i == nb - 1, 1)
        dp_ref[...] = dsv * (1.0 - muv) + nxt * muv
        dmu_ref[...] = jnp.sum(dsv * (prev - pv), axis=0, keepdims=True)

    return pl.pallas_call(
        body, name="token_shift_bwd", grid=(nb,),
        in_specs=[pl.BlockSpec((tm, c), lambda i: (i, 0)), _prev_halo_spec(c, tm), pl.BlockSpec((1, c), lambda i: (0, 0)),
                  pl.BlockSpec((tm, c), lambda i: (i, 0)), _next_halo_spec(c, tm, s)],
        out_specs=[pl.BlockSpec((tm, c), lambda i: (i, 0)), pl.BlockSpec((None, 1, c), lambda i: (i, 0, 0))],
        out_shape=[jax.ShapeDtypeStruct((s, c), F32), jax.ShapeDtypeStruct((nb, 1, c), F32)],
        compiler_params=_params(("parallel",)),
    )(p, p, mu, ds, ds)


def _ffn_tiles(s, f):
    return _pick(s, 256, SUBLANES), _pick(f, 1408, LANES)


def _conv_gate(gp, halo, first_block, cw, cb):
    p1 = _shift_down(gp, halo, first_block, 1)
    p2 = _shift_down(gp, halo, first_block, 2)
    return cw[0:1, :] * p2 + cw[1:2, :] * p1 + cw[2:3, :] * gp + cb, p1, p2


def ffn_act_fwd(gate_pre, up, conv_w, conv_b):
    s, f = gate_pre.shape
    tm, tc = _ffn_tiles(s, f)

    def body(gp_ref, halo_ref, up_ref, cw_ref, cb_ref, o_ref):
        gate, _, _ = _conv_gate(gp_ref[...], halo_ref[...], pl.program_id(0) == 0, cw_ref[...], cb_ref[...])
        o_ref[...] = (gate * jax.nn.sigmoid(gate) * up_ref[...]).astype(o_ref.dtype)

    blk = pl.BlockSpec((tm, tc), lambda i, j: (i, j))
    return pl.pallas_call(
        body, name="ffn_act_fwd", grid=(s // tm, f // tc),
        in_specs=[blk, pl.BlockSpec((SUBLANES, tc), lambda i, j: (jnp.maximum(i * (tm // SUBLANES) - 1, 0), j)), blk,
                  pl.BlockSpec((CONV_W, tc), lambda i, j: (0, j)), pl.BlockSpec((1, tc), lambda i, j: (0, j))],
        out_specs=blk, out_shape=jax.ShapeDtypeStruct((s, f), BF16),
        compiler_params=_params(("parallel", "parallel")),
    )(gate_pre, gate_pre, up, conv_w, conv_b)


def ffn_act_bwd1(gate_pre, up, conv_w, conv_b, d_act):
    s, f = gate_pre.shape
    tm, tc = _ffn_tiles(s, f)
    nb = s // tm

    def body(gp_ref, halo_ref, up_ref, cw_ref, cb_ref, da_ref, dg_ref, du_ref, dcw_ref, dcb_ref):
        gp = gp_ref[...]
        gate, p1, p2 = _conv_gate(gp, halo_ref[...], pl.program_id(0) == 0, cw_ref[...], cb_ref[...])
        sig = jax.nn.sigmoid(gate)
        da = da_ref[...].astype(F32)
        du_ref[...] = (da * gate * sig).astype(du_ref.dtype)
        dg = da * up_ref[...] * (sig * (1.0 + gate * (1.0 - sig)))
        dg_ref[...] = dg
        dcb_ref[...] = jnp.sum(dg, axis=0, keepdims=True)
        dcw_ref[0:1, :] = jnp.sum(dg * p2, axis=0, keepdims=True)
        dcw_ref[1:2, :] = jnp.sum(dg * p1, axis=0, keepdims=True)
        dcw_ref[2:3, :] = jnp.sum(dg * gp, axis=0, keepdims=True)

    blk = pl.BlockSpec((tm, tc), lambda i, j: (i, j))
    return pl.pallas_call(
        body, name="ffn_act_bwd1", grid=(nb, f // tc),
        in_specs=[blk, pl.BlockSpec((SUBLANES, tc), lambda i, j: (jnp.maximum(i * (tm // SUBLANES) - 1, 0), j)), blk,
                  pl.BlockSpec((CONV_W, tc), lambda i, j: (0, j)), pl.BlockSpec((1, tc), lambda i, j: (0, j)), blk],
        out_specs=[blk, blk, pl.BlockSpec((None, CONV_W, tc), lambda i, j: (i, 0, j)),
                   pl.BlockSpec((None, 1, tc), lambda i, j: (i, 0, j))],
        out_shape=[jax.ShapeDtypeStruct((s, f), F32), jax.ShapeDtypeStruct((s, f), BF16),
                   jax.ShapeDtypeStruct((nb, CONV_W, f), F32), jax.ShapeDtypeStruct((nb, 1, f), F32)],
        compiler_params=_params(("parallel", "parallel")),
    )(gate_pre, gate_pre, up, conv_w, conv_b, d_act)


def ffn_act_bwd2(d_gate, conv_w):
    s, f = d_gate.shape
    tm, tc = _ffn_tiles(s, f)
    nb = s // tm

    def body(dg_ref, halo_ref, cw_ref, o_ref):
        dg, cw = dg_ref[...], cw_ref[...]
        last = pl.program_id(0) == nb - 1
        n1 = _shift_up(dg, halo_ref[...], last, 1)
        n2 = _shift_up(dg, halo_ref[...], last, 2)
        o_ref[...] = (cw[2:3, :] * dg + cw[1:2, :] * n1 + cw[0:1, :] * n2).astype(o_ref.dtype)

    blk = pl.BlockSpec((tm, tc), lambda i, j: (i, j))
    return pl.pallas_call(
        body, name="ffn_act_bwd2", grid=(nb, f // tc),
        in_specs=[blk, pl.BlockSpec((SUBLANES, tc), lambda i, j: (jnp.minimum((i + 1) * (tm // SUBLANES), s // SUBLANES - 1), j)),
                  pl.BlockSpec((CONV_W, tc), lambda i, j: (0, j))],
        out_specs=blk, out_shape=jax.ShapeDtypeStruct((s, f), BF16),
        compiler_params=_params(("parallel", "parallel")),
    )(d_gate, d_gate, conv_w)


def _mxu(x, y, cx, cy):
    if x.ndim == 3:
        return lax.dot_general(x, y, (((cx + 1,), (cy + 1,)), ((0,), (0,))), preferred_element_type=F32)
    return lax.dot_general(x, y, (((cx,), (cy,)), ((), ())), preferred_element_type=F32)


def _split(x):
    hi = x.astype(BF16)
    return hi, (x - hi.astype(F32)).astype(BF16)


def _make_dot3(cx, cy, passes):
    @jax.custom_vjp
    def f(x, y):
        if passes == 1:
            return _mxu(x.astype(BF16), y.astype(BF16), cx, cy)
        xh, xl = _split(x)
        yh, yl = _split(y)
        return _mxu(xh, yh, cx, cy) + (_mxu(xh, yl, cx, cy) + _mxu(xl, yh, cx, cy))

    def fwd(x, y):
        return f(x, y), (x, y)

    def bwd(res, g):
        x, y = res
        dx = dot3(g, y, 1, 1 - cy, passes) if cx == 1 else dot3(y, g, 1 - cy, 1, passes)
        dy = dot3(x, g, 1 - cx, 0, passes) if cy == 0 else dot3(g, x, 0, 1 - cx, passes)
        return dx, dy

    f.defvjp(fwd, bwd)
    return f


_DOT3 = {}


def dot3(x, y, cx, cy, passes=3):
    if (cx, cy, passes) not in _DOT3:
        _DOT3[(cx, cy, passes)] = _make_dot3(cx, cy, passes)
    return _DOT3[(cx, cy, passes)](x, y)


def _dot(x, y, passes=3):
    return dot3(x, y, 1, 0, passes)


def _dot_nt(x, y, passes=3):
    return dot3(x, y, 1, 1, passes)


def _dot_tn(x, y, passes=3):
    return dot3(x, y, 0, 0, passes)


def _tri_sum(x, lower):
    t = x.shape[-2]
    row = lax.broadcasted_iota(jnp.int32, (t, t), 0)
    col = lax.broadcasted_iota(jnp.int32, (t, t), 1)
    tri = jnp.where((col <= row) if lower else (col >= row), 1.0, 0.0).astype(BF16)
    if x.ndim == 3:
        tri = jnp.broadcast_to(tri[None], (x.shape[0], t, t))
    hi = x.astype(BF16)
    rest = x - hi.astype(F32)
    mid = rest.astype(BF16)
    low = (rest - mid.astype(F32)).astype(BF16)
    return _mxu(tri, hi, 1, 0) + (_mxu(tri, mid, 1, 0) + _mxu(tri, low, 1, 0))


@jax.custom_vjp
def _cumsum_rows(x):
    return _tri_sum(x, True)


_cumsum_rows.defvjp(lambda x: (_tri_sum(x, True), None), lambda _, g: (_tri_sum(g, False),))


def _scan_chunk(s0, r, w, k, v, a, b):
    t = r.shape[1]
    row = lax.broadcasted_iota(jnp.int32, (1, t, t), 1)
    col = lax.broadcasted_iota(jnp.int32, (1, t, t), 2)
    strict, incl = col < row, col <= row
    logw = jnp.log(w)
    cum = _cumsum_rows(logw)
    w_in, w_ex, w_inv = jnp.exp(cum), jnp.exp(cum - logw), jnp.exp(-cum)
    w_all = jnp.exp(jnp.sum(logw, axis=1, keepdims=True))
    at, rt, kt, bt = a * w_ex, r * w_in, k * w_inv, b * w_inv
    ps, po = SCAN_PASSES_SOLVE, SCAN_PASSES_OUT
    a_ab = jnp.where(strict, _dot_nt(at, bt, ps), 0.0)
    a_ak = jnp.where(strict, _dot_nt(at, kt, ps), 0.0)
    a_rk = jnp.where(incl, _dot_nt(rt, kt, po), 0.0)
    a_rb = jnp.where(incl, _dot_nt(rt, bt, po), 0.0)
    u = _dot_nt(at, s0, ps) + _dot(a_ak, v, ps)
    p = a_ab
    steps = int(math.log2(t))
    assert 2 ** steps == t
    for j in range(steps):
        u = u + _dot(p, u, ps)
        if j < steps - 1:
            p = _dot(p, p, ps)
    y = _dot_nt(rt, s0, po) + _dot(a_rk, v, po) + _dot(a_rb, u, po)
    s_new = s0 * w_all + _dot_tn(v, kt * w_all, po) + _dot_tn(u, bt * w_all, po)
    return y, s_new


def scan_fwd(r, w, k, v, a, b, comm=None):
    h, s, n = r.shape
    t = min(SCAN_CHUNK, s)
    nc = s // t

    hb = SCAN_HEADS if h % SCAN_HEADS == 0 else 1

    def body(r_ref, w_ref, k_ref, v_ref, a_ref, b_ref, y_ref, ck_ref, st_ref):
        @pl.when(pl.program_id(1) == 0)
        def _():
            st_ref[...] = jnp.zeros_like(st_ref)

        s0 = st_ref[...]
        ck_ref[...] = s0
        y, s_new = _scan_chunk(s0, r_ref[...], w_ref[...], k_ref[...], v_ref[...], a_ref[...], b_ref[...])
        y_ref[...] = y
        st_ref[...] = s_new

    blk = pl.BlockSpec((hb, t, n), lambda hh, c: (hh, c, 0))
    return _pallas(
        body, name="rwkv_scan_fwd", grid=(h // hb, nc), in_specs=[blk] * 6,
        out_specs=[blk, pl.BlockSpec((hb, None, n, n), lambda hh, c: (hh, c, 0, 0))],
        out_shape=[jax.ShapeDtypeStruct((h, s, n), F32), jax.ShapeDtypeStruct((h, nc, n, n), F32)],
        scratch_shapes=[pltpu.VMEM((hb, n, n), F32)], sem=("parallel", "arbitrary"), comm=comm,
    )(r, w, k, v, a, b)


def scan_bwd(r, w, k, v, a, b, ck, dy, comm=None):
    h, s, n = r.shape
    t = min(SCAN_CHUNK, s)
    nc = s // t

    hb = SCAN_HEADS if h % SCAN_HEADS == 0 else 1

    def body(r_ref, w_ref, k_ref, v_ref, a_ref, b_ref, ck_ref, dy_ref, dr_ref, dw_ref, dk_ref, dv_ref, da_ref, db_ref, ds_ref):
        @pl.when(pl.program_id(1) == 0)
        def _():
            ds_ref[...] = jnp.zeros_like(ds_ref)

        _, vjp_fn = jax.vjp(_scan_chunk, ck_ref[...], r_ref[...], w_ref[...], k_ref[...], v_ref[...], a_ref[...], b_ref[...])
        ds0, dr, dw, dk, dv, da, db = vjp_fn((dy_ref[...], ds_ref[...]))
        ds_ref[...] = ds0
        dr_ref[...], dw_ref[...], dk_ref[...], dv_ref[...], da_ref[...], db_ref[...] = dr, dw, dk, dv, da, db

    blk = pl.BlockSpec((hb, t, n), lambda hh, c: (hh, nc - 1 - c, 0))
    return _pallas(
        body, name="rwkv_scan_bwd", grid=(h // hb, nc),
        in_specs=[blk] * 6 + [pl.BlockSpec((hb, None, n, n), lambda hh, c: (hh, nc - 1 - c, 0, 0)), blk],
        out_specs=[blk] * 6, out_shape=[jax.ShapeDtypeStruct((h, s, n), F32)] * 6,
        scratch_shapes=[pltpu.VMEM((hb, n, n), F32)], sem=("parallel", "arbitrary"), comm=comm,
    )(r, w, k, v, a, b, ck, dy)


ATTN_BLOCK = 256
ATTN_LEVELS = 4
ATTN_SLAB = 2 * LANES


def _attn_specs(h, s, tq):
    qblk = lambda c, part: pl.BlockSpec((None, tq, c), lambda hh, i: (hh, i, part))
    kblk = lambda part: pl.BlockSpec((None, s, LANES), lambda hh, i: (hh, 0, part))
    row64 = pl.BlockSpec((tq, QK_ROPE), lambda hh, i: (i, 0))
    return [qblk(LANES, 0), qblk(LANES, 1), kblk(0), kblk(1), pl.BlockSpec((s, QK_ROPE), lambda hh, i: (0, 0)),
            row64, row64, pl.BlockSpec((QK_ROPE, QK_ROPE), lambda hh, i: (0, 0))]


def _attn_levels(s, tq):
    nq = s // tq
    n_lev = min(ATTN_LEVELS, nq)
    per = nq // n_lev
    return [(lv * per, (lv + 1) * per, (lv + 1) * per * tq) for lv in range(n_lev)]


def _attn_scores(qn_b, qp_b, kn_ref, kp_ref, klen, i, tq):
    scale = (QK_NOPE + QK_ROPE) ** -0.5
    kn_b = kn_ref[0:klen, :].astype(BF16)
    kp_b = kp_ref[0:klen, :].astype(BF16)
    sc = lax.dot_general(qn_b, kn_b, (((1,), (1,)), ((), ())), preferred_element_type=F32)
    sc = sc + lax.dot_general(qp_b, kp_b, (((1,), (1,)), ((), ())), preferred_element_type=F32)
    row = i * tq + lax.broadcasted_iota(jnp.int32, sc.shape, 0)
    col = lax.broadcasted_iota(jnp.int32, sc.shape, 1)
    return jnp.where(row >= col, sc * scale, NEG_INF), scale, kn_b, kp_b


def attn_fwd(q_h, kv_h, kp, cos2, sin2, rot, comm=None):
    h, s, _ = q_h.shape
    tq = _pick(s, ATTN_BLOCK, SUBLANES)

    def body(qn_ref, qp_ref, kn_ref, v_ref, kp_ref, cos_ref, sin_ref, rot_ref, o_ref, lse_ref):
        i = pl.program_id(1)
        qn_b = qn_ref[...].astype(BF16)
        qp_b = _rope(qp_ref[:, :QK_ROPE], cos_ref[...], sin_ref[...], rot_ref[...]).astype(BF16)

        def level(klen):
            sc, _, _, _ = _attn_scores(qn_b, qp_b, kn_ref, kp_ref, klen, i, tq)
            mx = jnp.max(sc, axis=-1, keepdims=True)
            e = jnp.exp(sc - mx)
            den = jnp.sum(e, axis=-1, keepdims=True)
            o_ref[...] = jnp.dot((e / den).astype(BF16), v_ref[0:klen, :].astype(BF16), preferred_element_type=F32)
            lse_ref[...] = mx + jnp.log(den)

        for lo, hi, klen in _attn_levels(s, tq):
            pl.when((i >= lo) & (i < hi))(functools.partial(level, klen))

    oblk = lambda c: pl.BlockSpec((None, tq, c), lambda hh, i: (hh, i, 0))
    return _pallas(
        body, name="mla_attn_fwd", grid=(h, s // tq), in_specs=_attn_specs(h, s, tq),
        out_specs=[oblk(V_HEAD), oblk(1)],
        out_shape=[jax.ShapeDtypeStruct((h, s, V_HEAD), F32), jax.ShapeDtypeStruct((h, s, 1), F32)],
        sem=("parallel", "parallel"), comm=comm,
    )(q_h, q_h, kv_h, kv_h, kp, cos2, sin2, rot)


def attn_bwd(q_h, kv_h, kp, cos2, sin2, rot, o, lse, do, comm=None):
    h, s, _ = q_h.shape
    tq = _pick(s, ATTN_BLOCK, SUBLANES)
    nq = s // tq

    def body(qn_ref, qp_ref, kn_ref, v_ref, kp_ref, cos_ref, sin_ref, rot_ref, o_ref, lse_ref, do_ref,
             dq_ref, dkv_ref, dkp_ref, dkv_acc, dkp_acc):
        i = pl.program_id(1)

        @pl.when(i == 0)
        def _():
            dkv_acc[...] = jnp.zeros_like(dkv_acc)
            dkp_acc[...] = jnp.zeros_like(dkp_acc)

        cosv, sinv, rotv = cos_ref[...], sin_ref[...], rot_ref[...]
        qn_b = qn_ref[...].astype(BF16)
        qp_b = _rope(qp_ref[:, :QK_ROPE], cosv, sinv, rotv).astype(BF16)
        dov = do_ref[...]
        do_b = dov.astype(BF16)
        delta = jnp.sum(dov * o_ref[...], axis=-1, keepdims=True)
        lsev = lse_ref[...]

        def level(klen):
            sc, scale, kn_b, kp_b = _attn_scores(qn_b, qp_b, kn_ref, kp_ref, klen, i, tq)
            p = jnp.exp(sc - lsev)
            dp = lax.dot_general(do_b, v_ref[0:klen, :].astype(BF16), (((1,), (1,)), ((), ())), preferred_element_type=F32)
            ds = (p * (dp - delta) * scale).astype(BF16)
            dkv_acc[0:klen, :LANES] += lax.dot_general(ds, qn_b, (((0,), (0,)), ((), ())), preferred_element_type=F32)
            dkv_acc[0:klen, LANES:] += lax.dot_general(p.astype(BF16), do_b, (((0,), (0,)), ((), ())), preferred_element_type=F32)
            dkp_acc[0:klen, :] += lax.dot_general(ds, qp_b, (((0,), (0,)), ((), ())), preferred_element_type=F32)
            dqp = jnp.dot(ds, kp_b, preferred_element_type=F32)
            dqp_raw = dqp * cosv + lax.dot_general(dqp * sinv, rotv, (((1,), (1,)), ((), ())), precision=HIGHEST,
                                                   preferred_element_type=F32)
            dq_ref[:, :QK_NOPE] = jnp.dot(ds, kn_b, preferred_element_type=F32).astype(dq_ref.dtype)
            dq_ref[:, QK_NOPE:QK_NOPE + QK_ROPE] = dqp_raw.astype(dq_ref.dtype)
            dq_ref[:, QK_NOPE + QK_ROPE:] = jnp.zeros((tq, ATTN_SLAB - QK_NOPE - QK_ROPE), dq_ref.dtype)

        for lo, hi, klen in _attn_levels(s, tq):
            pl.when((i >= lo) & (i < hi))(functools.partial(level, klen))

        @pl.when(i == nq - 1)
        def _():
            dkv_ref[...] = dkv_acc[...].astype(dkv_ref.dtype)
            dkp_ref[...] = dkp_acc[...]

    rblk = lambda c: pl.BlockSpec((None, tq, c), lambda hh, i: (hh, i, 0))
    sblk = lambda c: pl.BlockSpec((None, s, c), lambda hh, i: (hh, 0, 0))
    return _pallas(
        body, name="mla_attn_bwd", grid=(h, nq),
        in_specs=_attn_specs(h, s, tq) + [rblk(V_HEAD), rblk(1), rblk(V_HEAD)],
        out_specs=[rblk(ATTN_SLAB), sblk(ATTN_SLAB), sblk(QK_ROPE)],
        out_shape=[jax.ShapeDtypeStruct((h, s, ATTN_SLAB), BF16), jax.ShapeDtypeStruct((h, s, ATTN_SLAB), BF16),
                   jax.ShapeDtypeStruct((h, s, QK_ROPE), F32)],
        scratch_shapes=[pltpu.VMEM((s, ATTN_SLAB), F32), pltpu.VMEM((s, QK_ROPE), F32)],
        sem=("parallel", "arbitrary"), comm=comm,
    )(q_h, q_h, kv_h, kv_h, kp, cos2, sin2, rot, o, lse, do)


def _my_pos():
    return lax.axis_index("x"), lax.axis_index("y"), lax.axis_index("c")


def _dev_index(px, py, pc):
    return 4 * px + 2 * py + pc


def all_gather(name, shard):
    r, c = shard.shape

    def body(x_ref, out_ref, send_sems, recv_sems, local_sem):
        x, y, cc = _my_pos()
        me, sibling = (x, y, cc), (x, y, 1 - cc)
        chips = [(1 - x, y), (x, 1 - y), (1 - x, 1 - y)]

        def rows(px, py, pc):
            return out_ref.at[_dev_index(px, py, pc)]

        def copy(kk, block, to, src=None):
            return pltpu.make_async_remote_copy(
                src_ref=rows(*block) if src is None else src, dst_ref=rows(*block),
                send_sem=send_sems.at[kk], recv_sem=recv_sems.at[kk], device_id=to, device_id_type=MESH)

        mine = pltpu.make_async_copy(x_ref, rows(*me), local_sem)
        mine.start()
        first = [copy(0, me, sibling, src=x_ref)]
        first += [copy(1 + j, me, (*chip, cc), src=x_ref) for j, chip in enumerate(chips)]
        for cp in first:
            cp.start()
        passed = [copy(4 + j, (*chip, cc), sibling) for j, chip in enumerate(chips)]
        for j, chip in enumerate(chips):
            copy(1 + j, (*chip, cc), me).wait_recv()
            passed[j].start()
        copy(0, sibling, me).wait_recv()
        for j, chip in enumerate(chips):
            copy(4 + j, (*chip, 1 - cc), me).wait_recv()
        for cp in first + passed:
            cp.wait_send()
        mine.wait()

    return pl.pallas_call(
        body, name=name, out_shape=jax.ShapeDtypeStruct((N_DEV, r, c), shard.dtype),
        in_specs=[pl.BlockSpec(memory_space=pl.ANY)], out_specs=pl.BlockSpec(memory_space=pl.ANY),
        scratch_shapes=[pltpu.SemaphoreType.DMA((7,)), pltpu.SemaphoreType.DMA((7,)), pltpu.SemaphoreType.DMA],
    )(shard)


def exchange_sibling(name, g):
    _, r, c = g.shape

    def body(g_ref, out_ref, send_sems, recv_sems):
        x, y, cc = _my_pos()
        copies = []
        for px in range(2):
            for py in range(2):
                slot = 2 * px + py
                copies.append(pltpu.make_async_remote_copy(
                    src_ref=g_ref.at[_dev_index(px, py, 1 - cc)], dst_ref=out_ref.at[slot],
                    send_sem=send_sems.at[slot], recv_sem=recv_sems.at[slot], device_id=(x, y, 1 - cc), device_id_type=MESH))
        for cp in copies:
            cp.start()
        for cp in copies:
            cp.wait()

    return pl.pallas_call(
        body, name=name, out_shape=jax.ShapeDtypeStruct((4, r, c), g.dtype),
        in_specs=[pl.BlockSpec(memory_space=pl.ANY)], out_specs=pl.BlockSpec(memory_space=pl.ANY),
        scratch_shapes=[pltpu.SemaphoreType.DMA((4,)), pltpu.SemaphoreType.DMA((4,))],
    )(g)


def exchange_chips(name, hsum):
    _, r, c = hsum.shape

    def body(h_ref, out_ref, send_sems, recv_sems):
        x, y, cc = _my_pos()
        copies = []
        for j, (px, py) in enumerate([(1 - x, y), (x, 1 - y), (1 - x, 1 - y)]):
            copies.append(pltpu.make_async_remote_copy(
                src_ref=h_ref.at[2 * px + py], dst_ref=out_ref.at[j],
                send_sem=send_sems.at[j], recv_sem=recv_sems.at[j], device_id=(px, py, cc), device_id_type=MESH))
        for cp in copies:
            cp.start()
        for cp in copies:
            cp.wait()

    return pl.pallas_call(
        body, name=name, out_shape=jax.ShapeDtypeStruct((3, r, c), hsum.dtype),
        in_specs=[pl.BlockSpec(memory_space=pl.ANY)], out_specs=pl.BlockSpec(memory_space=pl.ANY),
        scratch_shapes=[pltpu.SemaphoreType.DMA((3,)), pltpu.SemaphoreType.DMA((3,))],
    )(hsum)


def _rs_add_sibling(name, g, from_sibling, cc):
    _, r, c = g.shape
    tr = _pick(r, 512, SUBLANES * 2)

    def body(cc_ref, g_ref, s_ref, o_ref, ob_ref):
        tot = g_ref[...] + s_ref[...]
        o_ref[...] = tot
        ob_ref[...] = tot.astype(BF16)

    blk = pl.BlockSpec((None, tr, c), lambda s_, i, cc_ref: (s_, i, 0))
    return pl.pallas_call(
        body, name=name,
        grid_spec=pltpu.PrefetchScalarGridSpec(
            num_scalar_prefetch=1, grid=(4, r // tr),
            in_specs=[pl.BlockSpec((None, None, tr, c), lambda s_, i, cc_ref: (s_, cc_ref[0], i, 0)), blk], out_specs=[blk, blk]),
        out_shape=[jax.ShapeDtypeStruct((4, r, c), F32), jax.ShapeDtypeStruct((4, r, c), BF16)],
        compiler_params=_params(("parallel", "parallel")),
    )(cc.reshape(1).astype(jnp.int32), g.reshape(4, 2, r, c), from_sibling)


def _rs_add_chips(name, chip_sum, from_chips, slot):
    _, r, c = chip_sum.shape
    tr = _pick(r, 512, SUBLANES * 2)

    def body(slot_ref, h_ref, f0_ref, f1_ref, f2_ref, o_ref):
        o_ref[...] = ((h_ref[...] + f0_ref[...].astype(F32)) + f1_ref[...].astype(F32)) + f2_ref[...].astype(F32)

    def from_blk(j):
        return pl.BlockSpec((None, tr, c), lambda i, slot_ref: (j, i, 0))

    return pl.pallas_call(
        body, name=name,
        grid_spec=pltpu.PrefetchScalarGridSpec(
            num_scalar_prefetch=1, grid=(r // tr,),
            in_specs=[pl.BlockSpec((None, tr, c), lambda i, slot_ref: (slot_ref[0], i, 0)), from_blk(0), from_blk(1), from_blk(2)],
            out_specs=pl.BlockSpec((tr, c), lambda i, slot_ref: (i, 0))),
        out_shape=jax.ShapeDtypeStruct((r, c), F32), compiler_params=_params(("parallel",)),
    )(slot.reshape(1).astype(jnp.int32), chip_sum, from_chips, from_chips, from_chips)


def adamw_scatter(name, w, m, v, chip_sum, from_chips, slot):
    _, a, b = w.shape
    _, r, c = chip_sum.shape
    assert a <= r and b <= c, (name, w.shape, chip_sum.shape)
    tr = _pick(a, 256, SUBLANES * 2)

    def body(slot_ref, h_ref, f0_ref, f1_ref, f2_ref, w_ref, m_ref, v_ref, g_out, d_out, m_out, v_out):
        g = ((h_ref[...] + f0_ref[...].astype(F32)) + f1_ref[...].astype(F32)) + f2_ref[...].astype(F32)
        g = g[:, :b]
        g_out[...] = g
        d_out[...], m_out[...], v_out[...] = _adamw_fn(w_ref[...], g, m_ref[...], v_ref[...])

    def from_blk(j):
        return pl.BlockSpec((None, tr, c), lambda i, slot_ref: (j, i, 0))

    mine = pl.BlockSpec((None, tr, b), lambda i, slot_ref: (0, i, 0))
    return pl.pallas_call(
        body, name=name,
        grid_spec=pltpu.PrefetchScalarGridSpec(
            num_scalar_prefetch=1, grid=(a // tr,),
            in_specs=[pl.BlockSpec((None, tr, c), lambda i, slot_ref: (slot_ref[0], i, 0)), from_blk(0), from_blk(1), from_blk(2),
                      mine, mine, mine],
            out_specs=[mine] * 4),
        out_shape=[jax.ShapeDtypeStruct((1, a, b), F32)] * 4, compiler_params=_params(("parallel",)),
    )(slot.reshape(1).astype(jnp.int32), chip_sum, from_chips, from_chips, from_chips, w, m, v)


def rs_chip_sum(tag, g):
    _, _, cc = _my_pos()
    from_sibling = exchange_sibling("rs_sibling_" + tag, g)
    return _rs_add_sibling("rs_add_sibling_" + tag, g, from_sibling, cc)


def rs_finish(tag, chip_sum, from_chips):
    x, y, _ = _my_pos()
    return _rs_add_chips("rs_add_chips_" + tag, chip_sum, from_chips, 2 * x + y)


def reduce_scatter(tag, g):
    chip_sum, chip_sum_b = rs_chip_sum(tag, g)
    return rs_finish(tag, chip_sum, exchange_chips("rs_chips_" + tag, chip_sum_b))


class GatherIci:
    def __init__(self, shards):
        self.inputs = list(shards)
        self.out_shapes = [jax.ShapeDtypeStruct((N_DEV,) + s.shape, s.dtype) for s in shards]
        self.n_remote, self.n_local = 3 * len(shards), len(shards)

    def make(self, cins, couts, send, recv, local):
        x, y, cc = _my_pos()
        me = _dev_index(x, y, cc)
        copies = []
        for w, (src, out) in enumerate(zip(cins, couts, strict=True)):
            copies.append(pltpu.make_async_copy(src, out.at[me], local.at[w]))
            for j, (px, py) in enumerate([(1 - x, y), (x, 1 - y), (1 - x, 1 - y)]):
                copies.append(pltpu.make_async_remote_copy(
                    src_ref=src, dst_ref=out.at[me], send_sem=send.at[3 * w + j], recv_sem=recv.at[3 * w + j],
                    device_id=(px, py, cc), device_id_type=MESH))
        return copies


class RsChips:
    def __init__(self, chip_sums):
        self.inputs = list(chip_sums)
        self.out_shapes = [jax.ShapeDtypeStruct((3,) + h.shape[1:], h.dtype) for h in chip_sums]
        self.n_remote, self.n_local = 3 * len(chip_sums), 0

    def make(self, cins, couts, send, recv, local):
        x, y, cc = _my_pos()
        copies = []
        for w, (h_ref, out) in enumerate(zip(cins, couts, strict=True)):
            for j, (px, py) in enumerate([(1 - x, y), (x, 1 - y), (1 - x, 1 - y)]):
                copies.append(pltpu.make_async_remote_copy(
                    src_ref=h_ref.at[2 * px + py], dst_ref=out.at[j], send_sem=send.at[3 * w + j], recv_sem=recv.at[3 * w + j],
                    device_id=(px, py, cc), device_id_type=MESH))
        return copies


class SiblingSwap:
    def __init__(self, gs):
        self.inputs = list(gs)
        self.out_shapes = [jax.ShapeDtypeStruct((4,) + g.shape[1:], g.dtype) for g in gs]
        self.n_remote, self.n_local = 4 * len(gs), 0

    def make(self, cins, couts, send, recv, local):
        x, y, cc = _my_pos()
        copies = []
        for w, (g_ref, out) in enumerate(zip(cins, couts, strict=True)):
            for px in range(2):
                for py in range(2):
                    q = 4 * w + 2 * px + py
                    copies.append(pltpu.make_async_remote_copy(
                        src_ref=g_ref.at[_dev_index(px, py, 1 - cc)], dst_ref=out.at[2 * px + py],
                        send_sem=send.at[q], recv_sem=recv.at[q], device_id=(x, y, 1 - cc), device_id_type=MESH))
        return copies


class _SemSlice:
    def __init__(self, base, start):
        self.base, self.start = base, start

    @property
    def at(self):
        return self

    def __getitem__(self, k):
        return self.base.at[self.start + k]


class CommGroup:
    def __init__(self, plans):
        self.plans = [p for p in plans if p.inputs]
        self.inputs = [a for p in self.plans for a in p.inputs]
        self.out_shapes = [s_ for p in self.plans for s_ in p.out_shapes]
        self.n_remote = sum(p.n_remote for p in self.plans)
        self.n_local = sum(p.n_local for p in self.plans)

    def make(self, cins, couts, send, recv, local):
        copies, i0, o0, r0, l0 = [], 0, 0, 0, 0
        for p in self.plans:
            ni, no = len(p.inputs), len(p.out_shapes)
            copies += p.make(cins[i0:i0 + ni], couts[o0:o0 + no], _SemSlice(send, r0), _SemSlice(recv, r0), _SemSlice(local, l0))
            i0, o0, r0, l0 = i0 + ni, o0 + no, r0 + p.n_remote, l0 + p.n_local
        return copies


def gather_d2d(name, arrays):
    n = len(arrays)

    def body(*refs):
        outs, send, recv = refs[n:2 * n], refs[2 * n], refs[2 * n + 1]
        x, y, cc = _my_pos()
        copies = []
        for w, out in enumerate(outs):
            for px in range(2):
                for py in range(2):
                    q = 4 * w + 2 * px + py
                    slab = out.at[_dev_index(px, py, cc)]
                    copies.append(pltpu.make_async_remote_copy(
                        src_ref=slab, dst_ref=slab, send_sem=send.at[q], recv_sem=recv.at[q],
                        device_id=(x, y, 1 - cc), device_id_type=MESH))
        for cp in copies:
            cp.start()
        for cp in copies:
            cp.wait()

    any_spec = pl.BlockSpec(memory_space=pl.ANY)
    return pl.pallas_call(
        body, name=name, out_shape=[jax.ShapeDtypeStruct(a.shape, a.dtype) for a in arrays],
        in_specs=[any_spec] * n, out_specs=[any_spec] * n, input_output_aliases={i: i for i in range(n)},
        scratch_shapes=[pltpu.SemaphoreType.DMA((4 * n,)), pltpu.SemaphoreType.DMA((4 * n,))],
    )(*arrays)


def _pallas(body, *, name, grid, in_specs, out_specs, out_shape, scratch_shapes=(), sem, comm=None):
    in_specs, out_specs, out_shape, scratch_shapes = list(in_specs), list(out_specs), list(out_shape), list(scratch_shapes)
    if comm is None:
        return pl.pallas_call(body, name=name, grid=grid, in_specs=in_specs, out_specs=out_specs, out_shape=out_shape,
                              scratch_shapes=scratch_shapes, compiler_params=_params(sem))
    n_in, n_out, n_scr = len(in_specs), len(out_specs), len(scratch_shapes)
    nci, nco = len(comm.inputs), len(comm.out_shapes)

    def body2(*refs):
        ins, cins = refs[:n_in], refs[n_in:n_in + nci]
        o0 = n_in + nci
        outs, couts = refs[o0:o0 + n_out], refs[o0 + n_out:o0 + n_out + nco]
        s0 = o0 + n_out + nco
        scr = refs[s0:s0 + n_scr]
        send, recv, local = refs[s0 + n_scr:]
        pids = [pl.program_id(k) for k in range(len(grid))]
        first = functools.reduce(jnp.logical_and, [p == 0 for p in pids])
        last = functools.reduce(jnp.logical_and, [p == g - 1 for p, g in zip(pids, grid)])

        @pl.when(first)
        def _():
            for cp in comm.make(cins, couts, send, recv, local):
                cp.start()

        body(*ins, *outs, *scr)

        @pl.when(last)
        def _():
            for cp in comm.make(cins, couts, send, recv, local):
                cp.wait()

    any_spec = pl.BlockSpec(memory_space=pl.ANY)
    call = pl.pallas_call(
        body2, name=name, grid=grid, in_specs=in_specs + [any_spec] * nci, out_specs=out_specs + [any_spec] * nco,
        out_shape=out_shape + list(comm.out_shapes),
        scratch_shapes=scratch_shapes + [pltpu.SemaphoreType.DMA((comm.n_remote,)), pltpu.SemaphoreType.DMA((comm.n_remote,)),
                                         pltpu.SemaphoreType.DMA((max(comm.n_local, 1),))],
        compiler_params=_params(tuple("arbitrary" for _ in grid)))
    return lambda *args: call(*args, *comm.inputs)


PACK_W = 1024


class Pack:
    def __init__(self, entries, row_unit):
        self.entries = entries
        self.sizes = [int(np.prod(sh)) for _, sh in entries]
        self.offsets = np.concatenate([[0], np.cumsum(self.sizes)]).tolist()
        self.total = _round_up(self.offsets[-1], PACK_W * row_unit)
        self.rows = self.total // PACK_W

    def pack(self, arrays, dtype, lead=()):
        flat = [arrays[n].astype(dtype).reshape(lead + (-1,)) for n, _ in self.entries]
        pad = self.total - self.offsets[-1]
        if pad:
            flat.append(jnp.zeros(lead + (pad,), dtype))
        return jnp.concatenate(flat, axis=-1).reshape(lead + (self.rows, PACK_W))

    def unpack(self, buf, lead=()):
        flat = buf.reshape(lead + (self.total,))
        out = {}
        for (n, sh), off, sz in zip(self.entries, self.offsets, self.sizes):
            out[n] = lax.slice_in_dim(flat, off, off + sz, axis=len(lead)).reshape(lead + tuple(sh))
        return out


def _gathered_to_full(g, how):
    _, a, b = g.shape
    if how == 'row':
        return g.reshape(N_DEV * a, b)
    return jnp.transpose(g, (1, 0, 2)).reshape(a, N_DEV * b)


def _full_to_shards(w, how):
    a, b = w.shape
    if how == 'row':
        return w.reshape(N_DEV, a // N_DEV, b)
    return jnp.transpose(w.reshape(a, N_DEV, b // N_DEV), (1, 0, 2))


def _to_heads(t, width):
    s, c = t.shape
    return jnp.transpose(t.reshape(s, c // width, width), (1, 0, 2))


def _from_heads(t):
    h, s, w = t.shape
    return jnp.transpose(t, (1, 0, 2)).reshape(s, h * w)


def _rot_matrix():
    half = QK_ROPE // 2
    rot = np.zeros((QK_ROPE, QK_ROPE), np.float32)
    for i in range(half):
        rot[i + half, i] = -1.0
        rot[i, i + half] = 1.0
    return jnp.asarray(rot)


def _inv_freq2():
    half = QK_ROPE // 2
    inv = ROPE_THETA ** (-np.arange(half, dtype=np.float32) / half)
    return jnp.asarray(np.concatenate([inv, inv])[None, :].astype(np.float32))


def kernel(x, positions, attn_norm_g, w_in, rwkv_mu, rwkv_w0, rwkv_w2, rwkv_a0, rwkv_a2, rwkv_g2, rwkv_k_k, rwkv_k_a, rwkv_r_k, rwkv_gn_w, rwkv_gn_b, mla_q_norm_g, mla_w_uq, mla_kv_norm_g, mla_w_ukv, w_out, ffn_norm_g, ffn_w_gate, ffn_w_up, ffn_conv_w, ffn_conv_b, ffn_w_down, final_norm_g, loss_target, m_attn_norm_g, m_w_in, m_rwkv_mu, m_rwkv_w0, m_rwkv_w2, m_rwkv_a0, m_rwkv_a2, m_rwkv_g2, m_rwkv_k_k, m_rwkv_k_a, m_rwkv_r_k, m_rwkv_gn_w, m_rwkv_gn_b, m_mla_q_norm_g, m_mla_w_uq, m_mla_kv_norm_g, m_mla_w_ukv, m_w_out, m_ffn_norm_g, m_ffn_w_gate, m_ffn_w_up, m_ffn_conv_w, m_ffn_conv_b, m_ffn_w_down, m_final_norm_g, v_attn_norm_g, v_w_in, v_rwkv_mu, v_rwkv_w0, v_rwkv_w2, v_rwkv_a0, v_rwkv_a2, v_rwkv_g2, v_rwkv_k_k, v_rwkv_k_a, v_rwkv_r_k, v_rwkv_gn_w, v_rwkv_gn_b, v_mla_q_norm_g, v_mla_w_uq, v_mla_kv_norm_g, v_mla_w_ukv, v_w_out, v_ffn_norm_g, v_ffn_w_gate, v_ffn_w_up, v_ffn_conv_w, v_ffn_conv_b, v_ffn_w_down, v_final_norm_g):
    given = dict(locals())
    wts = {n: given[n] for n in WEIGHTS}
    mom_m = {n: given["m_" + n] for n in WEIGHTS}
    mom_v = {n: given["v_" + n] for n in WEIGHTS}
    out_shapes = {n: wts[n].shape for n in WEIGHTS}

    def local2d(n, a):
        if n == 'rwkv_r_k' or a.ndim <= 2:
            return a.reshape(1, -1)
        return a.reshape(a.shape[1:])

    w2d = {n: local2d(n, wts[n]) for n in WEIGHTS}
    m2d = {n: local2d(n, mom_m[n]) for n in WEIGHTS}
    v2d = {n: local2d(n, mom_v[n]) for n in WEIGHTS}

    xs = x.reshape(x.shape[1:])
    tgt = loss_target.reshape(loss_target.shape[1:])
    s, d = xs.shape
    c_rwkv = w2d['rwkv_w0'].shape[1]
    n_rh = c_rwkv // RWKV_HEAD
    decay_lora, aaa_lora, gate_lora = w2d['rwkv_w2'].shape[0], w2d['rwkv_a2'].shape[0], w2d['rwkv_g2'].shape[0]
    q_lora, kv_lora = w2d['mla_q_norm_g'].shape[1], w2d['mla_kv_norm_g'].shape[1]
    shift_dim = w2d['rwkv_mu'].shape[1]
    d_in = w2d['w_in'].shape[1] * N_DEV
    n_mh = w2d['mla_w_uq'].shape[1] * N_DEV // (QK_NOPE + QK_ROPE)
    tm = _pick(s, 256, SUBLANES)
    tm_wide = _pick(s, 128, SUBLANES)
    tm_heads = _pick(s, 512, SUBLANES)

    nb = {n: w2d[n].shape[1] for n in BIG if BIG[n] == 'col'}
    nbp = {n: _round_up(v_, LANES) for n, v_ in nb.items()}
    shards = {}
    for n in BIG:
        w = w2d[n].astype(BF16)
        if BIG[n] == 'col':
            w = jnp.pad(w, ((0, 0), (0, nbp[n] - nb[n])))
        elif n == 'ffn_w_down':
            w = jnp.pad(w, ((0, nbp['ffn_w_gate'] - w.shape[0]), (0, 0)))
        shards[n] = w

    def as_used(n, g):
        return g if BIG[n] == 'col' else g.reshape(N_DEV * g.shape[1], g.shape[2])

    gathered = {'w_in': as_used('w_in', all_gather("gather_w_in", shards['w_in']))}
    f_pad = N_DEV * nbp['ffn_w_gate']
    small_pack = Pack([(n, w2d[n].shape) for n in SMALL_SHARDED], 8)
    small_all = all_gather("gather_small", small_pack.pack(w2d, F32))
    full = {}
    for n, g in small_pack.unpack(small_all, lead=(N_DEV,)).items():
        full[n] = _gathered_to_full(g, SMALL_SHARDED[n])
    conv_w_pad = pad_cols(full['ffn_conv_w'], nb['ffn_w_gate'], nbp['ffn_w_gate'])
    conv_b_pad = pad_cols(w2d['ffn_conv_b'], nb['ffn_w_gate'], nbp['ffn_w_gate'])

    (h1,) = rowwise("rms_attn", _rms_fn, [xs, w2d['attn_norm_g']], ['row', 'const'], [('row', d, BF16)], heads=1, s=s, tm=tm)
    def gather_behind(names, run):
        *res, = run(GatherIci([shards[n] for n in names]))
        landed = res[len(res) - len(names):]
        for n, g in zip(names, gather_d2d("gather_d2d_" + names[0], landed), strict=True):
            gathered[n] = as_used(n, g)
        return res[:len(res) - len(names)]

    w_in_nat = jnp.transpose(gathered['w_in'], (1, 0, 2))[:, :, :nb['w_in']].reshape(d, d_in)
    w_in_nat = jnp.pad(w_in_nat, ((0, 0), (0, _round_up(d_in, LANES) - d_in)))
    (proj,) = gather_behind(['mla_w_uq', 'mla_w_ukv', 'w_out'], lambda c: mm("proj_in", h1, w_in_nat, comm=c))
    p_rwkv = proj[:, :shift_dim]
    c_q = proj[:, shift_dim:shift_dim + q_lora]
    c_kv = proj[:, shift_dim + q_lora:shift_dim + q_lora + kv_lora]
    k_pe = proj[:, shift_dim + q_lora + kv_lora:d_in]
    shifted = token_shift_fwd(p_rwkv, w2d['rwkv_mu'], tm_wide)
    o1, o2, o3 = c_rwkv, 2 * c_rwkv, 3 * c_rwkv
    hr = _to_heads(shifted[:, :o1], RWKV_HEAD)
    hk = _to_heads(shifted[:, o1:o2], RWKV_HEAD)
    hv = _to_heads(shifted[:, o2:o3], RWKV_HEAD)
    hw = shifted[:, o3:o3 + decay_lora]
    ha = shifted[:, o3 + decay_lora:o3 + decay_lora + aaa_lora]
    hg = shifted[:, o3 + decay_lora + aaa_lora:]

    def per_head(vec):
        return vec.reshape(n_rh, 1, RWKV_HEAD)

    def lora_heads(w):
        return jnp.transpose(w.reshape(w.shape[0], n_rh, RWKV_HEAD), (1, 0, 2))

    pre_args = [hk, hw, ha, hg, per_head(w2d['rwkv_w0']), lora_heads(full['rwkv_w2']), per_head(w2d['rwkv_a0']),
                lora_heads(full['rwkv_a2']), lora_heads(full['rwkv_g2']), per_head(w2d['rwkv_k_k']), per_head(w2d['rwkv_k_a'])]
    pre_kinds = ['hrow', 'row', 'row', 'row', 'hconst', 'hconst', 'hconst', 'hconst', 'hconst', 'hconst', 'hconst']
    decay, kx, a_sc, b_sc, gate_r = gather_behind(['ffn_w_gate'], lambda c: rowwise(
        "rwkv_pre", _rwkv_pre_fn, pre_args, pre_kinds, [('hrow', RWKV_HEAD, F32)] * 5, heads=n_rh, s=s, tm=tm_heads, comm=c))
    y_scan, ckpt = gather_behind(['ffn_w_up'], lambda c: scan_fwd(hr, decay, kx, hv, a_sc, b_sc, comm=c))
    post_args = [y_scan, hr, kx, hv, gate_r, per_head(w2d['rwkv_gn_w']), per_head(w2d['rwkv_gn_b']), per_head(w2d['rwkv_r_k'])]
    post_kinds = ['hrow'] * 5 + ['hconst'] * 3
    (y_rwkv_h,) = rowwise("rwkv_post", _rwkv_post_fn, post_args, post_kinds, [('hrow', RWKV_HEAD, F32)], heads=n_rh, s=s, tm=tm_heads)

    pos = positions.reshape(s, 1).astype(F32)
    rot, inv2 = _rot_matrix(), _inv_freq2()
    mla_args = [c_q, c_kv, k_pe, pos, w2d['mla_q_norm_g'], w2d['mla_kv_norm_g'], inv2, rot]
    mla_kinds = ['row', 'row', 'row', 'row', 'const', 'const', 'const', 'const']
    qn, kvn, kp_rot, cos2, sin2 = rowwise(
        "mla_pre", _mla_pre_fn, mla_args, mla_kinds,
        [('row', q_lora, BF16), ('row', kv_lora, BF16), ('row', QK_ROPE, F32), ('row', QK_ROPE, F32), ('row', QK_ROPE, F32)],
        heads=1, s=s, tm=tm)
    assert n_mh == N_DEV and nb['mla_w_uq'] == QK_NOPE + QK_ROPE and nb['mla_w_ukv'] == QK_NOPE + V_HEAD
    assert nbp['mla_w_uq'] == ATTN_SLAB and nbp['mla_w_ukv'] == ATTN_SLAB
    q_h = mm_sh("proj_q", qn, gathered['mla_w_uq'], slabs=True)
    kv_h = mm_sh("proj_kv", kvn, gathered['mla_w_ukv'], slabs=True)
    o_att, lse = gather_behind(['ffn_w_down'], lambda c: attn_fwd(q_h, kv_h, kp_rot, cos2, sin2, rot, comm=c))
    ycat = jnp.concatenate([_from_heads(y_rwkv_h), _from_heads(o_att)], axis=-1).astype(BF16)
    x1 = mm("proj_out", ycat, gathered['w_out'], add=xs)
    (h2,) = rowwise("rms_ffn", _rms_fn, [x1, w2d['ffn_norm_g']], ['row', 'const'], [('row', d, BF16)], heads=1, s=s, tm=tm)
    gate_pre = mm_sh("ffn_gate", h2, gathered['ffn_w_gate'])
    up = mm_sh("ffn_up", h2, gathered['ffn_w_up'])
    act = ffn_act_fwd(gate_pre, up, conv_w_pad, conv_b_pad)
    x2 = mm("ffn_down", act, gathered['ffn_w_down'], add=x1)

    ones = jnp.ones((s, 1), F32)
    fin_g = w2d['final_norm_g']
    d_x2, dg_final_p, loss_rows = rowwise_vjp("loss_bwd", _loss_fn, [x2, fin_g, tgt], ['row', 'const', 'row'], [ones], ['row'],
                                              [0, 1], heads=1, s=s, tm=tm, primal=True)
    d_x2_b = d_x2.astype(BF16)
    d_act = mm_nt("d_act", d_x2_b, gathered['ffn_w_down'], out_dtype=BF16)
    gsh, chip_sums, from_chips = {}, {}, {}

    def scatter_behind(run, ici=(), swap=()):
        *res, = run(CommGroup([RsChips([chip_sums[n][1] for n in ici]), SiblingSwap([gsh[n] for n in swap])]))
        n_own = len(res) - len(ici) - len(swap)
        from_chips.update(zip(ici, res[n_own:n_own + len(ici)], strict=True))
        _, _, cc = _my_pos()
        for n, from_sibling in zip(swap, res[n_own + len(ici):], strict=True):
            chip_sums[n] = _rs_add_sibling("rs_add_sibling_" + n, gsh[n], from_sibling, cc)
        return res[:n_own]

    gsh['ffn_w_down'] = mm_tn("dw_down", act, d_x2_b).reshape(N_DEV, nbp['ffn_w_gate'], d)
    d_gate, d_up, dcw_p, dcb_p = ffn_act_bwd1(gate_pre, up, conv_w_pad, conv_b_pad, d_act)
    d_gp = ffn_act_bwd2(d_gate, conv_w_pad)
    (d_h2_g,) = scatter_behind(lambda c: mm_sh_nt("d_h2_gate", d_gp, gathered['ffn_w_gate'], comm=c), swap=['ffn_w_down'])
    d_h2 = mm_sh_nt("d_h2_up", d_up, gathered['ffn_w_up'], add=d_h2_g)
    gsh['ffn_w_gate'] = mm_sh_out("dw_gate", h2, d_gp)
    (gsh['ffn_w_up'],) = scatter_behind(lambda c: mm_sh_out("dw_up", h2, d_up, comm=c), swap=['ffn_w_gate'])
    d_x1, dg_ffn_p = rowwise_vjp("rms_ffn_bwd", _rms_fn, [x1, w2d['ffn_norm_g']], ['row', 'const'], [d_h2], ['row'], [0, 1],
                                 heads=1, s=s, tm=tm, plus=d_x2)
    d_x1_b = d_x1.astype(BF16)
    d_ycat = mm_nt("d_ycat", d_x1_b, gathered['w_out'])
    gsh['w_out'] = mm_tn("dw_out", ycat, d_x1_b).reshape((N_DEV,) + w2d['w_out'].shape)
    d_yr_h = _to_heads(d_ycat[:, :c_rwkv], RWKV_HEAD)
    d_o_h = _to_heads(d_ycat[:, c_rwkv:], V_HEAD)

    d_q, d_kv, d_kp_h = scatter_behind(
        lambda c: attn_bwd(q_h, kv_h, kp_rot, cos2, sin2, rot, o_att, lse, d_o_h, comm=c),
        ici=['ffn_w_down'], swap=['ffn_w_up', 'w_out'])
    d_kp_rot = headsum("d_kpe_heads", d_kp_h)
    d_qn = mm_sh_nt("d_qn", d_q, gathered['mla_w_uq'])
    d_kvn = mm_sh_nt("d_kvn", d_kv, gathered['mla_w_ukv'])
    gsh['mla_w_uq'] = mm_sh_out("dw_uq", qn, d_q)
    gsh['mla_w_ukv'] = mm_sh_out("dw_ukv", kvn, d_kv)
    d_cq, d_ckv, d_kpe, dg_q_p, dg_kv_p = rowwise_vjp(
        "mla_pre_bwd", _mla_pre_grad_fn, mla_args, mla_kinds, [d_qn, d_kvn, d_kp_rot], ['row', 'row', 'row'], [0, 1, 2, 4, 5],
        heads=1, s=s, tm=tm)

    d_y, d_r_post, d_k_post, d_v_post, d_gate_r, dgnw_p, dgnb_p, drk_p = scatter_behind(
        lambda c: rowwise_vjp("rwkv_post_bwd", _rwkv_post_fn, post_args, post_kinds, [d_yr_h], ['hrow'], list(range(8)),
                              heads=n_rh, s=s, tm=tm_heads, comm=c),
        ici=['ffn_w_gate'], swap=['mla_w_uq', 'mla_w_ukv'])
    d_r_sc, d_w_sc, d_k_sc, d_v_sc, d_a_sc, d_b_sc = scatter_behind(
        lambda c: scan_bwd(hr, decay, kx, hv, a_sc, b_sc, ckpt, d_y, comm=c), ici=['ffn_w_up', 'w_out'])
    d_hk, d_hw_p, d_ha_p, d_hg_p, dw0_p, dw2_p, da0_p, da2_p, dg2_p, dkk_p, dka_p, d_hr, d_hv = scatter_behind(
        lambda c: rowwise_vjp(
            "rwkv_pre_bwd", _rwkv_pre_grad_fn, pre_args + [hr, hv], pre_kinds + ['hrow', 'hrow'],
            [d_w_sc, d_k_sc, d_k_post, d_a_sc, d_b_sc, d_gate_r, d_r_sc, d_r_post, d_v_sc, d_v_post], ['hrow'] * 10,
            list(range(13)), heads=n_rh, s=s, tm=tm_heads, comm=c),
        ici=['mla_w_uq', 'mla_w_ukv'])
    d_shifted = jnp.concatenate([_from_heads(d_hr), _from_heads(d_hk), _from_heads(d_hv), headsum("d_hw_heads", d_hw_p),
                                 headsum("d_ha_heads", d_ha_p), headsum("d_hg_heads", d_hg_p)], axis=-1)
    d_p_rwkv, dmu_p = token_shift_bwd(p_rwkv, w2d['rwkv_mu'], d_shifted, tm_wide)
    d_proj = pad_cols(jnp.concatenate([d_p_rwkv, d_cq, d_ckv, d_kpe], axis=-1).astype(BF16), nb['w_in'], nbp['w_in'])
    gsh['w_in'] = mm_sh_out("dw_in", h1, d_proj)
    chip_sums['w_in'] = rs_chip_sum('w_in', gsh['w_in'])
    (d_h1,) = scatter_behind(lambda c: mm_sh_nt("d_h1", d_proj, gathered['w_in'], comm=c), ici=['w_in'])
    grad_x, dg_attn_p = rowwise_vjp("rms_attn_bwd", _rms_fn, [xs, w2d['attn_norm_g']], ['row', 'const'], [d_h1], ['row'], [0, 1],
                                    heads=1, s=s, tm=tm, plus=d_x1)

    def from_heads_lora(g):
        return jnp.transpose(g, (1, 0, 2)).reshape(g.shape[1], n_rh * RWKV_HEAD)

    gw = {}
    gw['rwkv_w2'] = from_heads_lora(sum_partials("sum_dw2", dw2_p, True))
    gw['rwkv_a2'] = from_heads_lora(sum_partials("sum_da2", da2_p, True))
    gw['rwkv_g2'] = from_heads_lora(sum_partials("sum_dg2", dg2_p, True))
    dcw_pad = colsum("sum_dconv_w", dcw_p.reshape(dcw_p.shape[0], CONV_W * f_pad)).reshape(CONV_W, f_pad)
    gw['ffn_conv_w'] = unpad_cols(dcw_pad, nb['ffn_w_gate'], nbp['ffn_w_gate'])

    rep = {
        'attn_norm_g': sum_partials("sum_dg_attn", dg_attn_p, False),
        'rwkv_mu': colsum("sum_dmu", dmu_p.reshape(dmu_p.shape[0], shift_dim)),
        'rwkv_w0': sum_partials("sum_dw0", dw0_p, True).reshape(1, c_rwkv),
        'rwkv_a0': sum_partials("sum_da0", da0_p, True).reshape(1, c_rwkv),
        'rwkv_k_k': sum_partials("sum_dkk", dkk_p, True).reshape(1, c_rwkv),
        'rwkv_k_a': sum_partials("sum_dka", dka_p, True).reshape(1, c_rwkv),
        'rwkv_r_k': sum_partials("sum_drk", drk_p, True).reshape(1, c_rwkv),
        'rwkv_gn_w': sum_partials("sum_dgnw", dgnw_p, True).reshape(1, c_rwkv),
        'rwkv_gn_b': sum_partials("sum_dgnb", dgnb_p, True).reshape(1, c_rwkv),
        'mla_q_norm_g': sum_partials("sum_dg_q", dg_q_p, False),
        'mla_kv_norm_g': sum_partials("sum_dg_kv", dg_kv_p, False),
        'ffn_norm_g': sum_partials("sum_dg_ffn", dg_ffn_p, False),
        'ffn_conv_b': unpad_cols(colsum("sum_dconv_b", dcb_p.reshape(dcb_p.shape[0], f_pad)), nb['ffn_w_gate'], nbp['ffn_w_gate']),
        'final_norm_g': sum_partials("sum_dg_final", dg_final_p, False),
        'loss': sum_all("sum_loss", loss_rows.reshape(s // SUBLANES, SUBLANES)),
    }
    rep_pack = Pack([(n, w2d[n].shape) for n in REPLICATED] + [('loss', (1, 1))], 8)
    rep_all = all_gather("gather_rep_grads", rep_pack.pack(rep, F32))
    rep_sum = colsum("sum_rep_grads", rep_all.reshape(N_DEV, rep_pack.total)).reshape(rep_pack.rows, PACK_W)
    rep_g = rep_pack.unpack(rep_sum)
    loss = rep_g.pop('loss').reshape(())

    grads, deltas, new_m, new_v = dict(rep_g), {}, {}, {}
    my_x, my_y, _ = _my_pos()
    for n in BIG:
        grads[n], deltas[n], new_m[n], new_v[n] = adamw_scatter(
            "adamw_" + n, wts[n], mom_m[n], mom_v[n], chip_sums[n][0], from_chips[n], 2 * my_x + my_y)
    sm_pack = Pack([(n, w2d[n].shape) for n in SMALL_SHARDED], 8)
    g_shards = {n: _full_to_shards(gw[n], SMALL_SHARDED[n]) for n in SMALL_SHARDED}
    grads.update(sm_pack.unpack(reduce_scatter("small", sm_pack.pack(g_shards, F32, lead=(N_DEV,)))))
    rest_pack = Pack([(n, w2d[n].shape) for n in WEIGHTS if n not in BIG], 8)
    d_r, m_r, v_r = rowwise(
        "adamw_small", _adamw_fn, [rest_pack.pack(w2d, F32), rest_pack.pack(grads, F32), rest_pack.pack(m2d, F32),
                                   rest_pack.pack(v2d, F32)],
        ['row'] * 4, [('row', PACK_W, F32)] * 3, heads=1, s=rest_pack.rows, tm=_pick(rest_pack.rows, 512, SUBLANES))
    deltas.update(rest_pack.unpack(d_r))
    new_m.update(rest_pack.unpack(m_r))
    new_v.update(rest_pack.unpack(v_r))

    def shaped(dct):
        return [dct[n].reshape(out_shapes[n]) for n in WEIGHTS]

    return (loss, grad_x.reshape(x.shape), *shaped(grads), *shaped(deltas), *shaped(new_m), *shaped(new_v))
```

```python
import functools
import math

import jax
import jax.numpy as jnp
import numpy as np
from jax import lax
from jax.experimental import pallas as pl
from jax.experimental.pallas import tpu as pltpu

F32 = jnp.float32
BF16 = jnp.bfloat16
HIGHEST = lax.Precision.HIGHEST
MESH = pl.DeviceIdType.MESH

N_DEV = 8
LANES = 128
SUBLANES = 8
VMEM_LIMIT = 48 * 1024 * 1024
RESIDENT_BYTES = 8 * 1024 * 1024

NORM_EPS = 1e-6
GN_EPS = 64e-5
RWKV_HEAD = 64
QK_NOPE = 128
QK_ROPE = 64
V_HEAD = 128
ROPE_THETA = 10000.0
CONV_W = 3
NEG_INF = -1e30
SCAN_CHUNK = 64
SCAN_HEADS = 16
SCAN_PASSES_SOLVE = 3
SCAN_PASSES_OUT = 3

ADAM_LR = 0.001
ADAM_B1 = 0.9
ADAM_B2 = 0.999
ADAM_EPS = 1e-08
ADAM_WD = 0.01
ADAM_STEP = 10

WEIGHTS = ['attn_norm_g', 'w_in', 'rwkv_mu', 'rwkv_w0', 'rwkv_w2', 'rwkv_a0', 'rwkv_a2', 'rwkv_g2', 'rwkv_k_k',
           'rwkv_k_a', 'rwkv_r_k', 'rwkv_gn_w', 'rwkv_gn_b', 'mla_q_norm_g', 'mla_w_uq', 'mla_kv_norm_g', 'mla_w_ukv',
           'w_out', 'ffn_norm_g', 'ffn_w_gate', 'ffn_w_up', 'ffn_conv_w', 'ffn_conv_b', 'ffn_w_down', 'final_norm_g']
BIG = {'w_in': 'col', 'mla_w_uq': 'col', 'mla_w_ukv': 'col', 'w_out': 'row', 'ffn_w_gate': 'col', 'ffn_w_up': 'col',
       'ffn_w_down': 'row'}
SMALL_SHARDED = {'rwkv_w2': 'col', 'rwkv_a2': 'col', 'rwkv_g2': 'col', 'ffn_conv_w': 'col'}
SHARDED = {**BIG, **SMALL_SHARDED}
REPLICATED = [n for n in WEIGHTS if n not in SHARDED]


def _round_up(n, m):
    return (n + m - 1) // m * m


def _pick(n, cap, unit):
    if n <= cap:
        return n
    best = None
    for t in range(unit, cap + 1, unit):
        if n % t == 0:
            best = t
    assert best is not None, (n, cap, unit)
    return best


def _params(sem):
    return pltpu.CompilerParams(dimension_semantics=sem, vmem_limit_bytes=VMEM_LIMIT)


def mm(name, a, b, add=None, out_dtype=F32, comm=None):
    m, k = a.shape
    k2, n = b.shape
    assert k == k2, (name, a.shape, b.shape)
    tn = n if k * n * 2 <= RESIDENT_BYTES else _pick(n, 640, LANES)
    tm = _pick(m, 2048 if (tn < n and m * k * 2 <= RESIDENT_BYTES) else 512, SUBLANES * 2)
    has_add = add is not None

    def body(a_ref, b_ref, *rest):
        o_ref = rest[-1]
        acc = jnp.dot(a_ref[...].astype(BF16), b_ref[...].astype(BF16), preferred_element_type=F32)
        if has_add:
            acc = acc + rest[0][...].astype(F32)
        o_ref[...] = acc.astype(o_ref.dtype)

    in_specs = [pl.BlockSpec((tm, k), lambda i, j: (i, 0)), pl.BlockSpec((k, tn), lambda i, j: (0, j))]
    ops = [a, b]
    if has_add:
        in_specs.append(pl.BlockSpec((tm, tn), lambda i, j: (i, j)))
        ops.append(add)
    res = _pallas(
        body, name=name, grid=(m // tm, n // tn), in_specs=in_specs,
        out_specs=[pl.BlockSpec((tm, tn), lambda i, j: (i, j))],
        out_shape=[jax.ShapeDtypeStruct((m, n), out_dtype)], sem=("parallel", "parallel"), comm=comm,
    )(*ops)
    return res[0] if comm is None else res


def mm_nt(name, a, b, out_dtype=F32):
    m, k = a.shape
    n, k2 = b.shape
    assert k == k2, (name, a.shape, b.shape)
    tm = _pick(m, 2048 if m * k * 2 <= RESIDENT_BYTES else 512, SUBLANES * 2)
    tn = _pick(n, 1024, LANES)

    def body(a_ref, b_ref, o_ref):
        acc = lax.dot_general(a_ref[...].astype(BF16), b_ref[...].astype(BF16), (((1,), (1,)), ((), ())),
                              preferred_element_type=F32)
        o_ref[...] = acc.astype(o_ref.dtype)

    return pl.pallas_call(
        body, name=name, grid=(m // tm, n // tn),
        in_specs=[pl.BlockSpec((tm, k), lambda i, j: (i, 0)), pl.BlockSpec((tn, k), lambda i, j: (j, 0))],
        out_specs=pl.BlockSpec((tm, tn), lambda i, j: (i, j)),
        out_shape=jax.ShapeDtypeStruct((m, n), out_dtype),
        compiler_params=_params(("parallel", "parallel")),
    )(a, b)


def mm_sh(name, a, g, out_dtype=F32, comm=None, slabs=False):
    m, k = a.shape
    nd, k2, nbp = g.shape
    assert k == k2, (name, a.shape, g.shape)
    tm = _pick(m, 2048 if m * k * 2 <= RESIDENT_BYTES else 512, SUBLANES * 2)

    def body(a_ref, b_ref, o_ref):
        o_ref[...] = jnp.dot(a_ref[...].astype(BF16), b_ref[...].astype(BF16), preferred_element_type=F32).astype(o_ref.dtype)

    if slabs:
        out_spec, out_shape = pl.BlockSpec((None, tm, nbp), lambda i, j: (j, i, 0)), (nd, m, nbp)
    else:
        out_spec, out_shape = pl.BlockSpec((tm, nbp), lambda i, j: (i, j)), (m, nd * nbp)
    res = _pallas(
        body, name=name, grid=(m // tm, nd),
        in_specs=[pl.BlockSpec((tm, k), lambda i, j: (i, 0)), pl.BlockSpec((None, k, nbp), lambda i, j: (j, 0, 0))],
        out_specs=[out_spec], out_shape=[jax.ShapeDtypeStruct(out_shape, out_dtype)], sem=("parallel", "parallel"), comm=comm,
    )(a, g)
    return res[0] if comm is None else res


def mm_sh_nt(name, a, g, add=None, comm=None):
    nd, k, nbp = g.shape
    slabs = a.ndim == 3
    m = a.shape[1] if slabs else a.shape[0]
    assert a.shape == ((nd, m, nbp) if slabs else (m, nd * nbp)), (name, a.shape, g.shape)
    has_add = add is not None
    tm = _pick(m, 512 if has_add else 1024, SUBLANES * 2)

    def body(a_ref, b_ref, *rest):
        o_ref = rest[-1]
        part = lax.dot_general(a_ref[...].astype(BF16), b_ref[...].astype(BF16), (((1,), (1,)), ((), ())),
                               preferred_element_type=F32)

        @pl.when(pl.program_id(1) == 0)
        def _():
            o_ref[...] = part + rest[0][...] if has_add else part

        @pl.when(pl.program_id(1) != 0)
        def _():
            o_ref[...] += part

    a_spec = pl.BlockSpec((None, tm, nbp), lambda i, j: (j, i, 0)) if slabs else pl.BlockSpec((tm, nbp), lambda i, j: (i, j))
    in_specs = [a_spec, pl.BlockSpec((None, k, nbp), lambda i, j: (j, 0, 0))]
    ops = [a, g]
    if has_add:
        in_specs.append(pl.BlockSpec((tm, k), lambda i, j: (i, 0)))
        ops.append(add)
    res = _pallas(
        body, name=name, grid=(m // tm, nd), in_specs=in_specs,
        out_specs=[pl.BlockSpec((tm, k), lambda i, j: (i, 0))],
        out_shape=[jax.ShapeDtypeStruct((m, k), F32)], sem=("parallel", "arbitrary"), comm=comm,
    )(*ops)
    return res[0] if comm is None else res


def mm_tn(name, a, b):
    m, k = a.shape
    m2, n = b.shape
    assert m == m2, (name, a.shape, b.shape)
    tk = _pick(k, 512, LANES)
    tn = n if m * n * 2 <= RESIDENT_BYTES else _pick(n, 640, LANES)

    def body(a_ref, b_ref, o_ref):
        o_ref[...] = lax.dot_general(a_ref[...].astype(BF16), b_ref[...].astype(BF16), (((0,), (0,)), ((), ())),
                                     preferred_element_type=F32)

    return pl.pallas_call(
        body, name=name, grid=(k // tk, n // tn),
        in_specs=[pl.BlockSpec((m, tk), lambda i, j: (0, i)), pl.BlockSpec((m, tn), lambda i, j: (0, j))],
        out_specs=pl.BlockSpec((tk, tn), lambda i, j: (i, j)),
        out_shape=jax.ShapeDtypeStruct((k, n), F32),
        compiler_params=_params(("parallel", "parallel")),
    )(a, b)


def mm_sh_out(name, a, b, comm=None):
    m, k = a.shape
    slabs = b.ndim == 3
    nbp = b.shape[2] if slabs else b.shape[1] // N_DEV
    assert b.shape == ((N_DEV, m, nbp) if slabs else (m, N_DEV * nbp)), (name, a.shape, b.shape)
    tk = _pick(k, 2048 if k * m * 2 <= RESIDENT_BYTES else 512, LANES)
    b_spec = pl.BlockSpec((None, m, nbp), lambda i, j: (j, 0, 0)) if slabs else pl.BlockSpec((m, nbp), lambda i, j: (0, j))

    def body(a_ref, b_ref, o_ref):
        o_ref[...] = lax.dot_general(a_ref[...].astype(BF16), b_ref[...].astype(BF16), (((0,), (0,)), ((), ())),
                                     preferred_element_type=F32)

    res = _pallas(
        body, name=name, grid=(k // tk, N_DEV),
        in_specs=[pl.BlockSpec((m, tk), lambda i, j: (0, i)), b_spec],
        out_specs=[pl.BlockSpec((None, tk, nbp), lambda i, j: (j, i, 0))],
        out_shape=[jax.ShapeDtypeStruct((N_DEV, k, nbp), F32)], sem=("parallel", "parallel"), comm=comm,
    )(a, b)
    return res[0] if comm is None else res


def pad_cols(y, nb, nbp):
    m = y.shape[0]
    if nb == nbp:
        return y
    return jnp.pad(y.reshape(m, N_DEV, nb), ((0, 0), (0, 0), (0, nbp - nb))).reshape(m, N_DEV * nbp)


def unpad_cols(y, nb, nbp):
    m = y.shape[0]
    if nb == nbp:
        return y
    return y.reshape(m, N_DEV, nbp)[:, :, :nb].reshape(m, N_DEV * nb)


def _in_spec(kind, a, tm):
    if kind == 'row':
        return pl.BlockSpec((tm, a.shape[1]), lambda h, i: (i, 0))
    if kind == 'hrow':
        return pl.BlockSpec((None, tm, a.shape[2]), lambda h, i: (h, i, 0))
    if kind == 'const':
        return pl.BlockSpec(a.shape, lambda h, i: (0, 0))
    assert kind == 'hconst', kind
    return pl.BlockSpec((None,) + a.shape[1:], lambda h, i: (h, 0, 0))


def _row_out(kind, c, dtype, heads, s, tm):
    if kind == 'row':
        assert heads == 1
        return jax.ShapeDtypeStruct((s, c), dtype), pl.BlockSpec((tm, c), lambda h, i: (i, 0))
    return jax.ShapeDtypeStruct((heads, s, c), dtype), pl.BlockSpec((None, tm, c), lambda h, i: (h, i, 0))


def rowwise(name, fn, arrs, kinds, outs, *, heads, s, tm, comm=None):
    n_in = len(arrs)

    def body(*refs):
        vals = fn(*[r[...] for r in refs[:n_in]])
        for o, v in zip(refs[n_in:], vals, strict=True):
            o[...] = v.astype(o.dtype)

    shapes, specs = zip(*[_row_out(k, c, dt, heads, s, tm) for k, c, dt in outs])
    return _pallas(
        body, name=name, grid=(heads, s // tm),
        in_specs=[_in_spec(k, a, tm) for k, a in zip(kinds, arrs, strict=True)],
        out_specs=list(specs), out_shape=list(shapes), sem=("parallel", "parallel"), comm=comm,
    )(*arrs)


def rowwise_vjp(name, fn, arrs, kinds, cots, cot_kinds, wrt, *, heads, s, tm, out_dtypes=None, primal=False, comm=None,
                plus=None):
    n_in, n_cot = len(arrs), len(cots)
    nb = s // tm
    out_dtypes = out_dtypes or [F32] * len(wrt)
    extra = [] if plus is None else [plus]

    def body(*refs):
        vals = [r[...] for r in refs[:n_in]]
        cvals = tuple(r[...].astype(F32) for r in refs[n_in:n_in + n_cot])
        outs = refs[n_in + n_cot + len(extra):]

        def f(*dv):
            full = list(vals)
            for j, i in enumerate(wrt):
                full[i] = dv[j]
            return tuple(fn(*full))

        prim, vjp_fn = jax.vjp(f, *[vals[i].astype(F32) for i in wrt])
        grads = list(vjp_fn(cvals))
        if plus is not None:
            grads[0] = grads[0] + refs[n_in + n_cot][...]
        for o, g in zip(outs[:len(wrt)], grads, strict=True):
            o[...] = g.astype(o.dtype)
        if primal:
            for o, p in zip(outs[len(wrt):], prim, strict=True):
                o[...] = p.astype(o.dtype)

    shapes, specs = [], []
    for i, dt in zip(wrt, out_dtypes, strict=True):
        kind, a = kinds[i], arrs[i]
        if kind in ('row', 'hrow'):
            c = a.shape[-1]
            sh, sp = _row_out('row' if (kind == 'row' and heads == 1) else 'hrow', c, dt, heads, s, tm)
        else:
            r, c = a.shape[-2:]
            sh = jax.ShapeDtypeStruct((heads, nb, r, c), dt)
            sp = pl.BlockSpec((None, None, r, c), lambda h, i: (h, i, 0, 0))
        shapes.append(sh)
        specs.append(sp)
    if primal:
        for ck, c in zip(cot_kinds, cots, strict=True):
            sh, sp = _row_out(ck, c.shape[-1], F32, heads, s, tm)
            shapes.append(sh)
            specs.append(sp)
    in_specs = [_in_spec(k, a, tm) for k, a in zip(kinds, arrs, strict=True)]
    in_specs += [_in_spec(k, a, tm) for k, a in zip(cot_kinds, cots, strict=True)]
    in_specs += [_in_spec('row', a, tm) for a in extra]
    return _pallas(
        body, name=name, grid=(heads, nb), in_specs=in_specs, out_specs=specs, out_shape=shapes,
        sem=("parallel", "parallel"), comm=comm,
    )(*arrs, *cots, *extra)


def colsum(name, x):
    n, m = x.shape
    tc = _pick(m, 32768, LANES) if m % LANES == 0 else m

    def body(x_ref, o_ref):
        acc = x_ref[0:1, :].astype(F32)
        for r in range(1, n):
            acc = acc + x_ref[r:r + 1, :].astype(F32)
        o_ref[...] = acc

    return pl.pallas_call(
        body, name=name, grid=(m // tc,), in_specs=[pl.BlockSpec((n, tc), lambda j: (0, j))],
        out_specs=pl.BlockSpec((1, tc), lambda j: (0, j)), out_shape=jax.ShapeDtypeStruct((1, m), F32),
        compiler_params=_params(("parallel",)),
    )(x)


def headsum(name, x):
    h, s, c = x.shape
    tm = _pick(s, 256, SUBLANES)

    def body(x_ref, o_ref):
        acc = x_ref[0]
        for j in range(1, h):
            acc = acc + x_ref[j]
        o_ref[...] = acc

    return pl.pallas_call(
        body, name=name, grid=(s // tm,), in_specs=[pl.BlockSpec((h, tm, c), lambda i: (0, i, 0))],
        out_specs=pl.BlockSpec((tm, c), lambda i: (i, 0)), out_shape=jax.ShapeDtypeStruct((s, c), F32),
        compiler_params=_params(("parallel",)),
    )(x)


def sum_all(name, x):
    def body(x_ref, o_ref):
        o_ref[...] = jnp.sum(x_ref[...], keepdims=True)

    return pl.pallas_call(body, name=name, out_shape=jax.ShapeDtypeStruct((1, 1), F32))(x)


def sum_partials(name, p, per_head):
    h, nb, r, c = p.shape
    if per_head:
        flat = jnp.transpose(p, (1, 0, 2, 3)).reshape(nb, h * r * c)
        if nb == 1:
            return flat.reshape(h, r, c)
        return colsum(name, flat).reshape(h, r, c)
    flat = p.reshape(h * nb, r * c)
    if h * nb == 1:
        return flat.reshape(r, c)
    return colsum(name, flat).reshape(r, c)


def _rms_fn(x, g):
    xf = x.astype(F32)
    return (xf * lax.rsqrt(jnp.mean(xf * xf, axis=-1, keepdims=True) + NORM_EPS) * g,)


def _softplus(z):
    return jnp.maximum(z, 0.0) + jnp.log(1.0 + jnp.exp(-jnp.abs(z)))


def _rwkv_pre_fn(hk, hw, ha, hg, w0, w2, a0, a2, g2, k_k, k_a):
    zw = w0 + jnp.dot(jnp.tanh(hw), w2, preferred_element_type=F32)
    w_log = -_softplus(-zw) - 0.5
    decay = jnp.exp(-jnp.exp(w_log))
    a = jax.nn.sigmoid(a0 + jnp.dot(ha, a2, preferred_element_type=F32))
    g = jnp.dot(jax.nn.sigmoid(hg), g2, preferred_element_type=F32)
    kk = hk * k_k
    kk = kk * lax.rsqrt(jnp.maximum(jnp.sum(kk * kk, axis=-1, keepdims=True), 1e-24))
    k = hk * (1.0 + (a - 1.0) * k_a)
    return decay, k, -kk, kk * a, g


def _rwkv_pre_grad_fn(hk, hw, ha, hg, w0, w2, a0, a2, g2, k_k, k_a, hr, hv):
    decay, k, a_sc, b_sc, g = _rwkv_pre_fn(hk, hw, ha, hg, w0, w2, a0, a2, g2, k_k, k_a)
    return decay, k, k, a_sc, b_sc, g, hr, hr, hv, hv


def _rwkv_post_fn(y, r, k, v, g, gn_w, gn_b, r_k):
    mu = jnp.mean(y, axis=-1, keepdims=True)
    var = jnp.mean(jnp.square(y - mu), axis=-1, keepdims=True)
    yn = (y - mu) * lax.rsqrt(var + GN_EPS) * gn_w + gn_b
    bonus = jnp.sum(r * k * r_k, axis=-1, keepdims=True) * v
    return ((yn + bonus) * g,)


def _rope_tables(pos, inv_freq2):
    ang = pos * inv_freq2
    return jnp.cos(ang), jnp.sin(ang)


def _rope(t, cos2, sin2, rot):
    return t * cos2 + jnp.dot(t, rot, precision=HIGHEST, preferred_element_type=F32) * sin2


def _mla_pre_fn(c_q, c_kv, k_pe, pos, q_g, kv_g, inv_freq2, rot):
    cos2, sin2 = _rope_tables(pos, inv_freq2)
    return _rms_fn(c_q, q_g)[0], _rms_fn(c_kv, kv_g)[0], _rope(k_pe, cos2, sin2, rot), cos2, sin2


def _mla_pre_grad_fn(c_q, c_kv, k_pe, pos, q_g, kv_g, inv_freq2, rot):
    return _mla_pre_fn(c_q, c_kv, k_pe, pos, q_g, kv_g, inv_freq2, rot)[:3]


def _loss_fn(x2, g, target):
    y = _rms_fn(x2, g)[0]
    return (0.5 * jnp.mean(jnp.square(y - target), axis=-1, keepdims=True),)


def _adamw_fn(w, g, m, v):
    m = ADAM_B1 * m + (1.0 - ADAM_B1) * g
    v = ADAM_B2 * v + (1.0 - ADAM_B2) * jnp.square(g)
    m_hat = m / (1.0 - ADAM_B1 ** ADAM_STEP)
    v_hat = v / (1.0 - ADAM_B2 ** ADAM_STEP)
    delta = -ADAM_LR * (m_hat / (jnp.sqrt(v_hat) + ADAM_EPS) + ADAM_WD * w)
    return delta, m, v


def _prev_halo_spec(c, tm):
    return pl.BlockSpec((SUBLANES, c), lambda i: (jnp.maximum(i * (tm // SUBLANES) - 1, 0), 0))


def _next_halo_spec(c, tm, s):
    return pl.BlockSpec((SUBLANES, c), lambda i: (jnp.minimum((i + 1) * (tm // SUBLANES), s // SUBLANES - 1), 0))


def _shift_down(p, halo, first_block, n):
    out = pltpu.roll(p, n, 0)
    row = lax.broadcasted_iota(jnp.int32, p.shape, 0)
    for j in range(n):
        top = jnp.where(first_block, 0.0, halo[SUBLANES - n + j:SUBLANES - n + j + 1, :])
        out = jnp.where(row == j, top, out)
    return out


def _shift_up(p, halo, last_block, n):
    rows = p.shape[0]
    out = pltpu.roll(p, rows - n, 0)
    row = lax.broadcasted_iota(jnp.int32, p.shape, 0)
    for j in range(n):
        bot = jnp.where(last_block, 0.0, halo[j:j + 1, :])
        out = jnp.where(row == rows - n + j, bot, out)
    return out


def token_shift_fwd(p, mu, tm):
    s, c = p.shape

    def body(p_ref, halo_ref, mu_ref, o_ref):
        pv = p_ref[...]
        prev = _shift_down(pv, halo_ref[...], pl.program_id(0) == 0, 1)
        o_ref[...] = pv + (prev - pv) * mu_ref[...]

    return pl.pallas_call(
        body, name="token_shift_fwd", grid=(s // tm,),
        in_specs=[pl.BlockSpec((tm, c), lambda i: (i, 0)), _prev_halo_spec(c, tm), pl.BlockSpec((1, c), lambda i: (0, 0))],
        out_specs=pl.BlockSpec((tm, c), lambda i: (i, 0)), out_shape=jax.ShapeDtypeStruct((s, c), F32),
        compiler_params=_params(("parallel",)),
    )(p, p, mu)


def token_shift_bwd(p, mu, ds, tm):
    s, c = p.shape
    nb = s // tm

    def body(p_ref, halo_ref, mu_ref, ds_ref, dsn_ref, dp_ref, dmu_ref):
        i = pl.program_id(0)
        pv, dsv, muv = p_ref[...], ds_ref[...], mu_ref[...]
        prev = _shift_down(pv, halo_ref[...], i == 0, 1)
        nxt = _shift_up(dsv, dsn_ref[...], i == nb - 1, 1)
        dp_ref[...] = dsv * (1.0 - muv) + nxt * muv
        dmu_ref[...] = jnp.sum(dsv * (prev - pv), axis=0, keepdims=True)

    return pl.pallas_call(
        body, name="token_shift_bwd", grid=(nb,),
        in_specs=[pl.BlockSpec((tm, c), lambda i: (i, 0)), _prev_halo_spec(c, tm), pl.BlockSpec((1, c), lambda i: (0, 0)),
                  pl.BlockSpec((tm, c), lambda i: (i, 0)), _next_halo_spec(c, tm, s)],
        out_specs=[pl.BlockSpec((tm, c), lambda i: (i, 0)), pl.BlockSpec((None, 1, c), lambda i: (i, 0, 0))],
        out_shape=[jax.ShapeDtypeStruct((s, c), F32), jax.ShapeDtypeStruct((nb, 1, c), F32)],
        compiler_params=_params(("parallel",)),
    )(p, p, mu, ds, ds)


def _ffn_tiles(s, f):
    return _pick(s, 256, SUBLANES), _pick(f, 1408, LANES)


def _conv_gate(gp, halo, first_block, cw, cb):
    p1 = _shift_down(gp, halo, first_block, 1)
    p2 = _shift_down(gp, halo, first_block, 2)
    return cw[0:1, :] * p2 + cw[1:2, :] * p1 + cw[2:3, :] * gp + cb, p1, p2


def ffn_act_fwd(gate_pre, up, conv_w, conv_b):
    s, f = gate_pre.shape
    tm, tc = _ffn_tiles(s, f)

    def body(gp_ref, halo_ref, up_ref, cw_ref, cb_ref, o_ref):
        gate, _, _ = _conv_gate(gp_ref[...], halo_ref[...], pl.program_id(0) == 0, cw_ref[...], cb_ref[...])
        o_ref[...] = (gate * jax.nn.sigmoid(gate) * up_ref[...]).astype(o_ref.dtype)

    blk = pl.BlockSpec((tm, tc), lambda i, j: (i, j))
    return pl.pallas_call(
        body, name="ffn_act_fwd", grid=(s // tm, f // tc),
        in_specs=[blk, pl.BlockSpec((SUBLANES, tc), lambda i, j: (jnp.maximum(i * (tm // SUBLANES) - 1, 0), j)), blk,
                  pl.BlockSpec((CONV_W, tc), lambda i, j: (0, j)), pl.BlockSpec((1, tc), lambda i, j: (0, j))],
        out_specs=blk, out_shape=jax.ShapeDtypeStruct((s, f), BF16),
        compiler_params=_params(("parallel", "parallel")),
    )(gate_pre, gate_pre, up, conv_w, conv_b)


def ffn_act_bwd1(gate_pre, up, conv_w, conv_b, d_act):
    s, f = gate_pre.shape
    tm, tc = _ffn_tiles(s, f)
    nb = s // tm

    def body(gp_ref, halo_ref, up_ref, cw_ref, cb_ref, da_ref, dg_ref, du_ref, dcw_ref, dcb_ref):
        gp = gp_ref[...]
        gate, p1, p2 = _conv_gate(gp, halo_ref[...], pl.program_id(0) == 0, cw_ref[...], cb_ref[...])
        sig = jax.nn.sigmoid(gate)
        da = da_ref[...].astype(F32)
        du_ref[...] = (da * gate * sig).astype(du_ref.dtype)
        dg = da * up_ref[...] * (sig * (1.0 + gate * (1.0 - sig)))
        dg_ref[...] = dg
        dcb_ref[...] = jnp.sum(dg, axis=0, keepdims=True)
        dcw_ref[0:1, :] = jnp.sum(dg * p2, axis=0, keepdims=True)
        dcw_ref[1:2, :] = jnp.sum(dg * p1, axis=0, keepdims=True)
        dcw_ref[2:3, :] = jnp.sum(dg * gp, axis=0, keepdims=True)

    blk = pl.BlockSpec((tm, tc), lambda i, j: (i, j))
    return pl.pallas_call(
        body, name="ffn_act_bwd1", grid=(nb, f // tc),
        in_specs=[blk, pl.BlockSpec((SUBLANES, tc), lambda i, j: (jnp.maximum(i * (tm // SUBLANES) - 1, 0), j)), blk,
                  pl.BlockSpec((CONV_W, tc), lambda i, j: (0, j)), pl.BlockSpec((1, tc), lambda i, j: (0, j)), blk],
        out_specs=[blk, blk, pl.BlockSpec((None, CONV_W, tc), lambda i, j: (i, 0, j)),
                   pl.BlockSpec((None, 1, tc), lambda i, j: (i, 0, j))],
        out_shape=[jax.ShapeDtypeStruct((s, f), F32), jax.ShapeDtypeStruct((s, f), BF16),
                   jax.ShapeDtypeStruct((nb, CONV_W, f), F32), jax.ShapeDtypeStruct((nb, 1, f), F32)],
        compiler_params=_params(("parallel", "parallel")),
    )(gate_pre, gate_pre, up, conv_w, conv_b, d_act)


def ffn_act_bwd2(d_gate, conv_w):
    s, f = d_gate.shape
    tm, tc = _ffn_tiles(s, f)
    nb = s // tm

    def body(dg_ref, halo_ref, cw_ref, o_ref):
        dg, cw = dg_ref[...], cw_ref[...]
        last = pl.program_id(0) == nb - 1
        n1 = _shift_up(dg, halo_ref[...], last, 1)
        n2 = _shift_up(dg, halo_ref[...], last, 2)
        o_ref[...] = (cw[2:3, :] * dg + cw[1:2, :] * n1 + cw[0:1, :] * n2).astype(o_ref.dtype)

    blk = pl.BlockSpec((tm, tc), lambda i, j: (i, j))
    return pl.pallas_call(
        body, name="ffn_act_bwd2", grid=(nb, f // tc),
        in_specs=[blk, pl.BlockSpec((SUBLANES, tc), lambda i, j: (jnp.minimum((i + 1) * (tm // SUBLANES), s // SUBLANES - 1), j)),
                  pl.BlockSpec((CONV_W, tc), lambda i, j: (0, j))],
        out_specs=blk, out_shape=jax.ShapeDtypeStruct((s, f), BF16),
        compiler_params=_params(("parallel", "parallel")),
    )(d_gate, d_gate, conv_w)


def _mxu(x, y, cx, cy):
    if x.ndim == 3:
        return lax.dot_general(x, y, (((cx + 1,), (cy + 1,)), ((0,), (0,))), preferred_element_type=F32)
    return lax.dot_general(x, y, (((cx,), (cy,)), ((), ())), preferred_element_type=F32)


def _split(x):
    hi = x.astype(BF16)
    return hi, (x - hi.astype(F32)).astype(BF16)


def _make_dot3(cx, cy, passes):
    @jax.custom_vjp
    def f(x, y):
        if passes == 1:
            return _mxu(x.astype(BF16), y.astype(BF16), cx, cy)
        xh, xl = _split(x)
        yh, yl = _split(y)
        return _mxu(xh, yh, cx, cy) + (_mxu(xh, yl, cx, cy) + _mxu(xl, yh, cx, cy))

    def fwd(x, y):
        return f(x, y), (x, y)

    def bwd(res, g):
        x, y = res
        dx = dot3(g, y, 1, 1 - cy, passes) if cx == 1 else dot3(y, g, 1 - cy, 1, passes)
        dy = dot3(x, g, 1 - cx, 0, passes) if cy == 0 else dot3(g, x, 0, 1 - cx, passes)
        return dx, dy

    f.defvjp(fwd, bwd)
    return f


_DOT3 = {}


def dot3(x, y, cx, cy, passes=3):
    if (cx, cy, passes) not in _DOT3:
        _DOT3[(cx, cy, passes)] = _make_dot3(cx, cy, passes)
    return _DOT3[(cx, cy, passes)](x, y)


def _dot(x, y, passes=3):
    return dot3(x, y, 1, 0, passes)


def _dot_nt(x, y, passes=3):
    return dot3(x, y, 1, 1, passes)


def _dot_tn(x, y, passes=3):
    return dot3(x, y, 0, 0, passes)


def _tri_sum(x, lower):
    t = x.shape[-2]
    row = lax.broadcasted_iota(jnp.int32, (t, t), 0)
    col = lax.broadcasted_iota(jnp.int32, (t, t), 1)
    tri = jnp.where((col <= row) if lower else (col >= row), 1.0, 0.0).astype(BF16)
    if x.ndim == 3:
        tri = jnp.broadcast_to(tri[None], (x.shape[0], t, t))
    hi = x.astype(BF16)
    rest = x - hi.astype(F32)
    mid = rest.astype(BF16)
    low = (rest - mid.astype(F32)).astype(BF16)
    return _mxu(tri, hi, 1, 0) + (_mxu(tri, mid, 1, 0) + _mxu(tri, low, 1, 0))


@jax.custom_vjp
def _cumsum_rows(x):
    return _tri_sum(x, True)


_cumsum_rows.defvjp(lambda x: (_tri_sum(x, True), None), lambda _, g: (_tri_sum(g, False),))


def _scan_chunk(s0, r, w, k, v, a, b):
    t = r.shape[1]
    row = lax.broadcasted_iota(jnp.int32, (1, t, t), 1)
    col = lax.broadcasted_iota(jnp.int32, (1, t, t), 2)
    strict, incl = col < row, col <= row
    logw = jnp.log(w)
    cum = _cumsum_rows(logw)
    w_in, w_ex, w_inv = jnp.exp(cum), jnp.exp(cum - logw), jnp.exp(-cum)
    w_all = jnp.exp(jnp.sum(logw, axis=1, keepdims=True))
    at, rt, kt, bt = a * w_ex, r * w_in, k * w_inv, b * w_inv
    ps, po = SCAN_PASSES_SOLVE, SCAN_PASSES_OUT
    a_ab = jnp.where(strict, _dot_nt(at, bt, ps), 0.0)
    a_ak = jnp.where(strict, _dot_nt(at, kt, ps), 0.0)
    a_rk = jnp.where(incl, _dot_nt(rt, kt, po), 0.0)
    a_rb = jnp.where(incl, _dot_nt(rt, bt, po), 0.0)
    u = _dot_nt(at, s0, ps) + _dot(a_ak, v, ps)
    p = a_ab
    steps = int(math.log2(t))
    assert 2 ** steps == t
    for j in range(steps):
        u = u + _dot(p, u, ps)
        if j < steps - 1:
            p = _dot(p, p, ps)
    y = _dot_nt(rt, s0, po) + _dot(a_rk, v, po) + _dot(a_rb, u, po)
    s_new = s0 * w_all + _dot_tn(v, kt * w_all, po) + _dot_tn(u, bt * w_all, po)
    return y, s_new


def scan_fwd(r, w, k, v, a, b, comm=None):
    h, s, n = r.shape
    t = min(SCAN_CHUNK, s)
    nc = s // t

    hb = SCAN_HEADS if h % SCAN_HEADS == 0 else 1

    def body(r_ref, w_ref, k_ref, v_ref, a_ref, b_ref, y_ref, ck_ref, st_ref):
        @pl.when(pl.program_id(1) == 0)
        def _():
            st_ref[...] = jnp.zeros_like(st_ref)

        s0 = st_ref[...]
        ck_ref[...] = s0
        y, s_new = _scan_chunk(s0, r_ref[...], w_ref[...], k_ref[...], v_ref[...], a_ref[...], b_ref[...])
        y_ref[...] = y
        st_ref[...] = s_new

    blk = pl.BlockSpec((hb, t, n), lambda hh, c: (hh, c, 0))
    return _pallas(
        body, name="rwkv_scan_fwd", grid=(h // hb, nc), in_specs=[blk] * 6,
        out_specs=[blk, pl.BlockSpec((hb, None, n, n), lambda hh, c: (hh, c, 0, 0))],
        out_shape=[jax.ShapeDtypeStruct((h, s, n), F32), jax.ShapeDtypeStruct((h, nc, n, n), F32)],
        scratch_shapes=[pltpu.VMEM((hb, n, n), F32)], sem=("parallel", "arbitrary"), comm=comm,
    )(r, w, k, v, a, b)


def scan_bwd(r, w, k, v, a, b, ck, dy, comm=None):
    h, s, n = r.shape
    t = min(SCAN_CHUNK, s)
    nc = s // t

    hb = SCAN_HEADS if h % SCAN_HEADS == 0 else 1

    def body(r_ref, w_ref, k_ref, v_ref, a_ref, b_ref, ck_ref, dy_ref, dr_ref, dw_ref, dk_ref, dv_ref, da_ref, db_ref, ds_ref):
        @pl.when(pl.program_id(1) == 0)
        def _():
            ds_ref[...] = jnp.zeros_like(ds_ref)

        _, vjp_fn = jax.vjp(_scan_chunk, ck_ref[...], r_ref[...], w_ref[...], k_ref[...], v_ref[...], a_ref[...], b_ref[...])
        ds0, dr, dw, dk, dv, da, db = vjp_fn((dy_ref[...], ds_ref[...]))
        ds_ref[...] = ds0
        dr_ref[...], dw_ref[...], dk_ref[...], dv_ref[...], da_ref[...], db_ref[...] = dr, dw, dk, dv, da, db

    blk = pl.BlockSpec((hb, t, n), lambda hh, c: (hh, nc - 1 - c, 0))
    return _pallas(
        body, name="rwkv_scan_bwd", grid=(h // hb, nc),
        in_specs=[blk] * 6 + [pl.BlockSpec((hb, None, n, n), lambda hh, c: (hh, nc - 1 - c, 0, 0)), blk],
        out_specs=[blk] * 6, out_shape=[jax.ShapeDtypeStruct((h, s, n), F32)] * 6,
        scratch_shapes=[pltpu.VMEM((hb, n, n), F32)], sem=("parallel", "arbitrary"), comm=comm,
    )(r, w, k, v, a, b, ck, dy)


ATTN_BLOCK = 256
ATTN_LEVELS = 4
ATTN_SLAB = 2 * LANES


def _attn_specs(h, s, tq):
    qblk = lambda c, part: pl.BlockSpec((None, tq, c), lambda hh, i: (hh, i, part))
    kblk = lambda part: pl.BlockSpec((None, s, LANES), lambda hh, i: (hh, 0, part))
    row64 = pl.BlockSpec((tq, QK_ROPE), lambda hh, i: (i, 0))
    return [qblk(LANES, 0), qblk(LANES, 1), kblk(0), kblk(1), pl.BlockSpec((s, QK_ROPE), lambda hh, i: (0, 0)),
            row64, row64, pl.BlockSpec((QK_ROPE, QK_ROPE), lambda hh, i: (0, 0))]


def _attn_levels(s, tq):
    nq = s // tq
    n_lev = min(ATTN_LEVELS, nq)
    per = nq // n_lev
    return [(lv * per, (lv + 1) * per, (lv + 1) * per * tq) for lv in range(n_lev)]


def _attn_scores(qn_b, qp_b, kn_ref, kp_ref, klen, i, tq):
    scale = (QK_NOPE + QK_ROPE) ** -0.5
    kn_b = kn_ref[0:klen, :].astype(BF16)
    kp_b = kp_ref[0:klen, :].astype(BF16)
    sc = lax.dot_general(qn_b, kn_b, (((1,), (1,)), ((), ())), preferred_element_type=F32)
    sc = sc + lax.dot_general(qp_b, kp_b, (((1,), (1,)), ((), ())), preferred_element_type=F32)
    row = i * tq + lax.broadcasted_iota(jnp.int32, sc.shape, 0)
    col = lax.broadcasted_iota(jnp.int32, sc.shape, 1)
    return jnp.where(row >= col, sc * scale, NEG_INF), scale, kn_b, kp_b


def attn_fwd(q_h, kv_h, kp, cos2, sin2, rot, comm=None):
    h, s, _ = q_h.shape
    tq = _pick(s, ATTN_BLOCK, SUBLANES)

    def body(qn_ref, qp_ref, kn_ref, v_ref, kp_ref, cos_ref, sin_ref, rot_ref, o_ref, lse_ref):
        i = pl.program_id(1)
        qn_b = qn_ref[...].astype(BF16)
        qp_b = _rope(qp_ref[:, :QK_ROPE], cos_ref[...], sin_ref[...], rot_ref[...]).astype(BF16)

        def level(klen):
            sc, _, _, _ = _attn_scores(qn_b, qp_b, kn_ref, kp_ref, klen, i, tq)
            mx = jnp.max(sc, axis=-1, keepdims=True)
            e = jnp.exp(sc - mx)
            den = jnp.sum(e, axis=-1, keepdims=True)
            o_ref[...] = jnp.dot((e / den).astype(BF16), v_ref[0:klen, :].astype(BF16), preferred_element_type=F32)
            lse_ref[...] = mx + jnp.log(den)

        for lo, hi, klen in _attn_levels(s, tq):
            pl.when((i >= lo) & (i < hi))(functools.partial(level, klen))

    oblk = lambda c: pl.BlockSpec((None, tq, c), lambda hh, i: (hh, i, 0))
    return _pallas(
        body, name="mla_attn_fwd", grid=(h, s // tq), in_specs=_attn_specs(h, s, tq),
        out_specs=[oblk(V_HEAD), oblk(1)],
        out_shape=[jax.ShapeDtypeStruct((h, s, V_HEAD), F32), jax.ShapeDtypeStruct((h, s, 1), F32)],
        sem=("parallel", "parallel"), comm=comm,
    )(q_h, q_h, kv_h, kv_h, kp, cos2, sin2, rot)


def attn_bwd(q_h, kv_h, kp, cos2, sin2, rot, o, lse, do, comm=None):
    h, s, _ = q_h.shape
    tq = _pick(s, ATTN_BLOCK, SUBLANES)
    nq = s // tq

    def body(qn_ref, qp_ref, kn_ref, v_ref, kp_ref, cos_ref, sin_ref, rot_ref, o_ref, lse_ref, do_ref,
             dq_ref, dkv_ref, dkp_ref, dkv_acc, dkp_acc):
        i = pl.program_id(1)

        @pl.when(i == 0)
        def _():
            dkv_acc[...] = jnp.zeros_like(dkv_acc)
            dkp_acc[...] = jnp.zeros_like(dkp_acc)

        cosv, sinv, rotv = cos_ref[...], sin_ref[...], rot_ref[...]
        qn_b = qn_ref[...].astype(BF16)
        qp_b = _rope(qp_ref[:, :QK_ROPE], cosv, sinv, rotv).astype(BF16)
        dov = do_ref[...]
        do_b = dov.astype(BF16)
        delta = jnp.sum(dov * o_ref[...], axis=-1, keepdims=True)
        lsev = lse_ref[...]

        def level(klen):
            sc, scale, kn_b, kp_b = _attn_scores(qn_b, qp_b, kn_ref, kp_ref, klen, i, tq)
            p = jnp.exp(sc - lsev)
            dp = lax.dot_general(do_b, v_ref[0:klen, :].astype(BF16), (((1,), (1,)), ((), ())), preferred_element_type=F32)
            ds = (p * (dp - delta) * scale).astype(BF16)
            dkv_acc[0:klen, :LANES] += lax.dot_general(ds, qn_b, (((0,), (0,)), ((), ())), preferred_element_type=F32)
            dkv_acc[0:klen, LANES:] += lax.dot_general(p.astype(BF16), do_b, (((0,), (0,)), ((), ())), preferred_element_type=F32)
            dkp_acc[0:klen, :] += lax.dot_general(ds, qp_b, (((0,), (0,)), ((), ())), preferred_element_type=F32)
            dqp = jnp.dot(ds, kp_b, preferred_element_type=F32)
            dqp_raw = dqp * cosv + lax.dot_general(dqp * sinv, rotv, (((1,), (1,)), ((), ())), precision=HIGHEST,
                                                   preferred_element_type=F32)
            dq_ref[:, :QK_NOPE] = jnp.dot(ds, kn_b, preferred_element_type=F32).astype(dq_ref.dtype)
            dq_ref[:, QK_NOPE:QK_NOPE + QK_ROPE] = dqp_raw.astype(dq_ref.dtype)
            dq_ref[:, QK_NOPE + QK_ROPE:] = jnp.zeros((tq, ATTN_SLAB - QK_NOPE - QK_ROPE), dq_ref.dtype)

        for lo, hi, klen in _attn_levels(s, tq):
            pl.when((i >= lo) & (i < hi))(functools.partial(level, klen))

        @pl.when(i == nq - 1)
        def _():
            dkv_ref[...] = dkv_acc[...].astype(dkv_ref.dtype)
            dkp_ref[...] = dkp_acc[...]

    rblk = lambda c: pl.BlockSpec((None, tq, c), lambda hh, i: (hh, i, 0))
    sblk = lambda c: pl.BlockSpec((None, s, c), lambda hh, i: (hh, 0, 0))
    return _pallas(
        body, name="mla_attn_bwd", grid=(h, nq),
        in_specs=_attn_specs(h, s, tq) + [rblk(V_HEAD), rblk(1), rblk(V_HEAD)],
        out_specs=[rblk(ATTN_SLAB), sblk(ATTN_SLAB), sblk(QK_ROPE)],
        out_shape=[jax.ShapeDtypeStruct((h, s, ATTN_SLAB), BF16), jax.ShapeDtypeStruct((h, s, ATTN_SLAB), BF16),
                   jax.ShapeDtypeStruct((h, s, QK_ROPE), F32)],
        scratch_shapes=[pltpu.VMEM((s, ATTN_SLAB), F32), pltpu.VMEM((s, QK_ROPE), F32)],
        sem=("parallel", "arbitrary"), comm=comm,
    )(q_h, q_h, kv_h, kv_h, kp, cos2, sin2, rot, o, lse, do)


def _my_pos():
    return lax.axis_index("x"), lax.axis_index("y"), lax.axis_index("c")


def _dev_index(px, py, pc):
    return 4 * px + 2 * py + pc


def all_gather(name, shard):
    r, c = shard.shape

    def body(x_ref, out_ref, send_sems, recv_sems, local_sem):
        x, y, cc = _my_pos()
        me, sibling = (x, y, cc), (x, y, 1 - cc)
        chips = [(1 - x, y), (x, 1 - y), (1 - x, 1 - y)]

        def rows(px, py, pc):
            return out_ref.at[_dev_index(px, py, pc)]

        def copy(kk, block, to, src=None):
            return pltpu.make_async_remote_copy(
                src_ref=rows(*block) if src is None else src, dst_ref=rows(*block),
                send_sem=send_sems.at[kk], recv_sem=recv_sems.at[kk], device_id=to, device_id_type=MESH)

        mine = pltpu.make_async_copy(x_ref, rows(*me), local_sem)
        mine.start()
        first = [copy(0, me, sibling, src=x_ref)]
        first += [copy(1 + j, me, (*chip, cc), src=x_ref) for j, chip in enumerate(chips)]
        for cp in first:
            cp.start()
        passed = [copy(4 + j, (*chip, cc), sibling) for j, chip in enumerate(chips)]
        for j, chip in enumerate(chips):
            copy(1 + j, (*chip, cc), me).wait_recv()
            passed[j].start()
        copy(0, sibling, me).wait_recv()
        for j, chip in enumerate(chips):
            copy(4 + j, (*chip, 1 - cc), me).wait_recv()
        for cp in first + passed:
            cp.wait_send()
        mine.wait()

    return pl.pallas_call(
        body, name=name, out_shape=jax.ShapeDtypeStruct((N_DEV, r, c), shard.dtype),
        in_specs=[pl.BlockSpec(memory_space=pl.ANY)], out_specs=pl.BlockSpec(memory_space=pl.ANY),
        scratch_shapes=[pltpu.SemaphoreType.DMA((7,)), pltpu.SemaphoreType.DMA((7,)), pltpu.SemaphoreType.DMA],
    )(shard)


def exchange_sibling(name, g):
    _, r, c = g.shape

    def body(g_ref, out_ref, send_sems, recv_sems):
        x, y, cc = _my_pos()
        copies = []
        for px in range(2):
            for py in range(2):
                slot = 2 * px + py
                copies.append(pltpu.make_async_remote_copy(
                    src_ref=g_ref.at[_dev_index(px, py, 1 - cc)], dst_ref=out_ref.at[slot],
                    send_sem=send_sems.at[slot], recv_sem=recv_sems.at[slot], device_id=(x, y, 1 - cc), device_id_type=MESH))
        for cp in copies:
            cp.start()
        for cp in copies:
            cp.wait()

    return pl.pallas_call(
        body, name=name, out_shape=jax.ShapeDtypeStruct((4, r, c), g.dtype),
        in_specs=[pl.BlockSpec(memory_space=pl.ANY)], out_specs=pl.BlockSpec(memory_space=pl.ANY),
        scratch_shapes=[pltpu.SemaphoreType.DMA((4,)), pltpu.SemaphoreType.DMA((4,))],
    )(g)


def exchange_chips(name, hsum):
    _, r, c = hsum.shape

    def body(h_ref, out_ref, send_sems, recv_sems):
        x, y, cc = _my_pos()
        copies = []
        for j, (px, py) in enumerate([(1 - x, y), (x, 1 - y), (1 - x, 1 - y)]):
            copies.append(pltpu.make_async_remote_copy(
                src_ref=h_ref.at[2 * px + py], dst_ref=out_ref.at[j],
                send_sem=send_sems.at[j], recv_sem=recv_sems.at[j], device_id=(px, py, cc), device_id_type=MESH))
        for cp in copies:
            cp.start()
        for cp in copies:
            cp.wait()

    return pl.pallas_call(
        body, name=name, out_shape=jax.ShapeDtypeStruct((3, r, c), hsum.dtype),
        in_specs=[pl.BlockSpec(memory_space=pl.ANY)], out_specs=pl.BlockSpec(memory_space=pl.ANY),
        scratch_shapes=[pltpu.SemaphoreType.DMA((3,)), pltpu.SemaphoreType.DMA((3,))],
    )(hsum)


def _rs_add_sibling(name, g, from_sibling, cc):
    _, r, c = g.shape
    tr = _pick(r, 512, SUBLANES * 2)

    def body(cc_ref, g_ref, s_ref, o_ref, ob_ref):
        tot = g_ref[...] + s_ref[...]
        o_ref[...] = tot
        ob_ref[...] = tot.astype(BF16)

    blk = pl.BlockSpec((None, tr, c), lambda s_, i, cc_ref: (s_, i, 0))
    return pl.pallas_call(
        body, name=name,
        grid_spec=pltpu.PrefetchScalarGridSpec(
            num_scalar_prefetch=1, grid=(4, r // tr),
            in_specs=[pl.BlockSpec((None, None, tr, c), lambda s_, i, cc_ref: (s_, cc_ref[0], i, 0)), blk], out_specs=[blk, blk]),
        out_shape=[jax.ShapeDtypeStruct((4, r, c), F32), jax.ShapeDtypeStruct((4, r, c), BF16)],
        compiler_params=_params(("parallel", "parallel")),
    )(cc.reshape(1).astype(jnp.int32), g.reshape(4, 2, r, c), from_sibling)


def _rs_add_chips(name, chip_sum, from_chips, slot):
    _, r, c = chip_sum.shape
    tr = _pick(r, 512, SUBLANES * 2)

    def body(slot_ref, h_ref, f0_ref, f1_ref, f2_ref, o_ref):
        o_ref[...] = ((h_ref[...] + f0_ref[...].astype(F32)) + f1_ref[...].astype(F32)) + f2_ref[...].astype(F32)

    def from_blk(j):
        return pl.BlockSpec((None, tr, c), lambda i, slot_ref: (j, i, 0))

    return pl.pallas_call(
        body, name=name,
        grid_spec=pltpu.PrefetchScalarGridSpec(
            num_scalar_prefetch=1, grid=(r // tr,),
            in_specs=[pl.BlockSpec((None, tr, c), lambda i, slot_ref: (slot_ref[0], i, 0)), from_blk(0), from_blk(1), from_blk(2)],
            out_specs=pl.BlockSpec((tr, c), lambda i, slot_ref: (i, 0))),
        out_shape=jax.ShapeDtypeStruct((r, c), F32), compiler_params=_params(("parallel",)),
    )(slot.reshape(1).astype(jnp.int32), chip_sum, from_chips, from_chips, from_chips)


def adamw_scatter(name, w, m, v, chip_sum, from_chips, slot):
    _, a, b = w.shape
    _, r, c = chip_sum.shape
    assert a <= r and b <= c, (name, w.shape, chip_sum.shape)
    tr = _pick(a, 256, SUBLANES * 2)

    def body(slot_ref, h_ref, f0_ref, f1_ref, f2_ref, w_ref, m_ref, v_ref, g_out, d_out, m_out, v_out):
        g = ((h_ref[...] + f0_ref[...].astype(F32)) + f1_ref[...].astype(F32)) + f2_ref[...].astype(F32)
        g = g[:, :b]
        g_out[...] = g
        d_out[...], m_out[...], v_out[...] = _adamw_fn(w_ref[...], g, m_ref[...], v_ref[...])

    def from_blk(j):
        return pl.BlockSpec((None, tr, c), lambda i, slot_ref: (j, i, 0))

    mine = pl.BlockSpec((None, tr, b), lambda i, slot_ref: (0, i, 0))
    return pl.pallas_call(
        body, name=name,
        grid_spec=pltpu.PrefetchScalarGridSpec(
            num_scalar_prefetch=1, grid=(a // tr,),
            in_specs=[pl.BlockSpec((None, tr, c), lambda i, slot_ref: (slot_ref[0], i, 0)), from_blk(0), from_blk(1), from_blk(2),
                      mine, mine, mine],
            out_specs=[mine] * 4),
        out_shape=[jax.ShapeDtypeStruct((1, a, b), F32)] * 4, compiler_params=_params(("parallel",)),
    )(slot.reshape(1).astype(jnp.int32), chip_sum, from_chips, from_chips, from_chips, w, m, v)


def rs_chip_sum(tag, g):
    _, _, cc = _my_pos()
    from_sibling = exchange_sibling("rs_sibling_" + tag, g)
    return _rs_add_sibling("rs_add_sibling_" + tag, g, from_sibling, cc)


def rs_finish(tag, chip_sum, from_chips):
    x, y, _ = _my_pos()
    return _rs_add_chips("rs_add_chips_" + tag, chip_sum, from_chips, 2 * x + y)


def reduce_scatter(tag, g):
    chip_sum, chip_sum_b = rs_chip_sum(tag, g)
    return rs_finish(tag, chip_sum, exchange_chips("rs_chips_" + tag, chip_sum_b))


class GatherIci:
    def __init__(self, shards):
        self.inputs = list(shards)
        self.out_shapes = [jax.ShapeDtypeStruct((N_DEV,) + s.shape, s.dtype) for s in shards]
        self.n_remote, self.n_local = 3 * len(shards), len(shards)

    def make(self, cins, couts, send, recv, local):
        x, y, cc = _my_pos()
        me = _dev_index(x, y, cc)
        copies = []
        for w, (src, out) in enumerate(zip(cins, couts, strict=True)):
            copies.append(pltpu.make_async_copy(src, out.at[me], local.at[w]))
            for j, (px, py) in enumerate([(1 - x, y), (x, 1 - y), (1 - x, 1 - y)]):
                copies.append(pltpu.make_async_remote_copy(
                    src_ref=src, dst_ref=out.at[me], send_sem=send.at[3 * w + j], recv_sem=recv.at[3 * w + j],
                    device_id=(px, py, cc), device_id_type=MESH))
        return copies


class RsChips:
    def __init__(self, chip_sums):
        self.inputs = list(chip_sums)
        self.out_shapes = [jax.ShapeDtypeStruct((3,) + h.shape[1:], h.dtype) for h in chip_sums]
        self.n_remote, self.n_local = 3 * len(chip_sums), 0

    def make(self, cins, couts, send, recv, local):
        x, y, cc = _my_pos()
        copies = []
        for w, (h_ref, out) in enumerate(zip(cins, couts, strict=True)):
            for j, (px, py) in enumerate([(1 - x, y), (x, 1 - y), (1 - x, 1 - y)]):
                copies.append(pltpu.make_async_remote_copy(
                    src_ref=h_ref.at[2 * px + py], dst_ref=out.at[j], send_sem=send.at[3 * w + j], recv_sem=recv.at[3 * w + j],
                    device_id=(px, py, cc), device_id_type=MESH))
        return copies


class SiblingSwap:
    def __init__(self, gs):
        self.inputs = list(gs)
        self.out_shapes = [jax.ShapeDtypeStruct((4,) + g.shape[1:], g.dtype) for g in gs]
        self.n_remote, self.n_local = 4 * len(gs), 0

    def make(self, cins, couts, send, recv, local):
        x, y, cc = _my_pos()
        copies = []
        for w, (g_ref, out) in enumerate(zip(cins, couts, strict=True)):
            for px in range(2):
                for py in range(2):
                    q = 4 * w + 2 * px + py
                    copies.append(pltpu.make_async_remote_copy(
                        src_ref=g_ref.at[_dev_index(px, py, 1 - cc)], dst_ref=out.at[2 * px + py],
                        send_sem=send.at[q], recv_sem=recv.at[q], device_id=(x, y, 1 - cc), device_id_type=MESH))
        return copies


class _SemSlice:
    def __init__(self, base, start):
        self.base, self.start = base, start

    @property
    def at(self):
        return self

    def __getitem__(self, k):
        return self.base.at[self.start + k]


class CommGroup:
    def __init__(self, plans):
        self.plans = [p for p in plans if p.inputs]
        self.inputs = [a for p in self.plans for a in p.inputs]
        self.out_shapes = [s_ for p in self.plans for s_ in p.out_shapes]
        self.n_remote = sum(p.n_remote for p in self.plans)
        self.n_local = sum(p.n_local for p in self.plans)

    def make(self, cins, couts, send, recv, local):
        copies, i0, o0, r0, l0 = [], 0, 0, 0, 0
        for p in self.plans:
            ni, no = len(p.inputs), len(p.out_shapes)
            copies += p.make(cins[i0:i0 + ni], couts[o0:o0 + no], _SemSlice(send, r0), _SemSlice(recv, r0), _SemSlice(local, l0))
            i0, o0, r0, l0 = i0 + ni, o0 + no, r0 + p.n_remote, l0 + p.n_local
        return copies


def gather_d2d(name, arrays):
    n = len(arrays)

    def body(*refs):
        outs, send, recv = refs[n:2 * n], refs[2 * n], refs[2 * n + 1]
        x, y, cc = _my_pos()
        copies = []
        for w, out in enumerate(outs):
            for px in range(2):
                for py in range(2):
                    q = 4 * w + 2 * px + py
                    slab = out.at[_dev_index(px, py, cc)]
                    copies.append(pltpu.make_async_remote_copy(
                        src_ref=slab, dst_ref=slab, send_sem=send.at[q], recv_sem=recv.at[q],
                        device_id=(x, y, 1 - cc), device_id_type=MESH))
        for cp in copies:
            cp.start()
        for cp in copies:
            cp.wait()

    any_spec = pl.BlockSpec(memory_space=pl.ANY)
    return pl.pallas_call(
        body, name=name, out_shape=[jax.ShapeDtypeStruct(a.shape, a.dtype) for a in arrays],
        in_specs=[any_spec] * n, out_specs=[any_spec] * n, input_output_aliases={i: i for i in range(n)},
        scratch_shapes=[pltpu.SemaphoreType.DMA((4 * n,)), pltpu.SemaphoreType.DMA((4 * n,))],
    )(*arrays)


def _pallas(body, *, name, grid, in_specs, out_specs, out_shape, scratch_shapes=(), sem, comm=None):
    in_specs, out_specs, out_shape, scratch_shapes = list(in_specs), list(out_specs), list(out_shape), list(scratch_shapes)
    if comm is None:
        return pl.pallas_call(body, name=name, grid=grid, in_specs=in_specs, out_specs=out_specs, out_shape=out_shape,
                              scratch_shapes=scratch_shapes, compiler_params=_params(sem))
    n_in, n_out, n_scr = len(in_specs), len(out_specs), len(scratch_shapes)
    nci, nco = len(comm.inputs), len(comm.out_shapes)

    def body2(*refs):
        ins, cins = refs[:n_in], refs[n_in:n_in + nci]
        o0 = n_in + nci
        outs, couts = refs[o0:o0 + n_out], refs[o0 + n_out:o0 + n_out + nco]
        s0 = o0 + n_out + nco
        scr = refs[s0:s0 + n_scr]
        send, recv, local = refs[s0 + n_scr:]
        pids = [pl.program_id(k) for k in range(len(grid))]
        first = functools.reduce(jnp.logical_and, [p == 0 for p in pids])
        last = functools.reduce(jnp.logical_and, [p == g - 1 for p, g in zip(pids, grid)])

        @pl.when(first)
        def _():
            for cp in comm.make(cins, couts, send, recv, local):
                cp.start()

        body(*ins, *outs, *scr)

        @pl.when(last)
        def _():
            for cp in comm.make(cins, couts, send, recv, local):
                cp.wait()

    any_spec = pl.BlockSpec(memory_space=pl.ANY)
    call = pl.pallas_call(
        body2, name=name, grid=grid, in_specs=in_specs + [any_spec] * nci, out_specs=out_specs + [any_spec] * nco,
        out_shape=out_shape + list(comm.out_shapes),
        scratch_shapes=scratch_shapes + [pltpu.SemaphoreType.DMA((comm.n_remote,)), pltpu.SemaphoreType.DMA((comm.n_remote,)),
                                         pltpu.SemaphoreType.DMA((max(comm.n_local, 1),))],
        compiler_params=_params(tuple("arbitrary" for _ in grid)))
    return lambda *args: call(*args, *comm.inputs)


PACK_W = 1024


class Pack:
    def __init__(self, entries, row_unit):
        self.entries = entries
        self.sizes = [int(np.prod(sh)) for _, sh in entries]
        self.offsets = np.concatenate([[0], np.cumsum(self.sizes)]).tolist()
        self.total = _round_up(self.offsets[-1], PACK_W * row_unit)
        self.rows = self.total // PACK_W

    def pack(self, arrays, dtype, lead=()):
        flat = [arrays[n].astype(dtype).reshape(lead + (-1,)) for n, _ in self.entries]
        pad = self.total - self.offsets[-1]
        if pad:
            flat.append(jnp.zeros(lead + (pad,), dtype))
        return jnp.concatenate(flat, axis=-1).reshape(lead + (self.rows, PACK_W))

    def unpack(self, buf, lead=()):
        flat = buf.reshape(lead + (self.total,))
        out = {}
        for (n, sh), off, sz in zip(self.entries, self.offsets, self.sizes):
            out[n] = lax.slice_in_dim(flat, off, off + sz, axis=len(lead)).reshape(lead + tuple(sh))
        return out


def _gathered_to_full(g, how):
    _, a, b = g.shape
    if how == 'row':
        return g.reshape(N_DEV * a, b)
    return jnp.transpose(g, (1, 0, 2)).reshape(a, N_DEV * b)


def _full_to_shards(w, how):
    a, b = w.shape
    if how == 'row':
        return w.reshape(N_DEV, a // N_DEV, b)
    return jnp.transpose(w.reshape(a, N_DEV, b // N_DEV), (1, 0, 2))


def _to_heads(t, width):
    s, c = t.shape
    return jnp.transpose(t.reshape(s, c // width, width), (1, 0, 2))


def _from_heads(t):
    h, s, w = t.shape
    return jnp.transpose(t, (1, 0, 2)).reshape(s, h * w)


def _rot_matrix():
    half = QK_ROPE // 2
    rot = np.zeros((QK_ROPE, QK_ROPE), np.float32)
    for i in range(half):
        rot[i + half, i] = -1.0
        rot[i, i + half] = 1.0
    return jnp.asarray(rot)


def _inv_freq2():
    half = QK_ROPE // 2
    inv = ROPE_THETA ** (-np.arange(half, dtype=np.float32) / half)
    return jnp.asarray(np.concatenate([inv, inv])[None, :].astype(np.float32))


def kernel(x, positions, attn_norm_g, w_in, rwkv_mu, rwkv_w0, rwkv_w2, rwkv_a0, rwkv_a2, rwkv_g2, rwkv_k_k, rwkv_k_a, rwkv_r_k, rwkv_gn_w, rwkv_gn_b, mla_q_norm_g, mla_w_uq, mla_kv_norm_g, mla_w_ukv, w_out, ffn_norm_g, ffn_w_gate, ffn_w_up, ffn_conv_w, ffn_conv_b, ffn_w_down, final_norm_g, loss_target, m_attn_norm_g, m_w_in, m_rwkv_mu, m_rwkv_w0, m_rwkv_w2, m_rwkv_a0, m_rwkv_a2, m_rwkv_g2, m_rwkv_k_k, m_rwkv_k_a, m_rwkv_r_k, m_rwkv_gn_w, m_rwkv_gn_b, m_mla_q_norm_g, m_mla_w_uq, m_mla_kv_norm_g, m_mla_w_ukv, m_w_out, m_ffn_norm_g, m_ffn_w_gate, m_ffn_w_up, m_ffn_conv_w, m_ffn_conv_b, m_ffn_w_down, m_final_norm_g, v_attn_norm_g, v_w_in, v_rwkv_mu, v_rwkv_w0, v_rwkv_w2, v_rwkv_a0, v_rwkv_a2, v_rwkv_g2, v_rwkv_k_k, v_rwkv_k_a, v_rwkv_r_k, v_rwkv_gn_w, v_rwkv_gn_b, v_mla_q_norm_g, v_mla_w_uq, v_mla_kv_norm_g, v_mla_w_ukv, v_w_out, v_ffn_norm_g, v_ffn_w_gate, v_ffn_w_up, v_ffn_conv_w, v_ffn_conv_b, v_ffn_w_down, v_final_norm_g):
    given = dict(locals())
    wts = {n: given[n] for n in WEIGHTS}
    mom_m = {n: given["m_" + n] for n in WEIGHTS}
    mom_v = {n: given["v_" + n] for n in WEIGHTS}
    out_shapes = {n: wts[n].shape for n in WEIGHTS}

    def local2d(n, a):
        if n == 'rwkv_r_k' or a.ndim <= 2:
            return a.reshape(1, -1)
        return a.reshape(a.shape[1:])

    w2d = {n: local2d(n, wts[n]) for n in WEIGHTS}
    m2d = {n: local2d(n, mom_m[n]) for n in WEIGHTS}
    v2d = {n: local2d(n, mom_v[n]) for n in WEIGHTS}

    xs = x.reshape(x.shape[1:])
    tgt = loss_target.reshape(loss_target.shape[1:])
    s, d = xs.shape
    c_rwkv = w2d['rwkv_w0'].shape[1]
    n_rh = c_rwkv // RWKV_HEAD
    decay_lora, aaa_lora, gate_lora = w2d['rwkv_w2'].shape[0], w2d['rwkv_a2'].shape[0], w2d['rwkv_g2'].shape[0]
    q_lora, kv_lora = w2d['mla_q_norm_g'].shape[1], w2d['mla_kv_norm_g'].shape[1]
    shift_dim = w2d['rwkv_mu'].shape[1]
    d_in = w2d['w_in'].shape[1] * N_DEV
    n_mh = w2d['mla_w_uq'].shape[1] * N_DEV // (QK_NOPE + QK_ROPE)
    tm = _pick(s, 256, SUBLANES)
    tm_wide = _pick(s, 128, SUBLANES)
    tm_heads = _pick(s, 512, SUBLANES)

    nb = {n: w2d[n].shape[1] for n in BIG if BIG[n] == 'col'}
    nbp = {n: _round_up(v_, LANES) for n, v_ in nb.items()}
    shards = {}
    for n in BIG:
        w = w2d[n].astype(BF16)
        if BIG[n] == 'col':
            w = jnp.pad(w, ((0, 0), (0, nbp[n] - nb[n])))
        elif n == 'ffn_w_down':
            w = jnp.pad(w, ((0, nbp['ffn_w_gate'] - w.shape[0]), (0, 0)))
        shards[n] = w

    def as_used(n, g):
        return g if BIG[n] == 'col' else g.reshape(N_DEV * g.shape[1], g.shape[2])

    gathered = {'w_in': as_used('w_in', all_gather("gather_w_in", shards['w_in']))}
    f_pad = N_DEV * nbp['ffn_w_gate']
    small_pack = Pack([(n, w2d[n].shape) for n in SMALL_SHARDED], 8)
    small_all = all_gather("gather_small", small_pack.pack(w2d, F32))
    full = {}
    for n, g in small_pack.unpack(small_all, lead=(N_DEV,)).items():
        full[n] = _gathered_to_full(g, SMALL_SHARDED[n])
    conv_w_pad = pad_cols(full['ffn_conv_w'], nb['ffn_w_gate'], nbp['ffn_w_gate'])
    conv_b_pad = pad_cols(w2d['ffn_conv_b'], nb['ffn_w_gate'], nbp['ffn_w_gate'])

    (h1,) = rowwise("rms_attn", _rms_fn, [xs, w2d['attn_norm_g']], ['row', 'const'], [('row', d, BF16)], heads=1, s=s, tm=tm)
    def gather_behind(names, run):
        *res, = run(GatherIci([shards[n] for n in names]))
        landed = res[len(res) - len(names):]
        for n, g in zip(names, gather_d2d("gather_d2d_" + names[0], landed), strict=True):
            gathered[n] = as_used(n, g)
        return res[:len(res) - len(names)]

    w_in_nat = jnp.transpose(gathered['w_in'], (1, 0, 2))[:, :, :nb['w_in']].reshape(d, d_in)
    w_in_nat = jnp.pad(w_in_nat, ((0, 0), (0, _round_up(d_in, LANES) - d_in)))
    (proj,) = gather_behind(['mla_w_uq', 'mla_w_ukv', 'w_out'], lambda c: mm("proj_in", h1, w_in_nat, comm=c))
    p_rwkv = proj[:, :shift_dim]
    c_q = proj[:, shift_dim:shift_dim + q_lora]
    c_kv = proj[:, shift_dim + q_lora:shift_dim + q_lora + kv_lora]
    k_pe = proj[:, shift_dim + q_lora + kv_lora:d_in]
    shifted = token_shift_fwd(p_rwkv, w2d['rwkv_mu'], tm_wide)
    o1, o2, o3 = c_rwkv, 2 * c_rwkv, 3 * c_rwkv
    hr = _to_heads(shifted[:, :o1], RWKV_HEAD)
    hk = _to_heads(shifted[:, o1:o2], RWKV_HEAD)
    hv = _to_heads(shifted[:, o2:o3], RWKV_HEAD)
    hw = shifted[:, o3:o3 + decay_lora]
    ha = shifted[:, o3 + decay_lora:o3 + decay_lora + aaa_lora]
    hg = shifted[:, o3 + decay_lora + aaa_lora:]

    def per_head(vec):
        return vec.reshape(n_rh, 1, RWKV_HEAD)

    def lora_heads(w):
        return jnp.transpose(w.reshape(w.shape[0], n_rh, RWKV_HEAD), (1, 0, 2))

    pre_args = [hk, hw, ha, hg, per_head(w2d['rwkv_w0']), lora_heads(full['rwkv_w2']), per_head(w2d['rwkv_a0']),
                lora_heads(full['rwkv_a2']), lora_heads(full['rwkv_g2']), per_head(w2d['rwkv_k_k']), per_head(w2d['rwkv_k_a'])]
    pre_kinds = ['hrow', 'row', 'row', 'row', 'hconst', 'hconst', 'hconst', 'hconst', 'hconst', 'hconst', 'hconst']
    decay, kx, a_sc, b_sc, gate_r = gather_behind(['ffn_w_gate'], lambda c: rowwise(
        "rwkv_pre", _rwkv_pre_fn, pre_args, pre_kinds, [('hrow', RWKV_HEAD, F32)] * 5, heads=n_rh, s=s, tm=tm_heads, comm=c))
    y_scan, ckpt = gather_behind(['ffn_w_up'], lambda c: scan_fwd(hr, decay, kx, hv, a_sc, b_sc, comm=c))
    post_args = [y_scan, hr, kx, hv, gate_r, per_head(w2d['rwkv_gn_w']), per_head(w2d['rwkv_gn_b']), per_head(w2d['rwkv_r_k'])]
    post_kinds = ['hrow'] * 5 + ['hconst'] * 3
    (y_rwkv_h,) = rowwise("rwkv_post", _rwkv_post_fn, post_args, post_kinds, [('hrow', RWKV_HEAD, F32)], heads=n_rh, s=s, tm=tm_heads)

    pos = positions.reshape(s, 1).astype(F32)
    rot, inv2 = _rot_matrix(), _inv_freq2()
    mla_args = [c_q, c_kv, k_pe, pos, w2d['mla_q_norm_g'], w2d['mla_kv_norm_g'], inv2, rot]
    mla_kinds = ['row', 'row', 'row', 'row', 'const', 'const', 'const', 'const']
    qn, kvn, kp_rot, cos2, sin2 = rowwise(
        "mla_pre", _mla_pre_fn, mla_args, mla_kinds,
        [('row', q_lora, BF16), ('row', kv_lora, BF16), ('row', QK_ROPE, F32), ('row', QK_ROPE, F32), ('row', QK_ROPE, F32)],
        heads=1, s=s, tm=tm)
    assert n_mh == N_DEV and nb['mla_w_uq'] == QK_NOPE + QK_ROPE and nb['mla_w_ukv'] == QK_NOPE + V_HEAD
    assert nbp['mla_w_uq'] == ATTN_SLAB and nbp['mla_w_ukv'] == ATTN_SLAB
    q_h = mm_sh("proj_q", qn, gathered['mla_w_uq'], slabs=True)
    kv_h = mm_sh("proj_kv", kvn, gathered['mla_w_ukv'], slabs=True)
    o_att, lse = gather_behind(['ffn_w_down'], lambda c: attn_fwd(q_h, kv_h, kp_rot, cos2, sin2, rot, comm=c))
    ycat = jnp.concatenate([_from_heads(y_rwkv_h), _from_heads(o_att)], axis=-1).astype(BF16)
    x1 = mm("proj_out", ycat, gathered['w_out'], add=xs)
    (h2,) = rowwise("rms_ffn", _rms_fn, [x1, w2d['ffn_norm_g']], ['row', 'const'], [('row', d, BF16)], heads=1, s=s, tm=tm)
    gate_pre = mm_sh("ffn_gate", h2, gathered['ffn_w_gate'])
    up = mm_sh("ffn_up", h2, gathered['ffn_w_up'])
    act = ffn_act_fwd(gate_pre, up, conv_w_pad, conv_b_pad)
    x2 = mm("ffn_down", act, gathered['ffn_w_down'], add=x1)

    ones = jnp.ones((s, 1), F32)
    fin_g = w2d['final_norm_g']
    d_x2, dg_final_p, loss_rows = rowwise_vjp("loss_bwd", _loss_fn, [x2, fin_g, tgt], ['row', 'const', 'row'], [ones], ['row'],
                                              [0, 1], heads=1, s=s, tm=tm, primal=True)
    d_x2_b = d_x2.astype(BF16)
    d_act = mm_nt("d_act", d_x2_b, gathered['ffn_w_down'], out_dtype=BF16)
    gsh, chip_sums, from_chips = {}, {}, {}

    def scatter_behind(run, ici=(), swap=()):
        *res, = run(CommGroup([RsChips([chip_sums[n][1] for n in ici]), SiblingSwap([gsh[n] for n in swap])]))
        n_own = len(res) - len(ici) - len(swap)
        from_chips.update(zip(ici, res[n_own:n_own + len(ici)], strict=True))
        _, _, cc = _my_pos()
        for n, from_sibling in zip(swap, res[n_own + len(ici):], strict=True):
            chip_sums[n] = _rs_add_sibling("rs_add_sibling_" + n, gsh[n], from_sibling, cc)
        return res[:n_own]

    gsh['ffn_w_down'] = mm_tn("dw_down", act, d_x2_b).reshape(N_DEV, nbp['ffn_w_gate'], d)
    d_gate, d_up, dcw_p, dcb_p = ffn_act_bwd1(gate_pre, up, conv_w_pad, conv_b_pad, d_act)
    d_gp = ffn_act_bwd2(d_gate, conv_w_pad)
    (d_h2_g,) = scatter_behind(lambda c: mm_sh_nt("d_h2_gate", d_gp, gathered['ffn_w_gate'], comm=c), swap=['ffn_w_down'])
    d_h2 = mm_sh_nt("d_h2_up", d_up, gathered['ffn_w_up'], add=d_h2_g)
    gsh['ffn_w_gate'] = mm_sh_out("dw_gate", h2, d_gp)
    (gsh['ffn_w_up'],) = scatter_behind(lambda c: mm_sh_out("dw_up", h2, d_up, comm=c), swap=['ffn_w_gate'])
    d_x1, dg_ffn_p = rowwise_vjp("rms_ffn_bwd", _rms_fn, [x1, w2d['ffn_norm_g']], ['row', 'const'], [d_h2], ['row'], [0, 1],
                                 heads=1, s=s, tm=tm, plus=d_x2)
    d_x1_b = d_x1.astype(BF16)
    d_ycat = mm_nt("d_ycat", d_x1_b, gathered['w_out'])
    gsh['w_out'] = mm_tn("dw_out", ycat, d_x1_b).reshape((N_DEV,) + w2d['w_out'].shape)
    d_yr_h = _to_heads(d_ycat[:, :c_rwkv], RWKV_HEAD)
    d_o_h = _to_heads(d_ycat[:, c_rwkv:], V_HEAD)

    d_q, d_kv, d_kp_h = scatter_behind(
        lambda c: attn_bwd(q_h, kv_h, kp_rot, cos2, sin2, rot, o_att, lse, d_o_h, comm=c),
        ici=['ffn_w_down'], swap=['ffn_w_up', 'w_out'])
    d_kp_rot = headsum("d_kpe_heads", d_kp_h)
    d_qn = mm_sh_nt("d_qn", d_q, gathered['mla_w_uq'])
    d_kvn = mm_sh_nt("d_kvn", d_kv, gathered['mla_w_ukv'])
    gsh['mla_w_uq'] = mm_sh_out("dw_uq", qn, d_q)
    gsh['mla_w_ukv'] = mm_sh_out("dw_ukv", kvn, d_kv)
    d_cq, d_ckv, d_kpe, dg_q_p, dg_kv_p = rowwise_vjp(
        "mla_pre_bwd", _mla_pre_grad_fn, mla_args, mla_kinds, [d_qn, d_kvn, d_kp_rot], ['row', 'row', 'row'], [0, 1, 2, 4, 5],
        heads=1, s=s, tm=tm)

    d_y, d_r_post, d_k_post, d_v_post, d_gate_r, dgnw_p, dgnb_p, drk_p = scatter_behind(
        lambda c: rowwise_vjp("rwkv_post_bwd", _rwkv_post_fn, post_args, post_kinds, [d_yr_h], ['hrow'], list(range(8)),
                              heads=n_rh, s=s, tm=tm_heads, comm=c),
        ici=['ffn_w_gate'], swap=['mla_w_uq', 'mla_w_ukv'])
    d_r_sc, d_w_sc, d_k_sc, d_v_sc, d_a_sc, d_b_sc = scatter_behind(
        lambda c: scan_bwd(hr, decay, kx, hv, a_sc, b_sc, ckpt, d_y, comm=c), ici=['ffn_w_up', 'w_out'])
    d_hk, d_hw_p, d_ha_p, d_hg_p, dw0_p, dw2_p, da0_p, da2_p, dg2_p, dkk_p, dka_p, d_hr, d_hv = scatter_behind(
        lambda c: rowwise_vjp(
            "rwkv_pre_bwd", _rwkv_pre_grad_fn, pre_args + [hr, hv], pre_kinds + ['hrow', 'hrow'],
            [d_w_sc, d_k_sc, d_k_post, d_a_sc, d_b_sc, d_gate_r, d_r_sc, d_r_post, d_v_sc, d_v_post], ['hrow'] * 10,
            list(range(13)), heads=n_rh, s=s, tm=tm_heads, comm=c),
        ici=['mla_w_uq', 'mla_w_ukv'])
    d_shifted = jnp.concatenate([_from_heads(d_hr), _from_heads(d_hk), _from_heads(d_hv), headsum("d_hw_heads", d_hw_p),
                                 headsum("d_ha_heads", d_ha_p), headsum("d_hg_heads", d_hg_p)], axis=-1)
    d_p_rwkv, dmu_p = token_shift_bwd(p_rwkv, w2d['rwkv_mu'], d_shifted, tm_wide)
    d_proj = pad_cols(jnp.concatenate([d_p_rwkv, d_cq, d_ckv, d_kpe], axis=-1).astype(BF16), nb['w_in'], nbp['w_in'])
    gsh['w_in'] = mm_sh_out("dw_in", h1, d_proj)
    chip_sums['w_in'] = rs_chip_sum('w_in', gsh['w_in'])
    (d_h1,) = scatter_behind(lambda c: mm_sh_nt("d_h1", d_proj, gathered['w_in'], comm=c), ici=['w_in'])
    grad_x, dg_attn_p = rowwise_vjp("rms_attn_bwd", _rms_fn, [xs, w2d['attn_norm_g']], ['row', 'const'], [d_h1], ['row'], [0, 1],
                                    heads=1, s=s, tm=tm, plus=d_x1)

    def from_heads_lora(g):
        return jnp.transpose(g, (1, 0, 2)).reshape(g.shape[1], n_rh * RWKV_HEAD)

    gw = {}
    gw['rwkv_w2'] = from_heads_lora(sum_partials("sum_dw2", dw2_p, True))
    gw['rwkv_a2'] = from_heads_lora(sum_partials("sum_da2", da2_p, True))
    gw['rwkv_g2'] = from_heads_lora(sum_partials("sum_dg2", dg2_p, True))
    dcw_pad = colsum("sum_dconv_w", dcw_p.reshape(dcw_p.shape[0], CONV_W * f_pad)).reshape(CONV_W, f_pad)
    gw['ffn_conv_w'] = unpad_cols(dcw_pad, nb['ffn_w_gate'], nbp['ffn_w_gate'])

    rep = {
        'attn_norm_g': sum_partials("sum_dg_attn", dg_attn_p, False),
        'rwkv_mu': colsum("sum_dmu", dmu_p.reshape(dmu_p.shape[0], shift_dim)),
        'rwkv_w0': sum_partials("sum_dw0", dw0_p, True).reshape(1, c_rwkv),
        'rwkv_a0': sum_partials("sum_da0", da0_p, True).reshape(1, c_rwkv),
        'rwkv_k_k': sum_partials("sum_dkk", dkk_p, True).reshape(1, c_rwkv),
        'rwkv_k_a': sum_partials("sum_dka", dka_p, True).reshape(1, c_rwkv),
        'rwkv_r_k': sum_partials("sum_drk", drk_p, True).reshape(1, c_rwkv),
        'rwkv_gn_w': sum_partials("sum_dgnw", dgnw_p, True).reshape(1, c_rwkv),
        'rwkv_gn_b': sum_partials("sum_dgnb", dgnb_p, True).reshape(1, c_rwkv),
        'mla_q_norm_g': sum_partials("sum_dg_q", dg_q_p, False),
        'mla_kv_norm_g': sum_partials("sum_dg_kv", dg_kv_p, False),
        'ffn_norm_g': sum_partials("sum_dg_ffn", dg_ffn_p, False),
        'ffn_conv_b': unpad_cols(colsum("sum_dconv_b", dcb_p.reshape(dcb_p.shape[0], f_pad)), nb['ffn_w_gate'], nbp['ffn_w_gate']),
        'final_norm_g': sum_partials("sum_dg_final", dg_final_p, False),
        'loss': sum_all("sum_loss", loss_rows.reshape(s // SUBLANES, SUBLANES)),
    }
    rep_pack = Pack([(n, w2d[n].shape) for n in REPLICATED] + [('loss', (1, 1))], 8)
    rep_all = all_gather("gather_rep_grads", rep_pack.pack(rep, F32))
    rep_sum = colsum("sum_rep_grads", rep_all.reshape(N_DEV, rep_pack.total)).reshape(rep_pack.rows, PACK_W)
    rep_g = rep_pack.unpack(rep_sum)
    loss = rep_g.pop('loss').reshape(())

    grads, deltas, new_m, new_v = dict(rep_g), {}, {}, {}
    my_x, my_y, _ = _my_pos()
    for n in BIG:
        grads[n], deltas[n], new_m[n], new_v[n] = adamw_scatter(
            "adamw_" + n, wts[n], mom_m[n], mom_v[n], chip_sums[n][0], from_chips[n], 2 * my_x + my_y)
    sm_pack = Pack([(n, w2d[n].shape) for n in SMALL_SHARDED], 8)
    g_shards = {n: _full_to_shards(gw[n], SMALL_SHARDED[n]) for n in SMALL_SHARDED}
    grads.update(sm_pack.unpack(reduce_scatter("small", sm_pack.pack(g_shards, F32, lead=(N_DEV,)))))
    rest_pack = Pack([(n, w2d[n].shape) for n in WEIGHTS if n not in BIG], 8)
    d_r, m_r, v_r = rowwise(
        "adamw_small", _adamw_fn, [rest_pack.pack(w2d, F32), rest_pack.pack(grads, F32), rest_pack.pack(m2d, F32),
                                   rest_pack.pack(v2d, F32)],
        ['row'] * 4, [('row', PACK_W, F32)] * 3, heads=1, s=rest_pack.rows, tm=_pick(rest_pack.rows, 512, SUBLANES))
    deltas.update(rest_pack.unpack(d_r))
    new_m.update(rest_pack.unpack(m_r))
    new_v.update(rest_pack.unpack(v_r))

    def shaped(dct):
        return [dct[n].reshape(out_shapes[n]) for n in WEIGHTS]

    return (loss, grad_x.reshape(x.shape), *shaped(grads), *shaped(deltas), *shaped(new_m), *shaped(new_v))
```

```python
import functools
import math

import jax
import jax.numpy as jnp
import numpy as np
from jax import lax
from jax.experimental import pallas as pl
from jax.experimental.pallas import tpu as pltpu

F32 = jnp.float32
BF16 = jnp.bfloat16
HIGHEST = lax.Precision.HIGHEST
MESH = pl.DeviceIdType.MESH

N_DEV = 8
LANES = 128
SUBLANES = 8
VMEM_LIMIT = 48 * 1024 * 1024
RESIDENT_BYTES = 8 * 1024 * 1024

NORM_EPS = 1e-6
GN_EPS = 64e-5
RWKV_HEAD = 64
QK_NOPE = 128
QK_ROPE = 64
V_HEAD = 128
ROPE_THETA = 10000.0
CONV_W = 3
NEG_INF = -1e30
SCAN_CHUNK = 64
SCAN_HEADS = 16
SCAN_PASSES_SOLVE = 1
SCAN_PASSES_OUT = 1

ADAM_LR = 0.001
ADAM_B1 = 0.9
ADAM_B2 = 0.999
ADAM_EPS = 1e-08
ADAM_WD = 0.01
ADAM_STEP = 10

WEIGHTS = ['attn_norm_g', 'w_in', 'rwkv_mu', 'rwkv_w0', 'rwkv_w2', 'rwkv_a0', 'rwkv_a2', 'rwkv_g2', 'rwkv_k_k',
           'rwkv_k_a', 'rwkv_r_k', 'rwkv_gn_w', 'rwkv_gn_b', 'mla_q_norm_g', 'mla_w_uq', 'mla_kv_norm_g', 'mla_w_ukv',
           'w_out', 'ffn_norm_g', 'ffn_w_gate', 'ffn_w_up', 'ffn_conv_w', 'ffn_conv_b', 'ffn_w_down', 'final_norm_g']
BIG = {'w_in': 'col', 'mla_w_uq': 'col', 'mla_w_ukv': 'col', 'w_out': 'row', 'ffn_w_gate': 'col', 'ffn_w_up': 'col',
       'ffn_w_down': 'row'}
SMALL_SHARDED = {'rwkv_w2': 'col', 'rwkv_a2': 'col', 'rwkv_g2': 'col', 'ffn_conv_w': 'col'}
SHARDED = {**BIG, **SMALL_SHARDED}
REPLICATED = [n for n in WEIGHTS if n not in SHARDED]


def _round_up(n, m):
    return (n + m - 1) // m * m


def _pick(n, cap, unit):
    if n <= cap:
        return n
    best = None
    for t in range(unit, cap + 1, unit):
        if n % t == 0:
            best = t
    assert best is not None, (n, cap, unit)
    return best


def _params(sem):
    return pltpu.CompilerParams(dimension_semantics=sem, vmem_limit_bytes=VMEM_LIMIT)


def mm(name, a, b, add=None, out_dtype=F32, comm=None):
    m, k = a.shape
    k2, n = b.shape
    assert k == k2, (name, a.shape, b.shape)
    tn = n if k * n * 2 <= RESIDENT_BYTES else _pick(n, 640, LANES)
    tm = _pick(m, 2048 if (tn < n and m * k * 2 <= RESIDENT_BYTES) else 512, SUBLANES * 2)
    has_add = add is not None

    def body(a_ref, b_ref, *rest):
        o_ref = rest[-1]
        acc = jnp.dot(a_ref[...].astype(BF16), b_ref[...].astype(BF16), preferred_element_type=F32)
        if has_add:
            acc = acc + rest[0][...].astype(F32)
        o_ref[...] = acc.astype(o_ref.dtype)

    in_specs = [pl.BlockSpec((tm, k), lambda i, j: (i, 0)), pl.BlockSpec((k, tn), lambda i, j: (0, j))]
    ops = [a, b]
    if has_add:
        in_specs.append(pl.BlockSpec((tm, tn), lambda i, j: (i, j)))
        ops.append(add)
    res = _pallas(
        body, name=name, grid=(m // tm, n // tn), in_specs=in_specs,
        out_specs=[pl.BlockSpec((tm, tn), lambda i, j: (i, j))],
        out_shape=[jax.ShapeDtypeStruct((m, n), out_dtype)], sem=("parallel", "parallel"), comm=comm,
    )(*ops)
    return res[0] if comm is None else res


def mm_nt(name, a, b, out_dtype=F32):
    m, k = a.shape
    n, k2 = b.shape
    assert k == k2, (name, a.shape, b.shape)
    tm = _pick(m, 2048 if m * k * 2 <= RESIDENT_BYTES else 512, SUBLANES * 2)
    tn = _pick(n, 1024, LANES)

    def body(a_ref, b_ref, o_ref):
        acc = lax.dot_general(a_ref[...].astype(BF16), b_ref[...].astype(BF16), (((1,), (1,)), ((), ())),
                              preferred_element_type=F32)
        o_ref[...] = acc.astype(o_ref.dtype)

    return pl.pallas_call(
        body, name=name, grid=(m // tm, n // tn),
        in_specs=[pl.BlockSpec((tm, k), lambda i, j: (i, 0)), pl.BlockSpec((tn, k), lambda i, j: (j, 0))],
        out_specs=pl.BlockSpec((tm, tn), lambda i, j: (i, j)),
        out_shape=jax.ShapeDtypeStruct((m, n), out_dtype),
        compiler_params=_params(("parallel", "parallel")),
    )(a, b)


def mm_sh(name, a, g, out_dtype=F32, comm=None, slabs=False):
    m, k = a.shape
    nd, k2, nbp = g.shape
    assert k == k2, (name, a.shape, g.shape)
    tm = _pick(m, 2048 if m * k * 2 <= RESIDENT_BYTES else 512, SUBLANES * 2)

    def body(a_ref, b_ref, o_ref):
        o_ref[...] = jnp.dot(a_ref[...].astype(BF16), b_ref[...].astype(BF16), preferred_element_type=F32).astype(o_ref.dtype)

    if slabs:
        out_spec, out_shape = pl.BlockSpec((None, tm, nbp), lambda i, j: (j, i, 0)), (nd, m, nbp)
    else:
        out_spec, out_shape = pl.BlockSpec((tm, nbp), lambda i, j: (i, j)), (m, nd * nbp)
    res = _pallas(
        body, name=name, grid=(m // tm, nd),
        in_specs=[pl.BlockSpec((tm, k), lambda i, j: (i, 0)), pl.BlockSpec((None, k, nbp), lambda i, j: (j, 0, 0))],
        out_specs=[out_spec], out_shape=[jax.ShapeDtypeStruct(out_shape, out_dtype)], sem=("parallel", "parallel"), comm=comm,
    )(a, g)
    return res[0] if comm is None else res


def mm_sh_nt(name, a, g, add=None, comm=None):
    nd, k, nbp = g.shape
    slabs = a.ndim == 3
    m = a.shape[1] if slabs else a.shape[0]
    assert a.shape == ((nd, m, nbp) if slabs else (m, nd * nbp)), (name, a.shape, g.shape)
    has_add = add is not None
    tm = _pick(m, 512 if has_add else 1024, SUBLANES * 2)

    def body(a_ref, b_ref, *rest):
        o_ref = rest[-1]
        part = lax.dot_general(a_ref[...].astype(BF16), b_ref[...].astype(BF16), (((1,), (1,)), ((), ())),
                               preferred_element_type=F32)

        @pl.when(pl.program_id(1) == 0)
        def _():
            o_ref[...] = part + rest[0][...] if has_add else part

        @pl.when(pl.program_id(1) != 0)
        def _():
            o_ref[...] += part

    a_spec = pl.BlockSpec((None, tm, nbp), lambda i, j: (j, i, 0)) if slabs else pl.BlockSpec((tm, nbp), lambda i, j: (i, j))
    in_specs = [a_spec, pl.BlockSpec((None, k, nbp), lambda i, j: (j, 0, 0))]
    ops = [a, g]
    if has_add:
        in_specs.append(pl.BlockSpec((tm, k), lambda i, j: (i, 0)))
        ops.append(add)
    res = _pallas(
        body, name=name, grid=(m // tm, nd), in_specs=in_specs,
        out_specs=[pl.BlockSpec((tm, k), lambda i, j: (i, 0))],
        out_shape=[jax.ShapeDtypeStruct((m, k), F32)], sem=("parallel", "arbitrary"), comm=comm,
    )(*ops)
    return res[0] if comm is None else res


def mm_tn(name, a, b):
    m, k = a.shape
    m2, n = b.shape
    assert m == m2, (name, a.shape, b.shape)
    tk = _pick(k, 512, LANES)
    tn = n if m * n * 2 <= RESIDENT_BYTES else _pick(n, 640, LANES)

    def body(a_ref, b_ref, o_ref):
        o_ref[...] = lax.dot_general(a_ref[...].astype(BF16), b_ref[...].astype(BF16), (((0,), (0,)), ((), ())),
                                     preferred_element_type=F32)

    return pl.pallas_call(
        body, name=name, grid=(k // tk, n // tn),
        in_specs=[pl.BlockSpec((m, tk), lambda i, j: (0, i)), pl.BlockSpec((m, tn), lambda i, j: (0, j))],
        out_specs=pl.BlockSpec((tk, tn), lambda i, j: (i, j)),
        out_shape=jax.ShapeDtypeStruct((k, n), F32),
        compiler_params=_params(("parallel", "parallel")),
    )(a, b)


def mm_sh_out(name, a, b, comm=None):
    m, k = a.shape
    slabs = b.ndim == 3
    nbp = b.shape[2] if slabs else b.shape[1] // N_DEV
    assert b.shape == ((N_DEV, m, nbp) if slabs else (m, N_DEV * nbp)), (name, a.shape, b.shape)
    tk = _pick(k, 2048 if k * m * 2 <= RESIDENT_BYTES else 512, LANES)
    b_spec = pl.BlockSpec((None, m, nbp), lambda i, j: (j, 0, 0)) if slabs else pl.BlockSpec((m, nbp), lambda i, j: (0, j))

    def body(a_ref, b_ref, o_ref):
        o_ref[...] = lax.dot_general(a_ref[...].astype(BF16), b_ref[...].astype(BF16), (((0,), (0,)), ((), ())),
                                     preferred_element_type=F32)

    res = _pallas(
        body, name=name, grid=(k // tk, N_DEV),
        in_specs=[pl.BlockSpec((m, tk), lambda i, j: (0, i)), b_spec],
        out_specs=[pl.BlockSpec((None, tk, nbp), lambda i, j: (j, i, 0))],
        out_shape=[jax.ShapeDtypeStruct((N_DEV, k, nbp), F32)], sem=("parallel", "parallel"), comm=comm,
    )(a, b)
    return res[0] if comm is None else res


def pad_cols(y, nb, nbp):
    m = y.shape[0]
    if nb == nbp:
        return y
    return jnp.pad(y.reshape(m, N_DEV, nb), ((0, 0), (0, 0), (0, nbp - nb))).reshape(m, N_DEV * nbp)


def unpad_cols(y, nb, nbp):
    m = y.shape[0]
    if nb == nbp:
        return y
    return y.reshape(m, N_DEV, nbp)[:, :, :nb].reshape(m, N_DEV * nb)


def _in_spec(kind, a, tm):
    if kind == 'row':
        return pl.BlockSpec((tm, a.shape[1]), lambda h, i: (i, 0))
    if kind == 'hrow':
        return pl.BlockSpec((None, tm, a.shape[2]), lambda h, i: (h, i, 0))
    if kind == 'const':
        return pl.BlockSpec(a.shape, lambda h, i: (0, 0))
    assert kind == 'hconst', kind
    return pl.BlockSpec((None,) + a.shape[1:], lambda h, i: (h, 0, 0))


def _row_out(kind, c, dtype, heads, s, tm):
    if kind == 'row':
        assert heads == 1
        return jax.ShapeDtypeStruct((s, c), dtype), pl.BlockSpec((tm, c), lambda h, i: (i, 0))
    return jax.ShapeDtypeStruct((heads, s, c), dtype), pl.BlockSpec((None, tm, c), lambda h, i: (h, i, 0))


def rowwise(name, fn, arrs, kinds, outs, *, heads, s, tm, comm=None):
    n_in = len(arrs)

    def body(*refs):
        vals = fn(*[r[...] for r in refs[:n_in]])
        for o, v in zip(refs[n_in:], vals, strict=True):
            o[...] = v.astype(o.dtype)

    shapes, specs = zip(*[_row_out(k, c, dt, heads, s, tm) for k, c, dt in outs])
    return _pallas(
        body, name=name, grid=(heads, s // tm),
        in_specs=[_in_spec(k, a, tm) for k, a in zip(kinds, arrs, strict=True)],
        out_specs=list(specs), out_shape=list(shapes), sem=("parallel", "parallel"), comm=comm,
    )(*arrs)


def rowwise_vjp(name, fn, arrs, kinds, cots, cot_kinds, wrt, *, heads, s, tm, out_dtypes=None, primal=False, comm=None,
                plus=None):
    n_in, n_cot = len(arrs), len(cots)
    nb = s // tm
    out_dtypes = out_dtypes or [F32] * len(wrt)
    extra = [] if plus is None else [plus]

    def body(*refs):
        vals = [r[...] for r in refs[:n_in]]
        cvals = tuple(r[...].astype(F32) for r in refs[n_in:n_in + n_cot])
        outs = refs[n_in + n_cot + len(extra):]

        def f(*dv):
            full = list(vals)
            for j, i in enumerate(wrt):
                full[i] = dv[j]
            return tuple(fn(*full))

        prim, vjp_fn = jax.vjp(f, *[vals[i].astype(F32) for i in wrt])
        grads = list(vjp_fn(cvals))
        if plus is not None:
            grads[0] = grads[0] + refs[n_in + n_cot][...]
        for o, g in zip(outs[:len(wrt)], grads, strict=True):
            o[...] = g.astype(o.dtype)
        if primal:
            for o, p in zip(outs[len(wrt):], prim, strict=True):
                o[...] = p.astype(o.dtype)

    shapes, specs = [], []
    for i, dt in zip(wrt, out_dtypes, strict=True):
        kind, a = kinds[i], arrs[i]
        if kind in ('row', 'hrow'):
            c = a.shape[-1]
            sh, sp = _row_out('row' if (kind == 'row' and heads == 1) else 'hrow', c, dt, heads, s, tm)
        else:
            r, c = a.shape[-2:]
            sh = jax.ShapeDtypeStruct((heads, nb, r, c), dt)
            sp = pl.BlockSpec((None, None, r, c), lambda h, i: (h, i, 0, 0))
        shapes.append(sh)
        specs.append(sp)
    if primal:
        for ck, c in zip(cot_kinds, cots, strict=True):
            sh, sp = _row_out(ck, c.shape[-1], F32, heads, s, tm)
            shapes.append(sh)
            specs.append(sp)
    in_specs = [_in_spec(k, a, tm) for k, a in zip(kinds, arrs, strict=True)]
    in_specs += [_in_spec(k, a, tm) for k, a in zip(cot_kinds, cots, strict=True)]
    in_specs += [_in_spec('row', a, tm) for a in extra]
    return _pallas(
        body, name=name, grid=(heads, nb), in_specs=in_specs, out_specs=specs, out_shape=shapes,
        sem=("parallel", "parallel"), comm=comm,
    )(*arrs, *cots, *extra)


def colsum(name, x):
    n, m = x.shape
    tc = _pick(m, 32768, LANES) if m % LANES == 0 else m

    def body(x_ref, o_ref):
        acc = x_ref[0:1, :].astype(F32)
        for r in range(1, n):
            acc = acc + x_ref[r:r + 1, :].astype(F32)
        o_ref[...] = acc

    return pl.pallas_call(
        body, name=name, grid=(m // tc,), in_specs=[pl.BlockSpec((n, tc), lambda j: (0, j))],
        out_specs=pl.BlockSpec((1, tc), lambda j: (0, j)), out_shape=jax.ShapeDtypeStruct((1, m), F32),
        compiler_params=_params(("parallel",)),
    )(x)


def headsum(name, x):
    h, s, c = x.shape
    tm = _pick(s, 256, SUBLANES)

    def body(x_ref, o_ref):
        acc = x_ref[0]
        for j in range(1, h):
            acc = acc + x_ref[j]
        o_ref[...] = acc

    return pl.pallas_call(
        body, name=name, grid=(s // tm,), in_specs=[pl.BlockSpec((h, tm, c), lambda i: (0, i, 0))],
        out_specs=pl.BlockSpec((tm, c), lambda i: (i, 0)), out_shape=jax.ShapeDtypeStruct((s, c), F32),
        compiler_params=_params(("parallel",)),
    )(x)


def sum_all(name, x):
    def body(x_ref, o_ref):
        o_ref[...] = jnp.sum(x_ref[...], keepdims=True)

    return pl.pallas_call(body, name=name, out_shape=jax.ShapeDtypeStruct((1, 1), F32))(x)


def sum_partials(name, p, per_head):
    h, nb, r, c = p.shape
    if per_head:
        flat = jnp.transpose(p, (1, 0, 2, 3)).reshape(nb, h * r * c)
        if nb == 1:
            return flat.reshape(h, r, c)
        return colsum(name, flat).reshape(h, r, c)
    flat = p.reshape(h * nb, r * c)
    if h * nb == 1:
        return flat.reshape(r, c)
    return colsum(name, flat).reshape(r, c)


def _rms_fn(x, g):
    xf = x.astype(F32)
    return (xf * lax.rsqrt(jnp.mean(xf * xf, axis=-1, keepdims=True) + NORM_EPS) * g,)


def _softplus(z):
    return jnp.maximum(z, 0.0) + jnp.log(1.0 + jnp.exp(-jnp.abs(z)))


def _rwkv_pre_fn(hk, hw, ha, hg, w0, w2, a0, a2, g2, k_k, k_a):
    zw = w0 + jnp.dot(jnp.tanh(hw), w2, preferred_element_type=F32)
    w_log = -_softplus(-zw) - 0.5
    decay = jnp.exp(-jnp.exp(w_log))
    a = jax.nn.sigmoid(a0 + jnp.dot(ha, a2, preferred_element_type=F32))
    g = jnp.dot(jax.nn.sigmoid(hg), g2, preferred_element_type=F32)
    kk = hk * k_k
    kk = kk * lax.rsqrt(jnp.maximum(jnp.sum(kk * kk, axis=-1, keepdims=True), 1e-24))
    k = hk * (1.0 + (a - 1.0) * k_a)
    return decay, k, -kk, kk * a, g


def _rwkv_pre_grad_fn(hk, hw, ha, hg, w0, w2, a0, a2, g2, k_k, k_a, hr, hv):
    decay, k, a_sc, b_sc, g = _rwkv_pre_fn(hk, hw, ha, hg, w0, w2, a0, a2, g2, k_k, k_a)
    return decay, k, k, a_sc, b_sc, g, hr, hr, hv, hv


def _rwkv_post_fn(y, r, k, v, g, gn_w, gn_b, r_k):
    mu = jnp.mean(y, axis=-1, keepdims=True)
    var = jnp.mean(jnp.square(y - mu), axis=-1, keepdims=True)
    yn = (y - mu) * lax.rsqrt(var + GN_EPS) * gn_w + gn_b
    bonus = jnp.sum(r * k * r_k, axis=-1, keepdims=True) * v
    return ((yn + bonus) * g,)


def _rope_tables(pos, inv_freq2):
    ang = pos * inv_freq2
    return jnp.cos(ang), jnp.sin(ang)


def _rope(t, cos2, sin2, rot):
    return t * cos2 + jnp.dot(t, rot, precision=HIGHEST, preferred_element_type=F32) * sin2


def _mla_pre_fn(c_q, c_kv, k_pe, pos, q_g, kv_g, inv_freq2, rot):
    cos2, sin2 = _rope_tables(pos, inv_freq2)
    return _rms_fn(c_q, q_g)[0], _rms_fn(c_kv, kv_g)[0], _rope(k_pe, cos2, sin2, rot), cos2, sin2


def _mla_pre_grad_fn(c_q, c_kv, k_pe, pos, q_g, kv_g, inv_freq2, rot):
    return _mla_pre_fn(c_q, c_kv, k_pe, pos, q_g, kv_g, inv_freq2, rot)[:3]


def _loss_fn(x2, g, target):
    y = _rms_fn(x2, g)[0]
    return (0.5 * jnp.mean(jnp.square(y - target), axis=-1, keepdims=True),)


def _adamw_fn(w, g, m, v):
    m = ADAM_B1 * m + (1.0 - ADAM_B1) * g
    v = ADAM_B2 * v + (1.0 - ADAM_B2) * jnp.square(g)
    m_hat = m / (1.0 - ADAM_B1 ** ADAM_STEP)
    v_hat = v / (1.0 - ADAM_B2 ** ADAM_STEP)
    delta = -ADAM_LR * (m_hat / (jnp.sqrt(v_hat) + ADAM_EPS) + ADAM_WD * w)
    return delta, m, v


def _prev_halo_spec(c, tm):
    return pl.BlockSpec((SUBLANES, c), lambda i: (jnp.maximum(i * (tm // SUBLANES) - 1, 0), 0))


def _next_halo_spec(c, tm, s):
    return pl.BlockSpec((SUBLANES, c), lambda i: (jnp.minimum((i + 1) * (tm // SUBLANES), s // SUBLANES - 1), 0))


def _shift_down(p, halo, first_block, n):
    out = pltpu.roll(p, n, 0)
    row = lax.broadcasted_iota(jnp.int32, p.shape, 0)
    for j in range(n):
        top = jnp.where(first_block, 0.0, halo[SUBLANES - n + j:SUBLANES - n + j + 1, :])
        out = jnp.where(row == j, top, out)
    return out


def _shift_up(p, halo, last_block, n):
    rows = p.shape[0]
    out = pltpu.roll(p, rows - n, 0)
    row = lax.broadcasted_iota(jnp.int32, p.shape, 0)
    for j in range(n):
        bot = jnp.where(last_block, 0.0, halo[j:j + 1, :])
        out = jnp.where(row == rows - n + j, bot, out)
    return out


def token_shift_fwd(p, mu, tm):
    s, c = p.shape

    def body(p_ref, halo_ref, mu_ref, o_ref):
        pv = p_ref[...]
        prev = _shift_down(pv, halo_ref[...], pl.program_id(0) == 0, 1)
        o_ref[...] = pv + (prev - pv) * mu_ref[...]

    return pl.pallas_call(
        body, name="token_shift_fwd", grid=(s // tm,),
        in_specs=[pl.BlockSpec((tm, c), lambda i: (i, 0)), _prev_halo_spec(c, tm), pl.BlockSpec((1, c), lambda i: (0, 0))],
        out_specs=pl.BlockSpec((tm, c), lambda i: (i, 0)), out_shape=jax.ShapeDtypeStruct((s, c), F32),
        compiler_params=_params(("parallel",)),
    )(p, p, mu)


def token_shift_bwd(p, mu, ds, tm):
    s, c = p.shape
    nb = s // tm

    def body(p_ref, halo_ref, mu_ref, ds_ref, dsn_ref, dp_ref, dmu_ref):
        i = pl.program_id(0)
        pv, dsv, muv = p_ref[...], ds_ref[...], mu_ref[...]
        prev = _shift_down(pv, halo_ref[...], i == 0, 1)
        nxt = _shift_up(dsv, dsn_ref[...], i == nb - 1, 1)
        dp_ref[...] = dsv * (1.0 - muv) + nxt * muv
        dmu_ref[...] = jnp.sum(dsv * (prev - pv), axis=0, keepdims=True)

    return pl.pallas_call(
        body, name="token_shift_bwd", grid=(nb,),
        in_specs=[pl.BlockSpec((tm, c), lambda i: (i, 0)), _prev_halo_spec(c, tm), pl.BlockSpec((1, c), lambda i: (0, 0)),
                  pl.BlockSpec((tm, c), lambda i: (i, 0)), _next_halo_spec(c, tm, s)],
        out_specs=[pl.BlockSpec((tm, c), lambda i: (i, 0)), pl.BlockSpec((None, 1, c), lambda i: (i, 0, 0))],
        out_shape=[jax.ShapeDtypeStruct((s, c), F32), jax.ShapeDtypeStruct((nb, 1, c), F32)],
        compiler_params=_params(("parallel",)),
    )(p, p, mu, ds, ds)


def _ffn_tiles(s, f):
    return _pick(s, 256, SUBLANES), _pick(f, 1408, LANES)


def _conv_gate(gp, halo, first_block, cw, cb):
    p1 = _shift_down(gp, halo, first_block, 1)
    p2 = _shift_down(gp, halo, first_block, 2)
    return cw[0:1, :] * p2 + cw[1:2, :] * p1 + cw[2:3, :] * gp + cb, p1, p2


def ffn_act_fwd(gate_pre, up, conv_w, conv_b):
    s, f = gate_pre.shape
    tm, tc = _ffn_tiles(s, f)

    def body(gp_ref, halo_ref, up_ref, cw_ref, cb_ref, o_ref):
        gate, _, _ = _conv_gate(gp_ref[...], halo_ref[...], pl.program_id(0) == 0, cw_ref[...], cb_ref[...])
        o_ref[...] = (gate * jax.nn.sigmoid(gate) * up_ref[...]).astype(o_ref.dtype)

    blk = pl.BlockSpec((tm, tc), lambda i, j: (i, j))
    return pl.pallas_call(
        body, name="ffn_act_fwd", grid=(s // tm, f // tc),
        in_specs=[blk, pl.BlockSpec((SUBLANES, tc), lambda i, j: (jnp.maximum(i * (tm // SUBLANES) - 1, 0), j)), blk,
                  pl.BlockSpec((CONV_W, tc), lambda i, j: (0, j)), pl.BlockSpec((1, tc), lambda i, j: (0, j))],
        out_specs=blk, out_shape=jax.ShapeDtypeStruct((s, f), BF16),
        compiler_params=_params(("parallel", "parallel")),
    )(gate_pre, gate_pre, up, conv_w, conv_b)


def ffn_act_bwd1(gate_pre, up, conv_w, conv_b, d_act):
    s, f = gate_pre.shape
    tm, tc = _ffn_tiles(s, f)
    nb = s // tm

    def body(gp_ref, halo_ref, up_ref, cw_ref, cb_ref, da_ref, dg_ref, du_ref, dcw_ref, dcb_ref):
        gp = gp_ref[...]
        gate, p1, p2 = _conv_gate(gp, halo_ref[...], pl.program_id(0) == 0, cw_ref[...], cb_ref[...])
        sig = jax.nn.sigmoid(gate)
        da = da_ref[...].astype(F32)
        du_ref[...] = (da * gate * sig).astype(du_ref.dtype)
        dg = da * up_ref[...] * (sig * (1.0 + gate * (1.0 - sig)))
        dg_ref[...] = dg
        dcb_ref[...] = jnp.sum(dg, axis=0, keepdims=True)
        dcw_ref[0:1, :] = jnp.sum(dg * p2, axis=0, keepdims=True)
        dcw_ref[1:2, :] = jnp.sum(dg * p1, axis=0, keepdims=True)
        dcw_ref[2:3, :] = jnp.sum(dg * gp, axis=0, keepdims=True)

    blk = pl.BlockSpec((tm, tc), lambda i, j: (i, j))
    return pl.pallas_call(
        body, name="ffn_act_bwd1", grid=(nb, f // tc),
        in_specs=[blk, pl.BlockSpec((SUBLANES, tc), lambda i, j: (jnp.maximum(i * (tm // SUBLANES) - 1, 0), j)), blk,
                  pl.BlockSpec((CONV_W, tc), lambda i, j: (0, j)), pl.BlockSpec((1, tc), lambda i, j: (0, j)), blk],
        out_specs=[blk, blk, pl.BlockSpec((None, CONV_W, tc), lambda i, j: (i, 0, j)),
                   pl.BlockSpec((None, 1, tc), lambda i, j: (i, 0, j))],
        out_shape=[jax.ShapeDtypeStruct((s, f), F32), jax.ShapeDtypeStruct((s, f), BF16),
                   jax.ShapeDtypeStruct((nb, CONV_W, f), F32), jax.ShapeDtypeStruct((nb, 1, f), F32)],
        compiler_params=_params(("parallel", "parallel")),
    )(gate_pre, gate_pre, up, conv_w, conv_b, d_act)


def ffn_act_bwd2(d_gate, conv_w):
    s, f = d_gate.shape
    tm, tc = _ffn_tiles(s, f)
    nb = s // tm

    def body(dg_ref, halo_ref, cw_ref, o_ref):
        dg, cw = dg_ref[...], cw_ref[...]
        last = pl.program_id(0) == nb - 1
        n1 = _shift_up(dg, halo_ref[...], last, 1)
        n2 = _shift_up(dg, halo_ref[...], last, 2)
        o_ref[...] = (cw[2:3, :] * dg + cw[1:2, :] * n1 + cw[0:1, :] * n2).astype(o_ref.dtype)

    blk = pl.BlockSpec((tm, tc), lambda i, j: (i, j))
    return pl.pallas_call(
        body, name="ffn_act_bwd2", grid=(nb, f // tc),
        in_specs=[blk, pl.BlockSpec((SUBLANES, tc), lambda i, j: (jnp.minimum((i + 1) * (tm // SUBLANES), s // SUBLANES - 1), j)),
                  pl.BlockSpec((CONV_W, tc), lambda i, j: (0, j))],
        out_specs=blk, out_shape=jax.ShapeDtypeStruct((s, f), BF16),
        compiler_params=_params(("parallel", "parallel")),
    )(d_gate, d_gate, conv_w)


def _mxu(x, y, cx, cy):
    if x.ndim == 3:
        return lax.dot_general(x, y, (((cx + 1,), (cy + 1,)), ((0,), (0,))), preferred_element_type=F32)
    return lax.dot_general(x, y, (((cx,), (cy,)), ((), ())), preferred_element_type=F32)


def _split(x):
    hi = x.astype(BF16)
    return hi, (x - hi.astype(F32)).astype(BF16)


def _make_dot3(cx, cy, passes):
    @jax.custom_vjp
    def f(x, y):
        if passes == 1:
            return _mxu(x.astype(BF16), y.astype(BF16), cx, cy)
        xh, xl = _split(x)
        yh, yl = _split(y)
        return _mxu(xh, yh, cx, cy) + (_mxu(xh, yl, cx, cy) + _mxu(xl, yh, cx, cy))

    def fwd(x, y):
        return f(x, y), (x, y)

    def bwd(res, g):
        x, y = res
        dx = dot3(g, y, 1, 1 - cy, passes) if cx == 1 else dot3(y, g, 1 - cy, 1, passes)
        dy = dot3(x, g, 1 - cx, 0, passes) if cy == 0 else dot3(g, x, 0, 1 - cx, passes)
        return dx, dy

    f.defvjp(fwd, bwd)
    return f


_DOT3 = {}


def dot3(x, y, cx, cy, passes=3):
    if (cx, cy, passes) not in _DOT3:
        _DOT3[(cx, cy, passes)] = _make_dot3(cx, cy, passes)
    return _DOT3[(cx, cy, passes)](x, y)


def _dot(x, y, passes=3):
    return dot3(x, y, 1, 0, passes)


def _dot_nt(x, y, passes=3):
    return dot3(x, y, 1, 1, passes)


def _dot_tn(x, y, passes=3):
    return dot3(x, y, 0, 0, passes)


def _tri_sum(x, lower):
    t = x.shape[-2]
    row = lax.broadcasted_iota(jnp.int32, (t, t), 0)
    col = lax.broadcasted_iota(jnp.int32, (t, t), 1)
    tri = jnp.where((col <= row) if lower else (col >= row), 1.0, 0.0).astype(BF16)
    if x.ndim == 3:
        tri = jnp.broadcast_to(tri[None], (x.shape[0], t, t))
    hi = x.astype(BF16)
    rest = x - hi.astype(F32)
    mid = rest.astype(BF16)
    low = (rest - mid.astype(F32)).astype(BF16)
    return _mxu(tri, hi, 1, 0) + (_mxu(tri, mid, 1, 0) + _mxu(tri, low, 1, 0))


@jax.custom_vjp
def _cumsum_rows(x):
    return _tri_sum(x, True)


_cumsum_rows.defvjp(lambda x: (_tri_sum(x, True), None), lambda _, g: (_tri_sum(g, False),))


def _scan_chunk(s0, r, w, k, v, a, b):
    t = r.shape[1]
    row = lax.broadcasted_iota(jnp.int32, (1, t, t), 1)
    col = lax.broadcasted_iota(jnp.int32, (1, t, t), 2)
    strict, incl = col < row, col <= row
    logw = jnp.log(w)
    cum = _cumsum_rows(logw)
    w_in, w_ex, w_inv = jnp.exp(cum), jnp.exp(cum - logw), jnp.exp(-cum)
    w_all = jnp.exp(jnp.sum(logw, axis=1, keepdims=True))
    at, rt, kt, bt = a * w_ex, r * w_in, k * w_inv, b * w_inv
    ps, po = SCAN_PASSES_SOLVE, SCAN_PASSES_OUT
    a_ab = jnp.where(strict, _dot_nt(at, bt, ps), 0.0)
    a_ak = jnp.where(strict, _dot_nt(at, kt, ps), 0.0)
    a_rk = jnp.where(incl, _dot_nt(rt, kt, po), 0.0)
    a_rb = jnp.where(incl, _dot_nt(rt, bt, po), 0.0)
    u = _dot_nt(at, s0, ps) + _dot(a_ak, v, ps)
    p = a_ab
    steps = int(math.log2(t))
    assert 2 ** steps == t
    for j in range(steps):
        u = u + _dot(p, u, ps)
        if j < steps - 1:
            p = _dot(p, p, ps)
    y = _dot_nt(rt, s0, po) + _dot(a_rk, v, po) + _dot(a_rb, u, po)
    s_new = s0 * w_all + _dot_tn(v, kt * w_all, po) + _dot_tn(u, bt * w_all, po)
    return y, s_new


def scan_fwd(r, w, k, v, a, b, comm=None):
    h, s, n = r.shape
    t = min(SCAN_CHUNK, s)
    nc = s // t

    hb = SCAN_HEADS if h % SCAN_HEADS == 0 else 1

    def body(r_ref, w_ref, k_ref, v_ref, a_ref, b_ref, y_ref, ck_ref, st_ref):
        @pl.when(pl.program_id(1) == 0)
        def _():
            st_ref[...] = jnp.zeros_like(st_ref)

        s0 = st_ref[...]
        ck_ref[...] = s0
        y, s_new = _scan_chunk(s0, r_ref[...], w_ref[...], k_ref[...], v_ref[...], a_ref[...], b_ref[...])
        y_ref[...] = y
        st_ref[...] = s_new

    blk = pl.BlockSpec((hb, t, n), lambda hh, c: (hh, c, 0))
    return _pallas(
        body, name="rwkv_scan_fwd", grid=(h // hb, nc), in_specs=[blk] * 6,
        out_specs=[blk, pl.BlockSpec((hb, None, n, n), lambda hh, c: (hh, c, 0, 0))],
        out_shape=[jax.ShapeDtypeStruct((h, s, n), F32), jax.ShapeDtypeStruct((h, nc, n, n), F32)],
        scratch_shapes=[pltpu.VMEM((hb, n, n), F32)], sem=("parallel", "arbitrary"), comm=comm,
    )(r, w, k, v, a, b)


def scan_bwd(r, w, k, v, a, b, ck, dy, comm=None):
    h, s, n = r.shape
    t = min(SCAN_CHUNK, s)
    nc = s // t

    hb = SCAN_HEADS if h % SCAN_HEADS == 0 else 1

    def body(r_ref, w_ref, k_ref, v_ref, a_ref, b_ref, ck_ref, dy_ref, dr_ref, dw_ref, dk_ref, dv_ref, da_ref, db_ref, ds_ref):
        @pl.when(pl.program_id(1) == 0)
        def _():
            ds_ref[...] = jnp.zeros_like(ds_ref)

        _, vjp_fn = jax.vjp(_scan_chunk, ck_ref[...], r_ref[...], w_ref[...], k_ref[...], v_ref[...], a_ref[...], b_ref[...])
        ds0, dr, dw, dk, dv, da, db = vjp_fn((dy_ref[...], ds_ref[...]))
        ds_ref[...] = ds0
        dr_ref[...], dw_ref[...], dk_ref[...], dv_ref[...], da_ref[...], db_ref[...] = dr, dw, dk, dv, da, db

    blk = pl.BlockSpec((hb, t, n), lambda hh, c: (hh, nc - 1 - c, 0))
    return _pallas(
        body, name="rwkv_scan_bwd", grid=(h // hb, nc),
        in_specs=[blk] * 6 + [pl.BlockSpec((hb, None, n, n), lambda hh, c: (hh, nc - 1 - c, 0, 0)), blk],
        out_specs=[blk] * 6, out_shape=[jax.ShapeDtypeStruct((h, s, n), F32)] * 6,
        scratch_shapes=[pltpu.VMEM((hb, n, n), F32)], sem=("parallel", "arbitrary"), comm=comm,
    )(r, w, k, v, a, b, ck, dy)


ATTN_BLOCK = 256
ATTN_LEVELS = 4
ATTN_SLAB = 2 * LANES


def _attn_specs(h, s, tq):
    qblk = lambda c, part: pl.BlockSpec((None, tq, c), lambda hh, i: (hh, i, part))
    kblk = lambda part: pl.BlockSpec((None, s, LANES), lambda hh, i: (hh, 0, part))
    row64 = pl.BlockSpec((tq, QK_ROPE), lambda hh, i: (i, 0))
    return [qblk(LANES, 0), qblk(LANES, 1), kblk(0), kblk(1), pl.BlockSpec((s, QK_ROPE), lambda hh, i: (0, 0)),
            row64, row64, pl.BlockSpec((QK_ROPE, QK_ROPE), lambda hh, i: (0, 0))]


def _attn_levels(s, tq):
    nq = s // tq
    n_lev = min(ATTN_LEVELS, nq)
    per = nq // n_lev
    return [(lv * per, (lv + 1) * per, (lv + 1) * per * tq) for lv in range(n_lev)]


def _attn_scores(qn_b, qp_b, kn_ref, kp_ref, klen, i, tq):
    scale = (QK_NOPE + QK_ROPE) ** -0.5
    kn_b = kn_ref[0:klen, :].astype(BF16)
    kp_b = kp_ref[0:klen, :].astype(BF16)
    sc = lax.dot_general(qn_b, kn_b, (((1,), (1,)), ((), ())), preferred_element_type=F32)
    sc = sc + lax.dot_general(qp_b, kp_b, (((1,), (1,)), ((), ())), preferred_element_type=F32)
    row = i * tq + lax.broadcasted_iota(jnp.int32, sc.shape, 0)
    col = lax.broadcasted_iota(jnp.int32, sc.shape, 1)
    return jnp.where(row >= col, sc * scale, NEG_INF), scale, kn_b, kp_b


def attn_fwd(q_h, kv_h, kp, cos2, sin2, rot, comm=None):
    h, s, _ = q_h.shape
    tq = _pick(s, ATTN_BLOCK, SUBLANES)

    def body(qn_ref, qp_ref, kn_ref, v_ref, kp_ref, cos_ref, sin_ref, rot_ref, o_ref, lse_ref):
        i = pl.program_id(1)
        qn_b = qn_ref[...].astype(BF16)
        qp_b = _rope(qp_ref[:, :QK_ROPE], cos_ref[...], sin_ref[...], rot_ref[...]).astype(BF16)

        def level(klen):
            sc, _, _, _ = _attn_scores(qn_b, qp_b, kn_ref, kp_ref, klen, i, tq)
            mx = jnp.max(sc, axis=-1, keepdims=True)
            e = jnp.exp(sc - mx)
            den = jnp.sum(e, axis=-1, keepdims=True)
            o_ref[...] = jnp.dot((e / den).astype(BF16), v_ref[0:klen, :].astype(BF16), preferred_element_type=F32)
            lse_ref[...] = mx + jnp.log(den)

        for lo, hi, klen in _attn_levels(s, tq):
            pl.when((i >= lo) & (i < hi))(functools.partial(level, klen))

    oblk = lambda c: pl.BlockSpec((None, tq, c), lambda hh, i: (hh, i, 0))
    return _pallas(
        body, name="mla_attn_fwd", grid=(h, s // tq), in_specs=_attn_specs(h, s, tq),
        out_specs=[oblk(V_HEAD), oblk(1)],
        out_shape=[jax.ShapeDtypeStruct((h, s, V_HEAD), F32), jax.ShapeDtypeStruct((h, s, 1), F32)],
        sem=("parallel", "parallel"), comm=comm,
    )(q_h, q_h, kv_h, kv_h, kp, cos2, sin2, rot)


def attn_bwd(q_h, kv_h, kp, cos2, sin2, rot, o, lse, do, comm=None):
    h, s, _ = q_h.shape
    tq = _pick(s, ATTN_BLOCK, SUBLANES)
    nq = s // tq

    def body(qn_ref, qp_ref, kn_ref, v_ref, kp_ref, cos_ref, sin_ref, rot_ref, o_ref, lse_ref, do_ref,
             dq_ref, dkv_ref, dkp_ref, dkv_acc, dkp_acc):
        i = pl.program_id(1)

        @pl.when(i == 0)
        def _():
            dkv_acc[...] = jnp.zeros_like(dkv_acc)
            dkp_acc[...] = jnp.zeros_like(dkp_acc)

        cosv, sinv, rotv = cos_ref[...], sin_ref[...], rot_ref[...]
        qn_b = qn_ref[...].astype(BF16)
        qp_b = _rope(qp_ref[:, :QK_ROPE], cosv, sinv, rotv).astype(BF16)
        dov = do_ref[...]
        do_b = dov.astype(BF16)
        delta = jnp.sum(dov * o_ref[...], axis=-1, keepdims=True)
        lsev = lse_ref[...]

        def level(klen):
            sc, scale, kn_b, kp_b = _attn_scores(qn_b, qp_b, kn_ref, kp_ref, klen, i, tq)
            p = jnp.exp(sc - lsev)
            dp = lax.dot_general(do_b, v_ref[0:klen, :].astype(BF16), (((1,), (1,)), ((), ())), preferred_element_type=F32)
            ds = (p * (dp - delta) * scale).astype(BF16)
            dkv_acc[0:klen, :LANES] += lax.dot_general(ds, qn_b, (((0,), (0,)), ((), ())), preferred_element_type=F32)
            dkv_acc[0:klen, LANES:] += lax.dot_general(p.astype(BF16), do_b, (((0,), (0,)), ((), ())), preferred_element_type=F32)
            dkp_acc[0:klen, :] += lax.dot_general(ds, qp_b, (((0,), (0,)), ((), ())), preferred_element_type=F32)
            dqp = jnp.dot(ds, kp_b, preferred_element_type=F32)
            dqp_raw = dqp * cosv + lax.dot_general(dqp * sinv, rotv, (((1,), (1,)), ((), ())), precision=HIGHEST,
                                                   preferred_element_type=F32)
            dq_ref[:, :QK_NOPE] = jnp.dot(ds, kn_b, preferred_element_type=F32).astype(dq_ref.dtype)
            dq_ref[:, QK_NOPE:QK_NOPE + QK_ROPE] = dqp_raw.astype(dq_ref.dtype)
            dq_ref[:, QK_NOPE + QK_ROPE:] = jnp.zeros((tq, ATTN_SLAB - QK_NOPE - QK_ROPE), dq_ref.dtype)

        for lo, hi, klen in _attn_levels(s, tq):
            pl.when((i >= lo) & (i < hi))(functools.partial(level, klen))

        @pl.when(i == nq - 1)
        def _():
            dkv_ref[...] = dkv_acc[...].astype(dkv_ref.dtype)
            dkp_ref[...] = dkp_acc[...]

    rblk = lambda c: pl.BlockSpec((None, tq, c), lambda hh, i: (hh, i, 0))
    sblk = lambda c: pl.BlockSpec((None, s, c), lambda hh, i: (hh, 0, 0))
    return _pallas(
        body, name="mla_attn_bwd", grid=(h, nq),
        in_specs=_attn_specs(h, s, tq) + [rblk(V_HEAD), rblk(1), rblk(V_HEAD)],
        out_specs=[rblk(ATTN_SLAB), sblk(ATTN_SLAB), sblk(QK_ROPE)],
        out_shape=[jax.ShapeDtypeStruct((h, s, ATTN_SLAB), BF16), jax.ShapeDtypeStruct((h, s, ATTN_SLAB), BF16),
                   jax.ShapeDtypeStruct((h, s, QK_ROPE), F32)],
        scratch_shapes=[pltpu.VMEM((s, ATTN_SLAB), F32), pltpu.VMEM((s, QK_ROPE), F32)],
        sem=("parallel", "arbitrary"), comm=comm,
    )(q_h, q_h, kv_h, kv_h, kp, cos2, sin2, rot, o, lse, do)


def _my_pos():
    return lax.axis_index("x"), lax.axis_index("y"), lax.axis_index("c")


def _dev_index(px, py, pc):
    return 4 * px + 2 * py + pc


def all_gather(name, shard):
    r, c = shard.shape

    def body(x_ref, out_ref, send_sems, recv_sems, local_sem):
        x, y, cc = _my_pos()
        me, sibling = (x, y, cc), (x, y, 1 - cc)
        chips = [(1 - x, y), (x, 1 - y), (1 - x, 1 - y)]

        def rows(px, py, pc):
            return out_ref.at[_dev_index(px, py, pc)]

        def copy(kk, block, to, src=None):
            return pltpu.make_async_remote_copy(
                src_ref=rows(*block) if src is None else src, dst_ref=rows(*block),
                send_sem=send_sems.at[kk], recv_sem=recv_sems.at[kk], device_id=to, device_id_type=MESH)

        mine = pltpu.make_async_copy(x_ref, rows(*me), local_sem)
        mine.start()
        first = [copy(0, me, sibling, src=x_ref)]
        first += [copy(1 + j, me, (*chip, cc), src=x_ref) for j, chip in enumerate(chips)]
        for cp in first:
            cp.start()
        passed = [copy(4 + j, (*chip, cc), sibling) for j, chip in enumerate(chips)]
        for j, chip in enumerate(chips):
            copy(1 + j, (*chip, cc), me).wait_recv()
            passed[j].start()
        copy(0, sibling, me).wait_recv()
        for j, chip in enumerate(chips):
            copy(4 + j, (*chip, 1 - cc), me).wait_recv()
        for cp in first + passed:
            cp.wait_send()
        mine.wait()

    return pl.pallas_call(
        body, name=name, out_shape=jax.ShapeDtypeStruct((N_DEV, r, c), shard.dtype),
        in_specs=[pl.BlockSpec(memory_space=pl.ANY)], out_specs=pl.BlockSpec(memory_space=pl.ANY),
        scratch_shapes=[pltpu.SemaphoreType.DMA((7,)), pltpu.SemaphoreType.DMA((7,)), pltpu.SemaphoreType.DMA],
    )(shard)


def exchange_sibling(name, g):
    _, r, c = g.shape

    def body(g_ref, out_ref, send_sems, recv_sems):
        x, y, cc = _my_pos()
        copies = []
        for px in range(2):
            for py in range(2):
                slot = 2 * px + py
                copies.append(pltpu.make_async_remote_copy(
                    src_ref=g_ref.at[_dev_index(px, py, 1 - cc)], dst_ref=out_ref.at[slot],
                    send_sem=send_sems.at[slot], recv_sem=recv_sems.at[slot], device_id=(x, y, 1 - cc), device_id_type=MESH))
        for cp in copies:
            cp.start()
        for cp in copies:
            cp.wait()

    return pl.pallas_call(
        body, name=name, out_shape=jax.ShapeDtypeStruct((4, r, c), g.dtype),
        in_specs=[pl.BlockSpec(memory_space=pl.ANY)], out_specs=pl.BlockSpec(memory_space=pl.ANY),
        scratch_shapes=[pltpu.SemaphoreType.DMA((4,)), pltpu.SemaphoreType.DMA((4,))],
    )(g)


def exchange_chips(name, hsum):
    _, r, c = hsum.shape

    def body(h_ref, out_ref, send_sems, recv_sems):
        x, y, cc = _my_pos()
        copies = []
        for j, (px, py) in enumerate([(1 - x, y), (x, 1 - y), (1 - x, 1 - y)]):
            copies.append(pltpu.make_async_remote_copy(
                src_ref=h_ref.at[2 * px + py], dst_ref=out_ref.at[j],
                send_sem=send_sems.at[j], recv_sem=recv_sems.at[j], device_id=(px, py, cc), device_id_type=MESH))
        for cp in copies:
            cp.start()
        for cp in copies:
            cp.wait()

    return pl.pallas_call(
        body, name=name, out_shape=jax.ShapeDtypeStruct((3, r, c), hsum.dtype),
        in_specs=[pl.BlockSpec(memory_space=pl.ANY)], out_specs=pl.BlockSpec(memory_space=pl.ANY),
        scratch_shapes=[pltpu.SemaphoreType.DMA((3,)), pltpu.SemaphoreType.DMA((3,))],
    )(hsum)


def _rs_add_sibling(name, g, from_sibling, cc):
    _, r, c = g.shape
    tr = _pick(r, 512, SUBLANES * 2)

    def body(cc_ref, g_ref, s_ref, o_ref, ob_ref):
        tot = g_ref[...] + s_ref[...]
        o_ref[...] = tot
        ob_ref[...] = tot.astype(BF16)

    blk = pl.BlockSpec((None, tr, c), lambda s_, i, cc_ref: (s_, i, 0))
    return pl.pallas_call(
        body, name=name,
        grid_spec=pltpu.PrefetchScalarGridSpec(
            num_scalar_prefetch=1, grid=(4, r // tr),
            in_specs=[pl.BlockSpec((None, None, tr, c), lambda s_, i, cc_ref: (s_, cc_ref[0], i, 0)), blk], out_specs=[blk, blk]),
        out_shape=[jax.ShapeDtypeStruct((4, r, c), F32), jax.ShapeDtypeStruct((4, r, c), BF16)],
        compiler_params=_params(("parallel", "parallel")),
    )(cc.reshape(1).astype(jnp.int32), g.reshape(4, 2, r, c), from_sibling)


def _rs_add_chips(name, chip_sum, from_chips, slot):
    _, r, c = chip_sum.shape
    tr = _pick(r, 512, SUBLANES * 2)

    def body(slot_ref, h_ref, f0_ref, f1_ref, f2_ref, o_ref):
        o_ref[...] = ((h_ref[...] + f0_ref[...].astype(F32)) + f1_ref[...].astype(F32)) + f2_ref[...].astype(F32)

    def from_blk(j):
        return pl.BlockSpec((None, tr, c), lambda i, slot_ref: (j, i, 0))

    return pl.pallas_call(
        body, name=name,
        grid_spec=pltpu.PrefetchScalarGridSpec(
            num_scalar_prefetch=1, grid=(r // tr,),
            in_specs=[pl.BlockSpec((None, tr, c), lambda i, slot_ref: (slot_ref[0], i, 0)), from_blk(0), from_blk(1), from_blk(2)],
            out_specs=pl.BlockSpec((tr, c), lambda i, slot_ref: (i, 0))),
        out_shape=jax.ShapeDtypeStruct((r, c), F32), compiler_params=_params(("parallel",)),
    )(slot.reshape(1).astype(jnp.int32), chip_sum, from_chips, from_chips, from_chips)


def adamw_scatter(name, w, m, v, chip_sum, from_chips, slot):
    _, a, b = w.shape
    _, r, c = chip_sum.shape
    assert a <= r and b <= c, (name, w.shape, chip_sum.shape)
    tr = _pick(a, 256, SUBLANES * 2)

    def body(slot_ref, h_ref, f0_ref, f1_ref, f2_ref, w_ref, m_ref, v_ref, g_out, d_out, m_out, v_out):
        g = ((h_ref[...] + f0_ref[...].astype(F32)) + f1_ref[...].astype(F32)) + f2_ref[...].astype(F32)
        g = g[:, :b]
        g_out[...] = g
        d_out[...], m_out[...], v_out[...] = _adamw_fn(w_ref[...], g, m_ref[...], v_ref[...])

    def from_blk(j):
        return pl.BlockSpec((None, tr, c), lambda i, slot_ref: (j, i, 0))

    mine = pl.BlockSpec((None, tr, b), lambda i, slot_ref: (0, i, 0))
    return pl.pallas_call(
        body, name=name,
        grid_spec=pltpu.PrefetchScalarGridSpec(
            num_scalar_prefetch=1, grid=(a // tr,),
            in_specs=[pl.BlockSpec((None, tr, c), lambda i, slot_ref: (slot_ref[0], i, 0)), from_blk(0), from_blk(1), from_blk(2),
                      mine, mine, mine],
            out_specs=[mine] * 4),
        out_shape=[jax.ShapeDtypeStruct((1, a, b), F32)] * 4, compiler_params=_params(("parallel",)),
    )(slot.reshape(1).astype(jnp.int32), chip_sum, from_chips, from_chips, from_chips, w, m, v)


def rs_chip_sum(tag, g):
    _, _, cc = _my_pos()
    from_sibling = exchange_sibling("rs_sibling_" + tag, g)
    return _rs_add_sibling("rs_add_sibling_" + tag, g, from_sibling, cc)


def rs_finish(tag, chip_sum, from_chips):
    x, y, _ = _my_pos()
    return _rs_add_chips("rs_add_chips_" + tag, chip_sum, from_chips, 2 * x + y)


def reduce_scatter(tag, g):
    chip_sum, chip_sum_b = rs_chip_sum(tag, g)
    return rs_finish(tag, chip_sum, exchange_chips("rs_chips_" + tag, chip_sum_b))


class GatherIci:
    def __init__(self, shards):
        self.inputs = list(shards)
        self.out_shapes = [jax.ShapeDtypeStruct((N_DEV,) + s.shape, s.dtype) for s in shards]
        self.n_remote, self.n_local = 3 * len(shards), len(shards)

    def make(self, cins, couts, send, recv, local):
        x, y, cc = _my_pos()
        me = _dev_index(x, y, cc)
        copies = []
        for w, (src, out) in enumerate(zip(cins, couts, strict=True)):
            copies.append(pltpu.make_async_copy(src, out.at[me], local.at[w]))
            for j, (px, py) in enumerate([(1 - x, y), (x, 1 - y), (1 - x, 1 - y)]):
                copies.append(pltpu.make_async_remote_copy(
                    src_ref=src, dst_ref=out.at[me], send_sem=send.at[3 * w + j], recv_sem=recv.at[3 * w + j],
                    device_id=(px, py, cc), device_id_type=MESH))
        return copies


class RsChips:
    def __init__(self, chip_sums):
        self.inputs = list(chip_sums)
        self.out_shapes = [jax.ShapeDtypeStruct((3,) + h.shape[1:], h.dtype) for h in chip_sums]
        self.n_remote, self.n_local = 3 * len(chip_sums), 0

    def make(self, cins, couts, send, recv, local):
        x, y, cc = _my_pos()
        copies = []
        for w, (h_ref, out) in enumerate(zip(cins, couts, strict=True)):
            for j, (px, py) in enumerate([(1 - x, y), (x, 1 - y), (1 - x, 1 - y)]):
                copies.append(pltpu.make_async_remote_copy(
                    src_ref=h_ref.at[2 * px + py], dst_ref=out.at[j], send_sem=send.at[3 * w + j], recv_sem=recv.at[3 * w + j],
                    device_id=(px, py, cc), device_id_type=MESH))
        return copies


class SiblingSwap:
    def __init__(self, gs):
        self.inputs = list(gs)
        self.out_shapes = [jax.ShapeDtypeStruct((4,) + g.shape[1:], g.dtype) for g in gs]
        self.n_remote, self.n_local = 4 * len(gs), 0

    def make(self, cins, couts, send, recv, local):
        x, y, cc = _my_pos()
        copies = []
        for w, (g_ref, out) in enumerate(zip(cins, couts, strict=True)):
            for px in range(2):
                for py in range(2):
                    q = 4 * w + 2 * px + py
                    copies.append(pltpu.make_async_remote_copy(
                        src_ref=g_ref.at[_dev_index(px, py, 1 - cc)], dst_ref=out.at[2 * px + py],
                        send_sem=send.at[q], recv_sem=recv.at[q], device_id=(x, y, 1 - cc), device_id_type=MESH))
        return copies


class GatherD2D:
    def __init__(self, arrays):
        self.inputs = list(arrays)
        self.out_shapes = [jax.ShapeDtypeStruct(a.shape, a.dtype) for a in arrays]
        self.n_remote, self.n_local = 4 * len(arrays), 0
        self.aliases = [(i, i) for i in range(len(arrays))]

    def make(self, cins, couts, send, recv, local):
        x, y, cc = _my_pos()
        copies = []
        for w, out in enumerate(couts):
            for px in range(2):
                for py in range(2):
                    q = 4 * w + 2 * px + py
                    slab = out.at[_dev_index(px, py, cc)]
                    copies.append(pltpu.make_async_remote_copy(
                        src_ref=slab, dst_ref=slab, send_sem=send.at[q], recv_sem=recv.at[q],
                        device_id=(x, y, 1 - cc), device_id_type=MESH))
        return copies


class _SemSlice:
    def __init__(self, base, start):
        self.base, self.start = base, start

    @property
    def at(self):
        return self

    def __getitem__(self, k):
        return self.base.at[self.start + k]


class CommGroup:
    def __init__(self, plans):
        self.plans = [p for p in plans if p.inputs]
        self.inputs = [a for p in self.plans for a in p.inputs]
        self.out_shapes = [s_ for p in self.plans for s_ in p.out_shapes]
        self.n_remote = sum(p.n_remote for p in self.plans)
        self.n_local = sum(p.n_local for p in self.plans)
        self.aliases, i0, o0 = [], 0, 0
        for p in self.plans:
            self.aliases += [(i0 + i, o0 + j) for i, j in getattr(p, 'aliases', [])]
            i0, o0 = i0 + len(p.inputs), o0 + len(p.out_shapes)

    def make(self, cins, couts, send, recv, local):
        copies, i0, o0, r0, l0 = [], 0, 0, 0, 0
        for p in self.plans:
            ni, no = len(p.inputs), len(p.out_shapes)
            copies += p.make(cins[i0:i0 + ni], couts[o0:o0 + no], _SemSlice(send, r0), _SemSlice(recv, r0), _SemSlice(local, l0))
            i0, o0, r0, l0 = i0 + ni, o0 + no, r0 + p.n_remote, l0 + p.n_local
        return copies


def gather_d2d(name, arrays):
    n = len(arrays)

    def body(*refs):
        outs, send, recv = refs[n:2 * n], refs[2 * n], refs[2 * n + 1]
        x, y, cc = _my_pos()
        copies = []
        for w, out in enumerate(outs):
            for px in range(2):
                for py in range(2):
                    q = 4 * w + 2 * px + py
                    slab = out.at[_dev_index(px, py, cc)]
                    copies.append(pltpu.make_async_remote_copy(
                        src_ref=slab, dst_ref=slab, send_sem=send.at[q], recv_sem=recv.at[q],
                        device_id=(x, y, 1 - cc), device_id_type=MESH))
        for cp in copies:
            cp.start()
        for cp in copies:
            cp.wait()

    any_spec = pl.BlockSpec(memory_space=pl.ANY)
    return pl.pallas_call(
        body, name=name, out_shape=[jax.ShapeDtypeStruct(a.shape, a.dtype) for a in arrays],
        in_specs=[any_spec] * n, out_specs=[any_spec] * n, input_output_aliases={i: i for i in range(n)},
        scratch_shapes=[pltpu.SemaphoreType.DMA((4 * n,)), pltpu.SemaphoreType.DMA((4 * n,))],
    )(*arrays)


def _pallas(body, *, name, grid, in_specs, out_specs, out_shape, scratch_shapes=(), sem, comm=None):
    in_specs, out_specs, out_shape, scratch_shapes = list(in_specs), list(out_specs), list(out_shape), list(scratch_shapes)
    if comm is None:
        return pl.pallas_call(body, name=name, grid=grid, in_specs=in_specs, out_specs=out_specs, out_shape=out_shape,
                              scratch_shapes=scratch_shapes, compiler_params=_params(sem))
    n_in, n_out, n_scr = len(in_specs), len(out_specs), len(scratch_shapes)
    nci, nco = len(comm.inputs), len(comm.out_shapes)

    def body2(*refs):
        ins, cins = refs[:n_in], refs[n_in:n_in + nci]
        o0 = n_in + nci
        outs, couts = refs[o0:o0 + n_out], refs[o0 + n_out:o0 + n_out + nco]
        s0 = o0 + n_out + nco
        scr = refs[s0:s0 + n_scr]
        send, recv, local = refs[s0 + n_scr:]
        pids = [pl.program_id(k) for k in range(len(grid))]
        first = functools.reduce(jnp.logical_and, [p == 0 for p in pids])
        last = functools.reduce(jnp.logical_and, [p == g - 1 for p, g in zip(pids, grid)])

        @pl.when(first)
        def _():
            for cp in comm.make(cins, couts, send, recv, local):
                cp.start()

        body(*ins, *outs, *scr)

        @pl.when(last)
        def _():
            for cp in comm.make(cins, couts, send, recv, local):
                cp.wait()

    any_spec = pl.BlockSpec(memory_space=pl.ANY)
    call = pl.pallas_call(
        body2, name=name, grid=grid, in_specs=in_specs + [any_spec] * nci, out_specs=out_specs + [any_spec] * nco,
        out_shape=out_shape + list(comm.out_shapes),
        input_output_aliases={n_in + i: n_out + j for i, j in getattr(comm, 'aliases', [])},
        scratch_shapes=scratch_shapes + [pltpu.SemaphoreType.DMA((comm.n_remote,)), pltpu.SemaphoreType.DMA((comm.n_remote,)),
                                         pltpu.SemaphoreType.DMA((max(comm.n_local, 1),))],
        compiler_params=_params(tuple("arbitrary" for _ in grid)))
    return lambda *args: call(*args, *comm.inputs)


PACK_W = 1024


class Pack:
    def __init__(self, entries, row_unit):
        self.entries = entries
        self.sizes = [int(np.prod(sh)) for _, sh in entries]
        self.offsets = np.concatenate([[0], np.cumsum(self.sizes)]).tolist()
        self.total = _round_up(self.offsets[-1], PACK_W * row_unit)
        self.rows = self.total // PACK_W

    def pack(self, arrays, dtype, lead=()):
        flat = [arrays[n].astype(dtype).reshape(lead + (-1,)) for n, _ in self.entries]
        pad = self.total - self.offsets[-1]
        if pad:
            flat.append(jnp.zeros(lead + (pad,), dtype))
        return jnp.concatenate(flat, axis=-1).reshape(lead + (self.rows, PACK_W))

    def unpack(self, buf, lead=()):
        flat = buf.reshape(lead + (self.total,))
        out = {}
        for (n, sh), off, sz in zip(self.entries, self.offsets, self.sizes):
            out[n] = lax.slice_in_dim(flat, off, off + sz, axis=len(lead)).reshape(lead + tuple(sh))
        return out


def _gathered_to_full(g, how):
    _, a, b = g.shape
    if how == 'row':
        return g.reshape(N_DEV * a, b)
    return jnp.transpose(g, (1, 0, 2)).reshape(a, N_DEV * b)


def _full_to_shards(w, how):
    a, b = w.shape
    if how == 'row':
        return w.reshape(N_DEV, a // N_DEV, b)
    return jnp.transpose(w.reshape(a, N_DEV, b // N_DEV), (1, 0, 2))


def _to_heads(t, width):
    s, c = t.shape
    return jnp.transpose(t.reshape(s, c // width, width), (1, 0, 2))


def _from_heads(t):
    h, s, w = t.shape
    return jnp.transpose(t, (1, 0, 2)).reshape(s, h * w)


def _rot_matrix():
    half = QK_ROPE // 2
    rot = np.zeros((QK_ROPE, QK_ROPE), np.float32)
    for i in range(half):
        rot[i + half, i] = -1.0
        rot[i, i + half] = 1.0
    return jnp.asarray(rot)


def _inv_freq2():
    half = QK_ROPE // 2
    inv = ROPE_THETA ** (-np.arange(half, dtype=np.float32) / half)
    return jnp.asarray(np.concatenate([inv, inv])[None, :].astype(np.float32))


def kernel(x, positions, attn_norm_g, w_in, rwkv_mu, rwkv_w0, rwkv_w2, rwkv_a0, rwkv_a2, rwkv_g2, rwkv_k_k, rwkv_k_a, rwkv_r_k, rwkv_gn_w, rwkv_gn_b, mla_q_norm_g, mla_w_uq, mla_kv_norm_g, mla_w_ukv, w_out, ffn_norm_g, ffn_w_gate, ffn_w_up, ffn_conv_w, ffn_conv_b, ffn_w_down, final_norm_g, loss_target, m_attn_norm_g, m_w_in, m_rwkv_mu, m_rwkv_w0, m_rwkv_w2, m_rwkv_a0, m_rwkv_a2, m_rwkv_g2, m_rwkv_k_k, m_rwkv_k_a, m_rwkv_r_k, m_rwkv_gn_w, m_rwkv_gn_b, m_mla_q_norm_g, m_mla_w_uq, m_mla_kv_norm_g, m_mla_w_ukv, m_w_out, m_ffn_norm_g, m_ffn_w_gate, m_ffn_w_up, m_ffn_conv_w, m_ffn_conv_b, m_ffn_w_down, m_final_norm_g, v_attn_norm_g, v_w_in, v_rwkv_mu, v_rwkv_w0, v_rwkv_w2, v_rwkv_a0, v_rwkv_a2, v_rwkv_g2, v_rwkv_k_k, v_rwkv_k_a, v_rwkv_r_k, v_rwkv_gn_w, v_rwkv_gn_b, v_mla_q_norm_g, v_mla_w_uq, v_mla_kv_norm_g, v_mla_w_ukv, v_w_out, v_ffn_norm_g, v_ffn_w_gate, v_ffn_w_up, v_ffn_conv_w, v_ffn_conv_b, v_ffn_w_down, v_final_norm_g):
    given = dict(locals())
    wts = {n: given[n] for n in WEIGHTS}
    mom_m = {n: given["m_" + n] for n in WEIGHTS}
    mom_v = {n: given["v_" + n] for n in WEIGHTS}
    out_shapes = {n: wts[n].shape for n in WEIGHTS}

    def local2d(n, a):
        if n == 'rwkv_r_k' or a.ndim <= 2:
            return a.reshape(1, -1)
        return a.reshape(a.shape[1:])

    w2d = {n: local2d(n, wts[n]) for n in WEIGHTS}
    m2d = {n: local2d(n, mom_m[n]) for n in WEIGHTS}
    v2d = {n: local2d(n, mom_v[n]) for n in WEIGHTS}

    xs = x.reshape(x.shape[1:])
    tgt = loss_target.reshape(loss_target.shape[1:])
    s, d = xs.shape
    c_rwkv = w2d['rwkv_w0'].shape[1]
    n_rh = c_rwkv // RWKV_HEAD
    decay_lora, aaa_lora, gate_lora = w2d['rwkv_w2'].shape[0], w2d['rwkv_a2'].shape[0], w2d['rwkv_g2'].shape[0]
    q_lora, kv_lora = w2d['mla_q_norm_g'].shape[1], w2d['mla_kv_norm_g'].shape[1]
    shift_dim = w2d['rwkv_mu'].shape[1]
    d_in = w2d['w_in'].shape[1] * N_DEV
    n_mh = w2d['mla_w_uq'].shape[1] * N_DEV // (QK_NOPE + QK_ROPE)
    tm = _pick(s, 256, SUBLANES)
    tm_wide = _pick(s, 128, SUBLANES)
    tm_heads = _pick(s, 512, SUBLANES)

    nb = {n: w2d[n].shape[1] for n in BIG if BIG[n] == 'col'}
    nbp = {n: _round_up(v_, LANES) for n, v_ in nb.items()}
    shards = {}
    for n in BIG:
        w = w2d[n].astype(BF16)
        if BIG[n] == 'col':
            w = jnp.pad(w, ((0, 0), (0, nbp[n] - nb[n])))
        elif n == 'ffn_w_down':
            w = jnp.pad(w, ((0, nbp['ffn_w_gate'] - w.shape[0]), (0, 0)))
        shards[n] = w

    def as_used(n, g):
        return g if BIG[n] == 'col' else g.reshape(N_DEV * g.shape[1], g.shape[2])

    gathered = {'w_in': as_used('w_in', all_gather("gather_w_in", shards['w_in']))}
    f_pad = N_DEV * nbp['ffn_w_gate']
    small_pack = Pack([(n, w2d[n].shape) for n in SMALL_SHARDED], 8)
    small_all = all_gather("gather_small", small_pack.pack(w2d, F32))
    full = {}
    for n, g in small_pack.unpack(small_all, lead=(N_DEV,)).items():
        full[n] = _gathered_to_full(g, SMALL_SHARDED[n])
    conv_w_pad = pad_cols(full['ffn_conv_w'], nb['ffn_w_gate'], nbp['ffn_w_gate'])
    conv_b_pad = pad_cols(w2d['ffn_conv_b'], nb['ffn_w_gate'], nbp['ffn_w_gate'])

    (h1,) = rowwise("rms_attn", _rms_fn, [xs, w2d['attn_norm_g']], ['row', 'const'], [('row', d, BF16)], heads=1, s=s, tm=tm)
    landed = {}

    def gather_behind(run, ici=(), d2d=()):
        *res, = run(CommGroup([GatherIci([shards[n] for n in ici]), GatherD2D([landed[n] for n in d2d])]))
        n_own = len(res) - len(ici) - len(d2d)
        landed.update(zip(ici, res[n_own:n_own + len(ici)], strict=True))
        for n, g in zip(d2d, res[n_own + len(ici):], strict=True):
            gathered[n] = as_used(n, g)
        return res[:n_own]

    w_in_nat = jnp.transpose(gathered['w_in'], (1, 0, 2))[:, :, :nb['w_in']].reshape(d, d_in)
    w_in_nat = jnp.pad(w_in_nat, ((0, 0), (0, _round_up(d_in, LANES) - d_in)))
    (proj,) = gather_behind(lambda c: mm("proj_in", h1, w_in_nat, comm=c), ici=['mla_w_uq', 'mla_w_ukv', 'w_out'])
    p_rwkv = proj[:, :shift_dim]
    c_q = proj[:, shift_dim:shift_dim + q_lora]
    c_kv = proj[:, shift_dim + q_lora:shift_dim + q_lora + kv_lora]
    k_pe = proj[:, shift_dim + q_lora + kv_lora:d_in]
    shifted = token_shift_fwd(p_rwkv, w2d['rwkv_mu'], tm_wide)
    o1, o2, o3 = c_rwkv, 2 * c_rwkv, 3 * c_rwkv
    hr = _to_heads(shifted[:, :o1], RWKV_HEAD)
    hk = _to_heads(shifted[:, o1:o2], RWKV_HEAD)
    hv = _to_heads(shifted[:, o2:o3], RWKV_HEAD)
    hw = shifted[:, o3:o3 + decay_lora]
    ha = shifted[:, o3 + decay_lora:o3 + decay_lora + aaa_lora]
    hg = shifted[:, o3 + decay_lora + aaa_lora:]

    def per_head(vec):
        return vec.reshape(n_rh, 1, RWKV_HEAD)

    def lora_heads(w):
        return jnp.transpose(w.reshape(w.shape[0], n_rh, RWKV_HEAD), (1, 0, 2))

    pre_args = [hk, hw, ha, hg, per_head(w2d['rwkv_w0']), lora_heads(full['rwkv_w2']), per_head(w2d['rwkv_a0']),
                lora_heads(full['rwkv_a2']), lora_heads(full['rwkv_g2']), per_head(w2d['rwkv_k_k']), per_head(w2d['rwkv_k_a'])]
    pre_kinds = ['hrow', 'row', 'row', 'row', 'hconst', 'hconst', 'hconst', 'hconst', 'hconst', 'hconst', 'hconst']
    decay, kx, a_sc, b_sc, gate_r = gather_behind(
        lambda c: rowwise("rwkv_pre", _rwkv_pre_fn, pre_args, pre_kinds, [('hrow', RWKV_HEAD, F32)] * 5, heads=n_rh, s=s,
                          tm=tm_heads, comm=c),
        ici=['ffn_w_gate'], d2d=['mla_w_uq', 'mla_w_ukv', 'w_out'])
    y_scan, ckpt = gather_behind(lambda c: scan_fwd(hr, decay, kx, hv, a_sc, b_sc, comm=c), ici=['ffn_w_up'], d2d=['ffn_w_gate'])
    post_args = [y_scan, hr, kx, hv, gate_r, per_head(w2d['rwkv_gn_w']), per_head(w2d['rwkv_gn_b']), per_head(w2d['rwkv_r_k'])]
    post_kinds = ['hrow'] * 5 + ['hconst'] * 3
    (y_rwkv_h,) = rowwise("rwkv_post", _rwkv_post_fn, post_args, post_kinds, [('hrow', RWKV_HEAD, F32)], heads=n_rh, s=s, tm=tm_heads)

    pos = positions.reshape(s, 1).astype(F32)
    rot, inv2 = _rot_matrix(), _inv_freq2()
    mla_args = [c_q, c_kv, k_pe, pos, w2d['mla_q_norm_g'], w2d['mla_kv_norm_g'], inv2, rot]
    mla_kinds = ['row', 'row', 'row', 'row', 'const', 'const', 'const', 'const']
    qn, kvn, kp_rot, cos2, sin2 = rowwise(
        "mla_pre", _mla_pre_fn, mla_args, mla_kinds,
        [('row', q_lora, BF16), ('row', kv_lora, BF16), ('row', QK_ROPE, F32), ('row', QK_ROPE, F32), ('row', QK_ROPE, F32)],
        heads=1, s=s, tm=tm)
    assert n_mh == N_DEV and nb['mla_w_uq'] == QK_NOPE + QK_ROPE and nb['mla_w_ukv'] == QK_NOPE + V_HEAD
    assert nbp['mla_w_uq'] == ATTN_SLAB and nbp['mla_w_ukv'] == ATTN_SLAB
    q_h = mm_sh("proj_q", qn, gathered['mla_w_uq'], slabs=True)
    kv_h = mm_sh("proj_kv", kvn, gathered['mla_w_ukv'], slabs=True)
    o_att, lse = gather_behind(lambda c: attn_fwd(q_h, kv_h, kp_rot, cos2, sin2, rot, comm=c), ici=['ffn_w_down'], d2d=['ffn_w_up'])
    ycat = jnp.concatenate([_from_heads(y_rwkv_h), _from_heads(o_att)], axis=-1).astype(BF16)
    (x1,) = gather_behind(lambda c: mm("proj_out", ycat, gathered['w_out'], add=xs, comm=c), d2d=['ffn_w_down'])
    (h2,) = rowwise("rms_ffn", _rms_fn, [x1, w2d['ffn_norm_g']], ['row', 'const'], [('row', d, BF16)], heads=1, s=s, tm=tm)
    gate_pre = mm_sh("ffn_gate", h2, gathered['ffn_w_gate'])
    up = mm_sh("ffn_up", h2, gathered['ffn_w_up'])
    act = ffn_act_fwd(gate_pre, up, conv_w_pad, conv_b_pad)
    x2 = mm("ffn_down", act, gathered['ffn_w_down'], add=x1)

    ones = jnp.ones((s, 1), F32)
    fin_g = w2d['final_norm_g']
    d_x2, dg_final_p, loss_rows = rowwise_vjp("loss_bwd", _loss_fn, [x2, fin_g, tgt], ['row', 'const', 'row'], [ones], ['row'],
                                              [0, 1], heads=1, s=s, tm=tm, primal=True)
    d_x2_b = d_x2.astype(BF16)
    d_act = mm_nt("d_act", d_x2_b, gathered['ffn_w_down'], out_dtype=BF16)
    gsh, chip_sums, from_chips = {}, {}, {}

    def scatter_behind(run, ici=(), swap=()):
        *res, = run(CommGroup([RsChips([chip_sums[n][1] for n in ici]), SiblingSwap([gsh[n] for n in swap])]))
        n_own = len(res) - len(ici) - len(swap)
        from_chips.update(zip(ici, res[n_own:n_own + len(ici)], strict=True))
        _, _, cc = _my_pos()
        for n, from_sibling in zip(swap, res[n_own + len(ici):], strict=True):
            chip_sums[n] = _rs_add_sibling("rs_add_sibling_" + n, gsh[n], from_sibling, cc)
        return res[:n_own]

    gsh['ffn_w_down'] = mm_tn("dw_down", act, d_x2_b).reshape(N_DEV, nbp['ffn_w_gate'], d)
    d_gate, d_up, dcw_p, dcb_p = ffn_act_bwd1(gate_pre, up, conv_w_pad, conv_b_pad, d_act)
    d_gp = ffn_act_bwd2(d_gate, conv_w_pad)
    (d_h2_g,) = scatter_behind(lambda c: mm_sh_nt("d_h2_gate", d_gp, gathered['ffn_w_gate'], comm=c), swap=['ffn_w_down'])
    d_h2 = mm_sh_nt("d_h2_up", d_up, gathered['ffn_w_up'], add=d_h2_g)
    gsh['ffn_w_gate'] = mm_sh_out("dw_gate", h2, d_gp)
    (gsh['ffn_w_up'],) = scatter_behind(lambda c: mm_sh_out("dw_up", h2, d_up, comm=c), swap=['ffn_w_gate'])
    d_x1, dg_ffn_p = rowwise_vjp("rms_ffn_bwd", _rms_fn, [x1, w2d['ffn_norm_g']], ['row', 'const'], [d_h2], ['row'], [0, 1],
                                 heads=1, s=s, tm=tm, plus=d_x2)
    d_x1_b = d_x1.astype(BF16)
    d_ycat = mm_nt("d_ycat", d_x1_b, gathered['w_out'])
    gsh['w_out'] = mm_tn("dw_out", ycat, d_x1_b).reshape((N_DEV,) + w2d['w_out'].shape)
    d_yr_h = _to_heads(d_ycat[:, :c_rwkv], RWKV_HEAD)
    d_o_h = _to_heads(d_ycat[:, c_rwkv:], V_HEAD)

    d_q, d_kv, d_kp_h = scatter_behind(
        lambda c: attn_bwd(q_h, kv_h, kp_rot, cos2, sin2, rot, o_att, lse, d_o_h, comm=c),
        ici=['ffn_w_down'], swap=['ffn_w_up', 'w_out'])
    d_kp_rot = headsum("d_kpe_heads", d_kp_h)
    d_qn = mm_sh_nt("d_qn", d_q, gathered['mla_w_uq'])
    d_kvn = mm_sh_nt("d_kvn", d_kv, gathered['mla_w_ukv'])
    gsh['mla_w_uq'] = mm_sh_out("dw_uq", qn, d_q)
    gsh['mla_w_ukv'] = mm_sh_out("dw_ukv", kvn, d_kv)
    d_cq, d_ckv, d_kpe, dg_q_p, dg_kv_p = rowwise_vjp(
        "mla_pre_bwd", _mla_pre_grad_fn, mla_args, mla_kinds, [d_qn, d_kvn, d_kp_rot], ['row', 'row', 'row'], [0, 1, 2, 4, 5],
        heads=1, s=s, tm=tm)

    d_y, d_r_post, d_k_post, d_v_post, d_gate_r, dgnw_p, dgnb_p, drk_p = scatter_behind(
        lambda c: rowwise_vjp("rwkv_post_bwd", _rwkv_post_fn, post_args, post_kinds, [d_yr_h], ['hrow'], list(range(8)),
                              heads=n_rh, s=s, tm=tm_heads, comm=c),
        ici=['ffn_w_gate'], swap=['mla_w_uq', 'mla_w_ukv'])
    d_r_sc, d_w_sc, d_k_sc, d_v_sc, d_a_sc, d_b_sc = scatter_behind(
        lambda c: scan_bwd(hr, decay, kx, hv, a_sc, b_sc, ckpt, d_y, comm=c), ici=['ffn_w_up', 'w_out'])
    d_hk, d_hw_p, d_ha_p, d_hg_p, dw0_p, dw2_p, da0_p, da2_p, dg2_p, dkk_p, dka_p, d_hr, d_hv = scatter_behind(
        lambda c: rowwise_vjp(
            "rwkv_pre_bwd", _rwkv_pre_grad_fn, pre_args + [hr, hv], pre_kinds + ['hrow', 'hrow'],
            [d_w_sc, d_k_sc, d_k_post, d_a_sc, d_b_sc, d_gate_r, d_r_sc, d_r_post, d_v_sc, d_v_post], ['hrow'] * 10,
            list(range(13)), heads=n_rh, s=s, tm=tm_heads, comm=c),
        ici=['mla_w_uq', 'mla_w_ukv'])
    d_shifted = jnp.concatenate([_from_heads(d_hr), _from_heads(d_hk), _from_heads(d_hv), headsum("d_hw_heads", d_hw_p),
                                 headsum("d_ha_heads", d_ha_p), headsum("d_hg_heads", d_hg_p)], axis=-1)
    d_p_rwkv, dmu_p = token_shift_bwd(p_rwkv, w2d['rwkv_mu'], d_shifted, tm_wide)
    d_proj = pad_cols(jnp.concatenate([d_p_rwkv, d_cq, d_ckv, d_kpe], axis=-1).astype(BF16), nb['w_in'], nbp['w_in'])
    gsh['w_in'] = mm_sh_out("dw_in", h1, d_proj)
    chip_sums['w_in'] = rs_chip_sum('w_in', gsh['w_in'])
    (d_h1,) = scatter_behind(lambda c: mm_sh_nt("d_h1", d_proj, gathered['w_in'], comm=c), ici=['w_in'])
    grad_x, dg_attn_p = rowwise_vjp("rms_attn_bwd", _rms_fn, [xs, w2d['attn_norm_g']], ['row', 'const'], [d_h1], ['row'], [0, 1],
                                    heads=1, s=s, tm=tm, plus=d_x1)

    def from_heads_lora(g):
        return jnp.transpose(g, (1, 0, 2)).reshape(g.shape[1], n_rh * RWKV_HEAD)

    gw = {}
    gw['rwkv_w2'] = from_heads_lora(sum_partials("sum_dw2", dw2_p, True))
    gw['rwkv_a2'] = from_heads_lora(sum_partials("sum_da2", da2_p, True))
    gw['rwkv_g2'] = from_heads_lora(sum_partials("sum_dg2", dg2_p, True))
    dcw_pad = colsum("sum_dconv_w", dcw_p.reshape(dcw_p.shape[0], CONV_W * f_pad)).reshape(CONV_W, f_pad)
    gw['ffn_conv_w'] = unpad_cols(dcw_pad, nb['ffn_w_gate'], nbp['ffn_w_gate'])

    rep = {
        'attn_norm_g': sum_partials("sum_dg_attn", dg_attn_p, False),
        'rwkv_mu': colsum("sum_dmu", dmu_p.reshape(dmu_p.shape[0], shift_dim)),
        'rwkv_w0': sum_partials("sum_dw0", dw0_p, True).reshape(1, c_rwkv),
        'rwkv_a0': sum_partials("sum_da0", da0_p, True).reshape(1, c_rwkv),
        'rwkv_k_k': sum_partials("sum_dkk", dkk_p, True).reshape(1, c_rwkv),
        'rwkv_k_a': sum_partials("sum_dka", dka_p, True).reshape(1, c_rwkv),
        'rwkv_r_k': sum_partials("sum_drk", drk_p, True).reshape(1, c_rwkv),
        'rwkv_gn_w': sum_partials("sum_dgnw", dgnw_p, True).reshape(1, c_rwkv),
        'rwkv_gn_b': sum_partials("sum_dgnb", dgnb_p, True).reshape(1, c_rwkv),
        'mla_q_norm_g': sum_partials("sum_dg_q", dg_q_p, False),
        'mla_kv_norm_g': sum_partials("sum_dg_kv", dg_kv_p, False),
        'ffn_norm_g': sum_partials("sum_dg_ffn", dg_ffn_p, False),
        'ffn_conv_b': unpad_cols(colsum("sum_dconv_b", dcb_p.reshape(dcb_p.shape[0], f_pad)), nb['ffn_w_gate'], nbp['ffn_w_gate']),
        'final_norm_g': sum_partials("sum_dg_final", dg_final_p, False),
        'loss': sum_all("sum_loss", loss_rows.reshape(s // SUBLANES, SUBLANES)),
    }
    rep_pack = Pack([(n, w2d[n].shape) for n in REPLICATED] + [('loss', (1, 1))], 8)
    rep_all = all_gather("gather_rep_grads", rep_pack.pack(rep, F32))
    rep_sum = colsum("sum_rep_grads", rep_all.reshape(N_DEV, rep_pack.total)).reshape(rep_pack.rows, PACK_W)
    rep_g = rep_pack.unpack(rep_sum)
    loss = rep_g.pop('loss').reshape(())

    grads, deltas, new_m, new_v = dict(rep_g), {}, {}, {}
    my_x, my_y, _ = _my_pos()
    for n in BIG:
        grads[n], deltas[n], new_m[n], new_v[n] = adamw_scatter(
            "adamw_" + n, wts[n], mom_m[n], mom_v[n], chip_sums[n][0], from_chips[n], 2 * my_x + my_y)
    sm_pack = Pack([(n, w2d[n].shape) for n in SMALL_SHARDED], 8)
    g_shards = {n: _full_to_shards(gw[n], SMALL_SHARDED[n]) for n in SMALL_SHARDED}
    grads.update(sm_pack.unpack(reduce_scatter("small", sm_pack.pack(g_shards, F32, lead=(N_DEV,)))))
    rest_pack = Pack([(n, w2d[n].shape) for n in WEIGHTS if n not in BIG], 8)
    d_r, m_r, v_r = rowwise(
        "adamw_small", _adamw_fn, [rest_pack.pack(w2d, F32), rest_pack.pack(grads, F32), rest_pack.pack(m2d, F32),
                                   rest_pack.pack(v2d, F32)],
        ['row'] * 4, [('row', PACK_W, F32)] * 3, heads=1, s=rest_pack.rows, tm=_pick(rest_pack.rows, 512, SUBLANES))
    deltas.update(rest_pack.unpack(d_r))
    new_m.update(rest_pack.unpack(m_r))
    new_v.update(rest_pack.unpack(v_r))

    def shaped(dct):
        return [dct[n].reshape(out_shapes[n]) for n in WEIGHTS]

    return (loss, grad_x.reshape(x.shape), *shaped(grads), *shaped(deltas), *shaped(new_m), *shaped(new_v))
```

```python
import functools
import math

import jax
import jax.numpy as jnp
import numpy as np
from jax import lax
from jax.experimental import pallas as pl
from jax.experimental.pallas import tpu as pltpu

F32 = jnp.float32
BF16 = jnp.bfloat16
HIGHEST = lax.Precision.HIGHEST
MESH = pl.DeviceIdType.MESH

N_DEV = 8
LANES = 128
SUBLANES = 8
VMEM_LIMIT = 48 * 1024 * 1024
RESIDENT_BYTES = 8 * 1024 * 1024

NORM_EPS = 1e-6
GN_EPS = 64e-5
RWKV_HEAD = 64
QK_NOPE = 128
QK_ROPE = 64
V_HEAD = 128
ROPE_THETA = 10000.0
CONV_W = 3
NEG_INF = -1e30
SCAN_CHUNK = 64
SCAN_HEADS = 16
SCAN_PASSES_SOLVE = 1
SCAN_PASSES_OUT = 1

ADAM_LR = 0.001
ADAM_B1 = 0.9
ADAM_B2 = 0.999
ADAM_EPS = 1e-08
ADAM_WD = 0.01
ADAM_STEP = 10

WEIGHTS = ['attn_norm_g', 'w_in', 'rwkv_mu', 'rwkv_w0', 'rwkv_w2', 'rwkv_a0', 'rwkv_a2', 'rwkv_g2', 'rwkv_k_k',
           'rwkv_k_a', 'rwkv_r_k', 'rwkv_gn_w', 'rwkv_gn_b', 'mla_q_norm_g', 'mla_w_uq', 'mla_kv_norm_g', 'mla_w_ukv',
           'w_out', 'ffn_norm_g', 'ffn_w_gate', 'ffn_w_up', 'ffn_conv_w', 'ffn_conv_b', 'ffn_w_down', 'final_norm_g']
BIG = {'w_in': 'col', 'mla_w_uq': 'col', 'mla_w_ukv': 'col', 'w_out': 'row', 'ffn_w_gate': 'col', 'ffn_w_up': 'col',
       'ffn_w_down': 'row'}
SMALL_SHARDED = {'rwkv_w2': 'col', 'rwkv_a2': 'col', 'rwkv_g2': 'col', 'ffn_conv_w': 'col'}
SHARDED = {**BIG, **SMALL_SHARDED}
REPLICATED = [n for n in WEIGHTS if n not in SHARDED]


def _round_up(n, m):
    return (n + m - 1) // m * m


def _pick(n, cap, unit):
    if n <= cap:
        return n
    best = None
    for t in range(unit, cap + 1, unit):
        if n % t == 0:
            best = t
    assert best is not None, (n, cap, unit)
    return best


def _params(sem):
    return pltpu.CompilerParams(dimension_semantics=sem, vmem_limit_bytes=VMEM_LIMIT)


def mm(name, a, b, add=None, out_dtype=F32, comm=None):
    m, k = a.shape
    k2, n = b.shape
    assert k == k2, (name, a.shape, b.shape)
    tn = n if k * n * 2 <= RESIDENT_BYTES else _pick(n, 640, LANES)
    tm = _pick(m, 2048 if (tn < n and m * k * 2 <= RESIDENT_BYTES) else 512, SUBLANES * 2)
    has_add = add is not None

    def body(a_ref, b_ref, *rest):
        o_ref = rest[-1]
        acc = jnp.dot(a_ref[...].astype(BF16), b_ref[...].astype(BF16), preferred_element_type=F32)
        if has_add:
            acc = acc + rest[0][...].astype(F32)
        o_ref[...] = acc.astype(o_ref.dtype)

    in_specs = [pl.BlockSpec((tm, k), lambda i, j: (i, 0)), pl.BlockSpec((k, tn), lambda i, j: (0, j))]
    ops = [a, b]
    if has_add:
        in_specs.append(pl.BlockSpec((tm, tn), lambda i, j: (i, j)))
        ops.append(add)
    res = _pallas(
        body, name=name, grid=(m // tm, n // tn), in_specs=in_specs,
        out_specs=[pl.BlockSpec((tm, tn), lambda i, j: (i, j))],
        out_shape=[jax.ShapeDtypeStruct((m, n), out_dtype)], sem=("parallel", "parallel"), comm=comm,
    )(*ops)
    return res[0] if comm is None else res


def mm_nt(name, a, b, out_dtype=F32):
    m, k = a.shape
    n, k2 = b.shape
    assert k == k2, (name, a.shape, b.shape)
    tm = _pick(m, 2048 if m * k * 2 <= RESIDENT_BYTES else 512, SUBLANES * 2)
    tn = _pick(n, 1024, LANES)

    def body(a_ref, b_ref, o_ref):
        acc = lax.dot_general(a_ref[...].astype(BF16), b_ref[...].astype(BF16), (((1,), (1,)), ((), ())),
                              preferred_element_type=F32)
        o_ref[...] = acc.astype(o_ref.dtype)

    return pl.pallas_call(
        body, name=name, grid=(m // tm, n // tn),
        in_specs=[pl.BlockSpec((tm, k), lambda i, j: (i, 0)), pl.BlockSpec((tn, k), lambda i, j: (j, 0))],
        out_specs=pl.BlockSpec((tm, tn), lambda i, j: (i, j)),
        out_shape=jax.ShapeDtypeStruct((m, n), out_dtype),
        compiler_params=_params(("parallel", "parallel")),
    )(a, b)


def mm_sh(name, a, g, out_dtype=F32, comm=None, slabs=False):
    m, k = a.shape
    nd, k2, nbp = g.shape
    assert k == k2, (name, a.shape, g.shape)
    tm = _pick(m, 2048 if m * k * 2 <= RESIDENT_BYTES else 512, SUBLANES * 2)

    def body(a_ref, b_ref, o_ref):
        o_ref[...] = jnp.dot(a_ref[...].astype(BF16), b_ref[...].astype(BF16), preferred_element_type=F32).astype(o_ref.dtype)

    if slabs:
        out_spec, out_shape = pl.BlockSpec((None, tm, nbp), lambda i, j: (j, i, 0)), (nd, m, nbp)
    else:
        out_spec, out_shape = pl.BlockSpec((tm, nbp), lambda i, j: (i, j)), (m, nd * nbp)
    res = _pallas(
        body, name=name, grid=(m // tm, nd),
        in_specs=[pl.BlockSpec((tm, k), lambda i, j: (i, 0)), pl.BlockSpec((None, k, nbp), lambda i, j: (j, 0, 0))],
        out_specs=[out_spec], out_shape=[jax.ShapeDtypeStruct(out_shape, out_dtype)], sem=("parallel", "parallel"), comm=comm,
    )(a, g)
    return res[0] if comm is None else res


def mm_sh_nt(name, a, g, add=None, comm=None):
    nd, k, nbp = g.shape
    slabs = a.ndim == 3
    m = a.shape[1] if slabs else a.shape[0]
    assert a.shape == ((nd, m, nbp) if slabs else (m, nd * nbp)), (name, a.shape, g.shape)
    has_add = add is not None
    tm = _pick(m, 512 if has_add else 1024, SUBLANES * 2)

    def body(a_ref, b_ref, *rest):
        o_ref = rest[-1]
        part = lax.dot_general(a_ref[...].astype(BF16), b_ref[...].astype(BF16), (((1,), (1,)), ((), ())),
                               preferred_element_type=F32)

        @pl.when(pl.program_id(1) == 0)
        def _():
            o_ref[...] = part + rest[0][...] if has_add else part

        @pl.when(pl.program_id(1) != 0)
        def _():
            o_ref[...] += part

    a_spec = pl.BlockSpec((None, tm, nbp), lambda i, j: (j, i, 0)) if slabs else pl.BlockSpec((tm, nbp), lambda i, j: (i, j))
    in_specs = [a_spec, pl.BlockSpec((None, k, nbp), lambda i, j: (j, 0, 0))]
    ops = [a, g]
    if has_add:
        in_specs.append(pl.BlockSpec((tm, k), lambda i, j: (i, 0)))
        ops.append(add)
    res = _pallas(
        body, name=name, grid=(m // tm, nd), in_specs=in_specs,
        out_specs=[pl.BlockSpec((tm, k), lambda i, j: (i, 0))],
        out_shape=[jax.ShapeDtypeStruct((m, k), F32)], sem=("parallel", "arbitrary"), comm=comm,
    )(*ops)
    return res[0] if comm is None else res


def mm_tn(name, a, b):
    m, k = a.shape
    m2, n = b.shape
    assert m == m2, (name, a.shape, b.shape)
    tk = _pick(k, 512, LANES)
    tn = n if m * n * 2 <= RESIDENT_BYTES else _pick(n, 640, LANES)

    def body(a_ref, b_ref, o_ref):
        o_ref[...] = lax.dot_general(a_ref[...].astype(BF16), b_ref[...].astype(BF16), (((0,), (0,)), ((), ())),
                                     preferred_element_type=F32)

    return pl.pallas_call(
        body, name=name, grid=(k // tk, n // tn),
        in_specs=[pl.BlockSpec((m, tk), lambda i, j: (0, i)), pl.BlockSpec((m, tn), lambda i, j: (0, j))],
        out_specs=pl.BlockSpec((tk, tn), lambda i, j: (i, j)),
        out_shape=jax.ShapeDtypeStruct((k, n), F32),
        compiler_params=_params(("parallel", "parallel")),
    )(a, b)


def mm_sh_out(name, a, b, comm=None):
    m, k = a.shape
    slabs = b.ndim == 3
    nbp = b.shape[2] if slabs else b.shape[1] // N_DEV
    assert b.shape == ((N_DEV, m, nbp) if slabs else (m, N_DEV * nbp)), (name, a.shape, b.shape)
    tk = _pick(k, 2048 if k * m * 2 <= RESIDENT_BYTES else 512, LANES)
    b_spec = pl.BlockSpec((None, m, nbp), lambda i, j: (j, 0, 0)) if slabs else pl.BlockSpec((m, nbp), lambda i, j: (0, j))

    def body(a_ref, b_ref, o_ref):
        o_ref[...] = lax.dot_general(a_ref[...].astype(BF16), b_ref[...].astype(BF16), (((0,), (0,)), ((), ())),
                                     preferred_element_type=F32)

    res = _pallas(
        body, name=name, grid=(k // tk, N_DEV),
        in_specs=[pl.BlockSpec((m, tk), lambda i, j: (0, i)), b_spec],
        out_specs=[pl.BlockSpec((None, tk, nbp), lambda i, j: (j, i, 0))],
        out_shape=[jax.ShapeDtypeStruct((N_DEV, k, nbp), F32)], sem=("parallel", "parallel"), comm=comm,
    )(a, b)
    return res[0] if comm is None else res


def pad_cols(y, nb, nbp):
    m = y.shape[0]
    if nb == nbp:
        return y
    return jnp.pad(y.reshape(m, N_DEV, nb), ((0, 0), (0, 0), (0, nbp - nb))).reshape(m, N_DEV * nbp)


def unpad_cols(y, nb, nbp):
    m = y.shape[0]
    if nb == nbp:
        return y
    return y.reshape(m, N_DEV, nbp)[:, :, :nb].reshape(m, N_DEV * nb)


def _in_spec(kind, a, tm):
    if kind == 'row':
        return pl.BlockSpec((tm, a.shape[1]), lambda h, i: (i, 0))
    if kind == 'hrow':
        return pl.BlockSpec((None, tm, a.shape[2]), lambda h, i: (h, i, 0))
    if kind == 'const':
        return pl.BlockSpec(a.shape, lambda h, i: (0, 0))
    assert kind == 'hconst', kind
    return pl.BlockSpec((None,) + a.shape[1:], lambda h, i: (h, 0, 0))


def _row_out(kind, c, dtype, heads, s, tm):
    if kind == 'row':
        assert heads == 1
        return jax.ShapeDtypeStruct((s, c), dtype), pl.BlockSpec((tm, c), lambda h, i: (i, 0))
    return jax.ShapeDtypeStruct((heads, s, c), dtype), pl.BlockSpec((None, tm, c), lambda h, i: (h, i, 0))


def rowwise(name, fn, arrs, kinds, outs, *, heads, s, tm, comm=None):
    n_in = len(arrs)

    def body(*refs):
        vals = fn(*[r[...] for r in refs[:n_in]])
        for o, v in zip(refs[n_in:], vals, strict=True):
            o[...] = v.astype(o.dtype)

    shapes, specs = zip(*[_row_out(k, c, dt, heads, s, tm) for k, c, dt in outs])
    return _pallas(
        body, name=name, grid=(heads, s // tm),
        in_specs=[_in_spec(k, a, tm) for k, a in zip(kinds, arrs, strict=True)],
        out_specs=list(specs), out_shape=list(shapes), sem=("parallel", "parallel"), comm=comm,
    )(*arrs)


def rowwise_vjp(name, fn, arrs, kinds, cots, cot_kinds, wrt, *, heads, s, tm, out_dtypes=None, primal=False, comm=None,
                plus=None):
    n_in, n_cot = len(arrs), len(cots)
    nb = s // tm
    out_dtypes = out_dtypes or [F32] * len(wrt)
    extra = [] if plus is None else [plus]

    def body(*refs):
        vals = [r[...] for r in refs[:n_in]]
        cvals = tuple(r[...].astype(F32) for r in refs[n_in:n_in + n_cot])
        outs = refs[n_in + n_cot + len(extra):]

        def f(*dv):
            full = list(vals)
            for j, i in enumerate(wrt):
                full[i] = dv[j]
            return tuple(fn(*full))

        prim, vjp_fn = jax.vjp(f, *[vals[i].astype(F32) for i in wrt])
        grads = list(vjp_fn(cvals))
        if plus is not None:
            grads[0] = grads[0] + refs[n_in + n_cot][...]
        for o, g in zip(outs[:len(wrt)], grads, strict=True):
            o[...] = g.astype(o.dtype)
        if primal:
            for o, p in zip(outs[len(wrt):], prim, strict=True):
                o[...] = p.astype(o.dtype)

    shapes, specs = [], []
    for i, dt in zip(wrt, out_dtypes, strict=True):
        kind, a = kinds[i], arrs[i]
        if kind in ('row', 'hrow'):
            c = a.shape[-1]
            sh, sp = _row_out('row' if (kind == 'row' and heads == 1) else 'hrow', c, dt, heads, s, tm)
        else:
            r, c = a.shape[-2:]
            sh = jax.ShapeDtypeStruct((heads, nb, r, c), dt)
            sp = pl.BlockSpec((None, None, r, c), lambda h, i: (h, i, 0, 0))
        shapes.append(sh)
        specs.append(sp)
    if primal:
        for ck, c in zip(cot_kinds, cots, strict=True):
            sh, sp = _row_out(ck, c.shape[-1], F32, heads, s, tm)
            shapes.append(sh)
            specs.append(sp)
    in_specs = [_in_spec(k, a, tm) for k, a in zip(kinds, arrs, strict=True)]
    in_specs += [_in_spec(k, a, tm) for k, a in zip(cot_kinds, cots, strict=True)]
    in_specs += [_in_spec('row', a, tm) for a in extra]
    return _pallas(
        body, name=name, grid=(heads, nb), in_specs=in_specs, out_specs=specs, out_shape=shapes,
        sem=("parallel", "parallel"), comm=comm,
    )(*arrs, *cots, *extra)


def colsum(name, x):
    n, m = x.shape
    tc = _pick(m, 32768, LANES) if m % LANES == 0 else m

    def body(x_ref, o_ref):
        acc = x_ref[0:1, :].astype(F32)
        for r in range(1, n):
            acc = acc + x_ref[r:r + 1, :].astype(F32)
        o_ref[...] = acc

    return pl.pallas_call(
        body, name=name, grid=(m // tc,), in_specs=[pl.BlockSpec((n, tc), lambda j: (0, j))],
        out_specs=pl.BlockSpec((1, tc), lambda j: (0, j)), out_shape=jax.ShapeDtypeStruct((1, m), F32),
        compiler_params=_params(("parallel",)),
    )(x)


def headsum(name, x):
    h, s, c = x.shape
    tm = _pick(s, 256, SUBLANES)

    def body(x_ref, o_ref):
        acc = x_ref[0]
        for j in range(1, h):
            acc = acc + x_ref[j]
        o_ref[...] = acc

    return pl.pallas_call(
        body, name=name, grid=(s // tm,), in_specs=[pl.BlockSpec((h, tm, c), lambda i: (0, i, 0))],
        out_specs=pl.BlockSpec((tm, c), lambda i: (i, 0)), out_shape=jax.ShapeDtypeStruct((s, c), F32),
        compiler_params=_params(("parallel",)),
    )(x)


def sum_all(name, x):
    def body(x_ref, o_ref):
        o_ref[...] = jnp.sum(x_ref[...], keepdims=True)

    return pl.pallas_call(body, name=name, out_shape=jax.ShapeDtypeStruct((1, 1), F32))(x)


def sum_partials(name, p, per_head):
    h, nb, r, c = p.shape
    if per_head:
        flat = jnp.transpose(p, (1, 0, 2, 3)).reshape(nb, h * r * c)
        if nb == 1:
            return flat.reshape(h, r, c)
        return colsum(name, flat).reshape(h, r, c)
    flat = p.reshape(h * nb, r * c)
    if h * nb == 1:
        return flat.reshape(r, c)
    return colsum(name, flat).reshape(r, c)


def _rms_fn(x, g):
    xf = x.astype(F32)
    return (xf * lax.rsqrt(jnp.mean(xf * xf, axis=-1, keepdims=True) + NORM_EPS) * g,)


def _softplus(z):
    return jnp.maximum(z, 0.0) + jnp.log(1.0 + jnp.exp(-jnp.abs(z)))


def _rwkv_pre_fn(hk, hw, ha, hg, w0, w2, a0, a2, g2, k_k, k_a):
    zw = w0 + jnp.dot(jnp.tanh(hw), w2, preferred_element_type=F32)
    w_log = -_softplus(-zw) - 0.5
    decay = jnp.exp(-jnp.exp(w_log))
    a = jax.nn.sigmoid(a0 + jnp.dot(ha, a2, preferred_element_type=F32))
    g = jnp.dot(jax.nn.sigmoid(hg), g2, preferred_element_type=F32)
    kk = hk * k_k
    kk = kk * lax.rsqrt(jnp.maximum(jnp.sum(kk * kk, axis=-1, keepdims=True), 1e-24))
    k = hk * (1.0 + (a - 1.0) * k_a)
    return decay, k, -kk, kk * a, g


def _rwkv_pre_grad_fn(hk, hw, ha, hg, w0, w2, a0, a2, g2, k_k, k_a, hr, hv):
    decay, k, a_sc, b_sc, g = _rwkv_pre_fn(hk, hw, ha, hg, w0, w2, a0, a2, g2, k_k, k_a)
    return decay, k, k, a_sc, b_sc, g, hr, hr, hv, hv


def _rwkv_post_fn(y, r, k, v, g, gn_w, gn_b, r_k):
    mu = jnp.mean(y, axis=-1, keepdims=True)
    var = jnp.mean(jnp.square(y - mu), axis=-1, keepdims=True)
    yn = (y - mu) * lax.rsqrt(var + GN_EPS) * gn_w + gn_b
    bonus = jnp.sum(r * k * r_k, axis=-1, keepdims=True) * v
    return ((yn + bonus) * g,)


def _rope_tables(pos, inv_freq2):
    ang = pos * inv_freq2
    return jnp.cos(ang), jnp.sin(ang)


def _rope(t, cos2, sin2, rot):
    return t * cos2 + jnp.dot(t, rot, precision=HIGHEST, preferred_element_type=F32) * sin2


def _mla_pre_fn(c_q, c_kv, k_pe, pos, q_g, kv_g, inv_freq2, rot):
    cos2, sin2 = _rope_tables(pos, inv_freq2)
    return _rms_fn(c_q, q_g)[0], _rms_fn(c_kv, kv_g)[0], _rope(k_pe, cos2, sin2, rot), cos2, sin2


def _mla_pre_grad_fn(c_q, c_kv, k_pe, pos, q_g, kv_g, inv_freq2, rot):
    return _mla_pre_fn(c_q, c_kv, k_pe, pos, q_g, kv_g, inv_freq2, rot)[:3]


def _loss_fn(x2, g, target):
    y = _rms_fn(x2, g)[0]
    return (0.5 * jnp.mean(jnp.square(y - target), axis=-1, keepdims=True),)


def _adamw_fn(w, g, m, v):
    m = ADAM_B1 * m + (1.0 - ADAM_B1) * g
    v = ADAM_B2 * v + (1.0 - ADAM_B2) * jnp.square(g)
    m_hat = m / (1.0 - ADAM_B1 ** ADAM_STEP)
    v_hat = v / (1.0 - ADAM_B2 ** ADAM_STEP)
    delta = -ADAM_LR * (m_hat / (jnp.sqrt(v_hat) + ADAM_EPS) + ADAM_WD * w)
    return delta, m, v


def _prev_halo_spec(c, tm):
    return pl.BlockSpec((SUBLANES, c), lambda i: (jnp.maximum(i * (tm // SUBLANES) - 1, 0), 0))


def _next_halo_spec(c, tm, s):
    return pl.BlockSpec((SUBLANES, c), lambda i: (jnp.minimum((i + 1) * (tm // SUBLANES), s // SUBLANES - 1), 0))


def _shift_down(p, halo, first_block, n):
    out = pltpu.roll(p, n, 0)
    row = lax.broadcasted_iota(jnp.int32, p.shape, 0)
    for j in range(n):
        top = jnp.where(first_block, 0.0, halo[SUBLANES - n + j:SUBLANES - n + j + 1, :])
        out = jnp.where(row == j, top, out)
    return out


def _shift_up(p, halo, last_block, n):
    rows = p.shape[0]
    out = pltpu.roll(p, rows - n, 0)
    row = lax.broadcasted_iota(jnp.int32, p.shape, 0)
    for j in range(n):
        bot = jnp.where(last_block, 0.0, halo[j:j + 1, :])
        out = jnp.where(row == rows - n + j, bot, out)
    return out


def token_shift_fwd(p, mu, tm):
    s, c = p.shape

    def body(p_ref, halo_ref, mu_ref, o_ref):
        pv = p_ref[...]
        prev = _shift_down(pv, halo_ref[...], pl.program_id(0) == 0, 1)
        o_ref[...] = pv + (prev - pv) * mu_ref[...]

    return pl.pallas_call(
        body, name="token_shift_fwd", grid=(s // tm,),
        in_specs=[pl.BlockSpec((tm, c), lambda i: (i, 0)), _prev_halo_spec(c, tm), pl.BlockSpec((1, c), lambda i: (0, 0))],
        out_specs=pl.BlockSpec((tm, c), lambda i: (i, 0)), out_shape=jax.ShapeDtypeStruct((s, c), F32),
        compiler_params=_params(("parallel",)),
    )(p, p, mu)


def token_shift_bwd(p, mu, ds, tm):
    s, c = p.shape
    nb = s // tm

    def body(p_ref, halo_ref, mu_ref, ds_ref, dsn_ref, dp_ref, dmu_ref):
        i = pl.program_id(0)
        pv, dsv, muv = p_ref[...], ds_ref[...], mu_ref[...]
        prev = _shift_down(pv, halo_ref[...], i == 0, 1)
        nxt = _shift_up(dsv, dsn_ref[...], i == nb - 1, 1)
        dp_ref[...] = dsv * (1.0 - muv) + nxt * muv
        dmu_ref[...] = jnp.sum(dsv * (prev - pv), axis=0, keepdims=True)

    return pl.pallas_call(
        body, name="token_shift_bwd", grid=(nb,),
        in_specs=[pl.BlockSpec((tm, c), lambda i: (i, 0)), _prev_halo_spec(c, tm), pl.BlockSpec((1, c), lambda i: (0, 0)),
                  pl.BlockSpec((tm, c), lambda i: (i, 0)), _next_halo_spec(c, tm, s)],
        out_specs=[pl.BlockSpec((tm, c), lambda i: (i, 0)), pl.BlockSpec((None, 1, c), lambda i: (i, 0, 0))],
        out_shape=[jax.ShapeDtypeStruct((s, c), F32), jax.ShapeDtypeStruct((nb, 1, c), F32)],
        compiler_params=_params(("parallel",)),
    )(p, p, mu, ds, ds)


def _ffn_tiles(s, f):
    return _pick(s, 256, SUBLANES), _pick(f, 1408, LANES)


def _conv_gate(gp, halo, first_block, cw, cb):
    p1 = _shift_down(gp, halo, first_block, 1)
    p2 = _shift_down(gp, halo, first_block, 2)
    return cw[0:1, :] * p2 + cw[1:2, :] * p1 + cw[2:3, :] * gp + cb, p1, p2


def ffn_act_fwd(gate_pre, up, conv_w, conv_b):
    s, f = gate_pre.shape
    tm, tc = _ffn_tiles(s, f)

    def body(gp_ref, halo_ref, up_ref, cw_ref, cb_ref, o_ref):
        gate, _, _ = _conv_gate(gp_ref[...], halo_ref[...], pl.program_id(0) == 0, cw_ref[...], cb_ref[...])
        o_ref[...] = (gate * jax.nn.sigmoid(gate) * up_ref[...]).astype(o_ref.dtype)

    blk = pl.BlockSpec((tm, tc), lambda i, j: (i, j))
    return pl.pallas_call(
        body, name="ffn_act_fwd", grid=(s // tm, f // tc),
        in_specs=[blk, pl.BlockSpec((SUBLANES, tc), lambda i, j: (jnp.maximum(i * (tm // SUBLANES) - 1, 0), j)), blk,
                  pl.BlockSpec((CONV_W, tc), lambda i, j: (0, j)), pl.BlockSpec((1, tc), lambda i, j: (0, j))],
        out_specs=blk, out_shape=jax.ShapeDtypeStruct((s, f), BF16),
        compiler_params=_params(("parallel", "parallel")),
    )(gate_pre, gate_pre, up, conv_w, conv_b)


def ffn_act_bwd1(gate_pre, up, conv_w, conv_b, d_act):
    s, f = gate_pre.shape
    tm, tc = _ffn_tiles(s, f)
    nb = s // tm

    def body(gp_ref, halo_ref, up_ref, cw_ref, cb_ref, da_ref, dg_ref, du_ref, dcw_ref, dcb_ref):
        gp = gp_ref[...]
        gate, p1, p2 = _conv_gate(gp, halo_ref[...], pl.program_id(0) == 0, cw_ref[...], cb_ref[...])
        sig = jax.nn.sigmoid(gate)
        da = da_ref[...].astype(F32)
        du_ref[...] = (da * gate * sig).astype(du_ref.dtype)
        dg = da * up_ref[...] * (sig * (1.0 + gate * (1.0 - sig)))
        dg_ref[...] = dg
        dcb_ref[...] = jnp.sum(dg, axis=0, keepdims=True)
        dcw_ref[0:1, :] = jnp.sum(dg * p2, axis=0, keepdims=True)
        dcw_ref[1:2, :] = jnp.sum(dg * p1, axis=0, keepdims=True)
        dcw_ref[2:3, :] = jnp.sum(dg * gp, axis=0, keepdims=True)

    blk = pl.BlockSpec((tm, tc), lambda i, j: (i, j))
    return pl.pallas_call(
        body, name="ffn_act_bwd1", grid=(nb, f // tc),
        in_specs=[blk, pl.BlockSpec((SUBLANES, tc), lambda i, j: (jnp.maximum(i * (tm // SUBLANES) - 1, 0), j)), blk,
                  pl.BlockSpec((CONV_W, tc), lambda i, j: (0, j)), pl.BlockSpec((1, tc), lambda i, j: (0, j)), blk],
        out_specs=[blk, blk, pl.BlockSpec((None, CONV_W, tc), lambda i, j: (i, 0, j)),
                   pl.BlockSpec((None, 1, tc), lambda i, j: (i, 0, j))],
        out_shape=[jax.ShapeDtypeStruct((s, f), F32), jax.ShapeDtypeStruct((s, f), BF16),
                   jax.ShapeDtypeStruct((nb, CONV_W, f), F32), jax.ShapeDtypeStruct((nb, 1, f), F32)],
        compiler_params=_params(("parallel", "parallel")),
    )(gate_pre, gate_pre, up, conv_w, conv_b, d_act)


def ffn_act_bwd2(d_gate, conv_w):
    s, f = d_gate.shape
    tm, tc = _ffn_tiles(s, f)
    nb = s // tm

    def body(dg_ref, halo_ref, cw_ref, o_ref):
        dg, cw = dg_ref[...], cw_ref[...]
        last = pl.program_id(0) == nb - 1
        n1 = _shift_up(dg, halo_ref[...], last, 1)
        n2 = _shift_up(dg, halo_ref[...], last, 2)
        o_ref[...] = (cw[2:3, :] * dg + cw[1:2, :] * n1 + cw[0:1, :] * n2).astype(o_ref.dtype)

    blk = pl.BlockSpec((tm, tc), lambda i, j: (i, j))
    return pl.pallas_call(
        body, name="ffn_act_bwd2", grid=(nb, f // tc),
        in_specs=[blk, pl.BlockSpec((SUBLANES, tc), lambda i, j: (jnp.minimum((i + 1) * (tm // SUBLANES), s // SUBLANES - 1), j)),
                  pl.BlockSpec((CONV_W, tc), lambda i, j: (0, j))],
        out_specs=blk, out_shape=jax.ShapeDtypeStruct((s, f), BF16),
        compiler_params=_params(("parallel", "parallel")),
    )(d_gate, d_gate, conv_w)


def _mxu(x, y, cx, cy):
    if x.ndim == 3:
        return lax.dot_general(x, y, (((cx + 1,), (cy + 1,)), ((0,), (0,))), preferred_element_type=F32)
    return lax.dot_general(x, y, (((cx,), (cy,)), ((), ())), preferred_element_type=F32)


def _split(x):
    hi = x.astype(BF16)
    return hi, (x - hi.astype(F32)).astype(BF16)


def _make_dot3(cx, cy, passes):
    @jax.custom_vjp
    def f(x, y):
        if passes == 1:
            return _mxu(x.astype(BF16), y.astype(BF16), cx, cy)
        xh, xl = _split(x)
        yh, yl = _split(y)
        return _mxu(xh, yh, cx, cy) + (_mxu(xh, yl, cx, cy) + _mxu(xl, yh, cx, cy))

    def fwd(x, y):
        return f(x, y), (x, y)

    def bwd(res, g):
        x, y = res
        dx = dot3(g, y, 1, 1 - cy, passes) if cx == 1 else dot3(y, g, 1 - cy, 1, passes)
        dy = dot3(x, g, 1 - cx, 0, passes) if cy == 0 else dot3(g, x, 0, 1 - cx, passes)
        return dx, dy

    f.defvjp(fwd, bwd)
    return f


_DOT3 = {}


def dot3(x, y, cx, cy, passes=3):
    if (cx, cy, passes) not in _DOT3:
        _DOT3[(cx, cy, passes)] = _make_dot3(cx, cy, passes)
    return _DOT3[(cx, cy, passes)](x, y)


def _dot(x, y, passes=3):
    return dot3(x, y, 1, 0, passes)


def _dot_nt(x, y, passes=3):
    return dot3(x, y, 1, 1, passes)


def _dot_tn(x, y, passes=3):
    return dot3(x, y, 0, 0, passes)


def _tri_sum(x, lower):
    t = x.shape[-2]
    row = lax.broadcasted_iota(jnp.int32, (t, t), 0)
    col = lax.broadcasted_iota(jnp.int32, (t, t), 1)
    tri = jnp.where((col <= row) if lower else (col >= row), 1.0, 0.0).astype(BF16)
    if x.ndim == 3:
        tri = jnp.broadcast_to(tri[None], (x.shape[0], t, t))
    hi = x.astype(BF16)
    rest = x - hi.astype(F32)
    mid = rest.astype(BF16)
    low = (rest - mid.astype(F32)).astype(BF16)
    return _mxu(tri, hi, 1, 0) + (_mxu(tri, mid, 1, 0) + _mxu(tri, low, 1, 0))


@jax.custom_vjp
def _cumsum_rows(x):
    return _tri_sum(x, True)


_cumsum_rows.defvjp(lambda x: (_tri_sum(x, True), None), lambda _, g: (_tri_sum(g, False),))


def _scan_chunk(s0, r, w, k, v, a, b):
    t = r.shape[1]
    row = lax.broadcasted_iota(jnp.int32, (1, t, t), 1)
    col = lax.broadcasted_iota(jnp.int32, (1, t, t), 2)
    strict, incl = col < row, col <= row
    logw = jnp.log(w)
    cum = _cumsum_rows(logw)
    w_in, w_ex, w_inv = jnp.exp(cum), jnp.exp(cum - logw), jnp.exp(-cum)
    w_all = jnp.exp(jnp.sum(logw, axis=1, keepdims=True))
    at, rt, kt, bt = a * w_ex, r * w_in, k * w_inv, b * w_inv
    ps, po = SCAN_PASSES_SOLVE, SCAN_PASSES_OUT
    a_ab = jnp.where(strict, _dot_nt(at, bt, ps), 0.0)
    a_ak = jnp.where(strict, _dot_nt(at, kt, ps), 0.0)
    a_rk = jnp.where(incl, _dot_nt(rt, kt, po), 0.0)
    a_rb = jnp.where(incl, _dot_nt(rt, bt, po), 0.0)
    u = _dot_nt(at, s0, ps) + _dot(a_ak, v, ps)
    p = a_ab
    steps = int(math.log2(t))
    assert 2 ** steps == t
    for j in range(steps):
        u = u + _dot(p, u, ps)
        if j < steps - 1:
            p = _dot(p, p, ps)
    y = _dot_nt(rt, s0, po) + _dot(a_rk, v, po) + _dot(a_rb, u, po)
    s_new = s0 * w_all + _dot_tn(v, kt * w_all, po) + _dot_tn(u, bt * w_all, po)
    return y, s_new


def scan_fwd(r, w, k, v, a, b, comm=None):
    h, s, n = r.shape
    t = min(SCAN_CHUNK, s)
    nc = s // t

    hb = SCAN_HEADS if h % SCAN_HEADS == 0 else 1

    def body(r_ref, w_ref, k_ref, v_ref, a_ref, b_ref, y_ref, ck_ref, st_ref):
        @pl.when(pl.program_id(1) == 0)
        def _():
            st_ref[...] = jnp.zeros_like(st_ref)

        s0 = st_ref[...]
        ck_ref[...] = s0
        y, s_new = _scan_chunk(s0, r_ref[...], w_ref[...], k_ref[...], v_ref[...], a_ref[...], b_ref[...])
        y_ref[...] = y
        st_ref[...] = s_new

    blk = pl.BlockSpec((hb, t, n), lambda hh, c: (hh, c, 0))
    return _pallas(
        body, name="rwkv_scan_fwd", grid=(h // hb, nc), in_specs=[blk] * 6,
        out_specs=[blk, pl.BlockSpec((hb, None, n, n), lambda hh, c: (hh, c, 0, 0))],
        out_shape=[jax.ShapeDtypeStruct((h, s, n), F32), jax.ShapeDtypeStruct((h, nc, n, n), F32)],
        scratch_shapes=[pltpu.VMEM((hb, n, n), F32)], sem=("parallel", "arbitrary"), comm=comm,
    )(r, w, k, v, a, b)


def scan_bwd(r, w, k, v, a, b, ck, dy, comm=None):
    h, s, n = r.shape
    t = min(SCAN_CHUNK, s)
    nc = s // t

    hb = SCAN_HEADS if h % SCAN_HEADS == 0 else 1

    def body(r_ref, w_ref, k_ref, v_ref, a_ref, b_ref, ck_ref, dy_ref, dr_ref, dw_ref, dk_ref, dv_ref, da_ref, db_ref, ds_ref):
        @pl.when(pl.program_id(1) == 0)
        def _():
            ds_ref[...] = jnp.zeros_like(ds_ref)

        _, vjp_fn = jax.vjp(_scan_chunk, ck_ref[...], r_ref[...], w_ref[...], k_ref[...], v_ref[...], a_ref[...], b_ref[...])
        ds0, dr, dw, dk, dv, da, db = vjp_fn((dy_ref[...], ds_ref[...]))
        ds_ref[...] = ds0
        dr_ref[...], dw_ref[...], dk_ref[...], dv_ref[...], da_ref[...], db_ref[...] = dr, dw, dk, dv, da, db

    blk = pl.BlockSpec((hb, t, n), lambda hh, c: (hh, nc - 1 - c, 0))
    return _pallas(
        body, name="rwkv_scan_bwd", grid=(h // hb, nc),
        in_specs=[blk] * 6 + [pl.BlockSpec((hb, None, n, n), lambda hh, c: (hh, nc - 1 - c, 0, 0)), blk],
        out_specs=[blk] * 6, out_shape=[jax.ShapeDtypeStruct((h, s, n), F32)] * 6,
        scratch_shapes=[pltpu.VMEM((hb, n, n), F32)], sem=("parallel", "arbitrary"), comm=comm,
    )(r, w, k, v, a, b, ck, dy)


ATTN_BLOCK = 256
ATTN_LEVELS = 8
ATTN_SLAB = 2 * LANES


def _attn_specs(h, s, tq):
    qblk = lambda c, part: pl.BlockSpec((None, tq, c), lambda hh, i: (hh, i, part))
    kblk = lambda part: pl.BlockSpec((None, s, LANES), lambda hh, i: (hh, 0, part))
    row64 = pl.BlockSpec((tq, QK_ROPE), lambda hh, i: (i, 0))
    return [qblk(LANES, 0), qblk(LANES, 1), kblk(0), kblk(1), pl.BlockSpec((s, QK_ROPE), lambda hh, i: (0, 0)),
            row64, row64, pl.BlockSpec((QK_ROPE, QK_ROPE), lambda hh, i: (0, 0))]


def _attn_levels(s, tq):
    nq = s // tq
    n_lev = min(ATTN_LEVELS, nq)
    per = nq // n_lev
    return [(lv * per, (lv + 1) * per, (lv + 1) * per * tq) for lv in range(n_lev)]


def _attn_scores(qn_b, qp_b, kn_ref, kp_ref, klen, i, tq):
    scale = (QK_NOPE + QK_ROPE) ** -0.5
    kn_b = kn_ref[0:klen, :].astype(BF16)
    kp_b = kp_ref[0:klen, :].astype(BF16)
    sc = lax.dot_general(qn_b, kn_b, (((1,), (1,)), ((), ())), preferred_element_type=F32)
    sc = sc + lax.dot_general(qp_b, kp_b, (((1,), (1,)), ((), ())), preferred_element_type=F32)
    row = i * tq + lax.broadcasted_iota(jnp.int32, sc.shape, 0)
    col = lax.broadcasted_iota(jnp.int32, sc.shape, 1)
    return jnp.where(row >= col, sc * scale, NEG_INF), scale, kn_b, kp_b


def attn_fwd(q_h, kv_h, kp, cos2, sin2, rot, comm=None):
    h, s, _ = q_h.shape
    tq = _pick(s, ATTN_BLOCK, SUBLANES)

    def body(qn_ref, qp_ref, kn_ref, v_ref, kp_ref, cos_ref, sin_ref, rot_ref, o_ref, lse_ref):
        i = pl.program_id(1)
        qn_b = qn_ref[...].astype(BF16)
        qp_b = _rope(qp_ref[:, :QK_ROPE], cos_ref[...], sin_ref[...], rot_ref[...]).astype(BF16)

        def level(klen):
            sc, _, _, _ = _attn_scores(qn_b, qp_b, kn_ref, kp_ref, klen, i, tq)
            mx = jnp.max(sc, axis=-1, keepdims=True)
            e = jnp.exp(sc - mx)
            den = jnp.sum(e, axis=-1, keepdims=True)
            o_ref[...] = jnp.dot((e / den).astype(BF16), v_ref[0:klen, :].astype(BF16), preferred_element_type=F32)
            lse_ref[...] = mx + jnp.log(den)

        for lo, hi, klen in _attn_levels(s, tq):
            pl.when((i >= lo) & (i < hi))(functools.partial(level, klen))

    oblk = lambda c: pl.BlockSpec((None, tq, c), lambda hh, i: (hh, i, 0))
    return _pallas(
        body, name="mla_attn_fwd", grid=(h, s // tq), in_specs=_attn_specs(h, s, tq),
        out_specs=[oblk(V_HEAD), oblk(1)],
        out_shape=[jax.ShapeDtypeStruct((h, s, V_HEAD), F32), jax.ShapeDtypeStruct((h, s, 1), F32)],
        sem=("parallel", "parallel"), comm=comm,
    )(q_h, q_h, kv_h, kv_h, kp, cos2, sin2, rot)


def attn_bwd(q_h, kv_h, kp, cos2, sin2, rot, o, lse, do, comm=None):
    h, s, _ = q_h.shape
    tq = _pick(s, ATTN_BLOCK, SUBLANES)
    nq = s // tq

    def body(qn_ref, qp_ref, kn_ref, v_ref, kp_ref, cos_ref, sin_ref, rot_ref, o_ref, lse_ref, do_ref,
             dq_ref, dkv_ref, dkp_ref, dkv_acc, dkp_acc):
        i = pl.program_id(1)

        @pl.when(i == 0)
        def _():
            dkv_acc[...] = jnp.zeros_like(dkv_acc)
            dkp_acc[...] = jnp.zeros_like(dkp_acc)

        cosv, sinv, rotv = cos_ref[...], sin_ref[...], rot_ref[...]
        qn_b = qn_ref[...].astype(BF16)
        qp_b = _rope(qp_ref[:, :QK_ROPE], cosv, sinv, rotv).astype(BF16)
        dov = do_ref[...]
        do_b = dov.astype(BF16)
        delta = jnp.sum(dov * o_ref[...], axis=-1, keepdims=True)
        lsev = lse_ref[...]

        def level(klen):
            sc, scale, kn_b, kp_b = _attn_scores(qn_b, qp_b, kn_ref, kp_ref, klen, i, tq)
            p = jnp.exp(sc - lsev)
            dp = lax.dot_general(do_b, v_ref[0:klen, :].astype(BF16), (((1,), (1,)), ((), ())), preferred_element_type=F32)
            ds = (p * (dp - delta) * scale).astype(BF16)
            dkv_acc[0:klen, :LANES] += lax.dot_general(ds, qn_b, (((0,), (0,)), ((), ())), preferred_element_type=F32)
            dkv_acc[0:klen, LANES:] += lax.dot_general(p.astype(BF16), do_b, (((0,), (0,)), ((), ())), preferred_element_type=F32)
            dkp_acc[0:klen, :] += lax.dot_general(ds, qp_b, (((0,), (0,)), ((), ())), preferred_element_type=F32)
            dqp = jnp.dot(ds, kp_b, preferred_element_type=F32)
            dqp_raw = dqp * cosv + lax.dot_general(dqp * sinv, rotv, (((1,), (1,)), ((), ())), precision=HIGHEST,
                                                   preferred_element_type=F32)
            dq_ref[:, :QK_NOPE] = jnp.dot(ds, kn_b, preferred_element_type=F32).astype(dq_ref.dtype)
            dq_ref[:, QK_NOPE:QK_NOPE + QK_ROPE] = dqp_raw.astype(dq_ref.dtype)
            dq_ref[:, QK_NOPE + QK_ROPE:] = jnp.zeros((tq, ATTN_SLAB - QK_NOPE - QK_ROPE), dq_ref.dtype)

        for lo, hi, klen in _attn_levels(s, tq):
            pl.when((i >= lo) & (i < hi))(functools.partial(level, klen))

        @pl.when(i == nq - 1)
        def _():
            dkv_ref[...] = dkv_acc[...].astype(dkv_ref.dtype)
            dkp_ref[...] = dkp_acc[...]

    rblk = lambda c: pl.BlockSpec((None, tq, c), lambda hh, i: (hh, i, 0))
    sblk = lambda c: pl.BlockSpec((None, s, c), lambda hh, i: (hh, 0, 0))
    return _pallas(
        body, name="mla_attn_bwd", grid=(h, nq),
        in_specs=_attn_specs(h, s, tq) + [rblk(V_HEAD), rblk(1), rblk(V_HEAD)],
        out_specs=[rblk(ATTN_SLAB), sblk(ATTN_SLAB), sblk(QK_ROPE)],
        out_shape=[jax.ShapeDtypeStruct((h, s, ATTN_SLAB), BF16), jax.ShapeDtypeStruct((h, s, ATTN_SLAB), BF16),
                   jax.ShapeDtypeStruct((h, s, QK_ROPE), F32)],
        scratch_shapes=[pltpu.VMEM((s, ATTN_SLAB), F32), pltpu.VMEM((s, QK_ROPE), F32)],
        sem=("parallel", "arbitrary"), comm=comm,
    )(q_h, q_h, kv_h, kv_h, kp, cos2, sin2, rot, o, lse, do)


def _my_pos():
    return lax.axis_index("x"), lax.axis_index("y"), lax.axis_index("c")


def _dev_index(px, py, pc):
    return 4 * px + 2 * py + pc


def all_gather(name, shard):
    r, c = shard.shape

    def body(x_ref, out_ref, send_sems, recv_sems, local_sem):
        x, y, cc = _my_pos()
        me, sibling = (x, y, cc), (x, y, 1 - cc)
        chips = [(1 - x, y), (x, 1 - y), (1 - x, 1 - y)]

        def rows(px, py, pc):
            return out_ref.at[_dev_index(px, py, pc)]

        def copy(kk, block, to, src=None):
            return pltpu.make_async_remote_copy(
                src_ref=rows(*block) if src is None else src, dst_ref=rows(*block),
                send_sem=send_sems.at[kk], recv_sem=recv_sems.at[kk], device_id=to, device_id_type=MESH)

        mine = pltpu.make_async_copy(x_ref, rows(*me), local_sem)
        mine.start()
        first = [copy(0, me, sibling, src=x_ref)]
        first += [copy(1 + j, me, (*chip, cc), src=x_ref) for j, chip in enumerate(chips)]
        for cp in first:
            cp.start()
        passed = [copy(4 + j, (*chip, cc), sibling) for j, chip in enumerate(chips)]
        for j, chip in enumerate(chips):
            copy(1 + j, (*chip, cc), me).wait_recv()
            passed[j].start()
        copy(0, sibling, me).wait_recv()
        for j, chip in enumerate(chips):
            copy(4 + j, (*chip, 1 - cc), me).wait_recv()
        for cp in first + passed:
            cp.wait_send()
        mine.wait()

    return pl.pallas_call(
        body, name=name, out_shape=jax.ShapeDtypeStruct((N_DEV, r, c), shard.dtype),
        in_specs=[pl.BlockSpec(memory_space=pl.ANY)], out_specs=pl.BlockSpec(memory_space=pl.ANY),
        scratch_shapes=[pltpu.SemaphoreType.DMA((7,)), pltpu.SemaphoreType.DMA((7,)), pltpu.SemaphoreType.DMA],
    )(shard)


def exchange_sibling(name, g):
    _, r, c = g.shape

    def body(g_ref, out_ref, send_sems, recv_sems):
        x, y, cc = _my_pos()
        copies = []
        for px in range(2):
            for py in range(2):
                slot = 2 * px + py
                copies.append(pltpu.make_async_remote_copy(
                    src_ref=g_ref.at[_dev_index(px, py, 1 - cc)], dst_ref=out_ref.at[slot],
                    send_sem=send_sems.at[slot], recv_sem=recv_sems.at[slot], device_id=(x, y, 1 - cc), device_id_type=MESH))
        for cp in copies:
            cp.start()
        for cp in copies:
            cp.wait()

    return pl.pallas_call(
        body, name=name, out_shape=jax.ShapeDtypeStruct((4, r, c), g.dtype),
        in_specs=[pl.BlockSpec(memory_space=pl.ANY)], out_specs=pl.BlockSpec(memory_space=pl.ANY),
        scratch_shapes=[pltpu.SemaphoreType.DMA((4,)), pltpu.SemaphoreType.DMA((4,))],
    )(g)


def exchange_chips(name, hsum):
    _, r, c = hsum.shape

    def body(h_ref, out_ref, send_sems, recv_sems):
        x, y, cc = _my_pos()
        copies = []
        for j, (px, py) in enumerate([(1 - x, y), (x, 1 - y), (1 - x, 1 - y)]):
            copies.append(pltpu.make_async_remote_copy(
                src_ref=h_ref.at[2 * px + py], dst_ref=out_ref.at[j],
                send_sem=send_sems.at[j], recv_sem=recv_sems.at[j], device_id=(px, py, cc), device_id_type=MESH))
        for cp in copies:
            cp.start()
        for cp in copies:
            cp.wait()

    return pl.pallas_call(
        body, name=name, out_shape=jax.ShapeDtypeStruct((3, r, c), hsum.dtype),
        in_specs=[pl.BlockSpec(memory_space=pl.ANY)], out_specs=pl.BlockSpec(memory_space=pl.ANY),
        scratch_shapes=[pltpu.SemaphoreType.DMA((3,)), pltpu.SemaphoreType.DMA((3,))],
    )(hsum)


def _rs_add_sibling(name, g, from_sibling, cc):
    _, r, c = g.shape
    tr = _pick(r, 512, SUBLANES * 2)

    def body(cc_ref, g_ref, s_ref, o_ref, ob_ref):
        tot = g_ref[...] + s_ref[...]
        o_ref[...] = tot
        ob_ref[...] = tot.astype(BF16)

    blk = pl.BlockSpec((None, tr, c), lambda s_, i, cc_ref: (s_, i, 0))
    return pl.pallas_call(
        body, name=name,
        grid_spec=pltpu.PrefetchScalarGridSpec(
            num_scalar_prefetch=1, grid=(4, r // tr),
            in_specs=[pl.BlockSpec((None, None, tr, c), lambda s_, i, cc_ref: (s_, cc_ref[0], i, 0)), blk], out_specs=[blk, blk]),
        out_shape=[jax.ShapeDtypeStruct((4, r, c), F32), jax.ShapeDtypeStruct((4, r, c), BF16)],
        compiler_params=_params(("parallel", "parallel")),
    )(cc.reshape(1).astype(jnp.int32), g.reshape(4, 2, r, c), from_sibling)


def _rs_add_chips(name, chip_sum, from_chips, slot):
    _, r, c = chip_sum.shape
    tr = _pick(r, 512, SUBLANES * 2)

    def body(slot_ref, h_ref, f0_ref, f1_ref, f2_ref, o_ref):
        o_ref[...] = ((h_ref[...] + f0_ref[...].astype(F32)) + f1_ref[...].astype(F32)) + f2_ref[...].astype(F32)

    def from_blk(j):
        return pl.BlockSpec((None, tr, c), lambda i, slot_ref: (j, i, 0))

    return pl.pallas_call(
        body, name=name,
        grid_spec=pltpu.PrefetchScalarGridSpec(
            num_scalar_prefetch=1, grid=(r // tr,),
            in_specs=[pl.BlockSpec((None, tr, c), lambda i, slot_ref: (slot_ref[0], i, 0)), from_blk(0), from_blk(1), from_blk(2)],
            out_specs=pl.BlockSpec((tr, c), lambda i, slot_ref: (i, 0))),
        out_shape=jax.ShapeDtypeStruct((r, c), F32), compiler_params=_params(("parallel",)),
    )(slot.reshape(1).astype(jnp.int32), chip_sum, from_chips, from_chips, from_chips)


def adamw_scatter(name, w, m, v, chip_sum, from_chips, slot):
    _, a, b = w.shape
    _, r, c = chip_sum.shape
    assert a <= r and b <= c, (name, w.shape, chip_sum.shape)
    tr = _pick(a, 256, SUBLANES * 2)

    def body(slot_ref, h_ref, f0_ref, f1_ref, f2_ref, w_ref, m_ref, v_ref, g_out, d_out, m_out, v_out):
        g = ((h_ref[...] + f0_ref[...].astype(F32)) + f1_ref[...].astype(F32)) + f2_ref[...].astype(F32)
        g = g[:, :b]
        g_out[...] = g
        d_out[...], m_out[...], v_out[...] = _adamw_fn(w_ref[...], g, m_ref[...], v_ref[...])

    def from_blk(j):
        return pl.BlockSpec((None, tr, c), lambda i, slot_ref: (j, i, 0))

    mine = pl.BlockSpec((None, tr, b), lambda i, slot_ref: (0, i, 0))
    return pl.pallas_call(
        body, name=name,
        grid_spec=pltpu.PrefetchScalarGridSpec(
            num_scalar_prefetch=1, grid=(a // tr,),
            in_specs=[pl.BlockSpec((None, tr, c), lambda i, slot_ref: (slot_ref[0], i, 0)), from_blk(0), from_blk(1), from_blk(2),
                      mine, mine, mine],
            out_specs=[mine] * 4),
        out_shape=[jax.ShapeDtypeStruct((1, a, b), F32)] * 4, compiler_params=_params(("parallel",)),
    )(slot.reshape(1).astype(jnp.int32), chip_sum, from_chips, from_chips, from_chips, w, m, v)


def rs_chip_sum(tag, g):
    _, _, cc = _my_pos()
    from_sibling = exchange_sibling("rs_sibling_" + tag, g)
    return _rs_add_sibling("rs_add_sibling_" + tag, g, from_sibling, cc)


def rs_finish(tag, chip_sum, from_chips):
    x, y, _ = _my_pos()
    return _rs_add_chips("rs_add_chips_" + tag, chip_sum, from_chips, 2 * x + y)


def reduce_scatter(tag, g):
    chip_sum, chip_sum_b = rs_chip_sum(tag, g)
    return rs_finish(tag, chip_sum, exchange_chips("rs_chips_" + tag, chip_sum_b))


class GatherIci:
    def __init__(self, shards):
        self.inputs = list(shards)
        self.out_shapes = [jax.ShapeDtypeStruct((N_DEV,) + s.shape, s.dtype) for s in shards]
        self.n_remote, self.n_local = 3 * len(shards), len(shards)

    def make(self, cins, couts, send, recv, local):
        x, y, cc = _my_pos()
        me = _dev_index(x, y, cc)
        copies = []
        for w, (src, out) in enumerate(zip(cins, couts, strict=True)):
            copies.append(pltpu.make_async_copy(src, out.at[me], local.at[w]))
            for j, (px, py) in enumerate([(1 - x, y), (x, 1 - y), (1 - x, 1 - y)]):
                copies.append(pltpu.make_async_remote_copy(
                    src_ref=src, dst_ref=out.at[me], send_sem=send.at[3 * w + j], recv_sem=recv.at[3 * w + j],
                    device_id=(px, py, cc), device_id_type=MESH))
        return copies


class RsChips:
    def __init__(self, chip_sums):
        self.inputs = list(chip_sums)
        self.out_shapes = [jax.ShapeDtypeStruct((3,) + h.shape[1:], h.dtype) for h in chip_sums]
        self.n_remote, self.n_local = 3 * len(chip_sums), 0

    def make(self, cins, couts, send, recv, local):
        x, y, cc = _my_pos()
        copies = []
        for w, (h_ref, out) in enumerate(zip(cins, couts, strict=True)):
            for j, (px, py) in enumerate([(1 - x, y), (x, 1 - y), (1 - x, 1 - y)]):
                copies.append(pltpu.make_async_remote_copy(
                    src_ref=h_ref.at[2 * px + py], dst_ref=out.at[j], send_sem=send.at[3 * w + j], recv_sem=recv.at[3 * w + j],
                    device_id=(px, py, cc), device_id_type=MESH))
        return copies


class SiblingSwap:
    def __init__(self, gs):
        self.inputs = list(gs)
        self.out_shapes = [jax.ShapeDtypeStruct((4,) + g.shape[1:], g.dtype) for g in gs]
        self.n_remote, self.n_local = 4 * len(gs), 0

    def make(self, cins, couts, send, recv, local):
        x, y, cc = _my_pos()
        copies = []
        for w, (g_ref, out) in enumerate(zip(cins, couts, strict=True)):
            for px in range(2):
                for py in range(2):
                    q = 4 * w + 2 * px + py
                    copies.append(pltpu.make_async_remote_copy(
                        src_ref=g_ref.at[_dev_index(px, py, 1 - cc)], dst_ref=out.at[2 * px + py],
                        send_sem=send.at[q], recv_sem=recv.at[q], device_id=(x, y, 1 - cc), device_id_type=MESH))
        return copies


class GatherD2D:
    def __init__(self, arrays):
        self.inputs = list(arrays)
        self.out_shapes = [jax.ShapeDtypeStruct(a.shape, a.dtype) for a in arrays]
        self.n_remote, self.n_local = 4 * len(arrays), 0
        self.aliases = [(i, i) for i in range(len(arrays))]

    def make(self, cins, couts, send, recv, local):
        x, y, cc = _my_pos()
        copies = []
        for w, out in enumerate(couts):
            for px in range(2):
                for py in range(2):
                    q = 4 * w + 2 * px + py
                    slab = out.at[_dev_index(px, py, cc)]
                    copies.append(pltpu.make_async_remote_copy(
                        src_ref=slab, dst_ref=slab, send_sem=send.at[q], recv_sem=recv.at[q],
                        device_id=(x, y, 1 - cc), device_id_type=MESH))
        return copies


class _SemSlice:
    def __init__(self, base, start):
        self.base, self.start = base, start

    @property
    def at(self):
        return self

    def __getitem__(self, k):
        return self.base.at[self.start + k]


class CommGroup:
    def __init__(self, plans):
        self.plans = [p for p in plans if p.inputs]
        self.inputs = [a for p in self.plans for a in p.inputs]
        self.out_shapes = [s_ for p in self.plans for s_ in p.out_shapes]
        self.n_remote = sum(p.n_remote for p in self.plans)
        self.n_local = sum(p.n_local for p in self.plans)
        self.aliases, i0, o0 = [], 0, 0
        for p in self.plans:
            self.aliases += [(i0 + i, o0 + j) for i, j in getattr(p, 'aliases', [])]
            i0, o0 = i0 + len(p.inputs), o0 + len(p.out_shapes)

    def make(self, cins, couts, send, recv, local):
        copies, i0, o0, r0, l0 = [], 0, 0, 0, 0
        for p in self.plans:
            ni, no = len(p.inputs), len(p.out_shapes)
            copies += p.make(cins[i0:i0 + ni], couts[o0:o0 + no], _SemSlice(send, r0), _SemSlice(recv, r0), _SemSlice(local, l0))
            i0, o0, r0, l0 = i0 + ni, o0 + no, r0 + p.n_remote, l0 + p.n_local
        return copies


def _pallas(body, *, name, grid, in_specs, out_specs, out_shape, scratch_shapes=(), sem, comm=None):
    in_specs, out_specs, out_shape, scratch_shapes = list(in_specs), list(out_specs), list(out_shape), list(scratch_shapes)
    if comm is None:
        return pl.pallas_call(body, name=name, grid=grid, in_specs=in_specs, out_specs=out_specs, out_shape=out_shape,
                              scratch_shapes=scratch_shapes, compiler_params=_params(sem))
    n_in, n_out, n_scr = len(in_specs), len(out_specs), len(scratch_shapes)
    nci, nco = len(comm.inputs), len(comm.out_shapes)

    def body2(*refs):
        ins, cins = refs[:n_in], refs[n_in:n_in + nci]
        o0 = n_in + nci
        outs, couts = refs[o0:o0 + n_out], refs[o0 + n_out:o0 + n_out + nco]
        s0 = o0 + n_out + nco
        scr = refs[s0:s0 + n_scr]
        send, recv, local = refs[s0 + n_scr:]
        pids = [pl.program_id(k) for k in range(len(grid))]
        first = functools.reduce(jnp.logical_and, [p == 0 for p in pids])
        last = functools.reduce(jnp.logical_and, [p == g - 1 for p, g in zip(pids, grid)])

        @pl.when(first)
        def _():
            for cp in comm.make(cins, couts, send, recv, local):
                cp.start()

        body(*ins, *outs, *scr)

        @pl.when(last)
        def _():
            for cp in comm.make(cins, couts, send, recv, local):
                cp.wait()

    any_spec = pl.BlockSpec(memory_space=pl.ANY)
    call = pl.pallas_call(
        body2, name=name, grid=grid, in_specs=in_specs + [any_spec] * nci, out_specs=out_specs + [any_spec] * nco,
        out_shape=out_shape + list(comm.out_shapes),
        input_output_aliases={n_in + i: n_out + j for i, j in getattr(comm, 'aliases', [])},
        scratch_shapes=scratch_shapes + [pltpu.SemaphoreType.DMA((comm.n_remote,)), pltpu.SemaphoreType.DMA((comm.n_remote,)),
                                         pltpu.SemaphoreType.DMA((max(comm.n_local, 1),))],
        compiler_params=_params(tuple("arbitrary" for _ in grid)))
    return lambda *args: call(*args, *comm.inputs)


PACK_W = 1024


class Pack:
    def __init__(self, entries, row_unit):
        self.entries = entries
        self.sizes = [int(np.prod(sh)) for _, sh in entries]
        self.offsets = np.concatenate([[0], np.cumsum(self.sizes)]).tolist()
        self.total = _round_up(self.offsets[-1], PACK_W * row_unit)
        self.rows = self.total // PACK_W

    def pack(self, arrays, dtype, lead=()):
        flat = [arrays[n].astype(dtype).reshape(lead + (-1,)) for n, _ in self.entries]
        pad = self.total - self.offsets[-1]
        if pad:
            flat.append(jnp.zeros(lead + (pad,), dtype))
        return jnp.concatenate(flat, axis=-1).reshape(lead + (self.rows, PACK_W))

    def unpack(self, buf, lead=()):
        flat = buf.reshape(lead + (self.total,))
        out = {}
        for (n, sh), off, sz in zip(self.entries, self.offsets, self.sizes):
            out[n] = lax.slice_in_dim(flat, off, off + sz, axis=len(lead)).reshape(lead + tuple(sh))
        return out


def _gathered_to_full(g, how):
    _, a, b = g.shape
    if how == 'row':
        return g.reshape(N_DEV * a, b)
    return jnp.transpose(g, (1, 0, 2)).reshape(a, N_DEV * b)


def _full_to_shards(w, how):
    a, b = w.shape
    if how == 'row':
        return w.reshape(N_DEV, a // N_DEV, b)
    return jnp.transpose(w.reshape(a, N_DEV, b // N_DEV), (1, 0, 2))


def _to_heads(t, width):
    s, c = t.shape
    return jnp.transpose(t.reshape(s, c // width, width), (1, 0, 2))


def _from_heads(t):
    h, s, w = t.shape
    return jnp.transpose(t, (1, 0, 2)).reshape(s, h * w)


def _rot_matrix():
    half = QK_ROPE // 2
    rot = np.zeros((QK_ROPE, QK_ROPE), np.float32)
    for i in range(half):
        rot[i + half, i] = -1.0
        rot[i, i + half] = 1.0
    return jnp.asarray(rot)


def _inv_freq2():
    half = QK_ROPE // 2
    inv = ROPE_THETA ** (-np.arange(half, dtype=np.float32) / half)
    return jnp.asarray(np.concatenate([inv, inv])[None, :].astype(np.float32))


def kernel(x, positions, attn_norm_g, w_in, rwkv_mu, rwkv_w0, rwkv_w2, rwkv_a0, rwkv_a2, rwkv_g2, rwkv_k_k, rwkv_k_a, rwkv_r_k, rwkv_gn_w, rwkv_gn_b, mla_q_norm_g, mla_w_uq, mla_kv_norm_g, mla_w_ukv, w_out, ffn_norm_g, ffn_w_gate, ffn_w_up, ffn_conv_w, ffn_conv_b, ffn_w_down, final_norm_g, loss_target, m_attn_norm_g, m_w_in, m_rwkv_mu, m_rwkv_w0, m_rwkv_w2, m_rwkv_a0, m_rwkv_a2, m_rwkv_g2, m_rwkv_k_k, m_rwkv_k_a, m_rwkv_r_k, m_rwkv_gn_w, m_rwkv_gn_b, m_mla_q_norm_g, m_mla_w_uq, m_mla_kv_norm_g, m_mla_w_ukv, m_w_out, m_ffn_norm_g, m_ffn_w_gate, m_ffn_w_up, m_ffn_conv_w, m_ffn_conv_b, m_ffn_w_down, m_final_norm_g, v_attn_norm_g, v_w_in, v_rwkv_mu, v_rwkv_w0, v_rwkv_w2, v_rwkv_a0, v_rwkv_a2, v_rwkv_g2, v_rwkv_k_k, v_rwkv_k_a, v_rwkv_r_k, v_rwkv_gn_w, v_rwkv_gn_b, v_mla_q_norm_g, v_mla_w_uq, v_mla_kv_norm_g, v_mla_w_ukv, v_w_out, v_ffn_norm_g, v_ffn_w_gate, v_ffn_w_up, v_ffn_conv_w, v_ffn_conv_b, v_ffn_w_down, v_final_norm_g):
    given = dict(locals())
    wts = {n: given[n] for n in WEIGHTS}
    mom_m = {n: given["m_" + n] for n in WEIGHTS}
    mom_v = {n: given["v_" + n] for n in WEIGHTS}
    out_shapes = {n: wts[n].shape for n in WEIGHTS}

    def local2d(n, a):
        if n == 'rwkv_r_k' or a.ndim <= 2:
            return a.reshape(1, -1)
        return a.reshape(a.shape[1:])

    w2d = {n: local2d(n, wts[n]) for n in WEIGHTS}
    m2d = {n: local2d(n, mom_m[n]) for n in WEIGHTS}
    v2d = {n: local2d(n, mom_v[n]) for n in WEIGHTS}

    xs = x.reshape(x.shape[1:])
    tgt = loss_target.reshape(loss_target.shape[1:])
    s, d = xs.shape
    c_rwkv = w2d['rwkv_w0'].shape[1]
    n_rh = c_rwkv // RWKV_HEAD
    decay_lora, aaa_lora, gate_lora = w2d['rwkv_w2'].shape[0], w2d['rwkv_a2'].shape[0], w2d['rwkv_g2'].shape[0]
    q_lora, kv_lora = w2d['mla_q_norm_g'].shape[1], w2d['mla_kv_norm_g'].shape[1]
    shift_dim = w2d['rwkv_mu'].shape[1]
    d_in = w2d['w_in'].shape[1] * N_DEV
    n_mh = w2d['mla_w_uq'].shape[1] * N_DEV // (QK_NOPE + QK_ROPE)
    tm = _pick(s, 256, SUBLANES)
    tm_wide = _pick(s, 128, SUBLANES)
    tm_heads = _pick(s, 512, SUBLANES)

    nb = {n: w2d[n].shape[1] for n in BIG if BIG[n] == 'col'}
    nbp = {n: _round_up(v_, LANES) for n, v_ in nb.items()}
    shards = {}
    for n in BIG:
        w = w2d[n].astype(BF16)
        if BIG[n] == 'col':
            w = jnp.pad(w, ((0, 0), (0, nbp[n] - nb[n])))
        elif n == 'ffn_w_down':
            w = jnp.pad(w, ((0, nbp['ffn_w_gate'] - w.shape[0]), (0, 0)))
        shards[n] = w

    def as_used(n, g):
        return g if BIG[n] == 'col' else g.reshape(N_DEV * g.shape[1], g.shape[2])

    gathered = {'w_in': as_used('w_in', all_gather("gather_w_in", shards['w_in']))}
    f_pad = N_DEV * nbp['ffn_w_gate']
    small_pack = Pack([(n, w2d[n].shape) for n in SMALL_SHARDED], 8)
    small_all = all_gather("gather_small", small_pack.pack(w2d, F32))
    full = {}
    for n, g in small_pack.unpack(small_all, lead=(N_DEV,)).items():
        full[n] = _gathered_to_full(g, SMALL_SHARDED[n])
    conv_w_pad = pad_cols(full['ffn_conv_w'], nb['ffn_w_gate'], nbp['ffn_w_gate'])
    conv_b_pad = pad_cols(w2d['ffn_conv_b'], nb['ffn_w_gate'], nbp['ffn_w_gate'])

    (h1,) = rowwise("rms_attn", _rms_fn, [xs, w2d['attn_norm_g']], ['row', 'const'], [('row', d, BF16)], heads=1, s=s, tm=tm)
    landed = {}

    def gather_behind(run, ici=(), d2d=()):
        *res, = run(CommGroup([GatherIci([shards[n] for n in ici]), GatherD2D([landed[n] for n in d2d])]))
        n_own = len(res) - len(ici) - len(d2d)
        landed.update(zip(ici, res[n_own:n_own + len(ici)], strict=True))
        for n, g in zip(d2d, res[n_own + len(ici):], strict=True):
            gathered[n] = as_used(n, g)
        return res[:n_own]

    w_in_nat = jnp.transpose(gathered['w_in'], (1, 0, 2))[:, :, :nb['w_in']].reshape(d, d_in)
    w_in_nat = jnp.pad(w_in_nat, ((0, 0), (0, _round_up(d_in, LANES) - d_in)))
    (proj,) = gather_behind(lambda c: mm("proj_in", h1, w_in_nat, comm=c), ici=['mla_w_uq', 'mla_w_ukv', 'w_out'])
    p_rwkv = proj[:, :shift_dim]
    c_q = proj[:, shift_dim:shift_dim + q_lora]
    c_kv = proj[:, shift_dim + q_lora:shift_dim + q_lora + kv_lora]
    k_pe = proj[:, shift_dim + q_lora + kv_lora:d_in]
    shifted = token_shift_fwd(p_rwkv, w2d['rwkv_mu'], tm_wide)
    o1, o2, o3 = c_rwkv, 2 * c_rwkv, 3 * c_rwkv
    hr = _to_heads(shifted[:, :o1], RWKV_HEAD)
    hk = _to_heads(shifted[:, o1:o2], RWKV_HEAD)
    hv = _to_heads(shifted[:, o2:o3], RWKV_HEAD)
    hw = shifted[:, o3:o3 + decay_lora]
    ha = shifted[:, o3 + decay_lora:o3 + decay_lora + aaa_lora]
    hg = shifted[:, o3 + decay_lora + aaa_lora:]

    def per_head(vec):
        return vec.reshape(n_rh, 1, RWKV_HEAD)

    def lora_heads(w):
        return jnp.transpose(w.reshape(w.shape[0], n_rh, RWKV_HEAD), (1, 0, 2))

    pre_args = [hk, hw, ha, hg, per_head(w2d['rwkv_w0']), lora_heads(full['rwkv_w2']), per_head(w2d['rwkv_a0']),
                lora_heads(full['rwkv_a2']), lora_heads(full['rwkv_g2']), per_head(w2d['rwkv_k_k']), per_head(w2d['rwkv_k_a'])]
    pre_kinds = ['hrow', 'row', 'row', 'row', 'hconst', 'hconst', 'hconst', 'hconst', 'hconst', 'hconst', 'hconst']
    decay, kx, a_sc, b_sc, gate_r = gather_behind(
        lambda c: rowwise("rwkv_pre", _rwkv_pre_fn, pre_args, pre_kinds, [('hrow', RWKV_HEAD, F32)] * 5, heads=n_rh, s=s,
                          tm=tm_heads, comm=c),
        ici=['ffn_w_gate'], d2d=['mla_w_uq', 'mla_w_ukv', 'w_out'])
    y_scan, ckpt = gather_behind(lambda c: scan_fwd(hr, decay, kx, hv, a_sc, b_sc, comm=c), ici=['ffn_w_up'], d2d=['ffn_w_gate'])
    post_args = [y_scan, hr, kx, hv, gate_r, per_head(w2d['rwkv_gn_w']), per_head(w2d['rwkv_gn_b']), per_head(w2d['rwkv_r_k'])]
    post_kinds = ['hrow'] * 5 + ['hconst'] * 3
    (y_rwkv_h,) = rowwise("rwkv_post", _rwkv_post_fn, post_args, post_kinds, [('hrow', RWKV_HEAD, F32)], heads=n_rh, s=s, tm=tm_heads)

    pos = positions.reshape(s, 1).astype(F32)
    rot, inv2 = _rot_matrix(), _inv_freq2()
    mla_args = [c_q, c_kv, k_pe, pos, w2d['mla_q_norm_g'], w2d['mla_kv_norm_g'], inv2, rot]
    mla_kinds = ['row', 'row', 'row', 'row', 'const', 'const', 'const', 'const']
    qn, kvn, kp_rot, cos2, sin2 = rowwise(
        "mla_pre", _mla_pre_fn, mla_args, mla_kinds,
        [('row', q_lora, BF16), ('row', kv_lora, BF16), ('row', QK_ROPE, F32), ('row', QK_ROPE, F32), ('row', QK_ROPE, F32)],
        heads=1, s=s, tm=tm)
    assert n_mh == N_DEV and nb['mla_w_uq'] == QK_NOPE + QK_ROPE and nb['mla_w_ukv'] == QK_NOPE + V_HEAD
    assert nbp['mla_w_uq'] == ATTN_SLAB and nbp['mla_w_ukv'] == ATTN_SLAB
    q_h = mm_sh("proj_q", qn, gathered['mla_w_uq'], slabs=True)
    kv_h = mm_sh("proj_kv", kvn, gathered['mla_w_ukv'], slabs=True)
    o_att, lse = gather_behind(lambda c: attn_fwd(q_h, kv_h, kp_rot, cos2, sin2, rot, comm=c), ici=['ffn_w_down'], d2d=['ffn_w_up'])
    ycat = jnp.concatenate([_from_heads(y_rwkv_h), _from_heads(o_att)], axis=-1).astype(BF16)
    (x1,) = gather_behind(lambda c: mm("proj_out", ycat, gathered['w_out'], add=xs, comm=c), d2d=['ffn_w_down'])
    (h2,) = rowwise("rms_ffn", _rms_fn, [x1, w2d['ffn_norm_g']], ['row', 'const'], [('row', d, BF16)], heads=1, s=s, tm=tm)
    gate_pre = mm_sh("ffn_gate", h2, gathered['ffn_w_gate'])
    up = mm_sh("ffn_up", h2, gathered['ffn_w_up'])
    act = ffn_act_fwd(gate_pre, up, conv_w_pad, conv_b_pad)
    x2 = mm("ffn_down", act, gathered['ffn_w_down'], add=x1)

    ones = jnp.ones((s, 1), F32)
    fin_g = w2d['final_norm_g']
    d_x2, dg_final_p, loss_rows = rowwise_vjp("loss_bwd", _loss_fn, [x2, fin_g, tgt], ['row', 'const', 'row'], [ones], ['row'],
                                              [0, 1], heads=1, s=s, tm=tm, primal=True)
    d_x2_b = d_x2.astype(BF16)
    d_act = mm_nt("d_act", d_x2_b, gathered['ffn_w_down'], out_dtype=BF16)
    gsh, chip_sums, from_chips = {}, {}, {}

    def scatter_behind(run, ici=(), swap=()):
        *res, = run(CommGroup([RsChips([chip_sums[n][1] for n in ici]), SiblingSwap([gsh[n] for n in swap])]))
        n_own = len(res) - len(ici) - len(swap)
        from_chips.update(zip(ici, res[n_own:n_own + len(ici)], strict=True))
        _, _, cc = _my_pos()
        for n, from_sibling in zip(swap, res[n_own + len(ici):], strict=True):
            chip_sums[n] = _rs_add_sibling("rs_add_sibling_" + n, gsh[n], from_sibling, cc)
        return res[:n_own]

    gsh['ffn_w_down'] = mm_tn("dw_down", act, d_x2_b).reshape(N_DEV, nbp['ffn_w_gate'], d)
    d_gate, d_up, dcw_p, dcb_p = ffn_act_bwd1(gate_pre, up, conv_w_pad, conv_b_pad, d_act)
    d_gp = ffn_act_bwd2(d_gate, conv_w_pad)
    (d_h2_g,) = scatter_behind(lambda c: mm_sh_nt("d_h2_gate", d_gp, gathered['ffn_w_gate'], comm=c), swap=['ffn_w_down'])
    d_h2 = mm_sh_nt("d_h2_up", d_up, gathered['ffn_w_up'], add=d_h2_g)
    gsh['ffn_w_gate'] = mm_sh_out("dw_gate", h2, d_gp)
    (gsh['ffn_w_up'],) = scatter_behind(lambda c: mm_sh_out("dw_up", h2, d_up, comm=c), swap=['ffn_w_gate'])
    d_x1, dg_ffn_p = rowwise_vjp("rms_ffn_bwd", _rms_fn, [x1, w2d['ffn_norm_g']], ['row', 'const'], [d_h2], ['row'], [0, 1],
                                 heads=1, s=s, tm=tm, plus=d_x2)
    d_x1_b = d_x1.astype(BF16)
    d_ycat = mm_nt("d_ycat", d_x1_b, gathered['w_out'])
    gsh['w_out'] = mm_tn("dw_out", ycat, d_x1_b).reshape((N_DEV,) + w2d['w_out'].shape)
    d_yr_h = _to_heads(d_ycat[:, :c_rwkv], RWKV_HEAD)
    d_o_h = _to_heads(d_ycat[:, c_rwkv:], V_HEAD)

    d_q, d_kv, d_kp_h = scatter_behind(
        lambda c: attn_bwd(q_h, kv_h, kp_rot, cos2, sin2, rot, o_att, lse, d_o_h, comm=c),
        ici=['ffn_w_down'], swap=['ffn_w_up', 'w_out'])
    d_kp_rot = headsum("d_kpe_heads", d_kp_h)
    d_qn = mm_sh_nt("d_qn", d_q, gathered['mla_w_uq'])
    d_kvn = mm_sh_nt("d_kvn", d_kv, gathered['mla_w_ukv'])
    gsh['mla_w_uq'] = mm_sh_out("dw_uq", qn, d_q)
    gsh['mla_w_ukv'] = mm_sh_out("dw_ukv", kvn, d_kv)
    d_cq, d_ckv, d_kpe, dg_q_p, dg_kv_p = rowwise_vjp(
        "mla_pre_bwd", _mla_pre_grad_fn, mla_args, mla_kinds, [d_qn, d_kvn, d_kp_rot], ['row', 'row', 'row'], [0, 1, 2, 4, 5],
        heads=1, s=s, tm=tm)

    d_y, d_r_post, d_k_post, d_v_post, d_gate_r, dgnw_p, dgnb_p, drk_p = scatter_behind(
        lambda c: rowwise_vjp("rwkv_post_bwd", _rwkv_post_fn, post_args, post_kinds, [d_yr_h], ['hrow'], list(range(8)),
                              heads=n_rh, s=s, tm=tm_heads, comm=c),
        ici=['ffn_w_gate'], swap=['mla_w_uq', 'mla_w_ukv'])
    d_r_sc, d_w_sc, d_k_sc, d_v_sc, d_a_sc, d_b_sc = scatter_behind(
        lambda c: scan_bwd(hr, decay, kx, hv, a_sc, b_sc, ckpt, d_y, comm=c), ici=['ffn_w_up', 'w_out'])
    d_hk, d_hw_p, d_ha_p, d_hg_p, dw0_p, dw2_p, da0_p, da2_p, dg2_p, dkk_p, dka_p, d_hr, d_hv = scatter_behind(
        lambda c: rowwise_vjp(
            "rwkv_pre_bwd", _rwkv_pre_grad_fn, pre_args + [hr, hv], pre_kinds + ['hrow', 'hrow'],
            [d_w_sc, d_k_sc, d_k_post, d_a_sc, d_b_sc, d_gate_r, d_r_sc, d_r_post, d_v_sc, d_v_post], ['hrow'] * 10,
            list(range(13)), heads=n_rh, s=s, tm=tm_heads, comm=c),
        ici=['mla_w_uq', 'mla_w_ukv'])
    d_shifted = jnp.concatenate([_from_heads(d_hr), _from_heads(d_hk), _from_heads(d_hv), headsum("d_hw_heads", d_hw_p),
                                 headsum("d_ha_heads", d_ha_p), headsum("d_hg_heads", d_hg_p)], axis=-1)
    d_p_rwkv, dmu_p = token_shift_bwd(p_rwkv, w2d['rwkv_mu'], d_shifted, tm_wide)
    d_proj = pad_cols(jnp.concatenate([d_p_rwkv, d_cq, d_ckv, d_kpe], axis=-1).astype(BF16), nb['w_in'], nbp['w_in'])
    gsh['w_in'] = mm_sh_out("dw_in", h1, d_proj)
    chip_sums['w_in'] = rs_chip_sum('w_in', gsh['w_in'])
    (d_h1,) = scatter_behind(lambda c: mm_sh_nt("d_h1", d_proj, gathered['w_in'], comm=c), ici=['w_in'])
    grad_x, dg_attn_p = rowwise_vjp("rms_attn_bwd", _rms_fn, [xs, w2d['attn_norm_g']], ['row', 'const'], [d_h1], ['row'], [0, 1],
                                    heads=1, s=s, tm=tm, plus=d_x1)

    def from_heads_lora(g):
        return jnp.transpose(g, (1, 0, 2)).reshape(g.shape[1], n_rh * RWKV_HEAD)

    gw = {}
    gw['rwkv_w2'] = from_heads_lora(sum_partials("sum_dw2", dw2_p, True))
    gw['rwkv_a2'] = from_heads_lora(sum_partials("sum_da2", da2_p, True))
    gw['rwkv_g2'] = from_heads_lora(sum_partials("sum_dg2", dg2_p, True))
    dcw_pad = colsum("sum_dconv_w", dcw_p.reshape(dcw_p.shape[0], CONV_W * f_pad)).reshape(CONV_W, f_pad)
    gw['ffn_conv_w'] = unpad_cols(dcw_pad, nb['ffn_w_gate'], nbp['ffn_w_gate'])

    rep = {
        'attn_norm_g': sum_partials("sum_dg_attn", dg_attn_p, False),
        'rwkv_mu': colsum("sum_dmu", dmu_p.reshape(dmu_p.shape[0], shift_dim)),
        'rwkv_w0': sum_partials("sum_dw0", dw0_p, True).reshape(1, c_rwkv),
        'rwkv_a0': sum_partials("sum_da0", da0_p, True).reshape(1, c_rwkv),
        'rwkv_k_k': sum_partials("sum_dkk", dkk_p, True).reshape(1, c_rwkv),
        'rwkv_k_a': sum_partials("sum_dka", dka_p, True).reshape(1, c_rwkv),
        'rwkv_r_k': sum_partials("sum_drk", drk_p, True).reshape(1, c_rwkv),
        'rwkv_gn_w': sum_partials("sum_dgnw", dgnw_p, True).reshape(1, c_rwkv),
        'rwkv_gn_b': sum_partials("sum_dgnb", dgnb_p, True).reshape(1, c_rwkv),
        'mla_q_norm_g': sum_partials("sum_dg_q", dg_q_p, False),
        'mla_kv_norm_g': sum_partials("sum_dg_kv", dg_kv_p, False),
        'ffn_norm_g': sum_partials("sum_dg_ffn", dg_ffn_p, False),
        'ffn_conv_b': unpad_cols(colsum("sum_dconv_b", dcb_p.reshape(dcb_p.shape[0], f_pad)), nb['ffn_w_gate'], nbp['ffn_w_gate']),
        'final_norm_g': sum_partials("sum_dg_final", dg_final_p, False),
        'loss': sum_all("sum_loss", loss_rows.reshape(s // SUBLANES, SUBLANES)),
    }
    rep_pack = Pack([(n, w2d[n].shape) for n in REPLICATED] + [('loss', (1, 1))], 8)
    rep_all = all_gather("gather_rep_grads", rep_pack.pack(rep, F32))
    rep_sum = colsum("sum_rep_grads", rep_all.reshape(N_DEV, rep_pack.total)).reshape(rep_pack.rows, PACK_W)
    rep_g = rep_pack.unpack(rep_sum)
    loss = rep_g.pop('loss').reshape(())

    grads, deltas, new_m, new_v = dict(rep_g), {}, {}, {}
    my_x, my_y, _ = _my_pos()
    for n in BIG:
        grads[n], deltas[n], new_m[n], new_v[n] = adamw_scatter(
            "adamw_" + n, wts[n], mom_m[n], mom_v[n], chip_sums[n][0], from_chips[n], 2 * my_x + my_y)
    sm_pack = Pack([(n, w2d[n].shape) for n in SMALL_SHARDED], 8)
    g_shards = {n: _full_to_shards(gw[n], SMALL_SHARDED[n]) for n in SMALL_SHARDED}
    grads.update(sm_pack.unpack(reduce_scatter("small", sm_pack.pack(g_shards, F32, lead=(N_DEV,)))))
    rest_pack = Pack([(n, w2d[n].shape) for n in WEIGHTS if n not in BIG], 8)
    d_r, m_r, v_r = rowwise(
        "adamw_small", _adamw_fn, [rest_pack.pack(w2d, F32), rest_pack.pack(grads, F32), rest_pack.pack(m2d, F32),
                                   rest_pack.pack(v2d, F32)],
        ['row'] * 4, [('row', PACK_W, F32)] * 3, heads=1, s=rest_pack.rows, tm=_pick(rest_pack.rows, 512, SUBLANES))
    deltas.update(rest_pack.unpack(d_r))
    new_m.update(rest_pack.unpack(m_r))
    new_v.update(rest_pack.unpack(v_r))

    def shaped(dct):
        return [dct[n].reshape(out_shapes[n]) for n in WEIGHTS]

    return (loss, grad_x.reshape(x.shape), *shaped(grads), *shaped(deltas), *shaped(new_m), *shaped(new_v))
```

```python
import functools
import math

import jax
import jax.numpy as jnp
import numpy as np
from jax import lax
from jax.experimental import pallas as pl
from jax.experimental.pallas import tpu as pltpu

F32 = jnp.float32
BF16 = jnp.bfloat16
HIGHEST = lax.Precision.HIGHEST
MESH = pl.DeviceIdType.MESH

N_DEV = 8
LANES = 128
SUBLANES = 8
VMEM_LIMIT = 48 * 1024 * 1024
RESIDENT_BYTES = 8 * 1024 * 1024

NORM_EPS = 1e-6
GN_EPS = 64e-5
RWKV_HEAD = 64
QK_NOPE = 128
QK_ROPE = 64
V_HEAD = 128
ROPE_THETA = 10000.0
CONV_W = 3
NEG_INF = -1e30
SCAN_CHUNK = 64
SCAN_HEADS = 16
SCAN_PASSES_SOLVE = 1
SCAN_PASSES_OUT = 1

ADAM_LR = 0.001
ADAM_B1 = 0.9
ADAM_B2 = 0.999
ADAM_EPS = 1e-08
ADAM_WD = 0.01
ADAM_STEP = 10

WEIGHTS = ['attn_norm_g', 'w_in', 'rwkv_mu', 'rwkv_w0', 'rwkv_w2', 'rwkv_a0', 'rwkv_a2', 'rwkv_g2', 'rwkv_k_k',
           'rwkv_k_a', 'rwkv_r_k', 'rwkv_gn_w', 'rwkv_gn_b', 'mla_q_norm_g', 'mla_w_uq', 'mla_kv_norm_g', 'mla_w_ukv',
           'w_out', 'ffn_norm_g', 'ffn_w_gate', 'ffn_w_up', 'ffn_conv_w', 'ffn_conv_b', 'ffn_w_down', 'final_norm_g']
BIG = {'w_in': 'col', 'mla_w_uq': 'col', 'mla_w_ukv': 'col', 'w_out': 'row', 'ffn_w_gate': 'col', 'ffn_w_up': 'col',
       'ffn_w_down': 'row'}
SMALL_SHARDED = {'rwkv_w2': 'col', 'rwkv_a2': 'col', 'rwkv_g2': 'col', 'ffn_conv_w': 'col'}
SHARDED = {**BIG, **SMALL_SHARDED}
REPLICATED = [n for n in WEIGHTS if n not in SHARDED]


def _round_up(n, m):
    return (n + m - 1) // m * m


def _pick(n, cap, unit):
    if n <= cap:
        return n
    best = None
    for t in range(unit, cap + 1, unit):
        if n % t == 0:
            best = t
    assert best is not None, (n, cap, unit)
    return best


def _params(sem):
    return pltpu.CompilerParams(dimension_semantics=sem, vmem_limit_bytes=VMEM_LIMIT)


def mm(name, a, b, add=None, out_dtype=F32, comm=None):
    m, k = a.shape
    k2, n = b.shape
    assert k == k2, (name, a.shape, b.shape)
    tn = n if k * n * 2 <= RESIDENT_BYTES else _pick(n, 640, LANES)
    tm = _pick(m, 2048 if (tn < n and m * k * 2 <= RESIDENT_BYTES) else 512, SUBLANES * 2)
    has_add = add is not None

    def body(a_ref, b_ref, *rest):
        o_ref = rest[-1]
        acc = jnp.dot(a_ref[...].astype(BF16), b_ref[...].astype(BF16), preferred_element_type=F32)
        if has_add:
            acc = acc + rest[0][...].astype(F32)
        o_ref[...] = acc.astype(o_ref.dtype)

    in_specs = [pl.BlockSpec((tm, k), lambda i, j: (i, 0)), pl.BlockSpec((k, tn), lambda i, j: (0, j))]
    ops = [a, b]
    if has_add:
        in_specs.append(pl.BlockSpec((tm, tn), lambda i, j: (i, j)))
        ops.append(add)
    res = _pallas(
        body, name=name, grid=(m // tm, n // tn), in_specs=in_specs,
        out_specs=[pl.BlockSpec((tm, tn), lambda i, j: (i, j))],
        out_shape=[jax.ShapeDtypeStruct((m, n), out_dtype)], sem=("parallel", "parallel"), comm=comm,
    )(*ops)
    return res[0] if comm is None else res


def mm_nt(name, a, b, out_dtype=F32):
    m, k = a.shape
    n, k2 = b.shape
    assert k == k2, (name, a.shape, b.shape)
    tm = _pick(m, 2048 if m * k * 2 <= RESIDENT_BYTES else 512, SUBLANES * 2)
    tn = _pick(n, 1024, LANES)

    def body(a_ref, b_ref, o_ref):
        acc = lax.dot_general(a_ref[...].astype(BF16), b_ref[...].astype(BF16), (((1,), (1,)), ((), ())),
                              preferred_element_type=F32)
        o_ref[...] = acc.astype(o_ref.dtype)

    return pl.pallas_call(
        body, name=name, grid=(m // tm, n // tn),
        in_specs=[pl.BlockSpec((tm, k), lambda i, j: (i, 0)), pl.BlockSpec((tn, k), lambda i, j: (j, 0))],
        out_specs=pl.BlockSpec((tm, tn), lambda i, j: (i, j)),
        out_shape=jax.ShapeDtypeStruct((m, n), out_dtype),
        compiler_params=_params(("parallel", "parallel")),
    )(a, b)


def mm_sh(name, a, g, out_dtype=F32, comm=None, slabs=False):
    m, k = a.shape
    nd, k2, nbp = g.shape
    assert k == k2, (name, a.shape, g.shape)
    tm = _pick(m, 2048 if m * k * 2 <= RESIDENT_BYTES else 512, SUBLANES * 2)

    def body(a_ref, b_ref, o_ref):
        o_ref[...] = jnp.dot(a_ref[...].astype(BF16), b_ref[...].astype(BF16), preferred_element_type=F32).astype(o_ref.dtype)

    if slabs:
        out_spec, out_shape = pl.BlockSpec((None, tm, nbp), lambda i, j: (j, i, 0)), (nd, m, nbp)
    else:
        out_spec, out_shape = pl.BlockSpec((tm, nbp), lambda i, j: (i, j)), (m, nd * nbp)
    res = _pallas(
        body, name=name, grid=(m // tm, nd),
        in_specs=[pl.BlockSpec((tm, k), lambda i, j: (i, 0)), pl.BlockSpec((None, k, nbp), lambda i, j: (j, 0, 0))],
        out_specs=[out_spec], out_shape=[jax.ShapeDtypeStruct(out_shape, out_dtype)], sem=("parallel", "parallel"), comm=comm,
    )(a, g)
    return res[0] if comm is None else res


def mm_sh_nt(name, a, g, add=None, comm=None):
    nd, k, nbp = g.shape
    slabs = a.ndim == 3
    m = a.shape[1] if slabs else a.shape[0]
    assert a.shape == ((nd, m, nbp) if slabs else (m, nd * nbp)), (name, a.shape, g.shape)
    has_add = add is not None
    tm = _pick(m, 512 if has_add else 1024, SUBLANES * 2)

    def body(a_ref, b_ref, *rest):
        o_ref = rest[-1]
        part = lax.dot_general(a_ref[...].astype(BF16), b_ref[...].astype(BF16), (((1,), (1,)), ((), ())),
                               preferred_element_type=F32)

        @pl.when(pl.program_id(1) == 0)
        def _():
            o_ref[...] = part + rest[0][...] if has_add else part

        @pl.when(pl.program_id(1) != 0)
        def _():
            o_ref[...] += part

    a_spec = pl.BlockSpec((None, tm, nbp), lambda i, j: (j, i, 0)) if slabs else pl.BlockSpec((tm, nbp), lambda i, j: (i, j))
    in_specs = [a_spec, pl.BlockSpec((None, k, nbp), lambda i, j: (j, 0, 0))]
    ops = [a, g]
    if has_add:
        in_specs.append(pl.BlockSpec((tm, k), lambda i, j: (i, 0)))
        ops.append(add)
    res = _pallas(
        body, name=name, grid=(m // tm, nd), in_specs=in_specs,
        out_specs=[pl.BlockSpec((tm, k), lambda i, j: (i, 0))],
        out_shape=[jax.ShapeDtypeStruct((m, k), F32)], sem=("parallel", "arbitrary"), comm=comm,
    )(*ops)
    return res[0] if comm is None else res


def mm_tn(name, a, b):
    m, k = a.shape
    m2, n = b.shape
    assert m == m2, (name, a.shape, b.shape)
    tk = _pick(k, 512, LANES)
    tn = n if m * n * 2 <= RESIDENT_BYTES else _pick(n, 640, LANES)

    def body(a_ref, b_ref, o_ref):
        o_ref[...] = lax.dot_general(a_ref[...].astype(BF16), b_ref[...].astype(BF16), (((0,), (0,)), ((), ())),
                                     preferred_element_type=F32)

    return pl.pallas_call(
        body, name=name, grid=(k // tk, n // tn),
        in_specs=[pl.BlockSpec((m, tk), lambda i, j: (0, i)), pl.BlockSpec((m, tn), lambda i, j: (0, j))],
        out_specs=pl.BlockSpec((tk, tn), lambda i, j: (i, j)),
        out_shape=jax.ShapeDtypeStruct((k, n), F32),
        compiler_params=_params(("parallel", "parallel")),
    )(a, b)


def mm_sh_out(name, a, b, comm=None):
    m, k = a.shape
    slabs = b.ndim == 3
    nbp = b.shape[2] if slabs else b.shape[1] // N_DEV
    assert b.shape == ((N_DEV, m, nbp) if slabs else (m, N_DEV * nbp)), (name, a.shape, b.shape)
    tk = _pick(k, 2048 if k * m * 2 <= RESIDENT_BYTES else 512, LANES)
    b_spec = pl.BlockSpec((None, m, nbp), lambda i, j: (j, 0, 0)) if slabs else pl.BlockSpec((m, nbp), lambda i, j: (0, j))

    def body(a_ref, b_ref, o_ref):
        o_ref[...] = lax.dot_general(a_ref[...].astype(BF16), b_ref[...].astype(BF16), (((0,), (0,)), ((), ())),
                                     preferred_element_type=F32)

    res = _pallas(
        body, name=name, grid=(k // tk, N_DEV),
        in_specs=[pl.BlockSpec((m, tk), lambda i, j: (0, i)), b_spec],
        out_specs=[pl.BlockSpec((None, tk, nbp), lambda i, j: (j, i, 0))],
        out_shape=[jax.ShapeDtypeStruct((N_DEV, k, nbp), F32)], sem=("parallel", "parallel"), comm=comm,
    )(a, b)
    return res[0] if comm is None else res


def pad_cols(y, nb, nbp):
    m = y.shape[0]
    if nb == nbp:
        return y
    return jnp.pad(y.reshape(m, N_DEV, nb), ((0, 0), (0, 0), (0, nbp - nb))).reshape(m, N_DEV * nbp)


def unpad_cols(y, nb, nbp):
    m = y.shape[0]
    if nb == nbp:
        return y
    return y.reshape(m, N_DEV, nbp)[:, :, :nb].reshape(m, N_DEV * nb)


def _in_spec(kind, a, tm):
    if kind == 'row':
        return pl.BlockSpec((tm, a.shape[1]), lambda h, i: (i, 0))
    if kind == 'hrow':
        return pl.BlockSpec((None, tm, a.shape[2]), lambda h, i: (h, i, 0))
    if kind == 'const':
        return pl.BlockSpec(a.shape, lambda h, i: (0, 0))
    assert kind == 'hconst', kind
    return pl.BlockSpec((None,) + a.shape[1:], lambda h, i: (h, 0, 0))


def _row_out(kind, c, dtype, heads, s, tm):
    if kind == 'row':
        assert heads == 1
        return jax.ShapeDtypeStruct((s, c), dtype), pl.BlockSpec((tm, c), lambda h, i: (i, 0))
    return jax.ShapeDtypeStruct((heads, s, c), dtype), pl.BlockSpec((None, tm, c), lambda h, i: (h, i, 0))


def rowwise(name, fn, arrs, kinds, outs, *, heads, s, tm, comm=None):
    n_in = len(arrs)

    def body(*refs):
        vals = fn(*[r[...] for r in refs[:n_in]])
        for o, v in zip(refs[n_in:], vals, strict=True):
            o[...] = v.astype(o.dtype)

    shapes, specs = zip(*[_row_out(k, c, dt, heads, s, tm) for k, c, dt in outs])
    return _pallas(
        body, name=name, grid=(heads, s // tm),
        in_specs=[_in_spec(k, a, tm) for k, a in zip(kinds, arrs, strict=True)],
        out_specs=list(specs), out_shape=list(shapes), sem=("parallel", "parallel"), comm=comm,
    )(*arrs)


def rowwise_vjp(name, fn, arrs, kinds, cots, cot_kinds, wrt, *, heads, s, tm, out_dtypes=None, primal=False, comm=None,
                plus=None):
    n_in, n_cot = len(arrs), len(cots)
    nb = s // tm
    out_dtypes = out_dtypes or [F32] * len(wrt)
    extra = [] if plus is None else [plus]

    def body(*refs):
        vals = [r[...] for r in refs[:n_in]]
        cvals = tuple(r[...].astype(F32) for r in refs[n_in:n_in + n_cot])
        outs = refs[n_in + n_cot + len(extra):]

        def f(*dv):
            full = list(vals)
            for j, i in enumerate(wrt):
                full[i] = dv[j]
            return tuple(fn(*full))

        prim, vjp_fn = jax.vjp(f, *[vals[i].astype(F32) for i in wrt])
        grads = list(vjp_fn(cvals))
        if plus is not None:
            grads[0] = grads[0] + refs[n_in + n_cot][...]
        for o, g in zip(outs[:len(wrt)], grads, strict=True):
            o[...] = g.astype(o.dtype)
        if primal:
            for o, p in zip(outs[len(wrt):], prim, strict=True):
                o[...] = p.astype(o.dtype)

    shapes, specs = [], []
    for i, dt in zip(wrt, out_dtypes, strict=True):
        kind, a = kinds[i], arrs[i]
        if kind in ('row', 'hrow'):
            c = a.shape[-1]
            sh, sp = _row_out('row' if (kind == 'row' and heads == 1) else 'hrow', c, dt, heads, s, tm)
        else:
            r, c = a.shape[-2:]
            sh = jax.ShapeDtypeStruct((heads, nb, r, c), dt)
            sp = pl.BlockSpec((None, None, r, c), lambda h, i: (h, i, 0, 0))
        shapes.append(sh)
        specs.append(sp)
    if primal:
        for ck, c in zip(cot_kinds, cots, strict=True):
            sh, sp = _row_out(ck, c.shape[-1], F32, heads, s, tm)
            shapes.append(sh)
            specs.append(sp)
    in_specs = [_in_spec(k, a, tm) for k, a in zip(kinds, arrs, strict=True)]
    in_specs += [_in_spec(k, a, tm) for k, a in zip(cot_kinds, cots, strict=True)]
    in_specs += [_in_spec('row', a, tm) for a in extra]
    return _pallas(
        body, name=name, grid=(heads, nb), in_specs=in_specs, out_specs=specs, out_shape=shapes,
        sem=("parallel", "parallel"), comm=comm,
    )(*arrs, *cots, *extra)


def colsum(name, x):
    n, m = x.shape
    tc = _pick(m, 32768, LANES) if m % LANES == 0 else m

    def body(x_ref, o_ref):
        acc = x_ref[0:1, :].astype(F32)
        for r in range(1, n):
            acc = acc + x_ref[r:r + 1, :].astype(F32)
        o_ref[...] = acc

    return pl.pallas_call(
        body, name=name, grid=(m // tc,), in_specs=[pl.BlockSpec((n, tc), lambda j: (0, j))],
        out_specs=pl.BlockSpec((1, tc), lambda j: (0, j)), out_shape=jax.ShapeDtypeStruct((1, m), F32),
        compiler_params=_params(("parallel",)),
    )(x)


def headsum(name, x):
    h, s, c = x.shape
    tm = _pick(s, 256, SUBLANES)

    def body(x_ref, o_ref):
        acc = x_ref[0]
        for j in range(1, h):
            acc = acc + x_ref[j]
        o_ref[...] = acc

    return pl.pallas_call(
        body, name=name, grid=(s // tm,), in_specs=[pl.BlockSpec((h, tm, c), lambda i: (0, i, 0))],
        out_specs=pl.BlockSpec((tm, c), lambda i: (i, 0)), out_shape=jax.ShapeDtypeStruct((s, c), F32),
        compiler_params=_params(("parallel",)),
    )(x)


def sum_all(name, x):
    def body(x_ref, o_ref):
        o_ref[...] = jnp.sum(x_ref[...], keepdims=True)

    return pl.pallas_call(body, name=name, out_shape=jax.ShapeDtypeStruct((1, 1), F32))(x)


def sum_partials(name, p, per_head):
    h, nb, r, c = p.shape
    if per_head:
        flat = jnp.transpose(p, (1, 0, 2, 3)).reshape(nb, h * r * c)
        if nb == 1:
            return flat.reshape(h, r, c)
        return colsum(name, flat).reshape(h, r, c)
    flat = p.reshape(h * nb, r * c)
    if h * nb == 1:
        return flat.reshape(r, c)
    return colsum(name, flat).reshape(r, c)


def _rms_fn(x, g):
    xf = x.astype(F32)
    return (xf * lax.rsqrt(jnp.mean(xf * xf, axis=-1, keepdims=True) + NORM_EPS) * g,)


def _softplus(z):
    return jnp.maximum(z, 0.0) + jnp.log(1.0 + jnp.exp(-jnp.abs(z)))


def _rwkv_pre_fn(hk, hw, ha, hg, w0, w2, a0, a2, g2, k_k, k_a):
    zw = w0 + jnp.dot(jnp.tanh(hw), w2, preferred_element_type=F32)
    w_log = -_softplus(-zw) - 0.5
    decay = jnp.exp(-jnp.exp(w_log))
    a = jax.nn.sigmoid(a0 + jnp.dot(ha, a2, preferred_element_type=F32))
    g = jnp.dot(jax.nn.sigmoid(hg), g2, preferred_element_type=F32)
    kk = hk * k_k
    kk = kk * lax.rsqrt(jnp.maximum(jnp.sum(kk * kk, axis=-1, keepdims=True), 1e-24))
    k = hk * (1.0 + (a - 1.0) * k_a)
    return decay, k, -kk, kk * a, g


def _rwkv_pre_grad_fn(hk, hw, ha, hg, w0, w2, a0, a2, g2, k_k, k_a, hr, hv):
    decay, k, a_sc, b_sc, g = _rwkv_pre_fn(hk, hw, ha, hg, w0, w2, a0, a2, g2, k_k, k_a)
    return decay, k, k, a_sc, b_sc, g, hr, hr, hv, hv


def _rwkv_post_fn(y, r, k, v, g, gn_w, gn_b, r_k):
    mu = jnp.mean(y, axis=-1, keepdims=True)
    var = jnp.mean(jnp.square(y - mu), axis=-1, keepdims=True)
    yn = (y - mu) * lax.rsqrt(var + GN_EPS) * gn_w + gn_b
    bonus = jnp.sum(r * k * r_k, axis=-1, keepdims=True) * v
    return ((yn + bonus) * g,)


def _rope_tables(pos, inv_freq2):
    ang = pos * inv_freq2
    return jnp.cos(ang), jnp.sin(ang)


def _rope(t, cos2, sin2, rot):
    return t * cos2 + jnp.dot(t, rot, precision=HIGHEST, preferred_element_type=F32) * sin2


def _mla_pre_fn(c_q, c_kv, k_pe, pos, q_g, kv_g, inv_freq2, rot):
    cos2, sin2 = _rope_tables(pos, inv_freq2)
    return _rms_fn(c_q, q_g)[0], _rms_fn(c_kv, kv_g)[0], _rope(k_pe, cos2, sin2, rot), cos2, sin2


def _mla_pre_grad_fn(c_q, c_kv, k_pe, pos, q_g, kv_g, inv_freq2, rot):
    return _mla_pre_fn(c_q, c_kv, k_pe, pos, q_g, kv_g, inv_freq2, rot)[:3]


def _loss_fn(x2, g, target):
    y = _rms_fn(x2, g)[0]
    return (0.5 * jnp.mean(jnp.square(y - target), axis=-1, keepdims=True),)


def _adamw_fn(w, g, m, v):
    m = ADAM_B1 * m + (1.0 - ADAM_B1) * g
    v = ADAM_B2 * v + (1.0 - ADAM_B2) * jnp.square(g)
    m_hat = m / (1.0 - ADAM_B1 ** ADAM_STEP)
    v_hat = v / (1.0 - ADAM_B2 ** ADAM_STEP)
    delta = -ADAM_LR * (m_hat / (jnp.sqrt(v_hat) + ADAM_EPS) + ADAM_WD * w)
    return delta, m, v


def _prev_halo_spec(c, tm):
    return pl.BlockSpec((SUBLANES, c), lambda i: (jnp.maximum(i * (tm // SUBLANES) - 1, 0), 0))


def _next_halo_spec(c, tm, s):
    return pl.BlockSpec((SUBLANES, c), lambda i: (jnp.minimum((i + 1) * (tm // SUBLANES), s // SUBLANES - 1), 0))


def _shift_down(p, halo, first_block, n):
    out = pltpu.roll(p, n, 0)
    row = lax.broadcasted_iota(jnp.int32, p.shape, 0)
    for j in range(n):
        top = jnp.where(first_block, 0.0, halo[SUBLANES - n + j:SUBLANES - n + j + 1, :])
        out = jnp.where(row == j, top, out)
    return out


def _shift_up(p, halo, last_block, n):
    rows = p.shape[0]
    out = pltpu.roll(p, rows - n, 0)
    row = lax.broadcasted_iota(jnp.int32, p.shape, 0)
    for j in range(n):
        bot = jnp.where(last_block, 0.0, halo[j:j + 1, :])
        out = jnp.where(row == rows - n + j, bot, out)
    return out


def token_shift_fwd(p, mu, tm):
    s, c = p.shape

    def body(p_ref, halo_ref, mu_ref, o_ref):
        pv = p_ref[...]
        prev = _shift_down(pv, halo_ref[...], pl.program_id(0) == 0, 1)
        o_ref[...] = pv + (prev - pv) * mu_ref[...]

    return pl.pallas_call(
        body, name="token_shift_fwd", grid=(s // tm,),
        in_specs=[pl.BlockSpec((tm, c), lambda i: (i, 0)), _prev_halo_spec(c, tm), pl.BlockSpec((1, c), lambda i: (0, 0))],
        out_specs=pl.BlockSpec((tm, c), lambda i: (i, 0)), out_shape=jax.ShapeDtypeStruct((s, c), F32),
        compiler_params=_params(("parallel",)),
    )(p, p, mu)


def token_shift_bwd(p, mu, ds, tm):
    s, c = p.shape
    nb = s // tm

    def body(p_ref, halo_ref, mu_ref, ds_ref, dsn_ref, dp_ref, dmu_ref):
        i = pl.program_id(0)
        pv, dsv, muv = p_ref[...], ds_ref[...], mu_ref[...]
        prev = _shift_down(pv, halo_ref[...], i == 0, 1)
        nxt = _shift_up(dsv, dsn_ref[...], i == nb - 1, 1)
        dp_ref[...] = dsv * (1.0 - muv) + nxt * muv
        dmu_ref[...] = jnp.sum(dsv * (prev - pv), axis=0, keepdims=True)

    return pl.pallas_call(
        body, name="token_shift_bwd", grid=(nb,),
        in_specs=[pl.BlockSpec((tm, c), lambda i: (i, 0)), _prev_halo_spec(c, tm), pl.BlockSpec((1, c), lambda i: (0, 0)),
                  pl.BlockSpec((tm, c), lambda i: (i, 0)), _next_halo_spec(c, tm, s)],
        out_specs=[pl.BlockSpec((tm, c), lambda i: (i, 0)), pl.BlockSpec((None, 1, c), lambda i: (i, 0, 0))],
        out_shape=[jax.ShapeDtypeStruct((s, c), F32), jax.ShapeDtypeStruct((nb, 1, c), F32)],
        compiler_params=_params(("parallel",)),
    )(p, p, mu, ds, ds)


def _ffn_tiles(s, f):
    return _pick(s, 256, SUBLANES), _pick(f, 1408, LANES)


def _conv_gate(gp, halo, first_block, cw, cb):
    p1 = _shift_down(gp, halo, first_block, 1)
    p2 = _shift_down(gp, halo, first_block, 2)
    return cw[0:1, :] * p2 + cw[1:2, :] * p1 + cw[2:3, :] * gp + cb, p1, p2


def ffn_act_fwd(gate_pre, up, conv_w, conv_b):
    s, f = gate_pre.shape
    tm, tc = _ffn_tiles(s, f)

    def body(gp_ref, halo_ref, up_ref, cw_ref, cb_ref, o_ref):
        gate, _, _ = _conv_gate(gp_ref[...], halo_ref[...], pl.program_id(0) == 0, cw_ref[...], cb_ref[...])
        o_ref[...] = (gate * jax.nn.sigmoid(gate) * up_ref[...]).astype(o_ref.dtype)

    blk = pl.BlockSpec((tm, tc), lambda i, j: (i, j))
    return pl.pallas_call(
        body, name="ffn_act_fwd", grid=(s // tm, f // tc),
        in_specs=[blk, pl.BlockSpec((SUBLANES, tc), lambda i, j: (jnp.maximum(i * (tm // SUBLANES) - 1, 0), j)), blk,
                  pl.BlockSpec((CONV_W, tc), lambda i, j: (0, j)), pl.BlockSpec((1, tc), lambda i, j: (0, j))],
        out_specs=blk, out_shape=jax.ShapeDtypeStruct((s, f), BF16),
        compiler_params=_params(("parallel", "parallel")),
    )(gate_pre, gate_pre, up, conv_w, conv_b)


def ffn_act_bwd1(gate_pre, up, conv_w, conv_b, d_act):
    s, f = gate_pre.shape
    tm, tc = _ffn_tiles(s, f)
    nb = s // tm

    def body(gp_ref, halo_ref, up_ref, cw_ref, cb_ref, da_ref, dg_ref, du_ref, dcw_ref, dcb_ref):
        gp = gp_ref[...]
        gate, p1, p2 = _conv_gate(gp, halo_ref[...], pl.program_id(0) == 0, cw_ref[...], cb_ref[...])
        sig = jax.nn.sigmoid(gate)
        da = da_ref[...].astype(F32)
        du_ref[...] = (da * gate * sig).astype(du_ref.dtype)
        dg = da * up_ref[...] * (sig * (1.0 + gate * (1.0 - sig)))
        dg_ref[...] = dg
        dcb_ref[...] = jnp.sum(dg, axis=0, keepdims=True)
        dcw_ref[0:1, :] = jnp.sum(dg * p2, axis=0, keepdims=True)
        dcw_ref[1:2, :] = jnp.sum(dg * p1, axis=0, keepdims=True)
        dcw_ref[2:3, :] = jnp.sum(dg * gp, axis=0, keepdims=True)

    blk = pl.BlockSpec((tm, tc), lambda i, j: (i, j))
    return pl.pallas_call(
        body, name="ffn_act_bwd1", grid=(nb, f // tc),
        in_specs=[blk, pl.BlockSpec((SUBLANES, tc), lambda i, j: (jnp.maximum(i * (tm // SUBLANES) - 1, 0), j)), blk,
                  pl.BlockSpec((CONV_W, tc), lambda i, j: (0, j)), pl.BlockSpec((1, tc), lambda i, j: (0, j)), blk],
        out_specs=[blk, blk, pl.BlockSpec((None, CONV_W, tc), lambda i, j: (i, 0, j)),
                   pl.BlockSpec((None, 1, tc), lambda i, j: (i, 0, j))],
        out_shape=[jax.ShapeDtypeStruct((s, f), F32), jax.ShapeDtypeStruct((s, f), BF16),
                   jax.ShapeDtypeStruct((nb, CONV_W, f), F32), jax.ShapeDtypeStruct((nb, 1, f), F32)],
        compiler_params=_params(("parallel", "parallel")),
    )(gate_pre, gate_pre, up, conv_w, conv_b, d_act)


def ffn_act_bwd2(d_gate, conv_w):
    s, f = d_gate.shape
    tm, tc = _ffn_tiles(s, f)
    nb = s // tm

    def body(dg_ref, halo_ref, cw_ref, o_ref):
        dg, cw = dg_ref[...], cw_ref[...]
        last = pl.program_id(0) == nb - 1
        n1 = _shift_up(dg, halo_ref[...], last, 1)
        n2 = _shift_up(dg, halo_ref[...], last, 2)
        o_ref[...] = (cw[2:3, :] * dg + cw[1:2, :] * n1 + cw[0:1, :] * n2).astype(o_ref.dtype)

    blk = pl.BlockSpec((tm, tc), lambda i, j: (i, j))
    return pl.pallas_call(
        body, name="ffn_act_bwd2", grid=(nb, f // tc),
        in_specs=[blk, pl.BlockSpec((SUBLANES, tc), lambda i, j: (jnp.minimum((i + 1) * (tm // SUBLANES), s // SUBLANES - 1), j)),
                  pl.BlockSpec((CONV_W, tc), lambda i, j: (0, j))],
        out_specs=blk, out_shape=jax.ShapeDtypeStruct((s, f), BF16),
        compiler_params=_params(("parallel", "parallel")),
    )(d_gate, d_gate, conv_w)


def _mxu(x, y, cx, cy):
    if x.ndim == 3:
        return lax.dot_general(x, y, (((cx + 1,), (cy + 1,)), ((0,), (0,))), preferred_element_type=F32)
    return lax.dot_general(x, y, (((cx,), (cy,)), ((), ())), preferred_element_type=F32)


def _split(x):
    hi = x.astype(BF16)
    return hi, (x - hi.astype(F32)).astype(BF16)


def _make_dot3(cx, cy, passes):
    @jax.custom_vjp
    def f(x, y):
        if passes == 1:
            return _mxu(x.astype(BF16), y.astype(BF16), cx, cy)
        xh, xl = _split(x)
        yh, yl = _split(y)
        return _mxu(xh, yh, cx, cy) + (_mxu(xh, yl, cx, cy) + _mxu(xl, yh, cx, cy))

    def fwd(x, y):
        return f(x, y), (x, y)

    def bwd(res, g):
        x, y = res
        dx = dot3(g, y, 1, 1 - cy, passes) if cx == 1 else dot3(y, g, 1 - cy, 1, passes)
        dy = dot3(x, g, 1 - cx, 0, passes) if cy == 0 else dot3(g, x, 0, 1 - cx, passes)
        return dx, dy

    f.defvjp(fwd, bwd)
    return f


_DOT3 = {}


def dot3(x, y, cx, cy, passes=3):
    if (cx, cy, passes) not in _DOT3:
        _DOT3[(cx, cy, passes)] = _make_dot3(cx, cy, passes)
    return _DOT3[(cx, cy, passes)](x, y)


def _dot(x, y, passes=3):
    return dot3(x, y, 1, 0, passes)


def _dot_nt(x, y, passes=3):
    return dot3(x, y, 1, 1, passes)


def _dot_tn(x, y, passes=3):
    return dot3(x, y, 0, 0, passes)


def _tri_sum(x, lower):
    t = x.shape[-2]
    row = lax.broadcasted_iota(jnp.int32, (t, t), 0)
    col = lax.broadcasted_iota(jnp.int32, (t, t), 1)
    tri = jnp.where((col <= row) if lower else (col >= row), 1.0, 0.0).astype(BF16)
    if x.ndim == 3:
        tri = jnp.broadcast_to(tri[None], (x.shape[0], t, t))
    hi = x.astype(BF16)
    rest = x - hi.astype(F32)
    mid = rest.astype(BF16)
    low = (rest - mid.astype(F32)).astype(BF16)
    return _mxu(tri, hi, 1, 0) + (_mxu(tri, mid, 1, 0) + _mxu(tri, low, 1, 0))


@jax.custom_vjp
def _cumsum_rows(x):
    return _tri_sum(x, True)


_cumsum_rows.defvjp(lambda x: (_tri_sum(x, True), None), lambda _, g: (_tri_sum(g, False),))


def _scan_chunk(s0, r, w, k, v, a, b):
    t = r.shape[1]
    row = lax.broadcasted_iota(jnp.int32, (1, t, t), 1)
    col = lax.broadcasted_iota(jnp.int32, (1, t, t), 2)
    strict, incl = col < row, col <= row
    logw = jnp.log(w)
    cum = _cumsum_rows(logw)
    w_in, w_ex, w_inv = jnp.exp(cum), jnp.exp(cum - logw), jnp.exp(-cum)
    w_all = jnp.exp(jnp.sum(logw, axis=1, keepdims=True))
    at, rt, kt, bt = a * w_ex, r * w_in, k * w_inv, b * w_inv
    ps, po = SCAN_PASSES_SOLVE, SCAN_PASSES_OUT
    a_ab = jnp.where(strict, _dot_nt(at, bt, ps), 0.0)
    a_ak = jnp.where(strict, _dot_nt(at, kt, ps), 0.0)
    a_rk = jnp.where(incl, _dot_nt(rt, kt, po), 0.0)
    a_rb = jnp.where(incl, _dot_nt(rt, bt, po), 0.0)
    u = _dot_nt(at, s0, ps) + _dot(a_ak, v, ps)
    p = a_ab
    steps = int(math.log2(t))
    assert 2 ** steps == t
    for j in range(steps):
        u = u + _dot(p, u, ps)
        if j < steps - 1:
            p = _dot(p, p, ps)
    y = _dot_nt(rt, s0, po) + _dot(a_rk, v, po) + _dot(a_rb, u, po)
    s_new = s0 * w_all + _dot_tn(v, kt * w_all, po) + _dot_tn(u, bt * w_all, po)
    return y, s_new


def scan_fwd(r, w, k, v, a, b, comm=None):
    h, s, n = r.shape
    t = min(SCAN_CHUNK, s)
    nc = s // t

    hb = SCAN_HEADS if h % SCAN_HEADS == 0 else 1

    def body(r_ref, w_ref, k_ref, v_ref, a_ref, b_ref, y_ref, ck_ref, st_ref):
        @pl.when(pl.program_id(1) == 0)
        def _():
            st_ref[...] = jnp.zeros_like(st_ref)

        s0 = st_ref[...]
        ck_ref[...] = s0
        y, s_new = _scan_chunk(s0, r_ref[...], w_ref[...], k_ref[...], v_ref[...], a_ref[...], b_ref[...])
        y_ref[...] = y
        st_ref[...] = s_new

    blk = pl.BlockSpec((hb, t, n), lambda hh, c: (hh, c, 0))
    return _pallas(
        body, name="rwkv_scan_fwd", grid=(h // hb, nc), in_specs=[blk] * 6,
        out_specs=[blk, pl.BlockSpec((hb, None, n, n), lambda hh, c: (hh, c, 0, 0))],
        out_shape=[jax.ShapeDtypeStruct((h, s, n), F32), jax.ShapeDtypeStruct((h, nc, n, n), F32)],
        scratch_shapes=[pltpu.VMEM((hb, n, n), F32)], sem=("parallel", "arbitrary"), comm=comm,
    )(r, w, k, v, a, b)


def scan_bwd(r, w, k, v, a, b, ck, dy, comm=None):
    h, s, n = r.shape
    t = min(SCAN_CHUNK, s)
    nc = s // t

    hb = SCAN_HEADS if h % SCAN_HEADS == 0 else 1

    def body(r_ref, w_ref, k_ref, v_ref, a_ref, b_ref, ck_ref, dy_ref, dr_ref, dw_ref, dk_ref, dv_ref, da_ref, db_ref, ds_ref):
        @pl.when(pl.program_id(1) == 0)
        def _():
            ds_ref[...] = jnp.zeros_like(ds_ref)

        _, vjp_fn = jax.vjp(_scan_chunk, ck_ref[...], r_ref[...], w_ref[...], k_ref[...], v_ref[...], a_ref[...], b_ref[...])
        ds0, dr, dw, dk, dv, da, db = vjp_fn((dy_ref[...], ds_ref[...]))
        ds_ref[...] = ds0
        dr_ref[...], dw_ref[...], dk_ref[...], dv_ref[...], da_ref[...], db_ref[...] = dr, dw, dk, dv, da, db

    blk = pl.BlockSpec((hb, t, n), lambda hh, c: (hh, nc - 1 - c, 0))
    return _pallas(
        body, name="rwkv_scan_bwd", grid=(h // hb, nc),
        in_specs=[blk] * 6 + [pl.BlockSpec((hb, None, n, n), lambda hh, c: (hh, nc - 1 - c, 0, 0)), blk],
        out_specs=[blk] * 6, out_shape=[jax.ShapeDtypeStruct((h, s, n), F32)] * 6,
        scratch_shapes=[pltpu.VMEM((hb, n, n), F32)], sem=("parallel", "arbitrary"), comm=comm,
    )(r, w, k, v, a, b, ck, dy)


ATTN_BLOCK = 256
ATTN_LEVELS = 8
ATTN_SLAB = 2 * LANES


def _attn_specs(h, s, tq):
    qblk = lambda c, part: pl.BlockSpec((None, tq, c), lambda hh, i: (hh, i, part))
    kblk = lambda part: pl.BlockSpec((None, s, LANES), lambda hh, i: (hh, 0, part))
    row64 = pl.BlockSpec((tq, QK_ROPE), lambda hh, i: (i, 0))
    return [qblk(LANES, 0), qblk(LANES, 1), kblk(0), kblk(1), pl.BlockSpec((s, QK_ROPE), lambda hh, i: (0, 0)),
            row64, row64, pl.BlockSpec((QK_ROPE, QK_ROPE), lambda hh, i: (0, 0))]


def _attn_levels(s, tq):
    nq = s // tq
    n_lev = min(ATTN_LEVELS, nq)
    per = nq // n_lev
    return [(lv * per, (lv + 1) * per, (lv + 1) * per * tq) for lv in range(n_lev)]


def _attn_scores(qn_b, qp_b, kn_ref, kp_ref, klen, i, tq):
    scale = (QK_NOPE + QK_ROPE) ** -0.5
    kn_b = kn_ref[0:klen, :].astype(BF16)
    kp_b = kp_ref[0:klen, :].astype(BF16)
    sc = lax.dot_general(qn_b, kn_b, (((1,), (1,)), ((), ())), preferred_element_type=F32)
    sc = sc + lax.dot_general(qp_b, kp_b, (((1,), (1,)), ((), ())), preferred_element_type=F32)
    row = i * tq + lax.broadcasted_iota(jnp.int32, sc.shape, 0)
    col = lax.broadcasted_iota(jnp.int32, sc.shape, 1)
    return jnp.where(row >= col, sc * scale, NEG_INF), scale, kn_b, kp_b


def attn_fwd(q_h, kv_h, kp, cos2, sin2, rot, comm=None):
    h, s, _ = q_h.shape
    tq = _pick(s, ATTN_BLOCK, SUBLANES)

    def body(qn_ref, qp_ref, kn_ref, v_ref, kp_ref, cos_ref, sin_ref, rot_ref, o_ref, lse_ref):
        i = pl.program_id(1)
        qn_b = qn_ref[...].astype(BF16)
        qp_b = _rope(qp_ref[:, :QK_ROPE], cos_ref[...], sin_ref[...], rot_ref[...]).astype(BF16)

        def level(klen):
            sc, _, _, _ = _attn_scores(qn_b, qp_b, kn_ref, kp_ref, klen, i, tq)
            mx = jnp.max(sc, axis=-1, keepdims=True)
            e = jnp.exp(sc - mx)
            den = jnp.sum(e, axis=-1, keepdims=True)
            o_ref[...] = jnp.dot((e / den).astype(BF16), v_ref[0:klen, :].astype(BF16), preferred_element_type=F32)
            lse_ref[...] = mx + jnp.log(den)

        for lo, hi, klen in _attn_levels(s, tq):
            pl.when((i >= lo) & (i < hi))(functools.partial(level, klen))

    oblk = lambda c: pl.BlockSpec((None, tq, c), lambda hh, i: (hh, i, 0))
    return _pallas(
        body, name="mla_attn_fwd", grid=(h, s // tq), in_specs=_attn_specs(h, s, tq),
        out_specs=[oblk(V_HEAD), oblk(1)],
        out_shape=[jax.ShapeDtypeStruct((h, s, V_HEAD), F32), jax.ShapeDtypeStruct((h, s, 1), F32)],
        sem=("parallel", "parallel"), comm=comm,
    )(q_h, q_h, kv_h, kv_h, kp, cos2, sin2, rot)


def attn_bwd(q_h, kv_h, kp, cos2, sin2, rot, o, lse, do, comm=None):
    h, s, _ = q_h.shape
    tq = _pick(s, ATTN_BLOCK, SUBLANES)
    nq = s // tq

    def body(qn_ref, qp_ref, kn_ref, v_ref, kp_ref, cos_ref, sin_ref, rot_ref, o_ref, lse_ref, do_ref,
             dq_ref, dkv_ref, dkp_ref, dkv_acc, dkp_acc):
        i = pl.program_id(1)

        @pl.when(i == 0)
        def _():
            dkv_acc[...] = jnp.zeros_like(dkv_acc)
            dkp_acc[...] = jnp.zeros_like(dkp_acc)

        cosv, sinv, rotv = cos_ref[...], sin_ref[...], rot_ref[...]
        qn_b = qn_ref[...].astype(BF16)
        qp_b = _rope(qp_ref[:, :QK_ROPE], cosv, sinv, rotv).astype(BF16)
        dov = do_ref[...]
        do_b = dov.astype(BF16)
        delta = jnp.sum(dov * o_ref[...], axis=-1, keepdims=True)
        lsev = lse_ref[...]

        def level(klen):
            sc, scale, kn_b, kp_b = _attn_scores(qn_b, qp_b, kn_ref, kp_ref, klen, i, tq)
            p = jnp.exp(sc - lsev)
            dp = lax.dot_general(do_b, v_ref[0:klen, :].astype(BF16), (((1,), (1,)), ((), ())), preferred_element_type=F32)
            ds = (p * (dp - delta) * scale).astype(BF16)
            dkv_acc[0:klen, :LANES] += lax.dot_general(ds, qn_b, (((0,), (0,)), ((), ())), preferred_element_type=F32)
            dkv_acc[0:klen, LANES:] += lax.dot_general(p.astype(BF16), do_b, (((0,), (0,)), ((), ())), preferred_element_type=F32)
            dkp_acc[0:klen, :] += lax.dot_general(ds, qp_b, (((0,), (0,)), ((), ())), preferred_element_type=F32)
            dqp = jnp.dot(ds, kp_b, preferred_element_type=F32)
            dqp_raw = dqp * cosv + lax.dot_general(dqp * sinv, rotv, (((1,), (1,)), ((), ())), precision=HIGHEST,
                                                   preferred_element_type=F32)
            dq_ref[:, :QK_NOPE] = jnp.dot(ds, kn_b, preferred_element_type=F32).astype(dq_ref.dtype)
            dq_ref[:, QK_NOPE:QK_NOPE + QK_ROPE] = dqp_raw.astype(dq_ref.dtype)
            dq_ref[:, QK_NOPE + QK_ROPE:] = jnp.zeros((tq, ATTN_SLAB - QK_NOPE - QK_ROPE), dq_ref.dtype)

        for lo, hi, klen in _attn_levels(s, tq):
            pl.when((i >= lo) & (i < hi))(functools.partial(level, klen))

        @pl.when(i == nq - 1)
        def _():
            dkv_ref[...] = dkv_acc[...].astype(dkv_ref.dtype)
            dkp_ref[...] = dkp_acc[...]

    rblk = lambda c: pl.BlockSpec((None, tq, c), lambda hh, i: (hh, i, 0))
    sblk = lambda c: pl.BlockSpec((None, s, c), lambda hh, i: (hh, 0, 0))
    return _pallas(
        body, name="mla_attn_bwd", grid=(h, nq),
        in_specs=_attn_specs(h, s, tq) + [rblk(V_HEAD), rblk(1), rblk(V_HEAD)],
        out_specs=[rblk(ATTN_SLAB), sblk(ATTN_SLAB), sblk(QK_ROPE)],
        out_shape=[jax.ShapeDtypeStruct((h, s, ATTN_SLAB), BF16), jax.ShapeDtypeStruct((h, s, ATTN_SLAB), BF16),
                   jax.ShapeDtypeStruct((h, s, QK_ROPE), F32)],
        scratch_shapes=[pltpu.VMEM((s, ATTN_SLAB), F32), pltpu.VMEM((s, QK_ROPE), F32)],
        sem=("parallel", "arbitrary"), comm=comm,
    )(q_h, q_h, kv_h, kv_h, kp, cos2, sin2, rot, o, lse, do)


def _my_pos():
    return lax.axis_index("x"), lax.axis_index("y"), lax.axis_index("c")


def _dev_index(px, py, pc):
    return 4 * px + 2 * py + pc


def all_gather(name, shard):
    r, c = shard.shape

    def body(x_ref, out_ref, send_sems, recv_sems, local_sem):
        x, y, cc = _my_pos()
        me, sibling = (x, y, cc), (x, y, 1 - cc)
        chips = [(1 - x, y), (x, 1 - y), (1 - x, 1 - y)]

        def rows(px, py, pc):
            return out_ref.at[_dev_index(px, py, pc)]

        def copy(kk, block, to, src=None):
            return pltpu.make_async_remote_copy(
                src_ref=rows(*block) if src is None else src, dst_ref=rows(*block),
                send_sem=send_sems.at[kk], recv_sem=recv_sems.at[kk], device_id=to, device_id_type=MESH)

        mine = pltpu.make_async_copy(x_ref, rows(*me), local_sem)
        mine.start()
        first = [copy(0, me, sibling, src=x_ref)]
        first += [copy(1 + j, me, (*chip, cc), src=x_ref) for j, chip in enumerate(chips)]
        for cp in first:
            cp.start()
        passed = [copy(4 + j, (*chip, cc), sibling) for j, chip in enumerate(chips)]
        for j, chip in enumerate(chips):
            copy(1 + j, (*chip, cc), me).wait_recv()
            passed[j].start()
        copy(0, sibling, me).wait_recv()
        for j, chip in enumerate(chips):
            copy(4 + j, (*chip, 1 - cc), me).wait_recv()
        for cp in first + passed:
            cp.wait_send()
        mine.wait()

    return pl.pallas_call(
        body, name=name, out_shape=jax.ShapeDtypeStruct((N_DEV, r, c), shard.dtype),
        in_specs=[pl.BlockSpec(memory_space=pl.ANY)], out_specs=pl.BlockSpec(memory_space=pl.ANY),
        scratch_shapes=[pltpu.SemaphoreType.DMA((7,)), pltpu.SemaphoreType.DMA((7,)), pltpu.SemaphoreType.DMA],
    )(shard)


def exchange_sibling(name, g):
    _, r, c = g.shape

    def body(g_ref, out_ref, send_sems, recv_sems):
        x, y, cc = _my_pos()
        copies = []
        for px in range(2):
            for py in range(2):
                slot = 2 * px + py
                copies.append(pltpu.make_async_remote_copy(
                    src_ref=g_ref.at[_dev_index(px, py, 1 - cc)], dst_ref=out_ref.at[slot],
                    send_sem=send_sems.at[slot], recv_sem=recv_sems.at[slot], device_id=(x, y, 1 - cc), device_id_type=MESH))
        for cp in copies:
            cp.start()
        for cp in copies:
            cp.wait()

    return pl.pallas_call(
        body, name=name, out_shape=jax.ShapeDtypeStruct((4, r, c), g.dtype),
        in_specs=[pl.BlockSpec(memory_space=pl.ANY)], out_specs=pl.BlockSpec(memory_space=pl.ANY),
        scratch_shapes=[pltpu.SemaphoreType.DMA((4,)), pltpu.SemaphoreType.DMA((4,))],
    )(g)


def exchange_chips(name, hsum):
    _, r, c = hsum.shape

    def body(h_ref, out_ref, send_sems, recv_sems):
        x, y, cc = _my_pos()
        copies = []
        for j, (px, py) in enumerate([(1 - x, y), (x, 1 - y), (1 - x, 1 - y)]):
            copies.append(pltpu.make_async_remote_copy(
                src_ref=h_ref.at[2 * px + py], dst_ref=out_ref.at[j],
                send_sem=send_sems.at[j], recv_sem=recv_sems.at[j], device_id=(px, py, cc), device_id_type=MESH))
        for cp in copies:
            cp.start()
        for cp in copies:
            cp.wait()

    return pl.pallas_call(
        body, name=name, out_shape=jax.ShapeDtypeStruct((3, r, c), hsum.dtype),
        in_specs=[pl.BlockSpec(memory_space=pl.ANY)], out_specs=pl.BlockSpec(memory_space=pl.ANY),
        scratch_shapes=[pltpu.SemaphoreType.DMA((3,)), pltpu.SemaphoreType.DMA((3,))],
    )(hsum)


def _rs_add_sibling(name, g, from_sibling):
    _, r, c = g.shape
    tr = _pick(r, 512, SUBLANES * 2)
    x, y, cc = _my_pos()

    def body(cc_ref, slot_ref, g_ref, s_ref, o_ref, ob_ref):
        tot = g_ref[...] + s_ref[...]
        ob_ref[...] = tot.astype(BF16)

        @pl.when(pl.program_id(1) == slot_ref[0])
        def _():
            o_ref[...] = tot

    return pl.pallas_call(
        body, name=name,
        grid_spec=pltpu.PrefetchScalarGridSpec(
            num_scalar_prefetch=2, grid=(r // tr, 4),
            in_specs=[pl.BlockSpec((None, None, tr, c), lambda i, s_, cc_ref, slot_ref: (s_, cc_ref[0], i, 0)),
                      pl.BlockSpec((None, tr, c), lambda i, s_, cc_ref, slot_ref: (s_, i, 0))],
            out_specs=[pl.BlockSpec((tr, c), lambda i, s_, cc_ref, slot_ref: (i, 0)),
                       pl.BlockSpec((None, tr, c), lambda i, s_, cc_ref, slot_ref: (s_, i, 0))]),
        out_shape=[jax.ShapeDtypeStruct((r, c), F32), jax.ShapeDtypeStruct((4, r, c), BF16)],
        compiler_params=_params(("parallel", "arbitrary")),
    )(cc.reshape(1).astype(jnp.int32), (2 * x + y).reshape(1).astype(jnp.int32), g.reshape(4, 2, r, c), from_sibling)


def _rs_add_chips(name, own_sum, from_chips):
    r, c = own_sum.shape
    tr = _pick(r, 512, SUBLANES * 2)

    def body(h_ref, f0_ref, f1_ref, f2_ref, o_ref):
        o_ref[...] = ((h_ref[...] + f0_ref[...].astype(F32)) + f1_ref[...].astype(F32)) + f2_ref[...].astype(F32)

    def from_blk(j):
        return pl.BlockSpec((None, tr, c), lambda i: (j, i, 0))

    blk = pl.BlockSpec((tr, c), lambda i: (i, 0))
    return pl.pallas_call(
        body, name=name, grid=(r // tr,), in_specs=[blk, from_blk(0), from_blk(1), from_blk(2)], out_specs=blk,
        out_shape=jax.ShapeDtypeStruct((r, c), F32), compiler_params=_params(("parallel",)),
    )(own_sum, from_chips, from_chips, from_chips)


def adamw_scatter(name, w, m, v, own_sum, from_chips):
    _, a, b = w.shape
    r, c = own_sum.shape
    assert a <= r and b <= c, (name, w.shape, own_sum.shape)
    tr = _pick(a, 256, SUBLANES * 2)

    def body(h_ref, f0_ref, f1_ref, f2_ref, w_ref, m_ref, v_ref, g_out, d_out, m_out, v_out):
        g = ((h_ref[...] + f0_ref[...].astype(F32)) + f1_ref[...].astype(F32)) + f2_ref[...].astype(F32)
        g = g[:, :b]
        g_out[...] = g
        d_out[...], m_out[...], v_out[...] = _adamw_fn(w_ref[...], g, m_ref[...], v_ref[...])

    def from_blk(j):
        return pl.BlockSpec((None, tr, c), lambda i: (j, i, 0))

    mine = pl.BlockSpec((None, tr, b), lambda i: (0, i, 0))
    return pl.pallas_call(
        body, name=name, grid=(a // tr,),
        in_specs=[pl.BlockSpec((tr, c), lambda i: (i, 0)), from_blk(0), from_blk(1), from_blk(2), mine, mine, mine],
        out_specs=[mine] * 4, out_shape=[jax.ShapeDtypeStruct((1, a, b), F32)] * 4, compiler_params=_params(("parallel",)),
    )(own_sum, from_chips, from_chips, from_chips, w, m, v)


def rs_chip_sum(tag, g):
    return _rs_add_sibling("rs_add_sibling_" + tag, g, exchange_sibling("rs_sibling_" + tag, g))


def reduce_scatter(tag, g):
    own_sum, chip_sums_b = rs_chip_sum(tag, g)
    return _rs_add_chips("rs_add_chips_" + tag, own_sum, exchange_chips("rs_chips_" + tag, chip_sums_b))


class GatherIci:
    def __init__(self, shards):
        self.inputs = list(shards)
        self.out_shapes = [jax.ShapeDtypeStruct((N_DEV,) + s.shape, s.dtype) for s in shards]
        self.n_remote, self.n_local = 3 * len(shards), len(shards)

    def make(self, cins, couts, send, recv, local):
        x, y, cc = _my_pos()
        me = _dev_index(x, y, cc)
        copies = []
        for w, (src, out) in enumerate(zip(cins, couts, strict=True)):
            copies.append(pltpu.make_async_copy(src, out.at[me], local.at[w]))
            for j, (px, py) in enumerate([(1 - x, y), (x, 1 - y), (1 - x, 1 - y)]):
                copies.append(pltpu.make_async_remote_copy(
                    src_ref=src, dst_ref=out.at[me], send_sem=send.at[3 * w + j], recv_sem=recv.at[3 * w + j],
                    device_id=(px, py, cc), device_id_type=MESH))
        return copies


class RsChips:
    def __init__(self, chip_sums):
        self.inputs = list(chip_sums)
        self.out_shapes = [jax.ShapeDtypeStruct((3,) + h.shape[1:], h.dtype) for h in chip_sums]
        self.n_remote, self.n_local = 3 * len(chip_sums), 0

    def make(self, cins, couts, send, recv, local):
        x, y, cc = _my_pos()
        copies = []
        for w, (h_ref, out) in enumerate(zip(cins, couts, strict=True)):
            for j, (px, py) in enumerate([(1 - x, y), (x, 1 - y), (1 - x, 1 - y)]):
                copies.append(pltpu.make_async_remote_copy(
                    src_ref=h_ref.at[2 * px + py], dst_ref=out.at[j], send_sem=send.at[3 * w + j], recv_sem=recv.at[3 * w + j],
                    device_id=(px, py, cc), device_id_type=MESH))
        return copies


class SiblingSwap:
    def __init__(self, gs):
        self.inputs = list(gs)
        self.out_shapes = [jax.ShapeDtypeStruct((4,) + g.shape[1:], g.dtype) for g in gs]
        self.n_remote, self.n_local = 4 * len(gs), 0

    def make(self, cins, couts, send, recv, local):
        x, y, cc = _my_pos()
        copies = []
        for w, (g_ref, out) in enumerate(zip(cins, couts, strict=True)):
            for px in range(2):
                for py in range(2):
                    q = 4 * w + 2 * px + py
                    copies.append(pltpu.make_async_remote_copy(
                        src_ref=g_ref.at[_dev_index(px, py, 1 - cc)], dst_ref=out.at[2 * px + py],
                        send_sem=send.at[q], recv_sem=recv.at[q], device_id=(x, y, 1 - cc), device_id_type=MESH))
        return copies


class GatherD2D:
    def __init__(self, arrays):
        self.inputs = list(arrays)
        self.out_shapes = [jax.ShapeDtypeStruct(a.shape, a.dtype) for a in arrays]
        self.n_remote, self.n_local = 4 * len(arrays), 0
        self.aliases = [(i, i) for i in range(len(arrays))]

    def make(self, cins, couts, send, recv, local):
        x, y, cc = _my_pos()
        copies = []
        for w, out in enumerate(couts):
            for px in range(2):
                for py in range(2):
                    q = 4 * w + 2 * px + py
                    slab = out.at[_dev_index(px, py, cc)]
                    copies.append(pltpu.make_async_remote_copy(
                        src_ref=slab, dst_ref=slab, send_sem=send.at[q], recv_sem=recv.at[q],
                        device_id=(x, y, 1 - cc), device_id_type=MESH))
        return copies


class _SemSlice:
    def __init__(self, base, start):
        self.base, self.start = base, start

    @property
    def at(self):
        return self

    def __getitem__(self, k):
        return self.base.at[self.start + k]


class CommGroup:
    def __init__(self, plans):
        self.plans = [p for p in plans if p.inputs]
        self.inputs = [a for p in self.plans for a in p.inputs]
        self.out_shapes = [s_ for p in self.plans for s_ in p.out_shapes]
        self.n_remote = sum(p.n_remote for p in self.plans)
        self.n_local = sum(p.n_local for p in self.plans)
        self.aliases, i0, o0 = [], 0, 0
        for p in self.plans:
            self.aliases += [(i0 + i, o0 + j) for i, j in getattr(p, 'aliases', [])]
            i0, o0 = i0 + len(p.inputs), o0 + len(p.out_shapes)

    def make(self, cins, couts, send, recv, local):
        copies, i0, o0, r0, l0 = [], 0, 0, 0, 0
        for p in self.plans:
            ni, no = len(p.inputs), len(p.out_shapes)
            copies += p.make(cins[i0:i0 + ni], couts[o0:o0 + no], _SemSlice(send, r0), _SemSlice(recv, r0), _SemSlice(local, l0))
            i0, o0, r0, l0 = i0 + ni, o0 + no, r0 + p.n_remote, l0 + p.n_local
        return copies


def _pallas(body, *, name, grid, in_specs, out_specs, out_shape, scratch_shapes=(), sem, comm=None):
    in_specs, out_specs, out_shape, scratch_shapes = list(in_specs), list(out_specs), list(out_shape), list(scratch_shapes)
    if comm is None:
        return pl.pallas_call(body, name=name, grid=grid, in_specs=in_specs, out_specs=out_specs, out_shape=out_shape,
                              scratch_shapes=scratch_shapes, compiler_params=_params(sem))
    n_in, n_out, n_scr = len(in_specs), len(out_specs), len(scratch_shapes)
    nci, nco = len(comm.inputs), len(comm.out_shapes)

    def body2(*refs):
        ins, cins = refs[:n_in], refs[n_in:n_in + nci]
        o0 = n_in + nci
        outs, couts = refs[o0:o0 + n_out], refs[o0 + n_out:o0 + n_out + nco]
        s0 = o0 + n_out + nco
        scr = refs[s0:s0 + n_scr]
        send, recv, local = refs[s0 + n_scr:]
        pids = [pl.program_id(k) for k in range(len(grid))]
        first = functools.reduce(jnp.logical_and, [p == 0 for p in pids])
        last = functools.reduce(jnp.logical_and, [p == g - 1 for p, g in zip(pids, grid)])

        @pl.when(first)
        def _():
            for cp in comm.make(cins, couts, send, recv, local):
                cp.start()

        body(*ins, *outs, *scr)

        @pl.when(last)
        def _():
            for cp in comm.make(cins, couts, send, recv, local):
                cp.wait()

    any_spec = pl.BlockSpec(memory_space=pl.ANY)
    call = pl.pallas_call(
        body2, name=name, grid=grid, in_specs=in_specs + [any_spec] * nci, out_specs=out_specs + [any_spec] * nco,
        out_shape=out_shape + list(comm.out_shapes),
        input_output_aliases={n_in + i: n_out + j for i, j in getattr(comm, 'aliases', [])},
        scratch_shapes=scratch_shapes + [pltpu.SemaphoreType.DMA((comm.n_remote,)), pltpu.SemaphoreType.DMA((comm.n_remote,)),
                                         pltpu.SemaphoreType.DMA((max(comm.n_local, 1),))],
        compiler_params=_params(tuple("arbitrary" for _ in grid)))
    return lambda *args: call(*args, *comm.inputs)


PACK_W = 1024


class Pack:
    def __init__(self, entries, row_unit):
        self.entries = entries
        self.sizes = [int(np.prod(sh)) for _, sh in entries]
        self.offsets = np.concatenate([[0], np.cumsum(self.sizes)]).tolist()
        self.total = _round_up(self.offsets[-1], PACK_W * row_unit)
        self.rows = self.total // PACK_W

    def pack(self, arrays, dtype, lead=()):
        flat = [arrays[n].astype(dtype).reshape(lead + (-1,)) for n, _ in self.entries]
        pad = self.total - self.offsets[-1]
        if pad:
            flat.append(jnp.zeros(lead + (pad,), dtype))
        return jnp.concatenate(flat, axis=-1).reshape(lead + (self.rows, PACK_W))

    def unpack(self, buf, lead=()):
        flat = buf.reshape(lead + (self.total,))
        out = {}
        for (n, sh), off, sz in zip(self.entries, self.offsets, self.sizes):
            out[n] = lax.slice_in_dim(flat, off, off + sz, axis=len(lead)).reshape(lead + tuple(sh))
        return out


def _gathered_to_full(g, how):
    _, a, b = g.shape
    if how == 'row':
        return g.reshape(N_DEV * a, b)
    return jnp.transpose(g, (1, 0, 2)).reshape(a, N_DEV * b)


def _full_to_shards(w, how):
    a, b = w.shape
    if how == 'row':
        return w.reshape(N_DEV, a // N_DEV, b)
    return jnp.transpose(w.reshape(a, N_DEV, b // N_DEV), (1, 0, 2))


def _to_heads(t, width):
    s, c = t.shape
    return jnp.transpose(t.reshape(s, c // width, width), (1, 0, 2))


def _from_heads(t):
    h, s, w = t.shape
    return jnp.transpose(t, (1, 0, 2)).reshape(s, h * w)


def _rot_matrix():
    half = QK_ROPE // 2
    rot = np.zeros((QK_ROPE, QK_ROPE), np.float32)
    for i in range(half):
        rot[i + half, i] = -1.0
        rot[i, i + half] = 1.0
    return jnp.asarray(rot)


def _inv_freq2():
    half = QK_ROPE // 2
    inv = ROPE_THETA ** (-np.arange(half, dtype=np.float32) / half)
    return jnp.asarray(np.concatenate([inv, inv])[None, :].astype(np.float32))


def kernel(x, positions, attn_norm_g, w_in, rwkv_mu, rwkv_w0, rwkv_w2, rwkv_a0, rwkv_a2, rwkv_g2, rwkv_k_k, rwkv_k_a, rwkv_r_k, rwkv_gn_w, rwkv_gn_b, mla_q_norm_g, mla_w_uq, mla_kv_norm_g, mla_w_ukv, w_out, ffn_norm_g, ffn_w_gate, ffn_w_up, ffn_conv_w, ffn_conv_b, ffn_w_down, final_norm_g, loss_target, m_attn_norm_g, m_w_in, m_rwkv_mu, m_rwkv_w0, m_rwkv_w2, m_rwkv_a0, m_rwkv_a2, m_rwkv_g2, m_rwkv_k_k, m_rwkv_k_a, m_rwkv_r_k, m_rwkv_gn_w, m_rwkv_gn_b, m_mla_q_norm_g, m_mla_w_uq, m_mla_kv_norm_g, m_mla_w_ukv, m_w_out, m_ffn_norm_g, m_ffn_w_gate, m_ffn_w_up, m_ffn_conv_w, m_ffn_conv_b, m_ffn_w_down, m_final_norm_g, v_attn_norm_g, v_w_in, v_rwkv_mu, v_rwkv_w0, v_rwkv_w2, v_rwkv_a0, v_rwkv_a2, v_rwkv_g2, v_rwkv_k_k, v_rwkv_k_a, v_rwkv_r_k, v_rwkv_gn_w, v_rwkv_gn_b, v_mla_q_norm_g, v_mla_w_uq, v_mla_kv_norm_g, v_mla_w_ukv, v_w_out, v_ffn_norm_g, v_ffn_w_gate, v_ffn_w_up, v_ffn_conv_w, v_ffn_conv_b, v_ffn_w_down, v_final_norm_g):
    given = dict(locals())
    wts = {n: given[n] for n in WEIGHTS}
    mom_m = {n: given["m_" + n] for n in WEIGHTS}
    mom_v = {n: given["v_" + n] for n in WEIGHTS}
    out_shapes = {n: wts[n].shape for n in WEIGHTS}

    def local2d(n, a):
        if n == 'rwkv_r_k' or a.ndim <= 2:
            return a.reshape(1, -1)
        return a.reshape(a.shape[1:])

    w2d = {n: local2d(n, wts[n]) for n in WEIGHTS}
    m2d = {n: local2d(n, mom_m[n]) for n in WEIGHTS}
    v2d = {n: local2d(n, mom_v[n]) for n in WEIGHTS}

    xs = x.reshape(x.shape[1:])
    tgt = loss_target.reshape(loss_target.shape[1:])
    s, d = xs.shape
    c_rwkv = w2d['rwkv_w0'].shape[1]
    n_rh = c_rwkv // RWKV_HEAD
    decay_lora, aaa_lora, gate_lora = w2d['rwkv_w2'].shape[0], w2d['rwkv_a2'].shape[0], w2d['rwkv_g2'].shape[0]
    q_lora, kv_lora = w2d['mla_q_norm_g'].shape[1], w2d['mla_kv_norm_g'].shape[1]
    shift_dim = w2d['rwkv_mu'].shape[1]
    d_in = w2d['w_in'].shape[1] * N_DEV
    n_mh = w2d['mla_w_uq'].shape[1] * N_DEV // (QK_NOPE + QK_ROPE)
    tm = _pick(s, 256, SUBLANES)
    tm_wide = _pick(s, 128, SUBLANES)
    tm_heads = _pick(s, 512, SUBLANES)

    nb = {n: w2d[n].shape[1] for n in BIG if BIG[n] == 'col'}
    nbp = {n: _round_up(v_, LANES) for n, v_ in nb.items()}
    shards = {}
    for n in BIG:
        w = w2d[n].astype(BF16)
        if BIG[n] == 'col':
            w = jnp.pad(w, ((0, 0), (0, nbp[n] - nb[n])))
        elif n == 'ffn_w_down':
            w = jnp.pad(w, ((0, nbp['ffn_w_gate'] - w.shape[0]), (0, 0)))
        shards[n] = w

    def as_used(n, g):
        return g if BIG[n] == 'col' else g.reshape(N_DEV * g.shape[1], g.shape[2])

    gathered = {'w_in': as_used('w_in', all_gather("gather_w_in", shards['w_in']))}
    f_pad = N_DEV * nbp['ffn_w_gate']
    small_pack = Pack([(n, w2d[n].shape) for n in SMALL_SHARDED], 8)
    small_all = all_gather("gather_small", small_pack.pack(w2d, F32))
    full = {}
    for n, g in small_pack.unpack(small_all, lead=(N_DEV,)).items():
        full[n] = _gathered_to_full(g, SMALL_SHARDED[n])
    conv_w_pad = pad_cols(full['ffn_conv_w'], nb['ffn_w_gate'], nbp['ffn_w_gate'])
    conv_b_pad = pad_cols(w2d['ffn_conv_b'], nb['ffn_w_gate'], nbp['ffn_w_gate'])

    (h1,) = rowwise("rms_attn", _rms_fn, [xs, w2d['attn_norm_g']], ['row', 'const'], [('row', d, BF16)], heads=1, s=s, tm=tm)
    landed = {}

    def gather_behind(run, ici=(), d2d=()):
        *res, = run(CommGroup([GatherIci([shards[n] for n in ici]), GatherD2D([landed[n] for n in d2d])]))
        n_own = len(res) - len(ici) - len(d2d)
        landed.update(zip(ici, res[n_own:n_own + len(ici)], strict=True))
        for n, g in zip(d2d, res[n_own + len(ici):], strict=True):
            gathered[n] = as_used(n, g)
        return res[:n_own]

    w_in_nat = jnp.transpose(gathered['w_in'], (1, 0, 2))[:, :, :nb['w_in']].reshape(d, d_in)
    w_in_nat = jnp.pad(w_in_nat, ((0, 0), (0, _round_up(d_in, LANES) - d_in)))
    (proj,) = gather_behind(lambda c: mm("proj_in", h1, w_in_nat, comm=c), ici=['mla_w_uq', 'mla_w_ukv', 'w_out'])
    p_rwkv = proj[:, :shift_dim]
    c_q = proj[:, shift_dim:shift_dim + q_lora]
    c_kv = proj[:, shift_dim + q_lora:shift_dim + q_lora + kv_lora]
    k_pe = proj[:, shift_dim + q_lora + kv_lora:d_in]
    shifted = token_shift_fwd(p_rwkv, w2d['rwkv_mu'], tm_wide)
    o1, o2, o3 = c_rwkv, 2 * c_rwkv, 3 * c_rwkv
    hr = _to_heads(shifted[:, :o1], RWKV_HEAD)
    hk = _to_heads(shifted[:, o1:o2], RWKV_HEAD)
    hv = _to_heads(shifted[:, o2:o3], RWKV_HEAD)
    hw = shifted[:, o3:o3 + decay_lora]
    ha = shifted[:, o3 + decay_lora:o3 + decay_lora + aaa_lora]
    hg = shifted[:, o3 + decay_lora + aaa_lora:]

    def per_head(vec):
        return vec.reshape(n_rh, 1, RWKV_HEAD)

    def lora_heads(w):
        return jnp.transpose(w.reshape(w.shape[0], n_rh, RWKV_HEAD), (1, 0, 2))

    pre_args = [hk, hw, ha, hg, per_head(w2d['rwkv_w0']), lora_heads(full['rwkv_w2']), per_head(w2d['rwkv_a0']),
                lora_heads(full['rwkv_a2']), lora_heads(full['rwkv_g2']), per_head(w2d['rwkv_k_k']), per_head(w2d['rwkv_k_a'])]
    pre_kinds = ['hrow', 'row', 'row', 'row', 'hconst', 'hconst', 'hconst', 'hconst', 'hconst', 'hconst', 'hconst']
    decay, kx, a_sc, b_sc, gate_r = gather_behind(
        lambda c: rowwise("rwkv_pre", _rwkv_pre_fn, pre_args, pre_kinds, [('hrow', RWKV_HEAD, F32)] * 5, heads=n_rh, s=s,
                          tm=tm_heads, comm=c),
        ici=['ffn_w_gate'], d2d=['mla_w_uq', 'mla_w_ukv', 'w_out'])
    y_scan, ckpt = gather_behind(lambda c: scan_fwd(hr, decay, kx, hv, a_sc, b_sc, comm=c), ici=['ffn_w_up'], d2d=['ffn_w_gate'])
    post_args = [y_scan, hr, kx, hv, gate_r, per_head(w2d['rwkv_gn_w']), per_head(w2d['rwkv_gn_b']), per_head(w2d['rwkv_r_k'])]
    post_kinds = ['hrow'] * 5 + ['hconst'] * 3
    (y_rwkv_h,) = rowwise("rwkv_post", _rwkv_post_fn, post_args, post_kinds, [('hrow', RWKV_HEAD, F32)], heads=n_rh, s=s, tm=tm_heads)

    pos = positions.reshape(s, 1).astype(F32)
    rot, inv2 = _rot_matrix(), _inv_freq2()
    mla_args = [c_q, c_kv, k_pe, pos, w2d['mla_q_norm_g'], w2d['mla_kv_norm_g'], inv2, rot]
    mla_kinds = ['row', 'row', 'row', 'row', 'const', 'const', 'const', 'const']
    qn, kvn, kp_rot, cos2, sin2 = rowwise(
        "mla_pre", _mla_pre_fn, mla_args, mla_kinds,
        [('row', q_lora, BF16), ('row', kv_lora, BF16), ('row', QK_ROPE, F32), ('row', QK_ROPE, F32), ('row', QK_ROPE, F32)],
        heads=1, s=s, tm=tm)
    assert n_mh == N_DEV and nb['mla_w_uq'] == QK_NOPE + QK_ROPE and nb['mla_w_ukv'] == QK_NOPE + V_HEAD
    assert nbp['mla_w_uq'] == ATTN_SLAB and nbp['mla_w_ukv'] == ATTN_SLAB
    q_h = mm_sh("proj_q", qn, gathered['mla_w_uq'], slabs=True)
    kv_h = mm_sh("proj_kv", kvn, gathered['mla_w_ukv'], slabs=True)
    o_att, lse = gather_behind(lambda c: attn_fwd(q_h, kv_h, kp_rot, cos2, sin2, rot, comm=c), ici=['ffn_w_down'], d2d=['ffn_w_up'])
    ycat = jnp.concatenate([_from_heads(y_rwkv_h), _from_heads(o_att)], axis=-1).astype(BF16)
    (x1,) = gather_behind(lambda c: mm("proj_out", ycat, gathered['w_out'], add=xs, comm=c), d2d=['ffn_w_down'])
    (h2,) = rowwise("rms_ffn", _rms_fn, [x1, w2d['ffn_norm_g']], ['row', 'const'], [('row', d, BF16)], heads=1, s=s, tm=tm)
    gate_pre = mm_sh("ffn_gate", h2, gathered['ffn_w_gate'])
    up = mm_sh("ffn_up", h2, gathered['ffn_w_up'])
    act = ffn_act_fwd(gate_pre, up, conv_w_pad, conv_b_pad)
    x2 = mm("ffn_down", act, gathered['ffn_w_down'], add=x1)

    ones = jnp.ones((s, 1), F32)
    fin_g = w2d['final_norm_g']
    d_x2, dg_final_p, loss_rows = rowwise_vjp("loss_bwd", _loss_fn, [x2, fin_g, tgt], ['row', 'const', 'row'], [ones], ['row'],
                                              [0, 1], heads=1, s=s, tm=tm, primal=True)
    d_x2_b = d_x2.astype(BF16)
    d_act = mm_nt("d_act", d_x2_b, gathered['ffn_w_down'], out_dtype=BF16)
    gsh, chip_sums, from_chips = {}, {}, {}

    def scatter_behind(run, ici=(), swap=()):
        *res, = run(CommGroup([RsChips([chip_sums[n][1] for n in ici]), SiblingSwap([gsh[n] for n in swap])]))
        n_own = len(res) - len(ici) - len(swap)
        from_chips.update(zip(ici, res[n_own:n_own + len(ici)], strict=True))
        for n, from_sibling in zip(swap, res[n_own + len(ici):], strict=True):
            chip_sums[n] = _rs_add_sibling("rs_add_sibling_" + n, gsh[n], from_sibling)
        return res[:n_own]

    gsh['ffn_w_down'] = mm_tn("dw_down", act, d_x2_b).reshape(N_DEV, nbp['ffn_w_gate'], d)
    d_gate, d_up, dcw_p, dcb_p = ffn_act_bwd1(gate_pre, up, conv_w_pad, conv_b_pad, d_act)
    d_gp = ffn_act_bwd2(d_gate, conv_w_pad)
    (d_h2_g,) = scatter_behind(lambda c: mm_sh_nt("d_h2_gate", d_gp, gathered['ffn_w_gate'], comm=c), swap=['ffn_w_down'])
    d_h2 = mm_sh_nt("d_h2_up", d_up, gathered['ffn_w_up'], add=d_h2_g)
    gsh['ffn_w_gate'] = mm_sh_out("dw_gate", h2, d_gp)
    (gsh['ffn_w_up'],) = scatter_behind(lambda c: mm_sh_out("dw_up", h2, d_up, comm=c), swap=['ffn_w_gate'])
    d_x1, dg_ffn_p = rowwise_vjp("rms_ffn_bwd", _rms_fn, [x1, w2d['ffn_norm_g']], ['row', 'const'], [d_h2], ['row'], [0, 1],
                                 heads=1, s=s, tm=tm, plus=d_x2)
    d_x1_b = d_x1.astype(BF16)
    d_ycat = mm_nt("d_ycat", d_x1_b, gathered['w_out'])
    gsh['w_out'] = mm_tn("dw_out", ycat, d_x1_b).reshape((N_DEV,) + w2d['w_out'].shape)
    d_yr_h = _to_heads(d_ycat[:, :c_rwkv], RWKV_HEAD)
    d_o_h = _to_heads(d_ycat[:, c_rwkv:], V_HEAD)

    d_q, d_kv, d_kp_h = scatter_behind(
        lambda c: attn_bwd(q_h, kv_h, kp_rot, cos2, sin2, rot, o_att, lse, d_o_h, comm=c),
        ici=['ffn_w_down'], swap=['ffn_w_up', 'w_out'])
    d_kp_rot = headsum("d_kpe_heads", d_kp_h)
    d_qn = mm_sh_nt("d_qn", d_q, gathered['mla_w_uq'])
    d_kvn = mm_sh_nt("d_kvn", d_kv, gathered['mla_w_ukv'])
    gsh['mla_w_uq'] = mm_sh_out("dw_uq", qn, d_q)
    gsh['mla_w_ukv'] = mm_sh_out("dw_ukv", kvn, d_kv)
    d_cq, d_ckv, d_kpe, dg_q_p, dg_kv_p = rowwise_vjp(
        "mla_pre_bwd", _mla_pre_grad_fn, mla_args, mla_kinds, [d_qn, d_kvn, d_kp_rot], ['row', 'row', 'row'], [0, 1, 2, 4, 5],
        heads=1, s=s, tm=tm)

    d_y, d_r_post, d_k_post, d_v_post, d_gate_r, dgnw_p, dgnb_p, drk_p = scatter_behind(
        lambda c: rowwise_vjp("rwkv_post_bwd", _rwkv_post_fn, post_args, post_kinds, [d_yr_h], ['hrow'], list(range(8)),
                              heads=n_rh, s=s, tm=tm_heads, comm=c),
        ici=['ffn_w_gate'], swap=['mla_w_uq', 'mla_w_ukv'])
    d_r_sc, d_w_sc, d_k_sc, d_v_sc, d_a_sc, d_b_sc = scatter_behind(
        lambda c: scan_bwd(hr, decay, kx, hv, a_sc, b_sc, ckpt, d_y, comm=c), ici=['ffn_w_up', 'w_out'])
    d_hk, d_hw_p, d_ha_p, d_hg_p, dw0_p, dw2_p, da0_p, da2_p, dg2_p, dkk_p, dka_p, d_hr, d_hv = scatter_behind(
        lambda c: rowwise_vjp(
            "rwkv_pre_bwd", _rwkv_pre_grad_fn, pre_args + [hr, hv], pre_kinds + ['hrow', 'hrow'],
            [d_w_sc, d_k_sc, d_k_post, d_a_sc, d_b_sc, d_gate_r, d_r_sc, d_r_post, d_v_sc, d_v_post], ['hrow'] * 10,
            list(range(13)), heads=n_rh, s=s, tm=tm_heads, comm=c),
        ici=['mla_w_uq', 'mla_w_ukv'])
    d_shifted = jnp.concatenate([_from_heads(d_hr), _from_heads(d_hk), _from_heads(d_hv), headsum("d_hw_heads", d_hw_p),
                                 headsum("d_ha_heads", d_ha_p), headsum("d_hg_heads", d_hg_p)], axis=-1)
    d_p_rwkv, dmu_p = token_shift_bwd(p_rwkv, w2d['rwkv_mu'], d_shifted, tm_wide)
    d_proj = pad_cols(jnp.concatenate([d_p_rwkv, d_cq, d_ckv, d_kpe], axis=-1).astype(BF16), nb['w_in'], nbp['w_in'])
    gsh['w_in'] = mm_sh_out("dw_in", h1, d_proj)
    chip_sums['w_in'] = rs_chip_sum('w_in', gsh['w_in'])
    (d_h1,) = scatter_behind(lambda c: mm_sh_nt("d_h1", d_proj, gathered['w_in'], comm=c), ici=['w_in'])
    grad_x, dg_attn_p = rowwise_vjp("rms_attn_bwd", _rms_fn, [xs, w2d['attn_norm_g']], ['row', 'const'], [d_h1], ['row'], [0, 1],
                                    heads=1, s=s, tm=tm, plus=d_x1)

    def from_heads_lora(g):
        return jnp.transpose(g, (1, 0, 2)).reshape(g.shape[1], n_rh * RWKV_HEAD)

    gw = {}
    gw['rwkv_w2'] = from_heads_lora(sum_partials("sum_dw2", dw2_p, True))
    gw['rwkv_a2'] = from_heads_lora(sum_partials("sum_da2", da2_p, True))
    gw['rwkv_g2'] = from_heads_lora(sum_partials("sum_dg2", dg2_p, True))
    dcw_pad = colsum("sum_dconv_w", dcw_p.reshape(dcw_p.shape[0], CONV_W * f_pad)).reshape(CONV_W, f_pad)
    gw['ffn_conv_w'] = unpad_cols(dcw_pad, nb['ffn_w_gate'], nbp['ffn_w_gate'])

    rep = {
        'attn_norm_g': sum_partials("sum_dg_attn", dg_attn_p, False),
        'rwkv_mu': colsum("sum_dmu", dmu_p.reshape(dmu_p.shape[0], shift_dim)),
        'rwkv_w0': sum_partials("sum_dw0", dw0_p, True).reshape(1, c_rwkv),
        'rwkv_a0': sum_partials("sum_da0", da0_p, True).reshape(1, c_rwkv),
        'rwkv_k_k': sum_partials("sum_dkk", dkk_p, True).reshape(1, c_rwkv),
        'rwkv_k_a': sum_partials("sum_dka", dka_p, True).reshape(1, c_rwkv),
        'rwkv_r_k': sum_partials("sum_drk", drk_p, True).reshape(1, c_rwkv),
        'rwkv_gn_w': sum_partials("sum_dgnw", dgnw_p, True).reshape(1, c_rwkv),
        'rwkv_gn_b': sum_partials("sum_dgnb", dgnb_p, True).reshape(1, c_rwkv),
        'mla_q_norm_g': sum_partials("sum_dg_q", dg_q_p, False),
        'mla_kv_norm_g': sum_partials("sum_dg_kv", dg_kv_p, False),
        'ffn_norm_g': sum_partials("sum_dg_ffn", dg_ffn_p, False),
        'ffn_conv_b': unpad_cols(colsum("sum_dconv_b", dcb_p.reshape(dcb_p.shape[0], f_pad)), nb['ffn_w_gate'], nbp['ffn_w_gate']),
        'final_norm_g': sum_partials("sum_dg_final", dg_final_p, False),
        'loss': sum_all("sum_loss", loss_rows.reshape(s // SUBLANES, SUBLANES)),
    }
    rep_pack = Pack([(n, w2d[n].shape) for n in REPLICATED] + [('loss', (1, 1))], 8)
    rep_all = all_gather("gather_rep_grads", rep_pack.pack(rep, F32))
    rep_sum = colsum("sum_rep_grads", rep_all.reshape(N_DEV, rep_pack.total)).reshape(rep_pack.rows, PACK_W)
    rep_g = rep_pack.unpack(rep_sum)
    loss = rep_g.pop('loss').reshape(())

    grads, deltas, new_m, new_v = dict(rep_g), {}, {}, {}
    for n in BIG:
        grads[n], deltas[n], new_m[n], new_v[n] = adamw_scatter(
            "adamw_" + n, wts[n], mom_m[n], mom_v[n], chip_sums[n][0], from_chips[n])
    sm_pack = Pack([(n, w2d[n].shape) for n in SMALL_SHARDED], 8)
    g_shards = {n: _full_to_shards(gw[n], SMALL_SHARDED[n]) for n in SMALL_SHARDED}
    grads.update(sm_pack.unpack(reduce_scatter("small", sm_pack.pack(g_shards, F32, lead=(N_DEV,)))))
    rest_pack = Pack([(n, w2d[n].shape) for n in WEIGHTS if n not in BIG], 8)
    d_r, m_r, v_r = rowwise(
        "adamw_small", _adamw_fn, [rest_pack.pack(w2d, F32), rest_pack.pack(grads, F32), rest_pack.pack(m2d, F32),
                                   rest_pack.pack(v2d, F32)],
        ['row'] * 4, [('row', PACK_W, F32)] * 3, heads=1, s=rest_pack.rows, tm=_pick(rest_pack.rows, 512, SUBLANES))
    deltas.update(rest_pack.unpack(d_r))
    new_m.update(rest_pack.unpack(m_r))
    new_v.update(rest_pack.unpack(v_r))

    def shaped(dct):
        return [dct[n].reshape(out_shapes[n]) for n in WEIGHTS]

    return (loss, grad_x.reshape(x.shape), *shaped(grads), *shaped(deltas), *shaped(new_m), *shaped(new_v))
```

```python
import functools
import math

import jax
import jax.numpy as jnp
import numpy as np
from jax import lax
from jax.experimental import pallas as pl
from jax.experimental.pallas import tpu as pltpu

F32 = jnp.float32
BF16 = jnp.bfloat16
HIGHEST = lax.Precision.HIGHEST
MESH = pl.DeviceIdType.MESH

N_DEV = 8
LANES = 128
SUBLANES = 8
VMEM_LIMIT = 48 * 1024 * 1024
RESIDENT_BYTES = 8 * 1024 * 1024

NORM_EPS = 1e-6
GN_EPS = 64e-5
RWKV_HEAD = 64
QK_NOPE = 128
QK_ROPE = 64
V_HEAD = 128
ROPE_THETA = 10000.0
CONV_W = 3
NEG_INF = -1e30
SCAN_CHUNK = 64
SCAN_HEADS = 16
SCAN_PASSES_SOLVE = 1
SCAN_PASSES_OUT = 3

ADAM_LR = 0.001
ADAM_B1 = 0.9
ADAM_B2 = 0.999
ADAM_EPS = 1e-08
ADAM_WD = 0.01
ADAM_STEP = 10

WEIGHTS = ['attn_norm_g', 'w_in', 'rwkv_mu', 'rwkv_w0', 'rwkv_w2', 'rwkv_a0', 'rwkv_a2', 'rwkv_g2', 'rwkv_k_k',
           'rwkv_k_a', 'rwkv_r_k', 'rwkv_gn_w', 'rwkv_gn_b', 'mla_q_norm_g', 'mla_w_uq', 'mla_kv_norm_g', 'mla_w_ukv',
           'w_out', 'ffn_norm_g', 'ffn_w_gate', 'ffn_w_up', 'ffn_conv_w', 'ffn_conv_b', 'ffn_w_down', 'final_norm_g']
BIG = {'w_in': 'col', 'mla_w_uq': 'col', 'mla_w_ukv': 'col', 'w_out': 'row', 'ffn_w_gate': 'col', 'ffn_w_up': 'col',
       'ffn_w_down': 'row'}
SMALL_SHARDED = {'rwkv_w2': 'col', 'rwkv_a2': 'col', 'rwkv_g2': 'col', 'ffn_conv_w': 'col'}
SHARDED = {**BIG, **SMALL_SHARDED}
REPLICATED = [n for n in WEIGHTS if n not in SHARDED]


def _round_up(n, m):
    return (n + m - 1) // m * m


def _pick(n, cap, unit):
    if n <= cap:
        return n
    best = None
    for t in range(unit, cap + 1, unit):
        if n % t == 0:
            best = t
    assert best is not None, (n, cap, unit)
    return best


def _params(sem):
    return pltpu.CompilerParams(dimension_semantics=sem, vmem_limit_bytes=VMEM_LIMIT)


def mm(name, a, b, add=None, out_dtype=F32, comm=None):
    m, k = a.shape
    k2, n = b.shape
    assert k == k2, (name, a.shape, b.shape)
    tn = n if k * n * 2 <= RESIDENT_BYTES else _pick(n, 640, LANES)
    tm = _pick(m, 2048 if (tn < n and m * k * 2 <= RESIDENT_BYTES) else 512, SUBLANES * 2)
    has_add = add is not None

    def body(a_ref, b_ref, *rest):
        o_ref = rest[-1]
        acc = jnp.dot(a_ref[...].astype(BF16), b_ref[...].astype(BF16), preferred_element_type=F32)
        if has_add:
            acc = acc + rest[0][...].astype(F32)
        o_ref[...] = acc.astype(o_ref.dtype)

    in_specs = [pl.BlockSpec((tm, k), lambda i, j: (i, 0)), pl.BlockSpec((k, tn), lambda i, j: (0, j))]
    ops = [a, b]
    if has_add:
        in_specs.append(pl.BlockSpec((tm, tn), lambda i, j: (i, j)))
        ops.append(add)
    res = _pallas(
        body, name=name, grid=(m // tm, n // tn), in_specs=in_specs,
        out_specs=[pl.BlockSpec((tm, tn), lambda i, j: (i, j))],
        out_shape=[jax.ShapeDtypeStruct((m, n), out_dtype)], sem=("parallel", "parallel"), comm=comm,
    )(*ops)
    return res[0] if comm is None else res


def mm_nt(name, a, b, out_dtype=F32):
    m, k = a.shape
    n, k2 = b.shape
    assert k == k2, (name, a.shape, b.shape)
    tm = _pick(m, 2048 if m * k * 2 <= RESIDENT_BYTES else 512, SUBLANES * 2)
    tn = _pick(n, 1024, LANES)

    def body(a_ref, b_ref, o_ref):
        acc = lax.dot_general(a_ref[...].astype(BF16), b_ref[...].astype(BF16), (((1,), (1,)), ((), ())),
                              preferred_element_type=F32)
        o_ref[...] = acc.astype(o_ref.dtype)

    return pl.pallas_call(
        body, name=name, grid=(m // tm, n // tn),
        in_specs=[pl.BlockSpec((tm, k), lambda i, j: (i, 0)), pl.BlockSpec((tn, k), lambda i, j: (j, 0))],
        out_specs=pl.BlockSpec((tm, tn), lambda i, j: (i, j)),
        out_shape=jax.ShapeDtypeStruct((m, n), out_dtype),
        compiler_params=_params(("parallel", "parallel")),
    )(a, b)


def mm_sh(name, a, g, out_dtype=F32, comm=None, slabs=False):
    m, k = a.shape
    nd, k2, nbp = g.shape
    assert k == k2, (name, a.shape, g.shape)
    tm = _pick(m, 2048 if m * k * 2 <= RESIDENT_BYTES else 512, SUBLANES * 2)

    def body(a_ref, b_ref, o_ref):
        o_ref[...] = jnp.dot(a_ref[...].astype(BF16), b_ref[...].astype(BF16), preferred_element_type=F32).astype(o_ref.dtype)

    if slabs:
        out_spec, out_shape = pl.BlockSpec((None, tm, nbp), lambda i, j: (j, i, 0)), (nd, m, nbp)
    else:
        out_spec, out_shape = pl.BlockSpec((tm, nbp), lambda i, j: (i, j)), (m, nd * nbp)
    res = _pallas(
        body, name=name, grid=(m // tm, nd),
        in_specs=[pl.BlockSpec((tm, k), lambda i, j: (i, 0)), pl.BlockSpec((None, k, nbp), lambda i, j: (j, 0, 0))],
        out_specs=[out_spec], out_shape=[jax.ShapeDtypeStruct(out_shape, out_dtype)], sem=("parallel", "parallel"), comm=comm,
    )(a, g)
    return res[0] if comm is None else res


def mm_sh_nt(name, a, g, add=None, comm=None):
    nd, k, nbp = g.shape
    slabs = a.ndim == 3
    m = a.shape[1] if slabs else a.shape[0]
    assert a.shape == ((nd, m, nbp) if slabs else (m, nd * nbp)), (name, a.shape, g.shape)
    has_add = add is not None
    tm = _pick(m, 512 if has_add else 1024, SUBLANES * 2)

    def body(a_ref, b_ref, *rest):
        o_ref = rest[-1]
        part = lax.dot_general(a_ref[...].astype(BF16), b_ref[...].astype(BF16), (((1,), (1,)), ((), ())),
                               preferred_element_type=F32)

        @pl.when(pl.program_id(1) == 0)
        def _():
            o_ref[...] = part + rest[0][...] if has_add else part

        @pl.when(pl.program_id(1) != 0)
        def _():
            o_ref[...] += part

    a_spec = pl.BlockSpec((None, tm, nbp), lambda i, j: (j, i, 0)) if slabs else pl.BlockSpec((tm, nbp), lambda i, j: (i, j))
    in_specs = [a_spec, pl.BlockSpec((None, k, nbp), lambda i, j: (j, 0, 0))]
    ops = [a, g]
    if has_add:
        in_specs.append(pl.BlockSpec((tm, k), lambda i, j: (i, 0)))
        ops.append(add)
    res = _pallas(
        body, name=name, grid=(m // tm, nd), in_specs=in_specs,
        out_specs=[pl.BlockSpec((tm, k), lambda i, j: (i, 0))],
        out_shape=[jax.ShapeDtypeStruct((m, k), F32)], sem=("parallel", "arbitrary"), comm=comm,
    )(*ops)
    return res[0] if comm is None else res


def mm_tn(name, a, b):
    m, k = a.shape
    m2, n = b.shape
    assert m == m2, (name, a.shape, b.shape)
    tk = _pick(k, 512, LANES)
    tn = n if m * n * 2 <= RESIDENT_BYTES else _pick(n, 640, LANES)

    def body(a_ref, b_ref, o_ref):
        o_ref[...] = lax.dot_general(a_ref[...].astype(BF16), b_ref[...].astype(BF16), (((0,), (0,)), ((), ())),
                                     preferred_element_type=F32)

    return pl.pallas_call(
        body, name=name, grid=(k // tk, n // tn),
        in_specs=[pl.BlockSpec((m, tk), lambda i, j: (0, i)), pl.BlockSpec((m, tn), lambda i, j: (0, j))],
        out_specs=pl.BlockSpec((tk, tn), lambda i, j: (i, j)),
        out_shape=jax.ShapeDtypeStruct((k, n), F32),
        compiler_params=_params(("parallel", "parallel")),
    )(a, b)


def mm_sh_out(name, a, b, comm=None):
    m, k = a.shape
    slabs = b.ndim == 3
    nbp = b.shape[2] if slabs else b.shape[1] // N_DEV
    assert b.shape == ((N_DEV, m, nbp) if slabs else (m, N_DEV * nbp)), (name, a.shape, b.shape)
    tk = _pick(k, 2048 if k * m * 2 <= RESIDENT_BYTES else 512, LANES)
    b_spec = pl.BlockSpec((None, m, nbp), lambda i, j: (j, 0, 0)) if slabs else pl.BlockSpec((m, nbp), lambda i, j: (0, j))

    def body(a_ref, b_ref, o_ref):
        o_ref[...] = lax.dot_general(a_ref[...].astype(BF16), b_ref[...].astype(BF16), (((0,), (0,)), ((), ())),
                                     preferred_element_type=F32)

    res = _pallas(
        body, name=name, grid=(k // tk, N_DEV),
        in_specs=[pl.BlockSpec((m, tk), lambda i, j: (0, i)), b_spec],
        out_specs=[pl.BlockSpec((None, tk, nbp), lambda i, j: (j, i, 0))],
        out_shape=[jax.ShapeDtypeStruct((N_DEV, k, nbp), F32)], sem=("parallel", "parallel"), comm=comm,
    )(a, b)
    return res[0] if comm is None else res


def pad_cols(y, nb, nbp):
    m = y.shape[0]
    if nb == nbp:
        return y
    return jnp.pad(y.reshape(m, N_DEV, nb), ((0, 0), (0, 0), (0, nbp - nb))).reshape(m, N_DEV * nbp)


def unpad_cols(y, nb, nbp):
    m = y.shape[0]
    if nb == nbp:
        return y
    return y.reshape(m, N_DEV, nbp)[:, :, :nb].reshape(m, N_DEV * nb)


def _in_spec(kind, a, tm):
    if kind == 'row':
        return pl.BlockSpec((tm, a.shape[1]), lambda h, i: (i, 0))
    if kind == 'hrow':
        return pl.BlockSpec((None, tm, a.shape[2]), lambda h, i: (h, i, 0))
    if kind == 'const':
        return pl.BlockSpec(a.shape, lambda h, i: (0, 0))
    assert kind == 'hconst', kind
    return pl.BlockSpec((None,) + a.shape[1:], lambda h, i: (h, 0, 0))


def _row_out(kind, c, dtype, heads, s, tm):
    if kind == 'row':
        assert heads == 1
        return jax.ShapeDtypeStruct((s, c), dtype), pl.BlockSpec((tm, c), lambda h, i: (i, 0))
    return jax.ShapeDtypeStruct((heads, s, c), dtype), pl.BlockSpec((None, tm, c), lambda h, i: (h, i, 0))


def rowwise(name, fn, arrs, kinds, outs, *, heads, s, tm, comm=None):
    n_in = len(arrs)

    def body(*refs):
        vals = fn(*[r[...] for r in refs[:n_in]])
        for o, v in zip(refs[n_in:], vals, strict=True):
            o[...] = v.astype(o.dtype)

    shapes, specs = zip(*[_row_out(k, c, dt, heads, s, tm) for k, c, dt in outs])
    return _pallas(
        body, name=name, grid=(heads, s // tm),
        in_specs=[_in_spec(k, a, tm) for k, a in zip(kinds, arrs, strict=True)],
        out_specs=list(specs), out_shape=list(shapes), sem=("parallel", "parallel"), comm=comm,
    )(*arrs)


def rowwise_vjp(name, fn, arrs, kinds, cots, cot_kinds, wrt, *, heads, s, tm, out_dtypes=None, primal=False, comm=None,
                plus=None):
    n_in, n_cot = len(arrs), len(cots)
    nb = s // tm
    out_dtypes = out_dtypes or [F32] * len(wrt)
    extra = [] if plus is None else [plus]

    def body(*refs):
        vals = [r[...] for r in refs[:n_in]]
        cvals = tuple(r[...].astype(F32) for r in refs[n_in:n_in + n_cot])
        outs = refs[n_in + n_cot + len(extra):]

        def f(*dv):
            full = list(vals)
            for j, i in enumerate(wrt):
                full[i] = dv[j]
            return tuple(fn(*full))

        prim, vjp_fn = jax.vjp(f, *[vals[i].astype(F32) for i in wrt])
        grads = list(vjp_fn(cvals))
        if plus is not None:
            grads[0] = grads[0] + refs[n_in + n_cot][...]
        for o, g in zip(outs[:len(wrt)], grads, strict=True):
            o[...] = g.astype(o.dtype)
        if primal:
            for o, p in zip(outs[len(wrt):], prim, strict=True):
                o[...] = p.astype(o.dtype)

    shapes, specs = [], []
    for i, dt in zip(wrt, out_dtypes, strict=True):
        kind, a = kinds[i], arrs[i]
        if kind in ('row', 'hrow'):
            c = a.shape[-1]
            sh, sp = _row_out('row' if (kind == 'row' and heads == 1) else 'hrow', c, dt, heads, s, tm)
        else:
            r, c = a.shape[-2:]
            sh = jax.ShapeDtypeStruct((heads, nb, r, c), dt)
            sp = pl.BlockSpec((None, None, r, c), lambda h, i: (h, i, 0, 0))
        shapes.append(sh)
        specs.append(sp)
    if primal:
        for ck, c in zip(cot_kinds, cots, strict=True):
            sh, sp = _row_out(ck, c.shape[-1], F32, heads, s, tm)
            shapes.append(sh)
            specs.append(sp)
    in_specs = [_in_spec(k, a, tm) for k, a in zip(kinds, arrs, strict=True)]
    in_specs += [_in_spec(k, a, tm) for k, a in zip(cot_kinds, cots, strict=True)]
    in_specs += [_in_spec('row', a, tm) for a in extra]
    return _pallas(
        body, name=name, grid=(heads, nb), in_specs=in_specs, out_specs=specs, out_shape=shapes,
        sem=("parallel", "parallel"), comm=comm,
    )(*arrs, *cots, *extra)


def colsum(name, x):
    n, m = x.shape
    tc = _pick(m, 32768, LANES) if m % LANES == 0 else m

    def body(x_ref, o_ref):
        acc = x_ref[0:1, :].astype(F32)
        for r in range(1, n):
            acc = acc + x_ref[r:r + 1, :].astype(F32)
        o_ref[...] = acc

    return pl.pallas_call(
        body, name=name, grid=(m // tc,), in_specs=[pl.BlockSpec((n, tc), lambda j: (0, j))],
        out_specs=pl.BlockSpec((1, tc), lambda j: (0, j)), out_shape=jax.ShapeDtypeStruct((1, m), F32),
        compiler_params=_params(("parallel",)),
    )(x)


def headsum(name, x):
    h, s, c = x.shape
    tm = _pick(s, 256, SUBLANES)

    def body(x_ref, o_ref):
        acc = x_ref[0]
        for j in range(1, h):
            acc = acc + x_ref[j]
        o_ref[...] = acc

    return pl.pallas_call(
        body, name=name, grid=(s // tm,), in_specs=[pl.BlockSpec((h, tm, c), lambda i: (0, i, 0))],
        out_specs=pl.BlockSpec((tm, c), lambda i: (i, 0)), out_shape=jax.ShapeDtypeStruct((s, c), F32),
        compiler_params=_params(("parallel",)),
    )(x)


def sum_all(name, x):
    def body(x_ref, o_ref):
        o_ref[...] = jnp.sum(x_ref[...], keepdims=True)

    return pl.pallas_call(body, name=name, out_shape=jax.ShapeDtypeStruct((1, 1), F32))(x)


def sum_partials(name, p, per_head):
    h, nb, r, c = p.shape
    if per_head:
        flat = jnp.transpose(p, (1, 0, 2, 3)).reshape(nb, h * r * c)
        if nb == 1:
            return flat.reshape(h, r, c)
        return colsum(name, flat).reshape(h, r, c)
    flat = p.reshape(h * nb, r * c)
    if h * nb == 1:
        return flat.reshape(r, c)
    return colsum(name, flat).reshape(r, c)


def _rms_fn(x, g):
    xf = x.astype(F32)
    return (xf * lax.rsqrt(jnp.mean(xf * xf, axis=-1, keepdims=True) + NORM_EPS) * g,)


def _softplus(z):
    return jnp.maximum(z, 0.0) + jnp.log(1.0 + jnp.exp(-jnp.abs(z)))


def _rwkv_pre_fn(hk, hw, ha, hg, w0, w2, a0, a2, g2, k_k, k_a):
    zw = w0 + jnp.dot(jnp.tanh(hw), w2, preferred_element_type=F32)
    w_log = -_softplus(-zw) - 0.5
    decay = jnp.exp(-jnp.exp(w_log))
    a = jax.nn.sigmoid(a0 + jnp.dot(ha, a2, preferred_element_type=F32))
    g = jnp.dot(jax.nn.sigmoid(hg), g2, preferred_element_type=F32)
    kk = hk * k_k
    kk = kk * lax.rsqrt(jnp.maximum(jnp.sum(kk * kk, axis=-1, keepdims=True), 1e-24))
    k = hk * (1.0 + (a - 1.0) * k_a)
    return decay, k, -kk, kk * a, g


def _rwkv_pre_grad_fn(hk, hw, ha, hg, w0, w2, a0, a2, g2, k_k, k_a, hr, hv):
    decay, k, a_sc, b_sc, g = _rwkv_pre_fn(hk, hw, ha, hg, w0, w2, a0, a2, g2, k_k, k_a)
    return decay, k, k, a_sc, b_sc, g, hr, hr, hv, hv


def _rwkv_post_fn(y, r, k, v, g, gn_w, gn_b, r_k):
    mu = jnp.mean(y, axis=-1, keepdims=True)
    var = jnp.mean(jnp.square(y - mu), axis=-1, keepdims=True)
    yn = (y - mu) * lax.rsqrt(var + GN_EPS) * gn_w + gn_b
    bonus = jnp.sum(r * k * r_k, axis=-1, keepdims=True) * v
    return ((yn + bonus) * g,)


def _rope_tables(pos, inv_freq2):
    ang = pos * inv_freq2
    return jnp.cos(ang), jnp.sin(ang)


def _rope(t, cos2, sin2, rot):
    return t * cos2 + jnp.dot(t, rot, precision=HIGHEST, preferred_element_type=F32) * sin2


def _mla_pre_fn(c_q, c_kv, k_pe, pos, q_g, kv_g, inv_freq2, rot):
    cos2, sin2 = _rope_tables(pos, inv_freq2)
    return _rms_fn(c_q, q_g)[0], _rms_fn(c_kv, kv_g)[0], _rope(k_pe, cos2, sin2, rot), cos2, sin2


def _mla_pre_grad_fn(c_q, c_kv, k_pe, pos, q_g, kv_g, inv_freq2, rot):
    return _mla_pre_fn(c_q, c_kv, k_pe, pos, q_g, kv_g, inv_freq2, rot)[:3]


def _loss_fn(x2, g, target):
    y = _rms_fn(x2, g)[0]
    return (0.5 * jnp.mean(jnp.square(y - target), axis=-1, keepdims=True),)


def _adamw_fn(w, g, m, v):
    m = ADAM_B1 * m + (1.0 - ADAM_B1) * g
    v = ADAM_B2 * v + (1.0 - ADAM_B2) * jnp.square(g)
    m_hat = m / (1.0 - ADAM_B1 ** ADAM_STEP)
    v_hat = v / (1.0 - ADAM_B2 ** ADAM_STEP)
    delta = -ADAM_LR * (m_hat / (jnp.sqrt(v_hat) + ADAM_EPS) + ADAM_WD * w)
    return delta, m, v


def _prev_halo_spec(c, tm):
    return pl.BlockSpec((SUBLANES, c), lambda i: (jnp.maximum(i * (tm // SUBLANES) - 1, 0), 0))


def _next_halo_spec(c, tm, s):
    return pl.BlockSpec((SUBLANES, c), lambda i: (jnp.minimum((i + 1) * (tm // SUBLANES), s // SUBLANES - 1), 0))


def _shift_down(p, halo, first_block, n):
    out = pltpu.roll(p, n, 0)
    row = lax.broadcasted_iota(jnp.int32, p.shape, 0)
    for j in range(n):
        top = jnp.where(first_block, 0.0, halo[SUBLANES - n + j:SUBLANES - n + j + 1, :])
        out = jnp.where(row == j, top, out)
    return out


def _shift_up(p, halo, last_block, n):
    rows = p.shape[0]
    out = pltpu.roll(p, rows - n, 0)
    row = lax.broadcasted_iota(jnp.int32, p.shape, 0)
    for j in range(n):
        bot = jnp.where(last_block, 0.0, halo[j:j + 1, :])
        out = jnp.where(row == rows - n + j, bot, out)
    return out


def token_shift_fwd(p, mu, tm):
    s, c = p.shape

    def body(p_ref, halo_ref, mu_ref, o_ref):
        pv = p_ref[...]
        prev = _shift_down(pv, halo_ref[...], pl.program_id(0) == 0, 1)
        o_ref[...] = pv + (prev - pv) * mu_ref[...]

    return pl.pallas_call(
        body, name="token_shift_fwd", grid=(s // tm,),
        in_specs=[pl.BlockSpec((tm, c), lambda i: (i, 0)), _prev_halo_spec(c, tm), pl.BlockSpec((1, c), lambda i: (0, 0))],
        out_specs=pl.BlockSpec((tm, c), lambda i: (i, 0)), out_shape=jax.ShapeDtypeStruct((s, c), F32),
        compiler_params=_params(("parallel",)),
    )(p, p, mu)


def token_shift_bwd(p, mu, ds, tm):
    s, c = p.shape
    nb = s // tm

    def body(p_ref, halo_ref, mu_ref, ds_ref, dsn_ref, dp_ref, dmu_ref):
        i = pl.program_id(0)
        pv, dsv, muv = p_ref[...], ds_ref[...], mu_ref[...]
        prev = _shift_down(pv, halo_ref[...], i == 0, 1)
        nxt = _shift_up(dsv, dsn_ref[...], i == nb - 1, 1)
        dp_ref[...] = dsv * (1.0 - muv) + nxt * muv
        dmu_ref[...] = jnp.sum(dsv * (prev - pv), axis=0, keepdims=True)

    return pl.pallas_call(
        body, name="token_shift_bwd", grid=(nb,),
        in_specs=[pl.BlockSpec((tm, c), lambda i: (i, 0)), _prev_halo_spec(c, tm), pl.BlockSpec((1, c), lambda i: (0, 0)),
                  pl.BlockSpec((tm, c), lambda i: (i, 0)), _next_halo_spec(c, tm, s)],
        out_specs=[pl.BlockSpec((tm, c), lambda i: (i, 0)), pl.BlockSpec((None, 1, c), lambda i: (i, 0, 0))],
        out_shape=[jax.ShapeDtypeStruct((s, c), F32), jax.ShapeDtypeStruct((nb, 1, c), F32)],
        compiler_params=_params(("parallel",)),
    )(p, p, mu, ds, ds)


def _ffn_tiles(s, f):
    return _pick(s, 256, SUBLANES), _pick(f, 1408, LANES)


def _conv_gate(gp, halo, first_block, cw, cb):
    p1 = _shift_down(gp, halo, first_block, 1)
    p2 = _shift_down(gp, halo, first_block, 2)
    return cw[0:1, :] * p2 + cw[1:2, :] * p1 + cw[2:3, :] * gp + cb, p1, p2


def ffn_act_fwd(gate_pre, up, conv_w, conv_b):
    s, f = gate_pre.shape
    tm, tc = _ffn_tiles(s, f)

    def body(gp_ref, halo_ref, up_ref, cw_ref, cb_ref, o_ref):
        gate, _, _ = _conv_gate(gp_ref[...], halo_ref[...], pl.program_id(0) == 0, cw_ref[...], cb_ref[...])
        o_ref[...] = (gate * jax.nn.sigmoid(gate) * up_ref[...]).astype(o_ref.dtype)

    blk = pl.BlockSpec((tm, tc), lambda i, j: (i, j))
    return pl.pallas_call(
        body, name="ffn_act_fwd", grid=(s // tm, f // tc),
        in_specs=[blk, pl.BlockSpec((SUBLANES, tc), lambda i, j: (jnp.maximum(i * (tm // SUBLANES) - 1, 0), j)), blk,
                  pl.BlockSpec((CONV_W, tc), lambda i, j: (0, j)), pl.BlockSpec((1, tc), lambda i, j: (0, j))],
        out_specs=blk, out_shape=jax.ShapeDtypeStruct((s, f), BF16),
        compiler_params=_params(("parallel", "parallel")),
    )(gate_pre, gate_pre, up, conv_w, conv_b)


def ffn_act_bwd1(gate_pre, up, conv_w, conv_b, d_act):
    s, f = gate_pre.shape
    tm, tc = _ffn_tiles(s, f)
    nb = s // tm

    def body(gp_ref, halo_ref, up_ref, cw_ref, cb_ref, da_ref, dg_ref, du_ref, dcw_ref, dcb_ref):
        gp = gp_ref[...]
        gate, p1, p2 = _conv_gate(gp, halo_ref[...], pl.program_id(0) == 0, cw_ref[...], cb_ref[...])
        sig = jax.nn.sigmoid(gate)
        da = da_ref[...].astype(F32)
        du_ref[...] = (da * gate * sig).astype(du_ref.dtype)
        dg = da * up_ref[...] * (sig * (1.0 + gate * (1.0 - sig)))
        dg_ref[...] = dg
        dcb_ref[...] = jnp.sum(dg, axis=0, keepdims=True)
        dcw_ref[0:1, :] = jnp.sum(dg * p2, axis=0, keepdims=True)
        dcw_ref[1:2, :] = jnp.sum(dg * p1, axis=0, keepdims=True)
        dcw_ref[2:3, :] = jnp.sum(dg * gp, axis=0, keepdims=True)

    blk = pl.BlockSpec((tm, tc), lambda i, j: (i, j))
    return pl.pallas_call(
        body, name="ffn_act_bwd1", grid=(nb, f // tc),
        in_specs=[blk, pl.BlockSpec((SUBLANES, tc), lambda i, j: (jnp.maximum(i * (tm // SUBLANES) - 1, 0), j)), blk,
                  pl.BlockSpec((CONV_W, tc), lambda i, j: (0, j)), pl.BlockSpec((1, tc), lambda i, j: (0, j)), blk],
        out_specs=[blk, blk, pl.BlockSpec((None, CONV_W, tc), lambda i, j: (i, 0, j)),
                   pl.BlockSpec((None, 1, tc), lambda i, j: (i, 0, j))],
        out_shape=[jax.ShapeDtypeStruct((s, f), F32), jax.ShapeDtypeStruct((s, f), BF16),
                   jax.ShapeDtypeStruct((nb, CONV_W, f), F32), jax.ShapeDtypeStruct((nb, 1, f), F32)],
        compiler_params=_params(("parallel", "parallel")),
    )(gate_pre, gate_pre, up, conv_w, conv_b, d_act)


def ffn_act_bwd2(d_gate, conv_w):
    s, f = d_gate.shape
    tm, tc = _ffn_tiles(s, f)
    nb = s // tm

    def body(dg_ref, halo_ref, cw_ref, o_ref):
        dg, cw = dg_ref[...], cw_ref[...]
        last = pl.program_id(0) == nb - 1
        n1 = _shift_up(dg, halo_ref[...], last, 1)
        n2 = _shift_up(dg, halo_ref[...], last, 2)
        o_ref[...] = (cw[2:3, :] * dg + cw[1:2, :] * n1 + cw[0:1, :] * n2).astype(o_ref.dtype)

    blk = pl.BlockSpec((tm, tc), lambda i, j: (i, j))
    return pl.pallas_call(
        body, name="ffn_act_bwd2", grid=(nb, f // tc),
        in_specs=[blk, pl.BlockSpec((SUBLANES, tc), lambda i, j: (jnp.minimum((i + 1) * (tm // SUBLANES), s // SUBLANES - 1), j)),
                  pl.BlockSpec((CONV_W, tc), lambda i, j: (0, j))],
        out_specs=blk, out_shape=jax.ShapeDtypeStruct((s, f), BF16),
        compiler_params=_params(("parallel", "parallel")),
    )(d_gate, d_gate, conv_w)


def _mxu(x, y, cx, cy):
    if x.ndim == 3:
        return lax.dot_general(x, y, (((cx + 1,), (cy + 1,)), ((0,), (0,))), preferred_element_type=F32)
    return lax.dot_general(x, y, (((cx,), (cy,)), ((), ())), preferred_element_type=F32)


def _split(x):
    hi = x.astype(BF16)
    return hi, (x - hi.astype(F32)).astype(BF16)


def _make_dot3(cx, cy, passes):
    @jax.custom_vjp
    def f(x, y):
        if passes == 1:
            return _mxu(x.astype(BF16), y.astype(BF16), cx, cy)
        xh, xl = _split(x)
        yh, yl = _split(y)
        return _mxu(xh, yh, cx, cy) + (_mxu(xh, yl, cx, cy) + _mxu(xl, yh, cx, cy))

    def fwd(x, y):
        return f(x, y), (x, y)

    def bwd(res, g):
        x, y = res
        dx = dot3(g, y, 1, 1 - cy, passes) if cx == 1 else dot3(y, g, 1 - cy, 1, passes)
        dy = dot3(x, g, 1 - cx, 0, passes) if cy == 0 else dot3(g, x, 0, 1 - cx, passes)
        return dx, dy

    f.defvjp(fwd, bwd)
    return f


_DOT3 = {}


def dot3(x, y, cx, cy, passes=3):
    if (cx, cy, passes) not in _DOT3:
        _DOT3[(cx, cy, passes)] = _make_dot3(cx, cy, passes)
    return _DOT3[(cx, cy, passes)](x, y)


def _dot(x, y, passes=3):
    return dot3(x, y, 1, 0, passes)


def _dot_nt(x, y, passes=3):
    return dot3(x, y, 1, 1, passes)


def _dot_tn(x, y, passes=3):
    return dot3(x, y, 0, 0, passes)


def _tri_sum(x, lower):
    t = x.shape[-2]
    row = lax.broadcasted_iota(jnp.int32, (t, t), 0)
    col = lax.broadcasted_iota(jnp.int32, (t, t), 1)
    tri = jnp.where((col <= row) if lower else (col >= row), 1.0, 0.0).astype(BF16)
    if x.ndim == 3:
        tri = jnp.broadcast_to(tri[None], (x.shape[0], t, t))
    hi = x.astype(BF16)
    rest = x - hi.astype(F32)
    mid = rest.astype(BF16)
    low = (rest - mid.astype(F32)).astype(BF16)
    return _mxu(tri, hi, 1, 0) + (_mxu(tri, mid, 1, 0) + _mxu(tri, low, 1, 0))


@jax.custom_vjp
def _cumsum_rows(x):
    return _tri_sum(x, True)


_cumsum_rows.defvjp(lambda x: (_tri_sum(x, True), None), lambda _, g: (_tri_sum(g, False),))


def _scan_chunk(s0, r, w, k, v, a, b):
    t = r.shape[1]
    row = lax.broadcasted_iota(jnp.int32, (1, t, t), 1)
    col = lax.broadcasted_iota(jnp.int32, (1, t, t), 2)
    strict, incl = col < row, col <= row
    logw = jnp.log(w)
    cum = _cumsum_rows(logw)
    w_in, w_ex, w_inv = jnp.exp(cum), jnp.exp(cum - logw), jnp.exp(-cum)
    w_all = jnp.exp(jnp.sum(logw, axis=1, keepdims=True))
    at, rt, kt, bt = a * w_ex, r * w_in, k * w_inv, b * w_inv
    ps, po = SCAN_PASSES_SOLVE, SCAN_PASSES_OUT
    a_ab = jnp.where(strict, _dot_nt(at, bt, ps), 0.0)
    a_ak = jnp.where(strict, _dot_nt(at, kt, ps), 0.0)
    a_rk = jnp.where(incl, _dot_nt(rt, kt, po), 0.0)
    a_rb = jnp.where(incl, _dot_nt(rt, bt, po), 0.0)
    u = _dot_nt(at, s0, ps) + _dot(a_ak, v, ps)
    p = a_ab
    steps = int(math.log2(t))
    assert 2 ** steps == t
    for j in range(steps):
        u = u + _dot(p, u, ps)
        if j < steps - 1:
            p = _dot(p, p, ps)
    y = _dot_nt(rt, s0, po) + _dot(a_rk, v, po) + _dot(a_rb, u, po)
    s_new = s0 * w_all + _dot_tn(v, kt * w_all, po) + _dot_tn(u, bt * w_all, po)
    return y, s_new


def scan_fwd(r, w, k, v, a, b, comm=None):
    h, s, n = r.shape
    t = min(SCAN_CHUNK, s)
    nc = s // t

    hb = SCAN_HEADS if h % SCAN_HEADS == 0 else 1

    def body(r_ref, w_ref, k_ref, v_ref, a_ref, b_ref, y_ref, ck_ref, st_ref):
        @pl.when(pl.program_id(1) == 0)
        def _():
            st_ref[...] = jnp.zeros_like(st_ref)

        s0 = st_ref[...]
        ck_ref[...] = s0
        y, s_new = _scan_chunk(s0, r_ref[...], w_ref[...], k_ref[...], v_ref[...], a_ref[...], b_ref[...])
        y_ref[...] = y
        st_ref[...] = s_new

    blk = pl.BlockSpec((hb, t, n), lambda hh, c: (hh, c, 0))
    return _pallas(
        body, name="rwkv_scan_fwd", grid=(h // hb, nc), in_specs=[blk] * 6,
        out_specs=[blk, pl.BlockSpec((hb, None, n, n), lambda hh, c: (hh, c, 0, 0))],
        out_shape=[jax.ShapeDtypeStruct((h, s, n), F32), jax.ShapeDtypeStruct((h, nc, n, n), F32)],
        scratch_shapes=[pltpu.VMEM((hb, n, n), F32)], sem=("parallel", "arbitrary"), comm=comm,
    )(r, w, k, v, a, b)


def scan_bwd(r, w, k, v, a, b, ck, dy, comm=None):
    h, s, n = r.shape
    t = min(SCAN_CHUNK, s)
    nc = s // t

    hb = SCAN_HEADS if h % SCAN_HEADS == 0 else 1

    def body(r_ref, w_ref, k_ref, v_ref, a_ref, b_ref, ck_ref, dy_ref, dr_ref, dw_ref, dk_ref, dv_ref, da_ref, db_ref, ds_ref):
        @pl.when(pl.program_id(1) == 0)
        def _():
            ds_ref[...] = jnp.zeros_like(ds_ref)

        _, vjp_fn = jax.vjp(_scan_chunk, ck_ref[...], r_ref[...], w_ref[...], k_ref[...], v_ref[...], a_ref[...], b_ref[...])
        ds0, dr, dw, dk, dv, da, db = vjp_fn((dy_ref[...], ds_ref[...]))
        ds_ref[...] = ds0
        dr_ref[...], dw_ref[...], dk_ref[...], dv_ref[...], da_ref[...], db_ref[...] = dr, dw, dk, dv, da, db

    blk = pl.BlockSpec((hb, t, n), lambda hh, c: (hh, nc - 1 - c, 0))
    return _pallas(
        body, name="rwkv_scan_bwd", grid=(h // hb, nc),
        in_specs=[blk] * 6 + [pl.BlockSpec((hb, None, n, n), lambda hh, c: (hh, nc - 1 - c, 0, 0)), blk],
        out_specs=[blk] * 6, out_shape=[jax.ShapeDtypeStruct((h, s, n), F32)] * 6,
        scratch_shapes=[pltpu.VMEM((hb, n, n), F32)], sem=("parallel", "arbitrary"), comm=comm,
    )(r, w, k, v, a, b, ck, dy)


ATTN_BLOCK = 256
ATTN_LEVELS = 8
ATTN_SLAB = 2 * LANES


def _attn_specs(h, s, tq):
    qblk = lambda c, part: pl.BlockSpec((None, tq, c), lambda hh, i: (hh, i, part))
    kblk = lambda part: pl.BlockSpec((None, s, LANES), lambda hh, i: (hh, 0, part))
    row64 = pl.BlockSpec((tq, QK_ROPE), lambda hh, i: (i, 0))
    return [qblk(LANES, 0), qblk(LANES, 1), kblk(0), kblk(1), pl.BlockSpec((s, QK_ROPE), lambda hh, i: (0, 0)),
            row64, row64, pl.BlockSpec((QK_ROPE, QK_ROPE), lambda hh, i: (0, 0))]


def _attn_levels(s, tq):
    nq = s // tq
    n_lev = min(ATTN_LEVELS, nq)
    per = nq // n_lev
    return [(lv * per, (lv + 1) * per, (lv + 1) * per * tq) for lv in range(n_lev)]


def _attn_scores(qn_b, qp_b, kn_ref, kp_ref, klen, i, tq):
    scale = (QK_NOPE + QK_ROPE) ** -0.5
    kn_b = kn_ref[0:klen, :].astype(BF16)
    kp_b = kp_ref[0:klen, :].astype(BF16)
    sc = lax.dot_general(qn_b, kn_b, (((1,), (1,)), ((), ())), preferred_element_type=F32)
    sc = sc + lax.dot_general(qp_b, kp_b, (((1,), (1,)), ((), ())), preferred_element_type=F32)
    row = i * tq + lax.broadcasted_iota(jnp.int32, sc.shape, 0)
    col = lax.broadcasted_iota(jnp.int32, sc.shape, 1)
    return jnp.where(row >= col, sc * scale, NEG_INF), scale, kn_b, kp_b


def attn_fwd(q_h, kv_h, kp, cos2, sin2, rot, comm=None):
    h, s, _ = q_h.shape
    tq = _pick(s, ATTN_BLOCK, SUBLANES)

    def body(qn_ref, qp_ref, kn_ref, v_ref, kp_ref, cos_ref, sin_ref, rot_ref, o_ref, lse_ref):
        i = pl.program_id(1)
        qn_b = qn_ref[...].astype(BF16)
        qp_b = _rope(qp_ref[:, :QK_ROPE], cos_ref[...], sin_ref[...], rot_ref[...]).astype(BF16)

        def level(klen):
            sc, _, _, _ = _attn_scores(qn_b, qp_b, kn_ref, kp_ref, klen, i, tq)
            mx = jnp.max(sc, axis=-1, keepdims=True)
            e = jnp.exp(sc - mx)
            den = jnp.sum(e, axis=-1, keepdims=True)
            o_ref[...] = jnp.dot((e / den).astype(BF16), v_ref[0:klen, :].astype(BF16), preferred_element_type=F32)
            lse_ref[...] = mx + jnp.log(den)

        for lo, hi, klen in _attn_levels(s, tq):
            pl.when((i >= lo) & (i < hi))(functools.partial(level, klen))

    oblk = lambda c: pl.BlockSpec((None, tq, c), lambda hh, i: (hh, i, 0))
    return _pallas(
        body, name="mla_attn_fwd", grid=(h, s // tq), in_specs=_attn_specs(h, s, tq),
        out_specs=[oblk(V_HEAD), oblk(1)],
        out_shape=[jax.ShapeDtypeStruct((h, s, V_HEAD), F32), jax.ShapeDtypeStruct((h, s, 1), F32)],
        sem=("parallel", "parallel"), comm=comm,
    )(q_h, q_h, kv_h, kv_h, kp, cos2, sin2, rot)


def attn_bwd(q_h, kv_h, kp, cos2, sin2, rot, o, lse, do, comm=None):
    h, s, _ = q_h.shape
    tq = _pick(s, ATTN_BLOCK, SUBLANES)
    nq = s // tq

    def body(qn_ref, qp_ref, kn_ref, v_ref, kp_ref, cos_ref, sin_ref, rot_ref, o_ref, lse_ref, do_ref,
             dq_ref, dkv_ref, dkp_ref, dkv_acc, dkp_acc):
        i = pl.program_id(1)

        @pl.when(i == 0)
        def _():
            dkv_acc[...] = jnp.zeros_like(dkv_acc)
            dkp_acc[...] = jnp.zeros_like(dkp_acc)

        cosv, sinv, rotv = cos_ref[...], sin_ref[...], rot_ref[...]
        qn_b = qn_ref[...].astype(BF16)
        qp_b = _rope(qp_ref[:, :QK_ROPE], cosv, sinv, rotv).astype(BF16)
        dov = do_ref[...]
        do_b = dov.astype(BF16)
        delta = jnp.sum(dov * o_ref[...], axis=-1, keepdims=True)
        lsev = lse_ref[...]

        def level(klen):
            sc, scale, kn_b, kp_b = _attn_scores(qn_b, qp_b, kn_ref, kp_ref, klen, i, tq)
            p = jnp.exp(sc - lsev)
            dp = lax.dot_general(do_b, v_ref[0:klen, :].astype(BF16), (((1,), (1,)), ((), ())), preferred_element_type=F32)
            ds = (p * (dp - delta) * scale).astype(BF16)
            dkv_acc[0:klen, :LANES] += lax.dot_general(ds, qn_b, (((0,), (0,)), ((), ())), preferred_element_type=F32)
            dkv_acc[0:klen, LANES:] += lax.dot_general(p.astype(BF16), do_b, (((0,), (0,)), ((), ())), preferred_element_type=F32)
            dkp_acc[0:klen, :] += lax.dot_general(ds, qp_b, (((0,), (0,)), ((), ())), preferred_element_type=F32)
            dqp = jnp.dot(ds, kp_b, preferred_element_type=F32)
            dqp_raw = dqp * cosv + lax.dot_general(dqp * sinv, rotv, (((1,), (1,)), ((), ())), precision=HIGHEST,
                                                   preferred_element_type=F32)
            dq_ref[:, :QK_NOPE] = jnp.dot(ds, kn_b, preferred_element_type=F32).astype(dq_ref.dtype)
            dq_ref[:, QK_NOPE:QK_NOPE + QK_ROPE] = dqp_raw.astype(dq_ref.dtype)
            dq_ref[:, QK_NOPE + QK_ROPE:] = jnp.zeros((tq, ATTN_SLAB - QK_NOPE - QK_ROPE), dq_ref.dtype)

        for lo, hi, klen in _attn_levels(s, tq):
            pl.when((i >= lo) & (i < hi))(functools.partial(level, klen))

        @pl.when(i == nq - 1)
        def _():
            dkv_ref[...] = dkv_acc[...].astype(dkv_ref.dtype)
            dkp_ref[...] = dkp_acc[...]

    rblk = lambda c: pl.BlockSpec((None, tq, c), lambda hh, i: (hh, i, 0))
    sblk = lambda c: pl.BlockSpec((None, s, c), lambda hh, i: (hh, 0, 0))
    return _pallas(
        body, name="mla_attn_bwd", grid=(h, nq),
        in_specs=_attn_specs(h, s, tq) + [rblk(V_HEAD), rblk(1), rblk(V_HEAD)],
        out_specs=[rblk(ATTN_SLAB), sblk(ATTN_SLAB), sblk(QK_ROPE)],
        out_shape=[jax.ShapeDtypeStruct((h, s, ATTN_SLAB), BF16), jax.ShapeDtypeStruct((h, s, ATTN_SLAB), BF16),
                   jax.ShapeDtypeStruct((h, s, QK_ROPE), F32)],
        scratch_shapes=[pltpu.VMEM((s, ATTN_SLAB), F32), pltpu.VMEM((s, QK_ROPE), F32)],
        sem=("parallel", "arbitrary"), comm=comm,
    )(q_h, q_h, kv_h, kv_h, kp, cos2, sin2, rot, o, lse, do)


def _my_pos():
    return lax.axis_index("x"), lax.axis_index("y"), lax.axis_index("c")


def _dev_index(px, py, pc):
    return 4 * px + 2 * py + pc


def all_gather(name, shard):
    r, c = shard.shape

    def body(x_ref, out_ref, send_sems, recv_sems, local_sem):
        x, y, cc = _my_pos()
        me, sibling = (x, y, cc), (x, y, 1 - cc)
        chips = [(1 - x, y), (x, 1 - y), (1 - x, 1 - y)]

        def rows(px, py, pc):
            return out_ref.at[_dev_index(px, py, pc)]

        def copy(kk, block, to, src=None):
            return pltpu.make_async_remote_copy(
                src_ref=rows(*block) if src is None else src, dst_ref=rows(*block),
                send_sem=send_sems.at[kk], recv_sem=recv_sems.at[kk], device_id=to, device_id_type=MESH)

        mine = pltpu.make_async_copy(x_ref, rows(*me), local_sem)
        mine.start()
        first = [copy(0, me, sibling, src=x_ref)]
        first += [copy(1 + j, me, (*chip, cc), src=x_ref) for j, chip in enumerate(chips)]
        for cp in first:
            cp.start()
        passed = [copy(4 + j, (*chip, cc), sibling) for j, chip in enumerate(chips)]
        for j, chip in enumerate(chips):
            copy(1 + j, (*chip, cc), me).wait_recv()
            passed[j].start()
        copy(0, sibling, me).wait_recv()
        for j, chip in enumerate(chips):
            copy(4 + j, (*chip, 1 - cc), me).wait_recv()
        for cp in first + passed:
            cp.wait_send()
        mine.wait()

    return pl.pallas_call(
        body, name=name, out_shape=jax.ShapeDtypeStruct((N_DEV, r, c), shard.dtype),
        in_specs=[pl.BlockSpec(memory_space=pl.ANY)], out_specs=pl.BlockSpec(memory_space=pl.ANY),
        scratch_shapes=[pltpu.SemaphoreType.DMA((7,)), pltpu.SemaphoreType.DMA((7,)), pltpu.SemaphoreType.DMA],
    )(shard)


def exchange_sibling(name, g):
    _, r, c = g.shape

    def body(g_ref, out_ref, send_sems, recv_sems):
        x, y, cc = _my_pos()
        copies = []
        for px in range(2):
            for py in range(2):
                slot = 2 * px + py
                copies.append(pltpu.make_async_remote_copy(
                    src_ref=g_ref.at[_dev_index(px, py, 1 - cc)], dst_ref=out_ref.at[slot],
                    send_sem=send_sems.at[slot], recv_sem=recv_sems.at[slot], device_id=(x, y, 1 - cc), device_id_type=MESH))
        for cp in copies:
            cp.start()
        for cp in copies:
            cp.wait()

    return pl.pallas_call(
        body, name=name, out_shape=jax.ShapeDtypeStruct((4, r, c), g.dtype),
        in_specs=[pl.BlockSpec(memory_space=pl.ANY)], out_specs=pl.BlockSpec(memory_space=pl.ANY),
        scratch_shapes=[pltpu.SemaphoreType.DMA((4,)), pltpu.SemaphoreType.DMA((4,))],
    )(g)


def exchange_chips(name, hsum):
    _, r, c = hsum.shape

    def body(h_ref, out_ref, send_sems, recv_sems):
        x, y, cc = _my_pos()
        copies = []
        for j, (px, py) in enumerate([(1 - x, y), (x, 1 - y), (1 - x, 1 - y)]):
            copies.append(pltpu.make_async_remote_copy(
                src_ref=h_ref.at[2 * px + py], dst_ref=out_ref.at[j],
                send_sem=send_sems.at[j], recv_sem=recv_sems.at[j], device_id=(px, py, cc), device_id_type=MESH))
        for cp in copies:
            cp.start()
        for cp in copies:
            cp.wait()

    return pl.pallas_call(
        body, name=name, out_shape=jax.ShapeDtypeStruct((3, r, c), hsum.dtype),
        in_specs=[pl.BlockSpec(memory_space=pl.ANY)], out_specs=pl.BlockSpec(memory_space=pl.ANY),
        scratch_shapes=[pltpu.SemaphoreType.DMA((3,)), pltpu.SemaphoreType.DMA((3,))],
    )(hsum)


def _rs_add_sibling(name, g, from_sibling):
    _, r, c = g.shape
    tr = _pick(r, 512, SUBLANES * 2)
    x, y, cc = _my_pos()

    def body(cc_ref, slot_ref, g_ref, s_ref, o_ref, ob_ref):
        tot = g_ref[...] + s_ref[...]
        ob_ref[...] = tot.astype(BF16)

        @pl.when(pl.program_id(1) == slot_ref[0])
        def _():
            o_ref[...] = tot

    return pl.pallas_call(
        body, name=name,
        grid_spec=pltpu.PrefetchScalarGridSpec(
            num_scalar_prefetch=2, grid=(r // tr, 4),
            in_specs=[pl.BlockSpec((None, None, tr, c), lambda i, s_, cc_ref, slot_ref: (s_, cc_ref[0], i, 0)),
                      pl.BlockSpec((None, tr, c), lambda i, s_, cc_ref, slot_ref: (s_, i, 0))],
            out_specs=[pl.BlockSpec((tr, c), lambda i, s_, cc_ref, slot_ref: (i, 0)),
                       pl.BlockSpec((None, tr, c), lambda i, s_, cc_ref, slot_ref: (s_, i, 0))]),
        out_shape=[jax.ShapeDtypeStruct((r, c), F32), jax.ShapeDtypeStruct((4, r, c), BF16)],
        compiler_params=_params(("parallel", "arbitrary")),
    )(cc.reshape(1).astype(jnp.int32), (2 * x + y).reshape(1).astype(jnp.int32), g.reshape(4, 2, r, c), from_sibling)


def _rs_add_chips(name, own_sum, from_chips):
    r, c = own_sum.shape
    tr = _pick(r, 512, SUBLANES * 2)

    def body(h_ref, f0_ref, f1_ref, f2_ref, o_ref):
        o_ref[...] = ((h_ref[...] + f0_ref[...].astype(F32)) + f1_ref[...].astype(F32)) + f2_ref[...].astype(F32)

    def from_blk(j):
        return pl.BlockSpec((None, tr, c), lambda i: (j, i, 0))

    blk = pl.BlockSpec((tr, c), lambda i: (i, 0))
    return pl.pallas_call(
        body, name=name, grid=(r // tr,), in_specs=[blk, from_blk(0), from_blk(1), from_blk(2)], out_specs=blk,
        out_shape=jax.ShapeDtypeStruct((r, c), F32), compiler_params=_params(("parallel",)),
    )(own_sum, from_chips, from_chips, from_chips)


def adamw_scatter(name, w, m, v, own_sum, from_chips):
    _, a, b = w.shape
    r, c = own_sum.shape
    assert a <= r and b <= c, (name, w.shape, own_sum.shape)
    tr = _pick(a, 256, SUBLANES * 2)

    def body(h_ref, f0_ref, f1_ref, f2_ref, w_ref, m_ref, v_ref, g_out, d_out, m_out, v_out):
        g = ((h_ref[...] + f0_ref[...].astype(F32)) + f1_ref[...].astype(F32)) + f2_ref[...].astype(F32)
        g = g[:, :b]
        g_out[...] = g
        d_out[...], m_out[...], v_out[...] = _adamw_fn(w_ref[...], g, m_ref[...], v_ref[...])

    def from_blk(j):
        return pl.BlockSpec((None, tr, c), lambda i: (j, i, 0))

    mine = pl.BlockSpec((None, tr, b), lambda i: (0, i, 0))
    return pl.pallas_call(
        body, name=name, grid=(a // tr,),
        in_specs=[pl.BlockSpec((tr, c), lambda i: (i, 0)), from_blk(0), from_blk(1), from_blk(2), mine, mine, mine],
        out_specs=[mine] * 4, out_shape=[jax.ShapeDtypeStruct((1, a, b), F32)] * 4, compiler_params=_params(("parallel",)),
    )(own_sum, from_chips, from_chips, from_chips, w, m, v)


def rs_chip_sum(tag, g):
    return _rs_add_sibling("rs_add_sibling_" + tag, g, exchange_sibling("rs_sibling_" + tag, g))


def reduce_scatter(tag, g):
    own_sum, chip_sums_b = rs_chip_sum(tag, g)
    return _rs_add_chips("rs_add_chips_" + tag, own_sum, exchange_chips("rs_chips_" + tag, chip_sums_b))


class GatherIci:
    def __init__(self, shards):
        self.inputs = list(shards)
        self.out_shapes = [jax.ShapeDtypeStruct((N_DEV,) + s.shape, s.dtype) for s in shards]
        self.n_remote, self.n_local = 3 * len(shards), len(shards)

    def make(self, cins, couts, send, recv, local):
        x, y, cc = _my_pos()
        me = _dev_index(x, y, cc)
        copies = []
        for w, (src, out) in enumerate(zip(cins, couts, strict=True)):
            copies.append(pltpu.make_async_copy(src, out.at[me], local.at[w]))
            for j, (px, py) in enumerate([(1 - x, y), (x, 1 - y), (1 - x, 1 - y)]):
                copies.append(pltpu.make_async_remote_copy(
                    src_ref=src, dst_ref=out.at[me], send_sem=send.at[3 * w + j], recv_sem=recv.at[3 * w + j],
                    device_id=(px, py, cc), device_id_type=MESH))
        return copies


class RsChips:
    def __init__(self, chip_sums):
        self.inputs = list(chip_sums)
        self.out_shapes = [jax.ShapeDtypeStruct((3,) + h.shape[1:], h.dtype) for h in chip_sums]
        self.n_remote, self.n_local = 3 * len(chip_sums), 0

    def make(self, cins, couts, send, recv, local):
        x, y, cc = _my_pos()
        copies = []
        for w, (h_ref, out) in enumerate(zip(cins, couts, strict=True)):
            for j, (px, py) in enumerate([(1 - x, y), (x, 1 - y), (1 - x, 1 - y)]):
                copies.append(pltpu.make_async_remote_copy(
                    src_ref=h_ref.at[2 * px + py], dst_ref=out.at[j], send_sem=send.at[3 * w + j], recv_sem=recv.at[3 * w + j],
                    device_id=(px, py, cc), device_id_type=MESH))
        return copies


class SiblingSwap:
    def __init__(self, gs):
        self.inputs = list(gs)
        self.out_shapes = [jax.ShapeDtypeStruct((4,) + g.shape[1:], g.dtype) for g in gs]
        self.n_remote, self.n_local = 4 * len(gs), 0

    def make(self, cins, couts, send, recv, local):
        x, y, cc = _my_pos()
        copies = []
        for w, (g_ref, out) in enumerate(zip(cins, couts, strict=True)):
            for px in range(2):
                for py in range(2):
                    q = 4 * w + 2 * px + py
                    copies.append(pltpu.make_async_remote_copy(
                        src_ref=g_ref.at[_dev_index(px, py, 1 - cc)], dst_ref=out.at[2 * px + py],
                        send_sem=send.at[q], recv_sem=recv.at[q], device_id=(x, y, 1 - cc), device_id_type=MESH))
        return copies


class GatherD2D:
    def __init__(self, arrays):
        self.inputs = list(arrays)
        self.out_shapes = [jax.ShapeDtypeStruct(a.shape, a.dtype) for a in arrays]
        self.n_remote, self.n_local = 4 * len(arrays), 0
        self.aliases = [(i, i) for i in range(len(arrays))]

    def make(self, cins, couts, send, recv, local):
        x, y, cc = _my_pos()
        copies = []
        for w, out in enumerate(couts):
            for px in range(2):
                for py in range(2):
                    q = 4 * w + 2 * px + py
                    slab = out.at[_dev_index(px, py, cc)]
                    copies.append(pltpu.make_async_remote_copy(
                        src_ref=slab, dst_ref=slab, send_sem=send.at[q], recv_sem=recv.at[q],
                        device_id=(x, y, 1 - cc), device_id_type=MESH))
        return copies


class _SemSlice:
    def __init__(self, base, start):
        self.base, self.start = base, start

    @property
    def at(self):
        return self

    def __getitem__(self, k):
        return self.base.at[self.start + k]


class CommGroup:
    def __init__(self, plans):
        self.plans = [p for p in plans if p.inputs]
        self.inputs = [a for p in self.plans for a in p.inputs]
        self.out_shapes = [s_ for p in self.plans for s_ in p.out_shapes]
        self.n_remote = sum(p.n_remote for p in self.plans)
        self.n_local = sum(p.n_local for p in self.plans)
        self.aliases, i0, o0 = [], 0, 0
        for p in self.plans:
            self.aliases += [(i0 + i, o0 + j) for i, j in getattr(p, 'aliases', [])]
            i0, o0 = i0 + len(p.inputs), o0 + len(p.out_shapes)

    def make(self, cins, couts, send, recv, local):
        copies, i0, o0, r0, l0 = [], 0, 0, 0, 0
        for p in self.plans:
            ni, no = len(p.inputs), len(p.out_shapes)
            copies += p.make(cins[i0:i0 + ni], couts[o0:o0 + no], _SemSlice(send, r0), _SemSlice(recv, r0), _SemSlice(local, l0))
            i0, o0, r0, l0 = i0 + ni, o0 + no, r0 + p.n_remote, l0 + p.n_local
        return copies


def _pallas(body, *, name, grid, in_specs, out_specs, out_shape, scratch_shapes=(), sem, comm=None):
    in_specs, out_specs, out_shape, scratch_shapes = list(in_specs), list(out_specs), list(out_shape), list(scratch_shapes)
    if comm is None:
        return pl.pallas_call(body, name=name, grid=grid, in_specs=in_specs, out_specs=out_specs, out_shape=out_shape,
                              scratch_shapes=scratch_shapes, compiler_params=_params(sem))
    n_in, n_out, n_scr = len(in_specs), len(out_specs), len(scratch_shapes)
    nci, nco = len(comm.inputs), len(comm.out_shapes)

    def body2(*refs):
        ins, cins = refs[:n_in], refs[n_in:n_in + nci]
        o0 = n_in + nci
        outs, couts = refs[o0:o0 + n_out], refs[o0 + n_out:o0 + n_out + nco]
        s0 = o0 + n_out + nco
        scr = refs[s0:s0 + n_scr]
        send, recv, local = refs[s0 + n_scr:]
        pids = [pl.program_id(k) for k in range(len(grid))]
        first = functools.reduce(jnp.logical_and, [p == 0 for p in pids])
        last = functools.reduce(jnp.logical_and, [p == g - 1 for p, g in zip(pids, grid)])

        @pl.when(first)
        def _():
            for cp in comm.make(cins, couts, send, recv, local):
                cp.start()

        body(*ins, *outs, *scr)

        @pl.when(last)
        def _():
            for cp in comm.make(cins, couts, send, recv, local):
                cp.wait()

    any_spec = pl.BlockSpec(memory_space=pl.ANY)
    call = pl.pallas_call(
        body2, name=name, grid=grid, in_specs=in_specs + [any_spec] * nci, out_specs=out_specs + [any_spec] * nco,
        out_shape=out_shape + list(comm.out_shapes),
        input_output_aliases={n_in + i: n_out + j for i, j in getattr(comm, 'aliases', [])},
        scratch_shapes=scratch_shapes + [pltpu.SemaphoreType.DMA((comm.n_remote,)), pltpu.SemaphoreType.DMA((comm.n_remote,)),
                                         pltpu.SemaphoreType.DMA((max(comm.n_local, 1),))],
        compiler_params=_params(tuple("arbitrary" for _ in grid)))
    return lambda *args: call(*args, *comm.inputs)


PACK_W = 1024


class Pack:
    def __init__(self, entries, row_unit):
        self.entries = entries
        self.sizes = [int(np.prod(sh)) for _, sh in entries]
        self.offsets = np.concatenate([[0], np.cumsum(self.sizes)]).tolist()
        self.total = _round_up(self.offsets[-1], PACK_W * row_unit)
        self.rows = self.total // PACK_W

    def pack(self, arrays, dtype, lead=()):
        flat = [arrays[n].astype(dtype).reshape(lead + (-1,)) for n, _ in self.entries]
        pad = self.total - self.offsets[-1]
        if pad:
            flat.append(jnp.zeros(lead + (pad,), dtype))
        return jnp.concatenate(flat, axis=-1).reshape(lead + (self.rows, PACK_W))

    def unpack(self, buf, lead=()):
        flat = buf.reshape(lead + (self.total,))
        out = {}
        for (n, sh), off, sz in zip(self.entries, self.offsets, self.sizes):
            out[n] = lax.slice_in_dim(flat, off, off + sz, axis=len(lead)).reshape(lead + tuple(sh))
        return out


def _gathered_to_full(g, how):
    _, a, b = g.shape
    if how == 'row':
        return g.reshape(N_DEV * a, b)
    return jnp.transpose(g, (1, 0, 2)).reshape(a, N_DEV * b)


def _full_to_shards(w, how):
    a, b = w.shape
    if how == 'row':
        return w.reshape(N_DEV, a // N_DEV, b)
    return jnp.transpose(w.reshape(a, N_DEV, b // N_DEV), (1, 0, 2))


def _to_heads(t, width):
    s, c = t.shape
    return jnp.transpose(t.reshape(s, c // width, width), (1, 0, 2))


def _from_heads(t):
    h, s, w = t.shape
    return jnp.transpose(t, (1, 0, 2)).reshape(s, h * w)


def _rot_matrix():
    half = QK_ROPE // 2
    rot = np.zeros((QK_ROPE, QK_ROPE), np.float32)
    for i in range(half):
        rot[i + half, i] = -1.0
        rot[i, i + half] = 1.0
    return jnp.asarray(rot)


def _inv_freq2():
    half = QK_ROPE // 2
    inv = ROPE_THETA ** (-np.arange(half, dtype=np.float32) / half)
    return jnp.asarray(np.concatenate([inv, inv])[None, :].astype(np.float32))


def kernel(x, positions, attn_norm_g, w_in, rwkv_mu, rwkv_w0, rwkv_w2, rwkv_a0, rwkv_a2, rwkv_g2, rwkv_k_k, rwkv_k_a, rwkv_r_k, rwkv_gn_w, rwkv_gn_b, mla_q_norm_g, mla_w_uq, mla_kv_norm_g, mla_w_ukv, w_out, ffn_norm_g, ffn_w_gate, ffn_w_up, ffn_conv_w, ffn_conv_b, ffn_w_down, final_norm_g, loss_target, m_attn_norm_g, m_w_in, m_rwkv_mu, m_rwkv_w0, m_rwkv_w2, m_rwkv_a0, m_rwkv_a2, m_rwkv_g2, m_rwkv_k_k, m_rwkv_k_a, m_rwkv_r_k, m_rwkv_gn_w, m_rwkv_gn_b, m_mla_q_norm_g, m_mla_w_uq, m_mla_kv_norm_g, m_mla_w_ukv, m_w_out, m_ffn_norm_g, m_ffn_w_gate, m_ffn_w_up, m_ffn_conv_w, m_ffn_conv_b, m_ffn_w_down, m_final_norm_g, v_attn_norm_g, v_w_in, v_rwkv_mu, v_rwkv_w0, v_rwkv_w2, v_rwkv_a0, v_rwkv_a2, v_rwkv_g2, v_rwkv_k_k, v_rwkv_k_a, v_rwkv_r_k, v_rwkv_gn_w, v_rwkv_gn_b, v_mla_q_norm_g, v_mla_w_uq, v_mla_kv_norm_g, v_mla_w_ukv, v_w_out, v_ffn_norm_g, v_ffn_w_gate, v_ffn_w_up, v_ffn_conv_w, v_ffn_conv_b, v_ffn_w_down, v_final_norm_g):
    given = dict(locals())
    wts = {n: given[n] for n in WEIGHTS}
    mom_m = {n: given["m_" + n] for n in WEIGHTS}
    mom_v = {n: given["v_" + n] for n in WEIGHTS}
    out_shapes = {n: wts[n].shape for n in WEIGHTS}

    def local2d(n, a):
        if n == 'rwkv_r_k' or a.ndim <= 2:
            return a.reshape(1, -1)
        return a.reshape(a.shape[1:])

    w2d = {n: local2d(n, wts[n]) for n in WEIGHTS}
    m2d = {n: local2d(n, mom_m[n]) for n in WEIGHTS}
    v2d = {n: local2d(n, mom_v[n]) for n in WEIGHTS}

    xs = x.reshape(x.shape[1:])
    tgt = loss_target.reshape(loss_target.shape[1:])
    s, d = xs.shape
    c_rwkv = w2d['rwkv_w0'].shape[1]
    n_rh = c_rwkv // RWKV_HEAD
    decay_lora, aaa_lora, gate_lora = w2d['rwkv_w2'].shape[0], w2d['rwkv_a2'].shape[0], w2d['rwkv_g2'].shape[0]
    q_lora, kv_lora = w2d['mla_q_norm_g'].shape[1], w2d['mla_kv_norm_g'].shape[1]
    shift_dim = w2d['rwkv_mu'].shape[1]
    d_in = w2d['w_in'].shape[1] * N_DEV
    n_mh = w2d['mla_w_uq'].shape[1] * N_DEV // (QK_NOPE + QK_ROPE)
    tm = _pick(s, 256, SUBLANES)
    tm_wide = _pick(s, 128, SUBLANES)
    tm_heads = _pick(s, 512, SUBLANES)

    nb = {n: w2d[n].shape[1] for n in BIG if BIG[n] == 'col'}
    nbp = {n: _round_up(v_, LANES) for n, v_ in nb.items()}
    shards = {}
    for n in BIG:
        w = w2d[n].astype(BF16)
        if BIG[n] == 'col':
            w = jnp.pad(w, ((0, 0), (0, nbp[n] - nb[n])))
        elif n == 'ffn_w_down':
            w = jnp.pad(w, ((0, nbp['ffn_w_gate'] - w.shape[0]), (0, 0)))
        shards[n] = w

    def as_used(n, g):
        return g if BIG[n] == 'col' else g.reshape(N_DEV * g.shape[1], g.shape[2])

    gathered = {'w_in': as_used('w_in', all_gather("gather_w_in", shards['w_in']))}
    f_pad = N_DEV * nbp['ffn_w_gate']
    small_pack = Pack([(n, w2d[n].shape) for n in SMALL_SHARDED], 8)
    small_all = all_gather("gather_small", small_pack.pack(w2d, F32))
    full = {}
    for n, g in small_pack.unpack(small_all, lead=(N_DEV,)).items():
        full[n] = _gathered_to_full(g, SMALL_SHARDED[n])
    conv_w_pad = pad_cols(full['ffn_conv_w'], nb['ffn_w_gate'], nbp['ffn_w_gate'])
    conv_b_pad = pad_cols(w2d['ffn_conv_b'], nb['ffn_w_gate'], nbp['ffn_w_gate'])

    (h1,) = rowwise("rms_attn", _rms_fn, [xs, w2d['attn_norm_g']], ['row', 'const'], [('row', d, BF16)], heads=1, s=s, tm=tm)
    landed = {}

    def gather_behind(run, ici=(), d2d=()):
        *res, = run(CommGroup([GatherIci([shards[n] for n in ici]), GatherD2D([landed[n] for n in d2d])]))
        n_own = len(res) - len(ici) - len(d2d)
        landed.update(zip(ici, res[n_own:n_own + len(ici)], strict=True))
        for n, g in zip(d2d, res[n_own + len(ici):], strict=True):
            gathered[n] = as_used(n, g)
        return res[:n_own]

    w_in_nat = jnp.transpose(gathered['w_in'], (1, 0, 2))[:, :, :nb['w_in']].reshape(d, d_in)
    w_in_nat = jnp.pad(w_in_nat, ((0, 0), (0, _round_up(d_in, LANES) - d_in)))
    (proj,) = gather_behind(lambda c: mm("proj_in", h1, w_in_nat, comm=c), ici=['mla_w_uq', 'mla_w_ukv', 'w_out'])
    p_rwkv = proj[:, :shift_dim]
    c_q = proj[:, shift_dim:shift_dim + q_lora]
    c_kv = proj[:, shift_dim + q_lora:shift_dim + q_lora + kv_lora]
    k_pe = proj[:, shift_dim + q_lora + kv_lora:d_in]
    shifted = token_shift_fwd(p_rwkv, w2d['rwkv_mu'], tm_wide)
    o1, o2, o3 = c_rwkv, 2 * c_rwkv, 3 * c_rwkv
    hr = _to_heads(shifted[:, :o1], RWKV_HEAD)
    hk = _to_heads(shifted[:, o1:o2], RWKV_HEAD)
    hv = _to_heads(shifted[:, o2:o3], RWKV_HEAD)
    hw = shifted[:, o3:o3 + decay_lora]
    ha = shifted[:, o3 + decay_lora:o3 + decay_lora + aaa_lora]
    hg = shifted[:, o3 + decay_lora + aaa_lora:]

    def per_head(vec):
        return vec.reshape(n_rh, 1, RWKV_HEAD)

    def lora_heads(w):
        return jnp.transpose(w.reshape(w.shape[0], n_rh, RWKV_HEAD), (1, 0, 2))

    pre_args = [hk, hw, ha, hg, per_head(w2d['rwkv_w0']), lora_heads(full['rwkv_w2']), per_head(w2d['rwkv_a0']),
                lora_heads(full['rwkv_a2']), lora_heads(full['rwkv_g2']), per_head(w2d['rwkv_k_k']), per_head(w2d['rwkv_k_a'])]
    pre_kinds = ['hrow', 'row', 'row', 'row', 'hconst', 'hconst', 'hconst', 'hconst', 'hconst', 'hconst', 'hconst']
    decay, kx, a_sc, b_sc, gate_r = gather_behind(
        lambda c: rowwise("rwkv_pre", _rwkv_pre_fn, pre_args, pre_kinds, [('hrow', RWKV_HEAD, F32)] * 5, heads=n_rh, s=s,
                          tm=tm_heads, comm=c),
        ici=['ffn_w_gate'], d2d=['mla_w_uq', 'mla_w_ukv', 'w_out'])
    y_scan, ckpt = gather_behind(lambda c: scan_fwd(hr, decay, kx, hv, a_sc, b_sc, comm=c), ici=['ffn_w_up'], d2d=['ffn_w_gate'])
    post_args = [y_scan, hr, kx, hv, gate_r, per_head(w2d['rwkv_gn_w']), per_head(w2d['rwkv_gn_b']), per_head(w2d['rwkv_r_k'])]
    post_kinds = ['hrow'] * 5 + ['hconst'] * 3
    (y_rwkv_h,) = rowwise("rwkv_post", _rwkv_post_fn, post_args, post_kinds, [('hrow', RWKV_HEAD, F32)], heads=n_rh, s=s, tm=tm_heads)

    pos = positions.reshape(s, 1).astype(F32)
    rot, inv2 = _rot_matrix(), _inv_freq2()
    mla_args = [c_q, c_kv, k_pe, pos, w2d['mla_q_norm_g'], w2d['mla_kv_norm_g'], inv2, rot]
    mla_kinds = ['row', 'row', 'row', 'row', 'const', 'const', 'const', 'const']
    qn, kvn, kp_rot, cos2, sin2 = rowwise(
        "mla_pre", _mla_pre_fn, mla_args, mla_kinds,
        [('row', q_lora, BF16), ('row', kv_lora, BF16), ('row', QK_ROPE, F32), ('row', QK_ROPE, F32), ('row', QK_ROPE, F32)],
        heads=1, s=s, tm=tm)
    assert n_mh == N_DEV and nb['mla_w_uq'] == QK_NOPE + QK_ROPE and nb['mla_w_ukv'] == QK_NOPE + V_HEAD
    assert nbp['mla_w_uq'] == ATTN_SLAB and nbp['mla_w_ukv'] == ATTN_SLAB
    q_h = mm_sh("proj_q", qn, gathered['mla_w_uq'], slabs=True)
    kv_h = mm_sh("proj_kv", kvn, gathered['mla_w_ukv'], slabs=True)
    o_att, lse = gather_behind(lambda c: attn_fwd(q_h, kv_h, kp_rot, cos2, sin2, rot, comm=c), ici=['ffn_w_down'], d2d=['ffn_w_up'])
    ycat = jnp.concatenate([_from_heads(y_rwkv_h), _from_heads(o_att)], axis=-1).astype(BF16)
    (x1,) = gather_behind(lambda c: mm("proj_out", ycat, gathered['w_out'], add=xs, comm=c), d2d=['ffn_w_down'])
    (h2,) = rowwise("rms_ffn", _rms_fn, [x1, w2d['ffn_norm_g']], ['row', 'const'], [('row', d, BF16)], heads=1, s=s, tm=tm)
    gate_pre = mm_sh("ffn_gate", h2, gathered['ffn_w_gate'])
    up = mm_sh("ffn_up", h2, gathered['ffn_w_up'])
    act = ffn_act_fwd(gate_pre, up, conv_w_pad, conv_b_pad)
    x2 = mm("ffn_down", act, gathered['ffn_w_down'], add=x1)

    ones = jnp.ones((s, 1), F32)
    fin_g = w2d['final_norm_g']
    d_x2, dg_final_p, loss_rows = rowwise_vjp("loss_bwd", _loss_fn, [x2, fin_g, tgt], ['row', 'const', 'row'], [ones], ['row'],
                                              [0, 1], heads=1, s=s, tm=tm, primal=True)
    d_x2_b = d_x2.astype(BF16)
    d_act = mm_nt("d_act", d_x2_b, gathered['ffn_w_down'], out_dtype=BF16)
    gsh, chip_sums, from_chips = {}, {}, {}

    def scatter_behind(run, ici=(), swap=()):
        *res, = run(CommGroup([RsChips([chip_sums[n][1] for n in ici]), SiblingSwap([gsh[n] for n in swap])]))
        n_own = len(res) - len(ici) - len(swap)
        from_chips.update(zip(ici, res[n_own:n_own + len(ici)], strict=True))
        for n, from_sibling in zip(swap, res[n_own + len(ici):], strict=True):
            chip_sums[n] = _rs_add_sibling("rs_add_sibling_" + n, gsh[n], from_sibling)
        return res[:n_own]

    gsh['ffn_w_down'] = mm_tn("dw_down", act, d_x2_b).reshape(N_DEV, nbp['ffn_w_gate'], d)
    d_gate, d_up, dcw_p, dcb_p = ffn_act_bwd1(gate_pre, up, conv_w_pad, conv_b_pad, d_act)
    d_gp = ffn_act_bwd2(d_gate, conv_w_pad)
    (d_h2_g,) = scatter_behind(lambda c: mm_sh_nt("d_h2_gate", d_gp, gathered['ffn_w_gate'], comm=c), swap=['ffn_w_down'])
    d_h2 = mm_sh_nt("d_h2_up", d_up, gathered['ffn_w_up'], add=d_h2_g)
    gsh['ffn_w_gate'] = mm_sh_out("dw_gate", h2, d_gp)
    (gsh['ffn_w_up'],) = scatter_behind(lambda c: mm_sh_out("dw_up", h2, d_up, comm=c), swap=['ffn_w_gate'])
    d_x1, dg_ffn_p = rowwise_vjp("rms_ffn_bwd", _rms_fn, [x1, w2d['ffn_norm_g']], ['row', 'const'], [d_h2], ['row'], [0, 1],
                                 heads=1, s=s, tm=tm, plus=d_x2)
    d_x1_b = d_x1.astype(BF16)
    d_ycat = mm_nt("d_ycat", d_x1_b, gathered['w_out'])
    gsh['w_out'] = mm_tn("dw_out", ycat, d_x1_b).reshape((N_DEV,) + w2d['w_out'].shape)
    d_yr_h = _to_heads(d_ycat[:, :c_rwkv], RWKV_HEAD)
    d_o_h = _to_heads(d_ycat[:, c_rwkv:], V_HEAD)

    d_q, d_kv, d_kp_h = scatter_behind(
        lambda c: attn_bwd(q_h, kv_h, kp_rot, cos2, sin2, rot, o_att, lse, d_o_h, comm=c),
        ici=['ffn_w_down'], swap=['ffn_w_up', 'w_out'])
    d_kp_rot = headsum("d_kpe_heads", d_kp_h)
    d_qn = mm_sh_nt("d_qn", d_q, gathered['mla_w_uq'])
    d_kvn = mm_sh_nt("d_kvn", d_kv, gathered['mla_w_ukv'])
    gsh['mla_w_uq'] = mm_sh_out("dw_uq", qn, d_q)
    gsh['mla_w_ukv'] = mm_sh_out("dw_ukv", kvn, d_kv)
    d_cq, d_ckv, d_kpe, dg_q_p, dg_kv_p = rowwise_vjp(
        "mla_pre_bwd", _mla_pre_grad_fn, mla_args, mla_kinds, [d_qn, d_kvn, d_kp_rot], ['row', 'row', 'row'], [0, 1, 2, 4, 5],
        heads=1, s=s, tm=tm)

    d_y, d_r_post, d_k_post, d_v_post, d_gate_r, dgnw_p, dgnb_p, drk_p = scatter_behind(
        lambda c: rowwise_vjp("rwkv_post_bwd", _rwkv_post_fn, post_args, post_kinds, [d_yr_h], ['hrow'], list(range(8)),
                              heads=n_rh, s=s, tm=tm_heads, comm=c),
        ici=['ffn_w_gate'], swap=['mla_w_uq', 'mla_w_ukv'])
    d_r_sc, d_w_sc, d_k_sc, d_v_sc, d_a_sc, d_b_sc = scatter_behind(
        lambda c: scan_bwd(hr, decay, kx, hv, a_sc, b_sc, ckpt, d_y, comm=c), ici=['ffn_w_up', 'w_out'])
    d_hk, d_hw_p, d_ha_p, d_hg_p, dw0_p, dw2_p, da0_p, da2_p, dg2_p, dkk_p, dka_p, d_hr, d_hv = scatter_behind(
        lambda c: rowwise_vjp(
            "rwkv_pre_bwd", _rwkv_pre_grad_fn, pre_args + [hr, hv], pre_kinds + ['hrow', 'hrow'],
            [d_w_sc, d_k_sc, d_k_post, d_a_sc, d_b_sc, d_gate_r, d_r_sc, d_r_post, d_v_sc, d_v_post], ['hrow'] * 10,
            list(range(13)), heads=n_rh, s=s, tm=tm_heads, comm=c),
        ici=['mla_w_uq', 'mla_w_ukv'])
    d_shifted = jnp.concatenate([_from_heads(d_hr), _from_heads(d_hk), _from_heads(d_hv), headsum("d_hw_heads", d_hw_p),
                                 headsum("d_ha_heads", d_ha_p), headsum("d_hg_heads", d_hg_p)], axis=-1)
    d_p_rwkv, dmu_p = token_shift_bwd(p_rwkv, w2d['rwkv_mu'], d_shifted, tm_wide)
    d_proj = pad_cols(jnp.concatenate([d_p_rwkv, d_cq, d_ckv, d_kpe], axis=-1).astype(BF16), nb['w_in'], nbp['w_in'])
    gsh['w_in'] = mm_sh_out("dw_in", h1, d_proj)
    chip_sums['w_in'] = rs_chip_sum('w_in', gsh['w_in'])
    (d_h1,) = scatter_behind(lambda c: mm_sh_nt("d_h1", d_proj, gathered['w_in'], comm=c), ici=['w_in'])
    grad_x, dg_attn_p = rowwise_vjp("rms_attn_bwd", _rms_fn, [xs, w2d['attn_norm_g']], ['row', 'const'], [d_h1], ['row'], [0, 1],
                                    heads=1, s=s, tm=tm, plus=d_x1)

    def from_heads_lora(g):
        return jnp.transpose(g, (1, 0, 2)).reshape(g.shape[1], n_rh * RWKV_HEAD)

    gw = {}
    gw['rwkv_w2'] = from_heads_lora(sum_partials("sum_dw2", dw2_p, True))
    gw['rwkv_a2'] = from_heads_lora(sum_partials("sum_da2", da2_p, True))
    gw['rwkv_g2'] = from_heads_lora(sum_partials("sum_dg2", dg2_p, True))
    dcw_pad = colsum("sum_dconv_w", dcw_p.reshape(dcw_p.shape[0], CONV_W * f_pad)).reshape(CONV_W, f_pad)
    gw['ffn_conv_w'] = unpad_cols(dcw_pad, nb['ffn_w_gate'], nbp['ffn_w_gate'])

    rep = {
        'attn_norm_g': sum_partials("sum_dg_attn", dg_attn_p, False),
        'rwkv_mu': colsum("sum_dmu", dmu_p.reshape(dmu_p.shape[0], shift_dim)),
        'rwkv_w0': sum_partials("sum_dw0", dw0_p, True).reshape(1, c_rwkv),
        'rwkv_a0': sum_partials("sum_da0", da0_p, True).reshape(1, c_rwkv),
        'rwkv_k_k': sum_partials("sum_dkk", dkk_p, True).reshape(1, c_rwkv),
        'rwkv_k_a': sum_partials("sum_dka", dka_p, True).reshape(1, c_rwkv),
        'rwkv_r_k': sum_partials("sum_drk", drk_p, True).reshape(1, c_rwkv),
        'rwkv_gn_w': sum_partials("sum_dgnw", dgnw_p, True).reshape(1, c_rwkv),
        'rwkv_gn_b': sum_partials("sum_dgnb", dgnb_p, True).reshape(1, c_rwkv),
        'mla_q_norm_g': sum_partials("sum_dg_q", dg_q_p, False),
        'mla_kv_norm_g': sum_partials("sum_dg_kv", dg_kv_p, False),
        'ffn_norm_g': sum_partials("sum_dg_ffn", dg_ffn_p, False),
        'ffn_conv_b': unpad_cols(colsum("sum_dconv_b", dcb_p.reshape(dcb_p.shape[0], f_pad)), nb['ffn_w_gate'], nbp['ffn_w_gate']),
        'final_norm_g': sum_partials("sum_dg_final", dg_final_p, False),
        'loss': sum_all("sum_loss", loss_rows.reshape(s // SUBLANES, SUBLANES)),
    }
    rep_pack = Pack([(n, w2d[n].shape) for n in REPLICATED] + [('loss', (1, 1))], 8)
    rep_all = all_gather("gather_rep_grads", rep_pack.pack(rep, F32))
    rep_sum = colsum("sum_rep_grads", rep_all.reshape(N_DEV, rep_pack.total)).reshape(rep_pack.rows, PACK_W)
    rep_g = rep_pack.unpack(rep_sum)
    loss = rep_g.pop('loss').reshape(())

    grads, deltas, new_m, new_v = dict(rep_g), {}, {}, {}
    for n in BIG:
        grads[n], deltas[n], new_m[n], new_v[n] = adamw_scatter(
            "adamw_" + n, wts[n], mom_m[n], mom_v[n], chip_sums[n][0], from_chips[n])
    sm_pack = Pack([(n, w2d[n].shape) for n in SMALL_SHARDED], 8)
    g_shards = {n: _full_to_shards(gw[n], SMALL_SHARDED[n]) for n in SMALL_SHARDED}
    grads.update(sm_pack.unpack(reduce_scatter("small", sm_pack.pack(g_shards, F32, lead=(N_DEV,)))))
    rest_pack = Pack([(n, w2d[n].shape) for n in WEIGHTS if n not in BIG], 8)
    d_r, m_r, v_r = rowwise(
        "adamw_small", _adamw_fn, [rest_pack.pack(w2d, F32), rest_pack.pack(grads, F32), rest_pack.pack(m2d, F32),
                                   rest_pack.pack(v2d, F32)],
        ['row'] * 4, [('row', PACK_W, F32)] * 3, heads=1, s=rest_pack.rows, tm=_pick(rest_pack.rows, 512, SUBLANES))
    deltas.update(rest_pack.unpack(d_r))
    new_m.update(rest_pack.unpack(m_r))
    new_v.update(rest_pack.unpack(v_r))

    def shaped(dct):
        return [dct[n].reshape(out_shapes[n]) for n in WEIGHTS]

    return (loss, grad_x.reshape(x.shape), *shaped(grads), *shaped(deltas), *shaped(new_m), *shaped(new_v))
```
